```python
import jax, jax.numpy as jnp
from jax import lax
import numpy as np

D_MODEL = 2048
BATCH = 8
SEQ = 4096
DEPTH = 4

N_MEM = 256
GRID_W = 64
HEAD_DIM = 128
MIX_WIDTH = D_MODEL
MEM_HEADS = 4
MEM_WIDTH = MEM_HEADS * HEAD_DIM
TOK_WIDTH = MIX_WIDTH - MEM_WIDTH
CHUNK = 128
A_GROUPS = TOK_WIDTH // HEAD_DIM
A_GROUP_DIM = HEAD_DIM
Q_HEADS = TOK_WIDTH // HEAD_DIM
KV_HEADS = 4
Q_PER_KV = Q_HEADS // KV_HEADS
KV_WIDTH = KV_HEADS * HEAD_DIM
Q_BLOCK = 128
ROPE_THETA = 10000.0
ROPE_PAIRS = HEAD_DIM // 4
D_FF = ((8 * D_MODEL // 3 + 255) // 256) * 256
N_MIXERS = 2
N_A = (DEPTH + 1) // 2
N_B = DEPTH // 2
EPS = 1e-6

kernel_name = "hybrid_gmlp_axial_gqa_memory_encoder"


def rms_norm(x, g):
    xf = x.astype(jnp.float32)
    y = xf * lax.rsqrt(jnp.mean(xf * xf, axis=-1, keepdims=True) + EPS)
    return (y * g.astype(jnp.float32)).astype(x.dtype)


def axial_rope_tables(seq_len):
    n_rows = seq_len // GRID_W
    rows = jnp.broadcast_to(jnp.arange(n_rows)[:, None], (n_rows, GRID_W)).reshape(seq_len)
    cols = jnp.broadcast_to(jnp.arange(GRID_W)[None, :], (n_rows, GRID_W)).reshape(seq_len)
    freqs = ROPE_THETA ** (-jnp.arange(ROPE_PAIRS, dtype=jnp.float32) / ROPE_PAIRS)
    ang_r = rows.astype(jnp.float32)[:, None] * freqs
    ang_c = cols.astype(jnp.float32)[:, None] * freqs
    ang = jnp.concatenate([ang_r, ang_r, ang_c, ang_c], axis=-1)
    return jnp.cos(ang), jnp.sin(ang)


def apply_axial_rope(x, cos, sin):
    shape = (1, cos.shape[0]) + (1,) * (x.ndim - 3) + (HEAD_DIM,)
    c, s = cos.reshape(shape), sin.reshape(shape)
    xf = x.astype(jnp.float32)
    xs = xf.reshape(x.shape[:-1] + (2, 2, ROPE_PAIRS))
    rot = jnp.stack([-xs[..., 1, :], xs[..., 0, :]], axis=-2).reshape(x.shape)
    return (xf * c + rot * s).astype(x.dtype)


def chunked_spatial_gating(uv, g_v, w_s, b_s):
    B, S, _ = uv.shape
    uv = jax.nn.gelu(uv, approximate=False)
    u, v = jnp.split(uv, 2, axis=-1)
    v = rms_norm(v, g_v)
    v = v.reshape(B, S // CHUNK, CHUNK, A_GROUPS, A_GROUP_DIM)
    s = jnp.einsum('gts,bcsgd->bctgd', w_s, v) + b_s.T[None, None, :, :, None]
    return u * s.reshape(B, S, TOK_WIDTH)


def gqa_axial_attention(qkv, g_q, g_k, cos, sin):
    B, S, _ = qkv.shape
    q, k, v = jnp.split(qkv, [TOK_WIDTH, TOK_WIDTH + KV_WIDTH], axis=-1)
    q = q.reshape(B, S, KV_HEADS, Q_PER_KV, HEAD_DIM)
    k = k.reshape(B, S, KV_HEADS, HEAD_DIM)
    v = v.reshape(B, S, KV_HEADS, HEAD_DIM)
    q = apply_axial_rope(rms_norm(q, g_q), cos, sin)
    k = apply_axial_rope(rms_norm(k, g_k), cos, sin)
    scale = HEAD_DIM ** -0.5
    n_blk = S // Q_BLOCK
    qb = q.reshape(B, n_blk, Q_BLOCK, KV_HEADS, Q_PER_KV, HEAD_DIM).transpose(1, 0, 2, 3, 4, 5)

    def block(qi):
        s = jnp.einsum('bqhgd,bkhd->bhgqk', qi, k).astype(jnp.float32) * scale
        p = jax.nn.softmax(s, axis=-1).astype(v.dtype)
        return jnp.einsum('bhgqk,bkhd->bqhgd', p, v)

    o = lax.map(block, qb)
    return o.transpose(1, 0, 2, 3, 4, 5).reshape(B, S, TOK_WIDTH)


def memory_attention(q_mem, mem, g_mem, w_kv, g_mq, g_mk):
    B, S, _ = q_mem.shape
    kv = rms_norm(mem, g_mem) @ w_kv
    k, v = jnp.split(kv, 2, axis=-1)
    k = rms_norm(k.reshape(B, -1, MEM_HEADS, HEAD_DIM), g_mk)
    v = v.reshape(B, -1, MEM_HEADS, HEAD_DIM)
    q = rms_norm(q_mem.reshape(B, S, MEM_HEADS, HEAD_DIM), g_mq)
    s = jnp.einsum('bshd,bmhd->bhsm', q, k).astype(jnp.float32) * (HEAD_DIM ** -0.5)
    p = jax.nn.softmax(s, axis=-1).astype(v.dtype)
    return jnp.einsum('bhsm,bmhd->bshd', p, v).reshape(B, S, MEM_WIDTH)


def swiglu(h, w_gate_up, w_down):
    gate, up = jnp.split(h @ w_gate_up, 2, axis=-1)
    return (jax.nn.silu(gate) * up) @ w_down


def _fwd_setup_inputs(seed: int = 0) -> dict:
    key = jax.random.key(seed)
    ks = jax.random.split(key, 20)
    f32 = jnp.float32

    def nrm(k, shape, fan_in):
        return jax.random.normal(k, shape, f32) * (fan_in ** -0.5)

    def gain(k, shape):
        return 1.0 + 0.02 * jax.random.normal(k, shape, f32)

    return {
        "x": jax.random.normal(ks[0], (BATCH, SEQ, D_MODEL), f32),
        "mem": jax.random.normal(ks[1], (BATCH, N_MEM, D_MODEL), f32),
        "g_mix": gain(ks[2], (DEPTH, D_MODEL)),
        "g_ffn": gain(ks[3], (DEPTH, D_MODEL)),
        "w_in_a": nrm(ks[4], (N_A, D_MODEL, 2 * TOK_WIDTH + MEM_WIDTH), D_MODEL),
        "g_v_a": gain(ks[5], (N_A, TOK_WIDTH)),
        "w_spatial": nrm(ks[6], (N_A, A_GROUPS, CHUNK, CHUNK), CHUNK),
        "b_spatial": 0.02 * jax.random.normal(ks[7], (N_A, A_GROUPS, CHUNK), f32),
        "w_in_b": nrm(ks[8], (N_B, D_MODEL, TOK_WIDTH + 2 * KV_WIDTH + MEM_WIDTH), D_MODEL),
        "g_q_b": gain(ks[9], (N_B, HEAD_DIM)),
        "g_k_b": gain(ks[10], (N_B, HEAD_DIM)),
        "g_mem": gain(ks[11], (DEPTH, D_MODEL)),
        "w_mem_kv": nrm(ks[12], (DEPTH, D_MODEL, 2 * MEM_WIDTH), D_MODEL),
        "g_mq": gain(ks[13], (DEPTH, HEAD_DIM)),
        "g_mk": gain(ks[14], (DEPTH, HEAD_DIM)),
        "w_out": nrm(ks[15], (DEPTH, MIX_WIDTH, D_MODEL), MIX_WIDTH),
        "w_gate_up": nrm(ks[16], (DEPTH, D_MODEL, 2 * D_FF), D_MODEL),
        "w_down": nrm(ks[17], (DEPTH, D_FF, D_MODEL), D_FF),
    }


def _fwd_reference(x, mem, g_mix, g_ffn, w_in_a, g_v_a, w_spatial, b_spatial, w_in_b, g_q_b, g_k_b,
              g_mem, w_mem_kv, g_mq, g_mk, w_out, w_gate_up, w_down):
    S = x.shape[1]
    cos, sin = axial_rope_tables(S)
    for l in range(DEPTH):
        h = rms_norm(x, g_mix[l])
        if l % N_MIXERS == 0:
            ia = l // N_MIXERS
            z = h @ w_in_a[ia]
            tok_in, q_mem = jnp.split(z, [2 * TOK_WIDTH], axis=-1)
            tok_out = chunked_spatial_gating(tok_in, g_v_a[ia], w_spatial[ia], b_spatial[ia])
        else:
            ib = l // N_MIXERS
            z = h @ w_in_b[ib]
            tok_in, q_mem = jnp.split(z, [TOK_WIDTH + 2 * KV_WIDTH], axis=-1)
            tok_out = gqa_axial_attention(tok_in, g_q_b[ib], g_k_b[ib], cos, sin)
        mem_out = memory_attention(q_mem, mem, g_mem[l], w_mem_kv[l], g_mq[l], g_mk[l])
        x = x + jnp.concatenate([tok_out, mem_out], axis=-1) @ w_out[l]
        x = x + swiglu(rms_norm(x, g_ffn[l]), w_gate_up[l], w_down[l])
    return x


import jax as _jax
import jax.numpy as _jnp

TWIN_FORMAT = 'train_step'
FWD_PARAMS = ['x', 'mem', 'g_mix', 'g_ffn', 'w_in_a', 'g_v_a', 'w_spatial', 'b_spatial', 'w_in_b', 'g_q_b', 'g_k_b', 'g_mem', 'w_mem_kv', 'g_mq', 'g_mk', 'w_out', 'w_gate_up', 'w_down']
TWIN_WEIGHTS = ['g_mix', 'g_ffn', 'w_in_a', 'g_v_a', 'w_spatial', 'b_spatial', 'w_in_b', 'g_q_b', 'g_k_b', 'g_mem', 'w_mem_kv', 'g_mq', 'g_mk', 'w_out', 'w_gate_up', 'w_down']
TWIN_DIFF_INPUT = 'x'
TWIN_INPUTS = ['x', 'mem', 'g_mix', 'g_ffn', 'w_in_a', 'g_v_a', 'w_spatial', 'b_spatial', 'w_in_b', 'g_q_b', 'g_k_b', 'g_mem', 'w_mem_kv', 'g_mq', 'g_mk', 'w_out', 'w_gate_up', 'w_down', 'loss_target', 'm_g_mix', 'm_g_ffn', 'm_w_in_a', 'm_g_v_a', 'm_w_spatial', 'm_b_spatial', 'm_w_in_b', 'm_g_q_b', 'm_g_k_b', 'm_g_mem', 'm_w_mem_kv', 'm_g_mq', 'm_g_mk', 'm_w_out', 'm_w_gate_up', 'm_w_down', 'v_g_mix', 'v_g_ffn', 'v_w_in_a', 'v_g_v_a', 'v_w_spatial', 'v_b_spatial', 'v_w_in_b', 'v_g_q_b', 'v_g_k_b', 'v_g_mem', 'v_w_mem_kv', 'v_g_mq', 'v_g_mk', 'v_w_out', 'v_w_gate_up', 'v_w_down']
TWIN_OUTPUTS = ['loss', 'grad_x', 'grad_g_mix', 'grad_g_ffn', 'grad_w_in_a', 'grad_g_v_a', 'grad_w_spatial', 'grad_b_spatial', 'grad_w_in_b', 'grad_g_q_b', 'grad_g_k_b', 'grad_g_mem', 'grad_w_mem_kv', 'grad_g_mq', 'grad_g_mk', 'grad_w_out', 'grad_w_gate_up', 'grad_w_down', 'delta_g_mix', 'delta_g_ffn', 'delta_w_in_a', 'delta_g_v_a', 'delta_w_spatial', 'delta_b_spatial', 'delta_w_in_b', 'delta_g_q_b', 'delta_g_k_b', 'delta_g_mem', 'delta_w_mem_kv', 'delta_g_mq', 'delta_g_mk', 'delta_w_out', 'delta_w_gate_up', 'delta_w_down', 'new_m_g_mix', 'new_m_g_ffn', 'new_m_w_in_a', 'new_m_g_v_a', 'new_m_w_spatial', 'new_m_b_spatial', 'new_m_w_in_b', 'new_m_g_q_b', 'new_m_g_k_b', 'new_m_g_mem', 'new_m_w_mem_kv', 'new_m_g_mq', 'new_m_g_mk', 'new_m_w_out', 'new_m_w_gate_up', 'new_m_w_down', 'new_v_g_mix', 'new_v_g_ffn', 'new_v_w_in_a', 'new_v_g_v_a', 'new_v_w_spatial', 'new_v_b_spatial', 'new_v_w_in_b', 'new_v_g_q_b', 'new_v_g_k_b', 'new_v_g_mem', 'new_v_w_mem_kv', 'new_v_g_mq', 'new_v_g_mk', 'new_v_w_out', 'new_v_w_gate_up', 'new_v_w_down']
TWIN_LEAF_KINDS = {'loss': 'loss', 'grad_x': 'grad_x', 'grad_g_mix': 'grad_w', 'grad_g_ffn': 'grad_w', 'grad_w_in_a': 'grad_w', 'grad_g_v_a': 'grad_w', 'grad_w_spatial': 'grad_w', 'grad_b_spatial': 'grad_w', 'grad_w_in_b': 'grad_w', 'grad_g_q_b': 'grad_w', 'grad_g_k_b': 'grad_w', 'grad_g_mem': 'grad_w', 'grad_w_mem_kv': 'grad_w', 'grad_g_mq': 'grad_w', 'grad_g_mk': 'grad_w', 'grad_w_out': 'grad_w', 'grad_w_gate_up': 'grad_w', 'grad_w_down': 'grad_w', 'delta_g_mix': 'delta_w', 'delta_g_ffn': 'delta_w', 'delta_w_in_a': 'delta_w', 'delta_g_v_a': 'delta_w', 'delta_w_spatial': 'delta_w', 'delta_b_spatial': 'delta_w', 'delta_w_in_b': 'delta_w', 'delta_g_q_b': 'delta_w', 'delta_g_k_b': 'delta_w', 'delta_g_mem': 'delta_w', 'delta_w_mem_kv': 'delta_w', 'delta_g_mq': 'delta_w', 'delta_g_mk': 'delta_w', 'delta_w_out': 'delta_w', 'delta_w_gate_up': 'delta_w', 'delta_w_down': 'delta_w', 'new_m_g_mix': 'new_m', 'new_m_g_ffn': 'new_m', 'new_m_w_in_a': 'new_m', 'new_m_g_v_a': 'new_m', 'new_m_w_spatial': 'new_m', 'new_m_b_spatial': 'new_m', 'new_m_w_in_b': 'new_m', 'new_m_g_q_b': 'new_m', 'new_m_g_k_b': 'new_m', 'new_m_g_mem': 'new_m', 'new_m_w_mem_kv': 'new_m', 'new_m_g_mq': 'new_m', 'new_m_g_mk': 'new_m', 'new_m_w_out': 'new_m', 'new_m_w_gate_up': 'new_m', 'new_m_w_down': 'new_m', 'new_v_g_mix': 'new_v', 'new_v_g_ffn': 'new_v', 'new_v_w_in_a': 'new_v', 'new_v_g_v_a': 'new_v', 'new_v_w_spatial': 'new_v', 'new_v_b_spatial': 'new_v', 'new_v_w_in_b': 'new_v', 'new_v_g_q_b': 'new_v', 'new_v_g_k_b': 'new_v', 'new_v_g_mem': 'new_v', 'new_v_w_mem_kv': 'new_v', 'new_v_g_mq': 'new_v', 'new_v_g_mk': 'new_v', 'new_v_w_out': 'new_v', 'new_v_w_gate_up': 'new_v', 'new_v_w_down': 'new_v'}


def _forward(args):
    return _fwd_reference(*[args[k] for k in FWD_PARAMS])


def _output_shape():
    def fwd():
        inp = _fwd_setup_inputs(0)
        return _fwd_reference(*[inp[k] for k in FWD_PARAMS])
    out = _jax.eval_shape(fwd)
    return out.shape, out.dtype

N_MICROBATCH = 1
ADAM_LR = 0.001
ADAM_B1 = 0.9
ADAM_B2 = 0.999
ADAM_EPS = 1e-08
ADAM_WD = 0.01
ADAM_STEP = 10
PER_EXAMPLE_BATCH_AXIS = {'x': 0, 'mem': 0, 'loss_target': 0}
SHARED_INPUTS = []
_WEIGHT_DTYPES = {'g_mix': _jnp.float32, 'g_ffn': _jnp.float32, 'w_in_a': _jnp.float32, 'g_v_a': _jnp.float32, 'w_spatial': _jnp.float32, 'b_spatial': _jnp.float32, 'w_in_b': _jnp.float32, 'g_q_b': _jnp.float32, 'g_k_b': _jnp.float32, 'g_mem': _jnp.float32, 'w_mem_kv': _jnp.float32, 'g_mq': _jnp.float32, 'g_mk': _jnp.float32, 'w_out': _jnp.float32, 'w_gate_up': _jnp.float32, 'w_down': _jnp.float32}
MOMENT_SCALE = {'g_mix': 4.035416e+00, 'g_ffn': 1.237249e+01, 'w_in_a': 2.604247e-01, 'g_v_a': 6.697008e+00, 'w_spatial': 1.259605e+00, 'b_spatial': 2.808879e+00, 'w_in_b': 4.537780e-02, 'g_q_b': 4.646739e-01, 'g_k_b': 4.655337e-01, 'g_mem': 6.400526e-02, 'w_mem_kv': 6.888430e-02, 'g_mq': 5.833005e-01, 'g_mk': 5.824130e-01, 'w_out': 1.696689e-01, 'w_gate_up': 1.167387e-01, 'w_down': 1.892123e-01}


def _to_microbatches(a, axis):
    t = _jnp.moveaxis(a, axis, 0)
    t = t.reshape((N_MICROBATCH, t.shape[0] // N_MICROBATCH) + t.shape[1:])
    return _jnp.moveaxis(t, 1, axis + 1)


def setup_inputs(seed: int = 0) -> dict:
    inp = _fwd_setup_inputs(seed)
    key = _jax.random.fold_in(_jax.random.key(seed), 7919)
    shape, _ = _output_shape()
    out = dict(inp)
    out["loss_target"] = _jax.random.normal(_jax.random.fold_in(key, 0), shape, _jnp.float32)
    for i, name in enumerate(TWIN_WEIGHTS):
        w = inp[name].astype(_jnp.float32)
        if MOMENT_SCALE is None:
            s = _jnp.sqrt(_jnp.mean(_jnp.square(w)) + 1e-30)
        else:
            s = MOMENT_SCALE[name]
        km, kv = _jax.random.split(_jax.random.fold_in(key, i + 1))
        out[name] = w
        out["m_" + name] = s * _jax.random.normal(km, w.shape, _jnp.float32)
        out["v_" + name] = (s * s) * _jax.random.uniform(kv, w.shape, _jnp.float32, 0.5, 1.5)
    if N_MICROBATCH > 1:
        for name, axis in PER_EXAMPLE_BATCH_AXIS.items():
            out[name] = _to_microbatches(out[name], axis)
    return {'x': out['x'], 'mem': out['mem'], 'g_mix': out['g_mix'], 'g_ffn': out['g_ffn'], 'w_in_a': out['w_in_a'], 'g_v_a': out['g_v_a'], 'w_spatial': out['w_spatial'], 'b_spatial': out['b_spatial'], 'w_in_b': out['w_in_b'], 'g_q_b': out['g_q_b'], 'g_k_b': out['g_k_b'], 'g_mem': out['g_mem'], 'w_mem_kv': out['w_mem_kv'], 'g_mq': out['g_mq'], 'g_mk': out['g_mk'], 'w_out': out['w_out'], 'w_gate_up': out['w_gate_up'], 'w_down': out['w_down'], 'loss_target': out['loss_target'], 'm_g_mix': out['m_g_mix'], 'm_g_ffn': out['m_g_ffn'], 'm_w_in_a': out['m_w_in_a'], 'm_g_v_a': out['m_g_v_a'], 'm_w_spatial': out['m_w_spatial'], 'm_b_spatial': out['m_b_spatial'], 'm_w_in_b': out['m_w_in_b'], 'm_g_q_b': out['m_g_q_b'], 'm_g_k_b': out['m_g_k_b'], 'm_g_mem': out['m_g_mem'], 'm_w_mem_kv': out['m_w_mem_kv'], 'm_g_mq': out['m_g_mq'], 'm_g_mk': out['m_g_mk'], 'm_w_out': out['m_w_out'], 'm_w_gate_up': out['m_w_gate_up'], 'm_w_down': out['m_w_down'], 'v_g_mix': out['v_g_mix'], 'v_g_ffn': out['v_g_ffn'], 'v_w_in_a': out['v_w_in_a'], 'v_g_v_a': out['v_g_v_a'], 'v_w_spatial': out['v_w_spatial'], 'v_b_spatial': out['v_b_spatial'], 'v_w_in_b': out['v_w_in_b'], 'v_g_q_b': out['v_g_q_b'], 'v_g_k_b': out['v_g_k_b'], 'v_g_mem': out['v_g_mem'], 'v_w_mem_kv': out['v_w_mem_kv'], 'v_g_mq': out['v_g_mq'], 'v_g_mk': out['v_g_mk'], 'v_w_out': out['v_w_out'], 'v_w_gate_up': out['v_w_gate_up'], 'v_w_down': out['v_w_down']}


def _loss(weights, diff, rest, loss_target):
    with _jax.named_scope("forward"):
        args = {**rest, TWIN_DIFF_INPUT: diff, **{k: w.astype(_WEIGHT_DTYPES[k]) for k, w in weights.items()}}
        y = _forward(args)
    with _jax.named_scope("loss_head"):
        err = _jnp.square(y.astype(_jnp.float32) - loss_target)
        return 0.5 * _jnp.sum(_jnp.mean(err, axis=-1)) if err.ndim else 0.5 * err


def _adamw(w, g, m, v):
    m = ADAM_B1 * m + (1.0 - ADAM_B1) * g
    v = ADAM_B2 * v + (1.0 - ADAM_B2) * _jnp.square(g)
    m_hat = m / (1.0 - ADAM_B1 ** ADAM_STEP)
    v_hat = v / (1.0 - ADAM_B2 ** ADAM_STEP)
    delta = -ADAM_LR * (m_hat / (_jnp.sqrt(v_hat) + ADAM_EPS) + ADAM_WD * w)
    return delta, m, v


def reference(x, mem, g_mix, g_ffn, w_in_a, g_v_a, w_spatial, b_spatial, w_in_b, g_q_b, g_k_b, g_mem, w_mem_kv, g_mq, g_mk, w_out, w_gate_up, w_down, loss_target, m_g_mix, m_g_ffn, m_w_in_a, m_g_v_a, m_w_spatial, m_b_spatial, m_w_in_b, m_g_q_b, m_g_k_b, m_g_mem, m_w_mem_kv, m_g_mq, m_g_mk, m_w_out, m_w_gate_up, m_w_down, v_g_mix, v_g_ffn, v_w_in_a, v_g_v_a, v_w_spatial, v_b_spatial, v_w_in_b, v_g_q_b, v_g_k_b, v_g_mem, v_w_mem_kv, v_g_mq, v_g_mk, v_w_out, v_w_gate_up, v_w_down):
    given = dict(x=x, mem=mem, g_mix=g_mix, g_ffn=g_ffn, w_in_a=w_in_a, g_v_a=g_v_a, w_spatial=w_spatial, b_spatial=b_spatial, w_in_b=w_in_b, g_q_b=g_q_b, g_k_b=g_k_b, g_mem=g_mem, w_mem_kv=w_mem_kv, g_mq=g_mq, g_mk=g_mk, w_out=w_out, w_gate_up=w_gate_up, w_down=w_down, loss_target=loss_target, m_g_mix=m_g_mix, m_g_ffn=m_g_ffn, m_w_in_a=m_w_in_a, m_g_v_a=m_g_v_a, m_w_spatial=m_w_spatial, m_b_spatial=m_b_spatial, m_w_in_b=m_w_in_b, m_g_q_b=m_g_q_b, m_g_k_b=m_g_k_b, m_g_mem=m_g_mem, m_w_mem_kv=m_w_mem_kv, m_g_mq=m_g_mq, m_g_mk=m_g_mk, m_w_out=m_w_out, m_w_gate_up=m_w_gate_up, m_w_down=m_w_down, v_g_mix=v_g_mix, v_g_ffn=v_g_ffn, v_w_in_a=v_w_in_a, v_g_v_a=v_g_v_a, v_w_spatial=v_w_spatial, v_b_spatial=v_b_spatial, v_w_in_b=v_w_in_b, v_g_q_b=v_g_q_b, v_g_k_b=v_g_k_b, v_g_mem=v_g_mem, v_w_mem_kv=v_w_mem_kv, v_g_mq=v_g_mq, v_g_mk=v_g_mk, v_w_out=v_w_out, v_w_gate_up=v_w_gate_up, v_w_down=v_w_down)
    weights = {n: given[n] for n in TWIN_WEIGHTS}
    shared = {n: given[n] for n in SHARED_INPUTS}
    per_example = {n: given[n] for n in ['x', 'mem']}
    grad_fn = _jax.value_and_grad(_loss, argnums=(0, 1))

    def one_microbatch(ex, loss_target):
        ex = dict(ex)
        diff = ex.pop(TWIN_DIFF_INPUT)
        return grad_fn(weights, diff, {**shared, **ex}, loss_target)

    if N_MICROBATCH == 1:
        loss, (grad_w, grad_x) = one_microbatch(per_example, given["loss_target"])
    else:
        def body(carry, xs):
            loss_sum, grad_sum = carry
            l_k, (gw_k, gx_k) = one_microbatch(xs[0], xs[1])
            with _jax.named_scope("update"):
                return (loss_sum + l_k, _jax.tree.map(_jnp.add, grad_sum, gw_k)), gx_k

        init = (_jnp.zeros((), _jnp.float32), _jax.tree.map(_jnp.zeros_like, weights))
        (loss, grad_w), grad_x = _jax.lax.scan(body, init, (per_example, given["loss_target"]))
    with _jax.named_scope("update"):
        delta_w, new_m, new_v = {}, {}, {}
        for n in TWIN_WEIGHTS:
            delta_w[n], new_m[n], new_v[n] = _adamw(weights[n], grad_w[n], given["m_" + n], given["v_" + n])
    return (loss, grad_x, *[grad_w[n] for n in TWIN_WEIGHTS], *[delta_w[n] for n in TWIN_WEIGHTS],
            *[new_m[n] for n in TWIN_WEIGHTS], *[new_v[n] for n in TWIN_WEIGHTS])
```

```python
import jax
import jax.numpy as jnp
import numpy as np
from jax import lax
from jax.experimental import pallas as pl
from jax.experimental.pallas import tpu as pltpu

_F32 = jnp.float32
_MXU = jnp.bfloat16
_WIRE = jnp.bfloat16
_KW = {}

EPS = 1e-6
HEAD = 128
CHUNK = 128
GRID_W = 64
ROPE_THETA = 10000.0
ADAM_LR, ADAM_B1, ADAM_B2, ADAM_EPS, ADAM_WD, ADAM_STEP = 0.001, 0.9, 0.999, 1e-08, 0.01, 10
_SQRT_HALF = float(np.sqrt(0.5))
_INV_SQRT_2PI = float(1.0 / np.sqrt(2.0 * np.pi))
_VMEM_LIMIT = 56 * 1024 * 1024
_MESH = pl.DeviceIdType.MESH

_NN = (((1,), (0,)), ((), ()))
_NT = (((1,), (1,)), ((), ()))
_TN = (((0,), (0,)), ((), ()))

S = jax.ShapeDtypeStruct
BS = pl.BlockSpec
_ANY = pl.BlockSpec(memory_space=pl.ANY)


def _tile(n, pref, mult=128):
    if n <= pref:
        return n
    d = (pref // mult) * mult
    while d >= mult:
        if n % d == 0:
            return d
        d -= mult
    raise ValueError(f"no tile for {n} (pref {pref}, mult {mult})")


def _mo(v, m):
    return v if isinstance(v, int) else pl.multiple_of(v, m)


def _cp(*sem, **kw):
    return pltpu.CompilerParams(dimension_semantics=sem or None, vmem_limit_bytes=_VMEM_LIMIT, **kw)


def _call(body, name, **kw):
    return pl.pallas_call(body, name=name, **kw, **_KW)


def _dot(a, b, dn=_NN):
    return lax.dot_general(a, b, dn, preferred_element_type=_F32)


def _gelu(x):
    return 0.5 * x * (1.0 + lax.erf(x * _SQRT_HALF))


def _gelu_grad(x):
    return 0.5 * (1.0 + lax.erf(x * _SQRT_HALF)) + x * jnp.exp(-0.5 * x * x) * _INV_SQRT_2PI


def _rstd(x):
    return lax.rsqrt(jnp.mean(x * x, axis=-1, keepdims=True) + EPS)


def _norm_bwd(dout, xhat, r, g):
    dy = dout * g
    return r * (dy - xhat * jnp.mean(dy * xhat, axis=-1, keepdims=True))


def _softmax(s):
    e = jnp.exp(s - jnp.max(s, axis=-1, keepdims=True))
    return e * (1.0 / jnp.sum(e, axis=-1, keepdims=True))


def _mm(name, a, b, a_spec, b_spec, dn, grid, acc_shape, out_shape, out_specs, epilogue, extra=(), extra_specs=()):
    nk = grid[2]
    n_ex = len(extra)
    multi = isinstance(out_shape, (list, tuple))
    n_out = len(out_shape) if multi else 1

    def body(*refs):
        a_ref, b_ref = refs[0], refs[1]
        ex = refs[2:2 + n_ex]
        outs = refs[2 + n_ex:2 + n_ex + n_out]

        def prod():
            return _dot(a_ref[...].astype(_MXU), b_ref[...].astype(_MXU), dn)

        if nk == 1:
            epilogue(prod(), ex, outs)
        else:
            acc = refs[-1]
            k = pl.program_id(2)

            @pl.when(k == 0)
            def _():
                acc[...] = jnp.zeros_like(acc)

            acc[...] += prod()

            @pl.when(k == nk - 1)
            def _():
                epilogue(acc[...], ex, outs)

    return _call(
        body, name, grid=grid, in_specs=[a_spec, b_spec, *extra_specs], out_specs=out_specs, out_shape=out_shape,
        scratch_shapes=[] if nk == 1 else [pltpu.VMEM(acc_shape, _F32)],
        compiler_params=_cp("parallel", "parallel", "arbitrary"),
    )(a, b, *extra)


def _ep_store(acc, ex, outs):
    for o in outs:
        o[...] = acc.astype(o.dtype)


def _ep_add(acc, ex, outs):
    outs[0][...] = (acc + ex[0][...]).astype(outs[0].dtype)


def _mm_nn(name, a, b, out_dtype=_F32, add=None):
    M, K = a.shape
    N = b.shape[1]
    tm, tn, tk = _tile(M, 1024, 8), _tile(N, 512), _tile(K, 2048)
    o_spec = BS((tm, tn), lambda i, j, k: (i, j))
    return _mm(name, a, b, BS((tm, tk), lambda i, j, k: (i, k)), BS((tk, tn), lambda i, j, k: (k, j)), _NN,
               (M // tm, N // tn, K // tk), (tm, tn), S((M, N), out_dtype), o_spec,
               _ep_store if add is None else _ep_add,
               extra=() if add is None else (add,), extra_specs=() if add is None else (o_spec,))


def _mm_nt(name, a, b, out_dtype=_F32):
    M, K = a.shape
    N = b.shape[0]
    tm, tn, tk = _tile(M, 1024, 8), _tile(N, 512), _tile(K, 2048)
    return _mm(name, a, b, BS((tm, tk), lambda i, j, k: (i, k)), BS((tn, tk), lambda i, j, k: (j, k)), _NT,
               (M // tm, N // tn, K // tk), (tm, tn), S((M, N), out_dtype), BS((tm, tn), lambda i, j, k: (i, j)), _ep_store)


def _mm_tn_dual(name, a, b):
    K, M = a.shape
    N = b.shape[1]
    tm, tn, tk = _tile(M, 1024), _tile(N, 512), _tile(K, 1024, 16)
    o_spec = BS((tm, tn), lambda i, j, k: (i, j))
    return _mm(name, a, b, BS((tk, tm), lambda i, j, k: (k, i)), BS((tk, tn), lambda i, j, k: (k, j)), _TN,
               (M // tm, N // tn, K // tk), (tm, tn), [S((M, N), _F32), S((M, N), _WIRE)], [o_spec, o_spec], _ep_store)


def _ffn_gate_up(h2, wgu):
    T, D = h2.shape
    F = wgu.shape[1] // 2
    tm, tn = _tile(T, 1024, 8), _tile(F, 512)
    nj = F // tn

    def body(a_ref, bg_ref, bu_ref, act_ref, gu_ref):
        a = a_ref[...]
        g = _dot(a, bg_ref[...])
        u = _dot(a, bu_ref[...])
        gu_ref[0] = g
        gu_ref[1] = u
        act_ref[...] = (g * (1.0 / (1.0 + jnp.exp(-g))) * u).astype(act_ref.dtype)

    return _call(
        body, "ffn_gate_up", grid=(T // tm, nj),
        in_specs=[BS((tm, D), lambda i, j: (i, 0)), BS((D, tn), lambda i, j: (0, j)), BS((D, tn), lambda i, j: (0, j + nj))],
        out_specs=[BS((tm, tn), lambda i, j: (i, j)), BS((2, tm, tn), lambda i, j: (0, i, j))],
        out_shape=[S((T, F), _MXU), S((2, T, F), _F32)],
        compiler_params=_cp("parallel", "parallel"),
    )(h2, wgu, wgu)


def _ffn_dact(dxm, wdown, gu):
    T, D = dxm.shape
    F = wdown.shape[0]
    tm, tn = _tile(T, 1024, 8), _tile(F, 512)

    def body(a_ref, b_ref, gu_ref, o_ref):
        d = _dot(a_ref[...], b_ref[...], _NT)
        g, u = gu_ref[0], gu_ref[1]
        sg = 1.0 / (1.0 + jnp.exp(-g))
        o_ref[0] = (d * u * (sg * (1.0 + g * (1.0 - sg)))).astype(o_ref.dtype)
        o_ref[1] = (d * (g * sg)).astype(o_ref.dtype)

    return _call(
        body, "ffn_dact", grid=(T // tm, F // tn),
        in_specs=[BS((tm, D), lambda i, j: (i, 0)), BS((tn, D), lambda i, j: (j, 0)), BS((2, tm, tn), lambda i, j: (0, i, j))],
        out_specs=BS((2, tm, tn), lambda i, j: (0, i, j)), out_shape=S((2, T, F), _MXU),
        compiler_params=_cp("parallel", "parallel"),
    )(dxm, wdown, gu)


def _ffn_dh(dgu, wgu):
    _, T, F = dgu.shape
    D = wgu.shape[0]
    tm, tn, tk = _tile(T, 1024, 8), _tile(D, 512), _tile(F, 2048)
    nkf = F // tk
    return _mm("ffn_dh", dgu, wgu, BS((None, tm, tk), lambda i, j, k: (k // nkf, i, k % nkf)),
               BS((tn, tk), lambda i, j, k: (j, k)), _NT, (T // tm, D // tn, 2 * nkf), (tm, tn),
               S((T, D), _F32), BS((tm, tn), lambda i, j, k: (i, j)), _ep_store)


def _ffn_dwgu(h2, dgu):
    _, T, F = dgu.shape
    D = h2.shape[1]
    tm, tn, tk = _tile(D, 1024), _tile(F, 512), _tile(T, 1024, 16)
    njf = F // tn
    o_spec = BS((tm, tn), lambda i, j, k: (i, j))
    return _mm("ffn_dwgu", h2, dgu, BS((tk, tm), lambda i, j, k: (k, i)),
               BS((None, tk, tn), lambda i, j, k: (j // njf, k, j % njf)), _TN, (D // tm, 2 * njf, T // tk), (tm, tn),
               [S((D, 2 * F), _F32), S((D, 2 * F), _WIRE)], [o_spec, o_spec], _ep_store)


def _rmsnorm_fwd(x, g, name):
    T, D = x.shape
    tr = _tile(T, 512, 8)

    def body(x_ref, g_ref, o_ref):
        xv = x_ref[...]
        o_ref[...] = (xv * _rstd(xv) * g_ref[...]).astype(o_ref.dtype)

    row = BS((tr, D), lambda i: (i, 0))
    return _call(body, name, grid=(T // tr,), in_specs=[row, BS((1, D), lambda i: (0, 0))], out_specs=row,
                 out_shape=S((T, D), _MXU), compiler_params=_cp("parallel"))(x, g.reshape(1, D))


def _rmsnorm_bwd(x, g, dh, dres, name):
    T, D = x.shape
    tr = _tile(T, 256, 8)
    has_res = dres is not None

    def body(*refs):
        x_ref, g_ref, dh_ref = refs[:3]
        dx_ref, dxm_ref, dg_ref = refs[-3:]

        @pl.when(pl.program_id(0) == 0)
        def _():
            dg_ref[...] = jnp.zeros_like(dg_ref)

        xv = x_ref[...]
        r = _rstd(xv)
        xhat = xv * r
        dh_v = dh_ref[...]
        dg_ref[...] += jnp.sum(dh_v * xhat, axis=0, keepdims=True)
        dx = _norm_bwd(dh_v, xhat, r, g_ref[...])
        if has_res:
            dx = dx + refs[3][...]
        dx_ref[...] = dx
        dxm_ref[...] = dx.astype(dxm_ref.dtype)

    row = BS((tr, D), lambda i: (i, 0))
    vec = BS((1, D), lambda i: (0, 0))
    return _call(body, name, grid=(T // tr,), in_specs=[row, vec, row] + ([row] if has_res else []),
                 out_specs=[row, row, vec], out_shape=[S((T, D), _F32), S((T, D), _MXU), S((1, D), _F32)],
                 compiler_params=_cp("arbitrary"))(x, g.reshape(1, D), dh, *([dres] if has_res else []))


def _loss_head(y, target):
    T, D = y.shape
    tr = _tile(T, 256, 8)

    def body(y_ref, t_ref, dy_ref, dym_ref, acc_ref):
        @pl.when(pl.program_id(0) == 0)
        def _():
            acc_ref[...] = jnp.zeros_like(acc_ref)

        err = y_ref[...] - t_ref[...]
        acc_ref[...] += jnp.sum(jnp.sum(err * err, axis=-1, keepdims=True), axis=0, keepdims=True)
        dy = err * (1.0 / D)
        dy_ref[...] = dy
        dym_ref[...] = dy.astype(dym_ref.dtype)

    row = BS((tr, D), lambda i: (i, 0))
    return _call(body, "loss_head", grid=(T // tr,), in_specs=[row, row],
                 out_specs=[row, row, BS((1, 128), lambda i: (0, 0))],
                 out_shape=[S((T, D), _F32), S((T, D), _MXU), S((1, 128), _F32)],
                 compiler_params=_cp("arbitrary"))(y, target)


def _mixa_blocks(T):
    return 2 if T % (2 * CHUNK) == 0 else 1


def _mixer_a_fwd(z, gv, ws_m, b_t, TOK):
    T = z.shape[0]
    G = TOK // HEAD
    CB = _mixa_blocks(T)
    R = CB * CHUNK

    def body(z_ref, gv_ref, ws_ref, bt_ref, o_ref):
        u = _gelu(z_ref[:, :TOK])
        v = _gelu(z_ref[:, TOK:])
        vn = (v * _rstd(v) * gv_ref[...]).astype(_MXU)
        for c in range(CB):
            rows = slice(c * CHUNK, (c + 1) * CHUNK)
            for g in range(G):
                cols = slice(g * HEAD, (g + 1) * HEAD)
                s = _dot(ws_ref[g], vn[rows, cols]) + bt_ref[:, g:g + 1]
                o_ref[rows, cols] = (u[rows, cols] * s).astype(o_ref.dtype)

    return _call(
        body, "mixer_a_fwd", grid=(T // R,),
        in_specs=[BS((R, 2 * TOK), lambda i: (i, 0)), BS((1, TOK), lambda i: (0, 0)),
                  BS((G, CHUNK, CHUNK), lambda i: (0, 0, 0)), BS((CHUNK, G), lambda i: (0, 0))],
        out_specs=BS((R, TOK), lambda i: (i, 0)), out_shape=S((T, TOK), _MXU), compiler_params=_cp("parallel"),
    )(z, gv.reshape(1, TOK), ws_m, b_t)


def _mixer_a_bwd(z, dcat, gv, ws_m, wst_m, b_t, TOK):
    T = z.shape[0]
    G = TOK // HEAD
    CB = _mixa_blocks(T)
    R = CB * CHUNK
    n = T // R

    def body(z_ref, d_ref, gv_ref, ws_ref, wst_ref, bt_ref, dz_ref, dws_ref, db_ref, dgv_ref, dvn_scr):
        i = pl.program_id(0)

        @pl.when(i == 0)
        def _():
            dws_ref[...] = jnp.zeros_like(dws_ref)
            db_ref[...] = jnp.zeros_like(db_ref)
            dgv_ref[...] = jnp.zeros_like(dgv_ref)

        zu = z_ref[:, :TOK]
        zv = z_ref[:, TOK:]
        u = _gelu(zu)
        v = _gelu(zv)
        r = _rstd(v)
        vhat = v * r
        gvv = gv_ref[...]
        vn = (vhat * gvv).astype(_MXU)
        d = d_ref[...]
        gpu = _gelu_grad(zu)
        for c in range(CB):
            rows = slice(c * CHUNK, (c + 1) * CHUNK)
            for g in range(G):
                cols = slice(g * HEAD, (g + 1) * HEAD)
                vn_cg = vn[rows, cols]
                s = _dot(ws_ref[g], vn_cg) + bt_ref[:, g:g + 1]
                d_cg = d[rows, cols]
                dz_ref[rows, cols] = (d_cg * s * gpu[rows, cols]).astype(dz_ref.dtype)
                ds = d_cg * u[rows, cols]
                ds_m = ds.astype(_MXU)
                dvn_scr[rows, cols] = _dot(wst_ref[g], ds_m)
                dws_ref[g] += _dot(ds_m, vn_cg, _NT)
                db_ref[g] += ds
        dvn = dvn_scr[...]
        dgv_ref[...] += jnp.sum(dvn * vhat, axis=0, keepdims=True)
        dv = _norm_bwd(dvn, vhat, r, gvv)
        dz_ref[:, TOK:] = (dv * _gelu_grad(zv)).astype(dz_ref.dtype)

        @pl.when(i == n - 1)
        def _():
            for g in range(G):
                db_ref[g] = jnp.broadcast_to(jnp.sum(db_ref[g], axis=1, keepdims=True), (CHUNK, CHUNK))

    full3 = BS((G, CHUNK, CHUNK), lambda i: (0, 0, 0))
    return _call(
        body, "mixer_a_bwd", grid=(n,),
        in_specs=[BS((R, 2 * TOK), lambda i: (i, 0)), BS((R, TOK), lambda i: (i, 0)), BS((1, TOK), lambda i: (0, 0)),
                  full3, full3, BS((CHUNK, G), lambda i: (0, 0))],
        out_specs=[BS((R, 2 * TOK), lambda i: (i, 0)), full3, full3, BS((1, TOK), lambda i: (0, 0))],
        out_shape=[S((T, 2 * TOK), _MXU), S((G, CHUNK, CHUNK), _F32), S((G, CHUNK, CHUNK), _F32), S((1, TOK), _F32)],
        scratch_shapes=[pltpu.VMEM((R, TOK), _F32)], compiler_params=_cp("arbitrary"),
    )(z, dcat, gv.reshape(1, TOK), ws_m, wst_m, b_t)


def _rope_tables(T):
    n_rows = T // GRID_W
    rows = jnp.broadcast_to(jnp.arange(n_rows)[:, None], (n_rows, GRID_W)).reshape(T)
    cols = jnp.broadcast_to(jnp.arange(GRID_W)[None, :], (n_rows, GRID_W)).reshape(T)
    pairs = HEAD // 4
    freqs = ROPE_THETA ** (-jnp.arange(pairs, dtype=_F32) / pairs)
    ang_r = rows.astype(_F32)[:, None] * freqs
    ang_c = cols.astype(_F32)[:, None] * freqs
    ang = jnp.concatenate([ang_r, ang_r, ang_c, ang_c], axis=-1)
    cos, sin = jnp.cos(ang), jnp.sin(ang)
    first = (jnp.arange(HEAD) % (HEAD // 2)) < (HEAD // 4)
    return cos, jnp.where(first, -sin, 0.0), jnp.where(first, 0.0, sin)


def _rope(x, cs, sa, sb):
    return x * cs + pltpu.roll(x, 96, 1) * sa + pltpu.roll(x, 32, 1) * sb


def _qk_rope_fwd(z, gq, gk, tabs, TOK, KV):
    T = z.shape[0]
    R = _tile(T, 512, 8)
    W = TOK + 2 * KV

    def body(z_ref, gq_ref, gk_ref, cos_ref, sa_ref, sb_ref, q_ref, k_ref, v_ref):
        cs, sa, sb = cos_ref[...], sa_ref[...], sb_ref[...]
        for h in range((TOK + KV) // HEAD):
            cols = slice(h * HEAD, (h + 1) * HEAD)
            xv = z_ref[:, cols]
            xn = xv * _rstd(xv) * (gq_ref[...] if h < TOK // HEAD else gk_ref[...])
            out = _rope(xn, cs, sa, sb)
            if h < TOK // HEAD:
                q_ref[:, cols] = out.astype(q_ref.dtype)
            else:
                k_ref[:, h * HEAD - TOK:(h + 1) * HEAD - TOK] = out.astype(k_ref.dtype)
        v_ref[...] = z_ref[:, TOK + KV:].astype(v_ref.dtype)

    vec = BS((1, HEAD), lambda i: (0, 0))
    tab = BS((R, HEAD), lambda i: (i, 0))
    return _call(
        body, "qk_rope_fwd", grid=(T // R,), in_specs=[BS((R, W), lambda i: (i, 0)), vec, vec, tab, tab, tab],
        out_specs=[BS((R, TOK), lambda i: (i, 0)), BS((R, KV), lambda i: (i, 0)), BS((R, KV), lambda i: (i, 0))],
        out_shape=[S((T, TOK), _MXU), S((T, KV), _MXU), S((T, KV), _MXU)], compiler_params=_cp("parallel"),
    )(z, gq.reshape(1, HEAD), gk.reshape(1, HEAD), *tabs)


def _qk_rope_bwd(z, dq, dk, gq, gk, tabs, TOK, KV):
    T = z.shape[0]
    R = _tile(T, 512, 8)
    W = TOK + KV

    def body(z_ref, dq_ref, dk_ref, gq_ref, gk_ref, cos_ref, sa_ref, sb_ref, dz_ref, dgq_ref, dgk_ref):
        @pl.when(pl.program_id(0) == 0)
        def _():
            dgq_ref[...] = jnp.zeros_like(dgq_ref)
            dgk_ref[...] = jnp.zeros_like(dgk_ref)

        cs, sa, sb = cos_ref[...], sa_ref[...], sb_ref[...]
        for h in range(W // HEAD):
            cols = slice(h * HEAD, (h + 1) * HEAD)
            is_q = h < TOK // HEAD
            do = dq_ref[:, cols] if is_q else dk_ref[:, h * HEAD - TOK:(h + 1) * HEAD - TOK]
            dxn = do * cs - pltpu.roll(do, 96, 1) * sa - pltpu.roll(do, 32, 1) * sb
            xv = z_ref[:, cols]
            r = _rstd(xv)
            xhat = xv * r
            dg_ref = dgq_ref if is_q else dgk_ref
            dg_ref[...] += jnp.sum(dxn * xhat, axis=0, keepdims=True)
            dz_ref[:, cols] = _norm_bwd(dxn, xhat, r, gq_ref[...] if is_q else gk_ref[...]).astype(dz_ref.dtype)

    vec = BS((1, HEAD), lambda i: (0, 0))
    tab = BS((R, HEAD), lambda i: (i, 0))
    return _call(
        body, "qk_rope_bwd", grid=(T // R,),
        in_specs=[BS((R, W), lambda i: (i, 0)), BS((R, TOK), lambda i: (i, 0)), BS((R, KV), lambda i: (i, 0)), vec, vec, tab, tab, tab],
        out_specs=[BS((R, W), lambda i: (i, 0)), vec, vec],
        out_shape=[S((T, W), _MXU), S((1, HEAD), _F32), S((1, HEAD), _F32)], compiler_params=_cp("arbitrary"),
    )(z, dq, dk, gq.reshape(1, HEAD), gk.reshape(1, HEAD), *tabs)


def _attn_fwd(q, k, v, QPK):
    T, TOK = q.shape
    KVH = k.shape[1] // HEAD
    tq = _tile(T, 256, 8)
    W = QPK * HEAD
    scale = HEAD ** -0.5

    def body(q_ref, k_ref, v_ref, o_ref):
        kk, vv = k_ref[...], v_ref[...]
        for g in range(QPK):
            cols = slice(g * HEAD, (g + 1) * HEAD)
            p = _softmax(_dot(q_ref[:, cols], kk, _NT) * scale)
            o_ref[:, cols] = _dot(p.astype(_MXU), vv).astype(o_ref.dtype)

    qs = BS((tq, W), lambda h, i: (i, h))
    ks = BS((T, HEAD), lambda h, i: (0, h))
    return _call(body, "attn_fwd", grid=(KVH, T // tq), in_specs=[qs, ks, ks], out_specs=qs,
                 out_shape=S((T, TOK), _MXU), compiler_params=_cp("parallel", "parallel"))(q, k, v)


def _attn_bwd(q, k, v, dcat, QPK):
    T, TOK = q.shape
    KV = k.shape[1]
    KVH = KV // HEAD
    tq = _tile(T, 256, 8)
    nq = T // tq
    W = QPK * HEAD
    scale = HEAD ** -0.5

    def body(q_ref, k_ref, v_ref, do_ref, dq_ref, dk_ref, dv_ref, dk_acc, dv_acc):
        i = pl.program_id(1)

        @pl.when(i == 0)
        def _():
            dk_acc[...] = jnp.zeros_like(dk_acc)
            dv_acc[...] = jnp.zeros_like(dv_acc)

        kk, vv = k_ref[...], v_ref[...]
        for g in range(QPK):
            cols = slice(g * HEAD, (g + 1) * HEAD)
            qg = q_ref[:, cols]
            p = _softmax(_dot(qg, kk, _NT) * scale)
            do = do_ref[:, cols].astype(_MXU)
            dp = _dot(do, vv, _NT)
            ds = (p * (dp - jnp.sum(p * dp, axis=-1, keepdims=True)) * scale).astype(_MXU)
            dq_ref[:, cols] = _dot(ds, kk)
            dk_acc[...] += _dot(ds, qg, _TN)
            dv_acc[...] += _dot(p.astype(_MXU), do, _TN)

        @pl.when(i == nq - 1)
        def _():
            dk_ref[...] = dk_acc[...]
            dv_ref[...] = dv_acc[...].astype(dv_ref.dtype)

    qs = BS((tq, W), lambda h, i: (i, h))
    ks = BS((T, HEAD), lambda h, i: (0, h))
    return _call(
        body, "attn_bwd", grid=(KVH, nq), in_specs=[qs, ks, ks, qs], out_specs=[qs, ks, ks],
        out_shape=[S((T, TOK), _F32), S((T, KV), _F32), S((T, KV), _MXU)],
        scratch_shapes=[pltpu.VMEM((T, HEAD), _F32), pltpu.VMEM((T, HEAD), _F32)],
        compiler_params=_cp("parallel", "arbitrary"),
    )(q, k, v, dcat)


def _mem_fwd(z, qblk, kv, gmq, gmk, MEMW):
    T = z.shape[0]
    NM = kv.shape[0]
    tq = _tile(T, 512, 8)
    scale = HEAD ** -0.5

    def body(q_ref, kv_ref, gq_ref, gk_ref, o_ref):
        for h in range(MEMW // HEAD):
            cols = slice(h * HEAD, (h + 1) * HEAD)
            kx = kv_ref[:, cols]
            kn = (kx * _rstd(kx) * gk_ref[...]).astype(_MXU)
            vv = kv_ref[:, MEMW + h * HEAD:MEMW + (h + 1) * HEAD].astype(_MXU)
            qx = q_ref[:, cols]
            qn = (qx * _rstd(qx) * gq_ref[...]).astype(_MXU)
            p = _softmax(_dot(qn, kn, _NT) * scale)
            o_ref[:, cols] = _dot(p.astype(_MXU), vv).astype(o_ref.dtype)

    vec = BS((1, HEAD), lambda i: (0, 0))
    return _call(
        body, "mem_fwd", grid=(T // tq,),
        in_specs=[BS((tq, MEMW), lambda i: (i, qblk)), BS((NM, 2 * MEMW), lambda i: (0, 0)), vec, vec],
        out_specs=BS((tq, MEMW), lambda i: (i, 0)), out_shape=S((T, MEMW), _MXU), compiler_params=_cp("parallel"),
    )(z, kv, gmq.reshape(1, HEAD), gmk.reshape(1, HEAD))


def _mem_bwd(z, qblk, kv, gmq, gmk, dcat, dblk, MEMW):
    T = z.shape[0]
    NM = kv.shape[0]
    tq = _tile(T, 512, 8)
    scale = HEAD ** -0.5

    def body(q_ref, kv_ref, gq_ref, gk_ref, do_ref, dz_ref, dkn_ref, dv_ref, dgq_ref):
        @pl.when(pl.program_id(0) == 0)
        def _():
            dkn_ref[...] = jnp.zeros_like(dkn_ref)
            dv_ref[...] = jnp.zeros_like(dv_ref)
            dgq_ref[...] = jnp.zeros_like(dgq_ref)

        for h in range(MEMW // HEAD):
            cols = slice(h * HEAD, (h + 1) * HEAD)
            kx = kv_ref[:, cols]
            kn = (kx * _rstd(kx) * gk_ref[...]).astype(_MXU)
            vv = kv_ref[:, MEMW + h * HEAD:MEMW + (h + 1) * HEAD].astype(_MXU)
            qx = q_ref[:, cols]
            rq = _rstd(qx)
            qhat = qx * rq
            qn = (qhat * gq_ref[...]).astype(_MXU)
            p = _softmax(_dot(qn, kn, _NT) * scale)
            do = do_ref[:, cols].astype(_MXU)
            dp = _dot(do, vv, _NT)
            ds = (p * (dp - jnp.sum(p * dp, axis=-1, keepdims=True)) * scale).astype(_MXU)
            dqn = _dot(ds, kn)
            dkn_ref[:, cols] += _dot(ds, qn, _TN)
            dv_ref[:, cols] += _dot(p.astype(_MXU), do, _TN)
            dgq_ref[...] += jnp.sum(dqn * qhat, axis=0, keepdims=True)
            dz_ref[:, cols] = _norm_bwd(dqn, qhat, rq, gq_ref[...]).astype(dz_ref.dtype)

    vec = BS((1, HEAD), lambda i: (0, 0))
    kvs = BS((NM, MEMW), lambda i: (0, 0))
    return _call(
        body, "mem_bwd", grid=(T // tq,),
        in_specs=[BS((tq, MEMW), lambda i: (i, qblk)), BS((NM, 2 * MEMW), lambda i: (0, 0)), vec, vec,
                  BS((tq, MEMW), lambda i: (i, dblk))],
        out_specs=[BS((tq, MEMW), lambda i: (i, 0)), kvs, kvs, vec],
        out_shape=[S((T, MEMW), _MXU), S((NM, MEMW), _F32), S((NM, MEMW), _F32), S((1, HEAD), _F32)],
        compiler_params=_cp("arbitrary"),
    )(z, kv, gmq.reshape(1, HEAD), gmk.reshape(1, HEAD), dcat)


def _memkv_bwd(kv, dkn, dv, gmk, MEMW):
    NM = kv.shape[0]

    def body(kv_ref, dkn_ref, dv_ref, gk_ref, dkv_ref, dgk_ref):
        dgk = jnp.zeros((1, HEAD), _F32)
        for h in range(MEMW // HEAD):
            cols = slice(h * HEAD, (h + 1) * HEAD)
            kx = kv_ref[:, cols]
            r = _rstd(kx)
            khat = kx * r
            dkn = dkn_ref[:, cols]
            dgk = dgk + jnp.sum(dkn * khat, axis=0, keepdims=True)
            dkv_ref[:, cols] = _norm_bwd(dkn, khat, r, gk_ref[...]).astype(dkv_ref.dtype)
        dgk_ref[...] = dgk
        dkv_ref[:, MEMW:] = dv_ref[...].astype(dkv_ref.dtype)

    return _call(body, "memkv_bwd", out_shape=[S((NM, 2 * MEMW), _MXU), S((1, HEAD), _F32)],
                 compiler_params=_cp())(kv, dkn, dv, gmk.reshape(1, HEAD))


def _cast_wire(w, name):
    shape = w.shape
    w2 = w.reshape(-1, shape[-1])
    R, C = w2.shape
    tr, tc = _tile(R, 512, 16), _tile(C, 2048)

    def body(w_ref, o_ref):
        o_ref[...] = w_ref[...].astype(o_ref.dtype)

    blk = BS((tr, tc), lambda i, j: (i, j))
    out = _call(body, name, grid=(R // tr, C // tc), in_specs=[blk], out_specs=blk, out_shape=S((R, C), _WIRE),
                compiler_params=_cp("parallel", "parallel"))(w2)
    return out.reshape(shape)


def _adamw(w, g, m, v, name):
    shape = w.shape
    C = shape[-1]
    args = [a.reshape(-1, C) for a in (w, g, m, v)]
    R = args[0].shape[0]
    tr, tc = _tile(R, 256, 8), _tile(C, 1024)
    c_m = 1.0 / (1.0 - ADAM_B1 ** ADAM_STEP)
    c_v = 1.0 / (1.0 - ADAM_B2 ** ADAM_STEP)

    def body(w_ref, g_ref, m_ref, v_ref, go_ref, d_ref, mo_ref, vo_ref):
        gv = g_ref[...]
        mn = ADAM_B1 * m_ref[...] + (1.0 - ADAM_B1) * gv
        vn = ADAM_B2 * v_ref[...] + (1.0 - ADAM_B2) * (gv * gv)
        go_ref[...] = gv
        mo_ref[...] = mn
        vo_ref[...] = vn
        d_ref[...] = -ADAM_LR * ((mn * c_m) / (jnp.sqrt(vn * c_v) + ADAM_EPS) + ADAM_WD * w_ref[...])

    blk = BS((tr, tc), lambda i, j: (i, j))
    outs = _call(body, name, grid=(R // tr, C // tc), in_specs=[blk] * 4, out_specs=[blk] * 4,
                 out_shape=[S((R, C), _F32)] * 4, compiler_params=_cp("parallel", "parallel"))(*args)
    return [o.reshape(shape) for o in outs]


def _where_am_i():
    x, y, c = lax.axis_index("x"), lax.axis_index("y"), lax.axis_index("c")
    chips = [(1 - x, y), (x, 1 - y), (1 - x, 1 - y)]
    return x, y, c, 2 * x + y, chips, [2 * cx + cy for cx, cy in chips]


class _Shard:
    def __init__(self, R, C, by_cols):
        self.R, self.C, self.by_cols = R, C, by_cols
        self.Rs, self.Cs = (R, C // 4) if by_cols else (R // 4, C)
        self.Rh = self.Rs // 2
        self.Q = R // 8

    def full_piece(self, ref, j, cc):
        if self.by_cols:
            return ref.at[pl.ds(cc * self.Rh, self.Rh), pl.ds(_mo(j * self.Cs, 128), self.Cs)]
        return ref.at[pl.ds(_mo(j * self.Rs + cc * self.Rh, 16), self.Rh), :]

    def full_shard(self, ref, j):
        if self.by_cols:
            return ref.at[:, pl.ds(_mo(j * self.Cs, 128), self.Cs)]
        return ref.at[pl.ds(_mo(j * self.Rs, 16), self.Rs), :]

    def shard_half(self, ref, cc):
        return ref.at[pl.ds(_mo(cc * self.Rh, 16), self.Rh), :]

    def half_piece(self, ref, j):
        if self.by_cols:
            return ref.at[:, pl.ds(_mo(j * self.Cs, 128), self.Cs)]
        return ref.at[pl.ds(_mo(j * self.Rh, 16), self.Rh), :]


def _remote(src, dst, ssem, rsem, dev):
    return pltpu.make_async_remote_copy(src_ref=src, dst_ref=dst, send_sem=ssem, recv_sem=rsem, device_id=dev, device_id_type=_MESH)


def _all_gather_weights(wsh, sh, name):
    L = wsh.shape[0]

    def body(w_ref, *rest):
        outs = rest[:L]
        send_sems, recv_sems, local_sem = rest[L:]
        x, y, c, me, chips, chip_ids = _where_am_i()
        sib = (x, y, 1 - c)
        local = [pltpu.make_async_copy(w_ref.at[l], sh.full_shard(outs[l], me), local_sem.at[l]) for l in range(L)]
        for cp in local:
            cp.start()
        first, passed = [], []
        for l in range(L):
            for r in range(3):
                first.append(_remote(sh.shard_half(w_ref.at[l], c), sh.full_piece(outs[l], me, c),
                                     send_sems.at[l, r], recv_sems.at[l, r], (*chips[r], c)))
        for cp in first:
            cp.start()
        for l in range(L):
            for r in range(3):
                piece = sh.full_piece(outs[l], chip_ids[r], c)
                _remote(piece, piece, send_sems.at[l, r], recv_sems.at[l, r], sib).wait_recv()
                cp = _remote(piece, piece, send_sems.at[l, 3 + r], recv_sems.at[l, 3 + r], sib)
                cp.start()
                passed.append(cp)
        for l in range(L):
            for r in range(3):
                piece = sh.full_piece(outs[l], chip_ids[r], 1 - c)
                _remote(piece, piece, send_sems.at[l, 3 + r], recv_sems.at[l, 3 + r], sib).wait_recv()
        for cp in first + passed:
            cp.wait_send()
        for cp in local:
            cp.wait()

    return _call(
        body, name, in_specs=[_ANY], out_specs=[_ANY] * L, out_shape=[S((sh.R, sh.C), wsh.dtype)] * L,
        scratch_shapes=[pltpu.SemaphoreType.DMA((L, 6)), pltpu.SemaphoreType.DMA((L, 6)), pltpu.SemaphoreType.DMA((L,))],
        compiler_params=pltpu.CompilerParams(has_side_effects=True),
    )(wsh)


def _rs_pair_exchange(dws, sh, name):
    L = len(dws)
    n = 1 if sh.by_cols else 4

    def body(*refs):
        ins, outs = refs[:L], refs[L:2 * L]
        send_sems, recv_sems = refs[2 * L:]
        x, y, c, *_ = _where_am_i()
        sib = (x, y, 1 - c)
        cps = []
        for l in range(L):
            if sh.by_cols:
                cps.append(_remote(ins[l].at[pl.ds((1 - c) * sh.Rh, sh.Rh), :], outs[l], send_sems.at[l, 0], recv_sems.at[l, 0], sib))
            else:
                for j in range(4):
                    cps.append(_remote(sh.full_piece(ins[l], j, 1 - c), sh.half_piece(outs[l], j),
                                       send_sems.at[l, j], recv_sems.at[l, j], sib))
        for cp in cps:
            cp.start()
        for cp in cps:
            cp.wait()

    return _call(
        body, name, in_specs=[_ANY] * L, out_specs=[_ANY] * L, out_shape=[S((sh.R // 2, sh.C), _WIRE)] * L,
        scratch_shapes=[pltpu.SemaphoreType.DMA((L, n)), pltpu.SemaphoreType.DMA((L, n))],
        compiler_params=pltpu.CompilerParams(has_side_effects=True),
    )(*dws)


def _rs_pair_add(dw32, recv, sh, cidx):
    tr, tc = _tile(sh.Q, 512, 16), _tile(sh.C, 2048)
    nb = sh.Q // tr

    def body(c_ref, a_ref, b_ref, o32_ref, ow_ref):
        p = a_ref[...] + b_ref[...].astype(_F32)
        o32_ref[...] = p
        ow_ref[...] = p.astype(ow_ref.dtype)

    if sh.by_cols:
        a_map = lambda j, i, b, c: (c[0] * 4 * nb + j * nb + i, b)
    else:
        a_map = lambda j, i, b, c: (j * 2 * nb + c[0] * nb + i, b)
    h_spec = BS((tr, tc), lambda j, i, b, c: (j * nb + i, b))
    return _call(
        body, "rs_pair_add",
        grid_spec=pltpu.PrefetchScalarGridSpec(num_scalar_prefetch=1, grid=(4, nb, sh.C // tc),
                                               in_specs=[BS((tr, tc), a_map), h_spec], out_specs=[h_spec, h_spec]),
        out_shape=[S((sh.R // 2, sh.C), _F32), S((sh.R // 2, sh.C), _WIRE)],
        compiler_params=_cp("parallel", "parallel", "parallel"),
    )(cidx, dw32, recv)


def _rs_chip_exchange(pws, sh, name):
    L = len(pws)

    def body(*refs):
        ins, outs = refs[:L], refs[L:2 * L]
        send_sems, recv_sems = refs[2 * L:]
        x, y, c, me, chips, chip_ids = _where_am_i()
        cps = []
        for l in range(L):
            for r in range(3):
                cps.append(_remote(sh.half_piece(ins[l], chip_ids[r]), outs[l].at[r], send_sems.at[l, r], recv_sems.at[l, r],
                                   (*chips[r], c)))
        for cp in cps:
            cp.start()
        for cp in cps:
            cp.wait()

    return _call(
        body, name, in_specs=[_ANY] * L, out_specs=[_ANY] * L, out_shape=[S((3, sh.Rh, sh.Cs), _WIRE)] * L,
        scratch_shapes=[pltpu.SemaphoreType.DMA((L, 3)), pltpu.SemaphoreType.DMA((L, 3))],
        compiler_params=pltpu.CompilerParams(has_side_effects=True),
    )(*pws)


def _rs_chip_add(p32, recv, sh, iidx):
    tr, tc = _tile(sh.Rh, 512, 16), _tile(sh.Cs, 2048)
    nr, nc = sh.Rh // tr, sh.Cs // tc

    def body(i_ref, a_ref, b_ref, o_ref):
        o_ref[...] = a_ref[...] + b_ref[0].astype(_F32) + b_ref[1].astype(_F32) + b_ref[2].astype(_F32)

    if sh.by_cols:
        a_map = lambda a, b, i: (a, i[0] * nc + b)
    else:
        a_map = lambda a, b, i: (i[0] * nr + a, b)
    return _call(
        body, "rs_chip_add",
        grid_spec=pltpu.PrefetchScalarGridSpec(
            num_scalar_prefetch=1, grid=(nr, nc),
            in_specs=[BS((tr, tc), a_map), BS((3, tr, tc), lambda a, b, i: (0, a, b))], out_specs=BS((tr, tc), lambda a, b, i: (a, b))),
        out_shape=S((sh.Rh, sh.Cs), _F32), compiler_params=_cp("parallel", "parallel"),
    )(iidx, p32, recv)


def _rs_pair_share(rs, sh, name):
    L = len(rs)

    def body(*refs):
        ins, out = refs[:L], refs[L]
        send_sems, recv_sems, local_sem = refs[L + 1:]
        x, y, c, *_ = _where_am_i()
        sib = (x, y, 1 - c)
        local = [pltpu.make_async_copy(ins[l], sh.shard_half(out.at[l], c), local_sem.at[l]) for l in range(L)]
        cps = [_remote(ins[l], sh.shard_half(out.at[l], c), send_sems.at[l], recv_sems.at[l], sib) for l in range(L)]
        for cp in local + cps:
            cp.start()
        for l in range(L):
            other = sh.shard_half(out.at[l], 1 - c)
            _remote(other, other, send_sems.at[l], recv_sems.at[l], sib).wait_recv()
        for cp in cps:
            cp.wait_send()
        for cp in local:
            cp.wait()

    return _call(
        body, name, in_specs=[_ANY] * L, out_specs=_ANY, out_shape=S((L, sh.Rs, sh.Cs), _F32),
        scratch_shapes=[pltpu.SemaphoreType.DMA((L,)), pltpu.SemaphoreType.DMA((L,)), pltpu.SemaphoreType.DMA((L,))],
        compiler_params=pltpu.CompilerParams(has_side_effects=True),
    )(*rs)


def _all_reduce_small(xs):
    M = xs.shape[0]

    def body(x_ref, tot_ref, out_ref, send_sems, recv_sems, local_sem):
        x, y, c, me, chips, chip_ids = _where_am_i()
        sib = (x, y, 1 - c)

        def rows(dev):
            return out_ref.at[pl.ds(_mo((4 * dev[0] + 2 * dev[1] + dev[2]) * M, 8), M), :]

        def copy(k, block, to, src=None):
            return _remote(rows(block) if src is None else src, rows(block), send_sems.at[k], recv_sems.at[k], to)

        mine = pltpu.make_async_copy(x_ref, rows((x, y, c)), local_sem)
        mine.start()
        first = [copy(0, (x, y, c), sib, src=x_ref)]
        first += [copy(1 + j, (x, y, c), (*chip, c), src=x_ref) for j, chip in enumerate(chips)]
        for cp in first:
            cp.start()
        passed = [copy(4 + j, (*chip, c), sib) for j, chip in enumerate(chips)]
        for j, chip in enumerate(chips):
            copy(1 + j, (*chip, c), (x, y, c)).wait_recv()
            passed[j].start()
        copy(0, sib, (x, y, c)).wait_recv()
        for j, chip in enumerate(chips):
            copy(4 + j, (*chip, 1 - c), (x, y, c)).wait_recv()
        for cp in first + passed:
            cp.wait_send()
        mine.wait()
        tot = out_ref[pl.ds(0, M), :]
        for d in range(1, 8):
            tot = tot + out_ref[pl.ds(d * M, M), :]
        tot_ref[...] = tot

    vm = pl.BlockSpec(memory_space=pltpu.VMEM)
    return _call(
        body, "all_reduce_small", in_specs=[vm], out_specs=[vm, vm], out_shape=[S((M, 128), _F32), S((8 * M, 128), _F32)],
        scratch_shapes=[pltpu.SemaphoreType.DMA((7,)), pltpu.SemaphoreType.DMA((7,)), pltpu.SemaphoreType.DMA],
        compiler_params=_cp(has_side_effects=True),
    )(xs)[0]


def _reduce_scatter_grads(dw32s, dwws, sh, cidx, iidx, tag):
    recv_a = _rs_pair_exchange(dwws, sh, "rs_pair_exchange_" + tag)
    p = [_rs_pair_add(a, b, sh, cidx) for a, b in zip(dw32s, recv_a)]
    recv_b = _rs_chip_exchange([pw for _, pw in p], sh, "rs_chip_exchange_" + tag)
    r = [_rs_chip_add(p32, b, sh, iidx) for (p32, _), b in zip(p, recv_b)]
    return _rs_pair_share(r, sh, "rs_pair_share_" + tag)


def _pack(parts):
    out = []
    for p in parts:
        p2 = p.reshape(-1, 128)
        pad = (-p2.shape[0]) % 8
        out.append(jnp.pad(p2, ((0, pad), (0, 0))) if pad else p2)
    return jnp.concatenate(out, axis=0)


def _unpack(packed, like):
    out, at = [], 0
    for p in like:
        n = p.size // 128
        out.append(packed[at:at + n].reshape(p.shape))
        at += n + ((-n) % 8)
    return out


def kernel(x, mem, g_mix, g_ffn, w_in_a, g_v_a, w_spatial, b_spatial, w_in_b, g_q_b, g_k_b, g_mem, w_mem_kv, g_mq, g_mk, w_out, w_gate_up, w_down, loss_target, m_g_mix, m_g_ffn, m_w_in_a, m_g_v_a, m_w_spatial, m_b_spatial, m_w_in_b, m_g_q_b, m_g_k_b, m_g_mem, m_w_mem_kv, m_g_mq, m_g_mk, m_w_out, m_w_gate_up, m_w_down, v_g_mix, v_g_ffn, v_w_in_a, v_g_v_a, v_w_spatial, v_b_spatial, v_w_in_b, v_g_q_b, v_g_k_b, v_g_mem, v_w_mem_kv, v_g_mq, v_g_mk, v_w_out, v_w_gate_up, v_w_down):
    xs = x[0]
    mem2 = mem[0]
    target = loss_target[0]
    T, D = xs.shape
    depth = g_mix.shape[0]
    MEMW = w_mem_kv.shape[2] // 2
    TOK = D - MEMW
    KV = (w_in_b.shape[2] * 4 - TOK - MEMW) // 2
    QPK = TOK // KV
    F = w_gate_up.shape[2] * 4 // 2

    cidx = lax.axis_index("c").astype(jnp.int32).reshape(1)
    iidx = (2 * lax.axis_index("x") + lax.axis_index("y")).astype(jnp.int32).reshape(1)

    big = {
        "w_in_a": (w_in_a, _Shard(D, w_in_a.shape[2] * 4, True)),
        "w_in_b": (w_in_b, _Shard(D, w_in_b.shape[2] * 4, True)),
        "w_mem_kv": (w_mem_kv, _Shard(D, 2 * MEMW, False)),
        "w_out": (w_out, _Shard(D, D, False)),
        "w_gate_up": (w_gate_up, _Shard(D, 2 * F, True)),
        "w_down": (w_down, _Shard(F, D, False)),
    }
    full = {n: _all_gather_weights(_cast_wire(w, "cast_" + n), sh, "all_gather_" + n) for n, (w, sh) in big.items()}
    tabs = _rope_tables(T)

    saved = []
    xc = xs
    for l in range(depth):
        is_a = l % 2 == 0
        li = l // 2
        h = _rmsnorm_fwd(xc, g_mix[l], "rmsnorm_fwd")
        w_in = full["w_in_a" if is_a else "w_in_b"][li]
        z = _mm_nn("mm_in", h, w_in)
        st = dict(x=xc, h=h, z=z)
        if is_a:
            ws_m = w_spatial[li].astype(_MXU)
            st["ws_m"], st["wst_m"], st["b_t"] = ws_m, jnp.swapaxes(ws_m, 1, 2), b_spatial[li].T
            tok = _mixer_a_fwd(z, g_v_a[li], ws_m, st["b_t"], TOK)
            qblk = 2 * TOK // MEMW
        else:
            q, k, v = _qk_rope_fwd(z, g_q_b[li], g_k_b[li], tabs, TOK, KV)
            st["q"], st["k"], st["v"] = q, k, v
            tok = _attn_fwd(q, k, v, QPK)
            qblk = (TOK + 2 * KV) // MEMW
        mem_n = _rmsnorm_fwd(mem2, g_mem[l], "rmsnorm_mem")
        kv = _mm_nn("mm_memkv", mem_n, full["w_mem_kv"][l])
        mo = _mem_fwd(z, qblk, kv, g_mq[l], g_mk[l], MEMW)
        cat = jnp.concatenate([tok, mo], axis=1)
        x1 = _mm_nn("mm_out", cat, full["w_out"][l], add=xc)
        h2 = _rmsnorm_fwd(x1, g_ffn[l], "rmsnorm_fwd")
        act, gu = _ffn_gate_up(h2, full["w_gate_up"][l])
        xc = _mm_nn("mm_down", act, full["w_down"][l], add=x1)
        st.update(mem_n=mem_n, kv=kv, qblk=qblk, cat=cat, x1=x1, h2=h2, act=act, gu=gu)
        saved.append(st)

    dx, dxm, sq = _loss_head(xc, target)
    loss = lax.psum(sq[0, 0] * (0.5 / D), ("x", "y", "c"))

    gsm = {n: [None] * len(a) for n, a in dict(g_mix=g_mix, g_ffn=g_ffn, g_v_a=g_v_a, w_spatial=w_spatial, b_spatial=b_spatial,
                                                g_q_b=g_q_b, g_k_b=g_k_b, g_mem=g_mem, g_mq=g_mq, g_mk=g_mk).items()}
    gbig = {n: [None] * w.shape[0] for n, (w, _) in big.items()}
    for l in reversed(range(depth)):
        st = saved[l]
        is_a = l % 2 == 0
        li = l // 2
        dgu = _ffn_dact(dxm, full["w_down"][l], st["gu"])
        gbig["w_down"][l] = _mm_tn_dual("mm_dw_down", st["act"], dxm)
        dh2 = _ffn_dh(dgu, full["w_gate_up"][l])
        gbig["w_gate_up"][l] = _ffn_dwgu(st["h2"], dgu)
        dx, dxm, dg = _rmsnorm_bwd(st["x1"], g_ffn[l], dh2, dx, "rmsnorm_bwd")
        gsm["g_ffn"][l] = dg[0]
        dcat = _mm_nt("mm_dcat", dxm, full["w_out"][l])
        gbig["w_out"][l] = _mm_tn_dual("mm_dw_out", st["cat"], dxm)
        dzq, dkn, dvm, dgq = _mem_bwd(st["z"], st["qblk"], st["kv"], g_mq[l], g_mk[l], dcat, TOK // MEMW, MEMW)
        dkv, dgk = _memkv_bwd(st["kv"], dkn, dvm, g_mk[l], MEMW)
        gsm["g_mq"][l], gsm["g_mk"][l] = dgq[0], dgk[0]
        gbig["w_mem_kv"][l] = _mm_tn_dual("mm_dw_memkv", st["mem_n"], dkv)
        dmem_n = _mm_nt("mm_dmemn", dkv, full["w_mem_kv"][l])
        gsm["g_mem"][l] = _rmsnorm_bwd(mem2, g_mem[l], dmem_n, None, "rmsnorm_bwd_mem")[2][0]
        if is_a:
            dz_tok, dws, dbs, dgv = _mixer_a_bwd(st["z"], dcat, g_v_a[li], st["ws_m"], st["wst_m"], st["b_t"], TOK)
            gsm["w_spatial"][li], gsm["b_spatial"][li], gsm["g_v_a"][li] = dws, dbs[:, :, 0], dgv[0]
            dz = jnp.concatenate([dz_tok, dzq], axis=1)
        else:
            dq, dk, dv = _attn_bwd(st["q"], st["k"], st["v"], dcat, QPK)
            dz_qk, dgq_b, dgk_b = _qk_rope_bwd(st["z"], dq, dk, g_q_b[li], g_k_b[li], tabs, TOK, KV)
            gsm["g_q_b"][li], gsm["g_k_b"][li] = dgq_b[0], dgk_b[0]
            dz = jnp.concatenate([dz_qk, dv, dzq], axis=1)
        n_in = "w_in_a" if is_a else "w_in_b"
        dh = _mm_nt("mm_dh", dz, full[n_in][li])
        gbig[n_in][li] = _mm_tn_dual("mm_dw_in", st["h"], dz)
        dx, dxm, dg = _rmsnorm_bwd(st["x"], g_mix[l], dh, dx, "rmsnorm_bwd")
        gsm["g_mix"][l] = dg[0]

    grads = {}
    for n, (w, sh) in big.items():
        grads[n] = _reduce_scatter_grads([g[0] for g in gbig[n]], [g[1] for g in gbig[n]], sh, cidx, iidx, n)
    small = ["g_mix", "g_ffn", "g_v_a", "w_spatial", "b_spatial", "g_q_b", "g_k_b", "g_mem", "g_mq", "g_mk"]
    env = dict(g_mix=g_mix, g_ffn=g_ffn, g_v_a=g_v_a, w_spatial=w_spatial, b_spatial=b_spatial, g_q_b=g_q_b, g_k_b=g_k_b,
               g_mem=g_mem, g_mq=g_mq, g_mk=g_mk,
               m_g_mix=m_g_mix, m_g_ffn=m_g_ffn, m_g_v_a=m_g_v_a, m_w_spatial=m_w_spatial, m_b_spatial=m_b_spatial,
               m_g_q_b=m_g_q_b, m_g_k_b=m_g_k_b, m_g_mem=m_g_mem, m_g_mq=m_g_mq, m_g_mk=m_g_mk,
               v_g_mix=v_g_mix, v_g_ffn=v_g_ffn, v_g_v_a=v_g_v_a, v_w_spatial=v_w_spatial, v_b_spatial=v_b_spatial,
               v_g_q_b=v_g_q_b, v_g_k_b=v_g_k_b, v_g_mem=v_g_mem, v_g_mq=v_g_mq, v_g_mk=v_g_mk,
               m_w_in_a=m_w_in_a, m_w_in_b=m_w_in_b, m_w_mem_kv=m_w_mem_kv, m_w_out=m_w_out, m_w_gate_up=m_w_gate_up, m_w_down=m_w_down,
               v_w_in_a=v_w_in_a, v_w_in_b=v_w_in_b, v_w_mem_kv=v_w_mem_kv, v_w_out=v_w_out, v_w_gate_up=v_w_gate_up, v_w_down=v_w_down)
    like = [env[n] for n in small]
    g_small = _all_reduce_small(_pack([jnp.stack(gsm[n]) for n in small]))

    res = {}
    outs = _adamw(_pack(like), g_small, _pack([env["m_" + n] for n in small]), _pack([env["v_" + n] for n in small]), "adamw_small")
    unpacked = [_unpack(o, like) for o in outs]
    for k, n in enumerate(small):
        res[n] = [u[k] for u in unpacked]
    for n, (w, _) in big.items():
        res[n] = _adamw(w, grads[n], env["m_" + n], env["v_" + n], "adamw_" + n)

    order = ["g_mix", "g_ffn", "w_in_a", "g_v_a", "w_spatial", "b_spatial", "w_in_b", "g_q_b", "g_k_b", "g_mem", "w_mem_kv",
             "g_mq", "g_mk", "w_out", "w_gate_up", "w_down"]
    return (loss, dx.reshape(1, T, D), *[res[n][0] for n in order], *[res[n][1] for n in order],
            *[res[n][2] for n in order], *[res[n][3] for n in order])
```

```python
import jax
import jax.numpy as jnp
import numpy as np
from jax import lax
from jax.experimental import pallas as pl
from jax.experimental.pallas import tpu as pltpu

_F32 = jnp.float32
_MXU = jnp.bfloat16
_WIRE = jnp.bfloat16
_KW = {}

EPS = 1e-6
HEAD = 128
CHUNK = 128
GRID_W = 64
ROPE_THETA = 10000.0
ADAM_LR, ADAM_B1, ADAM_B2, ADAM_EPS, ADAM_WD, ADAM_STEP = 0.001, 0.9, 0.999, 1e-08, 0.01, 10
_SQRT_HALF = float(np.sqrt(0.5))
_INV_SQRT_2PI = float(1.0 / np.sqrt(2.0 * np.pi))
_VMEM_LIMIT = 56 * 1024 * 1024
_MESH = pl.DeviceIdType.MESH

_NN = (((1,), (0,)), ((), ()))
_NT = (((1,), (1,)), ((), ()))
_TN = (((0,), (0,)), ((), ()))

S = jax.ShapeDtypeStruct
BS = pl.BlockSpec
_ANY = pl.BlockSpec(memory_space=pl.ANY)


def _tile(n, pref, mult=128):
    if n <= pref:
        return n
    d = (pref // mult) * mult
    while d >= mult:
        if n % d == 0:
            return d
        d -= mult
    raise ValueError(f"no tile for {n} (pref {pref}, mult {mult})")


def _mo(v, m):
    return v if isinstance(v, int) else pl.multiple_of(v, m)


def _cp(*sem, **kw):
    return pltpu.CompilerParams(dimension_semantics=sem or None, vmem_limit_bytes=_VMEM_LIMIT, **kw)


def _call(body, name, **kw):
    return pl.pallas_call(body, name=name, **kw, **_KW)


def _dot(a, b, dn=_NN):
    return lax.dot_general(a, b, dn, preferred_element_type=_F32)


def _gelu(x):
    return 0.5 * x * (1.0 + lax.erf(x * _SQRT_HALF))


def _gelu_grad(x):
    return 0.5 * (1.0 + lax.erf(x * _SQRT_HALF)) + x * jnp.exp(-0.5 * x * x) * _INV_SQRT_2PI


def _rstd(x):
    return lax.rsqrt(jnp.mean(x * x, axis=-1, keepdims=True) + EPS)


def _norm_bwd(dout, xhat, r, g):
    dy = dout * g
    return r * (dy - xhat * jnp.mean(dy * xhat, axis=-1, keepdims=True))


def _softmax(s):
    e = jnp.exp(s - jnp.max(s, axis=-1, keepdims=True))
    return e * (1.0 / jnp.sum(e, axis=-1, keepdims=True))


def _mm(name, a, b, a_spec, b_spec, dn, grid, acc_shape, out_shape, out_specs, epilogue, extra=(), extra_specs=()):
    nk = grid[2]
    n_ex = len(extra)
    multi = isinstance(out_shape, (list, tuple))
    n_out = len(out_shape) if multi else 1

    def body(*refs):
        a_ref, b_ref = refs[0], refs[1]
        ex = refs[2:2 + n_ex]
        outs = refs[2 + n_ex:2 + n_ex + n_out]

        def prod():
            return _dot(a_ref[...].astype(_MXU), b_ref[...].astype(_MXU), dn)

        if nk == 1:
            epilogue(prod(), ex, outs)
        else:
            acc = refs[-1]
            k = pl.program_id(2)

            @pl.when(k == 0)
            def _():
                acc[...] = jnp.zeros_like(acc)

            acc[...] += prod()

            @pl.when(k == nk - 1)
            def _():
                epilogue(acc[...], ex, outs)

    return _call(
        body, name, grid=grid, in_specs=[a_spec, b_spec, *extra_specs], out_specs=out_specs, out_shape=out_shape,
        scratch_shapes=[] if nk == 1 else [pltpu.VMEM(acc_shape, _F32)],
        compiler_params=_cp("parallel", "parallel", "arbitrary"),
    )(a, b, *extra)


def _ep_store(acc, ex, outs):
    for o in outs:
        o[...] = acc.astype(o.dtype)


def _ep_add(acc, ex, outs):
    outs[0][...] = (acc + ex[0][...]).astype(outs[0].dtype)


def _mm_nn(name, a, b, out_dtype=_F32, add=None, pm=1024, pn=1024, pk=2048):
    M, K = a.shape
    N = b.shape[1]
    tm, tn, tk = _tile(M, pm, 8), _tile(N, pn), _tile(K, pk)
    o_spec = BS((tm, tn), lambda i, j, k: (i, j))
    return _mm(name, a, b, BS((tm, tk), lambda i, j, k: (i, k)), BS((tk, tn), lambda i, j, k: (k, j)), _NN,
               (M // tm, N // tn, K // tk), (tm, tn), S((M, N), out_dtype), o_spec,
               _ep_store if add is None else _ep_add,
               extra=() if add is None else (add,), extra_specs=() if add is None else (o_spec,))


def _mm_nt(name, a, b, out_dtype=_F32, pm=1024, pn=1024, pk=4096):
    M, K = a.shape
    N = b.shape[0]
    tm, tn, tk = _tile(M, pm, 8), _tile(N, pn), _tile(K, pk)
    return _mm(name, a, b, BS((tm, tk), lambda i, j, k: (i, k)), BS((tn, tk), lambda i, j, k: (j, k)), _NT,
               (M // tm, N // tn, K // tk), (tm, tn), S((M, N), out_dtype), BS((tm, tn), lambda i, j, k: (i, j)), _ep_store)


def _mm_tn_dual(name, a, b, pm=1024, pn=1024, pk=2048):
    K, M = a.shape
    N = b.shape[1]
    tm, tn, tk = _tile(M, pm), _tile(N, pn), _tile(K, pk, 16)
    o_spec = BS((tm, tn), lambda i, j, k: (i, j))
    return _mm(name, a, b, BS((tk, tm), lambda i, j, k: (k, i)), BS((tk, tn), lambda i, j, k: (k, j)), _TN,
               (M // tm, N // tn, K // tk), (tm, tn), [S((M, N), _F32), S((M, N), _WIRE)], [o_spec, o_spec], _ep_store)


def _ffn_gate_up(h2, wgu):
    T, D = h2.shape
    F = wgu.shape[1] // 2
    tm, tn = _tile(T, 1024, 8), _tile(F, 512)
    nj = F // tn

    def body(a_ref, bg_ref, bu_ref, act_ref, gu_ref):
        a = a_ref[...]
        g = _dot(a, bg_ref[...])
        u = _dot(a, bu_ref[...])
        gu_ref[0] = g
        gu_ref[1] = u
        act_ref[...] = (g * (1.0 / (1.0 + jnp.exp(-g))) * u).astype(act_ref.dtype)

    return _call(
        body, "ffn_gate_up", grid=(T // tm, nj),
        in_specs=[BS((tm, D), lambda i, j: (i, 0)), BS((D, tn), lambda i, j: (0, j)), BS((D, tn), lambda i, j: (0, j + nj))],
        out_specs=[BS((tm, tn), lambda i, j: (i, j)), BS((2, tm, tn), lambda i, j: (0, i, j))],
        out_shape=[S((T, F), _MXU), S((2, T, F), _F32)],
        compiler_params=_cp("parallel", "parallel"),
    )(h2, wgu, wgu)


def _ffn_dact(dxm, wdown, gu):
    T, D = dxm.shape
    F = wdown.shape[0]
    tm, tn = _tile(T, 1024, 8), _tile(F, 512)

    def body(a_ref, b_ref, gu_ref, o_ref):
        d = _dot(a_ref[...], b_ref[...], _NT)
        g, u = gu_ref[0], gu_ref[1]
        sg = 1.0 / (1.0 + jnp.exp(-g))
        o_ref[0] = (d * u * (sg * (1.0 + g * (1.0 - sg)))).astype(o_ref.dtype)
        o_ref[1] = (d * (g * sg)).astype(o_ref.dtype)

    return _call(
        body, "ffn_dact", grid=(T // tm, F // tn),
        in_specs=[BS((tm, D), lambda i, j: (i, 0)), BS((tn, D), lambda i, j: (j, 0)), BS((2, tm, tn), lambda i, j: (0, i, j))],
        out_specs=BS((2, tm, tn), lambda i, j: (0, i, j)), out_shape=S((2, T, F), _MXU),
        compiler_params=_cp("parallel", "parallel"),
    )(dxm, wdown, gu)


def _ffn_dh(dgu, wgu):
    _, T, F = dgu.shape
    D = wgu.shape[0]
    tm, tn, tk = _tile(T, 1024, 8), _tile(D, 2048), _tile(F, 2048)
    nkf = F // tk
    return _mm("ffn_dh", dgu, wgu, BS((None, tm, tk), lambda i, j, k: (k // nkf, i, k % nkf)),
               BS((tn, tk), lambda i, j, k: (j, k)), _NT, (T // tm, D // tn, 2 * nkf), (tm, tn),
               S((T, D), _F32), BS((tm, tn), lambda i, j, k: (i, j)), _ep_store)


def _ffn_dwgu(h2, dgu):
    _, T, F = dgu.shape
    D = h2.shape[1]
    tm, tn, tk = _tile(D, 1024), _tile(F, 1408), _tile(T, 2048, 16)
    njf = F // tn
    o_spec = BS((tm, tn), lambda i, j, k: (i, j))
    return _mm("ffn_dwgu", h2, dgu, BS((tk, tm), lambda i, j, k: (k, i)),
               BS((None, tk, tn), lambda i, j, k: (j // njf, k, j % njf)), _TN, (D // tm, 2 * njf, T // tk), (tm, tn),
               [S((D, 2 * F), _F32), S((D, 2 * F), _WIRE)], [o_spec, o_spec], _ep_store)


def _rmsnorm_fwd(x, g, name):
    T, D = x.shape
    tr = _tile(T, 512, 8)

    def body(x_ref, g_ref, o_ref):
        xv = x_ref[...]
        o_ref[...] = (xv * _rstd(xv) * g_ref[...]).astype(o_ref.dtype)

    row = BS((tr, D), lambda i: (i, 0))
    return _call(body, name, grid=(T // tr,), in_specs=[row, BS((1, D), lambda i: (0, 0))], out_specs=row,
                 out_shape=S((T, D), _MXU), compiler_params=_cp("parallel"))(x, g.reshape(1, D))


def _rmsnorm_bwd(x, g, dh, dres, name):
    T, D = x.shape
    tr = _tile(T, 256, 8)
    has_res = dres is not None

    def body(*refs):
        x_ref, g_ref, dh_ref = refs[:3]
        dx_ref, dxm_ref, dg_ref = refs[-3:]

        @pl.when(pl.program_id(0) == 0)
        def _():
            dg_ref[...] = jnp.zeros_like(dg_ref)

        xv = x_ref[...]
        r = _rstd(xv)
        xhat = xv * r
        dh_v = dh_ref[...]
        dg_ref[...] += jnp.sum(dh_v * xhat, axis=0, keepdims=True)
        dx = _norm_bwd(dh_v, xhat, r, g_ref[...])
        if has_res:
            dx = dx + refs[3][...]
        dx_ref[...] = dx
        dxm_ref[...] = dx.astype(dxm_ref.dtype)

    row = BS((tr, D), lambda i: (i, 0))
    vec = BS((1, D), lambda i: (0, 0))
    return _call(body, name, grid=(T // tr,), in_specs=[row, vec, row] + ([row] if has_res else []),
                 out_specs=[row, row, vec], out_shape=[S((T, D), _F32), S((T, D), _MXU), S((1, D), _F32)],
                 compiler_params=_cp("arbitrary"))(x, g.reshape(1, D), dh, *([dres] if has_res else []))


def _loss_head(y, target):
    T, D = y.shape
    tr = _tile(T, 256, 8)

    def body(y_ref, t_ref, dy_ref, dym_ref, acc_ref):
        @pl.when(pl.program_id(0) == 0)
        def _():
            acc_ref[...] = jnp.zeros_like(acc_ref)

        err = y_ref[...] - t_ref[...]
        acc_ref[...] += jnp.sum(jnp.sum(err * err, axis=-1, keepdims=True), axis=0, keepdims=True)
        dy = err * (1.0 / D)
        dy_ref[...] = dy
        dym_ref[...] = dy.astype(dym_ref.dtype)

    row = BS((tr, D), lambda i: (i, 0))
    return _call(body, "loss_head", grid=(T // tr,), in_specs=[row, row],
                 out_specs=[row, row, BS((1, 128), lambda i: (0, 0))],
                 out_shape=[S((T, D), _F32), S((T, D), _MXU), S((1, 128), _F32)],
                 compiler_params=_cp("arbitrary"))(y, target)


def _mixa_blocks(T):
    return 2 if T % (2 * CHUNK) == 0 else 1


def _mixer_a_fwd(z, gv, ws_m, b_t, TOK):
    T = z.shape[0]
    G = TOK // HEAD
    CB = _mixa_blocks(T)
    R = CB * CHUNK

    def body(z_ref, gv_ref, ws_ref, bt_ref, o_ref):
        u = _gelu(z_ref[:, :TOK])
        v = _gelu(z_ref[:, TOK:])
        vn = (v * _rstd(v) * gv_ref[...]).astype(_MXU)
        for c in range(CB):
            rows = slice(c * CHUNK, (c + 1) * CHUNK)
            for g in range(G):
                cols = slice(g * HEAD, (g + 1) * HEAD)
                s = _dot(ws_ref[g], vn[rows, cols]) + bt_ref[:, g:g + 1]
                o_ref[rows, cols] = (u[rows, cols] * s).astype(o_ref.dtype)

    return _call(
        body, "mixer_a_fwd", grid=(T // R,),
        in_specs=[BS((R, 2 * TOK), lambda i: (i, 0)), BS((1, TOK), lambda i: (0, 0)),
                  BS((G, CHUNK, CHUNK), lambda i: (0, 0, 0)), BS((CHUNK, G), lambda i: (0, 0))],
        out_specs=BS((R, TOK), lambda i: (i, 0)), out_shape=S((T, TOK), _MXU), compiler_params=_cp("parallel"),
    )(z, gv.reshape(1, TOK), ws_m, b_t)


def _mixer_a_bwd(z, dcat, gv, ws_m, wst_m, b_t, TOK):
    T = z.shape[0]
    G = TOK // HEAD
    CB = _mixa_blocks(T)
    R = CB * CHUNK
    n = T // R

    def body(z_ref, d_ref, gv_ref, ws_ref, wst_ref, bt_ref, dz_ref, dws_ref, db_ref, dgv_ref, dvn_scr):
        i = pl.program_id(0)

        @pl.when(i == 0)
        def _():
            dws_ref[...] = jnp.zeros_like(dws_ref)
            db_ref[...] = jnp.zeros_like(db_ref)
            dgv_ref[...] = jnp.zeros_like(dgv_ref)

        zu = z_ref[:, :TOK]
        zv = z_ref[:, TOK:]
        u = _gelu(zu)
        v = _gelu(zv)
        r = _rstd(v)
        vhat = v * r
        gvv = gv_ref[...]
        vn = (vhat * gvv).astype(_MXU)
        d = d_ref[...]
        gpu = _gelu_grad(zu)
        for c in range(CB):
            rows = slice(c * CHUNK, (c + 1) * CHUNK)
            for g in range(G):
                cols = slice(g * HEAD, (g + 1) * HEAD)
                vn_cg = vn[rows, cols]
                s = _dot(ws_ref[g], vn_cg) + bt_ref[:, g:g + 1]
                d_cg = d[rows, cols]
                dz_ref[rows, cols] = (d_cg * s * gpu[rows, cols]).astype(dz_ref.dtype)
                ds = d_cg * u[rows, cols]
                ds_m = ds.astype(_MXU)
                dvn_scr[rows, cols] = _dot(wst_ref[g], ds_m)
                dws_ref[g] += _dot(ds_m, vn_cg, _NT)
                db_ref[g] += ds
        dvn = dvn_scr[...]
        dgv_ref[...] += jnp.sum(dvn * vhat, axis=0, keepdims=True)
        dv = _norm_bwd(dvn, vhat, r, gvv)
        dz_ref[:, TOK:] = (dv * _gelu_grad(zv)).astype(dz_ref.dtype)

        @pl.when(i == n - 1)
        def _():
            for g in range(G):
                db_ref[g] = jnp.broadcast_to(jnp.sum(db_ref[g], axis=1, keepdims=True), (CHUNK, CHUNK))

    full3 = BS((G, CHUNK, CHUNK), lambda i: (0, 0, 0))
    return _call(
        body, "mixer_a_bwd", grid=(n,),
        in_specs=[BS((R, 2 * TOK), lambda i: (i, 0)), BS((R, TOK), lambda i: (i, 0)), BS((1, TOK), lambda i: (0, 0)),
                  full3, full3, BS((CHUNK, G), lambda i: (0, 0))],
        out_specs=[BS((R, 2 * TOK), lambda i: (i, 0)), full3, full3, BS((1, TOK), lambda i: (0, 0))],
        out_shape=[S((T, 2 * TOK), _MXU), S((G, CHUNK, CHUNK), _F32), S((G, CHUNK, CHUNK), _F32), S((1, TOK), _F32)],
        scratch_shapes=[pltpu.VMEM((R, TOK), _F32)], compiler_params=_cp("arbitrary"),
    )(z, dcat, gv.reshape(1, TOK), ws_m, wst_m, b_t)


def _rope_tables(T):
    n_rows = T // GRID_W
    rows = jnp.broadcast_to(jnp.arange(n_rows)[:, None], (n_rows, GRID_W)).reshape(T)
    cols = jnp.broadcast_to(jnp.arange(GRID_W)[None, :], (n_rows, GRID_W)).reshape(T)
    pairs = HEAD // 4
    freqs = ROPE_THETA ** (-jnp.arange(pairs, dtype=_F32) / pairs)
    ang_r = rows.astype(_F32)[:, None] * freqs
    ang_c = cols.astype(_F32)[:, None] * freqs
    ang = jnp.concatenate([ang_r, ang_r, ang_c, ang_c], axis=-1)
    cos, sin = jnp.cos(ang), jnp.sin(ang)
    first = (jnp.arange(HEAD) % (HEAD // 2)) < (HEAD // 4)
    return cos, jnp.where(first, -sin, 0.0), jnp.where(first, 0.0, sin)


def _rope(x, cs, sa, sb):
    return x * cs + pltpu.roll(x, 96, 1) * sa + pltpu.roll(x, 32, 1) * sb


def _qk_rope_fwd(z, gq, gk, tabs, TOK, KV):
    T = z.shape[0]
    R = _tile(T, 512, 8)
    W = TOK + 2 * KV

    def body(z_ref, gq_ref, gk_ref, cos_ref, sa_ref, sb_ref, q_ref, k_ref, v_ref):
        cs, sa, sb = cos_ref[...], sa_ref[...], sb_ref[...]
        for h in range((TOK + KV) // HEAD):
            cols = slice(h * HEAD, (h + 1) * HEAD)
            xv = z_ref[:, cols]
            xn = xv * _rstd(xv) * (gq_ref[...] if h < TOK // HEAD else gk_ref[...])
            out = _rope(xn, cs, sa, sb)
            if h < TOK // HEAD:
                q_ref[:, cols] = out.astype(q_ref.dtype)
            else:
                k_ref[:, h * HEAD - TOK:(h + 1) * HEAD - TOK] = out.astype(k_ref.dtype)
        v_ref[...] = z_ref[:, TOK + KV:].astype(v_ref.dtype)

    vec = BS((1, HEAD), lambda i: (0, 0))
    tab = BS((R, HEAD), lambda i: (i, 0))
    return _call(
        body, "qk_rope_fwd", grid=(T // R,), in_specs=[BS((R, W), lambda i: (i, 0)), vec, vec, tab, tab, tab],
        out_specs=[BS((R, TOK), lambda i: (i, 0)), BS((R, KV), lambda i: (i, 0)), BS((R, KV), lambda i: (i, 0))],
        out_shape=[S((T, TOK), _MXU), S((T, KV), _MXU), S((T, KV), _MXU)], compiler_params=_cp("parallel"),
    )(z, gq.reshape(1, HEAD), gk.reshape(1, HEAD), *tabs)


def _qk_rope_bwd(z, dq, dk, gq, gk, tabs, TOK, KV):
    T = z.shape[0]
    R = _tile(T, 512, 8)
    W = TOK + KV

    def body(z_ref, dq_ref, dk_ref, gq_ref, gk_ref, cos_ref, sa_ref, sb_ref, dz_ref, dgq_ref, dgk_ref):
        @pl.when(pl.program_id(0) == 0)
        def _():
            dgq_ref[...] = jnp.zeros_like(dgq_ref)
            dgk_ref[...] = jnp.zeros_like(dgk_ref)

        cs, sa, sb = cos_ref[...], sa_ref[...], sb_ref[...]
        for h in range(W // HEAD):
            cols = slice(h * HEAD, (h + 1) * HEAD)
            is_q = h < TOK // HEAD
            do = dq_ref[:, cols] if is_q else dk_ref[:, h * HEAD - TOK:(h + 1) * HEAD - TOK]
            dxn = do * cs - pltpu.roll(do, 96, 1) * sa - pltpu.roll(do, 32, 1) * sb
            xv = z_ref[:, cols]
            r = _rstd(xv)
            xhat = xv * r
            dg_ref = dgq_ref if is_q else dgk_ref
            dg_ref[...] += jnp.sum(dxn * xhat, axis=0, keepdims=True)
            dz_ref[:, cols] = _norm_bwd(dxn, xhat, r, gq_ref[...] if is_q else gk_ref[...]).astype(dz_ref.dtype)

    vec = BS((1, HEAD), lambda i: (0, 0))
    tab = BS((R, HEAD), lambda i: (i, 0))
    return _call(
        body, "qk_rope_bwd", grid=(T // R,),
        in_specs=[BS((R, W), lambda i: (i, 0)), BS((R, TOK), lambda i: (i, 0)), BS((R, KV), lambda i: (i, 0)), vec, vec, tab, tab, tab],
        out_specs=[BS((R, W), lambda i: (i, 0)), vec, vec],
        out_shape=[S((T, W), _MXU), S((1, HEAD), _F32), S((1, HEAD), _F32)], compiler_params=_cp("arbitrary"),
    )(z, dq, dk, gq.reshape(1, HEAD), gk.reshape(1, HEAD), *tabs)


_ATTN_C2 = float(HEAD ** -0.5 * np.log2(np.e))


def _attn_fwd(q, k, v, QPK):
    T, TOK = q.shape
    KVH = k.shape[1] // HEAD
    tq = _tile(T, 256, 8)
    W = QPK * HEAD

    def body(q_ref, k_ref, v_ref, o_ref, st_ref, vaug):
        @pl.when(pl.program_id(1) == 0)
        def _():
            vaug[:, :HEAD] = v_ref[...]
            vaug[:, HEAD:] = jnp.ones((T, HEAD), vaug.dtype)

        kk, va = k_ref[...], vaug[...]
        for g in range(QPK):
            cols = slice(g * HEAD, (g + 1) * HEAD)
            s = _dot(q_ref[:, cols], kk, _NT)
            m = jnp.max(s, axis=-1, keepdims=True)
            ov = _dot(jnp.exp2((s - m) * _ATTN_C2).astype(_MXU), va)
            l = ov[:, HEAD:HEAD + 1]
            o_ref[:, cols] = (ov[:, :HEAD] * (1.0 / l)).astype(o_ref.dtype)
            st_ref[:, g:g + 1] = m + jnp.log2(l) * (1.0 / _ATTN_C2)

    qs = BS((tq, W), lambda h, i: (i, h))
    ks = BS((T, HEAD), lambda h, i: (0, h))
    return _call(body, "attn_fwd", grid=(KVH, T // tq), in_specs=[qs, ks, ks],
                 out_specs=[qs, BS((None, tq, QPK), lambda h, i: (h, i, 0))],
                 out_shape=[S((T, TOK), _MXU), S((KVH, T, QPK), _F32)],
                 scratch_shapes=[pltpu.VMEM((T, 2 * HEAD), _MXU)],
                 compiler_params=_cp("parallel", "arbitrary"))(q, k, v)


def _attn_bwd(q, k, v, dcat, o, stat, QPK):
    T, TOK = q.shape
    KV = k.shape[1]
    KVH = KV // HEAD
    tq = _tile(T, 256, 8)
    nq = T // tq
    W = QPK * HEAD
    scale = HEAD ** -0.5

    def body(q_ref, k_ref, v_ref, do_ref, o_ref, st_ref, dq_ref, dk_ref, dv_ref, dk_acc, dv_acc):
        i = pl.program_id(1)

        @pl.when(i == 0)
        def _():
            dk_acc[...] = jnp.zeros_like(dk_acc)
            dv_acc[...] = jnp.zeros_like(dv_acc)

        kk, vv = k_ref[...], v_ref[...]
        for g in range(QPK):
            cols = slice(g * HEAD, (g + 1) * HEAD)
            qg = q_ref[:, cols]
            p = jnp.exp2((_dot(qg, kk, _NT) - st_ref[:, g:g + 1]) * _ATTN_C2)
            do32 = do_ref[:, cols]
            do = do32.astype(_MXU)
            delta = jnp.sum(do32 * o_ref[:, cols].astype(_F32), axis=-1, keepdims=True)
            ds = (p * (_dot(do, vv, _NT) - delta)).astype(_MXU)
            dq_ref[:, cols] = _dot(ds, kk) * scale
            dk_acc[...] += _dot(ds, qg, _TN)
            dv_acc[...] += _dot(p.astype(_MXU), do, _TN)

        @pl.when(i == nq - 1)
        def _():
            dk_ref[...] = dk_acc[...] * scale
            dv_ref[...] = dv_acc[...].astype(dv_ref.dtype)

    qs = BS((tq, W), lambda h, i: (i, h))
    ks = BS((T, HEAD), lambda h, i: (0, h))
    return _call(
        body, "attn_bwd", grid=(KVH, nq), in_specs=[qs, ks, ks, qs, qs, BS((None, tq, QPK), lambda h, i: (h, i, 0))],
        out_specs=[qs, ks, ks], out_shape=[S((T, TOK), _F32), S((T, KV), _F32), S((T, KV), _MXU)],
        scratch_shapes=[pltpu.VMEM((T, HEAD), _F32), pltpu.VMEM((T, HEAD), _F32)],
        compiler_params=_cp("parallel", "arbitrary"),
    )(q, k, v, dcat, o, stat)


def _mem_fwd(z, qblk, kv, gmq, gmk, MEMW):
    T = z.shape[0]
    NM = kv.shape[0]
    tq = _tile(T, 512, 8)
    scale = HEAD ** -0.5

    def body(q_ref, kv_ref, gq_ref, gk_ref, o_ref):
        for h in range(MEMW // HEAD):
            cols = slice(h * HEAD, (h + 1) * HEAD)
            kx = kv_ref[:, cols]
            kn = (kx * _rstd(kx) * gk_ref[...]).astype(_MXU)
            vv = kv_ref[:, MEMW + h * HEAD:MEMW + (h + 1) * HEAD].astype(_MXU)
            qx = q_ref[:, cols]
            qn = (qx * _rstd(qx) * gq_ref[...]).astype(_MXU)
            p = _softmax(_dot(qn, kn, _NT) * scale)
            o_ref[:, cols] = _dot(p.astype(_MXU), vv).astype(o_ref.dtype)

    vec = BS((1, HEAD), lambda i: (0, 0))
    return _call(
        body, "mem_fwd", grid=(T // tq,),
        in_specs=[BS((tq, MEMW), lambda i: (i, qblk)), BS((NM, 2 * MEMW), lambda i: (0, 0)), vec, vec],
        out_specs=BS((tq, MEMW), lambda i: (i, 0)), out_shape=S((T, MEMW), _MXU), compiler_params=_cp("parallel"),
    )(z, kv, gmq.reshape(1, HEAD), gmk.reshape(1, HEAD))


def _mem_bwd(z, qblk, kv, gmq, gmk, dcat, dblk, MEMW):
    T = z.shape[0]
    NM = kv.shape[0]
    tq = _tile(T, 512, 8)
    scale = HEAD ** -0.5

    def body(q_ref, kv_ref, gq_ref, gk_ref, do_ref, dz_ref, dkn_ref, dv_ref, dgq_ref):
        @pl.when(pl.program_id(0) == 0)
        def _():
            dkn_ref[...] = jnp.zeros_like(dkn_ref)
            dv_ref[...] = jnp.zeros_like(dv_ref)
            dgq_ref[...] = jnp.zeros_like(dgq_ref)

        for h in range(MEMW // HEAD):
            cols = slice(h * HEAD, (h + 1) * HEAD)
            kx = kv_ref[:, cols]
            kn = (kx * _rstd(kx) * gk_ref[...]).astype(_MXU)
            vv = kv_ref[:, MEMW + h * HEAD:MEMW + (h + 1) * HEAD].astype(_MXU)
            qx = q_ref[:, cols]
            rq = _rstd(qx)
            qhat = qx * rq
            qn = (qhat * gq_ref[...]).astype(_MXU)
            p = _softmax(_dot(qn, kn, _NT) * scale)
            do = do_ref[:, cols].astype(_MXU)
            dp = _dot(do, vv, _NT)
            ds = (p * (dp - jnp.sum(p * dp, axis=-1, keepdims=True)) * scale).astype(_MXU)
            dqn = _dot(ds, kn)
            dkn_ref[:, cols] += _dot(ds, qn, _TN)
            dv_ref[:, cols] += _dot(p.astype(_MXU), do, _TN)
            dgq_ref[...] += jnp.sum(dqn * qhat, axis=0, keepdims=True)
            dz_ref[:, cols] = _norm_bwd(dqn, qhat, rq, gq_ref[...]).astype(dz_ref.dtype)

    vec = BS((1, HEAD), lambda i: (0, 0))
    kvs = BS((NM, MEMW), lambda i: (0, 0))
    return _call(
        body, "mem_bwd", grid=(T // tq,),
        in_specs=[BS((tq, MEMW), lambda i: (i, qblk)), BS((NM, 2 * MEMW), lambda i: (0, 0)), vec, vec,
                  BS((tq, MEMW), lambda i: (i, dblk))],
        out_specs=[BS((tq, MEMW), lambda i: (i, 0)), kvs, kvs, vec],
        out_shape=[S((T, MEMW), _MXU), S((NM, MEMW), _F32), S((NM, MEMW), _F32), S((1, HEAD), _F32)],
        compiler_params=_cp("arbitrary"),
    )(z, kv, gmq.reshape(1, HEAD), gmk.reshape(1, HEAD), dcat)


def _memkv_bwd(kv, dkn, dv, gmk, MEMW):
    NM = kv.shape[0]

    def body(kv_ref, dkn_ref, dv_ref, gk_ref, dkv_ref, dgk_ref):
        dgk = jnp.zeros((1, HEAD), _F32)
        for h in range(MEMW // HEAD):
            cols = slice(h * HEAD, (h + 1) * HEAD)
            kx = kv_ref[:, cols]
            r = _rstd(kx)
            khat = kx * r
            dkn = dkn_ref[:, cols]
            dgk = dgk + jnp.sum(dkn * khat, axis=0, keepdims=True)
            dkv_ref[:, cols] = _norm_bwd(dkn, khat, r, gk_ref[...]).astype(dkv_ref.dtype)
        dgk_ref[...] = dgk
        dkv_ref[:, MEMW:] = dv_ref[...].astype(dkv_ref.dtype)

    return _call(body, "memkv_bwd", out_shape=[S((NM, 2 * MEMW), _MXU), S((1, HEAD), _F32)],
                 compiler_params=_cp())(kv, dkn, dv, gmk.reshape(1, HEAD))


def _cast_into_full(w, l, sh, idx):
    tr, tc = _tile(sh.Rs, 512, 16), _tile(sh.Cs, 2048)
    nr, nc = sh.Rs // tr, sh.Cs // tc

    def body(i_ref, w_ref, o_ref):
        o_ref[...] = w_ref[...].astype(o_ref.dtype)

    if sh.by_cols:
        o_map = lambda a, b, i: (a, i[0] * nc + b)
    else:
        o_map = lambda a, b, i: (i[0] * nr + a, b)
    return _call(
        body, "cast_into_full",
        grid_spec=pltpu.PrefetchScalarGridSpec(num_scalar_prefetch=1, grid=(nr, nc),
                                               in_specs=[BS((None, tr, tc), lambda a, b, i: (l, a, b))], out_specs=BS((tr, tc), o_map)),
        out_shape=S((sh.R, sh.C), _WIRE), compiler_params=_cp("parallel", "parallel"),
    )(idx, w)


def _adamw(w, g, m, v, name):
    shape = w.shape
    C = shape[-1]
    args = [a.reshape(-1, C) for a in (w, g, m, v)]
    R = args[0].shape[0]
    tr, tc = _tile(R, 256, 8), _tile(C, 1024)
    c_m = 1.0 / (1.0 - ADAM_B1 ** ADAM_STEP)
    c_v = 1.0 / (1.0 - ADAM_B2 ** ADAM_STEP)

    def body(w_ref, g_ref, m_ref, v_ref, go_ref, d_ref, mo_ref, vo_ref):
        gv = g_ref[...]
        mn = ADAM_B1 * m_ref[...] + (1.0 - ADAM_B1) * gv
        vn = ADAM_B2 * v_ref[...] + (1.0 - ADAM_B2) * (gv * gv)
        go_ref[...] = gv
        mo_ref[...] = mn
        vo_ref[...] = vn
        d_ref[...] = -ADAM_LR * ((mn * c_m) / (jnp.sqrt(vn * c_v) + ADAM_EPS) + ADAM_WD * w_ref[...])

    blk = BS((tr, tc), lambda i, j: (i, j))
    outs = _call(body, name, grid=(R // tr, C // tc), in_specs=[blk] * 4, out_specs=[blk] * 4,
                 out_shape=[S((R, C), _F32)] * 4, compiler_params=_cp("parallel", "parallel"))(*args)
    return [o.reshape(shape) for o in outs]


def _where_am_i():
    x, y, c = lax.axis_index("x"), lax.axis_index("y"), lax.axis_index("c")
    chips = [(1 - x, y), (x, 1 - y), (1 - x, 1 - y)]
    return x, y, c, 2 * x + y, chips, [2 * cx + cy for cx, cy in chips]


class _Shard:
    def __init__(self, R, C, by_cols):
        self.R, self.C, self.by_cols = R, C, by_cols
        self.Rs, self.Cs = (R, C // 4) if by_cols else (R // 4, C)
        self.Rh = self.Rs // 2
        self.Q = R // 8

    def full_piece(self, ref, j, cc):
        if self.by_cols:
            return ref.at[pl.ds(cc * self.Rh, self.Rh), pl.ds(_mo(j * self.Cs, 128), self.Cs)]
        return ref.at[pl.ds(_mo(j * self.Rs + cc * self.Rh, 16), self.Rh), :]

    def full_shard(self, ref, j):
        if self.by_cols:
            return ref.at[:, pl.ds(_mo(j * self.Cs, 128), self.Cs)]
        return ref.at[pl.ds(_mo(j * self.Rs, 16), self.Rs), :]

    def shard_half(self, ref, cc):
        return ref.at[pl.ds(_mo(cc * self.Rh, 16), self.Rh), :]

    def half_piece(self, ref, j):
        if self.by_cols:
            return ref.at[:, pl.ds(_mo(j * self.Cs, 128), self.Cs)]
        return ref.at[pl.ds(_mo(j * self.Rh, 16), self.Rh), :]


def _remote(src, dst, ssem, rsem, dev):
    return pltpu.make_async_remote_copy(src_ref=src, dst_ref=dst, send_sem=ssem, recv_sem=rsem, device_id=dev, device_id_type=_MESH)


def _all_gather_layer(fulls, shs, name):
    n = len(fulls)

    def body(*refs):
        bufs = refs[n:2 * n]
        send_sems, recv_sems = refs[2 * n:]
        x, y, c, me, chips, chip_ids = _where_am_i()
        sib = (x, y, 1 - c)
        first, passed = [], []
        for t in range(n):
            mine = shs[t].full_piece(bufs[t], me, c)
            for r in range(3):
                first.append(_remote(mine, mine, send_sems.at[t, r], recv_sems.at[t, r], (*chips[r], c)))
        for cp in first:
            cp.start()
        for t in range(n):
            for r in range(3):
                piece = shs[t].full_piece(bufs[t], chip_ids[r], c)
                _remote(piece, piece, send_sems.at[t, r], recv_sems.at[t, r], sib).wait_recv()
                cp = _remote(piece, piece, send_sems.at[t, 3 + r], recv_sems.at[t, 3 + r], sib)
                cp.start()
                passed.append(cp)
        for t in range(n):
            for r in range(3):
                piece = shs[t].full_piece(bufs[t], chip_ids[r], 1 - c)
                _remote(piece, piece, send_sems.at[t, 3 + r], recv_sems.at[t, 3 + r], sib).wait_recv()
        for cp in first + passed:
            cp.wait_send()

    return _call(
        body, name, in_specs=[_ANY] * n, out_specs=[_ANY] * n, out_shape=[S(f.shape, f.dtype) for f in fulls],
        input_output_aliases={t: t for t in range(n)},
        scratch_shapes=[pltpu.SemaphoreType.DMA((n, 6)), pltpu.SemaphoreType.DMA((n, 6))],
        compiler_params=pltpu.CompilerParams(has_side_effects=True),
    )(*fulls)


def _rs_pair_exchange(dws, shs, name):
    n = len(dws)

    def body(*refs):
        ins, outs = refs[:n], refs[n:2 * n]
        send_sems, recv_sems = refs[2 * n:]
        x, y, c, *_ = _where_am_i()
        sib = (x, y, 1 - c)
        cps = []
        for t in range(n):
            sh = shs[t]
            if sh.by_cols:
                cps.append(_remote(ins[t].at[pl.ds((1 - c) * sh.Rh, sh.Rh), :], outs[t], send_sems.at[t, 0], recv_sems.at[t, 0], sib))
            else:
                for j in range(4):
                    cps.append(_remote(sh.full_piece(ins[t], j, 1 - c), sh.half_piece(outs[t], j),
                                       send_sems.at[t, j], recv_sems.at[t, j], sib))
        for cp in cps:
            cp.start()
        for cp in cps:
            cp.wait()

    return _call(
        body, name, in_specs=[_ANY] * n, out_specs=[_ANY] * n, out_shape=[S((sh.R // 2, sh.C), _WIRE) for sh in shs],
        scratch_shapes=[pltpu.SemaphoreType.DMA((n, 4)), pltpu.SemaphoreType.DMA((n, 4))],
        compiler_params=pltpu.CompilerParams(has_side_effects=True),
    )(*dws)


def _rs_pair_add(dw32, recv, sh, idx):
    tr, tc = _tile(sh.Q, 512, 16), _tile(sh.C, 2048)
    nb = sh.Q // tr

    def body(i_ref, a_ref, b_ref, o32_ref, ow_ref):
        p = a_ref[...] + b_ref[...].astype(_F32)
        o32_ref[...] = p
        ow_ref[...] = p.astype(ow_ref.dtype)

    if sh.by_cols:
        a_map = lambda j, i, b, s: (s[1] * 4 * nb + j * nb + i, b)
    else:
        a_map = lambda j, i, b, s: (j * 2 * nb + s[1] * nb + i, b)
    h_spec = BS((tr, tc), lambda j, i, b, s: (j * nb + i, b))
    return _call(
        body, "rs_pair_add",
        grid_spec=pltpu.PrefetchScalarGridSpec(num_scalar_prefetch=1, grid=(4, nb, sh.C // tc),
                                               in_specs=[BS((tr, tc), a_map), h_spec], out_specs=[h_spec, h_spec]),
        out_shape=[S((sh.R // 2, sh.C), _F32), S((sh.R // 2, sh.C), _WIRE)],
        compiler_params=_cp("parallel", "parallel", "parallel"),
    )(idx, dw32, recv)


def _rs_chip_exchange(pws, shs, name):
    n = len(pws)

    def body(*refs):
        ins, outs = refs[:n], refs[n:2 * n]
        send_sems, recv_sems = refs[2 * n:]
        x, y, c, me, chips, chip_ids = _where_am_i()
        cps = []
        for t in range(n):
            for r in range(3):
                cps.append(_remote(shs[t].half_piece(ins[t], chip_ids[r]), outs[t].at[r], send_sems.at[t, r], recv_sems.at[t, r],
                                   (*chips[r], c)))
        for cp in cps:
            cp.start()
        for cp in cps:
            cp.wait()

    return _call(
        body, name, in_specs=[_ANY] * n, out_specs=[_ANY] * n, out_shape=[S((3, sh.Rh, sh.Cs), _WIRE) for sh in shs],
        scratch_shapes=[pltpu.SemaphoreType.DMA((n, 3)), pltpu.SemaphoreType.DMA((n, 3))],
        compiler_params=pltpu.CompilerParams(has_side_effects=True),
    )(*pws)


def _rs_chip_add(p32, recv, sh, idx, g_prev, l, L):
    tr, tc = _tile(sh.Rh, 512, 16), _tile(sh.Cs, 2048)
    nr, nc = sh.Rh // tr, sh.Cs // tc

    def body(i_ref, a_ref, b_ref, *rest):
        rest[-1][...] = a_ref[...] + b_ref[0].astype(_F32) + b_ref[1].astype(_F32) + b_ref[2].astype(_F32)

    if sh.by_cols:
        a_map = lambda a, b, s: (a, s[0] * nc + b)
    else:
        a_map = lambda a, b, s: (s[0] * nr + a, b)
    in_specs = [BS((tr, tc), a_map), BS((3, tr, tc), lambda a, b, s: (0, a, b))]
    args = [idx, p32, recv]
    if g_prev is not None:
        in_specs.append(_ANY)
        args.append(g_prev)
    return _call(
        body, "rs_chip_add",
        grid_spec=pltpu.PrefetchScalarGridSpec(num_scalar_prefetch=1, grid=(nr, nc), in_specs=in_specs,
                                               out_specs=BS((None, tr, tc), lambda a, b, s: (l, s[1] * nr + a, b))),
        out_shape=S((L, sh.Rs, sh.Cs), _F32), input_output_aliases={} if g_prev is None else {3: 0},
        compiler_params=_cp("parallel", "parallel"),
    )(*args)


def _rs_pair_share(gs, ls, shs, name):
    n = len(gs)

    def body(*refs):
        bufs = refs[n:2 * n]
        send_sems, recv_sems = refs[2 * n:]
        x, y, c, *_ = _where_am_i()
        sib = (x, y, 1 - c)
        cps = []
        for t in range(n):
            mine = shs[t].shard_half(bufs[t].at[ls[t]], c)
            cps.append(_remote(mine, mine, send_sems.at[t], recv_sems.at[t], sib))
        for cp in cps:
            cp.start()
        for t in range(n):
            other = shs[t].shard_half(bufs[t].at[ls[t]], 1 - c)
            _remote(other, other, send_sems.at[t], recv_sems.at[t], sib).wait_recv()
        for cp in cps:
            cp.wait_send()

    return _call(
        body, name, in_specs=[_ANY] * n, out_specs=[_ANY] * n, out_shape=[S(g.shape, g.dtype) for g in gs],
        input_output_aliases={t: t for t in range(n)},
        scratch_shapes=[pltpu.SemaphoreType.DMA((n,)), pltpu.SemaphoreType.DMA((n,))],
        compiler_params=pltpu.CompilerParams(has_side_effects=True),
    )(*gs)


def _all_reduce_small(xs):
    M = xs.shape[0]

    def body(x_ref, tot_ref, out_ref, send_sems, recv_sems, local_sem):
        x, y, c, me, chips, chip_ids = _where_am_i()
        sib = (x, y, 1 - c)

        def rows(dev):
            return out_ref.at[pl.ds(_mo((4 * dev[0] + 2 * dev[1] + dev[2]) * M, 8), M), :]

        def copy(k, block, to, src=None):
            return _remote(rows(block) if src is None else src, rows(block), send_sems.at[k], recv_sems.at[k], to)

        mine = pltpu.make_async_copy(x_ref, rows((x, y, c)), local_sem)
        mine.start()
        first = [copy(0, (x, y, c), sib, src=x_ref)]
        first += [copy(1 + j, (x, y, c), (*chip, c), src=x_ref) for j, chip in enumerate(chips)]
        for cp in first:
            cp.start()
        passed = [copy(4 + j, (*chip, c), sib) for j, chip in enumerate(chips)]
        for j, chip in enumerate(chips):
            copy(1 + j, (*chip, c), (x, y, c)).wait_recv()
            passed[j].start()
        copy(0, sib, (x, y, c)).wait_recv()
        for j, chip in enumerate(chips):
            copy(4 + j, (*chip, 1 - c), (x, y, c)).wait_recv()
        for cp in first + passed:
            cp.wait_send()
        mine.wait()
        tot = out_ref[pl.ds(0, M), :]
        for d in range(1, 8):
            tot = tot + out_ref[pl.ds(d * M, M), :]
        tot_ref[...] = tot

    vm = pl.BlockSpec(memory_space=pltpu.VMEM)
    return _call(
        body, "all_reduce_small", in_specs=[vm], out_specs=[vm, vm], out_shape=[S((M, 128), _F32), S((8 * M, 128), _F32)],
        scratch_shapes=[pltpu.SemaphoreType.DMA((7,)), pltpu.SemaphoreType.DMA((7,)), pltpu.SemaphoreType.DMA],
        compiler_params=_cp(has_side_effects=True),
    )(xs)[0]


def _reduce_scatter_layer(dws, tensors, shs, gstack, idx):
    recv_a = _rs_pair_exchange([d[1] for d in dws], shs, "rs_pair_exchange")
    p = [_rs_pair_add(d[0], ra, sh, idx) for d, ra, sh in zip(dws, recv_a, shs)]
    recv_b = _rs_chip_exchange([pw for _, pw in p], shs, "rs_chip_exchange")
    gs = [_rs_chip_add(p32, rb, sh, idx, gstack[name], l, L) for (p32, _), rb, sh, (name, l, L) in zip(p, recv_b, shs, tensors)]
    gs = _rs_pair_share(gs, [l for _, l, _ in tensors], shs, "rs_pair_share")
    for (name, _, _), g in zip(tensors, gs):
        gstack[name] = g


def _pack(parts):
    out = []
    for p in parts:
        p2 = p.reshape(-1, 128)
        pad = (-p2.shape[0]) % 8
        out.append(jnp.pad(p2, ((0, pad), (0, 0))) if pad else p2)
    return jnp.concatenate(out, axis=0)


def _unpack(packed, like):
    out, at = [], 0
    for p in like:
        n = p.size // 128
        out.append(packed[at:at + n].reshape(p.shape))
        at += n + ((-n) % 8)
    return out


def kernel(x, mem, g_mix, g_ffn, w_in_a, g_v_a, w_spatial, b_spatial, w_in_b, g_q_b, g_k_b, g_mem, w_mem_kv, g_mq, g_mk, w_out, w_gate_up, w_down, loss_target, m_g_mix, m_g_ffn, m_w_in_a, m_g_v_a, m_w_spatial, m_b_spatial, m_w_in_b, m_g_q_b, m_g_k_b, m_g_mem, m_w_mem_kv, m_g_mq, m_g_mk, m_w_out, m_w_gate_up, m_w_down, v_g_mix, v_g_ffn, v_w_in_a, v_g_v_a, v_w_spatial, v_b_spatial, v_w_in_b, v_g_q_b, v_g_k_b, v_g_mem, v_w_mem_kv, v_g_mq, v_g_mk, v_w_out, v_w_gate_up, v_w_down):
    xs = x[0]
    mem2 = mem[0]
    target = loss_target[0]
    T, D = xs.shape
    depth = g_mix.shape[0]
    MEMW = w_mem_kv.shape[2] // 2
    TOK = D - MEMW
    KV = (w_in_b.shape[2] * 4 - TOK - MEMW) // 2
    QPK = TOK // KV
    F = w_gate_up.shape[2] * 4 // 2

    idx = jnp.stack([2 * lax.axis_index("x") + lax.axis_index("y"), lax.axis_index("c")]).astype(jnp.int32)

    big = {
        "w_in_a": (w_in_a, _Shard(D, w_in_a.shape[2] * 4, True)),
        "w_in_b": (w_in_b, _Shard(D, w_in_b.shape[2] * 4, True)),
        "w_mem_kv": (w_mem_kv, _Shard(D, 2 * MEMW, False)),
        "w_out": (w_out, _Shard(D, D, False)),
        "w_gate_up": (w_gate_up, _Shard(D, 2 * F, True)),
        "w_down": (w_down, _Shard(F, D, False)),
    }

    def layer_tensors(l):
        n_in = "w_in_a" if l % 2 == 0 else "w_in_b"
        return [(n_in, l // 2, big[n_in][0].shape[0])] + [(n, l, depth) for n in ("w_mem_kv", "w_out", "w_gate_up", "w_down")]

    full = {n: [None] * w.shape[0] for n, (w, _) in big.items()}
    for l in range(depth):
        tens = layer_tensors(l)
        bufs = [_cast_into_full(big[n][0], i, big[n][1], idx) for n, i, _ in tens]
        bufs = _all_gather_layer(bufs, [big[n][1] for n, _, _ in tens], "all_gather_layer")
        for (n, i, _), b in zip(tens, bufs):
            full[n][i] = b
    tabs = _rope_tables(T)

    saved = []
    xc = xs
    for l in range(depth):
        is_a = l % 2 == 0
        li = l // 2
        h = _rmsnorm_fwd(xc, g_mix[l], "rmsnorm_fwd")
        w_in = full["w_in_a" if is_a else "w_in_b"][li]
        z = _mm_nn("mm_in", h, w_in, pm=2048, pn=512)
        st = dict(x=xc, h=h, z=z)
        if is_a:
            ws_m = w_spatial[li].astype(_MXU)
            st["ws_m"], st["wst_m"], st["b_t"] = ws_m, jnp.swapaxes(ws_m, 1, 2), b_spatial[li].T
            tok = _mixer_a_fwd(z, g_v_a[li], ws_m, st["b_t"], TOK)
            qblk = 2 * TOK // MEMW
        else:
            q, k, v = _qk_rope_fwd(z, g_q_b[li], g_k_b[li], tabs, TOK, KV)
            tok, stat = _attn_fwd(q, k, v, QPK)
            st["q"], st["k"], st["v"], st["stat"] = q, k, v, stat
            qblk = (TOK + 2 * KV) // MEMW
        mem_n = _rmsnorm_fwd(mem2, g_mem[l], "rmsnorm_mem")
        kv = _mm_nn("mm_memkv", mem_n, full["w_mem_kv"][l])
        mo = _mem_fwd(z, qblk, kv, g_mq[l], g_mk[l], MEMW)
        cat = jnp.concatenate([tok, mo], axis=1)
        x1 = _mm_nn("mm_out", cat, full["w_out"][l], add=xc)
        h2 = _rmsnorm_fwd(x1, g_ffn[l], "rmsnorm_fwd")
        act, gu = _ffn_gate_up(h2, full["w_gate_up"][l])
        xc = _mm_nn("mm_down", act, full["w_down"][l], add=x1)
        st.update(mem_n=mem_n, kv=kv, qblk=qblk, cat=cat, x1=x1, h2=h2, act=act, gu=gu)
        saved.append(st)

    dx, dxm, sq = _loss_head(xc, target)
    loss = lax.psum(sq[0, 0] * (0.5 / D), ("x", "y", "c"))

    gsm = {n: [None] * len(a) for n, a in dict(g_mix=g_mix, g_ffn=g_ffn, g_v_a=g_v_a, w_spatial=w_spatial, b_spatial=b_spatial,
                                                g_q_b=g_q_b, g_k_b=g_k_b, g_mem=g_mem, g_mq=g_mq, g_mk=g_mk).items()}
    gstack = {n: None for n in big}
    for l in reversed(range(depth)):
        st = saved[l]
        is_a = l % 2 == 0
        li = l // 2
        gbig = {}
        dgu = _ffn_dact(dxm, full["w_down"][l], st["gu"])
        gbig["w_down"] = _mm_tn_dual("mm_dw_down", st["act"], dxm, pm=1408)
        dh2 = _ffn_dh(dgu, full["w_gate_up"][l])
        gbig["w_gate_up"] = _ffn_dwgu(st["h2"], dgu)
        dx, dxm, dg = _rmsnorm_bwd(st["x1"], g_ffn[l], dh2, dx, "rmsnorm_bwd")
        gsm["g_ffn"][l] = dg[0]
        dcat = _mm_nt("mm_dcat", dxm, full["w_out"][l])
        gbig["w_out"] = _mm_tn_dual("mm_dw_out", st["cat"], dxm)
        dzq, dkn, dvm, dgq = _mem_bwd(st["z"], st["qblk"], st["kv"], g_mq[l], g_mk[l], dcat, TOK // MEMW, MEMW)
        dkv, dgk = _memkv_bwd(st["kv"], dkn, dvm, g_mk[l], MEMW)
        gsm["g_mq"][l], gsm["g_mk"][l] = dgq[0], dgk[0]
        gbig["w_mem_kv"] = _mm_tn_dual("mm_dw_memkv", st["mem_n"], dkv)
        dmem_n = _mm_nt("mm_dmemn", dkv, full["w_mem_kv"][l])
        gsm["g_mem"][l] = _rmsnorm_bwd(mem2, g_mem[l], dmem_n, None, "rmsnorm_bwd_mem")[2][0]
        if is_a:
            dz_tok, dws, dbs, dgv = _mixer_a_bwd(st["z"], dcat, g_v_a[li], st["ws_m"], st["wst_m"], st["b_t"], TOK)
            gsm["w_spatial"][li], gsm["b_spatial"][li], gsm["g_v_a"][li] = dws, dbs[:, :, 0], dgv[0]
            dz = jnp.concatenate([dz_tok, dzq], axis=1)
        else:
            dq, dk, dv = _attn_bwd(st["q"], st["k"], st["v"], dcat, st["cat"], st["stat"], QPK)
            dz_qk, dgq_b, dgk_b = _qk_rope_bwd(st["z"], dq, dk, g_q_b[li], g_k_b[li], tabs, TOK, KV)
            gsm["g_q_b"][li], gsm["g_k_b"][li] = dgq_b[0], dgk_b[0]
            dz = jnp.concatenate([dz_qk, dv, dzq], axis=1)
        n_in = "w_in_a" if is_a else "w_in_b"
        dh = _mm_nt("mm_dh", dz, full[n_in][li])
        gbig[n_in] = _mm_tn_dual("mm_dw_in", st["h"], dz, pm=2048, pn=512)
        dx, dxm, dg = _rmsnorm_bwd(st["x"], g_mix[l], dh, dx, "rmsnorm_bwd")
        gsm["g_mix"][l] = dg[0]
        tens = layer_tensors(l)
        _reduce_scatter_layer([gbig[n] for n, _, _ in tens], tens, [big[n][1] for n, _, _ in tens], gstack, idx)

    grads = gstack
    small =["g_mix", "g_ffn", "g_v_a", "w_spatial", "b_spatial", "g_q_b", "g_k_b", "g_mem", "g_mq", "g_mk"]
    env = dict(g_mix=g_mix, g_ffn=g_ffn, g_v_a=g_v_a, w_spatial=w_spatial, b_spatial=b_spatial, g_q_b=g_q_b, g_k_b=g_k_b,
               g_mem=g_mem, g_mq=g_mq, g_mk=g_mk,
               m_g_mix=m_g_mix, m_g_ffn=m_g_ffn, m_g_v_a=m_g_v_a, m_w_spatial=m_w_spatial, m_b_spatial=m_b_spatial,
               m_g_q_b=m_g_q_b, m_g_k_b=m_g_k_b, m_g_mem=m_g_mem, m_g_mq=m_g_mq, m_g_mk=m_g_mk,
               v_g_mix=v_g_mix, v_g_ffn=v_g_ffn, v_g_v_a=v_g_v_a, v_w_spatial=v_w_spatial, v_b_spatial=v_b_spatial,
               v_g_q_b=v_g_q_b, v_g_k_b=v_g_k_b, v_g_mem=v_g_mem, v_g_mq=v_g_mq, v_g_mk=v_g_mk,
               m_w_in_a=m_w_in_a, m_w_in_b=m_w_in_b, m_w_mem_kv=m_w_mem_kv, m_w_out=m_w_out, m_w_gate_up=m_w_gate_up, m_w_down=m_w_down,
               v_w_in_a=v_w_in_a, v_w_in_b=v_w_in_b, v_w_mem_kv=v_w_mem_kv, v_w_out=v_w_out, v_w_gate_up=v_w_gate_up, v_w_down=v_w_down)
    like = [env[n] for n in small]
    g_small = _all_reduce_small(_pack([jnp.stack(gsm[n]) for n in small]))

    res = {}
    outs = _adamw(_pack(like), g_small, _pack([env["m_" + n] for n in small]), _pack([env["v_" + n] for n in small]), "adamw_small")
    unpacked = [_unpack(o, like) for o in outs]
    for k, n in enumerate(small):
        res[n] = [u[k] for u in unpacked]
    for n, (w, _) in big.items():
        res[n] = _adamw(w, grads[n], env["m_" + n], env["v_" + n], "adamw_" + n)

    order = ["g_mix", "g_ffn", "w_in_a", "g_v_a", "w_spatial", "b_spatial", "w_in_b", "g_q_b", "g_k_b", "g_mem", "w_mem_kv",
             "g_mq", "g_mk", "w_out", "w_gate_up", "w_down"]
    return (loss, dx.reshape(1, T, D), *[res[n][0] for n in order], *[res[n][1] for n in order],
            *[res[n][2] for n in order], *[res[n][3] for n in order])
```

```python
import jax
import jax.numpy as jnp
import numpy as np
from jax import lax
from jax.experimental import pallas as pl
from jax.experimental.pallas import tpu as pltpu

_F32 = jnp.float32
_MXU = jnp.bfloat16
_WIRE = jnp.bfloat16
_KW = {}

EPS = 1e-6
HEAD = 128
CHUNK = 128
GRID_W = 64
ROPE_THETA = 10000.0
ADAM_LR, ADAM_B1, ADAM_B2, ADAM_EPS, ADAM_WD, ADAM_STEP = 0.001, 0.9, 0.999, 1e-08, 0.01, 10
_SQRT_HALF = float(np.sqrt(0.5))
_INV_SQRT_2PI = float(1.0 / np.sqrt(2.0 * np.pi))
_VMEM_LIMIT = 56 * 1024 * 1024
_MESH = pl.DeviceIdType.MESH

_NN = (((1,), (0,)), ((), ()))
_NT = (((1,), (1,)), ((), ()))
_TN = (((0,), (0,)), ((), ()))

S = jax.ShapeDtypeStruct
BS = pl.BlockSpec
_ANY = pl.BlockSpec(memory_space=pl.ANY)


def _tile(n, pref, mult=128):
    if n <= pref:
        return n
    d = (pref // mult) * mult
    while d >= mult:
        if n % d == 0:
            return d
        d -= mult
    raise ValueError(f"no tile for {n} (pref {pref}, mult {mult})")


def _mo(v, m):
    return v if isinstance(v, int) else pl.multiple_of(v, m)


def _cp(*sem, **kw):
    return pltpu.CompilerParams(dimension_semantics=sem or None, vmem_limit_bytes=_VMEM_LIMIT, **kw)


def _call(body, name, **kw):
    return pl.pallas_call(body, name=name, **kw, **_KW)


def _dot(a, b, dn=_NN):
    return lax.dot_general(a, b, dn, preferred_element_type=_F32)


def _gelu(x):
    return 0.5 * x * (1.0 + lax.erf(x * _SQRT_HALF))


def _gelu_grad(x):
    return 0.5 * (1.0 + lax.erf(x * _SQRT_HALF)) + x * jnp.exp(-0.5 * x * x) * _INV_SQRT_2PI


def _rstd(x):
    return lax.rsqrt(jnp.mean(x * x, axis=-1, keepdims=True) + EPS)


def _norm_bwd(dout, xhat, r, g):
    dy = dout * g
    return r * (dy - xhat * jnp.mean(dy * xhat, axis=-1, keepdims=True))


def _softmax(s):
    e = jnp.exp(s - jnp.max(s, axis=-1, keepdims=True))
    return e * (1.0 / jnp.sum(e, axis=-1, keepdims=True))


def _mm(name, a, b, a_spec, b_spec, dn, grid, acc_shape, out_shape, out_specs, epilogue, extra=(), extra_specs=()):
    nk = grid[2]
    n_ex = len(extra)
    multi = isinstance(out_shape, (list, tuple))
    n_out = len(out_shape) if multi else 1

    def body(*refs):
        a_ref, b_ref = refs[0], refs[1]
        ex = refs[2:2 + n_ex]
        outs = refs[2 + n_ex:2 + n_ex + n_out]

        def prod():
            return _dot(a_ref[...].astype(_MXU), b_ref[...].astype(_MXU), dn)

        if nk == 1:
            epilogue(prod(), ex, outs)
        else:
            acc = refs[-1]
            k = pl.program_id(2)

            @pl.when(k == 0)
            def _():
                acc[...] = jnp.zeros_like(acc)

            acc[...] += prod()

            @pl.when(k == nk - 1)
            def _():
                epilogue(acc[...], ex, outs)

    return _call(
        body, name, grid=grid, in_specs=[a_spec, b_spec, *extra_specs], out_specs=out_specs, out_shape=out_shape,
        scratch_shapes=[] if nk == 1 else [pltpu.VMEM(acc_shape, _F32)],
        compiler_params=_cp("parallel", "parallel", "arbitrary"),
    )(a, b, *extra)


def _ep_store(acc, ex, outs):
    for o in outs:
        o[...] = acc.astype(o.dtype)


def _ep_add(acc, ex, outs):
    outs[0][...] = (acc + ex[0][...]).astype(outs[0].dtype)


def _mm_nn(name, a, b, out_dtype=_F32, add=None, pm=1024, pn=1024, pk=2048):
    M, K = a.shape
    N = b.shape[1]
    tm, tn, tk = _tile(M, pm, 8), _tile(N, pn), _tile(K, pk)
    o_spec = BS((tm, tn), lambda i, j, k: (i, j))
    return _mm(name, a, b, BS((tm, tk), lambda i, j, k: (i, k)), BS((tk, tn), lambda i, j, k: (k, j)), _NN,
               (M // tm, N // tn, K // tk), (tm, tn), S((M, N), out_dtype), o_spec,
               _ep_store if add is None else _ep_add,
               extra=() if add is None else (add,), extra_specs=() if add is None else (o_spec,))


def _mm_nt(name, a, b, out_dtype=_F32, pm=1024, pn=1024, pk=4096):
    M, K = a.shape
    N = b.shape[0]
    tm, tn, tk = _tile(M, pm, 8), _tile(N, pn), _tile(K, pk)
    return _mm(name, a, b, BS((tm, tk), lambda i, j, k: (i, k)), BS((tn, tk), lambda i, j, k: (j, k)), _NT,
               (M // tm, N // tn, K // tk), (tm, tn), S((M, N), out_dtype), BS((tm, tn), lambda i, j, k: (i, j)), _ep_store)


def _mm_tn_dual(name, a, b, pm=1024, pn=1024, pk=2048):
    K, M = a.shape
    N = b.shape[1]
    tm, tn, tk = _tile(M, pm), _tile(N, pn), _tile(K, pk, 16)
    o_spec = BS((tm, tn), lambda i, j, k: (i, j))
    return _mm(name, a, b, BS((tk, tm), lambda i, j, k: (k, i)), BS((tk, tn), lambda i, j, k: (k, j)), _TN,
               (M // tm, N // tn, K // tk), (tm, tn), [S((M, N), _F32), S((M, N), _WIRE)], [o_spec, o_spec], _ep_store)


def _ffn_gate_up(h2, wgu):
    T, D = h2.shape
    F = wgu.shape[1] // 2
    tm, tn = _tile(T, 1024, 8), _tile(F, 512)
    nj = F // tn

    def body(a_ref, bg_ref, bu_ref, act_ref, gu_ref):
        a = a_ref[...]
        g = _dot(a, bg_ref[...])
        u = _dot(a, bu_ref[...])
        gu_ref[0] = g
        gu_ref[1] = u
        act_ref[...] = (g * (1.0 / (1.0 + jnp.exp(-g))) * u).astype(act_ref.dtype)

    return _call(
        body, "ffn_gate_up", grid=(T // tm, nj),
        in_specs=[BS((tm, D), lambda i, j: (i, 0)), BS((D, tn), lambda i, j: (0, j)), BS((D, tn), lambda i, j: (0, j + nj))],
        out_specs=[BS((tm, tn), lambda i, j: (i, j)), BS((2, tm, tn), lambda i, j: (0, i, j))],
        out_shape=[S((T, F), _MXU), S((2, T, F), _F32)],
        compiler_params=_cp("parallel", "parallel"),
    )(h2, wgu, wgu)


def _ffn_dact(dxm, wdown, gu, dep=None):
    T, D = dxm.shape
    F = wdown.shape[0]
    tm, tn = _tile(T, 1024, 8), _tile(F, 512)
    deps = [] if dep is None else [dep]

    def body(a_ref, b_ref, gu_ref, *rest):
        o_ref = rest[-1]
        d = _dot(a_ref[...], b_ref[...], _NT)
        g, u = gu_ref[0], gu_ref[1]
        sg = 1.0 / (1.0 + jnp.exp(-g))
        o_ref[0] = (d * u * (sg * (1.0 + g * (1.0 - sg)))).astype(o_ref.dtype)
        o_ref[1] = (d * (g * sg)).astype(o_ref.dtype)

    return _call(
        body, "ffn_dact", grid=(T // tm, F // tn),
        in_specs=[BS((tm, D), lambda i, j: (i, 0)), BS((tn, D), lambda i, j: (j, 0)), BS((2, tm, tn), lambda i, j: (0, i, j))]
        + [_ANY] * len(deps),
        out_specs=BS((2, tm, tn), lambda i, j: (0, i, j)), out_shape=S((2, T, F), _MXU),
        compiler_params=_cp("parallel", "parallel"),
    )(dxm, wdown, gu, *deps)


def _ffn_dh(dgu, wgu):
    _, T, F = dgu.shape
    D = wgu.shape[0]
    tm, tn, tk = _tile(T, 1024, 8), _tile(D, 2048), _tile(F, 2048)
    nkf = F // tk
    return _mm("ffn_dh", dgu, wgu, BS((None, tm, tk), lambda i, j, k: (k // nkf, i, k % nkf)),
               BS((tn, tk), lambda i, j, k: (j, k)), _NT, (T // tm, D // tn, 2 * nkf), (tm, tn),
               S((T, D), _F32), BS((tm, tn), lambda i, j, k: (i, j)), _ep_store)


def _ffn_dwgu(h2, dgu):
    _, T, F = dgu.shape
    D = h2.shape[1]
    tm, tn, tk = _tile(D, 1024), _tile(F, 1408), _tile(T, 2048, 16)
    njf = F // tn
    o_spec = BS((tm, tn), lambda i, j, k: (i, j))
    return _mm("ffn_dwgu", h2, dgu, BS((tk, tm), lambda i, j, k: (k, i)),
               BS((None, tk, tn), lambda i, j, k: (j // njf, k, j % njf)), _TN, (D // tm, 2 * njf, T // tk), (tm, tn),
               [S((D, 2 * F), _F32), S((D, 2 * F), _WIRE)], [o_spec, o_spec], _ep_store)


def _rmsnorm_fwd(x, g, name, dep=None):
    T, D = x.shape
    tr = _tile(T, 512, 8)

    def body(x_ref, g_ref, *rest):
        xv = x_ref[...]
        rest[-1][...] = (xv * _rstd(xv) * g_ref[...]).astype(rest[-1].dtype)

    row = BS((tr, D), lambda i: (i, 0))
    deps = [] if dep is None else [dep]
    return _call(body, name, grid=(T // tr,), in_specs=[row, BS((1, D), lambda i: (0, 0))] + [_ANY] * len(deps), out_specs=row,
                 out_shape=S((T, D), _MXU), compiler_params=_cp("parallel"))(x, g.reshape(1, D), *deps)


def _rmsnorm_bwd(x, g, dh, dres, name):
    T, D = x.shape
    tr = _tile(T, 256, 8)
    has_res = dres is not None

    def body(*refs):
        x_ref, g_ref, dh_ref = refs[:3]
        dx_ref, dxm_ref, dg_ref = refs[-3:]

        @pl.when(pl.program_id(0) == 0)
        def _():
            dg_ref[...] = jnp.zeros_like(dg_ref)

        xv = x_ref[...]
        r = _rstd(xv)
        xhat = xv * r
        dh_v = dh_ref[...]
        dg_ref[...] += jnp.sum(dh_v * xhat, axis=0, keepdims=True)
        dx = _norm_bwd(dh_v, xhat, r, g_ref[...])
        if has_res:
            dx = dx + refs[3][...]
        dx_ref[...] = dx
        dxm_ref[...] = dx.astype(dxm_ref.dtype)

    row = BS((tr, D), lambda i: (i, 0))
    vec = BS((1, D), lambda i: (0, 0))
    return _call(body, name, grid=(T // tr,), in_specs=[row, vec, row] + ([row] if has_res else []),
                 out_specs=[row, row, vec], out_shape=[S((T, D), _F32), S((T, D), _MXU), S((1, D), _F32)],
                 compiler_params=_cp("arbitrary"))(x, g.reshape(1, D), dh, *([dres] if has_res else []))


def _loss_head(y, target):
    T, D = y.shape
    tr = _tile(T, 256, 8)

    def body(y_ref, t_ref, dy_ref, dym_ref, acc_ref):
        @pl.when(pl.program_id(0) == 0)
        def _():
            acc_ref[...] = jnp.zeros_like(acc_ref)

        err = y_ref[...] - t_ref[...]
        acc_ref[...] += jnp.sum(jnp.sum(err * err, axis=-1, keepdims=True), axis=0, keepdims=True)
        dy = err * (1.0 / D)
        dy_ref[...] = dy
        dym_ref[...] = dy.astype(dym_ref.dtype)

    row = BS((tr, D), lambda i: (i, 0))
    return _call(body, "loss_head", grid=(T // tr,), in_specs=[row, row],
                 out_specs=[row, row, BS((1, 128), lambda i: (0, 0))],
                 out_shape=[S((T, D), _F32), S((T, D), _MXU), S((1, 128), _F32)],
                 compiler_params=_cp("arbitrary"))(y, target)


def _mixa_blocks(T):
    return 2 if T % (2 * CHUNK) == 0 else 1


def _mixer_a_fwd(z, gv, ws_m, b_t, TOK):
    T = z.shape[0]
    G = TOK // HEAD
    CB = _mixa_blocks(T)
    R = CB * CHUNK

    def body(z_ref, gv_ref, ws_ref, bt_ref, o_ref):
        u = _gelu(z_ref[:, :TOK])
        v = _gelu(z_ref[:, TOK:])
        vn = (v * _rstd(v) * gv_ref[...]).astype(_MXU)
        for c in range(CB):
            rows = slice(c * CHUNK, (c + 1) * CHUNK)
            for g in range(G):
                cols = slice(g * HEAD, (g + 1) * HEAD)
                s = _dot(ws_ref[g], vn[rows, cols]) + bt_ref[:, g:g + 1]
                o_ref[rows, cols] = (u[rows, cols] * s).astype(o_ref.dtype)

    return _call(
        body, "mixer_a_fwd", grid=(T // R,),
        in_specs=[BS((R, 2 * TOK), lambda i: (i, 0)), BS((1, TOK), lambda i: (0, 0)),
                  BS((G, CHUNK, CHUNK), lambda i: (0, 0, 0)), BS((CHUNK, G), lambda i: (0, 0))],
        out_specs=BS((R, TOK), lambda i: (i, 0)), out_shape=S((T, TOK), _MXU), compiler_params=_cp("parallel"),
    )(z, gv.reshape(1, TOK), ws_m, b_t)


def _mixer_a_bwd(z, dcat, gv, ws_m, wst_m, b_t, TOK):
    T = z.shape[0]
    G = TOK // HEAD
    CB = _mixa_blocks(T)
    R = CB * CHUNK
    n = T // R

    def body(z_ref, d_ref, gv_ref, ws_ref, wst_ref, bt_ref, dz_ref, dws_ref, db_ref, dgv_ref, dvn_scr):
        i = pl.program_id(0)

        @pl.when(i == 0)
        def _():
            dws_ref[...] = jnp.zeros_like(dws_ref)
            db_ref[...] = jnp.zeros_like(db_ref)
            dgv_ref[...] = jnp.zeros_like(dgv_ref)

        zu = z_ref[:, :TOK]
        zv = z_ref[:, TOK:]
        u = _gelu(zu)
        v = _gelu(zv)
        r = _rstd(v)
        vhat = v * r
        gvv = gv_ref[...]
        vn = (vhat * gvv).astype(_MXU)
        d = d_ref[...]
        gpu = _gelu_grad(zu)
        for c in range(CB):
            rows = slice(c * CHUNK, (c + 1) * CHUNK)
            for g in range(G):
                cols = slice(g * HEAD, (g + 1) * HEAD)
                vn_cg = vn[rows, cols]
                s = _dot(ws_ref[g], vn_cg) + bt_ref[:, g:g + 1]
                d_cg = d[rows, cols]
                dz_ref[rows, cols] = (d_cg * s * gpu[rows, cols]).astype(dz_ref.dtype)
                ds = d_cg * u[rows, cols]
                ds_m = ds.astype(_MXU)
                dvn_scr[rows, cols] = _dot(wst_ref[g], ds_m)
                dws_ref[g] += _dot(ds_m, vn_cg, _NT)
                db_ref[g] += ds
        dvn = dvn_scr[...]
        dgv_ref[...] += jnp.sum(dvn * vhat, axis=0, keepdims=True)
        dv = _norm_bwd(dvn, vhat, r, gvv)
        dz_ref[:, TOK:] = (dv * _gelu_grad(zv)).astype(dz_ref.dtype)

        @pl.when(i == n - 1)
        def _():
            for g in range(G):
                db_ref[g] = jnp.broadcast_to(jnp.sum(db_ref[g], axis=1, keepdims=True), (CHUNK, CHUNK))

    full3 = BS((G, CHUNK, CHUNK), lambda i: (0, 0, 0))
    return _call(
        body, "mixer_a_bwd", grid=(n,),
        in_specs=[BS((R, 2 * TOK), lambda i: (i, 0)), BS((R, TOK), lambda i: (i, 0)), BS((1, TOK), lambda i: (0, 0)),
                  full3, full3, BS((CHUNK, G), lambda i: (0, 0))],
        out_specs=[BS((R, 2 * TOK), lambda i: (i, 0)), full3, full3, BS((1, TOK), lambda i: (0, 0))],
        out_shape=[S((T, 2 * TOK), _MXU), S((G, CHUNK, CHUNK), _F32), S((G, CHUNK, CHUNK), _F32), S((1, TOK), _F32)],
        scratch_shapes=[pltpu.VMEM((R, TOK), _F32)], compiler_params=_cp("arbitrary"),
    )(z, dcat, gv.reshape(1, TOK), ws_m, wst_m, b_t)


def _rope_tables(T):
    n_rows = T // GRID_W
    rows = jnp.broadcast_to(jnp.arange(n_rows)[:, None], (n_rows, GRID_W)).reshape(T)
    cols = jnp.broadcast_to(jnp.arange(GRID_W)[None, :], (n_rows, GRID_W)).reshape(T)
    pairs = HEAD // 4
    freqs = ROPE_THETA ** (-jnp.arange(pairs, dtype=_F32) / pairs)
    ang_r = rows.astype(_F32)[:, None] * freqs
    ang_c = cols.astype(_F32)[:, None] * freqs
    ang = jnp.concatenate([ang_r, ang_r, ang_c, ang_c], axis=-1)
    cos, sin = jnp.cos(ang), jnp.sin(ang)
    first = (jnp.arange(HEAD) % (HEAD // 2)) < (HEAD // 4)
    return cos, jnp.where(first, -sin, 0.0), jnp.where(first, 0.0, sin)


def _rope(x, cs, sa, sb):
    return x * cs + pltpu.roll(x, 96, 1) * sa + pltpu.roll(x, 32, 1) * sb


def _qk_rope_fwd(z, gq, gk, tabs, TOK, KV):
    T = z.shape[0]
    R = _tile(T, 512, 8)
    W = TOK + 2 * KV

    def body(z_ref, gq_ref, gk_ref, cos_ref, sa_ref, sb_ref, q_ref, k_ref, v_ref):
        cs, sa, sb = cos_ref[...], sa_ref[...], sb_ref[...]
        for h in range((TOK + KV) // HEAD):
            cols = slice(h * HEAD, (h + 1) * HEAD)
            xv = z_ref[:, cols]
            xn = xv * _rstd(xv) * (gq_ref[...] if h < TOK // HEAD else gk_ref[...])
            out = _rope(xn, cs, sa, sb)
            if h < TOK // HEAD:
                q_ref[:, cols] = out.astype(q_ref.dtype)
            else:
                k_ref[:, h * HEAD - TOK:(h + 1) * HEAD - TOK] = out.astype(k_ref.dtype)
        v_ref[...] = z_ref[:, TOK + KV:].astype(v_ref.dtype)

    vec = BS((1, HEAD), lambda i: (0, 0))
    tab = BS((R, HEAD), lambda i: (i, 0))
    return _call(
        body, "qk_rope_fwd", grid=(T // R,), in_specs=[BS((R, W), lambda i: (i, 0)), vec, vec, tab, tab, tab],
        out_specs=[BS((R, TOK), lambda i: (i, 0)), BS((R, KV), lambda i: (i, 0)), BS((R, KV), lambda i: (i, 0))],
        out_shape=[S((T, TOK), _MXU), S((T, KV), _MXU), S((T, KV), _MXU)], compiler_params=_cp("parallel"),
    )(z, gq.reshape(1, HEAD), gk.reshape(1, HEAD), *tabs)


def _qk_rope_bwd(z, dq, dk, gq, gk, tabs, TOK, KV):
    T = z.shape[0]
    R = _tile(T, 512, 8)
    W = TOK + KV

    def body(z_ref, dq_ref, dk_ref, gq_ref, gk_ref, cos_ref, sa_ref, sb_ref, dz_ref, dgq_ref, dgk_ref):
        @pl.when(pl.program_id(0) == 0)
        def _():
            dgq_ref[...] = jnp.zeros_like(dgq_ref)
            dgk_ref[...] = jnp.zeros_like(dgk_ref)

        cs, sa, sb = cos_ref[...], sa_ref[...], sb_ref[...]
        for h in range(W // HEAD):
            cols = slice(h * HEAD, (h + 1) * HEAD)
            is_q = h < TOK // HEAD
            do = dq_ref[:, cols] if is_q else dk_ref[:, h * HEAD - TOK:(h + 1) * HEAD - TOK]
            dxn = do * cs - pltpu.roll(do, 96, 1) * sa - pltpu.roll(do, 32, 1) * sb
            xv = z_ref[:, cols]
            r = _rstd(xv)
            xhat = xv * r
            dg_ref = dgq_ref if is_q else dgk_ref
            dg_ref[...] += jnp.sum(dxn * xhat, axis=0, keepdims=True)
            dz_ref[:, cols] = _norm_bwd(dxn, xhat, r, gq_ref[...] if is_q else gk_ref[...]).astype(dz_ref.dtype)

    vec = BS((1, HEAD), lambda i: (0, 0))
    tab = BS((R, HEAD), lambda i: (i, 0))
    return _call(
        body, "qk_rope_bwd", grid=(T // R,),
        in_specs=[BS((R, W), lambda i: (i, 0)), BS((R, TOK), lambda i: (i, 0)), BS((R, KV), lambda i: (i, 0)), vec, vec, tab, tab, tab],
        out_specs=[BS((R, W), lambda i: (i, 0)), vec, vec],
        out_shape=[S((T, W), _MXU), S((1, HEAD), _F32), S((1, HEAD), _F32)], compiler_params=_cp("arbitrary"),
    )(z, dq, dk, gq.reshape(1, HEAD), gk.reshape(1, HEAD), *tabs)


_ATTN_C2 = float(HEAD ** -0.5 * np.log2(np.e))


def _attn_fwd(q, k, v, QPK):
    T, TOK = q.shape
    KVH = k.shape[1] // HEAD
    tq = _tile(T, 256, 8)
    W = QPK * HEAD

    def body(q_ref, k_ref, v_ref, o_ref, st_ref, vaug):
        @pl.when(pl.program_id(1) == 0)
        def _():
            vaug[:, :HEAD] = v_ref[...]
            vaug[:, HEAD:] = jnp.ones((T, HEAD), vaug.dtype)

        kk, va = k_ref[...], vaug[...]
        for g in range(QPK):
            cols = slice(g * HEAD, (g + 1) * HEAD)
            s = _dot(q_ref[:, cols], kk, _NT)
            m = jnp.max(s, axis=-1, keepdims=True)
            ov = _dot(jnp.exp2((s - m) * _ATTN_C2).astype(_MXU), va)
            l = ov[:, HEAD:HEAD + 1]
            o_ref[:, cols] = (ov[:, :HEAD] * (1.0 / l)).astype(o_ref.dtype)
            st_ref[:, g:g + 1] = m + jnp.log2(l) * (1.0 / _ATTN_C2)

    qs = BS((tq, W), lambda h, i: (i, h))
    ks = BS((T, HEAD), lambda h, i: (0, h))
    return _call(body, "attn_fwd", grid=(KVH, T // tq), in_specs=[qs, ks, ks],
                 out_specs=[qs, BS((None, tq, QPK), lambda h, i: (h, i, 0))],
                 out_shape=[S((T, TOK), _MXU), S((KVH, T, QPK), _F32)],
                 scratch_shapes=[pltpu.VMEM((T, 2 * HEAD), _MXU)],
                 compiler_params=_cp("parallel", "arbitrary"))(q, k, v)


def _attn_bwd(q, k, v, dcat, o, stat, QPK):
    T, TOK = q.shape
    KV = k.shape[1]
    KVH = KV // HEAD
    tq = _tile(T, 256, 8)
    nq = T // tq
    W = QPK * HEAD
    scale = HEAD ** -0.5

    def body(q_ref, k_ref, v_ref, do_ref, o_ref, st_ref, dq_ref, dk_ref, dv_ref, dk_acc, dv_acc):
        i = pl.program_id(1)

        @pl.when(i == 0)
        def _():
            dk_acc[...] = jnp.zeros_like(dk_acc)
            dv_acc[...] = jnp.zeros_like(dv_acc)

        kk, vv = k_ref[...], v_ref[...]
        for g in range(QPK):
            cols = slice(g * HEAD, (g + 1) * HEAD)
            qg = q_ref[:, cols]
            p = jnp.exp2((_dot(qg, kk, _NT) - st_ref[:, g:g + 1]) * _ATTN_C2)
            do32 = do_ref[:, cols]
            do = do32.astype(_MXU)
            delta = jnp.sum(do32 * o_ref[:, cols].astype(_F32), axis=-1, keepdims=True)
            ds = (p * (_dot(do, vv, _NT) - delta)).astype(_MXU)
            dq_ref[:, cols] = _dot(ds, kk) * scale
            dk_acc[...] += _dot(ds, qg, _TN)
            dv_acc[...] += _dot(p.astype(_MXU), do, _TN)

        @pl.when(i == nq - 1)
        def _():
            dk_ref[...] = dk_acc[...] * scale
            dv_ref[...] = dv_acc[...].astype(dv_ref.dtype)

    qs = BS((tq, W), lambda h, i: (i, h))
    ks = BS((T, HEAD), lambda h, i: (0, h))
    return _call(
        body, "attn_bwd", grid=(KVH, nq), in_specs=[qs, ks, ks, qs, qs, BS((None, tq, QPK), lambda h, i: (h, i, 0))],
        out_specs=[qs, ks, ks], out_shape=[S((T, TOK), _F32), S((T, KV), _F32), S((T, KV), _MXU)],
        scratch_shapes=[pltpu.VMEM((T, HEAD), _F32), pltpu.VMEM((T, HEAD), _F32)],
        compiler_params=_cp("parallel", "arbitrary"),
    )(q, k, v, dcat, o, stat)


def _mem_fwd(z, qblk, kv, gmq, gmk, MEMW):
    T = z.shape[0]
    NM = kv.shape[0]
    tq = _tile(T, 512, 8)
    scale = HEAD ** -0.5

    def body(q_ref, kv_ref, gq_ref, gk_ref, o_ref):
        for h in range(MEMW // HEAD):
            cols = slice(h * HEAD, (h + 1) * HEAD)
            kx = kv_ref[:, cols]
            kn = (kx * _rstd(kx) * gk_ref[...]).astype(_MXU)
            vv = kv_ref[:, MEMW + h * HEAD:MEMW + (h + 1) * HEAD].astype(_MXU)
            qx = q_ref[:, cols]
            qn = (qx * _rstd(qx) * gq_ref[...]).astype(_MXU)
            p = _softmax(_dot(qn, kn, _NT) * scale)
            o_ref[:, cols] = _dot(p.astype(_MXU), vv).astype(o_ref.dtype)

    vec = BS((1, HEAD), lambda i: (0, 0))
    return _call(
        body, "mem_fwd", grid=(T // tq,),
        in_specs=[BS((tq, MEMW), lambda i: (i, qblk)), BS((NM, 2 * MEMW), lambda i: (0, 0)), vec, vec],
        out_specs=BS((tq, MEMW), lambda i: (i, 0)), out_shape=S((T, MEMW), _MXU), compiler_params=_cp("parallel"),
    )(z, kv, gmq.reshape(1, HEAD), gmk.reshape(1, HEAD))


def _mem_bwd(z, qblk, kv, gmq, gmk, dcat, dblk, MEMW):
    T = z.shape[0]
    NM = kv.shape[0]
    tq = _tile(T, 512, 8)
    scale = HEAD ** -0.5

    def body(q_ref, kv_ref, gq_ref, gk_ref, do_ref, dz_ref, dkn_ref, dv_ref, dgq_ref):
        @pl.when(pl.program_id(0) == 0)
        def _():
            dkn_ref[...] = jnp.zeros_like(dkn_ref)
            dv_ref[...] = jnp.zeros_like(dv_ref)
            dgq_ref[...] = jnp.zeros_like(dgq_ref)

        for h in range(MEMW // HEAD):
            cols = slice(h * HEAD, (h + 1) * HEAD)
            kx = kv_ref[:, cols]
            kn = (kx * _rstd(kx) * gk_ref[...]).astype(_MXU)
            vv = kv_ref[:, MEMW + h * HEAD:MEMW + (h + 1) * HEAD].astype(_MXU)
            qx = q_ref[:, cols]
            rq = _rstd(qx)
            qhat = qx * rq
            qn = (qhat * gq_ref[...]).astype(_MXU)
            p = _softmax(_dot(qn, kn, _NT) * scale)
            do = do_ref[:, cols].astype(_MXU)
            dp = _dot(do, vv, _NT)
            ds = (p * (dp - jnp.sum(p * dp, axis=-1, keepdims=True)) * scale).astype(_MXU)
            dqn = _dot(ds, kn)
            dkn_ref[:, cols] += _dot(ds, qn, _TN)
            dv_ref[:, cols] += _dot(p.astype(_MXU), do, _TN)
            dgq_ref[...] += jnp.sum(dqn * qhat, axis=0, keepdims=True)
            dz_ref[:, cols] = _norm_bwd(dqn, qhat, rq, gq_ref[...]).astype(dz_ref.dtype)

    vec = BS((1, HEAD), lambda i: (0, 0))
    kvs = BS((NM, MEMW), lambda i: (0, 0))
    return _call(
        body, "mem_bwd", grid=(T // tq,),
        in_specs=[BS((tq, MEMW), lambda i: (i, qblk)), BS((NM, 2 * MEMW), lambda i: (0, 0)), vec, vec,
                  BS((tq, MEMW), lambda i: (i, dblk))],
        out_specs=[BS((tq, MEMW), lambda i: (i, 0)), kvs, kvs, vec],
        out_shape=[S((T, MEMW), _MXU), S((NM, MEMW), _F32), S((NM, MEMW), _F32), S((1, HEAD), _F32)],
        compiler_params=_cp("arbitrary"),
    )(z, kv, gmq.reshape(1, HEAD), gmk.reshape(1, HEAD), dcat)


def _memkv_bwd(kv, dkn, dv, gmk, MEMW):
    NM = kv.shape[0]

    def body(kv_ref, dkn_ref, dv_ref, gk_ref, dkv_ref, dgk_ref):
        dgk = jnp.zeros((1, HEAD), _F32)
        for h in range(MEMW // HEAD):
            cols = slice(h * HEAD, (h + 1) * HEAD)
            kx = kv_ref[:, cols]
            r = _rstd(kx)
            khat = kx * r
            dkn = dkn_ref[:, cols]
            dgk = dgk + jnp.sum(dkn * khat, axis=0, keepdims=True)
            dkv_ref[:, cols] = _norm_bwd(dkn, khat, r, gk_ref[...]).astype(dkv_ref.dtype)
        dgk_ref[...] = dgk
        dkv_ref[:, MEMW:] = dv_ref[...].astype(dkv_ref.dtype)

    return _call(body, "memkv_bwd", out_shape=[S((NM, 2 * MEMW), _MXU), S((1, HEAD), _F32)],
                 compiler_params=_cp())(kv, dkn, dv, gmk.reshape(1, HEAD))


def _cast_into_full(w, l, sh, idx, dep=None):
    tr, tc = _tile(sh.Rs, 512, 16), _tile(sh.Cs, 2048)
    nr, nc = sh.Rs // tr, sh.Cs // tc
    deps = [] if dep is None else [dep]

    def body(i_ref, c_ref, w_ref, *rest):
        rest[-1][...] = w_ref[...].astype(rest[-1].dtype)

    if sh.by_cols:
        o_map = lambda a, b, si, sc: (a, si[0] * nc + b)
    else:
        o_map = lambda a, b, si, sc: (si[0] * nr + a, b)
    return _call(
        body, "cast_into_full",
        grid_spec=pltpu.PrefetchScalarGridSpec(
            num_scalar_prefetch=2, grid=(nr, nc),
            in_specs=[BS((None, tr, tc), lambda a, b, si, sc: (l, a, b))] + [_ANY] * len(deps), out_specs=BS((tr, tc), o_map)),
        out_shape=S((sh.R, sh.C), _WIRE), compiler_params=_cp("parallel", "parallel"),
    )(*idx, w, *deps)


def _adamw(w, g, m, v, name):
    shape = w.shape
    C = shape[-1]
    args = [a.reshape(-1, C) for a in (w, g, m, v)]
    R = args[0].shape[0]
    tr, tc = _tile(R, 256, 8), _tile(C, 1024)
    c_m = 1.0 / (1.0 - ADAM_B1 ** ADAM_STEP)
    c_v = 1.0 / (1.0 - ADAM_B2 ** ADAM_STEP)

    def body(w_ref, g_ref, m_ref, v_ref, go_ref, d_ref, mo_ref, vo_ref):
        gv = g_ref[...]
        mn = ADAM_B1 * m_ref[...] + (1.0 - ADAM_B1) * gv
        vn = ADAM_B2 * v_ref[...] + (1.0 - ADAM_B2) * (gv * gv)
        go_ref[...] = gv
        mo_ref[...] = mn
        vo_ref[...] = vn
        d_ref[...] = -ADAM_LR * ((mn * c_m) / (jnp.sqrt(vn * c_v) + ADAM_EPS) + ADAM_WD * w_ref[...])

    blk = BS((tr, tc), lambda i, j: (i, j))
    outs = _call(body, name, grid=(R // tr, C // tc), in_specs=[blk] * 4, out_specs=[blk] * 4,
                 out_shape=[S((R, C), _F32)] * 4, compiler_params=_cp("parallel", "parallel"))(*args)
    return [o.reshape(shape) for o in outs]


def _where_am_i():
    x, y, c = lax.axis_index("x"), lax.axis_index("y"), lax.axis_index("c")
    chips = [(1 - x, y), (x, 1 - y), (1 - x, 1 - y)]
    return x, y, c, 2 * x + y, chips, [2 * cx + cy for cx, cy in chips]


class _Shard:
    def __init__(self, R, C, by_cols):
        self.R, self.C, self.by_cols = R, C, by_cols
        self.Rs, self.Cs = (R, C // 4) if by_cols else (R // 4, C)
        self.Rh = self.Rs // 2
        self.Q = R // 8

    def full_piece(self, ref, j, cc):
        if self.by_cols:
            return ref.at[pl.ds(cc * self.Rh, self.Rh), pl.ds(_mo(j * self.Cs, 128), self.Cs)]
        return ref.at[pl.ds(_mo(j * self.Rs + cc * self.Rh, 16), self.Rh), :]

    def full_shard(self, ref, j):
        if self.by_cols:
            return ref.at[:, pl.ds(_mo(j * self.Cs, 128), self.Cs)]
        return ref.at[pl.ds(_mo(j * self.Rs, 16), self.Rs), :]

    def shard_half(self, ref, cc):
        return ref.at[pl.ds(_mo(cc * self.Rh, 16), self.Rh), :]

    def half_piece(self, ref, j):
        if self.by_cols:
            return ref.at[:, pl.ds(_mo(j * self.Cs, 128), self.Cs)]
        return ref.at[pl.ds(_mo(j * self.Rh, 16), self.Rh), :]


def _remote(src, dst, ssem, rsem, dev):
    return pltpu.make_async_remote_copy(src_ref=src, dst_ref=dst, send_sem=ssem, recv_sem=rsem, device_id=dev, device_id_type=_MESH)


_HBM = pl.BlockSpec(memory_space=pltpu.HBM)
_SEM = pl.BlockSpec(memory_space=pltpu.SEMAPHORE)
_EFFECT = pltpu.SideEffectType.DATAFLOW_SIDE_EFFECTING


def _in_hbm(a):
    return pltpu.with_memory_space_constraint(a, pltpu.HBM)


def _gather_start(fulls, shs, name):
    n = len(fulls)

    def body(*refs):
        bufs = refs[:n]
        send_sems, recv_sems = refs[n], refs[n + 1]
        token = refs[-1]
        x, y, c, me, chips, chip_ids = _where_am_i()
        for t in range(n):
            mine = shs[t].full_piece(bufs[t], me, c)
            for r in range(3):
                _remote(mine, mine, send_sems.at[3 * t + r], recv_sems.at[3 * t + r], (*chips[r], c)).start()
        token[...] = jnp.zeros_like(token)

    out = pl.pallas_call(
        body, name=name, in_specs=[_HBM] * n,
        out_shape=(pltpu.SemaphoreType.DMA((3 * n,)), pltpu.SemaphoreType.DMA((3 * n,)), *[pltpu.HBM(f.shape, f.dtype) for f in fulls],
                   S((8, 128), _F32)),
        out_specs=(_SEM, _SEM, *[_HBM] * n, pl.BlockSpec(memory_space=pltpu.VMEM)),
        input_output_aliases={t: 2 + t for t in range(n)},
        compiler_params=pltpu.CompilerParams(has_side_effects=_EFFECT), **_KW,
    )(*[_in_hbm(f) for f in fulls])
    return out[0], out[1], list(out[2:2 + n]), out[-1]


def _gather_wait(fulls, send_sems, recv_sems, after, shs, name):
    n = len(fulls)

    def body(*refs):
        bufs = refs[:n]
        ssem, rsem = refs[n], refs[n + 1]
        x, y, c, me, chips, chip_ids = _where_am_i()
        for t in range(n):
            mine = shs[t].full_piece(bufs[t], me, c)
            for r in range(3):
                _remote(mine, mine, ssem.at[3 * t + r], rsem.at[3 * t + r], (*chips[r], c)).wait_send()
        for t in range(n):
            for r in range(3):
                piece = shs[t].full_piece(bufs[t], chip_ids[r], c)
                _remote(piece, piece, ssem.at[3 * t + r], rsem.at[3 * t + r], (*chips[r], c)).wait_recv()

    out = pl.pallas_call(
        body, name=name, in_specs=[*[_HBM] * n, _SEM, _SEM, _ANY], out_specs=[_HBM] * n,
        out_shape=[pltpu.HBM(f.shape, f.dtype) for f in fulls], input_output_aliases={t: t for t in range(n)},
        compiler_params=pltpu.CompilerParams(has_side_effects=_EFFECT), **_KW,
    )(*fulls, send_sems, recv_sems, after)
    return list(out)


def _gather_pass_on(fulls, shs, name):
    n = len(fulls)

    def body(*refs):
        bufs = refs[n:2 * n]
        send_sems, recv_sems = refs[2 * n:]
        x, y, c, me, chips, chip_ids = _where_am_i()
        sib = (x, y, 1 - c)
        cps = []
        for t in range(n):
            for r in range(3):
                piece = shs[t].full_piece(bufs[t], chip_ids[r], c)
                cps.append(_remote(piece, piece, send_sems.at[t, r], recv_sems.at[t, r], sib))
        for cp in cps:
            cp.start()
        for t in range(n):
            for r in range(3):
                piece = shs[t].full_piece(bufs[t], chip_ids[r], 1 - c)
                _remote(piece, piece, send_sems.at[t, r], recv_sems.at[t, r], sib).wait_recv()
        for cp in cps:
            cp.wait_send()

    return _call(
        body, name, in_specs=[_ANY] * n, out_specs=[_ANY] * n, out_shape=[S(f.shape, f.dtype) for f in fulls],
        input_output_aliases={t: t for t in range(n)},
        scratch_shapes=[pltpu.SemaphoreType.DMA((n, 3)), pltpu.SemaphoreType.DMA((n, 3))],
        compiler_params=pltpu.CompilerParams(has_side_effects=True),
    )(*fulls)


def _rs_pair_exchange(dws, shs, name):
    n = len(dws)

    def body(*refs):
        ins, outs = refs[:n], refs[n:2 * n]
        send_sems, recv_sems = refs[2 * n:]
        x, y, c, *_ = _where_am_i()
        sib = (x, y, 1 - c)
        cps = []
        for t in range(n):
            sh = shs[t]
            if sh.by_cols:
                cps.append(_remote(ins[t].at[pl.ds((1 - c) * sh.Rh, sh.Rh), :], outs[t], send_sems.at[t, 0], recv_sems.at[t, 0], sib))
            else:
                for j in range(4):
                    cps.append(_remote(sh.full_piece(ins[t], j, 1 - c), sh.half_piece(outs[t], j),
                                       send_sems.at[t, j], recv_sems.at[t, j], sib))
        for cp in cps:
            cp.start()
        for cp in cps:
            cp.wait()

    return _call(
        body, name, in_specs=[_ANY] * n, out_specs=[_ANY] * n, out_shape=[S((sh.R // 2, sh.C), _WIRE) for sh in shs],
        scratch_shapes=[pltpu.SemaphoreType.DMA((n, 4)), pltpu.SemaphoreType.DMA((n, 4))],
        compiler_params=pltpu.CompilerParams(has_side_effects=True),
    )(*dws)


def _rs_pair_add(dw32, recv, sh, idx):
    tr, tc = _tile(sh.Q, 512, 16), _tile(sh.C, 2048)
    nb = sh.Q // tr

    def body(i_ref, c_ref, a_ref, b_ref, o32_ref, ow_ref):
        p = a_ref[...] + b_ref[...].astype(_F32)
        o32_ref[...] = p
        ow_ref[...] = p.astype(ow_ref.dtype)

    if sh.by_cols:
        a_map = lambda j, i, b, si, sc: (sc[0] * 4 * nb + j * nb + i, b)
    else:
        a_map = lambda j, i, b, si, sc: (j * 2 * nb + sc[0] * nb + i, b)
    h_spec = BS((tr, tc), lambda j, i, b, si, sc: (j * nb + i, b))
    return _call(
        body, "rs_pair_add",
        grid_spec=pltpu.PrefetchScalarGridSpec(num_scalar_prefetch=2, grid=(4, nb, sh.C // tc),
                                               in_specs=[BS((tr, tc), a_map), h_spec], out_specs=[h_spec, h_spec]),
        out_shape=[S((sh.R // 2, sh.C), _F32), S((sh.R // 2, sh.C), _WIRE)],
        compiler_params=_cp("parallel", "parallel", "parallel"),
    )(*idx, dw32, recv)


def _rs_chip_start(pws, shs, name):
    n = len(pws)
    lands = [lax.empty((3, sh.Rh, sh.Cs), _WIRE) for sh in shs]

    def body(*refs):
        ins, lnd = refs[:n], refs[n:2 * n]
        send_sems, recv_sems = refs[2 * n], refs[2 * n + 1]
        token = refs[-1]
        x, y, c, me, chips, chip_ids = _where_am_i()
        for t in range(n):
            for r in range(3):
                _remote(shs[t].half_piece(ins[t], chip_ids[r]), lnd[t].at[r], send_sems.at[3 * t + r], recv_sems.at[3 * t + r],
                        (*chips[r], c)).start()
        token[...] = jnp.zeros_like(token)

    out = pl.pallas_call(
        body, name=name, in_specs=[_HBM] * (2 * n),
        out_shape=(pltpu.SemaphoreType.DMA((3 * n,)), pltpu.SemaphoreType.DMA((3 * n,)),
                   *[pltpu.HBM(a.shape, a.dtype) for a in (*pws, *lands)], S((8, 128), _F32)),
        out_specs=(_SEM, _SEM, *[_HBM] * (2 * n), pl.BlockSpec(memory_space=pltpu.VMEM)),
        input_output_aliases={t: 2 + t for t in range(2 * n)},
        compiler_params=pltpu.CompilerParams(has_side_effects=_EFFECT), **_KW,
    )(*[_in_hbm(a) for a in (*pws, *lands)])
    return out[0], out[1], list(out[2:2 + n]), list(out[2 + n:2 + 2 * n]), out[-1]


def _rs_chip_wait(pws, lands, send_sems, recv_sems, after, shs, name):
    n = len(pws)

    def body(*refs):
        ins, lnd = refs[:n], refs[n:2 * n]
        ssem, rsem = refs[2 * n], refs[2 * n + 1]
        x, y, c, me, chips, chip_ids = _where_am_i()
        for t in range(n):
            for r in range(3):
                cp = _remote(shs[t].half_piece(ins[t], chip_ids[r]), lnd[t].at[r], ssem.at[3 * t + r], rsem.at[3 * t + r], (*chips[r], c))
                cp.wait_send()
        for t in range(n):
            for r in range(3):
                cp = _remote(shs[t].half_piece(ins[t], chip_ids[r]), lnd[t].at[r], ssem.at[3 * t + r], rsem.at[3 * t + r], (*chips[r], c))
                cp.wait_recv()

    out = pl.pallas_call(
        body, name=name, in_specs=[*[_HBM] * (2 * n), _SEM, _SEM, _ANY], out_specs=[_HBM] * (2 * n),
        out_shape=[pltpu.HBM(a.shape, a.dtype) for a in (*pws, *lands)], input_output_aliases={t: t for t in range(2 * n)},
        compiler_params=pltpu.CompilerParams(has_side_effects=_EFFECT), **_KW,
    )(*pws, *lands, send_sems, recv_sems, after)
    return list(out[n:])


def _rs_chip_add(p32, recv, sh, idx, g_prev, l, L):
    tr, tc = _tile(sh.Rh, 512, 16), _tile(sh.Cs, 2048)
    nr, nc = sh.Rh // tr, sh.Cs // tc

    def body(i_ref, c_ref, a_ref, b_ref, *rest):
        rest[-1][...] = a_ref[...] + b_ref[0].astype(_F32) + b_ref[1].astype(_F32) + b_ref[2].astype(_F32)

    if sh.by_cols:
        a_map = lambda a, b, si, sc: (a, si[0] * nc + b)
    else:
        a_map = lambda a, b, si, sc: (si[0] * nr + a, b)
    in_specs = [BS((tr, tc), a_map), BS((3, tr, tc), lambda a, b, si, sc: (0, a, b))]
    args = [*idx, p32, recv]
    if g_prev is not None:
        in_specs.append(_ANY)
        args.append(g_prev)
    return _call(
        body, "rs_chip_add",
        grid_spec=pltpu.PrefetchScalarGridSpec(num_scalar_prefetch=2, grid=(nr, nc), in_specs=in_specs,
                                               out_specs=BS((None, tr, tc), lambda a, b, si, sc: (l, sc[0] * nr + a, b))),
        out_shape=S((L, sh.Rs, sh.Cs), _F32), input_output_aliases={} if g_prev is None else {4: 0},
        compiler_params=_cp("parallel", "parallel"),
    )(*args)


def _rs_pair_share(gs, ls, shs, name):
    n = len(gs)

    def body(*refs):
        bufs = refs[n:2 * n]
        send_sems, recv_sems = refs[2 * n:]
        x, y, c, *_ = _where_am_i()
        sib = (x, y, 1 - c)
        cps = []
        for t in range(n):
            mine = shs[t].shard_half(bufs[t].at[ls[t]], c)
            cps.append(_remote(mine, mine, send_sems.at[t], recv_sems.at[t], sib))
        for cp in cps:
            cp.start()
        for t in range(n):
            other = shs[t].shard_half(bufs[t].at[ls[t]], 1 - c)
            _remote(other, other, send_sems.at[t], recv_sems.at[t], sib).wait_recv()
        for cp in cps:
            cp.wait_send()

    return _call(
        body, name, in_specs=[_ANY] * n, out_specs=[_ANY] * n, out_shape=[S(g.shape, g.dtype) for g in gs],
        input_output_aliases={t: t for t in range(n)},
        scratch_shapes=[pltpu.SemaphoreType.DMA((n,)), pltpu.SemaphoreType.DMA((n,))],
        compiler_params=pltpu.CompilerParams(has_side_effects=True),
    )(*gs)


def _all_reduce_small(xs):
    M = xs.shape[0]

    def body(x_ref, tot_ref, out_ref, send_sems, recv_sems, local_sem):
        x, y, c, me, chips, chip_ids = _where_am_i()
        sib = (x, y, 1 - c)

        def rows(dev):
            return out_ref.at[pl.ds(_mo((4 * dev[0] + 2 * dev[1] + dev[2]) * M, 8), M), :]

        def copy(k, block, to, src=None):
            return _remote(rows(block) if src is None else src, rows(block), send_sems.at[k], recv_sems.at[k], to)

        mine = pltpu.make_async_copy(x_ref, rows((x, y, c)), local_sem)
        mine.start()
        first = [copy(0, (x, y, c), sib, src=x_ref)]
        first += [copy(1 + j, (x, y, c), (*chip, c), src=x_ref) for j, chip in enumerate(chips)]
        for cp in first:
            cp.start()
        passed = [copy(4 + j, (*chip, c), sib) for j, chip in enumerate(chips)]
        for j, chip in enumerate(chips):
            copy(1 + j, (*chip, c), (x, y, c)).wait_recv()
            passed[j].start()
        copy(0, sib, (x, y, c)).wait_recv()
        for j, chip in enumerate(chips):
            copy(4 + j, (*chip, 1 - c), (x, y, c)).wait_recv()
        for cp in first + passed:
            cp.wait_send()
        mine.wait()
        tot = out_ref[pl.ds(0, M), :]
        for d in range(1, 8):
            tot = tot + out_ref[pl.ds(d * M, M), :]
        tot_ref[...] = tot

    vm = pl.BlockSpec(memory_space=pltpu.VMEM)
    return _call(
        body, "all_reduce_small", in_specs=[vm], out_specs=[vm, vm], out_shape=[S((M, 128), _F32), S((8 * M, 128), _F32)],
        scratch_shapes=[pltpu.SemaphoreType.DMA((7,)), pltpu.SemaphoreType.DMA((7,)), pltpu.SemaphoreType.DMA],
        compiler_params=_cp(has_side_effects=True),
    )(xs)[0]


def _reduce_scatter_begin(dws, shs, idx, l):
    recv_a = _rs_pair_exchange([d[1] for d in dws], shs, "rs_pair_exchange")
    p = [_rs_pair_add(d[0], ra, sh, idx) for d, ra, sh in zip(dws, recv_a, shs)]
    ssem, rsem, pws, lands, token = _rs_chip_start([pw for _, pw in p], shs, f"rs_chip_start_{l}")
    return ([p32 for p32, _ in p], pws, lands, ssem, rsem), token


def _reduce_scatter_end(state, after, tensors, shs, gstack, idx, l):
    p32s, pws, lands, ssem, rsem = state
    recv_b = _rs_chip_wait(pws, lands, ssem, rsem, after, shs, f"rs_chip_wait_{l}")
    gs = [_rs_chip_add(p32, rb, sh, idx, gstack[name], i, L) for p32, rb, sh, (name, i, L) in zip(p32s, recv_b, shs, tensors)]
    gs = _rs_pair_share(gs, [i for _, i, _ in tensors], shs, "rs_pair_share")
    for (name, _, _), g in zip(tensors, gs):
        gstack[name] = g


def _pack(parts):
    out = []
    for p in parts:
        p2 = p.reshape(-1, 128)
        pad = (-p2.shape[0]) % 8
        out.append(jnp.pad(p2, ((0, pad), (0, 0))) if pad else p2)
    return jnp.concatenate(out, axis=0)


def _unpack(packed, like):
    out, at = [], 0
    for p in like:
        n = p.size // 128
        out.append(packed[at:at + n].reshape(p.shape))
        at += n + ((-n) % 8)
    return out


def kernel(x, mem, g_mix, g_ffn, w_in_a, g_v_a, w_spatial, b_spatial, w_in_b, g_q_b, g_k_b, g_mem, w_mem_kv, g_mq, g_mk, w_out, w_gate_up, w_down, loss_target, m_g_mix, m_g_ffn, m_w_in_a, m_g_v_a, m_w_spatial, m_b_spatial, m_w_in_b, m_g_q_b, m_g_k_b, m_g_mem, m_w_mem_kv, m_g_mq, m_g_mk, m_w_out, m_w_gate_up, m_w_down, v_g_mix, v_g_ffn, v_w_in_a, v_g_v_a, v_w_spatial, v_b_spatial, v_w_in_b, v_g_q_b, v_g_k_b, v_g_mem, v_w_mem_kv, v_g_mq, v_g_mk, v_w_out, v_w_gate_up, v_w_down):
    xs = x[0]
    mem2 = mem[0]
    target = loss_target[0]
    T, D = xs.shape
    depth = g_mix.shape[0]
    MEMW = w_mem_kv.shape[2] // 2
    TOK = D - MEMW
    KV = (w_in_b.shape[2] * 4 - TOK - MEMW) // 2
    QPK = TOK // KV
    F = w_gate_up.shape[2] * 4 // 2

    idx = ((2 * lax.axis_index("x") + lax.axis_index("y")).astype(jnp.int32).reshape(1), lax.axis_index("c").astype(jnp.int32).reshape(1))

    big = {
        "w_in_a": (w_in_a, _Shard(D, w_in_a.shape[2] * 4, True)),
        "w_in_b": (w_in_b, _Shard(D, w_in_b.shape[2] * 4, True)),
        "w_mem_kv": (w_mem_kv, _Shard(D, 2 * MEMW, False)),
        "w_out": (w_out, _Shard(D, D, False)),
        "w_gate_up": (w_gate_up, _Shard(D, 2 * F, True)),
        "w_down": (w_down, _Shard(F, D, False)),
    }

    def layer_tensors(l):
        n_in = "w_in_a" if l % 2 == 0 else "w_in_b"
        return [(n_in, l // 2, big[n_in][0].shape[0])] + [(n, l, depth) for n in ("w_mem_kv", "w_out", "w_gate_up", "w_down")]

    def layer_shards(l):
        return [big[n][1] for n, _, _ in layer_tensors(l)]

    full = {n: [None] * w.shape[0] for n, (w, _) in big.items()}
    inflight = _gather_start([_cast_into_full(big[n][0], i, big[n][1], idx) for n, i, _ in layer_tensors(0)], layer_shards(0),
                             "gather_start_0")
    cast = {l: [_cast_into_full(big[n][0], i, big[n][1], idx, dep=inflight[3]) for n, i, _ in layer_tensors(l)]
            for l in range(1, depth)}
    tabs = _rope_tables(T)

    saved = []
    xc = xs
    after = cast[depth - 1][-1] if depth > 1 else inflight[3]
    for l in range(depth):
        is_a = l % 2 == 0
        li = l // 2
        ssem, rsem, bufs, _ = inflight
        bufs = _gather_wait(bufs, ssem, rsem, after, layer_shards(l), f"gather_wait_{l}")
        bufs = _gather_pass_on(bufs, layer_shards(l), "gather_pass_on")
        for (n, i, _), b in zip(layer_tensors(l), bufs):
            full[n][i] = b
        token = None
        if l + 1 < depth:
            inflight = _gather_start(cast[l + 1], layer_shards(l + 1), f"gather_start_{l + 1}")
            token = inflight[3]
        h = _rmsnorm_fwd(xc, g_mix[l], "rmsnorm_fwd", dep=token)
        w_in = full["w_in_a" if is_a else "w_in_b"][li]
        z = _mm_nn("mm_in", h, w_in, pm=2048, pn=512)
        st = dict(x=xc, h=h, z=z)
        if is_a:
            ws_m = w_spatial[li].astype(_MXU)
            st["ws_m"], st["wst_m"], st["b_t"] = ws_m, jnp.swapaxes(ws_m, 1, 2), b_spatial[li].T
            tok = _mixer_a_fwd(z, g_v_a[li], ws_m, st["b_t"], TOK)
            qblk = 2 * TOK // MEMW
        else:
            q, k, v = _qk_rope_fwd(z, g_q_b[li], g_k_b[li], tabs, TOK, KV)
            tok, stat = _attn_fwd(q, k, v, QPK)
            st["q"], st["k"], st["v"], st["stat"] = q, k, v, stat
            qblk = (TOK + 2 * KV) // MEMW
        mem_n = _rmsnorm_fwd(mem2, g_mem[l], "rmsnorm_mem")
        kv = _mm_nn("mm_memkv", mem_n, full["w_mem_kv"][l])
        mo = _mem_fwd(z, qblk, kv, g_mq[l], g_mk[l], MEMW)
        cat = jnp.concatenate([tok, mo], axis=1)
        x1 = _mm_nn("mm_out", cat, full["w_out"][l], add=xc)
        h2 = _rmsnorm_fwd(x1, g_ffn[l], "rmsnorm_fwd")
        act, gu = _ffn_gate_up(h2, full["w_gate_up"][l])
        xc = _mm_nn("mm_down", act, full["w_down"][l], add=x1)
        after = xc
        st.update(mem_n=mem_n, kv=kv, qblk=qblk, cat=cat, x1=x1, h2=h2, act=act, gu=gu)
        saved.append(st)

    dx, dxm, sq = _loss_head(xc, target)
    loss = lax.psum(sq[0, 0] * (0.5 / D), ("x", "y", "c"))

    gsm = {n: [None] * len(a) for n, a in dict(g_mix=g_mix, g_ffn=g_ffn, g_v_a=g_v_a, w_spatial=w_spatial, b_spatial=b_spatial,
                                                g_q_b=g_q_b, g_k_b=g_k_b, g_mem=g_mem, g_mq=g_mq, g_mk=g_mk).items()}
    gstack = {n: None for n in big}
    pending, token = None, None
    for l in reversed(range(depth)):
        st = saved[l]
        is_a = l % 2 == 0
        li = l // 2
        gbig = {}
        dgu = _ffn_dact(dxm, full["w_down"][l], st["gu"], dep=token)
        gbig["w_down"] = _mm_tn_dual("mm_dw_down", st["act"], dxm, pm=1408)
        dh2 = _ffn_dh(dgu, full["w_gate_up"][l])
        gbig["w_gate_up"] = _ffn_dwgu(st["h2"], dgu)
        dx, dxm, dg = _rmsnorm_bwd(st["x1"], g_ffn[l], dh2, dx, "rmsnorm_bwd")
        gsm["g_ffn"][l] = dg[0]
        dcat = _mm_nt("mm_dcat", dxm, full["w_out"][l])
        gbig["w_out"] = _mm_tn_dual("mm_dw_out", st["cat"], dxm)
        dzq, dkn, dvm, dgq = _mem_bwd(st["z"], st["qblk"], st["kv"], g_mq[l], g_mk[l], dcat, TOK // MEMW, MEMW)
        dkv, dgk = _memkv_bwd(st["kv"], dkn, dvm, g_mk[l], MEMW)
        gsm["g_mq"][l], gsm["g_mk"][l] = dgq[0], dgk[0]
        gbig["w_mem_kv"] = _mm_tn_dual("mm_dw_memkv", st["mem_n"], dkv)
        dmem_n = _mm_nt("mm_dmemn", dkv, full["w_mem_kv"][l])
        gsm["g_mem"][l] = _rmsnorm_bwd(mem2, g_mem[l], dmem_n, None, "rmsnorm_bwd_mem")[2][0]
        if is_a:
            dz_tok, dws, dbs, dgv = _mixer_a_bwd(st["z"], dcat, g_v_a[li], st["ws_m"], st["wst_m"], st["b_t"], TOK)
            gsm["w_spatial"][li], gsm["b_spatial"][li], gsm["g_v_a"][li] = dws, dbs[:, :, 0], dgv[0]
            dz = jnp.concatenate([dz_tok, dzq], axis=1)
        else:
            dq, dk, dv = _attn_bwd(st["q"], st["k"], st["v"], dcat, st["cat"], st["stat"], QPK)
            dz_qk, dgq_b, dgk_b = _qk_rope_bwd(st["z"], dq, dk, g_q_b[li], g_k_b[li], tabs, TOK, KV)
            gsm["g_q_b"][li], gsm["g_k_b"][li] = dgq_b[0], dgk_b[0]
            dz = jnp.concatenate([dz_qk, dv, dzq], axis=1)
        n_in = "w_in_a" if is_a else "w_in_b"
        dh = _mm_nt("mm_dh", dz, full[n_in][li])
        gbig[n_in] = _mm_tn_dual("mm_dw_in", st["h"], dz, pm=2048, pn=512)
        dx, dxm, dg = _rmsnorm_bwd(st["x"], g_mix[l], dh, dx, "rmsnorm_bwd")
        gsm["g_mix"][l] = dg[0]
        began = _reduce_scatter_begin([gbig[n] for n, _, _ in layer_tensors(l)], layer_shards(l), idx, l)
        if pending is not None:
            _reduce_scatter_end(pending[0], began[1], layer_tensors(pending[1]), layer_shards(pending[1]), gstack, idx, pending[1])
        pending, token = (began[0], l), began[1]
    _reduce_scatter_end(pending[0], dx, layer_tensors(pending[1]), layer_shards(pending[1]), gstack, idx, pending[1])

    grads = gstack
    small =["g_mix", "g_ffn", "g_v_a", "w_spatial", "b_spatial", "g_q_b", "g_k_b", "g_mem", "g_mq", "g_mk"]
    env = dict(g_mix=g_mix, g_ffn=g_ffn, g_v_a=g_v_a, w_spatial=w_spatial, b_spatial=b_spatial, g_q_b=g_q_b, g_k_b=g_k_b,
               g_mem=g_mem, g_mq=g_mq, g_mk=g_mk,
               m_g_mix=m_g_mix, m_g_ffn=m_g_ffn, m_g_v_a=m_g_v_a, m_w_spatial=m_w_spatial, m_b_spatial=m_b_spatial,
               m_g_q_b=m_g_q_b, m_g_k_b=m_g_k_b, m_g_mem=m_g_mem, m_g_mq=m_g_mq, m_g_mk=m_g_mk,
               v_g_mix=v_g_mix, v_g_ffn=v_g_ffn, v_g_v_a=v_g_v_a, v_w_spatial=v_w_spatial, v_b_spatial=v_b_spatial,
               v_g_q_b=v_g_q_b, v_g_k_b=v_g_k_b, v_g_mem=v_g_mem, v_g_mq=v_g_mq, v_g_mk=v_g_mk,
               m_w_in_a=m_w_in_a, m_w_in_b=m_w_in_b, m_w_mem_kv=m_w_mem_kv, m_w_out=m_w_out, m_w_gate_up=m_w_gate_up, m_w_down=m_w_down,
               v_w_in_a=v_w_in_a, v_w_in_b=v_w_in_b, v_w_mem_kv=v_w_mem_kv, v_w_out=v_w_out, v_w_gate_up=v_w_gate_up, v_w_down=v_w_down)
    like = [env[n] for n in small]
    g_small = _all_reduce_small(_pack([jnp.stack(gsm[n]) for n in small]))

    res = {}
    outs = _adamw(_pack(like), g_small, _pack([env["m_" + n] for n in small]), _pack([env["v_" + n] for n in small]), "adamw_small")
    unpacked = [_unpack(o, like) for o in outs]
    for k, n in enumerate(small):
        res[n] = [u[k] for u in unpacked]
    for n, (w, _) in big.items():
        res[n] = _adamw(w, grads[n], env["m_" + n], env["v_" + n], "adamw_" + n)

    order = ["g_mix", "g_ffn", "w_in_a", "g_v_a", "w_spatial", "b_spatial", "w_in_b", "g_q_b", "g_k_b", "g_mem", "w_mem_kv",
             "g_mq", "g_mk", "w_out", "w_gate_up", "w_down"]
    return (loss, dx.reshape(1, T, D), *[res[n][0] for n in order], *[res[n][1] for n in order],
            *[res[n][2] for n in order], *[res[n][3] for n in order])
```

```python
import jax
import jax.numpy as jnp
import numpy as np
from jax import lax
from jax.experimental import pallas as pl
from jax.experimental.pallas import tpu as pltpu

_F32 = jnp.float32
_MXU = jnp.bfloat16
_WIRE = jnp.bfloat16
_KW = {}

EPS = 1e-6
HEAD = 128
CHUNK = 128
GRID_W = 64
ROPE_THETA = 10000.0
ADAM_LR, ADAM_B1, ADAM_B2, ADAM_EPS, ADAM_WD, ADAM_STEP = 0.001, 0.9, 0.999, 1e-08, 0.01, 10
_SQRT_HALF = float(np.sqrt(0.5))
_INV_SQRT_2PI = float(1.0 / np.sqrt(2.0 * np.pi))
_VMEM_LIMIT = 56 * 1024 * 1024
_MESH = pl.DeviceIdType.MESH

_NN = (((1,), (0,)), ((), ()))
_NT = (((1,), (1,)), ((), ()))
_TN = (((0,), (0,)), ((), ()))

S = jax.ShapeDtypeStruct
BS = pl.BlockSpec
_ANY = pl.BlockSpec(memory_space=pl.ANY)


def _tile(n, pref, mult=128):
    if n <= pref:
        return n
    d = (pref // mult) * mult
    while d >= mult:
        if n % d == 0:
            return d
        d -= mult
    raise ValueError(f"no tile for {n} (pref {pref}, mult {mult})")


def _mo(v, m):
    return v if isinstance(v, int) else pl.multiple_of(v, m)


def _cp(*sem, **kw):
    return pltpu.CompilerParams(dimension_semantics=sem or None, vmem_limit_bytes=_VMEM_LIMIT, **kw)


def _call(body, name, **kw):
    return pl.pallas_call(body, name=name, **kw, **_KW)


def _dot(a, b, dn=_NN):
    return lax.dot_general(a, b, dn, preferred_element_type=_F32)


def _gelu(x):
    return 0.5 * x * (1.0 + lax.erf(x * _SQRT_HALF))


def _gelu_grad(x):
    return 0.5 * (1.0 + lax.erf(x * _SQRT_HALF)) + x * jnp.exp(-0.5 * x * x) * _INV_SQRT_2PI


def _rstd(x):
    return lax.rsqrt(jnp.mean(x * x, axis=-1, keepdims=True) + EPS)


def _norm_bwd(dout, xhat, r, g):
    dy = dout * g
    return r * (dy - xhat * jnp.mean(dy * xhat, axis=-1, keepdims=True))


def _softmax(s):
    e = jnp.exp(s - jnp.max(s, axis=-1, keepdims=True))
    return e * (1.0 / jnp.sum(e, axis=-1, keepdims=True))


def _mm(name, a, b, a_spec, b_spec, dn, grid, acc_shape, out_shape, out_specs, epilogue, extra=(), extra_specs=()):
    nk = grid[2]
    n_ex = len(extra)
    multi = isinstance(out_shape, (list, tuple))
    n_out = len(out_shape) if multi else 1

    def body(*refs):
        a_ref, b_ref = refs[0], refs[1]
        ex = refs[2:2 + n_ex]
        outs = refs[2 + n_ex:2 + n_ex + n_out]

        def prod():
            return _dot(a_ref[...].astype(_MXU), b_ref[...].astype(_MXU), dn)

        if nk == 1:
            epilogue(prod(), ex, outs)
        else:
            acc = refs[-1]
            k = pl.program_id(2)

            @pl.when(k == 0)
            def _():
                acc[...] = jnp.zeros_like(acc)

            acc[...] += prod()

            @pl.when(k == nk - 1)
            def _():
                epilogue(acc[...], ex, outs)

    return _call(
        body, name, grid=grid, in_specs=[a_spec, b_spec, *extra_specs], out_specs=out_specs, out_shape=out_shape,
        scratch_shapes=[] if nk == 1 else [pltpu.VMEM(acc_shape, _F32)],
        compiler_params=_cp("parallel", "parallel", "arbitrary"),
    )(a, b, *extra)


def _ep_store(acc, ex, outs):
    for o in outs:
        o[...] = acc.astype(o.dtype)


def _ep_add(acc, ex, outs):
    outs[0][...] = (acc + ex[0][...]).astype(outs[0].dtype)


def _mm_nn(name, a, b, out_dtype=_F32, add=None, pm=1024, pn=1024, pk=2048):
    M, K = a.shape
    N = b.shape[1]
    tm, tn, tk = _tile(M, pm, 8), _tile(N, pn), _tile(K, pk)
    o_spec = BS((tm, tn), lambda i, j, k: (i, j))
    return _mm(name, a, b, BS((tm, tk), lambda i, j, k: (i, k)), BS((tk, tn), lambda i, j, k: (k, j)), _NN,
               (M // tm, N // tn, K // tk), (tm, tn), S((M, N), out_dtype), o_spec,
               _ep_store if add is None else _ep_add,
               extra=() if add is None else (add,), extra_specs=() if add is None else (o_spec,))


def _mm_nt(name, a, b, out_dtype=_F32, pm=1024, pn=1024, pk=4096):
    M, K = a.shape
    N = b.shape[0]
    tm, tn, tk = _tile(M, pm, 8), _tile(N, pn), _tile(K, pk)
    return _mm(name, a, b, BS((tm, tk), lambda i, j, k: (i, k)), BS((tn, tk), lambda i, j, k: (j, k)), _NT,
               (M // tm, N // tn, K // tk), (tm, tn), S((M, N), out_dtype), BS((tm, tn), lambda i, j, k: (i, j)), _ep_store)


def _mm_tn_dual(name, a, b, pm=1024, pn=1024, pk=2048):
    K, M = a.shape
    N = b.shape[1]
    tm, tn, tk = _tile(M, pm), _tile(N, pn), _tile(K, pk, 16)
    o_spec = BS((tm, tn), lambda i, j, k: (i, j))
    return _mm(name, a, b, BS((tk, tm), lambda i, j, k: (k, i)), BS((tk, tn), lambda i, j, k: (k, j)), _TN,
               (M // tm, N // tn, K // tk), (tm, tn), [S((M, N), _F32), S((M, N), _WIRE)], [o_spec, o_spec], _ep_store)


def _ffn_gate_up(h2, wgu):
    T, D = h2.shape
    F = wgu.shape[1] // 2
    tm, tn = _tile(T, 1024, 8), _tile(F, 512)
    nj = F // tn

    def body(a_ref, bg_ref, bu_ref, act_ref, gu_ref):
        a = a_ref[...]
        g = _dot(a, bg_ref[...])
        u = _dot(a, bu_ref[...])
        gu_ref[0] = g.astype(gu_ref.dtype)
        gu_ref[1] = u.astype(gu_ref.dtype)
        act_ref[...] = (g * (1.0 / (1.0 + jnp.exp(-g))) * u).astype(act_ref.dtype)

    return _call(
        body, "ffn_gate_up", grid=(T // tm, nj),
        in_specs=[BS((tm, D), lambda i, j: (i, 0)), BS((D, tn), lambda i, j: (0, j)), BS((D, tn), lambda i, j: (0, j + nj))],
        out_specs=[BS((tm, tn), lambda i, j: (i, j)), BS((2, tm, tn), lambda i, j: (0, i, j))],
        out_shape=[S((T, F), _MXU), S((2, T, F), _MXU)],
        compiler_params=_cp("parallel", "parallel"),
    )(h2, wgu, wgu)


def _ffn_dact(dxm, wdown, gu, dep=None):
    T, D = dxm.shape
    F = wdown.shape[0]
    tm, tn = _tile(T, 1024, 8), _tile(F, 512)
    deps = [] if dep is None else [dep]

    def body(a_ref, b_ref, gu_ref, *rest):
        o_ref = rest[-1]
        d = _dot(a_ref[...], b_ref[...], _NT)
        g, u = gu_ref[0].astype(_F32), gu_ref[1].astype(_F32)
        sg = 1.0 / (1.0 + jnp.exp(-g))
        o_ref[0] = (d * u * (sg * (1.0 + g * (1.0 - sg)))).astype(o_ref.dtype)
        o_ref[1] = (d * (g * sg)).astype(o_ref.dtype)

    return _call(
        body, "ffn_dact", grid=(T // tm, F // tn),
        in_specs=[BS((tm, D), lambda i, j: (i, 0)), BS((tn, D), lambda i, j: (j, 0)), BS((2, tm, tn), lambda i, j: (0, i, j))]
        + [_ANY] * len(deps),
        out_specs=BS((2, tm, tn), lambda i, j: (0, i, j)), out_shape=S((2, T, F), _MXU),
        compiler_params=_cp("parallel", "parallel"),
    )(dxm, wdown, gu, *deps)


def _ffn_dh(dgu, wgu):
    _, T, F = dgu.shape
    D = wgu.shape[0]
    tm, tn, tk = _tile(T, 1024, 8), _tile(D, 2048), _tile(F, 2048)
    nkf = F // tk
    return _mm("ffn_dh", dgu, wgu, BS((None, tm, tk), lambda i, j, k: (k // nkf, i, k % nkf)),
               BS((tn, tk), lambda i, j, k: (j, k)), _NT, (T // tm, D // tn, 2 * nkf), (tm, tn),
               S((T, D), _F32), BS((tm, tn), lambda i, j, k: (i, j)), _ep_store)


def _ffn_dwgu(h2, dgu):
    _, T, F = dgu.shape
    D = h2.shape[1]
    tm, tn, tk = _tile(D, 1024), _tile(F, 1408), _tile(T, 2048, 16)
    njf = F // tn
    o_spec = BS((tm, tn), lambda i, j, k: (i, j))
    return _mm("ffn_dwgu", h2, dgu, BS((tk, tm), lambda i, j, k: (k, i)),
               BS((None, tk, tn), lambda i, j, k: (j // njf, k, j % njf)), _TN, (D // tm, 2 * njf, T // tk), (tm, tn),
               [S((D, 2 * F), _F32), S((D, 2 * F), _WIRE)], [o_spec, o_spec], _ep_store)


def _rmsnorm_fwd(x, g, name, dep=None):
    T, D = x.shape
    tr = _tile(T, 512, 8)

    def body(x_ref, g_ref, *rest):
        xv = x_ref[...]
        rest[-1][...] = (xv * _rstd(xv) * g_ref[...]).astype(rest[-1].dtype)

    row = BS((tr, D), lambda i: (i, 0))
    deps = [] if dep is None else [dep]
    return _call(body, name, grid=(T // tr,), in_specs=[row, BS((1, D), lambda i: (0, 0))] + [_ANY] * len(deps), out_specs=row,
                 out_shape=S((T, D), _MXU), compiler_params=_cp("parallel"))(x, g.reshape(1, D), *deps)


def _rmsnorm_bwd(x, g, dh, dres, name):
    T, D = x.shape
    tr = _tile(T, 256, 8)
    has_res = dres is not None

    def body(*refs):
        x_ref, g_ref, dh_ref = refs[:3]
        dx_ref, dxm_ref, dg_ref = refs[-3:]

        @pl.when(pl.program_id(0) == 0)
        def _():
            dg_ref[...] = jnp.zeros_like(dg_ref)

        xv = x_ref[...]
        r = _rstd(xv)
        xhat = xv * r
        dh_v = dh_ref[...]
        dg_ref[...] += jnp.sum(dh_v * xhat, axis=0, keepdims=True)
        dx = _norm_bwd(dh_v, xhat, r, g_ref[...])
        if has_res:
            dx = dx + refs[3][...]
        dx_ref[...] = dx
        dxm_ref[...] = dx.astype(dxm_ref.dtype)

    row = BS((tr, D), lambda i: (i, 0))
    vec = BS((1, D), lambda i: (0, 0))
    return _call(body, name, grid=(T // tr,), in_specs=[row, vec, row] + ([row] if has_res else []),
                 out_specs=[row, row, vec], out_shape=[S((T, D), _F32), S((T, D), _MXU), S((1, D), _F32)],
                 compiler_params=_cp("arbitrary"))(x, g.reshape(1, D), dh, *([dres] if has_res else []))


def _loss_head(y, target):
    T, D = y.shape
    tr = _tile(T, 256, 8)

    def body(y_ref, t_ref, dy_ref, dym_ref, acc_ref):
        @pl.when(pl.program_id(0) == 0)
        def _():
            acc_ref[...] = jnp.zeros_like(acc_ref)

        err = y_ref[...] - t_ref[...]
        acc_ref[...] += jnp.sum(jnp.sum(err * err, axis=-1, keepdims=True), axis=0, keepdims=True)
        dy = err * (1.0 / D)
        dy_ref[...] = dy
        dym_ref[...] = dy.astype(dym_ref.dtype)

    row = BS((tr, D), lambda i: (i, 0))
    return _call(body, "loss_head", grid=(T // tr,), in_specs=[row, row],
                 out_specs=[row, row, BS((1, 128), lambda i: (0, 0))],
                 out_shape=[S((T, D), _F32), S((T, D), _MXU), S((1, 128), _F32)],
                 compiler_params=_cp("arbitrary"))(y, target)


def _mixa_blocks(T):
    return 2 if T % (2 * CHUNK) == 0 else 1


def _mixer_a_fwd(z, gv, ws_m, b_t, TOK):
    T = z.shape[0]
    G = TOK // HEAD
    CB = _mixa_blocks(T)
    R = CB * CHUNK

    def body(z_ref, gv_ref, ws_ref, bt_ref, o_ref):
        u = _gelu(z_ref[:, :TOK])
        v = _gelu(z_ref[:, TOK:])
        vn = (v * _rstd(v) * gv_ref[...]).astype(_MXU)
        for c in range(CB):
            rows = slice(c * CHUNK, (c + 1) * CHUNK)
            for g in range(G):
                cols = slice(g * HEAD, (g + 1) * HEAD)
                s = _dot(ws_ref[g], vn[rows, cols]) + bt_ref[:, g:g + 1]
                o_ref[rows, cols] = (u[rows, cols] * s).astype(o_ref.dtype)

    return _call(
        body, "mixer_a_fwd", grid=(T // R,),
        in_specs=[BS((R, 2 * TOK), lambda i: (i, 0)), BS((1, TOK), lambda i: (0, 0)),
                  BS((G, CHUNK, CHUNK), lambda i: (0, 0, 0)), BS((CHUNK, G), lambda i: (0, 0))],
        out_specs=BS((R, TOK), lambda i: (i, 0)), out_shape=S((T, TOK), _MXU), compiler_params=_cp("parallel"),
    )(z, gv.reshape(1, TOK), ws_m, b_t)


def _mixer_a_bwd(z, dcat, gv, ws_m, wst_m, b_t, TOK):
    T = z.shape[0]
    G = TOK // HEAD
    CB = _mixa_blocks(T)
    R = CB * CHUNK
    n = T // R

    def body(z_ref, d_ref, gv_ref, ws_ref, wst_ref, bt_ref, dz_ref, dws_ref, db_ref, dgv_ref, dvn_scr):
        i = pl.program_id(0)

        @pl.when(i == 0)
        def _():
            dws_ref[...] = jnp.zeros_like(dws_ref)
            db_ref[...] = jnp.zeros_like(db_ref)
            dgv_ref[...] = jnp.zeros_like(dgv_ref)

        zu = z_ref[:, :TOK]
        zv = z_ref[:, TOK:]
        u = _gelu(zu)
        v = _gelu(zv)
        r = _rstd(v)
        vhat = v * r
        gvv = gv_ref[...]
        vn = (vhat * gvv).astype(_MXU)
        d = d_ref[...]
        gpu = _gelu_grad(zu)
        for c in range(CB):
            rows = slice(c * CHUNK, (c + 1) * CHUNK)
            for g in range(G):
                cols = slice(g * HEAD, (g + 1) * HEAD)
                vn_cg = vn[rows, cols]
                s = _dot(ws_ref[g], vn_cg) + bt_ref[:, g:g + 1]
                d_cg = d[rows, cols]
                dz_ref[rows, cols] = (d_cg * s * gpu[rows, cols]).astype(dz_ref.dtype)
                ds = d_cg * u[rows, cols]
                ds_m = ds.astype(_MXU)
                dvn_scr[rows, cols] = _dot(wst_ref[g], ds_m)
                dws_ref[g] += _dot(ds_m, vn_cg, _NT)
                db_ref[g] += ds
        dvn = dvn_scr[...]
        dgv_ref[...] += jnp.sum(dvn * vhat, axis=0, keepdims=True)
        dv = _norm_bwd(dvn, vhat, r, gvv)
        dz_ref[:, TOK:] = (dv * _gelu_grad(zv)).astype(dz_ref.dtype)

        @pl.when(i == n - 1)
        def _():
            for g in range(G):
                db_ref[g] = jnp.broadcast_to(jnp.sum(db_ref[g], axis=1, keepdims=True), (CHUNK, CHUNK))

    full3 = BS((G, CHUNK, CHUNK), lambda i: (0, 0, 0))
    return _call(
        body, "mixer_a_bwd", grid=(n,),
        in_specs=[BS((R, 2 * TOK), lambda i: (i, 0)), BS((R, TOK), lambda i: (i, 0)), BS((1, TOK), lambda i: (0, 0)),
                  full3, full3, BS((CHUNK, G), lambda i: (0, 0))],
        out_specs=[BS((R, 2 * TOK), lambda i: (i, 0)), full3, full3, BS((1, TOK), lambda i: (0, 0))],
        out_shape=[S((T, 2 * TOK), _MXU), S((G, CHUNK, CHUNK), _F32), S((G, CHUNK, CHUNK), _F32), S((1, TOK), _F32)],
        scratch_shapes=[pltpu.VMEM((R, TOK), _F32)], compiler_params=_cp("arbitrary"),
    )(z, dcat, gv.reshape(1, TOK), ws_m, wst_m, b_t)


def _rope_tables(T):
    n_rows = T // GRID_W
    rows = jnp.broadcast_to(jnp.arange(n_rows)[:, None], (n_rows, GRID_W)).reshape(T)
    cols = jnp.broadcast_to(jnp.arange(GRID_W)[None, :], (n_rows, GRID_W)).reshape(T)
    pairs = HEAD // 4
    freqs = ROPE_THETA ** (-jnp.arange(pairs, dtype=_F32) / pairs)
    ang_r = rows.astype(_F32)[:, None] * freqs
    ang_c = cols.astype(_F32)[:, None] * freqs
    ang = jnp.concatenate([ang_r, ang_r, ang_c, ang_c], axis=-1)
    cos, sin = jnp.cos(ang), jnp.sin(ang)
    first = (jnp.arange(HEAD) % (HEAD // 2)) < (HEAD // 4)
    return cos, jnp.where(first, -sin, 0.0), jnp.where(first, 0.0, sin)


def _rope(x, cs, sa, sb):
    return x * cs + pltpu.roll(x, 96, 1) * sa + pltpu.roll(x, 32, 1) * sb


def _qk_rope_fwd(z, gq, gk, tabs, TOK, KV):
    T = z.shape[0]
    R = _tile(T, 512, 8)
    W = TOK + 2 * KV

    def body(z_ref, gq_ref, gk_ref, cos_ref, sa_ref, sb_ref, q_ref, k_ref, v_ref):
        cs, sa, sb = cos_ref[...], sa_ref[...], sb_ref[...]
        for h in range((TOK + KV) // HEAD):
            cols = slice(h * HEAD, (h + 1) * HEAD)
            xv = z_ref[:, cols]
            xn = xv * _rstd(xv) * (gq_ref[...] if h < TOK // HEAD else gk_ref[...])
            out = _rope(xn, cs, sa, sb)
            if h < TOK // HEAD:
                q_ref[:, cols] = out.astype(q_ref.dtype)
            else:
                k_ref[:, h * HEAD - TOK:(h + 1) * HEAD - TOK] = out.astype(k_ref.dtype)
        v_ref[...] = z_ref[:, TOK + KV:].astype(v_ref.dtype)

    vec = BS((1, HEAD), lambda i: (0, 0))
    tab = BS((R, HEAD), lambda i: (i, 0))
    return _call(
        body, "qk_rope_fwd", grid=(T // R,), in_specs=[BS((R, W), lambda i: (i, 0)), vec, vec, tab, tab, tab],
        out_specs=[BS((R, TOK), lambda i: (i, 0)), BS((R, KV), lambda i: (i, 0)), BS((R, KV), lambda i: (i, 0))],
        out_shape=[S((T, TOK), _MXU), S((T, KV), _MXU), S((T, KV), _MXU)], compiler_params=_cp("parallel"),
    )(z, gq.reshape(1, HEAD), gk.reshape(1, HEAD), *tabs)


def _qk_rope_bwd(z, dq, dk, gq, gk, tabs, TOK, KV):
    T = z.shape[0]
    R = _tile(T, 512, 8)
    W = TOK + KV

    def body(z_ref, dq_ref, dk_ref, gq_ref, gk_ref, cos_ref, sa_ref, sb_ref, dz_ref, dgq_ref, dgk_ref):
        @pl.when(pl.program_id(0) == 0)
        def _():
            dgq_ref[...] = jnp.zeros_like(dgq_ref)
            dgk_ref[...] = jnp.zeros_like(dgk_ref)

        cs, sa, sb = cos_ref[...], sa_ref[...], sb_ref[...]
        for h in range(W // HEAD):
            cols = slice(h * HEAD, (h + 1) * HEAD)
            is_q = h < TOK // HEAD
            do = dq_ref[:, cols] if is_q else dk_ref[:, h * HEAD - TOK:(h + 1) * HEAD - TOK]
            dxn = do * cs - pltpu.roll(do, 96, 1) * sa - pltpu.roll(do, 32, 1) * sb
            xv = z_ref[:, cols]
            r = _rstd(xv)
            xhat = xv * r
            dg_ref = dgq_ref if is_q else dgk_ref
            dg_ref[...] += jnp.sum(dxn * xhat, axis=0, keepdims=True)
            dz_ref[:, cols] = _norm_bwd(dxn, xhat, r, gq_ref[...] if is_q else gk_ref[...]).astype(dz_ref.dtype)

    vec = BS((1, HEAD), lambda i: (0, 0))
    tab = BS((R, HEAD), lambda i: (i, 0))
    return _call(
        body, "qk_rope_bwd", grid=(T // R,),
        in_specs=[BS((R, W), lambda i: (i, 0)), BS((R, TOK), lambda i: (i, 0)), BS((R, KV), lambda i: (i, 0)), vec, vec, tab, tab, tab],
        out_specs=[BS((R, W), lambda i: (i, 0)), vec, vec],
        out_shape=[S((T, W), _MXU), S((1, HEAD), _F32), S((1, HEAD), _F32)], compiler_params=_cp("arbitrary"),
    )(z, dq, dk, gq.reshape(1, HEAD), gk.reshape(1, HEAD), *tabs)


_ATTN_C2 = float(HEAD ** -0.5 * np.log2(np.e))


def _attn_fwd(q, k, v, QPK):
    T, TOK = q.shape
    KVH = k.shape[1] // HEAD
    tq = _tile(T, 256, 8)
    W = QPK * HEAD

    def body(q_ref, k_ref, v_ref, o_ref, st_ref, vaug):
        @pl.when(pl.program_id(1) == 0)
        def _():
            vaug[:, :HEAD] = v_ref[...]
            vaug[:, HEAD:] = jnp.ones((T, HEAD), vaug.dtype)

        kk, va = k_ref[...], vaug[...]
        for g in range(QPK):
            cols = slice(g * HEAD, (g + 1) * HEAD)
            s = _dot(q_ref[:, cols], kk, _NT)
            m = jnp.max(s, axis=-1, keepdims=True)
            ov = _dot(jnp.exp2((s - m) * _ATTN_C2).astype(_MXU), va)
            l = ov[:, HEAD:HEAD + 1]
            o_ref[:, cols] = (ov[:, :HEAD] * (1.0 / l)).astype(o_ref.dtype)
            st_ref[:, g:g + 1] = m + jnp.log2(l) * (1.0 / _ATTN_C2)

    qs = BS((tq, W), lambda h, i: (i, h))
    ks = BS((T, HEAD), lambda h, i: (0, h))
    return _call(body, "attn_fwd", grid=(KVH, T // tq), in_specs=[qs, ks, ks],
                 out_specs=[qs, BS((None, tq, QPK), lambda h, i: (h, i, 0))],
                 out_shape=[S((T, TOK), _MXU), S((KVH, T, QPK), _F32)],
                 scratch_shapes=[pltpu.VMEM((T, 2 * HEAD), _MXU)],
                 compiler_params=_cp("parallel", "arbitrary"))(q, k, v)


def _attn_bwd(q, k, v, dcat, o, stat, QPK):
    T, TOK = q.shape
    KV = k.shape[1]
    KVH = KV // HEAD
    tq = _tile(T, 256, 8)
    nq = T // tq
    W = QPK * HEAD
    scale = HEAD ** -0.5

    def body(q_ref, k_ref, v_ref, do_ref, o_ref, st_ref, dq_ref, dk_ref, dv_ref, dk_acc, dv_acc):
        i = pl.program_id(1)

        @pl.when(i == 0)
        def _():
            dk_acc[...] = jnp.zeros_like(dk_acc)
            dv_acc[...] = jnp.zeros_like(dv_acc)

        kk, vv = k_ref[...], v_ref[...]
        for g in range(QPK):
            cols = slice(g * HEAD, (g + 1) * HEAD)
            qg = q_ref[:, cols]
            p = jnp.exp2((_dot(qg, kk, _NT) - st_ref[:, g:g + 1]) * _ATTN_C2)
            do32 = do_ref[:, cols]
            do = do32.astype(_MXU)
            delta = jnp.sum(do32 * o_ref[:, cols].astype(_F32), axis=-1, keepdims=True)
            ds = (p * (_dot(do, vv, _NT) - delta)).astype(_MXU)
            dq_ref[:, cols] = _dot(ds, kk) * scale
            dk_acc[...] += _dot(ds, qg, _TN)
            dv_acc[...] += _dot(p.astype(_MXU), do, _TN)

        @pl.when(i == nq - 1)
        def _():
            dk_ref[...] = dk_acc[...] * scale
            dv_ref[...] = dv_acc[...].astype(dv_ref.dtype)

    qs = BS((tq, W), lambda h, i: (i, h))
    ks = BS((T, HEAD), lambda h, i: (0, h))
    return _call(
        body, "attn_bwd", grid=(KVH, nq), in_specs=[qs, ks, ks, qs, qs, BS((None, tq, QPK), lambda h, i: (h, i, 0))],
        out_specs=[qs, ks, ks], out_shape=[S((T, TOK), _F32), S((T, KV), _F32), S((T, KV), _MXU)],
        scratch_shapes=[pltpu.VMEM((T, HEAD), _F32), pltpu.VMEM((T, HEAD), _F32)],
        compiler_params=_cp("parallel", "arbitrary"),
    )(q, k, v, dcat, o, stat)


def _mem_fwd(z, qblk, kv, gmq, gmk, MEMW):
    T = z.shape[0]
    NM = kv.shape[0]
    tq = _tile(T, 512, 8)
    scale = HEAD ** -0.5

    def body(q_ref, kv_ref, gq_ref, gk_ref, o_ref):
        for h in range(MEMW // HEAD):
            cols = slice(h * HEAD, (h + 1) * HEAD)
            kx = kv_ref[:, cols]
            kn = (kx * _rstd(kx) * gk_ref[...]).astype(_MXU)
            vv = kv_ref[:, MEMW + h * HEAD:MEMW + (h + 1) * HEAD].astype(_MXU)
            qx = q_ref[:, cols]
            qn = (qx * _rstd(qx) * gq_ref[...]).astype(_MXU)
            p = _softmax(_dot(qn, kn, _NT) * scale)
            o_ref[:, cols] = _dot(p.astype(_MXU), vv).astype(o_ref.dtype)

    vec = BS((1, HEAD), lambda i: (0, 0))
    return _call(
        body, "mem_fwd", grid=(T // tq,),
        in_specs=[BS((tq, MEMW), lambda i: (i, qblk)), BS((NM, 2 * MEMW), lambda i: (0, 0)), vec, vec],
        out_specs=BS((tq, MEMW), lambda i: (i, 0)), out_shape=S((T, MEMW), _MXU), compiler_params=_cp("parallel"),
    )(z, kv, gmq.reshape(1, HEAD), gmk.reshape(1, HEAD))


def _mem_bwd(z, qblk, kv, gmq, gmk, dcat, dblk, MEMW):
    T = z.shape[0]
    NM = kv.shape[0]
    tq = _tile(T, 512, 8)
    scale = HEAD ** -0.5

    def body(q_ref, kv_ref, gq_ref, gk_ref, do_ref, dz_ref, dkn_ref, dv_ref, dgq_ref):
        @pl.when(pl.program_id(0) == 0)
        def _():
            dkn_ref[...] = jnp.zeros_like(dkn_ref)
            dv_ref[...] = jnp.zeros_like(dv_ref)
            dgq_ref[...] = jnp.zeros_like(dgq_ref)

        for h in range(MEMW // HEAD):
            cols = slice(h * HEAD, (h + 1) * HEAD)
            kx = kv_ref[:, cols]
            kn = (kx * _rstd(kx) * gk_ref[...]).astype(_MXU)
            vv = kv_ref[:, MEMW + h * HEAD:MEMW + (h + 1) * HEAD].astype(_MXU)
            qx = q_ref[:, cols]
            rq = _rstd(qx)
            qhat = qx * rq
            qn = (qhat * gq_ref[...]).astype(_MXU)
            p = _softmax(_dot(qn, kn, _NT) * scale)
            do = do_ref[:, cols].astype(_MXU)
            dp = _dot(do, vv, _NT)
            ds = (p * (dp - jnp.sum(p * dp, axis=-1, keepdims=True)) * scale).astype(_MXU)
            dqn = _dot(ds, kn)
            dkn_ref[:, cols] += _dot(ds, qn, _TN)
            dv_ref[:, cols] += _dot(p.astype(_MXU), do, _TN)
            dgq_ref[...] += jnp.sum(dqn * qhat, axis=0, keepdims=True)
            dz_ref[:, cols] = _norm_bwd(dqn, qhat, rq, gq_ref[...]).astype(dz_ref.dtype)

    vec = BS((1, HEAD), lambda i: (0, 0))
    kvs = BS((NM, MEMW), lambda i: (0, 0))
    return _call(
        body, "mem_bwd", grid=(T // tq,),
        in_specs=[BS((tq, MEMW), lambda i: (i, qblk)), BS((NM, 2 * MEMW), lambda i: (0, 0)), vec, vec,
                  BS((tq, MEMW), lambda i: (i, dblk))],
        out_specs=[BS((tq, MEMW), lambda i: (i, 0)), kvs, kvs, vec],
        out_shape=[S((T, MEMW), _MXU), S((NM, MEMW), _F32), S((NM, MEMW), _F32), S((1, HEAD), _F32)],
        compiler_params=_cp("arbitrary"),
    )(z, kv, gmq.reshape(1, HEAD), gmk.reshape(1, HEAD), dcat)


def _memkv_bwd(kv, dkn, dv, gmk, MEMW):
    NM = kv.shape[0]

    def body(kv_ref, dkn_ref, dv_ref, gk_ref, dkv_ref, dgk_ref):
        dgk = jnp.zeros((1, HEAD), _F32)
        for h in range(MEMW // HEAD):
            cols = slice(h * HEAD, (h + 1) * HEAD)
            kx = kv_ref[:, cols]
            r = _rstd(kx)
            khat = kx * r
            dkn = dkn_ref[:, cols]
            dgk = dgk + jnp.sum(dkn * khat, axis=0, keepdims=True)
            dkv_ref[:, cols] = _norm_bwd(dkn, khat, r, gk_ref[...]).astype(dkv_ref.dtype)
        dgk_ref[...] = dgk
        dkv_ref[:, MEMW:] = dv_ref[...].astype(dkv_ref.dtype)

    return _call(body, "memkv_bwd", out_shape=[S((NM, 2 * MEMW), _MXU), S((1, HEAD), _F32)],
                 compiler_params=_cp())(kv, dkn, dv, gmk.reshape(1, HEAD))


def _cast_into_full(w, l, sh, idx, dep=None):
    tr, tc = _tile(sh.Rs, 512, 16), _tile(sh.Cs, 2048)
    nr, nc = sh.Rs // tr, sh.Cs // tc
    deps = [] if dep is None else [dep]

    def body(i_ref, c_ref, w_ref, *rest):
        rest[-1][...] = w_ref[...].astype(rest[-1].dtype)

    if sh.by_cols:
        o_map = lambda a, b, si, sc: (a, si[0] * nc + b)
    else:
        o_map = lambda a, b, si, sc: (si[0] * nr + a, b)
    return _call(
        body, "cast_into_full",
        grid_spec=pltpu.PrefetchScalarGridSpec(
            num_scalar_prefetch=2, grid=(nr, nc),
            in_specs=[BS((None, tr, tc), lambda a, b, si, sc: (l, a, b))] + [_ANY] * len(deps), out_specs=BS((tr, tc), o_map)),
        out_shape=S((sh.R, sh.C), _WIRE), compiler_params=_cp("parallel", "parallel"),
    )(*idx, w, *deps)


def _adamw(w, g, m, v, name, l0=0, l1=None, prev=None):
    L, R, C = w.shape
    l1 = L if l1 is None else l1
    tr, tc = _tile(R, 256, 8), _tile(C, 1024)
    c_m = 1.0 / (1.0 - ADAM_B1 ** ADAM_STEP)
    c_v = 1.0 / (1.0 - ADAM_B2 ** ADAM_STEP)

    def body(w_ref, g_ref, m_ref, v_ref, *rest):
        go_ref, d_ref, mo_ref, vo_ref = rest[-4:]
        gv = g_ref[...]
        mn = ADAM_B1 * m_ref[...] + (1.0 - ADAM_B1) * gv
        vn = ADAM_B2 * v_ref[...] + (1.0 - ADAM_B2) * (gv * gv)
        go_ref[...] = gv
        mo_ref[...] = mn
        vo_ref[...] = vn
        d_ref[...] = -ADAM_LR * ((mn * c_m) / (jnp.sqrt(vn * c_v) + ADAM_EPS) + ADAM_WD * w_ref[...])

    blk = BS((None, tr, tc), lambda a, i, j: (l0 + a, i, j))
    prevs = [] if prev is None else list(prev)
    return _call(body, name, grid=(l1 - l0, R // tr, C // tc), in_specs=[blk] * 4 + [_ANY] * len(prevs), out_specs=[blk] * 4,
                 out_shape=[S((L, R, C), _F32)] * 4, input_output_aliases={4 + k: k for k in range(len(prevs))},
                 compiler_params=_cp("parallel", "parallel", "parallel"))(w, g, m, v, *prevs)


def _where_am_i():
    x, y, c = lax.axis_index("x"), lax.axis_index("y"), lax.axis_index("c")
    chips = [(1 - x, y), (x, 1 - y), (1 - x, 1 - y)]
    return x, y, c, 2 * x + y, chips, [2 * cx + cy for cx, cy in chips]


class _Shard:
    def __init__(self, R, C, by_cols):
        self.R, self.C, self.by_cols = R, C, by_cols
        self.Rs, self.Cs = (R, C // 4) if by_cols else (R // 4, C)
        self.Rh = self.Rs // 2
        self.Q = R // 8

    def full_piece(self, ref, j, cc):
        if self.by_cols:
            return ref.at[pl.ds(cc * self.Rh, self.Rh), pl.ds(_mo(j * self.Cs, 128), self.Cs)]
        return ref.at[pl.ds(_mo(j * self.Rs + cc * self.Rh, 16), self.Rh), :]

    def full_shard(self, ref, j):
        if self.by_cols:
            return ref.at[:, pl.ds(_mo(j * self.Cs, 128), self.Cs)]
        return ref.at[pl.ds(_mo(j * self.Rs, 16), self.Rs), :]

    def shard_half(self, ref, cc):
        return ref.at[pl.ds(_mo(cc * self.Rh, 16), self.Rh), :]

    def half_piece(self, ref, j):
        if self.by_cols:
            return ref.at[:, pl.ds(_mo(j * self.Cs, 128), self.Cs)]
        return ref.at[pl.ds(_mo(j * self.Rh, 16), self.Rh), :]


def _remote(src, dst, ssem, rsem, dev):
    return pltpu.make_async_remote_copy(src_ref=src, dst_ref=dst, send_sem=ssem, recv_sem=rsem, device_id=dev, device_id_type=_MESH)


_HBM = pl.BlockSpec(memory_space=pltpu.HBM)
_SEM = pl.BlockSpec(memory_space=pltpu.SEMAPHORE)
_EFFECT = pltpu.SideEffectType.DATAFLOW_SIDE_EFFECTING


def _in_hbm(a):
    return pltpu.with_memory_space_constraint(a, pltpu.HBM)


def _gather_start(fulls, shs, name):
    n = len(fulls)

    def body(*refs):
        bufs = refs[:n]
        send_sems, recv_sems = refs[n], refs[n + 1]
        token = refs[-1]
        x, y, c, me, chips, chip_ids = _where_am_i()
        for t in range(n):
            mine = shs[t].full_piece(bufs[t], me, c)
            for r in range(3):
                _remote(mine, mine, send_sems.at[3 * t + r], recv_sems.at[3 * t + r], (*chips[r], c)).start()
        token[...] = jnp.zeros_like(token)

    out = pl.pallas_call(
        body, name=name, in_specs=[_HBM] * n,
        out_shape=(pltpu.SemaphoreType.DMA((3 * n,)), pltpu.SemaphoreType.DMA((3 * n,)), *[pltpu.HBM(f.shape, f.dtype) for f in fulls],
                   S((8, 128), _F32)),
        out_specs=(_SEM, _SEM, *[_HBM] * n, pl.BlockSpec(memory_space=pltpu.VMEM)),
        input_output_aliases={t: 2 + t for t in range(n)},
        compiler_params=pltpu.CompilerParams(has_side_effects=_EFFECT), **_KW,
    )(*[_in_hbm(f) for f in fulls])
    return out[0], out[1], list(out[2:2 + n]), out[-1]


def _gather_wait(fulls, send_sems, recv_sems, after, shs, name):
    n = len(fulls)

    def body(*refs):
        bufs = refs[:n]
        ssem, rsem = refs[n], refs[n + 1]
        x, y, c, me, chips, chip_ids = _where_am_i()
        for t in range(n):
            mine = shs[t].full_piece(bufs[t], me, c)
            for r in range(3):
                _remote(mine, mine, ssem.at[3 * t + r], rsem.at[3 * t + r], (*chips[r], c)).wait_send()
        for t in range(n):
            for r in range(3):
                piece = shs[t].full_piece(bufs[t], chip_ids[r], c)
                _remote(piece, piece, ssem.at[3 * t + r], rsem.at[3 * t + r], (*chips[r], c)).wait_recv()

    out = pl.pallas_call(
        body, name=name, in_specs=[*[_HBM] * n, _SEM, _SEM, _ANY], out_specs=[_HBM] * n,
        out_shape=[pltpu.HBM(f.shape, f.dtype) for f in fulls], input_output_aliases={t: t for t in range(n)},
        compiler_params=pltpu.CompilerParams(has_side_effects=_EFFECT), **_KW,
    )(*fulls, send_sems, recv_sems, after)
    return list(out)


def _gather_pass_on(fulls, shs, name):
    n = len(fulls)

    def body(*refs):
        bufs = refs[n:2 * n]
        send_sems, recv_sems = refs[2 * n:]
        x, y, c, me, chips, chip_ids = _where_am_i()
        sib = (x, y, 1 - c)
        cps = []
        for t in range(n):
            for r in range(3):
                piece = shs[t].full_piece(bufs[t], chip_ids[r], c)
                cps.append(_remote(piece, piece, send_sems.at[t, r], recv_sems.at[t, r], sib))
        for cp in cps:
            cp.start()
        for t in range(n):
            for r in range(3):
                piece = shs[t].full_piece(bufs[t], chip_ids[r], 1 - c)
                _remote(piece, piece, send_sems.at[t, r], recv_sems.at[t, r], sib).wait_recv()
        for cp in cps:
            cp.wait_send()

    return _call(
        body, name, in_specs=[_ANY] * n, out_specs=[_ANY] * n, out_shape=[S(f.shape, f.dtype) for f in fulls],
        input_output_aliases={t: t for t in range(n)},
        scratch_shapes=[pltpu.SemaphoreType.DMA((n, 3)), pltpu.SemaphoreType.DMA((n, 3))],
        compiler_params=pltpu.CompilerParams(has_side_effects=True),
    )(*fulls)


def _rs_pair_exchange(dws, shs, name):
    n = len(dws)

    def body(*refs):
        ins, outs = refs[:n], refs[n:2 * n]
        send_sems, recv_sems = refs[2 * n:]
        x, y, c, *_ = _where_am_i()
        sib = (x, y, 1 - c)
        cps = []
        for t in range(n):
            sh = shs[t]
            if sh.by_cols:
                cps.append(_remote(ins[t].at[pl.ds((1 - c) * sh.Rh, sh.Rh), :], outs[t], send_sems.at[t, 0], recv_sems.at[t, 0], sib))
            else:
                for j in range(4):
                    cps.append(_remote(sh.full_piece(ins[t], j, 1 - c), sh.half_piece(outs[t], j),
                                       send_sems.at[t, j], recv_sems.at[t, j], sib))
        for cp in cps:
            cp.start()
        for cp in cps:
            cp.wait()

    return _call(
        body, name, in_specs=[_ANY] * n, out_specs=[_ANY] * n, out_shape=[S((sh.R // 2, sh.C), _WIRE) for sh in shs],
        scratch_shapes=[pltpu.SemaphoreType.DMA((n, 4)), pltpu.SemaphoreType.DMA((n, 4))],
        compiler_params=pltpu.CompilerParams(has_side_effects=True),
    )(*dws)


def _rs_pair_add(dw32, recv, sh, idx):
    tr, tc = _tile(sh.Q, 512, 16), _tile(sh.C, 2048)
    nb = sh.Q // tr

    def body(i_ref, c_ref, a_ref, b_ref, o32_ref, ow_ref):
        p = a_ref[...] + b_ref[...].astype(_F32)
        o32_ref[...] = p
        ow_ref[...] = p.astype(ow_ref.dtype)

    if sh.by_cols:
        a_map = lambda j, i, b, si, sc: (sc[0] * 4 * nb + j * nb + i, b)
    else:
        a_map = lambda j, i, b, si, sc: (j * 2 * nb + sc[0] * nb + i, b)
    h_spec = BS((tr, tc), lambda j, i, b, si, sc: (j * nb + i, b))
    return _call(
        body, "rs_pair_add",
        grid_spec=pltpu.PrefetchScalarGridSpec(num_scalar_prefetch=2, grid=(4, nb, sh.C // tc),
                                               in_specs=[BS((tr, tc), a_map), h_spec], out_specs=[h_spec, h_spec]),
        out_shape=[S((sh.R // 2, sh.C), _F32), S((sh.R // 2, sh.C), _WIRE)],
        compiler_params=_cp("parallel", "parallel", "parallel"),
    )(*idx, dw32, recv)


def _rs_chip_start(pws, shs, name):
    n = len(pws)
    lands = [lax.empty((3, sh.Rh, sh.Cs), _WIRE) for sh in shs]

    def body(*refs):
        ins, lnd = refs[:n], refs[n:2 * n]
        send_sems, recv_sems = refs[2 * n], refs[2 * n + 1]
        token = refs[-1]
        x, y, c, me, chips, chip_ids = _where_am_i()
        for t in range(n):
            for r in range(3):
                _remote(shs[t].half_piece(ins[t], chip_ids[r]), lnd[t].at[r], send_sems.at[3 * t + r], recv_sems.at[3 * t + r],
                        (*chips[r], c)).start()
        token[...] = jnp.zeros_like(token)

    out = pl.pallas_call(
        body, name=name, in_specs=[_HBM] * (2 * n),
        out_shape=(pltpu.SemaphoreType.DMA((3 * n,)), pltpu.SemaphoreType.DMA((3 * n,)),
                   *[pltpu.HBM(a.shape, a.dtype) for a in (*pws, *lands)], S((8, 128), _F32)),
        out_specs=(_SEM, _SEM, *[_HBM] * (2 * n), pl.BlockSpec(memory_space=pltpu.VMEM)),
        input_output_aliases={t: 2 + t for t in range(2 * n)},
        compiler_params=pltpu.CompilerParams(has_side_effects=_EFFECT), **_KW,
    )(*[_in_hbm(a) for a in (*pws, *lands)])
    return out[0], out[1], list(out[2:2 + n]), list(out[2 + n:2 + 2 * n]), out[-1]


def _rs_chip_wait(pws, lands, send_sems, recv_sems, after, shs, name):
    n = len(pws)
    after = list(after) if isinstance(after, (list, tuple)) else [after]

    def body(*refs):
        ins, lnd = refs[:n], refs[n:2 * n]
        ssem, rsem = refs[2 * n], refs[2 * n + 1]
        x, y, c, me, chips, chip_ids = _where_am_i()
        for t in range(n):
            for r in range(3):
                cp = _remote(shs[t].half_piece(ins[t], chip_ids[r]), lnd[t].at[r], ssem.at[3 * t + r], rsem.at[3 * t + r], (*chips[r], c))
                cp.wait_send()
        for t in range(n):
            for r in range(3):
                cp = _remote(shs[t].half_piece(ins[t], chip_ids[r]), lnd[t].at[r], ssem.at[3 * t + r], rsem.at[3 * t + r], (*chips[r], c))
                cp.wait_recv()

    out = pl.pallas_call(
        body, name=name, in_specs=[*[_HBM] * (2 * n), _SEM, _SEM, *[_ANY] * len(after)], out_specs=[_HBM] * (2 * n),
        out_shape=[pltpu.HBM(a.shape, a.dtype) for a in (*pws, *lands)], input_output_aliases={t: t for t in range(2 * n)},
        compiler_params=pltpu.CompilerParams(has_side_effects=_EFFECT), **_KW,
    )(*pws, *lands, send_sems, recv_sems, *after)
    return list(out[n:])


def _rs_chip_add(p32, recv, sh, idx, g_prev, l, L):
    tr, tc = _tile(sh.Rh, 512, 16), _tile(sh.Cs, 2048)
    nr, nc = sh.Rh // tr, sh.Cs // tc

    def body(i_ref, c_ref, a_ref, b_ref, *rest):
        rest[-1][...] = a_ref[...] + b_ref[0].astype(_F32) + b_ref[1].astype(_F32) + b_ref[2].astype(_F32)

    if sh.by_cols:
        a_map = lambda a, b, si, sc: (a, si[0] * nc + b)
    else:
        a_map = lambda a, b, si, sc: (si[0] * nr + a, b)
    in_specs = [BS((tr, tc), a_map), BS((3, tr, tc), lambda a, b, si, sc: (0, a, b))]
    args = [*idx, p32, recv]
    if g_prev is not None:
        in_specs.append(_ANY)
        args.append(g_prev)
    return _call(
        body, "rs_chip_add",
        grid_spec=pltpu.PrefetchScalarGridSpec(num_scalar_prefetch=2, grid=(nr, nc), in_specs=in_specs,
                                               out_specs=BS((None, tr, tc), lambda a, b, si, sc: (l, sc[0] * nr + a, b))),
        out_shape=S((L, sh.Rs, sh.Cs), _F32), input_output_aliases={} if g_prev is None else {4: 0},
        compiler_params=_cp("parallel", "parallel"),
    )(*args)


def _rs_pair_share(gs, ls, shs, name):
    n = len(gs)

    def body(*refs):
        bufs = refs[n:2 * n]
        send_sems, recv_sems = refs[2 * n:]
        x, y, c, *_ = _where_am_i()
        sib = (x, y, 1 - c)
        cps = []
        for t in range(n):
            mine = shs[t].shard_half(bufs[t].at[ls[t]], c)
            cps.append(_remote(mine, mine, send_sems.at[t], recv_sems.at[t], sib))
        for cp in cps:
            cp.start()
        for t in range(n):
            other = shs[t].shard_half(bufs[t].at[ls[t]], 1 - c)
            _remote(other, other, send_sems.at[t], recv_sems.at[t], sib).wait_recv()
        for cp in cps:
            cp.wait_send()

    return _call(
        body, name, in_specs=[_ANY] * n, out_specs=[_ANY] * n, out_shape=[S(g.shape, g.dtype) for g in gs],
        input_output_aliases={t: t for t in range(n)},
        scratch_shapes=[pltpu.SemaphoreType.DMA((n,)), pltpu.SemaphoreType.DMA((n,))],
        compiler_params=pltpu.CompilerParams(has_side_effects=True),
    )(*gs)


def _all_reduce_small(xs):
    M = xs.shape[0]

    def body(x_ref, tot_ref, out_ref, send_sems, recv_sems, local_sem):
        x, y, c, me, chips, chip_ids = _where_am_i()
        sib = (x, y, 1 - c)

        def rows(dev):
            return out_ref.at[pl.ds(_mo((4 * dev[0] + 2 * dev[1] + dev[2]) * M, 8), M), :]

        def copy(k, block, to, src=None):
            return _remote(rows(block) if src is None else src, rows(block), send_sems.at[k], recv_sems.at[k], to)

        mine = pltpu.make_async_copy(x_ref, rows((x, y, c)), local_sem)
        mine.start()
        first = [copy(0, (x, y, c), sib, src=x_ref)]
        first += [copy(1 + j, (x, y, c), (*chip, c), src=x_ref) for j, chip in enumerate(chips)]
        for cp in first:
            cp.start()
        passed = [copy(4 + j, (*chip, c), sib) for j, chip in enumerate(chips)]
        for j, chip in enumerate(chips):
            copy(1 + j, (*chip, c), (x, y, c)).wait_recv()
            passed[j].start()
        copy(0, sib, (x, y, c)).wait_recv()
        for j, chip in enumerate(chips):
            copy(4 + j, (*chip, 1 - c), (x, y, c)).wait_recv()
        for cp in first + passed:
            cp.wait_send()
        mine.wait()
        tot = out_ref[pl.ds(0, M), :]
        for d in range(1, 8):
            tot = tot + out_ref[pl.ds(d * M, M), :]
        tot_ref[...] = tot

    vm = pl.BlockSpec(memory_space=pltpu.VMEM)
    return _call(
        body, "all_reduce_small", in_specs=[vm], out_specs=[vm, vm], out_shape=[S((M, 128), _F32), S((8 * M, 128), _F32)],
        scratch_shapes=[pltpu.SemaphoreType.DMA((7,)), pltpu.SemaphoreType.DMA((7,)), pltpu.SemaphoreType.DMA],
        compiler_params=_cp(has_side_effects=True),
    )(xs)[0]


def _reduce_scatter_begin(dws, shs, idx, l):
    recv_a = _rs_pair_exchange([d[1] for d in dws], shs, "rs_pair_exchange")
    p = [_rs_pair_add(d[0], ra, sh, idx) for d, ra, sh in zip(dws, recv_a, shs)]
    ssem, rsem, pws, lands, token = _rs_chip_start([pw for _, pw in p], shs, f"rs_chip_start_{l}")
    return ([p32 for p32, _ in p], pws, lands, ssem, rsem), token


def _reduce_scatter_end(state, after, tensors, shs, gstack, idx, l):
    p32s, pws, lands, ssem, rsem = state
    recv_b = _rs_chip_wait(pws, lands, ssem, rsem, after, shs, f"rs_chip_wait_{l}")
    gs = [_rs_chip_add(p32, rb, sh, idx, gstack[name], i, L) for p32, rb, sh, (name, i, L) in zip(p32s, recv_b, shs, tensors)]
    gs = _rs_pair_share(gs, [i for _, i, _ in tensors], shs, "rs_pair_share")
    for (name, _, _), g in zip(tensors, gs):
        gstack[name] = g


def _pack(parts):
    out = []
    for p in parts:
        p2 = p.reshape(-1, 128)
        pad = (-p2.shape[0]) % 8
        out.append(jnp.pad(p2, ((0, pad), (0, 0))) if pad else p2)
    return jnp.concatenate(out, axis=0)


def _unpack(packed, like):
    out, at = [], 0
    for p in like:
        n = p.size // 128
        out.append(packed[at:at + n].reshape(p.shape))
        at += n + ((-n) % 8)
    return out


def kernel(x, mem, g_mix, g_ffn, w_in_a, g_v_a, w_spatial, b_spatial, w_in_b, g_q_b, g_k_b, g_mem, w_mem_kv, g_mq, g_mk, w_out, w_gate_up, w_down, loss_target, m_g_mix, m_g_ffn, m_w_in_a, m_g_v_a, m_w_spatial, m_b_spatial, m_w_in_b, m_g_q_b, m_g_k_b, m_g_mem, m_w_mem_kv, m_g_mq, m_g_mk, m_w_out, m_w_gate_up, m_w_down, v_g_mix, v_g_ffn, v_w_in_a, v_g_v_a, v_w_spatial, v_b_spatial, v_w_in_b, v_g_q_b, v_g_k_b, v_g_mem, v_w_mem_kv, v_g_mq, v_g_mk, v_w_out, v_w_gate_up, v_w_down):
    xs = x[0]
    mem2 = mem[0]
    target = loss_target[0]
    T, D = xs.shape
    depth = g_mix.shape[0]
    MEMW = w_mem_kv.shape[2] // 2
    TOK = D - MEMW
    KV = (w_in_b.shape[2] * 4 - TOK - MEMW) // 2
    QPK = TOK // KV
    F = w_gate_up.shape[2] * 4 // 2

    idx = ((2 * lax.axis_index("x") + lax.axis_index("y")).astype(jnp.int32).reshape(1), lax.axis_index("c").astype(jnp.int32).reshape(1))

    big = {
        "w_in_a": (w_in_a, _Shard(D, w_in_a.shape[2] * 4, True)),
        "w_in_b": (w_in_b, _Shard(D, w_in_b.shape[2] * 4, True)),
        "w_mem_kv": (w_mem_kv, _Shard(D, 2 * MEMW, False)),
        "w_out": (w_out, _Shard(D, D, False)),
        "w_gate_up": (w_gate_up, _Shard(D, 2 * F, True)),
        "w_down": (w_down, _Shard(F, D, False)),
    }

    def layer_tensors(l):
        n_in = "w_in_a" if l % 2 == 0 else "w_in_b"
        return [(n_in, l // 2, big[n_in][0].shape[0])] + [(n, l, depth) for n in ("w_mem_kv", "w_out", "w_gate_up", "w_down")]

    def layer_shards(l):
        return [big[n][1] for n, _, _ in layer_tensors(l)]

    full = {n: [None] * w.shape[0] for n, (w, _) in big.items()}
    flying = {}

    def start_layer(l, dep):
        tens = layer_tensors(l)
        token = dep
        for gi, group in enumerate([tens[:2], tens[2:3], tens[3:4], tens[4:]] if l == 0 else [tens]):
            shs = [big[n][1] for n, _, _ in group]
            bufs = [_cast_into_full(big[n][0], i, big[n][1], idx, dep=dep) for n, i, _ in group]
            ssem, rsem, bufs, token = _gather_start(bufs, shs, f"gather_start_{l}_{gi}")
            for n, i, _ in group:
                flying[(n, i)] = dict(group=group, shs=shs, state=(ssem, rsem, bufs), name=f"{l}_{gi}")
        return token

    def weight(n, i, after):
        if full[n][i] is None:
            fl = flying[(n, i)]
            ssem, rsem, bufs = fl["state"]
            bufs = _gather_wait(bufs, ssem, rsem, after, fl["shs"], "gather_wait_" + fl["name"])
            bufs = _gather_pass_on(bufs, fl["shs"], "gather_pass_on")
            for (m, j, _), b in zip(fl["group"], bufs):
                full[m][j] = b
        return full[n][i]

    after = start_layer(0, None)
    tabs = _rope_tables(T)

    saved = []
    xc = xs
    for l in range(depth):
        is_a = l % 2 == 0
        li = l // 2
        w_in = weight("w_in_a" if is_a else "w_in_b", li, after)
        token = start_layer(l + 1, w_in) if l + 1 < depth else None
        h = _rmsnorm_fwd(xc, g_mix[l], "rmsnorm_fwd", dep=token)
        z = _mm_nn("mm_in", h, w_in, pm=2048, pn=512)
        st = dict(x=xc, h=h, z=z)
        if is_a:
            ws_m = w_spatial[li].astype(_MXU)
            st["ws_m"], st["wst_m"], st["b_t"] = ws_m, jnp.swapaxes(ws_m, 1, 2), b_spatial[li].T
            tok = _mixer_a_fwd(z, g_v_a[li], ws_m, st["b_t"], TOK)
            qblk = 2 * TOK // MEMW
        else:
            q, k, v = _qk_rope_fwd(z, g_q_b[li], g_k_b[li], tabs, TOK, KV)
            tok, stat = _attn_fwd(q, k, v, QPK)
            st["q"], st["k"], st["v"], st["stat"] = q, k, v, stat
            qblk = (TOK + 2 * KV) // MEMW
        mem_n = _rmsnorm_fwd(mem2, g_mem[l], "rmsnorm_mem")
        kv = _mm_nn("mm_memkv", mem_n, weight("w_mem_kv", l, z))
        mo = _mem_fwd(z, qblk, kv, g_mq[l], g_mk[l], MEMW)
        cat = jnp.concatenate([tok, mo], axis=1)
        x1 = _mm_nn("mm_out", cat, weight("w_out", l, cat), add=xc)
        h2 = _rmsnorm_fwd(x1, g_ffn[l], "rmsnorm_fwd")
        act, gu = _ffn_gate_up(h2, weight("w_gate_up", l, h2))
        xc = _mm_nn("mm_down", act, weight("w_down", l, act), add=x1, pm=512, pn=512, pk=8192)
        after = xc
        st.update(mem_n=mem_n, kv=kv, qblk=qblk, cat=cat, x1=x1, h2=h2, act=act, gu=gu)
        saved.append(st)

    dx, dxm, sq = _loss_head(xc, target)
    loss = lax.psum(sq[0, 0] * (0.5 / D), ("x", "y", "c"))

    gsm = {n: [None] * len(a) for n, a in dict(g_mix=g_mix, g_ffn=g_ffn, g_v_a=g_v_a, w_spatial=w_spatial, b_spatial=b_spatial,
                                                g_q_b=g_q_b, g_k_b=g_k_b, g_mem=g_mem, g_mq=g_mq, g_mk=g_mk).items()}
    gstack = {n: None for n in big}
    pending, token = None, None
    for l in reversed(range(depth)):
        st = saved[l]
        is_a = l % 2 == 0
        li = l // 2
        gbig = {}
        dgu = _ffn_dact(dxm, full["w_down"][l], st["gu"], dep=token)
        gbig["w_down"] = _mm_tn_dual("mm_dw_down", st["act"], dxm, pm=1408)
        dh2 = _ffn_dh(dgu, full["w_gate_up"][l])
        gbig["w_gate_up"] = _ffn_dwgu(st["h2"], dgu)
        dx, dxm, dg = _rmsnorm_bwd(st["x1"], g_ffn[l], dh2, dx, "rmsnorm_bwd")
        gsm["g_ffn"][l] = dg[0]
        dcat = _mm_nt("mm_dcat", dxm, full["w_out"][l])
        gbig["w_out"] = _mm_tn_dual("mm_dw_out", st["cat"], dxm)
        dzq, dkn, dvm, dgq = _mem_bwd(st["z"], st["qblk"], st["kv"], g_mq[l], g_mk[l], dcat, TOK // MEMW, MEMW)
        dkv, dgk = _memkv_bwd(st["kv"], dkn, dvm, g_mk[l], MEMW)
        gsm["g_mq"][l], gsm["g_mk"][l] = dgq[0], dgk[0]
        gbig["w_mem_kv"] = _mm_tn_dual("mm_dw_memkv", st["mem_n"], dkv)
        dmem_n = _mm_nt("mm_dmemn", dkv, full["w_mem_kv"][l])
        gsm["g_mem"][l] = _rmsnorm_bwd(mem2, g_mem[l], dmem_n, None, "rmsnorm_bwd_mem")[2][0]
        if is_a:
            dz_tok, dws, dbs, dgv = _mixer_a_bwd(st["z"], dcat, g_v_a[li], st["ws_m"], st["wst_m"], st["b_t"], TOK)
            gsm["w_spatial"][li], gsm["b_spatial"][li], gsm["g_v_a"][li] = dws, dbs[:, :, 0], dgv[0]
            dz = jnp.concatenate([dz_tok, dzq], axis=1)
        else:
            dq, dk, dv = _attn_bwd(st["q"], st["k"], st["v"], dcat, st["cat"], st["stat"], QPK)
            dz_qk, dgq_b, dgk_b = _qk_rope_bwd(st["z"], dq, dk, g_q_b[li], g_k_b[li], tabs, TOK, KV)
            gsm["g_q_b"][li], gsm["g_k_b"][li] = dgq_b[0], dgk_b[0]
            dz = jnp.concatenate([dz_qk, dv, dzq], axis=1)
        n_in = "w_in_a" if is_a else "w_in_b"
        dh = _mm_nt("mm_dh", dz, full[n_in][li])
        gbig[n_in] = _mm_tn_dual("mm_dw_in", st["h"], dz, pm=2048, pn=512)
        dx, dxm, dg = _rmsnorm_bwd(st["x"], g_mix[l], dh, dx, "rmsnorm_bwd")
        gsm["g_mix"][l] = dg[0]
        began = _reduce_scatter_begin([gbig[n] for n, _, _ in layer_tensors(l)], layer_shards(l), idx, l)
        if pending is not None:
            _reduce_scatter_end(pending[0], began[1], layer_tensors(pending[1]), layer_shards(pending[1]), gstack, idx, pending[1])
        pending, token = (began[0], l), began[1]

    small = ["g_mix", "g_ffn", "g_v_a", "w_spatial", "b_spatial", "g_q_b", "g_k_b", "g_mem", "g_mq", "g_mk"]
    env = dict(g_mix=g_mix, g_ffn=g_ffn, g_v_a=g_v_a, w_spatial=w_spatial, b_spatial=b_spatial, g_q_b=g_q_b, g_k_b=g_k_b,
               g_mem=g_mem, g_mq=g_mq, g_mk=g_mk,
               m_g_mix=m_g_mix, m_g_ffn=m_g_ffn, m_g_v_a=m_g_v_a, m_w_spatial=m_w_spatial, m_b_spatial=m_b_spatial,
               m_g_q_b=m_g_q_b, m_g_k_b=m_g_k_b, m_g_mem=m_g_mem, m_g_mq=m_g_mq, m_g_mk=m_g_mk,
               v_g_mix=v_g_mix, v_g_ffn=v_g_ffn, v_g_v_a=v_g_v_a, v_w_spatial=v_w_spatial, v_b_spatial=v_b_spatial,
               v_g_q_b=v_g_q_b, v_g_k_b=v_g_k_b, v_g_mem=v_g_mem, v_g_mq=v_g_mq, v_g_mk=v_g_mk,
               m_w_in_a=m_w_in_a, m_w_in_b=m_w_in_b, m_w_mem_kv=m_w_mem_kv, m_w_out=m_w_out, m_w_gate_up=m_w_gate_up, m_w_down=m_w_down,
               v_w_in_a=v_w_in_a, v_w_in_b=v_w_in_b, v_w_mem_kv=v_w_mem_kv, v_w_out=v_w_out, v_w_gate_up=v_w_gate_up, v_w_down=v_w_down)
    like = [env[n] for n in small]
    g_small = _all_reduce_small(_pack([jnp.stack(gsm[n]) for n in small]))

    res = {}
    outs = _adamw(_pack(like)[None], g_small[None], _pack([env["m_" + n] for n in small])[None],
                  _pack([env["v_" + n] for n in small])[None], "adamw_small")
    unpacked = [_unpack(o[0], like) for o in outs]
    for k, n in enumerate(small):
        res[n] = [u[k] for u in unpacked]
    last = {n: i for n, i, _ in layer_tensors(pending[1])}
    early = {}
    for n, (w, _) in big.items():
        L = w.shape[0]
        if n not in last:
            res[n] = _adamw(w, gstack[n], env["m_" + n], env["v_" + n], "adamw_" + n)
        elif L > 1:
            assert last[n] == 0
            early[n] = _adamw(w, gstack[n], env["m_" + n], env["v_" + n], "adamw_early_" + n, l0=1)
    done = [o[1] for o in early.values()] + [res[n][1] for n in big if n in res] + [res[small[0]][1]]
    _reduce_scatter_end(pending[0], done, layer_tensors(pending[1]), layer_shards(pending[1]), gstack, idx, pending[1])
    for n in last:
        res[n] = _adamw(big[n][0], gstack[n], env["m_" + n], env["v_" + n], "adamw_last_" + n, l0=0, l1=1, prev=early.get(n))

    order = ["g_mix", "g_ffn", "w_in_a", "g_v_a", "w_spatial", "b_spatial", "w_in_b", "g_q_b", "g_k_b", "g_mem", "w_mem_kv",
             "g_mq", "g_mk", "w_out", "w_gate_up", "w_down"]
    return (loss, dx.reshape(1, T, D), *[res[n][0] for n in order], *[res[n][1] for n in order],
            *[res[n][2] for n in order], *[res[n][3] for n in order])
```

```python
import jax
import jax.numpy as jnp
import numpy as np
from jax import lax
from jax.experimental import pallas as pl
from jax.experimental.pallas import tpu as pltpu

_F32 = jnp.float32
_MXU = jnp.bfloat16
_WIRE = jnp.bfloat16
_KW = {}

EPS = 1e-6
HEAD = 128
CHUNK = 128
GRID_W = 64
ROPE_THETA = 10000.0
ADAM_LR, ADAM_B1, ADAM_B2, ADAM_EPS, ADAM_WD, ADAM_STEP = 0.001, 0.9, 0.999, 1e-08, 0.01, 10
_SQRT_HALF = float(np.sqrt(0.5))
_INV_SQRT_2PI = float(1.0 / np.sqrt(2.0 * np.pi))
_VMEM_LIMIT = 56 * 1024 * 1024
_MESH = pl.DeviceIdType.MESH

_NN = (((1,), (0,)), ((), ()))
_NT = (((1,), (1,)), ((), ()))
_TN = (((0,), (0,)), ((), ()))

S = jax.ShapeDtypeStruct
BS = pl.BlockSpec
_ANY = pl.BlockSpec(memory_space=pl.ANY)


def _tile(n, pref, mult=128):
    if n <= pref:
        return n
    d = (pref // mult) * mult
    while d >= mult:
        if n % d == 0:
            return d
        d -= mult
    raise ValueError(f"no tile for {n} (pref {pref}, mult {mult})")


def _mo(v, m):
    return v if isinstance(v, int) else pl.multiple_of(v, m)


def _cp(*sem, **kw):
    return pltpu.CompilerParams(dimension_semantics=sem or None, vmem_limit_bytes=_VMEM_LIMIT, **kw)


def _call(body, name, **kw):
    return pl.pallas_call(body, name=name, **kw, **_KW)


def _dot(a, b, dn=_NN):
    return lax.dot_general(a, b, dn, preferred_element_type=_F32)


def _gelu(x):
    return 0.5 * x * (1.0 + lax.erf(x * _SQRT_HALF))


def _gelu_grad(x):
    return 0.5 * (1.0 + lax.erf(x * _SQRT_HALF)) + x * jnp.exp(-0.5 * x * x) * _INV_SQRT_2PI


def _rstd(x):
    return lax.rsqrt(jnp.mean(x * x, axis=-1, keepdims=True) + EPS)


def _norm_bwd(dout, xhat, r, g):
    dy = dout * g
    return r * (dy - xhat * jnp.mean(dy * xhat, axis=-1, keepdims=True))


def _softmax(s):
    e = jnp.exp(s - jnp.max(s, axis=-1, keepdims=True))
    return e * (1.0 / jnp.sum(e, axis=-1, keepdims=True))


def _mm(name, a, b, a_spec, b_spec, dn, grid, acc_shape, out_shape, out_specs, epilogue, extra=(), extra_specs=(), dep=None):
    nk = grid[2]
    n_ex = len(extra)
    deps = [] if dep is None else [dep]
    multi = isinstance(out_shape, (list, tuple))
    n_out = len(out_shape) if multi else 1

    def body(*refs):
        a_ref, b_ref = refs[0], refs[1]
        ex = refs[2:2 + n_ex]
        outs = refs[2 + n_ex + len(deps):2 + n_ex + len(deps) + n_out]

        def prod():
            return _dot(a_ref[...].astype(_MXU), b_ref[...].astype(_MXU), dn)

        if nk == 1:
            epilogue(prod(), ex, outs)
        else:
            acc = refs[-1]
            k = pl.program_id(2)

            @pl.when(k == 0)
            def _():
                acc[...] = jnp.zeros_like(acc)

            acc[...] += prod()

            @pl.when(k == nk - 1)
            def _():
                epilogue(acc[...], ex, outs)

    return _call(
        body, name, grid=grid, in_specs=[a_spec, b_spec, *extra_specs] + [_ANY] * len(deps), out_specs=out_specs, out_shape=out_shape,
        scratch_shapes=[] if nk == 1 else [pltpu.VMEM(acc_shape, _F32)],
        compiler_params=_cp("parallel", "parallel", "arbitrary"),
    )(a, b, *extra, *deps)


def _ep_store(acc, ex, outs):
    for o in outs:
        o[...] = acc.astype(o.dtype)


def _ep_add(acc, ex, outs):
    outs[0][...] = (acc + ex[0][...]).astype(outs[0].dtype)


def _mm_nn(name, a, b, out_dtype=_F32, add=None, pm=1024, pn=1024, pk=2048, dep=None):
    M, K = a.shape
    N = b.shape[1]
    tm, tn, tk = _tile(M, pm, 8), _tile(N, pn), _tile(K, pk)
    o_spec = BS((tm, tn), lambda i, j, k: (i, j))
    return _mm(name, a, b, BS((tm, tk), lambda i, j, k: (i, k)), BS((tk, tn), lambda i, j, k: (k, j)), _NN,
               (M // tm, N // tn, K // tk), (tm, tn), S((M, N), out_dtype), o_spec,
               _ep_store if add is None else _ep_add,
               extra=() if add is None else (add,), extra_specs=() if add is None else (o_spec,), dep=dep)


def _mm_nt(name, a, b, out_dtype=_F32, pm=1024, pn=1024, pk=4096):
    M, K = a.shape
    N = b.shape[0]
    tm, tn, tk = _tile(M, pm, 8), _tile(N, pn), _tile(K, pk)
    return _mm(name, a, b, BS((tm, tk), lambda i, j, k: (i, k)), BS((tn, tk), lambda i, j, k: (j, k)), _NT,
               (M // tm, N // tn, K // tk), (tm, tn), S((M, N), out_dtype), BS((tm, tn), lambda i, j, k: (i, j)), _ep_store)


def _mm_tn_dual(name, a, b, pm=1024, pn=1024, pk=2048, dep=None):
    K, M = a.shape
    N = b.shape[1]
    tm, tn, tk = _tile(M, pm), _tile(N, pn), _tile(K, pk, 16)
    o_spec = BS((tm, tn), lambda i, j, k: (i, j))
    return _mm(name, a, b, BS((tk, tm), lambda i, j, k: (k, i)), BS((tk, tn), lambda i, j, k: (k, j)), _TN,
               (M // tm, N // tn, K // tk), (tm, tn), [S((M, N), _F32), S((M, N), _WIRE)], [o_spec, o_spec], _ep_store, dep=dep)


def _ffn_gate_up(h2, wgu):
    T, D = h2.shape
    F = wgu.shape[1] // 2
    tm, tn = _tile(T, 1024, 8), _tile(F, 512)
    nj = F // tn

    def body(a_ref, bg_ref, bu_ref, act_ref, gu_ref):
        a = a_ref[...]
        g = _dot(a, bg_ref[...])
        u = _dot(a, bu_ref[...])
        gu_ref[0] = g.astype(gu_ref.dtype)
        gu_ref[1] = u.astype(gu_ref.dtype)
        act_ref[...] = (g * (1.0 / (1.0 + jnp.exp(-g))) * u).astype(act_ref.dtype)

    return _call(
        body, "ffn_gate_up", grid=(T // tm, nj),
        in_specs=[BS((tm, D), lambda i, j: (i, 0)), BS((D, tn), lambda i, j: (0, j)), BS((D, tn), lambda i, j: (0, j + nj))],
        out_specs=[BS((tm, tn), lambda i, j: (i, j)), BS((2, tm, tn), lambda i, j: (0, i, j))],
        out_shape=[S((T, F), _MXU), S((2, T, F), _MXU)],
        compiler_params=_cp("parallel", "parallel"),
    )(h2, wgu, wgu)


def _ffn_dact(dxm, wdown, gu, dep=None):
    T, D = dxm.shape
    F = wdown.shape[0]
    tm, tn = _tile(T, 1024, 8), _tile(F, 512)
    deps = [] if dep is None else [dep]

    def body(a_ref, b_ref, gu_ref, *rest):
        o_ref = rest[-1]
        d = _dot(a_ref[...], b_ref[...], _NT)
        g, u = gu_ref[0].astype(_F32), gu_ref[1].astype(_F32)
        sg = 1.0 / (1.0 + jnp.exp(-g))
        o_ref[0] = (d * u * (sg * (1.0 + g * (1.0 - sg)))).astype(o_ref.dtype)
        o_ref[1] = (d * (g * sg)).astype(o_ref.dtype)

    return _call(
        body, "ffn_dact", grid=(T // tm, F // tn),
        in_specs=[BS((tm, D), lambda i, j: (i, 0)), BS((tn, D), lambda i, j: (j, 0)), BS((2, tm, tn), lambda i, j: (0, i, j))]
        + [_ANY] * len(deps),
        out_specs=BS((2, tm, tn), lambda i, j: (0, i, j)), out_shape=S((2, T, F), _MXU),
        compiler_params=_cp("parallel", "parallel"),
    )(dxm, wdown, gu, *deps)


def _ffn_dh(dgu, wgu, dep=None):
    _, T, F = dgu.shape
    D = wgu.shape[0]
    tm, tn, tk = _tile(T, 1024, 8), _tile(D, 2048), _tile(F, 2048)
    nkf = F // tk
    return _mm("ffn_dh", dgu, wgu, BS((None, tm, tk), lambda i, j, k: (k // nkf, i, k % nkf)),
               BS((tn, tk), lambda i, j, k: (j, k)), _NT, (T // tm, D // tn, 2 * nkf), (tm, tn),
               S((T, D), _F32), BS((tm, tn), lambda i, j, k: (i, j)), _ep_store, dep=dep)


def _ffn_dwgu(h2, dgu):
    _, T, F = dgu.shape
    D = h2.shape[1]
    tm, tn, tk = _tile(D, 1024), _tile(F, 1408), _tile(T, 2048, 16)
    njf = F // tn
    o_spec = BS((tm, tn), lambda i, j, k: (i, j))
    return _mm("ffn_dwgu", h2, dgu, BS((tk, tm), lambda i, j, k: (k, i)),
               BS((None, tk, tn), lambda i, j, k: (j // njf, k, j % njf)), _TN, (D // tm, 2 * njf, T // tk), (tm, tn),
               [S((D, 2 * F), _F32), S((D, 2 * F), _WIRE)], [o_spec, o_spec], _ep_store)


def _rmsnorm_fwd(x, g, name, dep=None):
    T, D = x.shape
    tr = _tile(T, 512, 8)

    def body(x_ref, g_ref, *rest):
        xv = x_ref[...]
        rest[-1][...] = (xv * _rstd(xv) * g_ref[...]).astype(rest[-1].dtype)

    row = BS((tr, D), lambda i: (i, 0))
    deps = [] if dep is None else [dep]
    return _call(body, name, grid=(T // tr,), in_specs=[row, BS((1, D), lambda i: (0, 0))] + [_ANY] * len(deps), out_specs=row,
                 out_shape=S((T, D), _MXU), compiler_params=_cp("parallel"))(x, g.reshape(1, D), *deps)


def _rmsnorm_bwd(x, g, dh, dres, name):
    T, D = x.shape
    tr = _tile(T, 256, 8)
    has_res = dres is not None

    def body(*refs):
        x_ref, g_ref, dh_ref = refs[:3]
        dx_ref, dxm_ref, dg_ref = refs[-3:]

        @pl.when(pl.program_id(0) == 0)
        def _():
            dg_ref[...] = jnp.zeros_like(dg_ref)

        xv = x_ref[...]
        r = _rstd(xv)
        xhat = xv * r
        dh_v = dh_ref[...]
        dg_ref[...] += jnp.sum(dh_v * xhat, axis=0, keepdims=True)
        dx = _norm_bwd(dh_v, xhat, r, g_ref[...])
        if has_res:
            dx = dx + refs[3][...]
        dx_ref[...] = dx
        dxm_ref[...] = dx.astype(dxm_ref.dtype)

    row = BS((tr, D), lambda i: (i, 0))
    vec = BS((1, D), lambda i: (0, 0))
    return _call(body, name, grid=(T // tr,), in_specs=[row, vec, row] + ([row] if has_res else []),
                 out_specs=[row, row, vec], out_shape=[S((T, D), _F32), S((T, D), _MXU), S((1, D), _F32)],
                 compiler_params=_cp("arbitrary"))(x, g.reshape(1, D), dh, *([dres] if has_res else []))


def _loss_head(y, target):
    T, D = y.shape
    tr = _tile(T, 256, 8)

    def body(y_ref, t_ref, dy_ref, dym_ref, acc_ref):
        @pl.when(pl.program_id(0) == 0)
        def _():
            acc_ref[...] = jnp.zeros_like(acc_ref)

        err = y_ref[...] - t_ref[...]
        acc_ref[...] += jnp.sum(jnp.sum(err * err, axis=-1, keepdims=True), axis=0, keepdims=True)
        dy = err * (1.0 / D)
        dy_ref[...] = dy
        dym_ref[...] = dy.astype(dym_ref.dtype)

    row = BS((tr, D), lambda i: (i, 0))
    return _call(body, "loss_head", grid=(T // tr,), in_specs=[row, row],
                 out_specs=[row, row, BS((1, 128), lambda i: (0, 0))],
                 out_shape=[S((T, D), _F32), S((T, D), _MXU), S((1, 128), _F32)],
                 compiler_params=_cp("arbitrary"))(y, target)


def _mixa_blocks(T):
    return 2 if T % (2 * CHUNK) == 0 else 1


def _mixer_a_fwd(z, gv, ws_m, b_t, TOK):
    T = z.shape[0]
    G = TOK // HEAD
    CB = _mixa_blocks(T)
    R = CB * CHUNK

    def body(z_ref, gv_ref, ws_ref, bt_ref, o_ref):
        u = _gelu(z_ref[:, :TOK])
        v = _gelu(z_ref[:, TOK:])
        vn = (v * _rstd(v) * gv_ref[...]).astype(_MXU)
        for c in range(CB):
            rows = slice(c * CHUNK, (c + 1) * CHUNK)
            for g in range(G):
                cols = slice(g * HEAD, (g + 1) * HEAD)
                s = _dot(ws_ref[g], vn[rows, cols]) + bt_ref[:, g:g + 1]
                o_ref[rows, cols] = (u[rows, cols] * s).astype(o_ref.dtype)

    return _call(
        body, "mixer_a_fwd", grid=(T // R,),
        in_specs=[BS((R, 2 * TOK), lambda i: (i, 0)), BS((1, TOK), lambda i: (0, 0)),
                  BS((G, CHUNK, CHUNK), lambda i: (0, 0, 0)), BS((CHUNK, G), lambda i: (0, 0))],
        out_specs=BS((R, TOK), lambda i: (i, 0)), out_shape=S((T, TOK), _MXU), compiler_params=_cp("parallel"),
    )(z, gv.reshape(1, TOK), ws_m, b_t)


def _mixer_a_bwd(z, dcat, gv, ws_m, wst_m, b_t, TOK):
    T = z.shape[0]
    G = TOK // HEAD
    CB = _mixa_blocks(T)
    R = CB * CHUNK
    n = T // R

    def body(z_ref, d_ref, gv_ref, ws_ref, wst_ref, bt_ref, dz_ref, dws_ref, db_ref, dgv_ref, dvn_scr):
        i = pl.program_id(0)

        @pl.when(i == 0)
        def _():
            dws_ref[...] = jnp.zeros_like(dws_ref)
            db_ref[...] = jnp.zeros_like(db_ref)
            dgv_ref[...] = jnp.zeros_like(dgv_ref)

        zu = z_ref[:, :TOK]
        zv = z_ref[:, TOK:]
        u = _gelu(zu)
        v = _gelu(zv)
        r = _rstd(v)
        vhat = v * r
        gvv = gv_ref[...]
        vn = (vhat * gvv).astype(_MXU)
        d = d_ref[...]
        gpu = _gelu_grad(zu)
        for c in range(CB):
            rows = slice(c * CHUNK, (c + 1) * CHUNK)
            for g in range(G):
                cols = slice(g * HEAD, (g + 1) * HEAD)
                vn_cg = vn[rows, cols]
                s = _dot(ws_ref[g], vn_cg) + bt_ref[:, g:g + 1]
                d_cg = d[rows, cols]
                dz_ref[rows, cols] = (d_cg * s * gpu[rows, cols]).astype(dz_ref.dtype)
                ds = d_cg * u[rows, cols]
                ds_m = ds.astype(_MXU)
                dvn_scr[rows, cols] = _dot(wst_ref[g], ds_m)
                dws_ref[g] += _dot(ds_m, vn_cg, _NT)
                db_ref[g] += ds
        dvn = dvn_scr[...]
        dgv_ref[...] += jnp.sum(dvn * vhat, axis=0, keepdims=True)
        dv = _norm_bwd(dvn, vhat, r, gvv)
        dz_ref[:, TOK:] = (dv * _gelu_grad(zv)).astype(dz_ref.dtype)

        @pl.when(i == n - 1)
        def _():
            for g in range(G):
                db_ref[g] = jnp.broadcast_to(jnp.sum(db_ref[g], axis=1, keepdims=True), (CHUNK, CHUNK))

    full3 = BS((G, CHUNK, CHUNK), lambda i: (0, 0, 0))
    return _call(
        body, "mixer_a_bwd", grid=(n,),
        in_specs=[BS((R, 2 * TOK), lambda i: (i, 0)), BS((R, TOK), lambda i: (i, 0)), BS((1, TOK), lambda i: (0, 0)),
                  full3, full3, BS((CHUNK, G), lambda i: (0, 0))],
        out_specs=[BS((R, 2 * TOK), lambda i: (i, 0)), full3, full3, BS((1, TOK), lambda i: (0, 0))],
        out_shape=[S((T, 2 * TOK), _MXU), S((G, CHUNK, CHUNK), _F32), S((G, CHUNK, CHUNK), _F32), S((1, TOK), _F32)],
        scratch_shapes=[pltpu.VMEM((R, TOK), _F32)], compiler_params=_cp("arbitrary"),
    )(z, dcat, gv.reshape(1, TOK), ws_m, wst_m, b_t)


def _rope_tables(T):
    n_rows = T // GRID_W
    rows = jnp.broadcast_to(jnp.arange(n_rows)[:, None], (n_rows, GRID_W)).reshape(T)
    cols = jnp.broadcast_to(jnp.arange(GRID_W)[None, :], (n_rows, GRID_W)).reshape(T)
    pairs = HEAD // 4
    freqs = ROPE_THETA ** (-jnp.arange(pairs, dtype=_F32) / pairs)
    ang_r = rows.astype(_F32)[:, None] * freqs
    ang_c = cols.astype(_F32)[:, None] * freqs
    ang = jnp.concatenate([ang_r, ang_r, ang_c, ang_c], axis=-1)
    cos, sin = jnp.cos(ang), jnp.sin(ang)
    first = (jnp.arange(HEAD) % (HEAD // 2)) < (HEAD // 4)
    return cos, jnp.where(first, -sin, 0.0), jnp.where(first, 0.0, sin)


def _rope(x, cs, sa, sb):
    return x * cs + pltpu.roll(x, 96, 1) * sa + pltpu.roll(x, 32, 1) * sb


def _qk_rope_fwd(z, gq, gk, tabs, TOK, KV):
    T = z.shape[0]
    R = _tile(T, 512, 8)
    W = TOK + 2 * KV

    def body(z_ref, gq_ref, gk_ref, cos_ref, sa_ref, sb_ref, q_ref, k_ref, v_ref):
        cs, sa, sb = cos_ref[...], sa_ref[...], sb_ref[...]
        for h in range((TOK + KV) // HEAD):
            cols = slice(h * HEAD, (h + 1) * HEAD)
            xv = z_ref[:, cols]
            xn = xv * _rstd(xv) * (gq_ref[...] if h < TOK // HEAD else gk_ref[...])
            out = _rope(xn, cs, sa, sb)
            if h < TOK // HEAD:
                q_ref[:, cols] = out.astype(q_ref.dtype)
            else:
                k_ref[:, h * HEAD - TOK:(h + 1) * HEAD - TOK] = out.astype(k_ref.dtype)
        v_ref[...] = z_ref[:, TOK + KV:].astype(v_ref.dtype)

    vec = BS((1, HEAD), lambda i: (0, 0))
    tab = BS((R, HEAD), lambda i: (i, 0))
    return _call(
        body, "qk_rope_fwd", grid=(T // R,), in_specs=[BS((R, W), lambda i: (i, 0)), vec, vec, tab, tab, tab],
        out_specs=[BS((R, TOK), lambda i: (i, 0)), BS((R, KV), lambda i: (i, 0)), BS((R, KV), lambda i: (i, 0))],
        out_shape=[S((T, TOK), _MXU), S((T, KV), _MXU), S((T, KV), _MXU)], compiler_params=_cp("parallel"),
    )(z, gq.reshape(1, HEAD), gk.reshape(1, HEAD), *tabs)


def _qk_rope_bwd(z, dq, dk, gq, gk, tabs, TOK, KV):
    T = z.shape[0]
    R = _tile(T, 512, 8)
    W = TOK + KV

    def body(z_ref, dq_ref, dk_ref, gq_ref, gk_ref, cos_ref, sa_ref, sb_ref, dz_ref, dgq_ref, dgk_ref):
        @pl.when(pl.program_id(0) == 0)
        def _():
            dgq_ref[...] = jnp.zeros_like(dgq_ref)
            dgk_ref[...] = jnp.zeros_like(dgk_ref)

        cs, sa, sb = cos_ref[...], sa_ref[...], sb_ref[...]
        for h in range(W // HEAD):
            cols = slice(h * HEAD, (h + 1) * HEAD)
            is_q = h < TOK // HEAD
            do = dq_ref[:, cols] if is_q else dk_ref[:, h * HEAD - TOK:(h + 1) * HEAD - TOK]
            dxn = do * cs - pltpu.roll(do, 96, 1) * sa - pltpu.roll(do, 32, 1) * sb
            xv = z_ref[:, cols]
            r = _rstd(xv)
            xhat = xv * r
            dg_ref = dgq_ref if is_q else dgk_ref
            dg_ref[...] += jnp.sum(dxn * xhat, axis=0, keepdims=True)
            dz_ref[:, cols] = _norm_bwd(dxn, xhat, r, gq_ref[...] if is_q else gk_ref[...]).astype(dz_ref.dtype)

    vec = BS((1, HEAD), lambda i: (0, 0))
    tab = BS((R, HEAD), lambda i: (i, 0))
    return _call(
        body, "qk_rope_bwd", grid=(T // R,),
        in_specs=[BS((R, W), lambda i: (i, 0)), BS((R, TOK), lambda i: (i, 0)), BS((R, KV), lambda i: (i, 0)), vec, vec, tab, tab, tab],
        out_specs=[BS((R, W), lambda i: (i, 0)), vec, vec],
        out_shape=[S((T, W), _MXU), S((1, HEAD), _F32), S((1, HEAD), _F32)], compiler_params=_cp("arbitrary"),
    )(z, dq, dk, gq.reshape(1, HEAD), gk.reshape(1, HEAD), *tabs)


_ATTN_C2 = float(HEAD ** -0.5 * np.log2(np.e))


def _attn_fwd(q, k, v, QPK):
    T, TOK = q.shape
    KVH = k.shape[1] // HEAD
    tq = _tile(T, 256, 8)
    W = QPK * HEAD

    def body(q_ref, k_ref, v_ref, o_ref, st_ref, vaug):
        @pl.when(pl.program_id(1) == 0)
        def _():
            vaug[:, :HEAD] = v_ref[...]
            vaug[:, HEAD:] = jnp.ones((T, HEAD), vaug.dtype)

        kk, va = k_ref[...], vaug[...]
        for g in range(QPK):
            cols = slice(g * HEAD, (g + 1) * HEAD)
            s = _dot(q_ref[:, cols], kk, _NT)
            m = jnp.max(s, axis=-1, keepdims=True)
            ov = _dot(jnp.exp2((s - m) * _ATTN_C2).astype(_MXU), va)
            l = ov[:, HEAD:HEAD + 1]
            o_ref[:, cols] = (ov[:, :HEAD] * (1.0 / l)).astype(o_ref.dtype)
            st_ref[:, g:g + 1] = m + jnp.log2(l) * (1.0 / _ATTN_C2)

    qs = BS((tq, W), lambda h, i: (i, h))
    ks = BS((T, HEAD), lambda h, i: (0, h))
    return _call(body, "attn_fwd", grid=(KVH, T // tq), in_specs=[qs, ks, ks],
                 out_specs=[qs, BS((None, tq, QPK), lambda h, i: (h, i, 0))],
                 out_shape=[S((T, TOK), _MXU), S((KVH, T, QPK), _F32)],
                 scratch_shapes=[pltpu.VMEM((T, 2 * HEAD), _MXU)],
                 compiler_params=_cp("parallel", "arbitrary"))(q, k, v)


def _attn_bwd(q, k, v, dcat, o, stat, QPK):
    T, TOK = q.shape
    KV = k.shape[1]
    KVH = KV // HEAD
    tq = _tile(T, 256, 8)
    nq = T // tq
    W = QPK * HEAD
    scale = HEAD ** -0.5

    def body(q_ref, k_ref, v_ref, do_ref, o_ref, st_ref, dq_ref, dk_ref, dv_ref, dk_acc, dv_acc):
        i = pl.program_id(1)

        @pl.when(i == 0)
        def _():
            dk_acc[...] = jnp.zeros_like(dk_acc)
            dv_acc[...] = jnp.zeros_like(dv_acc)

        kk, vv = k_ref[...], v_ref[...]
        for g in range(QPK):
            cols = slice(g * HEAD, (g + 1) * HEAD)
            qg = q_ref[:, cols]
            p = jnp.exp2((_dot(qg, kk, _NT) - st_ref[:, g:g + 1]) * _ATTN_C2)
            do32 = do_ref[:, cols]
            do = do32.astype(_MXU)
            delta = jnp.sum(do32 * o_ref[:, cols].astype(_F32), axis=-1, keepdims=True)
            ds = (p * (_dot(do, vv, _NT) - delta)).astype(_MXU)
            dq_ref[:, cols] = _dot(ds, kk) * scale
            dk_acc[...] += _dot(ds, qg, _TN)
            dv_acc[...] += _dot(p.astype(_MXU), do, _TN)

        @pl.when(i == nq - 1)
        def _():
            dk_ref[...] = dk_acc[...] * scale
            dv_ref[...] = dv_acc[...].astype(dv_ref.dtype)

    qs = BS((tq, W), lambda h, i: (i, h))
    ks = BS((T, HEAD), lambda h, i: (0, h))
    return _call(
        body, "attn_bwd", grid=(KVH, nq), in_specs=[qs, ks, ks, qs, qs, BS((None, tq, QPK), lambda h, i: (h, i, 0))],
        out_specs=[qs, ks, ks], out_shape=[S((T, TOK), _F32), S((T, KV), _F32), S((T, KV), _MXU)],
        scratch_shapes=[pltpu.VMEM((T, HEAD), _F32), pltpu.VMEM((T, HEAD), _F32)],
        compiler_params=_cp("parallel", "arbitrary"),
    )(q, k, v, dcat, o, stat)


def _mem_fwd(z, qblk, kv, gmq, gmk, MEMW):
    T = z.shape[0]
    NM = kv.shape[0]
    tq = _tile(T, 512, 8)
    scale = HEAD ** -0.5

    def body(q_ref, kv_ref, gq_ref, gk_ref, o_ref):
        for h in range(MEMW // HEAD):
            cols = slice(h * HEAD, (h + 1) * HEAD)
            kx = kv_ref[:, cols]
            kn = (kx * _rstd(kx) * gk_ref[...]).astype(_MXU)
            vv = kv_ref[:, MEMW + h * HEAD:MEMW + (h + 1) * HEAD].astype(_MXU)
            qx = q_ref[:, cols]
            qn = (qx * _rstd(qx) * gq_ref[...]).astype(_MXU)
            p = _softmax(_dot(qn, kn, _NT) * scale)
            o_ref[:, cols] = _dot(p.astype(_MXU), vv).astype(o_ref.dtype)

    vec = BS((1, HEAD), lambda i: (0, 0))
    return _call(
        body, "mem_fwd", grid=(T // tq,),
        in_specs=[BS((tq, MEMW), lambda i: (i, qblk)), BS((NM, 2 * MEMW), lambda i: (0, 0)), vec, vec],
        out_specs=BS((tq, MEMW), lambda i: (i, 0)), out_shape=S((T, MEMW), _MXU), compiler_params=_cp("parallel"),
    )(z, kv, gmq.reshape(1, HEAD), gmk.reshape(1, HEAD))


def _mem_bwd(z, qblk, kv, gmq, gmk, dcat, dblk, MEMW):
    T = z.shape[0]
    NM = kv.shape[0]
    tq = _tile(T, 512, 8)
    scale = HEAD ** -0.5

    def body(q_ref, kv_ref, gq_ref, gk_ref, do_ref, dz_ref, dkn_ref, dv_ref, dgq_ref):
        @pl.when(pl.program_id(0) == 0)
        def _():
            dkn_ref[...] = jnp.zeros_like(dkn_ref)
            dv_ref[...] = jnp.zeros_like(dv_ref)
            dgq_ref[...] = jnp.zeros_like(dgq_ref)

        for h in range(MEMW // HEAD):
            cols = slice(h * HEAD, (h + 1) * HEAD)
            kx = kv_ref[:, cols]
            kn = (kx * _rstd(kx) * gk_ref[...]).astype(_MXU)
            vv = kv_ref[:, MEMW + h * HEAD:MEMW + (h + 1) * HEAD].astype(_MXU)
            qx = q_ref[:, cols]
            rq = _rstd(qx)
            qhat = qx * rq
            qn = (qhat * gq_ref[...]).astype(_MXU)
            p = _softmax(_dot(qn, kn, _NT) * scale)
            do = do_ref[:, cols].astype(_MXU)
            dp = _dot(do, vv, _NT)
            ds = (p * (dp - jnp.sum(p * dp, axis=-1, keepdims=True)) * scale).astype(_MXU)
            dqn = _dot(ds, kn)
            dkn_ref[:, cols] += _dot(ds, qn, _TN)
            dv_ref[:, cols] += _dot(p.astype(_MXU), do, _TN)
            dgq_ref[...] += jnp.sum(dqn * qhat, axis=0, keepdims=True)
            dz_ref[:, cols] = _norm_bwd(dqn, qhat, rq, gq_ref[...]).astype(dz_ref.dtype)

    vec = BS((1, HEAD), lambda i: (0, 0))
    kvs = BS((NM, MEMW), lambda i: (0, 0))
    return _call(
        body, "mem_bwd", grid=(T // tq,),
        in_specs=[BS((tq, MEMW), lambda i: (i, qblk)), BS((NM, 2 * MEMW), lambda i: (0, 0)), vec, vec,
                  BS((tq, MEMW), lambda i: (i, dblk))],
        out_specs=[BS((tq, MEMW), lambda i: (i, 0)), kvs, kvs, vec],
        out_shape=[S((T, MEMW), _MXU), S((NM, MEMW), _F32), S((NM, MEMW), _F32), S((1, HEAD), _F32)],
        compiler_params=_cp("arbitrary"),
    )(z, kv, gmq.reshape(1, HEAD), gmk.reshape(1, HEAD), dcat)


def _memkv_bwd(kv, dkn, dv, gmk, MEMW):
    NM = kv.shape[0]

    def body(kv_ref, dkn_ref, dv_ref, gk_ref, dkv_ref, dgk_ref):
        dgk = jnp.zeros((1, HEAD), _F32)
        for h in range(MEMW // HEAD):
            cols = slice(h * HEAD, (h + 1) * HEAD)
            kx = kv_ref[:, cols]
            r = _rstd(kx)
            khat = kx * r
            dkn = dkn_ref[:, cols]
            dgk = dgk + jnp.sum(dkn * khat, axis=0, keepdims=True)
            dkv_ref[:, cols] = _norm_bwd(dkn, khat, r, gk_ref[...]).astype(dkv_ref.dtype)
        dgk_ref[...] = dgk
        dkv_ref[:, MEMW:] = dv_ref[...].astype(dkv_ref.dtype)

    return _call(body, "memkv_bwd", out_shape=[S((NM, 2 * MEMW), _MXU), S((1, HEAD), _F32)],
                 compiler_params=_cp())(kv, dkn, dv, gmk.reshape(1, HEAD))


def _cast_into_full(w, l, sh, idx, dep=None):
    tr, tc = _tile(sh.Rs, 512, 16), _tile(sh.Cs, 2048)
    nr, nc = sh.Rs // tr, sh.Cs // tc
    deps = [] if dep is None else [dep]

    def body(i_ref, c_ref, w_ref, *rest):
        rest[-1][...] = w_ref[...].astype(rest[-1].dtype)

    if sh.by_cols:
        o_map = lambda a, b, si, sc: (a, si[0] * nc + b)
    else:
        o_map = lambda a, b, si, sc: (si[0] * nr + a, b)
    return _call(
        body, "cast_into_full",
        grid_spec=pltpu.PrefetchScalarGridSpec(
            num_scalar_prefetch=2, grid=(nr, nc),
            in_specs=[BS((None, tr, tc), lambda a, b, si, sc: (l, a, b))] + [_ANY] * len(deps), out_specs=BS((tr, tc), o_map)),
        out_shape=S((sh.R, sh.C), _WIRE), compiler_params=_cp("parallel", "parallel"),
    )(*idx, w, *deps)


def _adamw(w, g, m, v, name, l0=0, l1=None, prev=None):
    L, R, C = w.shape
    l1 = L if l1 is None else l1
    tr, tc = _tile(R, 256, 8), _tile(C, 1024)
    c_m = 1.0 / (1.0 - ADAM_B1 ** ADAM_STEP)
    c_v = 1.0 / (1.0 - ADAM_B2 ** ADAM_STEP)

    def body(w_ref, g_ref, m_ref, v_ref, *rest):
        go_ref, d_ref, mo_ref, vo_ref = rest[-4:]
        gv = g_ref[...]
        mn = ADAM_B1 * m_ref[...] + (1.0 - ADAM_B1) * gv
        vn = ADAM_B2 * v_ref[...] + (1.0 - ADAM_B2) * (gv * gv)
        go_ref[...] = gv
        mo_ref[...] = mn
        vo_ref[...] = vn
        d_ref[...] = -ADAM_LR * ((mn * c_m) / (jnp.sqrt(vn * c_v) + ADAM_EPS) + ADAM_WD * w_ref[...])

    blk = BS((None, tr, tc), lambda a, i, j: (l0 + a, i, j))
    prevs = [] if prev is None else list(prev)
    return _call(body, name, grid=(l1 - l0, R // tr, C // tc), in_specs=[blk] * 4 + [_ANY] * len(prevs), out_specs=[blk] * 4,
                 out_shape=[S((L, R, C), _F32)] * 4, input_output_aliases={4 + k: k for k in range(len(prevs))},
                 compiler_params=_cp("parallel", "parallel", "parallel"))(w, g, m, v, *prevs)


def _where_am_i():
    x, y, c = lax.axis_index("x"), lax.axis_index("y"), lax.axis_index("c")
    chips = [(1 - x, y), (x, 1 - y), (1 - x, 1 - y)]
    return x, y, c, 2 * x + y, chips, [2 * cx + cy for cx, cy in chips]


class _Shard:
    def __init__(self, R, C, by_cols):
        self.R, self.C, self.by_cols = R, C, by_cols
        self.Rs, self.Cs = (R, C // 4) if by_cols else (R // 4, C)
        self.Rh = self.Rs // 2
        self.Q = R // 8

    def full_piece(self, ref, j, cc):
        if self.by_cols:
            return ref.at[pl.ds(cc * self.Rh, self.Rh), pl.ds(_mo(j * self.Cs, 128), self.Cs)]
        return ref.at[pl.ds(_mo(j * self.Rs + cc * self.Rh, 16), self.Rh), :]

    def full_shard(self, ref, j):
        if self.by_cols:
            return ref.at[:, pl.ds(_mo(j * self.Cs, 128), self.Cs)]
        return ref.at[pl.ds(_mo(j * self.Rs, 16), self.Rs), :]

    def shard_half(self, ref, cc):
        return ref.at[pl.ds(_mo(cc * self.Rh, 16), self.Rh), :]

    def half_piece(self, ref, j):
        if self.by_cols:
            return ref.at[:, pl.ds(_mo(j * self.Cs, 128), self.Cs)]
        return ref.at[pl.ds(_mo(j * self.Rh, 16), self.Rh), :]


def _remote(src, dst, ssem, rsem, dev):
    return pltpu.make_async_remote_copy(src_ref=src, dst_ref=dst, send_sem=ssem, recv_sem=rsem, device_id=dev, device_id_type=_MESH)


_HBM = pl.BlockSpec(memory_space=pltpu.HBM)
_SEM = pl.BlockSpec(memory_space=pltpu.SEMAPHORE)
_EFFECT = pltpu.SideEffectType.DATAFLOW_SIDE_EFFECTING


def _in_hbm(a):
    return pltpu.with_memory_space_constraint(a, pltpu.HBM)


def _gather_start(fulls, shs, name):
    n = len(fulls)

    def body(*refs):
        bufs = refs[:n]
        send_sems, recv_sems = refs[n], refs[n + 1]
        token = refs[-1]
        x, y, c, me, chips, chip_ids = _where_am_i()
        for t in range(n):
            mine = shs[t].full_piece(bufs[t], me, c)
            for r in range(3):
                _remote(mine, mine, send_sems.at[3 * t + r], recv_sems.at[3 * t + r], (*chips[r], c)).start()
        token[...] = jnp.zeros_like(token)

    out = pl.pallas_call(
        body, name=name, in_specs=[_HBM] * n,
        out_shape=(pltpu.SemaphoreType.DMA((3 * n,)), pltpu.SemaphoreType.DMA((3 * n,)), *[pltpu.HBM(f.shape, f.dtype) for f in fulls],
                   S((8, 128), _F32)),
        out_specs=(_SEM, _SEM, *[_HBM] * n, pl.BlockSpec(memory_space=pltpu.VMEM)),
        input_output_aliases={t: 2 + t for t in range(n)},
        compiler_params=pltpu.CompilerParams(has_side_effects=_EFFECT), **_KW,
    )(*[_in_hbm(f) for f in fulls])
    return out[0], out[1], list(out[2:2 + n]), out[-1]


def _gather_wait(fulls, send_sems, recv_sems, after, shs, name):
    n = len(fulls)

    def body(*refs):
        bufs = refs[:n]
        ssem, rsem = refs[n], refs[n + 1]
        x, y, c, me, chips, chip_ids = _where_am_i()
        for t in range(n):
            mine = shs[t].full_piece(bufs[t], me, c)
            for r in range(3):
                _remote(mine, mine, ssem.at[3 * t + r], rsem.at[3 * t + r], (*chips[r], c)).wait_send()
        for t in range(n):
            for r in range(3):
                piece = shs[t].full_piece(bufs[t], chip_ids[r], c)
                _remote(piece, piece, ssem.at[3 * t + r], rsem.at[3 * t + r], (*chips[r], c)).wait_recv()

    out = pl.pallas_call(
        body, name=name, in_specs=[*[_HBM] * n, _SEM, _SEM, _ANY], out_specs=[_HBM] * n,
        out_shape=[pltpu.HBM(f.shape, f.dtype) for f in fulls], input_output_aliases={t: t for t in range(n)},
        compiler_params=pltpu.CompilerParams(has_side_effects=_EFFECT), **_KW,
    )(*fulls, send_sems, recv_sems, after)
    return list(out)


def _gather_pass_on(fulls, shs, name):
    n = len(fulls)

    def body(*refs):
        bufs = refs[n:2 * n]
        send_sems, recv_sems = refs[2 * n:]
        x, y, c, me, chips, chip_ids = _where_am_i()
        sib = (x, y, 1 - c)
        cps = []
        for t in range(n):
            for r in range(3):
                piece = shs[t].full_piece(bufs[t], chip_ids[r], c)
                cps.append(_remote(piece, piece, send_sems.at[t, r], recv_sems.at[t, r], sib))
        for cp in cps:
            cp.start()
        for t in range(n):
            for r in range(3):
                piece = shs[t].full_piece(bufs[t], chip_ids[r], 1 - c)
                _remote(piece, piece, send_sems.at[t, r], recv_sems.at[t, r], sib).wait_recv()
        for cp in cps:
            cp.wait_send()

    return _call(
        body, name, in_specs=[_ANY] * n, out_specs=[_ANY] * n, out_shape=[S(f.shape, f.dtype) for f in fulls],
        input_output_aliases={t: t for t in range(n)},
        scratch_shapes=[pltpu.SemaphoreType.DMA((n, 3)), pltpu.SemaphoreType.DMA((n, 3))],
        compiler_params=pltpu.CompilerParams(has_side_effects=True),
    )(*fulls)


def _pass_on_copies(bufs, shs, send_sems, recv_sems):
    x, y, c, me, chips, chip_ids = _where_am_i()
    sib = (x, y, 1 - c)
    out, back = [], []
    for t in range(len(bufs)):
        for r in range(3):
            piece = shs[t].full_piece(bufs[t], chip_ids[r], c)
            out.append(_remote(piece, piece, send_sems.at[3 * t + r], recv_sems.at[3 * t + r], sib))
            other = shs[t].full_piece(bufs[t], chip_ids[r], 1 - c)
            back.append(_remote(other, other, send_sems.at[3 * t + r], recv_sems.at[3 * t + r], sib))
    return out, back


def _pass_on_start(fulls, shs, name):
    n = len(fulls)

    def body(*refs):
        for cp in _pass_on_copies(refs[:n], shs, refs[n], refs[n + 1])[0]:
            cp.start()
        refs[-1][...] = jnp.zeros_like(refs[-1])

    out = pl.pallas_call(
        body, name=name, in_specs=[_HBM] * n,
        out_shape=(pltpu.SemaphoreType.DMA((3 * n,)), pltpu.SemaphoreType.DMA((3 * n,)), *[pltpu.HBM(f.shape, f.dtype) for f in fulls],
                   S((8, 128), _F32)),
        out_specs=(_SEM, _SEM, *[_HBM] * n, pl.BlockSpec(memory_space=pltpu.VMEM)),
        input_output_aliases={t: 2 + t for t in range(n)},
        compiler_params=pltpu.CompilerParams(has_side_effects=_EFFECT), **_KW,
    )(*[_in_hbm(f) for f in fulls])
    return out[0], out[1], list(out[2:2 + n]), out[-1]


def _pass_on_wait(fulls, send_sems, recv_sems, after, shs, name):
    n = len(fulls)

    def body(*refs):
        out, back = _pass_on_copies(refs[:n], shs, refs[n], refs[n + 1])
        for cp in out:
            cp.wait_send()
        for cp in back:
            cp.wait_recv()

    out = pl.pallas_call(
        body, name=name, in_specs=[*[_HBM] * n, _SEM, _SEM, _ANY], out_specs=[_HBM] * n,
        out_shape=[pltpu.HBM(f.shape, f.dtype) for f in fulls], input_output_aliases={t: t for t in range(n)},
        compiler_params=pltpu.CompilerParams(has_side_effects=_EFFECT), **_KW,
    )(*fulls, send_sems, recv_sems, after)
    return list(out)


def _rs_pair_copies(ins, outs, shs, send_sems, recv_sems):
    x, y, c, *_ = _where_am_i()
    sib = (x, y, 1 - c)
    cps = []
    for t in range(len(ins)):
        sh = shs[t]
        if sh.by_cols:
            cps.append(_remote(ins[t].at[pl.ds((1 - c) * sh.Rh, sh.Rh), :], outs[t], send_sems.at[4 * t], recv_sems.at[4 * t], sib))
        else:
            for j in range(4):
                cps.append(_remote(sh.full_piece(ins[t], j, 1 - c), sh.half_piece(outs[t], j),
                                   send_sems.at[4 * t + j], recv_sems.at[4 * t + j], sib))
    return cps


def _rs_pair_start(dws, shs, name):
    n = len(dws)
    lands = [lax.empty((sh.R // 2, sh.C), _WIRE) for sh in shs]

    def body(*refs):
        for cp in _rs_pair_copies(refs[:n], refs[n:2 * n], shs, refs[2 * n], refs[2 * n + 1]):
            cp.start()
        refs[-1][...] = jnp.zeros_like(refs[-1])

    out = pl.pallas_call(
        body, name=name, in_specs=[_HBM] * (2 * n),
        out_shape=(pltpu.SemaphoreType.DMA((4 * n,)), pltpu.SemaphoreType.DMA((4 * n,)),
                   *[pltpu.HBM(a.shape, a.dtype) for a in (*dws, *lands)], S((8, 128), _F32)),
        out_specs=(_SEM, _SEM, *[_HBM] * (2 * n), pl.BlockSpec(memory_space=pltpu.VMEM)),
        input_output_aliases={t: 2 + t for t in range(2 * n)},
        compiler_params=pltpu.CompilerParams(has_side_effects=_EFFECT), **_KW,
    )(*[_in_hbm(a) for a in (*dws, *lands)])
    return out[0], out[1], list(out[2:2 + n]), list(out[2 + n:2 + 2 * n]), out[-1]


def _rs_pair_wait(dws, lands, send_sems, recv_sems, after, shs, name):
    n = len(dws)

    def body(*refs):
        cps = _rs_pair_copies(refs[:n], refs[n:2 * n], shs, refs[2 * n], refs[2 * n + 1])
        for cp in cps:
            cp.wait_send()
        for cp in cps:
            cp.wait_recv()

    out = pl.pallas_call(
        body, name=name, in_specs=[*[_HBM] * (2 * n), _SEM, _SEM, _ANY], out_specs=[_HBM] * (2 * n),
        out_shape=[pltpu.HBM(a.shape, a.dtype) for a in (*dws, *lands)], input_output_aliases={t: t for t in range(2 * n)},
        compiler_params=pltpu.CompilerParams(has_side_effects=_EFFECT), **_KW,
    )(*dws, *lands, send_sems, recv_sems, after)
    return list(out[n:])


def _rs_pair_add(dw32, recv, sh, idx):
    tr, tc = _tile(sh.Q, 512, 16), _tile(sh.C, 2048)
    nb = sh.Q // tr

    def body(i_ref, c_ref, a_ref, b_ref, o32_ref, ow_ref):
        p = a_ref[...] + b_ref[...].astype(_F32)
        o32_ref[...] = p
        ow_ref[...] = p.astype(ow_ref.dtype)

    if sh.by_cols:
        a_map = lambda j, i, b, si, sc: (sc[0] * 4 * nb + j * nb + i, b)
    else:
        a_map = lambda j, i, b, si, sc: (j * 2 * nb + sc[0] * nb + i, b)
    h_spec = BS((tr, tc), lambda j, i, b, si, sc: (j * nb + i, b))
    return _call(
        body, "rs_pair_add",
        grid_spec=pltpu.PrefetchScalarGridSpec(num_scalar_prefetch=2, grid=(4, nb, sh.C // tc),
                                               in_specs=[BS((tr, tc), a_map), h_spec], out_specs=[h_spec, h_spec]),
        out_shape=[S((sh.R // 2, sh.C), _F32), S((sh.R // 2, sh.C), _WIRE)],
        compiler_params=_cp("parallel", "parallel", "parallel"),
    )(*idx, dw32, recv)


def _rs_chip_start(pws, shs, name):
    n = len(pws)
    lands = [lax.empty((3, sh.Rh, sh.Cs), _WIRE) for sh in shs]

    def body(*refs):
        ins, lnd = refs[:n], refs[n:2 * n]
        send_sems, recv_sems = refs[2 * n], refs[2 * n + 1]
        token = refs[-1]
        x, y, c, me, chips, chip_ids = _where_am_i()
        for t in range(n):
            for r in range(3):
                _remote(shs[t].half_piece(ins[t], chip_ids[r]), lnd[t].at[r], send_sems.at[3 * t + r], recv_sems.at[3 * t + r],
                        (*chips[r], c)).start()
        token[...] = jnp.zeros_like(token)

    out = pl.pallas_call(
        body, name=name, in_specs=[_HBM] * (2 * n),
        out_shape=(pltpu.SemaphoreType.DMA((3 * n,)), pltpu.SemaphoreType.DMA((3 * n,)),
                   *[pltpu.HBM(a.shape, a.dtype) for a in (*pws, *lands)], S((8, 128), _F32)),
        out_specs=(_SEM, _SEM, *[_HBM] * (2 * n), pl.BlockSpec(memory_space=pltpu.VMEM)),
        input_output_aliases={t: 2 + t for t in range(2 * n)},
        compiler_params=pltpu.CompilerParams(has_side_effects=_EFFECT), **_KW,
    )(*[_in_hbm(a) for a in (*pws, *lands)])
    return out[0], out[1], list(out[2:2 + n]), list(out[2 + n:2 + 2 * n]), out[-1]


def _rs_chip_wait(pws, lands, send_sems, recv_sems, after, shs, name):
    n = len(pws)
    after = list(after) if isinstance(after, (list, tuple)) else [after]

    def body(*refs):
        ins, lnd = refs[:n], refs[n:2 * n]
        ssem, rsem = refs[2 * n], refs[2 * n + 1]
        x, y, c, me, chips, chip_ids = _where_am_i()
        for t in range(n):
            for r in range(3):
                cp = _remote(shs[t].half_piece(ins[t], chip_ids[r]), lnd[t].at[r], ssem.at[3 * t + r], rsem.at[3 * t + r], (*chips[r], c))
                cp.wait_send()
        for t in range(n):
            for r in range(3):
                cp = _remote(shs[t].half_piece(ins[t], chip_ids[r]), lnd[t].at[r], ssem.at[3 * t + r], rsem.at[3 * t + r], (*chips[r], c))
                cp.wait_recv()

    out = pl.pallas_call(
        body, name=name, in_specs=[*[_HBM] * (2 * n), _SEM, _SEM, *[_ANY] * len(after)], out_specs=[_HBM] * (2 * n),
        out_shape=[pltpu.HBM(a.shape, a.dtype) for a in (*pws, *lands)], input_output_aliases={t: t for t in range(2 * n)},
        compiler_params=pltpu.CompilerParams(has_side_effects=_EFFECT), **_KW,
    )(*pws, *lands, send_sems, recv_sems, *after)
    return list(out[n:])


def _rs_chip_add(p32, recv, sh, idx, g_prev, l, L):
    tr, tc = _tile(sh.Rh, 512, 16), _tile(sh.Cs, 2048)
    nr, nc = sh.Rh // tr, sh.Cs // tc

    def body(i_ref, c_ref, a_ref, b_ref, *rest):
        rest[-1][...] = a_ref[...] + b_ref[0].astype(_F32) + b_ref[1].astype(_F32) + b_ref[2].astype(_F32)

    if sh.by_cols:
        a_map = lambda a, b, si, sc: (a, si[0] * nc + b)
    else:
        a_map = lambda a, b, si, sc: (si[0] * nr + a, b)
    in_specs = [BS((tr, tc), a_map), BS((3, tr, tc), lambda a, b, si, sc: (0, a, b))]
    args = [*idx, p32, recv]
    if g_prev is not None:
        in_specs.append(_ANY)
        args.append(g_prev)
    return _call(
        body, "rs_chip_add",
        grid_spec=pltpu.PrefetchScalarGridSpec(num_scalar_prefetch=2, grid=(nr, nc), in_specs=in_specs,
                                               out_specs=BS((None, tr, tc), lambda a, b, si, sc: (l, sc[0] * nr + a, b))),
        out_shape=S((L, sh.Rs, sh.Cs), _F32), input_output_aliases={} if g_prev is None else {4: 0},
        compiler_params=_cp("parallel", "parallel"),
    )(*args)


def _rs_pair_share(gs, ls, shs, name):
    n = len(gs)

    def body(*refs):
        bufs = refs[n:2 * n]
        send_sems, recv_sems = refs[2 * n:]
        x, y, c, *_ = _where_am_i()
        sib = (x, y, 1 - c)
        cps = []
        for t in range(n):
            mine = shs[t].shard_half(bufs[t].at[ls[t]], c)
            cps.append(_remote(mine, mine, send_sems.at[t], recv_sems.at[t], sib))
        for cp in cps:
            cp.start()
        for t in range(n):
            other = shs[t].shard_half(bufs[t].at[ls[t]], 1 - c)
            _remote(other, other, send_sems.at[t], recv_sems.at[t], sib).wait_recv()
        for cp in cps:
            cp.wait_send()

    return _call(
        body, name, in_specs=[_ANY] * n, out_specs=[_ANY] * n, out_shape=[S(g.shape, g.dtype) for g in gs],
        input_output_aliases={t: t for t in range(n)},
        scratch_shapes=[pltpu.SemaphoreType.DMA((n,)), pltpu.SemaphoreType.DMA((n,))],
        compiler_params=pltpu.CompilerParams(has_side_effects=True),
    )(*gs)


def _all_reduce_small(xs, dep=None):
    M = xs.shape[0]
    deps = [] if dep is None else [dep]

    def body(x_ref, *rest):
        tot_ref, out_ref, send_sems, recv_sems, local_sem = rest[len(deps):]
        x, y, c, me, chips, chip_ids = _where_am_i()
        sib = (x, y, 1 - c)

        def rows(dev):
            return out_ref.at[pl.ds(_mo((4 * dev[0] + 2 * dev[1] + dev[2]) * M, 8), M), :]

        def copy(k, block, to, src=None):
            return _remote(rows(block) if src is None else src, rows(block), send_sems.at[k], recv_sems.at[k], to)

        mine = pltpu.make_async_copy(x_ref, rows((x, y, c)), local_sem)
        mine.start()
        first = [copy(0, (x, y, c), sib, src=x_ref)]
        first += [copy(1 + j, (x, y, c), (*chip, c), src=x_ref) for j, chip in enumerate(chips)]
        for cp in first:
            cp.start()
        passed = [copy(4 + j, (*chip, c), sib) for j, chip in enumerate(chips)]
        for j, chip in enumerate(chips):
            copy(1 + j, (*chip, c), (x, y, c)).wait_recv()
            passed[j].start()
        copy(0, sib, (x, y, c)).wait_recv()
        for j, chip in enumerate(chips):
            copy(4 + j, (*chip, 1 - c), (x, y, c)).wait_recv()
        for cp in first + passed:
            cp.wait_send()
        mine.wait()
        tot = out_ref[pl.ds(0, M), :]
        for d in range(1, 8):
            tot = tot + out_ref[pl.ds(d * M, M), :]
        tot_ref[...] = tot

    vm = pl.BlockSpec(memory_space=pltpu.VMEM)
    return _call(
        body, "all_reduce_small", in_specs=[vm] + [_ANY] * len(deps), out_specs=[vm, vm],
        out_shape=[S((M, 128), _F32), S((8 * M, 128), _F32)],
        scratch_shapes=[pltpu.SemaphoreType.DMA((7,)), pltpu.SemaphoreType.DMA((7,)), pltpu.SemaphoreType.DMA],
        compiler_params=_cp(has_side_effects=True),
    )(xs, *deps)[0]


def _reduce_scatter_begin(dws, shs, l):
    ssem, rsem, dww, lands, token = _rs_pair_start([d[1] for d in dws], shs, f"rs_pair_start_{l}")
    return ([d[0] for d in dws], dww, lands, ssem, rsem), token


def _reduce_scatter_middle(state, after, shs, idx, l):
    dw32s, dww, lands, ssem, rsem = state
    recv_a = _rs_pair_wait(dww, lands, ssem, rsem, after, shs, f"rs_pair_wait_{l}")
    p = [_rs_pair_add(d32, ra, sh, idx) for d32, ra, sh in zip(dw32s, recv_a, shs)]
    ssem, rsem, pws, lands, token = _rs_chip_start([pw for _, pw in p], shs, f"rs_chip_start_{l}")
    return ([p32 for p32, _ in p], pws, lands, ssem, rsem), token


def _reduce_scatter_end(state, after, tensors, shs, gstack, idx, l):
    p32s, pws, lands, ssem, rsem = state
    recv_b = _rs_chip_wait(pws, lands, ssem, rsem, after, shs, f"rs_chip_wait_{l}")
    gs = [_rs_chip_add(p32, rb, sh, idx, gstack[name], i, L) for p32, rb, sh, (name, i, L) in zip(p32s, recv_b, shs, tensors)]
    gs = _rs_pair_share(gs, [i for _, i, _ in tensors], shs, "rs_pair_share")
    for (name, _, _), g in zip(tensors, gs):
        gstack[name] = g


def _pack(parts):
    out = []
    for p in parts:
        p2 = p.reshape(-1, 128)
        pad = (-p2.shape[0]) % 8
        out.append(jnp.pad(p2, ((0, pad), (0, 0))) if pad else p2)
    return jnp.concatenate(out, axis=0)


def _unpack(packed, like):
    out, at = [], 0
    for p in like:
        n = p.size // 128
        out.append(packed[at:at + n].reshape(p.shape))
        at += n + ((-n) % 8)
    return out


def kernel(x, mem, g_mix, g_ffn, w_in_a, g_v_a, w_spatial, b_spatial, w_in_b, g_q_b, g_k_b, g_mem, w_mem_kv, g_mq, g_mk, w_out, w_gate_up, w_down, loss_target, m_g_mix, m_g_ffn, m_w_in_a, m_g_v_a, m_w_spatial, m_b_spatial, m_w_in_b, m_g_q_b, m_g_k_b, m_g_mem, m_w_mem_kv, m_g_mq, m_g_mk, m_w_out, m_w_gate_up, m_w_down, v_g_mix, v_g_ffn, v_w_in_a, v_g_v_a, v_w_spatial, v_b_spatial, v_w_in_b, v_g_q_b, v_g_k_b, v_g_mem, v_w_mem_kv, v_g_mq, v_g_mk, v_w_out, v_w_gate_up, v_w_down):
    xs = x[0]
    mem2 = mem[0]
    target = loss_target[0]
    T, D = xs.shape
    depth = g_mix.shape[0]
    MEMW = w_mem_kv.shape[2] // 2
    TOK = D - MEMW
    KV = (w_in_b.shape[2] * 4 - TOK - MEMW) // 2
    QPK = TOK // KV
    F = w_gate_up.shape[2] * 4 // 2

    idx = ((2 * lax.axis_index("x") + lax.axis_index("y")).astype(jnp.int32).reshape(1), lax.axis_index("c").astype(jnp.int32).reshape(1))

    big = {
        "w_in_a": (w_in_a, _Shard(D, w_in_a.shape[2] * 4, True)),
        "w_in_b": (w_in_b, _Shard(D, w_in_b.shape[2] * 4, True)),
        "w_mem_kv": (w_mem_kv, _Shard(D, 2 * MEMW, False)),
        "w_out": (w_out, _Shard(D, D, False)),
        "w_gate_up": (w_gate_up, _Shard(D, 2 * F, True)),
        "w_down": (w_down, _Shard(F, D, False)),
    }

    def layer_tensors(l):
        n_in = "w_in_a" if l % 2 == 0 else "w_in_b"
        return [(n_in, l // 2, big[n_in][0].shape[0])] + [(n, l, depth) for n in ("w_mem_kv", "w_out", "w_gate_up", "w_down")]

    def layer_shards(l):
        return [big[n][1] for n, _, _ in layer_tensors(l)]

    full = {n: [None] * w.shape[0] for n, (w, _) in big.items()}
    flying = {}

    def start_layer(l, dep):
        tens = layer_tensors(l)
        token = dep
        for gi, group in enumerate([tens[:2], tens[2:3], tens[3:4], tens[4:]] if l == 0 else [tens]):
            shs = [big[n][1] for n, _, _ in group]
            bufs = [_cast_into_full(big[n][0], i, big[n][1], idx, dep=token) for n, i, _ in group]
            ssem, rsem, bufs, token = _gather_start(bufs, shs, f"gather_start_{l}_{gi}")
            for n, i, _ in group:
                flying[(n, i)] = dict(group=group, shs=shs, state=(ssem, rsem, bufs), name=f"{l}_{gi}", passing=False)
        return token

    def land(n, i, after):
        fl = flying[(n, i)]
        ssem, rsem, bufs = fl["state"]
        bufs = _gather_wait(bufs, ssem, rsem, after, fl["shs"], "gather_wait_" + fl["name"])
        ssem, rsem, bufs, token = _pass_on_start(bufs, fl["shs"], "pass_on_start_" + fl["name"])
        fl.update(state=(ssem, rsem, bufs), passing=True)
        return token

    def weight(n, i, after):
        if full[n][i] is None:
            fl = flying[(n, i)]
            ssem, rsem, bufs = fl["state"]
            if fl["passing"]:
                bufs = _pass_on_wait(bufs, ssem, rsem, after, fl["shs"], "pass_on_wait_" + fl["name"])
            else:
                bufs = _gather_wait(bufs, ssem, rsem, after, fl["shs"], "gather_wait_" + fl["name"])
                bufs = _gather_pass_on(bufs, fl["shs"], "gather_pass_on")
            for (m, j, _), b in zip(fl["group"], bufs):
                full[m][j] = b
        return full[n][i]

    after = start_layer(0, None)
    tabs = _rope_tables(T)

    saved = []
    xc = xs
    for l in range(depth):
        is_a = l % 2 == 0
        li = l // 2
        w_in = weight("w_in_a" if is_a else "w_in_b", li, after)
        token = start_layer(l + 1, w_in) if l + 1 < depth else None
        h = _rmsnorm_fwd(xc, g_mix[l], "rmsnorm_fwd", dep=token)
        z = _mm_nn("mm_in", h, w_in, pm=2048, pn=512)
        st = dict(x=xc, h=h, z=z)
        if is_a:
            ws_m = w_spatial[li].astype(_MXU)
            st["ws_m"], st["wst_m"], st["b_t"] = ws_m, jnp.swapaxes(ws_m, 1, 2), b_spatial[li].T
            tok = _mixer_a_fwd(z, g_v_a[li], ws_m, st["b_t"], TOK)
            qblk = 2 * TOK // MEMW
        else:
            q, k, v = _qk_rope_fwd(z, g_q_b[li], g_k_b[li], tabs, TOK, KV)
            tok, stat = _attn_fwd(q, k, v, QPK)
            st["q"], st["k"], st["v"], st["stat"] = q, k, v, stat
            qblk = (TOK + 2 * KV) // MEMW
        mem_n = _rmsnorm_fwd(mem2, g_mem[l], "rmsnorm_mem")
        kv = _mm_nn("mm_memkv", mem_n, weight("w_mem_kv", l, z))
        mo = _mem_fwd(z, qblk, kv, g_mq[l], g_mk[l], MEMW)
        cat = jnp.concatenate([tok, mo], axis=1)
        x1 = _mm_nn("mm_out", cat, weight("w_out", l, cat), add=xc)
        h2 = _rmsnorm_fwd(x1, g_ffn[l], "rmsnorm_fwd")
        act, gu = _ffn_gate_up(h2, weight("w_gate_up", l, h2))
        w_down_l = weight("w_down", l, act)
        token = land(*layer_tensors(l + 1)[0][:2], act) if l + 1 < depth else None
        xc = _mm_nn("mm_down", act, w_down_l, add=x1, pm=512, pn=512, pk=8192, dep=token)
        after = xc
        st.update(mem_n=mem_n, kv=kv, qblk=qblk, cat=cat, x1=x1, h2=h2, act=act, gu=gu)
        saved.append(st)

    dx, dxm, sq = _loss_head(xc, target)
    loss = lax.psum(sq[0, 0] * (0.5 / D), ("x", "y", "c"))

    gsm = {n: [None] * len(a) for n, a in dict(g_mix=g_mix, g_ffn=g_ffn, g_v_a=g_v_a, w_spatial=w_spatial, b_spatial=b_spatial,
                                                g_q_b=g_q_b, g_k_b=g_k_b, g_mem=g_mem, g_mq=g_mq, g_mk=g_mk).items()}
    gstack = {n: None for n in big}
    pairing, chipping, token = None, None, None

    def advance(after):
        nonlocal pairing, chipping
        state, tok = _reduce_scatter_middle(pairing[0], after, layer_shards(pairing[1]), idx, pairing[1])
        if chipping is not None:
            _reduce_scatter_end(chipping[0], tok, layer_tensors(chipping[1]), layer_shards(chipping[1]), gstack, idx, chipping[1])
        pairing, chipping = None, (state, pairing[1])
        return tok

    for l in reversed(range(depth)):
        st = saved[l]
        is_a = l % 2 == 0
        li = l // 2
        gbig = {}
        dgu = _ffn_dact(dxm, full["w_down"][l], st["gu"], dep=token)
        token = advance(dgu) if pairing is not None else None
        gbig["w_down"] = _mm_tn_dual("mm_dw_down", st["act"], dxm, pm=1408)
        dh2 = _ffn_dh(dgu, full["w_gate_up"][l], dep=token)
        gbig["w_gate_up"] = _ffn_dwgu(st["h2"], dgu)
        dx, dxm, dg = _rmsnorm_bwd(st["x1"], g_ffn[l], dh2, dx, "rmsnorm_bwd")
        gsm["g_ffn"][l] = dg[0]
        dcat = _mm_nt("mm_dcat", dxm, full["w_out"][l])
        gbig["w_out"] = _mm_tn_dual("mm_dw_out", st["cat"], dxm)
        dzq, dkn, dvm, dgq = _mem_bwd(st["z"], st["qblk"], st["kv"], g_mq[l], g_mk[l], dcat, TOK // MEMW, MEMW)
        dkv, dgk = _memkv_bwd(st["kv"], dkn, dvm, g_mk[l], MEMW)
        gsm["g_mq"][l], gsm["g_mk"][l] = dgq[0], dgk[0]
        gbig["w_mem_kv"] = _mm_tn_dual("mm_dw_memkv", st["mem_n"], dkv)
        dmem_n = _mm_nt("mm_dmemn", dkv, full["w_mem_kv"][l])
        gsm["g_mem"][l] = _rmsnorm_bwd(mem2, g_mem[l], dmem_n, None, "rmsnorm_bwd_mem")[2][0]
        if is_a:
            dz_tok, dws, dbs, dgv = _mixer_a_bwd(st["z"], dcat, g_v_a[li], st["ws_m"], st["wst_m"], st["b_t"], TOK)
            gsm["w_spatial"][li], gsm["b_spatial"][li], gsm["g_v_a"][li] = dws, dbs[:, :, 0], dgv[0]
            dz = jnp.concatenate([dz_tok, dzq], axis=1)
        else:
            dq, dk, dv = _attn_bwd(st["q"], st["k"], st["v"], dcat, st["cat"], st["stat"], QPK)
            dz_qk, dgq_b, dgk_b = _qk_rope_bwd(st["z"], dq, dk, g_q_b[li], g_k_b[li], tabs, TOK, KV)
            gsm["g_q_b"][li], gsm["g_k_b"][li] = dgq_b[0], dgk_b[0]
            dz = jnp.concatenate([dz_qk, dv, dzq], axis=1)
        n_in = "w_in_a" if is_a else "w_in_b"
        dh = _mm_nt("mm_dh", dz, full[n_in][li])
        gbig[n_in] = _mm_tn_dual("mm_dw_in", st["h"], dz, pm=2048, pn=512)
        dx, dxm, dg = _rmsnorm_bwd(st["x"], g_mix[l], dh, dx, "rmsnorm_bwd")
        gsm["g_mix"][l] = dg[0]
        state, token = _reduce_scatter_begin([gbig[n] for n, _, _ in layer_tensors(l)], layer_shards(l), l)
        pairing = (state, l)

    small = ["g_mix", "g_ffn", "g_v_a", "w_spatial", "b_spatial", "g_q_b", "g_k_b", "g_mem", "g_mq", "g_mk"]
    env = dict(g_mix=g_mix, g_ffn=g_ffn, g_v_a=g_v_a, w_spatial=w_spatial, b_spatial=b_spatial, g_q_b=g_q_b, g_k_b=g_k_b,
               g_mem=g_mem, g_mq=g_mq, g_mk=g_mk,
               m_g_mix=m_g_mix, m_g_ffn=m_g_ffn, m_g_v_a=m_g_v_a, m_w_spatial=m_w_spatial, m_b_spatial=m_b_spatial,
               m_g_q_b=m_g_q_b, m_g_k_b=m_g_k_b, m_g_mem=m_g_mem, m_g_mq=m_g_mq, m_g_mk=m_g_mk,
               v_g_mix=v_g_mix, v_g_ffn=v_g_ffn, v_g_v_a=v_g_v_a, v_w_spatial=v_w_spatial, v_b_spatial=v_b_spatial,
               v_g_q_b=v_g_q_b, v_g_k_b=v_g_k_b, v_g_mem=v_g_mem, v_g_mq=v_g_mq, v_g_mk=v_g_mk,
               m_w_in_a=m_w_in_a, m_w_in_b=m_w_in_b, m_w_mem_kv=m_w_mem_kv, m_w_out=m_w_out, m_w_gate_up=m_w_gate_up, m_w_down=m_w_down,
               v_w_in_a=v_w_in_a, v_w_in_b=v_w_in_b, v_w_mem_kv=v_w_mem_kv, v_w_out=v_w_out, v_w_gate_up=v_w_gate_up, v_w_down=v_w_down)
    like = [env[n] for n in small]
    g_small = _all_reduce_small(_pack([jnp.stack(gsm[n]) for n in small]), dep=token)

    res = {}
    outs = _adamw(_pack(like)[None], g_small[None], _pack([env["m_" + n] for n in small])[None],
                  _pack([env["v_" + n] for n in small])[None], "adamw_small")
    unpacked = [_unpack(o[0], like) for o in outs]
    for k, n in enumerate(small):
        res[n] = [u[k] for u in unpacked]
    advance(outs[1])
    pending = chipping
    last = {n: i for n, i, _ in layer_tensors(pending[1])}
    early = {}
    for n, (w, _) in big.items():
        L = w.shape[0]
        if n not in last:
            res[n] = _adamw(w, gstack[n], env["m_" + n], env["v_" + n], "adamw_" + n)
        elif L > 1:
            assert last[n] == 0
            early[n] = _adamw(w, gstack[n], env["m_" + n], env["v_" + n], "adamw_early_" + n, l0=1)
    done = [o[1] for o in early.values()] + [res[n][1] for n in big if n in res] + [res[small[0]][1]]
    _reduce_scatter_end(pending[0], done, layer_tensors(pending[1]), layer_shards(pending[1]), gstack, idx, pending[1])
    for n in last:
        res[n] = _adamw(big[n][0], gstack[n], env["m_" + n], env["v_" + n], "adamw_last_" + n, l0=0, l1=1, prev=early.get(n))

    order = ["g_mix", "g_ffn", "w_in_a", "g_v_a", "w_spatial", "b_spatial", "w_in_b", "g_q_b", "g_k_b", "g_mem", "w_mem_kv",
             "g_mq", "g_mk", "w_out", "w_gate_up", "w_down"]
    return (loss, dx.reshape(1, T, D), *[res[n][0] for n in order], *[res[n][1] for n in order],
            *[res[n][2] for n in order], *[res[n][3] for n in order])
```

```python
import jax
import jax.numpy as jnp
import numpy as np
from jax import lax
from jax.experimental import pallas as pl
from jax.experimental.pallas import tpu as pltpu

_F32 = jnp.float32
_MXU = jnp.bfloat16
_WIRE = jnp.bfloat16
_KW = {}

EPS = 1e-6
HEAD = 128
CHUNK = 128
GRID_W = 64
ROPE_THETA = 10000.0
ADAM_LR, ADAM_B1, ADAM_B2, ADAM_EPS, ADAM_WD, ADAM_STEP = 0.001, 0.9, 0.999, 1e-08, 0.01, 10
_SQRT_HALF = float(np.sqrt(0.5))
_INV_SQRT_2PI = float(1.0 / np.sqrt(2.0 * np.pi))
_VMEM_LIMIT = 56 * 1024 * 1024
_MESH = pl.DeviceIdType.MESH

_NN = (((1,), (0,)), ((), ()))
_NT = (((1,), (1,)), ((), ()))
_TN = (((0,), (0,)), ((), ()))

S = jax.ShapeDtypeStruct
BS = pl.BlockSpec
_ANY = pl.BlockSpec(memory_space=pl.ANY)


def _tile(n, pref, mult=128):
    if n <= pref:
        return n
    d = (pref // mult) * mult
    while d >= mult:
        if n % d == 0:
            return d
        d -= mult
    raise ValueError(f"no tile for {n} (pref {pref}, mult {mult})")


def _mo(v, m):
    return v if isinstance(v, int) else pl.multiple_of(v, m)


def _cp(*sem, **kw):
    return pltpu.CompilerParams(dimension_semantics=sem or None, vmem_limit_bytes=_VMEM_LIMIT, **kw)


def _call(body, name, **kw):
    return pl.pallas_call(body, name=name, **kw, **_KW)


def _dot(a, b, dn=_NN):
    return lax.dot_general(a, b, dn, preferred_element_type=_F32)


def _gelu(x):
    return 0.5 * x * (1.0 + lax.erf(x * _SQRT_HALF))


def _gelu_grad(x):
    return 0.5 * (1.0 + lax.erf(x * _SQRT_HALF)) + x * jnp.exp(-0.5 * x * x) * _INV_SQRT_2PI


def _rstd(x):
    return lax.rsqrt(jnp.mean(x * x, axis=-1, keepdims=True) + EPS)


def _norm_bwd(dout, xhat, r, g):
    dy = dout * g
    return r * (dy - xhat * jnp.mean(dy * xhat, axis=-1, keepdims=True))


def _softmax(s):
    e = jnp.exp(s - jnp.max(s, axis=-1, keepdims=True))
    return e * (1.0 / jnp.sum(e, axis=-1, keepdims=True))


def _mm(name, a, b, a_spec, b_spec, dn, grid, acc_shape, out_shape, out_specs, epilogue, extra=(), extra_specs=(), dep=None):
    nk = grid[2]
    n_ex = len(extra)
    deps = [] if dep is None else [dep]
    multi = isinstance(out_shape, (list, tuple))
    n_out = len(out_shape) if multi else 1

    def body(*refs):
        a_ref, b_ref = refs[0], refs[1]
        ex = refs[2:2 + n_ex]
        outs = refs[2 + n_ex + len(deps):2 + n_ex + len(deps) + n_out]

        def prod():
            return _dot(a_ref[...].astype(_MXU), b_ref[...].astype(_MXU), dn)

        if nk == 1:
            epilogue(prod(), ex, outs)
        else:
            acc = refs[-1]
            k = pl.program_id(2)

            @pl.when(k == 0)
            def _():
                acc[...] = jnp.zeros_like(acc)

            acc[...] += prod()

            @pl.when(k == nk - 1)
            def _():
                epilogue(acc[...], ex, outs)

    return _call(
        body, name, grid=grid, in_specs=[a_spec, b_spec, *extra_specs] + [_ANY] * len(deps), out_specs=out_specs, out_shape=out_shape,
        scratch_shapes=[] if nk == 1 else [pltpu.VMEM(acc_shape, _F32)],
        compiler_params=_cp("parallel", "parallel", "arbitrary"),
    )(a, b, *extra, *deps)


def _ep_store(acc, ex, outs):
    for o in outs:
        o[...] = acc.astype(o.dtype)


def _ep_add(acc, ex, outs):
    outs[0][...] = (acc + ex[0][...]).astype(outs[0].dtype)


def _mm_nn(name, a, b, out_dtype=_F32, add=None, pm=1024, pn=1024, pk=2048, dep=None):
    M, K = a.shape
    N = b.shape[1]
    tm, tn, tk = _tile(M, pm, 8), _tile(N, pn), _tile(K, pk)
    o_spec = BS((tm, tn), lambda i, j, k: (i, j))
    return _mm(name, a, b, BS((tm, tk), lambda i, j, k: (i, k)), BS((tk, tn), lambda i, j, k: (k, j)), _NN,
               (M // tm, N // tn, K // tk), (tm, tn), S((M, N), out_dtype), o_spec,
               _ep_store if add is None else _ep_add,
               extra=() if add is None else (add,), extra_specs=() if add is None else (o_spec,), dep=dep)


def _mm_nt(name, a, b, out_dtype=_F32, pm=1024, pn=1024, pk=4096):
    M, K = a.shape
    N = b.shape[0]
    tm, tn, tk = _tile(M, pm, 8), _tile(N, pn), _tile(K, pk)
    return _mm(name, a, b, BS((tm, tk), lambda i, j, k: (i, k)), BS((tn, tk), lambda i, j, k: (j, k)), _NT,
               (M // tm, N // tn, K // tk), (tm, tn), S((M, N), out_dtype), BS((tm, tn), lambda i, j, k: (i, j)), _ep_store)


def _mm_tn_dual(name, a, b, pm=1024, pn=1024, pk=2048, dep=None):
    K, M = a.shape
    N = b.shape[1]
    tm, tn, tk = _tile(M, pm), _tile(N, pn), _tile(K, pk, 16)
    o_spec = BS((tm, tn), lambda i, j, k: (i, j))
    return _mm(name, a, b, BS((tk, tm), lambda i, j, k: (k, i)), BS((tk, tn), lambda i, j, k: (k, j)), _TN,
               (M // tm, N // tn, K // tk), (tm, tn), [S((M, N), _F32), S((M, N), _WIRE)], [o_spec, o_spec], _ep_store, dep=dep)


def _ffn_gate_up(h2, wgu):
    T, D = h2.shape
    F = wgu.shape[1] // 2
    tm, tn = _tile(T, 1024, 8), _tile(F, 512)
    nj = F // tn

    def body(a_ref, bg_ref, bu_ref, act_ref, gu_ref):
        a = a_ref[...]
        g = _dot(a, bg_ref[...])
        u = _dot(a, bu_ref[...])
        gu_ref[0] = g.astype(gu_ref.dtype)
        gu_ref[1] = u.astype(gu_ref.dtype)
        act_ref[...] = (g * (1.0 / (1.0 + jnp.exp(-g))) * u).astype(act_ref.dtype)

    return _call(
        body, "ffn_gate_up", grid=(T // tm, nj),
        in_specs=[BS((tm, D), lambda i, j: (i, 0)), BS((D, tn), lambda i, j: (0, j)), BS((D, tn), lambda i, j: (0, j + nj))],
        out_specs=[BS((tm, tn), lambda i, j: (i, j)), BS((2, tm, tn), lambda i, j: (0, i, j))],
        out_shape=[S((T, F), _MXU), S((2, T, F), _MXU)],
        compiler_params=_cp("parallel", "parallel"),
    )(h2, wgu, wgu)


def _ffn_dact(dxm, wdown, gu, dep=None):
    T, D = dxm.shape
    F = wdown.shape[0]
    tm, tn = _tile(T, 1024, 8), _tile(F, 512)
    deps = [] if dep is None else [dep]

    halves = 2 if tn % 256 == 0 else 1
    th = tn // halves

    def body(a_ref, b_ref, gu_ref, *rest):
        o_ref = rest[-1]
        a = a_ref[...]
        for s in range(halves):
            cols = slice(s * th, (s + 1) * th)
            d = _dot(a, b_ref[cols, :], _NT)
            g, u = gu_ref[0, :, cols].astype(_F32), gu_ref[1, :, cols].astype(_F32)
            sg = 1.0 / (1.0 + jnp.exp(-g))
            o_ref[0, :, cols] = (d * u * (sg * (1.0 + g * (1.0 - sg)))).astype(o_ref.dtype)
            o_ref[1, :, cols] = (d * (g * sg)).astype(o_ref.dtype)

    return _call(
        body, "ffn_dact", grid=(T // tm, F // tn),
        in_specs=[BS((tm, D), lambda i, j: (i, 0)), BS((tn, D), lambda i, j: (j, 0)), BS((2, tm, tn), lambda i, j: (0, i, j))]
        + [_ANY] * len(deps),
        out_specs=BS((2, tm, tn), lambda i, j: (0, i, j)), out_shape=S((2, T, F), _MXU),
        compiler_params=_cp("parallel", "parallel"),
    )(dxm, wdown, gu, *deps)


def _ffn_dh(dgu, wgu, dep=None):
    _, T, F = dgu.shape
    D = wgu.shape[0]
    tm, tn, tk = _tile(T, 1024, 8), _tile(D, 2048), _tile(F, 2048)
    nkf = F // tk
    return _mm("ffn_dh", dgu, wgu, BS((None, tm, tk), lambda i, j, k: (k // nkf, i, k % nkf)),
               BS((tn, tk), lambda i, j, k: (j, k)), _NT, (T // tm, D // tn, 2 * nkf), (tm, tn),
               S((T, D), _F32), BS((tm, tn), lambda i, j, k: (i, j)), _ep_store, dep=dep)


def _ffn_dwgu(h2, dgu):
    _, T, F = dgu.shape
    D = h2.shape[1]
    tm, tn, tk = _tile(D, 1024), _tile(F, 1408), _tile(T, 2048, 16)
    njf = F // tn
    o_spec = BS((tm, tn), lambda i, j, k: (i, j))
    return _mm("ffn_dwgu", h2, dgu, BS((tk, tm), lambda i, j, k: (k, i)),
               BS((None, tk, tn), lambda i, j, k: (j // njf, k, j % njf)), _TN, (D // tm, 2 * njf, T // tk), (tm, tn),
               [S((D, 2 * F), _F32), S((D, 2 * F), _WIRE)], [o_spec, o_spec], _ep_store)


def _rmsnorm_fwd(x, g, name, dep=None):
    T, D = x.shape
    tr = _tile(T, 512, 8)

    def body(x_ref, g_ref, *rest):
        xv = x_ref[...]
        rest[-1][...] = (xv * _rstd(xv) * g_ref[...]).astype(rest[-1].dtype)

    row = BS((tr, D), lambda i: (i, 0))
    deps = [] if dep is None else [dep]
    return _call(body, name, grid=(T // tr,), in_specs=[row, BS((1, D), lambda i: (0, 0))] + [_ANY] * len(deps), out_specs=row,
                 out_shape=S((T, D), _MXU), compiler_params=_cp("parallel"))(x, g.reshape(1, D), *deps)


def _rmsnorm_bwd(x, g, dh, dres, name):
    T, D = x.shape
    tr = _tile(T, 256, 8)
    has_res = dres is not None

    def body(*refs):
        x_ref, g_ref, dh_ref = refs[:3]
        dx_ref, dxm_ref, dg_ref = refs[-3:]

        @pl.when(pl.program_id(0) == 0)
        def _():
            dg_ref[...] = jnp.zeros_like(dg_ref)

        xv = x_ref[...]
        r = _rstd(xv)
        xhat = xv * r
        dh_v = dh_ref[...]
        dg_ref[...] += jnp.sum(dh_v * xhat, axis=0, keepdims=True)
        dx = _norm_bwd(dh_v, xhat, r, g_ref[...])
        if has_res:
            dx = dx + refs[3][...]
        dx_ref[...] = dx
        dxm_ref[...] = dx.astype(dxm_ref.dtype)

    row = BS((tr, D), lambda i: (i, 0))
    vec = BS((1, D), lambda i: (0, 0))
    return _call(body, name, grid=(T // tr,), in_specs=[row, vec, row] + ([row] if has_res else []),
                 out_specs=[row, row, vec], out_shape=[S((T, D), _F32), S((T, D), _MXU), S((1, D), _F32)],
                 compiler_params=_cp("arbitrary"))(x, g.reshape(1, D), dh, *([dres] if has_res else []))


def _loss_head(y, target):
    T, D = y.shape
    tr = _tile(T, 256, 8)

    def body(y_ref, t_ref, dy_ref, dym_ref, acc_ref):
        @pl.when(pl.program_id(0) == 0)
        def _():
            acc_ref[...] = jnp.zeros_like(acc_ref)

        err = y_ref[...] - t_ref[...]
        acc_ref[...] += jnp.sum(jnp.sum(err * err, axis=-1, keepdims=True), axis=0, keepdims=True)
        dy = err * (1.0 / D)
        dy_ref[...] = dy
        dym_ref[...] = dy.astype(dym_ref.dtype)

    row = BS((tr, D), lambda i: (i, 0))
    return _call(body, "loss_head", grid=(T // tr,), in_specs=[row, row],
                 out_specs=[row, row, BS((1, 128), lambda i: (0, 0))],
                 out_shape=[S((T, D), _F32), S((T, D), _MXU), S((1, 128), _F32)],
                 compiler_params=_cp("arbitrary"))(y, target)


def _mixa_blocks(T):
    return 2 if T % (2 * CHUNK) == 0 else 1


def _mixer_a_fwd(z, gv, ws_m, b_t, TOK):
    T = z.shape[0]
    G = TOK // HEAD
    CB = _mixa_blocks(T)
    R = CB * CHUNK

    def body(z_ref, gv_ref, ws_ref, bt_ref, o_ref):
        u = _gelu(z_ref[:, :TOK])
        v = _gelu(z_ref[:, TOK:])
        vn = (v * _rstd(v) * gv_ref[...]).astype(_MXU)
        for c in range(CB):
            rows = slice(c * CHUNK, (c + 1) * CHUNK)
            for g in range(G):
                cols = slice(g * HEAD, (g + 1) * HEAD)
                s = _dot(ws_ref[g], vn[rows, cols]) + bt_ref[:, g:g + 1]
                o_ref[rows, cols] = (u[rows, cols] * s).astype(o_ref.dtype)

    return _call(
        body, "mixer_a_fwd", grid=(T // R,),
        in_specs=[BS((R, 2 * TOK), lambda i: (i, 0)), BS((1, TOK), lambda i: (0, 0)),
                  BS((G, CHUNK, CHUNK), lambda i: (0, 0, 0)), BS((CHUNK, G), lambda i: (0, 0))],
        out_specs=BS((R, TOK), lambda i: (i, 0)), out_shape=S((T, TOK), _MXU), compiler_params=_cp("parallel"),
    )(z, gv.reshape(1, TOK), ws_m, b_t)


def _mixer_a_bwd(z, dcat, gv, ws_m, wst_m, b_t, TOK):
    T = z.shape[0]
    G = TOK // HEAD
    CB = _mixa_blocks(T)
    R = CB * CHUNK
    n = T // R

    def body(z_ref, d_ref, gv_ref, ws_ref, wst_ref, bt_ref, dz_ref, dws_ref, db_ref, dgv_ref, dvn_scr):
        i = pl.program_id(0)

        @pl.when(i == 0)
        def _():
            dws_ref[...] = jnp.zeros_like(dws_ref)
            db_ref[...] = jnp.zeros_like(db_ref)
            dgv_ref[...] = jnp.zeros_like(dgv_ref)

        zu = z_ref[:, :TOK]
        zv = z_ref[:, TOK:]
        u = _gelu(zu)
        v = _gelu(zv)
        r = _rstd(v)
        vhat = v * r
        gvv = gv_ref[...]
        vn = (vhat * gvv).astype(_MXU)
        d = d_ref[...]
        gpu = _gelu_grad(zu)
        for c in range(CB):
            rows = slice(c * CHUNK, (c + 1) * CHUNK)
            for g in range(G):
                cols = slice(g * HEAD, (g + 1) * HEAD)
                vn_cg = vn[rows, cols]
                s = _dot(ws_ref[g], vn_cg) + bt_ref[:, g:g + 1]
                d_cg = d[rows, cols]
                dz_ref[rows, cols] = (d_cg * s * gpu[rows, cols]).astype(dz_ref.dtype)
                ds = d_cg * u[rows, cols]
                ds_m = ds.astype(_MXU)
                dvn_scr[rows, cols] = _dot(wst_ref[g], ds_m)
                dws_ref[g] += _dot(ds_m, vn_cg, _NT)
                db_ref[g] += ds
        dvn = dvn_scr[...]
        dgv_ref[...] += jnp.sum(dvn * vhat, axis=0, keepdims=True)
        dv = _norm_bwd(dvn, vhat, r, gvv)
        dz_ref[:, TOK:] = (dv * _gelu_grad(zv)).astype(dz_ref.dtype)

        @pl.when(i == n - 1)
        def _():
            for g in range(G):
                db_ref[g] = jnp.broadcast_to(jnp.sum(db_ref[g], axis=1, keepdims=True), (CHUNK, CHUNK))

    full3 = BS((G, CHUNK, CHUNK), lambda i: (0, 0, 0))
    return _call(
        body, "mixer_a_bwd", grid=(n,),
        in_specs=[BS((R, 2 * TOK), lambda i: (i, 0)), BS((R, TOK), lambda i: (i, 0)), BS((1, TOK), lambda i: (0, 0)),
                  full3, full3, BS((CHUNK, G), lambda i: (0, 0))],
        out_specs=[BS((R, 2 * TOK), lambda i: (i, 0)), full3, full3, BS((1, TOK), lambda i: (0, 0))],
        out_shape=[S((T, 2 * TOK), _MXU), S((G, CHUNK, CHUNK), _F32), S((G, CHUNK, CHUNK), _F32), S((1, TOK), _F32)],
        scratch_shapes=[pltpu.VMEM((R, TOK), _F32)], compiler_params=_cp("arbitrary"),
    )(z, dcat, gv.reshape(1, TOK), ws_m, wst_m, b_t)


def _rope_tables(T):
    n_rows = T // GRID_W
    rows = jnp.broadcast_to(jnp.arange(n_rows)[:, None], (n_rows, GRID_W)).reshape(T)
    cols = jnp.broadcast_to(jnp.arange(GRID_W)[None, :], (n_rows, GRID_W)).reshape(T)
    pairs = HEAD // 4
    freqs = ROPE_THETA ** (-jnp.arange(pairs, dtype=_F32) / pairs)
    ang_r = rows.astype(_F32)[:, None] * freqs
    ang_c = cols.astype(_F32)[:, None] * freqs
    ang = jnp.concatenate([ang_r, ang_r, ang_c, ang_c], axis=-1)
    cos, sin = jnp.cos(ang), jnp.sin(ang)
    first = (jnp.arange(HEAD) % (HEAD // 2)) < (HEAD // 4)
    return cos, jnp.where(first, -sin, 0.0), jnp.where(first, 0.0, sin)


def _rope(x, cs, sa, sb):
    return x * cs + pltpu.roll(x, 96, 1) * sa + pltpu.roll(x, 32, 1) * sb


def _qk_rope_fwd(z, gq, gk, tabs, TOK, KV):
    T = z.shape[0]
    R = _tile(T, 512, 8)
    W = TOK + 2 * KV

    def body(z_ref, gq_ref, gk_ref, cos_ref, sa_ref, sb_ref, q_ref, k_ref, v_ref):
        cs, sa, sb = cos_ref[...], sa_ref[...], sb_ref[...]
        for h in range((TOK + KV) // HEAD):
            cols = slice(h * HEAD, (h + 1) * HEAD)
            xv = z_ref[:, cols]
            xn = xv * _rstd(xv) * (gq_ref[...] if h < TOK // HEAD else gk_ref[...])
            out = _rope(xn, cs, sa, sb)
            if h < TOK // HEAD:
                q_ref[:, cols] = out.astype(q_ref.dtype)
            else:
                k_ref[:, h * HEAD - TOK:(h + 1) * HEAD - TOK] = out.astype(k_ref.dtype)
        v_ref[...] = z_ref[:, TOK + KV:].astype(v_ref.dtype)

    vec = BS((1, HEAD), lambda i: (0, 0))
    tab = BS((R, HEAD), lambda i: (i, 0))
    return _call(
        body, "qk_rope_fwd", grid=(T // R,), in_specs=[BS((R, W), lambda i: (i, 0)), vec, vec, tab, tab, tab],
        out_specs=[BS((R, TOK), lambda i: (i, 0)), BS((R, KV), lambda i: (i, 0)), BS((R, KV), lambda i: (i, 0))],
        out_shape=[S((T, TOK), _MXU), S((T, KV), _MXU), S((T, KV), _MXU)], compiler_params=_cp("parallel"),
    )(z, gq.reshape(1, HEAD), gk.reshape(1, HEAD), *tabs)


def _qk_rope_bwd(z, dq, dk, gq, gk, tabs, TOK, KV):
    T = z.shape[0]
    R = _tile(T, 512, 8)
    W = TOK + KV

    def body(z_ref, dq_ref, dk_ref, gq_ref, gk_ref, cos_ref, sa_ref, sb_ref, dz_ref, dgq_ref, dgk_ref):
        @pl.when(pl.program_id(0) == 0)
        def _():
            dgq_ref[...] = jnp.zeros_like(dgq_ref)
            dgk_ref[...] = jnp.zeros_like(dgk_ref)

        cs, sa, sb = cos_ref[...], sa_ref[...], sb_ref[...]
        for h in range(W // HEAD):
            cols = slice(h * HEAD, (h + 1) * HEAD)
            is_q = h < TOK // HEAD
            do = dq_ref[:, cols] if is_q else dk_ref[:, h * HEAD - TOK:(h + 1) * HEAD - TOK]
            dxn = do * cs - pltpu.roll(do, 96, 1) * sa - pltpu.roll(do, 32, 1) * sb
            xv = z_ref[:, cols]
            r = _rstd(xv)
            xhat = xv * r
            dg_ref = dgq_ref if is_q else dgk_ref
            dg_ref[...] += jnp.sum(dxn * xhat, axis=0, keepdims=True)
            dz_ref[:, cols] = _norm_bwd(dxn, xhat, r, gq_ref[...] if is_q else gk_ref[...]).astype(dz_ref.dtype)

    vec = BS((1, HEAD), lambda i: (0, 0))
    tab = BS((R, HEAD), lambda i: (i, 0))
    return _call(
        body, "qk_rope_bwd", grid=(T // R,),
        in_specs=[BS((R, W), lambda i: (i, 0)), BS((R, TOK), lambda i: (i, 0)), BS((R, KV), lambda i: (i, 0)), vec, vec, tab, tab, tab],
        out_specs=[BS((R, W), lambda i: (i, 0)), vec, vec],
        out_shape=[S((T, W), _MXU), S((1, HEAD), _F32), S((1, HEAD), _F32)], compiler_params=_cp("arbitrary"),
    )(z, dq, dk, gq.reshape(1, HEAD), gk.reshape(1, HEAD), *tabs)


_ATTN_C2 = float(HEAD ** -0.5 * np.log2(np.e))


def _attn_fwd(q, k, v, QPK):
    T, TOK = q.shape
    KVH = k.shape[1] // HEAD
    tq = _tile(T, 256, 8)
    W = QPK * HEAD

    def body(q_ref, k_ref, v_ref, o_ref, st_ref, vaug):
        @pl.when(pl.program_id(1) == 0)
        def _():
            vaug[:, :HEAD] = v_ref[...]
            vaug[:, HEAD:] = jnp.ones((T, HEAD), vaug.dtype)

        kk, va = k_ref[...], vaug[...]
        for g in range(QPK):
            cols = slice(g * HEAD, (g + 1) * HEAD)
            s = _dot(q_ref[:, cols], kk, _NT)
            m = jnp.max(s, axis=-1, keepdims=True)
            ov = _dot(jnp.exp2((s - m) * _ATTN_C2).astype(_MXU), va)
            l = ov[:, HEAD:HEAD + 1]
            o_ref[:, cols] = (ov[:, :HEAD] * (1.0 / l)).astype(o_ref.dtype)
            st_ref[:, g:g + 1] = m + jnp.log2(l) * (1.0 / _ATTN_C2)

    qs = BS((tq, W), lambda h, i: (i, h))
    ks = BS((T, HEAD), lambda h, i: (0, h))
    return _call(body, "attn_fwd", grid=(KVH, T // tq), in_specs=[qs, ks, ks],
                 out_specs=[qs, BS((None, tq, QPK), lambda h, i: (h, i, 0))],
                 out_shape=[S((T, TOK), _MXU), S((KVH, T, QPK), _F32)],
                 scratch_shapes=[pltpu.VMEM((T, 2 * HEAD), _MXU)],
                 compiler_params=_cp("parallel", "arbitrary"))(q, k, v)


def _attn_bwd(q, k, v, dcat, o, stat, QPK):
    T, TOK = q.shape
    KV = k.shape[1]
    KVH = KV // HEAD
    tq = _tile(T, 256, 8)
    nq = T // tq
    W = QPK * HEAD
    scale = HEAD ** -0.5

    def body(q_ref, k_ref, v_ref, do_ref, o_ref, st_ref, dq_ref, dk_ref, dv_ref, dk_acc, dv_acc, ds_all, p_all, q_all, do_all):
        i = pl.program_id(1)

        @pl.when(i == 0)
        def _():
            dk_acc[...] = jnp.zeros_like(dk_acc)
            dv_acc[...] = jnp.zeros_like(dv_acc)

        kk, vv = k_ref[...], v_ref[...]
        for g in range(QPK):
            cols = slice(g * HEAD, (g + 1) * HEAD)
            rows = slice(g * tq, (g + 1) * tq)
            qg = q_ref[:, cols]
            p = jnp.exp2((_dot(qg, kk, _NT) - st_ref[:, g:g + 1]) * _ATTN_C2)
            do32 = do_ref[:, cols]
            do = do32.astype(_MXU)
            delta = jnp.sum(do32 * o_ref[:, cols].astype(_F32), axis=-1, keepdims=True)
            ds = (p * (_dot(do, vv, _NT) - delta)).astype(_MXU)
            dq_ref[:, cols] = _dot(ds, kk) * scale
            ds_all[rows, :] = ds
            p_all[rows, :] = p.astype(_MXU)
            q_all[rows, :] = qg
            do_all[rows, :] = do
        dk_acc[...] += _dot(ds_all[...], q_all[...], _TN)
        dv_acc[...] += _dot(p_all[...], do_all[...], _TN)

        @pl.when(i == nq - 1)
        def _():
            dk_ref[...] = dk_acc[...] * scale
            dv_ref[...] = dv_acc[...].astype(dv_ref.dtype)

    qs = BS((tq, W), lambda h, i: (i, h))
    ks = BS((T, HEAD), lambda h, i: (0, h))
    return _call(
        body, "attn_bwd", grid=(KVH, nq), in_specs=[qs, ks, ks, qs, qs, BS((None, tq, QPK), lambda h, i: (h, i, 0))],
        out_specs=[qs, ks, ks], out_shape=[S((T, TOK), _F32), S((T, KV), _F32), S((T, KV), _MXU)],
        scratch_shapes=[pltpu.VMEM((T, HEAD), _F32), pltpu.VMEM((T, HEAD), _F32), pltpu.VMEM((QPK * tq, T), _MXU),
                        pltpu.VMEM((QPK * tq, T), _MXU), pltpu.VMEM((QPK * tq, HEAD), _MXU), pltpu.VMEM((QPK * tq, HEAD), _MXU)],
        compiler_params=_cp("parallel", "arbitrary"),
    )(q, k, v, dcat, o, stat)


def _mem_fwd(z, qblk, kv, gmq, gmk, MEMW):
    T = z.shape[0]
    NM = kv.shape[0]
    tq = _tile(T, 512, 8)
    scale = HEAD ** -0.5

    def body(q_ref, kv_ref, gq_ref, gk_ref, o_ref):
        for h in range(MEMW // HEAD):
            cols = slice(h * HEAD, (h + 1) * HEAD)
            kx = kv_ref[:, cols]
            kn = (kx * _rstd(kx) * gk_ref[...]).astype(_MXU)
            vv = kv_ref[:, MEMW + h * HEAD:MEMW + (h + 1) * HEAD].astype(_MXU)
            qx = q_ref[:, cols]
            qn = (qx * _rstd(qx) * gq_ref[...]).astype(_MXU)
            p = _softmax(_dot(qn, kn, _NT) * scale)
            o_ref[:, cols] = _dot(p.astype(_MXU), vv).astype(o_ref.dtype)

    vec = BS((1, HEAD), lambda i: (0, 0))
    return _call(
        body, "mem_fwd", grid=(T // tq,),
        in_specs=[BS((tq, MEMW), lambda i: (i, qblk)), BS((NM, 2 * MEMW), lambda i: (0, 0)), vec, vec],
        out_specs=BS((tq, MEMW), lambda i: (i, 0)), out_shape=S((T, MEMW), _MXU), compiler_params=_cp("parallel"),
    )(z, kv, gmq.reshape(1, HEAD), gmk.reshape(1, HEAD))


def _mem_bwd(z, qblk, kv, gmq, gmk, dcat, dblk, MEMW):
    T = z.shape[0]
    NM = kv.shape[0]
    tq = _tile(T, 512, 8)
    scale = HEAD ** -0.5

    def body(q_ref, kv_ref, gq_ref, gk_ref, do_ref, dz_ref, dkn_ref, dv_ref, dgq_ref):
        @pl.when(pl.program_id(0) == 0)
        def _():
            dkn_ref[...] = jnp.zeros_like(dkn_ref)
            dv_ref[...] = jnp.zeros_like(dv_ref)
            dgq_ref[...] = jnp.zeros_like(dgq_ref)

        for h in range(MEMW // HEAD):
            cols = slice(h * HEAD, (h + 1) * HEAD)
            kx = kv_ref[:, cols]
            kn = (kx * _rstd(kx) * gk_ref[...]).astype(_MXU)
            vv = kv_ref[:, MEMW + h * HEAD:MEMW + (h + 1) * HEAD].astype(_MXU)
            qx = q_ref[:, cols]
            rq = _rstd(qx)
            qhat = qx * rq
            qn = (qhat * gq_ref[...]).astype(_MXU)
            p = _softmax(_dot(qn, kn, _NT) * scale)
            do = do_ref[:, cols].astype(_MXU)
            dp = _dot(do, vv, _NT)
            ds = (p * (dp - jnp.sum(p * dp, axis=-1, keepdims=True)) * scale).astype(_MXU)
            dqn = _dot(ds, kn)
            dkn_ref[:, cols] += _dot(ds, qn, _TN)
            dv_ref[:, cols] += _dot(p.astype(_MXU), do, _TN)
            dgq_ref[...] += jnp.sum(dqn * qhat, axis=0, keepdims=True)
            dz_ref[:, cols] = _norm_bwd(dqn, qhat, rq, gq_ref[...]).astype(dz_ref.dtype)

    vec = BS((1, HEAD), lambda i: (0, 0))
    kvs = BS((NM, MEMW), lambda i: (0, 0))
    return _call(
        body, "mem_bwd", grid=(T // tq,),
        in_specs=[BS((tq, MEMW), lambda i: (i, qblk)), BS((NM, 2 * MEMW), lambda i: (0, 0)), vec, vec,
                  BS((tq, MEMW), lambda i: (i, dblk))],
        out_specs=[BS((tq, MEMW), lambda i: (i, 0)), kvs, kvs, vec],
        out_shape=[S((T, MEMW), _MXU), S((NM, MEMW), _F32), S((NM, MEMW), _F32), S((1, HEAD), _F32)],
        compiler_params=_cp("arbitrary"),
    )(z, kv, gmq.reshape(1, HEAD), gmk.reshape(1, HEAD), dcat)


def _memkv_bwd(kv, dkn, dv, gmk, MEMW):
    NM = kv.shape[0]

    def body(kv_ref, dkn_ref, dv_ref, gk_ref, dkv_ref, dgk_ref):
        dgk = jnp.zeros((1, HEAD), _F32)
        for h in range(MEMW // HEAD):
            cols = slice(h * HEAD, (h + 1) * HEAD)
            kx = kv_ref[:, cols]
            r = _rstd(kx)
            khat = kx * r
            dkn = dkn_ref[:, cols]
            dgk = dgk + jnp.sum(dkn * khat, axis=0, keepdims=True)
            dkv_ref[:, cols] = _norm_bwd(dkn, khat, r, gk_ref[...]).astype(dkv_ref.dtype)
        dgk_ref[...] = dgk
        dkv_ref[:, MEMW:] = dv_ref[...].astype(dkv_ref.dtype)

    return _call(body, "memkv_bwd", out_shape=[S((NM, 2 * MEMW), _MXU), S((1, HEAD), _F32)],
                 compiler_params=_cp())(kv, dkn, dv, gmk.reshape(1, HEAD))


def _cast_into_full(w, l, sh, idx, dep=None):
    tr, tc = _tile(sh.Rs, 512, 16), _tile(sh.Cs, 2048)
    nr, nc = sh.Rs // tr, sh.Cs // tc
    deps = [] if dep is None else [dep]

    def body(i_ref, c_ref, w_ref, *rest):
        rest[-1][...] = w_ref[...].astype(rest[-1].dtype)

    if sh.by_cols:
        o_map = lambda a, b, si, sc: (a, si[0] * nc + b)
    else:
        o_map = lambda a, b, si, sc: (si[0] * nr + a, b)
    return _call(
        body, "cast_into_full",
        grid_spec=pltpu.PrefetchScalarGridSpec(
            num_scalar_prefetch=2, grid=(nr, nc),
            in_specs=[BS((None, tr, tc), lambda a, b, si, sc: (l, a, b))] + [_ANY] * len(deps), out_specs=BS((tr, tc), o_map)),
        out_shape=S((sh.R, sh.C), _WIRE), compiler_params=_cp("parallel", "parallel"),
    )(*idx, w, *deps)


def _adamw(w, g, m, v, name, l0=0, l1=None, prev=None):
    L, R, C = w.shape
    l1 = L if l1 is None else l1
    tc = _tile(C, 2048)
    tr = _tile(R, max(8, (512 * 1024) // tc), 8)
    c_m = 1.0 / (1.0 - ADAM_B1 ** ADAM_STEP)
    c_v = 1.0 / (1.0 - ADAM_B2 ** ADAM_STEP)

    def body(w_ref, g_ref, m_ref, v_ref, *rest):
        go_ref, d_ref, mo_ref, vo_ref = rest[-4:]
        gv = g_ref[...]
        mn = ADAM_B1 * m_ref[...] + (1.0 - ADAM_B1) * gv
        vn = ADAM_B2 * v_ref[...] + (1.0 - ADAM_B2) * (gv * gv)
        go_ref[...] = gv
        mo_ref[...] = mn
        vo_ref[...] = vn
        d_ref[...] = -ADAM_LR * ((mn * c_m) / (jnp.sqrt(vn * c_v) + ADAM_EPS) + ADAM_WD * w_ref[...])

    blk = BS((None, tr, tc), lambda a, i, j: (l0 + a, i, j))
    prevs = [] if prev is None else list(prev)
    return _call(body, name, grid=(l1 - l0, R // tr, C // tc), in_specs=[blk] * 4 + [_ANY] * len(prevs), out_specs=[blk] * 4,
                 out_shape=[S((L, R, C), _F32)] * 4, input_output_aliases={4 + k: k for k in range(len(prevs))},
                 compiler_params=_cp("parallel", "parallel", "parallel"))(w, g, m, v, *prevs)


def _where_am_i():
    x, y, c = lax.axis_index("x"), lax.axis_index("y"), lax.axis_index("c")
    chips = [(1 - x, y), (x, 1 - y), (1 - x, 1 - y)]
    return x, y, c, 2 * x + y, chips, [2 * cx + cy for cx, cy in chips]


class _Shard:
    def __init__(self, R, C, by_cols):
        self.R, self.C, self.by_cols = R, C, by_cols
        self.Rs, self.Cs = (R, C // 4) if by_cols else (R // 4, C)
        self.Rh = self.Rs // 2
        self.Q = R // 8

    def full_piece(self, ref, j, cc):
        if self.by_cols:
            return ref.at[pl.ds(cc * self.Rh, self.Rh), pl.ds(_mo(j * self.Cs, 128), self.Cs)]
        return ref.at[pl.ds(_mo(j * self.Rs + cc * self.Rh, 16), self.Rh), :]

    def full_shard(self, ref, j):
        if self.by_cols:
            return ref.at[:, pl.ds(_mo(j * self.Cs, 128), self.Cs)]
        return ref.at[pl.ds(_mo(j * self.Rs, 16), self.Rs), :]

    def shard_half(self, ref, cc):
        return ref.at[pl.ds(_mo(cc * self.Rh, 16), self.Rh), :]

    def half_piece(self, ref, j):
        if self.by_cols:
            return ref.at[:, pl.ds(_mo(j * self.Cs, 128), self.Cs)]
        return ref.at[pl.ds(_mo(j * self.Rh, 16), self.Rh), :]


def _remote(src, dst, ssem, rsem, dev):
    return pltpu.make_async_remote_copy(src_ref=src, dst_ref=dst, send_sem=ssem, recv_sem=rsem, device_id=dev, device_id_type=_MESH)


_HBM = pl.BlockSpec(memory_space=pltpu.HBM)
_SEM = pl.BlockSpec(memory_space=pltpu.SEMAPHORE)
_EFFECT = pltpu.SideEffectType.DATAFLOW_SIDE_EFFECTING


def _in_hbm(a):
    return pltpu.with_memory_space_constraint(a, pltpu.HBM)


def _gather_start(fulls, shs, name):
    n = len(fulls)

    def body(*refs):
        bufs = refs[:n]
        send_sems, recv_sems = refs[n], refs[n + 1]
        token = refs[-1]
        x, y, c, me, chips, chip_ids = _where_am_i()
        for t in range(n):
            mine = shs[t].full_piece(bufs[t], me, c)
            for r in range(3):
                _remote(mine, mine, send_sems.at[3 * t + r], recv_sems.at[3 * t + r], (*chips[r], c)).start()
        token[...] = jnp.zeros_like(token)

    out = pl.pallas_call(
        body, name=name, in_specs=[_HBM] * n,
        out_shape=(pltpu.SemaphoreType.DMA((3 * n,)), pltpu.SemaphoreType.DMA((3 * n,)), *[pltpu.HBM(f.shape, f.dtype) for f in fulls],
                   S((8, 128), _F32)),
        out_specs=(_SEM, _SEM, *[_HBM] * n, pl.BlockSpec(memory_space=pltpu.VMEM)),
        input_output_aliases={t: 2 + t for t in range(n)},
        compiler_params=pltpu.CompilerParams(has_side_effects=_EFFECT), **_KW,
    )(*[_in_hbm(f) for f in fulls])
    return out[0], out[1], list(out[2:2 + n]), out[-1]


def _gather_wait(fulls, send_sems, recv_sems, after, shs, name):
    n = len(fulls)

    def body(*refs):
        bufs = refs[:n]
        ssem, rsem = refs[n], refs[n + 1]
        x, y, c, me, chips, chip_ids = _where_am_i()
        for t in range(n):
            mine = shs[t].full_piece(bufs[t], me, c)
            for r in range(3):
                _remote(mine, mine, ssem.at[3 * t + r], rsem.at[3 * t + r], (*chips[r], c)).wait_send()
        for t in range(n):
            for r in range(3):
                piece = shs[t].full_piece(bufs[t], chip_ids[r], c)
                _remote(piece, piece, ssem.at[3 * t + r], rsem.at[3 * t + r], (*chips[r], c)).wait_recv()

    out = pl.pallas_call(
        body, name=name, in_specs=[*[_HBM] * n, _SEM, _SEM, _ANY], out_specs=[_HBM] * n,
        out_shape=[pltpu.HBM(f.shape, f.dtype) for f in fulls], input_output_aliases={t: t for t in range(n)},
        compiler_params=pltpu.CompilerParams(has_side_effects=_EFFECT), **_KW,
    )(*fulls, send_sems, recv_sems, after)
    return list(out)


def _gather_pass_on(fulls, shs, name):
    n = len(fulls)

    def body(*refs):
        bufs = refs[n:2 * n]
        send_sems, recv_sems = refs[2 * n:]
        x, y, c, me, chips, chip_ids = _where_am_i()
        sib = (x, y, 1 - c)
        cps = []
        for t in range(n):
            for r in range(3):
                piece = shs[t].full_piece(bufs[t], chip_ids[r], c)
                cps.append(_remote(piece, piece, send_sems.at[t, r], recv_sems.at[t, r], sib))
        for cp in cps:
            cp.start()
        for t in range(n):
            for r in range(3):
                piece = shs[t].full_piece(bufs[t], chip_ids[r], 1 - c)
                _remote(piece, piece, send_sems.at[t, r], recv_sems.at[t, r], sib).wait_recv()
        for cp in cps:
            cp.wait_send()

    return _call(
        body, name, in_specs=[_ANY] * n, out_specs=[_ANY] * n, out_shape=[S(f.shape, f.dtype) for f in fulls],
        input_output_aliases={t: t for t in range(n)},
        scratch_shapes=[pltpu.SemaphoreType.DMA((n, 3)), pltpu.SemaphoreType.DMA((n, 3))],
        compiler_params=pltpu.CompilerParams(has_side_effects=True),
    )(*fulls)


def _pass_on_copies(bufs, shs, send_sems, recv_sems):
    x, y, c, me, chips, chip_ids = _where_am_i()
    sib = (x, y, 1 - c)
    out, back = [], []
    for t in range(len(bufs)):
        for r in range(3):
            piece = shs[t].full_piece(bufs[t], chip_ids[r], c)
            out.append(_remote(piece, piece, send_sems.at[3 * t + r], recv_sems.at[3 * t + r], sib))
            other = shs[t].full_piece(bufs[t], chip_ids[r], 1 - c)
            back.append(_remote(other, other, send_sems.at[3 * t + r], recv_sems.at[3 * t + r], sib))
    return out, back


def _pass_on_start(fulls, shs, name):
    n = len(fulls)

    def body(*refs):
        for cp in _pass_on_copies(refs[:n], shs, refs[n], refs[n + 1])[0]:
            cp.start()
        refs[-1][...] = jnp.zeros_like(refs[-1])

    out = pl.pallas_call(
        body, name=name, in_specs=[_HBM] * n,
        out_shape=(pltpu.SemaphoreType.DMA((3 * n,)), pltpu.SemaphoreType.DMA((3 * n,)), *[pltpu.HBM(f.shape, f.dtype) for f in fulls],
                   S((8, 128), _F32)),
        out_specs=(_SEM, _SEM, *[_HBM] * n, pl.BlockSpec(memory_space=pltpu.VMEM)),
        input_output_aliases={t: 2 + t for t in range(n)},
        compiler_params=pltpu.CompilerParams(has_side_effects=_EFFECT), **_KW,
    )(*[_in_hbm(f) for f in fulls])
    return out[0], out[1], list(out[2:2 + n]), out[-1]


def _pass_on_wait(fulls, send_sems, recv_sems, after, shs, name):
    n = len(fulls)

    def body(*refs):
        out, back = _pass_on_copies(refs[:n], shs, refs[n], refs[n + 1])
        for cp in out:
            cp.wait_send()
        for cp in back:
            cp.wait_recv()

    out = pl.pallas_call(
        body, name=name, in_specs=[*[_HBM] * n, _SEM, _SEM, _ANY], out_specs=[_HBM] * n,
        out_shape=[pltpu.HBM(f.shape, f.dtype) for f in fulls], input_output_aliases={t: t for t in range(n)},
        compiler_params=pltpu.CompilerParams(has_side_effects=_EFFECT), **_KW,
    )(*fulls, send_sems, recv_sems, after)
    return list(out)


def _rs_pair_copies(ins, outs, shs, send_sems, recv_sems):
    x, y, c, *_ = _where_am_i()
    sib = (x, y, 1 - c)
    cps = []
    for t in range(len(ins)):
        sh = shs[t]
        if sh.by_cols:
            cps.append(_remote(ins[t].at[pl.ds((1 - c) * sh.Rh, sh.Rh), :], outs[t], send_sems.at[4 * t], recv_sems.at[4 * t], sib))
        else:
            for j in range(4):
                cps.append(_remote(sh.full_piece(ins[t], j, 1 - c), sh.half_piece(outs[t], j),
                                   send_sems.at[4 * t + j], recv_sems.at[4 * t + j], sib))
    return cps


def _rs_pair_start(dws, shs, name):
    n = len(dws)
    lands = [lax.empty((sh.R // 2, sh.C), _WIRE) for sh in shs]

    def body(*refs):
        for cp in _rs_pair_copies(refs[:n], refs[n:2 * n], shs, refs[2 * n], refs[2 * n + 1]):
            cp.start()
        refs[-1][...] = jnp.zeros_like(refs[-1])

    out = pl.pallas_call(
        body, name=name, in_specs=[_HBM] * (2 * n),
        out_shape=(pltpu.SemaphoreType.DMA((4 * n,)), pltpu.SemaphoreType.DMA((4 * n,)),
                   *[pltpu.HBM(a.shape, a.dtype) for a in (*dws, *lands)], S((8, 128), _F32)),
        out_specs=(_SEM, _SEM, *[_HBM] * (2 * n), pl.BlockSpec(memory_space=pltpu.VMEM)),
        input_output_aliases={t: 2 + t for t in range(2 * n)},
        compiler_params=pltpu.CompilerParams(has_side_effects=_EFFECT), **_KW,
    )(*[_in_hbm(a) for a in (*dws, *lands)])
    return out[0], out[1], list(out[2:2 + n]), list(out[2 + n:2 + 2 * n]), out[-1]


def _rs_pair_wait(dws, lands, send_sems, recv_sems, after, shs, name):
    n = len(dws)

    def body(*refs):
        cps = _rs_pair_copies(refs[:n], refs[n:2 * n], shs, refs[2 * n], refs[2 * n + 1])
        for cp in cps:
            cp.wait_send()
        for cp in cps:
            cp.wait_recv()

    out = pl.pallas_call(
        body, name=name, in_specs=[*[_HBM] * (2 * n), _SEM, _SEM, _ANY], out_specs=[_HBM] * (2 * n),
        out_shape=[pltpu.HBM(a.shape, a.dtype) for a in (*dws, *lands)], input_output_aliases={t: t for t in range(2 * n)},
        compiler_params=pltpu.CompilerParams(has_side_effects=_EFFECT), **_KW,
    )(*dws, *lands, send_sems, recv_sems, after)
    return list(out[n:])


def _rs_pair_add(dw32, recv, sh, idx):
    tr, tc = _tile(sh.Q, 512, 16), _tile(sh.C, 2048)
    nb = sh.Q // tr

    def body(i_ref, c_ref, a_ref, b_ref, ow_ref):
        ow_ref[...] = (a_ref[...] + b_ref[...].astype(_F32)).astype(ow_ref.dtype)

    if sh.by_cols:
        a_map = lambda j, i, b, si, sc: (sc[0] * 4 * nb + j * nb + i, b)
    else:
        a_map = lambda j, i, b, si, sc: (j * 2 * nb + sc[0] * nb + i, b)
    h_spec = BS((tr, tc), lambda j, i, b, si, sc: (j * nb + i, b))
    return _call(
        body, "rs_pair_add",
        grid_spec=pltpu.PrefetchScalarGridSpec(num_scalar_prefetch=2, grid=(4, nb, sh.C // tc),
                                               in_specs=[BS((tr, tc), a_map), h_spec], out_specs=h_spec),
        out_shape=S((sh.R // 2, sh.C), _WIRE), compiler_params=_cp("parallel", "parallel", "parallel"),
    )(*idx, dw32, recv)


def _rs_chip_start(pws, shs, name):
    n = len(pws)
    lands = [lax.empty((3, sh.Rh, sh.Cs), _WIRE) for sh in shs]

    def body(*refs):
        ins, lnd = refs[:n], refs[n:2 * n]
        send_sems, recv_sems = refs[2 * n], refs[2 * n + 1]
        token = refs[-1]
        x, y, c, me, chips, chip_ids = _where_am_i()
        for t in range(n):
            for r in range(3):
                _remote(shs[t].half_piece(ins[t], chip_ids[r]), lnd[t].at[r], send_sems.at[3 * t + r], recv_sems.at[3 * t + r],
                        (*chips[r], c)).start()
        token[...] = jnp.zeros_like(token)

    out = pl.pallas_call(
        body, name=name, in_specs=[_HBM] * (2 * n),
        out_shape=(pltpu.SemaphoreType.DMA((3 * n,)), pltpu.SemaphoreType.DMA((3 * n,)),
                   *[pltpu.HBM(a.shape, a.dtype) for a in (*pws, *lands)], S((8, 128), _F32)),
        out_specs=(_SEM, _SEM, *[_HBM] * (2 * n), pl.BlockSpec(memory_space=pltpu.VMEM)),
        input_output_aliases={t: 2 + t for t in range(2 * n)},
        compiler_params=pltpu.CompilerParams(has_side_effects=_EFFECT), **_KW,
    )(*[_in_hbm(a) for a in (*pws, *lands)])
    return out[0], out[1], list(out[2:2 + n]), list(out[2 + n:2 + 2 * n]), out[-1]


def _rs_chip_wait(pws, lands, send_sems, recv_sems, after, shs, name):
    n = len(pws)
    after = list(after) if isinstance(after, (list, tuple)) else [after]

    def body(*refs):
        ins, lnd = refs[:n], refs[n:2 * n]
        ssem, rsem = refs[2 * n], refs[2 * n + 1]
        x, y, c, me, chips, chip_ids = _where_am_i()
        for t in range(n):
            for r in range(3):
                cp = _remote(shs[t].half_piece(ins[t], chip_ids[r]), lnd[t].at[r], ssem.at[3 * t + r], rsem.at[3 * t + r], (*chips[r], c))
                cp.wait_send()
        for t in range(n):
            for r in range(3):
                cp = _remote(shs[t].half_piece(ins[t], chip_ids[r]), lnd[t].at[r], ssem.at[3 * t + r], rsem.at[3 * t + r], (*chips[r], c))
                cp.wait_recv()

    out = pl.pallas_call(
        body, name=name, in_specs=[*[_HBM] * (2 * n), _SEM, _SEM, *[_ANY] * len(after)], out_specs=[_HBM] * (2 * n),
        out_shape=[pltpu.HBM(a.shape, a.dtype) for a in (*pws, *lands)], input_output_aliases={t: t for t in range(2 * n)},
        compiler_params=pltpu.CompilerParams(has_side_effects=_EFFECT), **_KW,
    )(*pws, *lands, send_sems, recv_sems, *after)
    return list(out[n:])


def _rs_chip_add(dw32, pair, recv, sh, idx, g_prev, l, L):
    tr, tc = _tile(sh.Rh, 512, 16), _tile(sh.Cs, 2048)
    nr, nc = sh.Rh // tr, sh.Cs // tc

    def body(i_ref, c_ref, d_ref, a_ref, b_ref, *rest):
        rest[-1][...] = ((d_ref[...] + a_ref[...].astype(_F32)) + b_ref[0].astype(_F32) + b_ref[1].astype(_F32)
                         + b_ref[2].astype(_F32))

    if sh.by_cols:
        d_map = lambda a, b, si, sc: (sc[0] * nr + a, si[0] * nc + b)
        a_map = lambda a, b, si, sc: (a, si[0] * nc + b)
    else:
        d_map = lambda a, b, si, sc: ((2 * si[0] + sc[0]) * nr + a, b)
        a_map = lambda a, b, si, sc: (si[0] * nr + a, b)
    in_specs = [BS((tr, tc), d_map), BS((tr, tc), a_map), BS((3, tr, tc), lambda a, b, si, sc: (0, a, b))]
    args = [*idx, dw32, pair, recv]
    if g_prev is not None:
        in_specs.append(_ANY)
        args.append(g_prev)
    return _call(
        body, "rs_chip_add",
        grid_spec=pltpu.PrefetchScalarGridSpec(num_scalar_prefetch=2, grid=(nr, nc), in_specs=in_specs,
                                               out_specs=BS((None, tr, tc), lambda a, b, si, sc: (l, sc[0] * nr + a, b))),
        out_shape=S((L, sh.Rs, sh.Cs), _F32), input_output_aliases={} if g_prev is None else {5: 0},
        compiler_params=_cp("parallel", "parallel"),
    )(*args)


def _rs_pair_share(gs, ls, shs, name):
    n = len(gs)

    def body(*refs):
        bufs = refs[n:2 * n]
        send_sems, recv_sems = refs[2 * n:]
        x, y, c, *_ = _where_am_i()
        sib = (x, y, 1 - c)
        cps = []
        for t in range(n):
            mine = shs[t].shard_half(bufs[t].at[ls[t]], c)
            cps.append(_remote(mine, mine, send_sems.at[t], recv_sems.at[t], sib))
        for cp in cps:
            cp.start()
        for t in range(n):
            other = shs[t].shard_half(bufs[t].at[ls[t]], 1 - c)
            _remote(other, other, send_sems.at[t], recv_sems.at[t], sib).wait_recv()
        for cp in cps:
            cp.wait_send()

    return _call(
        body, name, in_specs=[_ANY] * n, out_specs=[_ANY] * n, out_shape=[S(g.shape, g.dtype) for g in gs],
        input_output_aliases={t: t for t in range(n)},
        scratch_shapes=[pltpu.SemaphoreType.DMA((n,)), pltpu.SemaphoreType.DMA((n,))],
        compiler_params=pltpu.CompilerParams(has_side_effects=True),
    )(*gs)


def _all_reduce_small(xs, dep=None):
    M = xs.shape[0]
    deps = [] if dep is None else [dep]

    def body(x_ref, *rest):
        tot_ref, out_ref, send_sems, recv_sems, local_sem = rest[len(deps):]
        x, y, c, me, chips, chip_ids = _where_am_i()
        sib = (x, y, 1 - c)

        def rows(dev):
            return out_ref.at[pl.ds(_mo((4 * dev[0] + 2 * dev[1] + dev[2]) * M, 8), M), :]

        def copy(k, block, to, src=None):
            return _remote(rows(block) if src is None else src, rows(block), send_sems.at[k], recv_sems.at[k], to)

        mine = pltpu.make_async_copy(x_ref, rows((x, y, c)), local_sem)
        mine.start()
        first = [copy(0, (x, y, c), sib, src=x_ref)]
        first += [copy(1 + j, (x, y, c), (*chip, c), src=x_ref) for j, chip in enumerate(chips)]
        for cp in first:
            cp.start()
        passed = [copy(4 + j, (*chip, c), sib) for j, chip in enumerate(chips)]
        for j, chip in enumerate(chips):
            copy(1 + j, (*chip, c), (x, y, c)).wait_recv()
            passed[j].start()
        copy(0, sib, (x, y, c)).wait_recv()
        for j, chip in enumerate(chips):
            copy(4 + j, (*chip, 1 - c), (x, y, c)).wait_recv()
        for cp in first + passed:
            cp.wait_send()
        mine.wait()
        tot = out_ref[pl.ds(0, M), :]
        for d in range(1, 8):
            tot = tot + out_ref[pl.ds(d * M, M), :]
        tot_ref[...] = tot

    vm = pl.BlockSpec(memory_space=pltpu.VMEM)
    return _call(
        body, "all_reduce_small", in_specs=[vm] + [_ANY] * len(deps), out_specs=[vm, vm],
        out_shape=[S((M, 128), _F32), S((8 * M, 128), _F32)],
        scratch_shapes=[pltpu.SemaphoreType.DMA((7,)), pltpu.SemaphoreType.DMA((7,)), pltpu.SemaphoreType.DMA],
        compiler_params=_cp(has_side_effects=True),
    )(xs, *deps)[0]


def _reduce_scatter_begin(dws, shs, l):
    ssem, rsem, dww, lands, token = _rs_pair_start([d[1] for d in dws], shs, f"rs_pair_start_{l}")
    return ([d[0] for d in dws], dww, lands, ssem, rsem), token


def _reduce_scatter_middle(state, after, shs, idx, l):
    dw32s, dww, lands, ssem, rsem = state
    recv_a = _rs_pair_wait(dww, lands, ssem, rsem, after, shs, f"rs_pair_wait_{l}")
    pws = [_rs_pair_add(d32, ra, sh, idx) for d32, ra, sh in zip(dw32s, recv_a, shs)]
    ssem, rsem, pws, lands, token = _rs_chip_start(pws, shs, f"rs_chip_start_{l}")
    return (dw32s, recv_a, pws, lands, ssem, rsem), token


def _reduce_scatter_end(state, after, tensors, shs, gstack, idx, l):
    dw32s, recv_a, pws, lands, ssem, rsem = state
    recv_b = _rs_chip_wait(pws, lands, ssem, rsem, after, shs, f"rs_chip_wait_{l}")
    gs = [_rs_chip_add(d32, ra, rb, sh, idx, gstack[name], i, L)
          for d32, ra, rb, sh, (name, i, L) in zip(dw32s, recv_a, recv_b, shs, tensors)]
    gs = _rs_pair_share(gs, [i for _, i, _ in tensors], shs, "rs_pair_share")
    for (name, _, _), g in zip(tensors, gs):
        gstack[name] = g


def _pack(parts):
    out = []
    for p in parts:
        p2 = p.reshape(-1, 128)
        pad = (-p2.shape[0]) % 8
        out.append(jnp.pad(p2, ((0, pad), (0, 0))) if pad else p2)
    return jnp.concatenate(out, axis=0)


def _unpack(packed, like):
    out, at = [], 0
    for p in like:
        n = p.size // 128
        out.append(packed[at:at + n].reshape(p.shape))
        at += n + ((-n) % 8)
    return out


def kernel(x, mem, g_mix, g_ffn, w_in_a, g_v_a, w_spatial, b_spatial, w_in_b, g_q_b, g_k_b, g_mem, w_mem_kv, g_mq, g_mk, w_out, w_gate_up, w_down, loss_target, m_g_mix, m_g_ffn, m_w_in_a, m_g_v_a, m_w_spatial, m_b_spatial, m_w_in_b, m_g_q_b, m_g_k_b, m_g_mem, m_w_mem_kv, m_g_mq, m_g_mk, m_w_out, m_w_gate_up, m_w_down, v_g_mix, v_g_ffn, v_w_in_a, v_g_v_a, v_w_spatial, v_b_spatial, v_w_in_b, v_g_q_b, v_g_k_b, v_g_mem, v_w_mem_kv, v_g_mq, v_g_mk, v_w_out, v_w_gate_up, v_w_down):
    xs = x[0]
    mem2 = mem[0]
    target = loss_target[0]
    T, D = xs.shape
    depth = g_mix.shape[0]
    MEMW = w_mem_kv.shape[2] // 2
    TOK = D - MEMW
    KV = (w_in_b.shape[2] * 4 - TOK - MEMW) // 2
    QPK = TOK // KV
    F = w_gate_up.shape[2] * 4 // 2

    idx = ((2 * lax.axis_index("x") + lax.axis_index("y")).astype(jnp.int32).reshape(1), lax.axis_index("c").astype(jnp.int32).reshape(1))

    big = {
        "w_in_a": (w_in_a, _Shard(D, w_in_a.shape[2] * 4, True)),
        "w_in_b": (w_in_b, _Shard(D, w_in_b.shape[2] * 4, True)),
        "w_mem_kv": (w_mem_kv, _Shard(D, 2 * MEMW, False)),
        "w_out": (w_out, _Shard(D, D, False)),
        "w_gate_up": (w_gate_up, _Shard(D, 2 * F, True)),
        "w_down": (w_down, _Shard(F, D, False)),
    }

    def layer_tensors(l):
        n_in = "w_in_a" if l % 2 == 0 else "w_in_b"
        return [(n_in, l // 2, big[n_in][0].shape[0])] + [(n, l, depth) for n in ("w_mem_kv", "w_out", "w_gate_up", "w_down")]

    def layer_shards(l):
        return [big[n][1] for n, _, _ in layer_tensors(l)]

    full = {n: [None] * w.shape[0] for n, (w, _) in big.items()}
    flying = {}

    def start_layer(l, dep):
        tens = layer_tensors(l)
        token = dep
        for gi, group in enumerate([tens[:2], tens[2:3], tens[3:4], tens[4:]] if l == 0 else [tens]):
            shs = [big[n][1] for n, _, _ in group]
            bufs = [_cast_into_full(big[n][0], i, big[n][1], idx, dep=token) for n, i, _ in group]
            ssem, rsem, bufs, token = _gather_start(bufs, shs, f"gather_start_{l}_{gi}")
            for n, i, _ in group:
                flying[(n, i)] = dict(group=group, shs=shs, state=(ssem, rsem, bufs), name=f"{l}_{gi}", passing=False)
        return token

    def land(n, i, after):
        fl = flying[(n, i)]
        ssem, rsem, bufs = fl["state"]
        bufs = _gather_wait(bufs, ssem, rsem, after, fl["shs"], "gather_wait_" + fl["name"])
        ssem, rsem, bufs, token = _pass_on_start(bufs, fl["shs"], "pass_on_start_" + fl["name"])
        fl.update(state=(ssem, rsem, bufs), passing=True)
        return token

    def weight(n, i, after):
        if full[n][i] is None:
            fl = flying[(n, i)]
            ssem, rsem, bufs = fl["state"]
            if fl["passing"]:
                bufs = _pass_on_wait(bufs, ssem, rsem, after, fl["shs"], "pass_on_wait_" + fl["name"])
            else:
                bufs = _gather_wait(bufs, ssem, rsem, after, fl["shs"], "gather_wait_" + fl["name"])
                bufs = _gather_pass_on(bufs, fl["shs"], "gather_pass_on")
            for (m, j, _), b in zip(fl["group"], bufs):
                full[m][j] = b
        return full[n][i]

    after = None
    for l in range(depth):
        after = start_layer(l, after)
    tabs = _rope_tables(T)

    saved = []
    xc = xs
    for l in range(depth):
        is_a = l % 2 == 0
        li = l // 2
        w_in = weight("w_in_a" if is_a else "w_in_b", li, after)
        h = _rmsnorm_fwd(xc, g_mix[l], "rmsnorm_fwd")
        z = _mm_nn("mm_in", h, w_in, pm=2048, pn=512)
        st = dict(x=xc, h=h, z=z)
        if is_a:
            ws_m = w_spatial[li].astype(_MXU)
            st["ws_m"], st["wst_m"], st["b_t"] = ws_m, jnp.swapaxes(ws_m, 1, 2), b_spatial[li].T
            tok = _mixer_a_fwd(z, g_v_a[li], ws_m, st["b_t"], TOK)
            qblk = 2 * TOK // MEMW
        else:
            q, k, v = _qk_rope_fwd(z, g_q_b[li], g_k_b[li], tabs, TOK, KV)
            tok, stat = _attn_fwd(q, k, v, QPK)
            st["q"], st["k"], st["v"], st["stat"] = q, k, v, stat
            qblk = (TOK + 2 * KV) // MEMW
        mem_n = _rmsnorm_fwd(mem2, g_mem[l], "rmsnorm_mem")
        kv = _mm_nn("mm_memkv", mem_n, weight("w_mem_kv", l, z))
        mo = _mem_fwd(z, qblk, kv, g_mq[l], g_mk[l], MEMW)
        cat = jnp.concatenate([tok, mo], axis=1)
        x1 = _mm_nn("mm_out", cat, weight("w_out", l, cat), add=xc)
        h2 = _rmsnorm_fwd(x1, g_ffn[l], "rmsnorm_fwd")
        act, gu = _ffn_gate_up(h2, weight("w_gate_up", l, h2))
        w_down_l = weight("w_down", l, act)
        token = land(*layer_tensors(l + 1)[0][:2], act) if l + 1 < depth else None
        xc = _mm_nn("mm_down", act, w_down_l, add=x1, pm=512, pn=512, pk=8192, dep=token)
        after = xc
        st.update(mem_n=mem_n, kv=kv, qblk=qblk, cat=cat, x1=x1, h2=h2, act=act, gu=gu)
        saved.append(st)

    dx, dxm, sq = _loss_head(xc, target)
    loss = lax.psum(sq[0, 0] * (0.5 / D), ("x", "y", "c"))

    gsm = {n: [None] * len(a) for n, a in dict(g_mix=g_mix, g_ffn=g_ffn, g_v_a=g_v_a, w_spatial=w_spatial, b_spatial=b_spatial,
                                                g_q_b=g_q_b, g_k_b=g_k_b, g_mem=g_mem, g_mq=g_mq, g_mk=g_mk).items()}
    gstack = {n: None for n in big}
    pairing, chipping, token = None, None, None

    def advance(after):
        nonlocal pairing, chipping
        state, tok = _reduce_scatter_middle(pairing[0], after, layer_shards(pairing[1]), idx, pairing[1])
        if chipping is not None:
            _reduce_scatter_end(chipping[0], tok, layer_tensors(chipping[1]), layer_shards(chipping[1]), gstack, idx, chipping[1])
        pairing, chipping = None, (state, pairing[1])
        return tok

    for l in reversed(range(depth)):
        st = saved[l]
        is_a = l % 2 == 0
        li = l // 2
        gbig = {}
        dgu = _ffn_dact(dxm, full["w_down"][l], st["gu"], dep=token)
        token = advance(dgu) if pairing is not None else None
        gbig["w_down"] = _mm_tn_dual("mm_dw_down", st["act"], dxm, pm=1408)
        dh2 = _ffn_dh(dgu, full["w_gate_up"][l], dep=token)
        gbig["w_gate_up"] = _ffn_dwgu(st["h2"], dgu)
        dx, dxm, dg = _rmsnorm_bwd(st["x1"], g_ffn[l], dh2, dx, "rmsnorm_bwd")
        gsm["g_ffn"][l] = dg[0]
        dcat = _mm_nt("mm_dcat", dxm, full["w_out"][l])
        gbig["w_out"] = _mm_tn_dual("mm_dw_out", st["cat"], dxm)
        dzq, dkn, dvm, dgq = _mem_bwd(st["z"], st["qblk"], st["kv"], g_mq[l], g_mk[l], dcat, TOK // MEMW, MEMW)
        dkv, dgk = _memkv_bwd(st["kv"], dkn, dvm, g_mk[l], MEMW)
        gsm["g_mq"][l], gsm["g_mk"][l] = dgq[0], dgk[0]
        gbig["w_mem_kv"] = _mm_tn_dual("mm_dw_memkv", st["mem_n"], dkv)
        dmem_n = _mm_nt("mm_dmemn", dkv, full["w_mem_kv"][l])
        gsm["g_mem"][l] = _rmsnorm_bwd(mem2, g_mem[l], dmem_n, None, "rmsnorm_bwd_mem")[2][0]
        if is_a:
            dz_tok, dws, dbs, dgv = _mixer_a_bwd(st["z"], dcat, g_v_a[li], st["ws_m"], st["wst_m"], st["b_t"], TOK)
            gsm["w_spatial"][li], gsm["b_spatial"][li], gsm["g_v_a"][li] = dws, dbs[:, :, 0], dgv[0]
            dz = jnp.concatenate([dz_tok, dzq], axis=1)
        else:
            dq, dk, dv = _attn_bwd(st["q"], st["k"], st["v"], dcat, st["cat"], st["stat"], QPK)
            dz_qk, dgq_b, dgk_b = _qk_rope_bwd(st["z"], dq, dk, g_q_b[li], g_k_b[li], tabs, TOK, KV)
            gsm["g_q_b"][li], gsm["g_k_b"][li] = dgq_b[0], dgk_b[0]
            dz = jnp.concatenate([dz_qk, dv, dzq], axis=1)
        n_in = "w_in_a" if is_a else "w_in_b"
        dh = _mm_nt("mm_dh", dz, full[n_in][li])
        gbig[n_in] = _mm_tn_dual("mm_dw_in", st["h"], dz, pm=2048, pn=512)
        dx, dxm, dg = _rmsnorm_bwd(st["x"], g_mix[l], dh, dx, "rmsnorm_bwd")
        gsm["g_mix"][l] = dg[0]
        state, token = _reduce_scatter_begin([gbig[n] for n, _, _ in layer_tensors(l)], layer_shards(l), l)
        pairing = (state, l)

    small = ["g_mix", "g_ffn", "g_v_a", "w_spatial", "b_spatial", "g_q_b", "g_k_b", "g_mem", "g_mq", "g_mk"]
    env = dict(g_mix=g_mix, g_ffn=g_ffn, g_v_a=g_v_a, w_spatial=w_spatial, b_spatial=b_spatial, g_q_b=g_q_b, g_k_b=g_k_b,
               g_mem=g_mem, g_mq=g_mq, g_mk=g_mk,
               m_g_mix=m_g_mix, m_g_ffn=m_g_ffn, m_g_v_a=m_g_v_a, m_w_spatial=m_w_spatial, m_b_spatial=m_b_spatial,
               m_g_q_b=m_g_q_b, m_g_k_b=m_g_k_b, m_g_mem=m_g_mem, m_g_mq=m_g_mq, m_g_mk=m_g_mk,
               v_g_mix=v_g_mix, v_g_ffn=v_g_ffn, v_g_v_a=v_g_v_a, v_w_spatial=v_w_spatial, v_b_spatial=v_b_spatial,
               v_g_q_b=v_g_q_b, v_g_k_b=v_g_k_b, v_g_mem=v_g_mem, v_g_mq=v_g_mq, v_g_mk=v_g_mk,
               m_w_in_a=m_w_in_a, m_w_in_b=m_w_in_b, m_w_mem_kv=m_w_mem_kv, m_w_out=m_w_out, m_w_gate_up=m_w_gate_up, m_w_down=m_w_down,
               v_w_in_a=v_w_in_a, v_w_in_b=v_w_in_b, v_w_mem_kv=v_w_mem_kv, v_w_out=v_w_out, v_w_gate_up=v_w_gate_up, v_w_down=v_w_down)
    like = [env[n] for n in small]
    g_small = _all_reduce_small(_pack([jnp.stack(gsm[n]) for n in small]), dep=token)

    res = {}
    outs = _adamw(_pack(like)[None], g_small[None], _pack([env["m_" + n] for n in small])[None],
                  _pack([env["v_" + n] for n in small])[None], "adamw_small")
    unpacked = [_unpack(o[0], like) for o in outs]
    for k, n in enumerate(small):
        res[n] = [u[k] for u in unpacked]
    advance(outs[1])
    pending = chipping
    last = {n: i for n, i, _ in layer_tensors(pending[1])}
    early = {}
    for n, (w, _) in big.items():
        L = w.shape[0]
        if n not in last:
            res[n] = _adamw(w, gstack[n], env["m_" + n], env["v_" + n], "adamw_" + n)
        elif L > 1:
            assert last[n] == 0
            early[n] = _adamw(w, gstack[n], env["m_" + n], env["v_" + n], "adamw_early_" + n, l0=1)
    done = [o[1] for o in early.values()] + [res[n][1] for n in big if n in res] + [res[small[0]][1]]
    _reduce_scatter_end(pending[0], done, layer_tensors(pending[1]), layer_shards(pending[1]), gstack, idx, pending[1])
    for n in last:
        res[n] = _adamw(big[n][0], gstack[n], env["m_" + n], env["v_" + n], "adamw_last_" + n, l0=0, l1=1, prev=early.get(n))

    order = ["g_mix", "g_ffn", "w_in_a", "g_v_a", "w_spatial", "b_spatial", "w_in_b", "g_q_b", "g_k_b", "g_mem", "w_mem_kv",
             "g_mq", "g_mk", "w_out", "w_gate_up", "w_down"]
    return (loss, dx.reshape(1, T, D), *[res[n][0] for n in order], *[res[n][1] for n in order],
            *[res[n][2] for n in order], *[res[n][3] for n in order])
```

```python
import jax
import jax.numpy as jnp
import numpy as np
from jax import lax
from jax.experimental import pallas as pl
from jax.experimental.pallas import tpu as pltpu

_F32 = jnp.float32
_MXU = jnp.bfloat16
_WIRE = jnp.bfloat16
_KW = {}

EPS = 1e-6
HEAD = 128
CHUNK = 128
GRID_W = 64
ROPE_THETA = 10000.0
ADAM_LR, ADAM_B1, ADAM_B2, ADAM_EPS, ADAM_WD, ADAM_STEP = 0.001, 0.9, 0.999, 1e-08, 0.01, 10
_SQRT_HALF = float(np.sqrt(0.5))
_INV_SQRT_2PI = float(1.0 / np.sqrt(2.0 * np.pi))
_VMEM_LIMIT = 56 * 1024 * 1024
_MESH = pl.DeviceIdType.MESH

_NN = (((1,), (0,)), ((), ()))
_NT = (((1,), (1,)), ((), ()))
_TN = (((0,), (0,)), ((), ()))

S = jax.ShapeDtypeStruct
BS = pl.BlockSpec
_ANY = pl.BlockSpec(memory_space=pl.ANY)


def _tile(n, pref, mult=128):
    if n <= pref:
        return n
    d = (pref // mult) * mult
    while d >= mult:
        if n % d == 0:
            return d
        d -= mult
    raise ValueError(f"no tile for {n} (pref {pref}, mult {mult})")


def _mo(v, m):
    return v if isinstance(v, int) else pl.multiple_of(v, m)


def _cp(*sem, **kw):
    return pltpu.CompilerParams(dimension_semantics=sem or None, vmem_limit_bytes=_VMEM_LIMIT, **kw)


def _call(body, name, **kw):
    return pl.pallas_call(body, name=name, **kw, **_KW)


def _dot(a, b, dn=_NN):
    return lax.dot_general(a, b, dn, preferred_element_type=_F32)


def _gelu(x):
    return 0.5 * x * (1.0 + lax.erf(x * _SQRT_HALF))


def _gelu_grad(x):
    return 0.5 * (1.0 + lax.erf(x * _SQRT_HALF)) + x * jnp.exp(-0.5 * x * x) * _INV_SQRT_2PI


def _rstd(x):
    return lax.rsqrt(jnp.mean(x * x, axis=-1, keepdims=True) + EPS)


def _norm_bwd(dout, xhat, r, g):
    dy = dout * g
    return r * (dy - xhat * jnp.mean(dy * xhat, axis=-1, keepdims=True))


def _softmax(s):
    e = jnp.exp(s - jnp.max(s, axis=-1, keepdims=True))
    return e * (1.0 / jnp.sum(e, axis=-1, keepdims=True))


def _mm(name, a, b, a_spec, b_spec, dn, grid, acc_shape, out_shape, out_specs, epilogue, extra=(), extra_specs=(), dep=None):
    nk = grid[2]
    n_ex = len(extra)
    deps = [] if dep is None else [dep]
    multi = isinstance(out_shape, (list, tuple))
    n_out = len(out_shape) if multi else 1

    def body(*refs):
        a_ref, b_ref = refs[0], refs[1]
        ex = refs[2:2 + n_ex]
        outs = refs[2 + n_ex + len(deps):2 + n_ex + len(deps) + n_out]

        def prod():
            return _dot(a_ref[...].astype(_MXU), b_ref[...].astype(_MXU), dn)

        if nk == 1:
            epilogue(prod(), ex, outs)
        else:
            acc = refs[-1]
            k = pl.program_id(2)

            @pl.when(k == 0)
            def _():
                acc[...] = jnp.zeros_like(acc)

            acc[...] += prod()

            @pl.when(k == nk - 1)
            def _():
                epilogue(acc[...], ex, outs)

    return _call(
        body, name, grid=grid, in_specs=[a_spec, b_spec, *extra_specs] + [_ANY] * len(deps), out_specs=out_specs, out_shape=out_shape,
        scratch_shapes=[] if nk == 1 else [pltpu.VMEM(acc_shape, _F32)],
        compiler_params=_cp("parallel", "parallel", "arbitrary"),
    )(a, b, *extra, *deps)


def _ep_store(acc, ex, outs):
    for o in outs:
        o[...] = acc.astype(o.dtype)


def _ep_add(acc, ex, outs):
    outs[0][...] = (acc + ex[0][...]).astype(outs[0].dtype)


def _mm_nn(name, a, b, out_dtype=_F32, add=None, pm=1024, pn=1024, pk=2048, dep=None):
    M, K = a.shape
    N = b.shape[1]
    tm, tn, tk = _tile(M, pm, 8), _tile(N, pn), _tile(K, pk)
    o_spec = BS((tm, tn), lambda i, j, k: (i, j))
    return _mm(name, a, b, BS((tm, tk), lambda i, j, k: (i, k)), BS((tk, tn), lambda i, j, k: (k, j)), _NN,
               (M // tm, N // tn, K // tk), (tm, tn), S((M, N), out_dtype), o_spec,
               _ep_store if add is None else _ep_add,
               extra=() if add is None else (add,), extra_specs=() if add is None else (o_spec,), dep=dep)


def _mm_nt(name, a, b, out_dtype=_F32, pm=1024, pn=1024, pk=4096):
    M, K = a.shape
    N = b.shape[0]
    tm, tn, tk = _tile(M, pm, 8), _tile(N, pn), _tile(K, pk)
    return _mm(name, a, b, BS((tm, tk), lambda i, j, k: (i, k)), BS((tn, tk), lambda i, j, k: (j, k)), _NT,
               (M // tm, N // tn, K // tk), (tm, tn), S((M, N), out_dtype), BS((tm, tn), lambda i, j, k: (i, j)), _ep_store)


def _mm_tn_dual(name, a, b, pm=1024, pn=1024, pk=2048, dep=None):
    K, M = a.shape
    N = b.shape[1]
    tm, tn, tk = _tile(M, pm), _tile(N, pn), _tile(K, pk, 16)
    o_spec = BS((tm, tn), lambda i, j, k: (i, j))
    return _mm(name, a, b, BS((tk, tm), lambda i, j, k: (k, i)), BS((tk, tn), lambda i, j, k: (k, j)), _TN,
               (M // tm, N // tn, K // tk), (tm, tn), [S((M, N), _F32), S((M, N), _WIRE)], [o_spec, o_spec], _ep_store, dep=dep)


def _ffn_gate_up(h2, wgu):
    T, D = h2.shape
    F = wgu.shape[1] // 2
    tm, tn = _tile(T, 1024, 8), _tile(F, 512)
    nj = F // tn

    def body(a_ref, bg_ref, bu_ref, act_ref, gu_ref):
        a = a_ref[...]
        g = _dot(a, bg_ref[...])
        u = _dot(a, bu_ref[...])
        gu_ref[0] = g.astype(gu_ref.dtype)
        gu_ref[1] = u.astype(gu_ref.dtype)
        act_ref[...] = (g * (1.0 / (1.0 + jnp.exp(-g))) * u).astype(act_ref.dtype)

    return _call(
        body, "ffn_gate_up", grid=(T // tm, nj),
        in_specs=[BS((tm, D), lambda i, j: (i, 0)), BS((D, tn), lambda i, j: (0, j)), BS((D, tn), lambda i, j: (0, j + nj))],
        out_specs=[BS((tm, tn), lambda i, j: (i, j)), BS((2, tm, tn), lambda i, j: (0, i, j))],
        out_shape=[S((T, F), _MXU), S((2, T, F), _MXU)],
        compiler_params=_cp("parallel", "parallel"),
    )(h2, wgu, wgu)


def _ffn_dact(dxm, wdown, gu, dep=None):
    T, D = dxm.shape
    F = wdown.shape[0]
    tm, tn = _tile(T, 1024, 8), _tile(F, 512)
    deps = [] if dep is None else [dep]

    halves = 2 if tn % 256 == 0 else 1
    th = tn // halves

    def body(a_ref, b_ref, gu_ref, *rest):
        o_ref = rest[-1]
        a = a_ref[...]
        for s in range(halves):
            cols = slice(s * th, (s + 1) * th)
            d = _dot(a, b_ref[cols, :], _NT)
            g, u = gu_ref[0, :, cols].astype(_F32), gu_ref[1, :, cols].astype(_F32)
            sg = 1.0 / (1.0 + jnp.exp(-g))
            o_ref[0, :, cols] = (d * u * (sg * (1.0 + g * (1.0 - sg)))).astype(o_ref.dtype)
            o_ref[1, :, cols] = (d * (g * sg)).astype(o_ref.dtype)

    return _call(
        body, "ffn_dact", grid=(T // tm, F // tn),
        in_specs=[BS((tm, D), lambda i, j: (i, 0)), BS((tn, D), lambda i, j: (j, 0)), BS((2, tm, tn), lambda i, j: (0, i, j))]
        + [_ANY] * len(deps),
        out_specs=BS((2, tm, tn), lambda i, j: (0, i, j)), out_shape=S((2, T, F), _MXU),
        compiler_params=_cp("parallel", "parallel"),
    )(dxm, wdown, gu, *deps)


def _ffn_dh(dgu, wgu, dep=None):
    _, T, F = dgu.shape
    D = wgu.shape[0]
    tm, tn, tk = _tile(T, 1024, 8), _tile(D, 2048), _tile(F, 2048)
    nkf = F // tk
    return _mm("ffn_dh", dgu, wgu, BS((None, tm, tk), lambda i, j, k: (k // nkf, i, k % nkf)),
               BS((tn, tk), lambda i, j, k: (j, k)), _NT, (T // tm, D // tn, 2 * nkf), (tm, tn),
               S((T, D), _F32), BS((tm, tn), lambda i, j, k: (i, j)), _ep_store, dep=dep)


def _ffn_dwgu(h2, dgu):
    _, T, F = dgu.shape
    D = h2.shape[1]
    tm, tn, tk = _tile(D, 1024), _tile(F, 1408), _tile(T, 2048, 16)
    njf = F // tn
    o_spec = BS((tm, tn), lambda i, j, k: (i, j))
    return _mm("ffn_dwgu", h2, dgu, BS((tk, tm), lambda i, j, k: (k, i)),
               BS((None, tk, tn), lambda i, j, k: (j // njf, k, j % njf)), _TN, (D // tm, 2 * njf, T // tk), (tm, tn),
               [S((D, 2 * F), _F32), S((D, 2 * F), _WIRE)], [o_spec, o_spec], _ep_store)


def _rmsnorm_fwd(x, g, name, dep=None):
    T, D = x.shape
    tr = _tile(T, 512, 8)

    def body(x_ref, g_ref, *rest):
        xv = x_ref[...]
        rest[-1][...] = (xv * _rstd(xv) * g_ref[...]).astype(rest[-1].dtype)

    row = BS((tr, D), lambda i: (i, 0))
    deps = [] if dep is None else [dep]
    return _call(body, name, grid=(T // tr,), in_specs=[row, BS((1, D), lambda i: (0, 0))] + [_ANY] * len(deps), out_specs=row,
                 out_shape=S((T, D), _MXU), compiler_params=_cp("parallel"))(x, g.reshape(1, D), *deps)


def _rmsnorm_bwd(x, g, dh, dres, name):
    T, D = x.shape
    tr = _tile(T, 256, 8)
    has_res = dres is not None

    def body(*refs):
        x_ref, g_ref, dh_ref = refs[:3]
        dx_ref, dxm_ref, dg_ref = refs[-3:]

        @pl.when(pl.program_id(0) == 0)
        def _():
            dg_ref[...] = jnp.zeros_like(dg_ref)

        xv = x_ref[...]
        r = _rstd(xv)
        xhat = xv * r
        dh_v = dh_ref[...]
        dg_ref[...] += jnp.sum(dh_v * xhat, axis=0, keepdims=True)
        dx = _norm_bwd(dh_v, xhat, r, g_ref[...])
        if has_res:
            dx = dx + refs[3][...]
        dx_ref[...] = dx
        dxm_ref[...] = dx.astype(dxm_ref.dtype)

    row = BS((tr, D), lambda i: (i, 0))
    vec = BS((1, D), lambda i: (0, 0))
    return _call(body, name, grid=(T // tr,), in_specs=[row, vec, row] + ([row] if has_res else []),
                 out_specs=[row, row, vec], out_shape=[S((T, D), _F32), S((T, D), _MXU), S((1, D), _F32)],
                 compiler_params=_cp("arbitrary"))(x, g.reshape(1, D), dh, *([dres] if has_res else []))


def _loss_head(y, target):
    T, D = y.shape
    tr = _tile(T, 256, 8)

    def body(y_ref, t_ref, dy_ref, dym_ref, acc_ref):
        @pl.when(pl.program_id(0) == 0)
        def _():
            acc_ref[...] = jnp.zeros_like(acc_ref)

        err = y_ref[...] - t_ref[...]
        acc_ref[...] += jnp.sum(jnp.sum(err * err, axis=-1, keepdims=True), axis=0, keepdims=True)
        dy = err * (1.0 / D)
        dy_ref[...] = dy
        dym_ref[...] = dy.astype(dym_ref.dtype)

    row = BS((tr, D), lambda i: (i, 0))
    return _call(body, "loss_head", grid=(T // tr,), in_specs=[row, row],
                 out_specs=[row, row, BS((1, 128), lambda i: (0, 0))],
                 out_shape=[S((T, D), _F32), S((T, D), _MXU), S((1, 128), _F32)],
                 compiler_params=_cp("arbitrary"))(y, target)


def _mixa_blocks(T):
    return 2 if T % (2 * CHUNK) == 0 else 1


def _mixer_a_fwd(z, gv, ws_m, b_t, TOK):
    T = z.shape[0]
    G = TOK // HEAD
    CB = _mixa_blocks(T)
    R = CB * CHUNK

    def body(z_ref, gv_ref, ws_ref, bt_ref, o_ref):
        u = _gelu(z_ref[:, :TOK])
        v = _gelu(z_ref[:, TOK:])
        vn = (v * _rstd(v) * gv_ref[...]).astype(_MXU)
        for c in range(CB):
            rows = slice(c * CHUNK, (c + 1) * CHUNK)
            for g in range(G):
                cols = slice(g * HEAD, (g + 1) * HEAD)
                s = _dot(ws_ref[g], vn[rows, cols]) + bt_ref[:, g:g + 1]
                o_ref[rows, cols] = (u[rows, cols] * s).astype(o_ref.dtype)

    return _call(
        body, "mixer_a_fwd", grid=(T // R,),
        in_specs=[BS((R, 2 * TOK), lambda i: (i, 0)), BS((1, TOK), lambda i: (0, 0)),
                  BS((G, CHUNK, CHUNK), lambda i: (0, 0, 0)), BS((CHUNK, G), lambda i: (0, 0))],
        out_specs=BS((R, TOK), lambda i: (i, 0)), out_shape=S((T, TOK), _MXU), compiler_params=_cp("parallel"),
    )(z, gv.reshape(1, TOK), ws_m, b_t)


def _mixer_a_bwd(z, dcat, gv, ws_m, wst_m, b_t, TOK):
    T = z.shape[0]
    G = TOK // HEAD
    CB = _mixa_blocks(T)
    R = CB * CHUNK
    n = T // R

    def body(z_ref, d_ref, gv_ref, ws_ref, wst_ref, bt_ref, dz_ref, dws_ref, db_ref, dgv_ref, dvn_scr):
        i = pl.program_id(0)

        @pl.when(i == 0)
        def _():
            dws_ref[...] = jnp.zeros_like(dws_ref)
            db_ref[...] = jnp.zeros_like(db_ref)
            dgv_ref[...] = jnp.zeros_like(dgv_ref)

        zu = z_ref[:, :TOK]
        zv = z_ref[:, TOK:]
        u = _gelu(zu)
        v = _gelu(zv)
        r = _rstd(v)
        vhat = v * r
        gvv = gv_ref[...]
        vn = (vhat * gvv).astype(_MXU)
        d = d_ref[...]
        gpu = _gelu_grad(zu)
        for c in range(CB):
            rows = slice(c * CHUNK, (c + 1) * CHUNK)
            for g in range(G):
                cols = slice(g * HEAD, (g + 1) * HEAD)
                vn_cg = vn[rows, cols]
                s = _dot(ws_ref[g], vn_cg) + bt_ref[:, g:g + 1]
                d_cg = d[rows, cols]
                dz_ref[rows, cols] = (d_cg * s * gpu[rows, cols]).astype(dz_ref.dtype)
                ds = d_cg * u[rows, cols]
                ds_m = ds.astype(_MXU)
                dvn_scr[rows, cols] = _dot(wst_ref[g], ds_m)
                dws_ref[g] += _dot(ds_m, vn_cg, _NT)
                db_ref[g] += ds
        dvn = dvn_scr[...]
        dgv_ref[...] += jnp.sum(dvn * vhat, axis=0, keepdims=True)
        dv = _norm_bwd(dvn, vhat, r, gvv)
        dz_ref[:, TOK:] = (dv * _gelu_grad(zv)).astype(dz_ref.dtype)

        @pl.when(i == n - 1)
        def _():
            for g in range(G):
                db_ref[g] = jnp.broadcast_to(jnp.sum(db_ref[g], axis=1, keepdims=True), (CHUNK, CHUNK))

    full3 = BS((G, CHUNK, CHUNK), lambda i: (0, 0, 0))
    return _call(
        body, "mixer_a_bwd", grid=(n,),
        in_specs=[BS((R, 2 * TOK), lambda i: (i, 0)), BS((R, TOK), lambda i: (i, 0)), BS((1, TOK), lambda i: (0, 0)),
                  full3, full3, BS((CHUNK, G), lambda i: (0, 0))],
        out_specs=[BS((R, 2 * TOK), lambda i: (i, 0)), full3, full3, BS((1, TOK), lambda i: (0, 0))],
        out_shape=[S((T, 2 * TOK), _MXU), S((G, CHUNK, CHUNK), _F32), S((G, CHUNK, CHUNK), _F32), S((1, TOK), _F32)],
        scratch_shapes=[pltpu.VMEM((R, TOK), _F32)], compiler_params=_cp("arbitrary"),
    )(z, dcat, gv.reshape(1, TOK), ws_m, wst_m, b_t)


def _rope_tables(T):
    n_rows = T // GRID_W
    rows = jnp.broadcast_to(jnp.arange(n_rows)[:, None], (n_rows, GRID_W)).reshape(T)
    cols = jnp.broadcast_to(jnp.arange(GRID_W)[None, :], (n_rows, GRID_W)).reshape(T)
    pairs = HEAD // 4
    freqs = ROPE_THETA ** (-jnp.arange(pairs, dtype=_F32) / pairs)
    ang_r = rows.astype(_F32)[:, None] * freqs
    ang_c = cols.astype(_F32)[:, None] * freqs
    ang = jnp.concatenate([ang_r, ang_r, ang_c, ang_c], axis=-1)
    cos, sin = jnp.cos(ang), jnp.sin(ang)
    first = (jnp.arange(HEAD) % (HEAD // 2)) < (HEAD // 4)
    return cos, jnp.where(first, -sin, 0.0), jnp.where(first, 0.0, sin)


def _rope(x, cs, sa, sb):
    return x * cs + pltpu.roll(x, 96, 1) * sa + pltpu.roll(x, 32, 1) * sb


def _qk_rope_fwd(z, gq, gk, tabs, TOK, KV):
    T = z.shape[0]
    R = _tile(T, 512, 8)
    W = TOK + 2 * KV

    def body(z_ref, gq_ref, gk_ref, cos_ref, sa_ref, sb_ref, q_ref, k_ref, v_ref):
        cs, sa, sb = cos_ref[...], sa_ref[...], sb_ref[...]
        for h in range((TOK + KV) // HEAD):
            cols = slice(h * HEAD, (h + 1) * HEAD)
            xv = z_ref[:, cols]
            xn = xv * _rstd(xv) * (gq_ref[...] if h < TOK // HEAD else gk_ref[...])
            out = _rope(xn, cs, sa, sb)
            if h < TOK // HEAD:
                q_ref[:, cols] = out.astype(q_ref.dtype)
            else:
                k_ref[:, h * HEAD - TOK:(h + 1) * HEAD - TOK] = out.astype(k_ref.dtype)
        v_ref[...] = z_ref[:, TOK + KV:].astype(v_ref.dtype)

    vec = BS((1, HEAD), lambda i: (0, 0))
    tab = BS((R, HEAD), lambda i: (i, 0))
    return _call(
        body, "qk_rope_fwd", grid=(T // R,), in_specs=[BS((R, W), lambda i: (i, 0)), vec, vec, tab, tab, tab],
        out_specs=[BS((R, TOK), lambda i: (i, 0)), BS((R, KV), lambda i: (i, 0)), BS((R, KV), lambda i: (i, 0))],
        out_shape=[S((T, TOK), _MXU), S((T, KV), _MXU), S((T, KV), _MXU)], compiler_params=_cp("parallel"),
    )(z, gq.reshape(1, HEAD), gk.reshape(1, HEAD), *tabs)


def _qk_rope_bwd(z, dq, dk, gq, gk, tabs, TOK, KV):
    T = z.shape[0]
    R = _tile(T, 512, 8)
    W = TOK + KV

    def body(z_ref, dq_ref, dk_ref, gq_ref, gk_ref, cos_ref, sa_ref, sb_ref, dz_ref, dgq_ref, dgk_ref):
        @pl.when(pl.program_id(0) == 0)
        def _():
            dgq_ref[...] = jnp.zeros_like(dgq_ref)
            dgk_ref[...] = jnp.zeros_like(dgk_ref)

        cs, sa, sb = cos_ref[...], sa_ref[...], sb_ref[...]
        for h in range(W // HEAD):
            cols = slice(h * HEAD, (h + 1) * HEAD)
            is_q = h < TOK // HEAD
            do = dq_ref[:, cols] if is_q else dk_ref[:, h * HEAD - TOK:(h + 1) * HEAD - TOK]
            dxn = do * cs - pltpu.roll(do, 96, 1) * sa - pltpu.roll(do, 32, 1) * sb
            xv = z_ref[:, cols]
            r = _rstd(xv)
            xhat = xv * r
            dg_ref = dgq_ref if is_q else dgk_ref
            dg_ref[...] += jnp.sum(dxn * xhat, axis=0, keepdims=True)
            dz_ref[:, cols] = _norm_bwd(dxn, xhat, r, gq_ref[...] if is_q else gk_ref[...]).astype(dz_ref.dtype)

    vec = BS((1, HEAD), lambda i: (0, 0))
    tab = BS((R, HEAD), lambda i: (i, 0))
    return _call(
        body, "qk_rope_bwd", grid=(T // R,),
        in_specs=[BS((R, W), lambda i: (i, 0)), BS((R, TOK), lambda i: (i, 0)), BS((R, KV), lambda i: (i, 0)), vec, vec, tab, tab, tab],
        out_specs=[BS((R, W), lambda i: (i, 0)), vec, vec],
        out_shape=[S((T, W), _MXU), S((1, HEAD), _F32), S((1, HEAD), _F32)], compiler_params=_cp("arbitrary"),
    )(z, dq, dk, gq.reshape(1, HEAD), gk.reshape(1, HEAD), *tabs)


_ATTN_C2 = float(HEAD ** -0.5 * np.log2(np.e))


def _attn_fwd(q, k, v, QPK):
    T, TOK = q.shape
    KVH = k.shape[1] // HEAD
    tq = _tile(T, 256, 8)
    W = QPK * HEAD

    def body(q_ref, k_ref, v_ref, o_ref, st_ref, vaug):
        @pl.when(pl.program_id(1) == 0)
        def _():
            vaug[:, :HEAD] = v_ref[...]
            vaug[:, HEAD:] = jnp.ones((T, HEAD), vaug.dtype)

        kk, va = k_ref[...], vaug[...]
        for g in range(QPK):
            cols = slice(g * HEAD, (g + 1) * HEAD)
            s = _dot(q_ref[:, cols], kk, _NT)
            m = jnp.max(s, axis=-1, keepdims=True)
            ov = _dot(jnp.exp2((s - m) * _ATTN_C2).astype(_MXU), va)
            l = ov[:, HEAD:HEAD + 1]
            o_ref[:, cols] = (ov[:, :HEAD] * (1.0 / l)).astype(o_ref.dtype)
            st_ref[:, g:g + 1] = m + jnp.log2(l) * (1.0 / _ATTN_C2)

    qs = BS((tq, W), lambda h, i: (i, h))
    ks = BS((T, HEAD), lambda h, i: (0, h))
    return _call(body, "attn_fwd", grid=(KVH, T // tq), in_specs=[qs, ks, ks],
                 out_specs=[qs, BS((None, tq, QPK), lambda h, i: (h, i, 0))],
                 out_shape=[S((T, TOK), _MXU), S((KVH, T, QPK), _F32)],
                 scratch_shapes=[pltpu.VMEM((T, 2 * HEAD), _MXU)],
                 compiler_params=_cp("parallel", "arbitrary"))(q, k, v)


def _attn_bwd(q, k, v, dcat, o, stat, QPK):
    T, TOK = q.shape
    KV = k.shape[1]
    KVH = KV // HEAD
    tq = _tile(T, 256, 8)
    nq = T // tq
    W = QPK * HEAD
    scale = HEAD ** -0.5

    def body(q_ref, k_ref, v_ref, do_ref, o_ref, st_ref, dq_ref, dk_ref, dv_ref, dk_acc, dv_acc, ds_all, p_all, q_all, do_all):
        i = pl.program_id(1)

        @pl.when(i == 0)
        def _():
            dk_acc[...] = jnp.zeros_like(dk_acc)
            dv_acc[...] = jnp.zeros_like(dv_acc)

        kk, vv = k_ref[...], v_ref[...]
        for g in range(QPK):
            cols = slice(g * HEAD, (g + 1) * HEAD)
            rows = slice(g * tq, (g + 1) * tq)
            qg = q_ref[:, cols]
            p = jnp.exp2((_dot(qg, kk, _NT) - st_ref[:, g:g + 1]) * _ATTN_C2)
            do32 = do_ref[:, cols]
            do = do32.astype(_MXU)
            delta = jnp.sum(do32 * o_ref[:, cols].astype(_F32), axis=-1, keepdims=True)
            ds = (p * (_dot(do, vv, _NT) - delta)).astype(_MXU)
            dq_ref[:, cols] = _dot(ds, kk) * scale
            ds_all[rows, :] = ds
            p_all[rows, :] = p.astype(_MXU)
            q_all[rows, :] = qg
            do_all[rows, :] = do
        dk_acc[...] += _dot(ds_all[...], q_all[...], _TN)
        dv_acc[...] += _dot(p_all[...], do_all[...], _TN)

        @pl.when(i == nq - 1)
        def _():
            dk_ref[...] = dk_acc[...] * scale
            dv_ref[...] = dv_acc[...].astype(dv_ref.dtype)

    qs = BS((tq, W), lambda h, i: (i, h))
    ks = BS((T, HEAD), lambda h, i: (0, h))
    return _call(
        body, "attn_bwd", grid=(KVH, nq), in_specs=[qs, ks, ks, qs, qs, BS((None, tq, QPK), lambda h, i: (h, i, 0))],
        out_specs=[qs, ks, ks], out_shape=[S((T, TOK), _F32), S((T, KV), _F32), S((T, KV), _MXU)],
        scratch_shapes=[pltpu.VMEM((T, HEAD), _F32), pltpu.VMEM((T, HEAD), _F32), pltpu.VMEM((QPK * tq, T), _MXU),
                        pltpu.VMEM((QPK * tq, T), _MXU), pltpu.VMEM((QPK * tq, HEAD), _MXU), pltpu.VMEM((QPK * tq, HEAD), _MXU)],
        compiler_params=_cp("parallel", "arbitrary"),
    )(q, k, v, dcat, o, stat)


def _mem_fwd(z, qblk, kv, gmq, gmk, MEMW):
    T = z.shape[0]
    NM = kv.shape[0]
    tq = _tile(T, 512, 8)
    scale = HEAD ** -0.5

    def body(q_ref, kv_ref, gq_ref, gk_ref, o_ref):
        for h in range(MEMW // HEAD):
            cols = slice(h * HEAD, (h + 1) * HEAD)
            kx = kv_ref[:, cols]
            kn = (kx * _rstd(kx) * gk_ref[...]).astype(_MXU)
            vv = kv_ref[:, MEMW + h * HEAD:MEMW + (h + 1) * HEAD].astype(_MXU)
            qx = q_ref[:, cols]
            qn = (qx * _rstd(qx) * gq_ref[...]).astype(_MXU)
            p = _softmax(_dot(qn, kn, _NT) * scale)
            o_ref[:, cols] = _dot(p.astype(_MXU), vv).astype(o_ref.dtype)

    vec = BS((1, HEAD), lambda i: (0, 0))
    return _call(
        body, "mem_fwd", grid=(T // tq,),
        in_specs=[BS((tq, MEMW), lambda i: (i, qblk)), BS((NM, 2 * MEMW), lambda i: (0, 0)), vec, vec],
        out_specs=BS((tq, MEMW), lambda i: (i, 0)), out_shape=S((T, MEMW), _MXU), compiler_params=_cp("parallel"),
    )(z, kv, gmq.reshape(1, HEAD), gmk.reshape(1, HEAD))


def _mem_bwd(z, qblk, kv, gmq, gmk, dcat, dblk, MEMW):
    T = z.shape[0]
    NM = kv.shape[0]
    tq = _tile(T, 512, 8)
    scale = HEAD ** -0.5

    def body(q_ref, kv_ref, gq_ref, gk_ref, do_ref, dz_ref, dkn_ref, dv_ref, dgq_ref):
        @pl.when(pl.program_id(0) == 0)
        def _():
            dkn_ref[...] = jnp.zeros_like(dkn_ref)
            dv_ref[...] = jnp.zeros_like(dv_ref)
            dgq_ref[...] = jnp.zeros_like(dgq_ref)

        for h in range(MEMW // HEAD):
            cols = slice(h * HEAD, (h + 1) * HEAD)
            kx = kv_ref[:, cols]
            kn = (kx * _rstd(kx) * gk_ref[...]).astype(_MXU)
            vv = kv_ref[:, MEMW + h * HEAD:MEMW + (h + 1) * HEAD].astype(_MXU)
            qx = q_ref[:, cols]
            rq = _rstd(qx)
            qhat = qx * rq
            qn = (qhat * gq_ref[...]).astype(_MXU)
            p = _softmax(_dot(qn, kn, _NT) * scale)
            do = do_ref[:, cols].astype(_MXU)
            dp = _dot(do, vv, _NT)
            ds = (p * (dp - jnp.sum(p * dp, axis=-1, keepdims=True)) * scale).astype(_MXU)
            dqn = _dot(ds, kn)
            dkn_ref[:, cols] += _dot(ds, qn, _TN)
            dv_ref[:, cols] += _dot(p.astype(_MXU), do, _TN)
            dgq_ref[...] += jnp.sum(dqn * qhat, axis=0, keepdims=True)
            dz_ref[:, cols] = _norm_bwd(dqn, qhat, rq, gq_ref[...]).astype(dz_ref.dtype)

    vec = BS((1, HEAD), lambda i: (0, 0))
    kvs = BS((NM, MEMW), lambda i: (0, 0))
    return _call(
        body, "mem_bwd", grid=(T // tq,),
        in_specs=[BS((tq, MEMW), lambda i: (i, qblk)), BS((NM, 2 * MEMW), lambda i: (0, 0)), vec, vec,
                  BS((tq, MEMW), lambda i: (i, dblk))],
        out_specs=[BS((tq, MEMW), lambda i: (i, 0)), kvs, kvs, vec],
        out_shape=[S((T, MEMW), _MXU), S((NM, MEMW), _F32), S((NM, MEMW), _F32), S((1, HEAD), _F32)],
        compiler_params=_cp("arbitrary"),
    )(z, kv, gmq.reshape(1, HEAD), gmk.reshape(1, HEAD), dcat)


def _memkv_bwd(kv, dkn, dv, gmk, MEMW):
    NM = kv.shape[0]

    def body(kv_ref, dkn_ref, dv_ref, gk_ref, dkv_ref, dgk_ref):
        dgk = jnp.zeros((1, HEAD), _F32)
        for h in range(MEMW // HEAD):
            cols = slice(h * HEAD, (h + 1) * HEAD)
            kx = kv_ref[:, cols]
            r = _rstd(kx)
            khat = kx * r
            dkn = dkn_ref[:, cols]
            dgk = dgk + jnp.sum(dkn * khat, axis=0, keepdims=True)
            dkv_ref[:, cols] = _norm_bwd(dkn, khat, r, gk_ref[...]).astype(dkv_ref.dtype)
        dgk_ref[...] = dgk
        dkv_ref[:, MEMW:] = dv_ref[...].astype(dkv_ref.dtype)

    return _call(body, "memkv_bwd", out_shape=[S((NM, 2 * MEMW), _MXU), S((1, HEAD), _F32)],
                 compiler_params=_cp())(kv, dkn, dv, gmk.reshape(1, HEAD))


def _cast_into_full(w, l, sh, idx, dep=None):
    tr, tc = _tile(sh.Rs, 512, 16), _tile(sh.Cs, 2048)
    nr, nc = sh.Rs // tr, sh.Cs // tc
    deps = [] if dep is None else [dep]

    def body(i_ref, c_ref, w_ref, *rest):
        rest[-1][...] = w_ref[...].astype(rest[-1].dtype)

    if sh.by_cols:
        o_map = lambda a, b, si, sc: (a, si[0] * nc + b)
    else:
        o_map = lambda a, b, si, sc: (si[0] * nr + a, b)
    return _call(
        body, "cast_into_full",
        grid_spec=pltpu.PrefetchScalarGridSpec(
            num_scalar_prefetch=2, grid=(nr, nc),
            in_specs=[BS((None, tr, tc), lambda a, b, si, sc: (l, a, b))] + [_ANY] * len(deps), out_specs=BS((tr, tc), o_map)),
        out_shape=S((sh.R, sh.C), _WIRE), compiler_params=_cp("parallel", "parallel"),
    )(*idx, w, *deps)


def _adamw(w, g, m, v, name, l0=0, l1=None, prev=None):
    L, R, C = w.shape
    l1 = L if l1 is None else l1
    tc = _tile(C, 2048)
    tr = _tile(R, max(8, (512 * 1024) // tc), 8)
    c_m = 1.0 / (1.0 - ADAM_B1 ** ADAM_STEP)
    c_v = 1.0 / (1.0 - ADAM_B2 ** ADAM_STEP)

    def body(w_ref, g_ref, m_ref, v_ref, *rest):
        go_ref, d_ref, mo_ref, vo_ref = rest[-4:]
        gv = g_ref[...]
        mn = ADAM_B1 * m_ref[...] + (1.0 - ADAM_B1) * gv
        vn = ADAM_B2 * v_ref[...] + (1.0 - ADAM_B2) * (gv * gv)
        go_ref[...] = gv
        mo_ref[...] = mn
        vo_ref[...] = vn
        d_ref[...] = -ADAM_LR * ((mn * c_m) / (jnp.sqrt(vn * c_v) + ADAM_EPS) + ADAM_WD * w_ref[...])

    blk = BS((None, tr, tc), lambda a, i, j: (l0 + a, i, j))
    prevs = [] if prev is None else list(prev)
    return _call(body, name, grid=(l1 - l0, R // tr, C // tc), in_specs=[blk] * 4 + [_ANY] * len(prevs), out_specs=[blk] * 4,
                 out_shape=[S((L, R, C), _F32)] * 4, input_output_aliases={4 + k: k for k in range(len(prevs))},
                 compiler_params=_cp("parallel", "parallel", "parallel"))(w, g, m, v, *prevs)


def _where_am_i():
    x, y, c = lax.axis_index("x"), lax.axis_index("y"), lax.axis_index("c")
    chips = [(1 - x, y), (x, 1 - y), (1 - x, 1 - y)]
    return x, y, c, 2 * x + y, chips, [2 * cx + cy for cx, cy in chips]


class _Shard:
    def __init__(self, R, C, by_cols):
        self.R, self.C, self.by_cols = R, C, by_cols
        self.Rs, self.Cs = (R, C // 4) if by_cols else (R // 4, C)
        self.Rh = self.Rs // 2
        self.Q = R // 8

    def full_piece(self, ref, j, cc):
        if self.by_cols:
            return ref.at[pl.ds(cc * self.Rh, self.Rh), pl.ds(_mo(j * self.Cs, 128), self.Cs)]
        return ref.at[pl.ds(_mo(j * self.Rs + cc * self.Rh, 16), self.Rh), :]

    def full_shard(self, ref, j):
        if self.by_cols:
            return ref.at[:, pl.ds(_mo(j * self.Cs, 128), self.Cs)]
        return ref.at[pl.ds(_mo(j * self.Rs, 16), self.Rs), :]

    def shard_half(self, ref, cc):
        return ref.at[pl.ds(_mo(cc * self.Rh, 16), self.Rh), :]

    def half_piece(self, ref, j):
        if self.by_cols:
            return ref.at[:, pl.ds(_mo(j * self.Cs, 128), self.Cs)]
        return ref.at[pl.ds(_mo(j * self.Rh, 16), self.Rh), :]


def _remote(src, dst, ssem, rsem, dev):
    return pltpu.make_async_remote_copy(src_ref=src, dst_ref=dst, send_sem=ssem, recv_sem=rsem, device_id=dev, device_id_type=_MESH)


_HBM = pl.BlockSpec(memory_space=pltpu.HBM)
_SEM = pl.BlockSpec(memory_space=pltpu.SEMAPHORE)
_EFFECT = pltpu.SideEffectType.DATAFLOW_SIDE_EFFECTING


def _in_hbm(a):
    return pltpu.with_memory_space_constraint(a, pltpu.HBM)


def _gather_start(fulls, shs, name):
    n = len(fulls)

    def body(*refs):
        bufs = refs[:n]
        send_sems, recv_sems = refs[n], refs[n + 1]
        token = refs[-1]
        x, y, c, me, chips, chip_ids = _where_am_i()
        for t in range(n):
            mine = shs[t].full_piece(bufs[t], me, c)
            for r in range(3):
                _remote(mine, mine, send_sems.at[3 * t + r], recv_sems.at[3 * t + r], (*chips[r], c)).start()
        token[...] = jnp.zeros_like(token)

    out = pl.pallas_call(
        body, name=name, in_specs=[_HBM] * n,
        out_shape=(pltpu.SemaphoreType.DMA((3 * n,)), pltpu.SemaphoreType.DMA((3 * n,)), *[pltpu.HBM(f.shape, f.dtype) for f in fulls],
                   S((8, 128), _F32)),
        out_specs=(_SEM, _SEM, *[_HBM] * n, pl.BlockSpec(memory_space=pltpu.VMEM)),
        input_output_aliases={t: 2 + t for t in range(n)},
        compiler_params=pltpu.CompilerParams(has_side_effects=_EFFECT), **_KW,
    )(*[_in_hbm(f) for f in fulls])
    return out[0], out[1], list(out[2:2 + n]), out[-1]


def _gather_wait(fulls, send_sems, recv_sems, after, shs, name):
    n = len(fulls)

    def body(*refs):
        bufs = refs[:n]
        ssem, rsem = refs[n], refs[n + 1]
        x, y, c, me, chips, chip_ids = _where_am_i()
        for t in range(n):
            mine = shs[t].full_piece(bufs[t], me, c)
            for r in range(3):
                _remote(mine, mine, ssem.at[3 * t + r], rsem.at[3 * t + r], (*chips[r], c)).wait_send()
        for t in range(n):
            for r in range(3):
                piece = shs[t].full_piece(bufs[t], chip_ids[r], c)
                _remote(piece, piece, ssem.at[3 * t + r], rsem.at[3 * t + r], (*chips[r], c)).wait_recv()

    out = pl.pallas_call(
        body, name=name, in_specs=[*[_HBM] * n, _SEM, _SEM, _ANY], out_specs=[_HBM] * n,
        out_shape=[pltpu.HBM(f.shape, f.dtype) for f in fulls], input_output_aliases={t: t for t in range(n)},
        compiler_params=pltpu.CompilerParams(has_side_effects=_EFFECT), **_KW,
    )(*fulls, send_sems, recv_sems, after)
    return list(out)


def _gather_pass_on(fulls, shs, name):
    n = len(fulls)

    def body(*refs):
        bufs = refs[n:2 * n]
        send_sems, recv_sems = refs[2 * n:]
        x, y, c, me, chips, chip_ids = _where_am_i()
        sib = (x, y, 1 - c)
        cps = []
        for t in range(n):
            for r in range(3):
                piece = shs[t].full_piece(bufs[t], chip_ids[r], c)
                cps.append(_remote(piece, piece, send_sems.at[t, r], recv_sems.at[t, r], sib))
        for cp in cps:
            cp.start()
        for t in range(n):
            for r in range(3):
                piece = shs[t].full_piece(bufs[t], chip_ids[r], 1 - c)
                _remote(piece, piece, send_sems.at[t, r], recv_sems.at[t, r], sib).wait_recv()
        for cp in cps:
            cp.wait_send()

    return _call(
        body, name, in_specs=[_ANY] * n, out_specs=[_ANY] * n, out_shape=[S(f.shape, f.dtype) for f in fulls],
        input_output_aliases={t: t for t in range(n)},
        scratch_shapes=[pltpu.SemaphoreType.DMA((n, 3)), pltpu.SemaphoreType.DMA((n, 3))],
        compiler_params=pltpu.CompilerParams(has_side_effects=True),
    )(*fulls)


def _pass_on_copies(bufs, shs, send_sems, recv_sems):
    x, y, c, me, chips, chip_ids = _where_am_i()
    sib = (x, y, 1 - c)
    out, back = [], []
    for t in range(len(bufs)):
        for r in range(3):
            piece = shs[t].full_piece(bufs[t], chip_ids[r], c)
            out.append(_remote(piece, piece, send_sems.at[3 * t + r], recv_sems.at[3 * t + r], sib))
            other = shs[t].full_piece(bufs[t], chip_ids[r], 1 - c)
            back.append(_remote(other, other, send_sems.at[3 * t + r], recv_sems.at[3 * t + r], sib))
    return out, back


def _pass_on_start(fulls, shs, name):
    n = len(fulls)

    def body(*refs):
        for cp in _pass_on_copies(refs[:n], shs, refs[n], refs[n + 1])[0]:
            cp.start()
        refs[-1][...] = jnp.zeros_like(refs[-1])

    out = pl.pallas_call(
        body, name=name, in_specs=[_HBM] * n,
        out_shape=(pltpu.SemaphoreType.DMA((3 * n,)), pltpu.SemaphoreType.DMA((3 * n,)), *[pltpu.HBM(f.shape, f.dtype) for f in fulls],
                   S((8, 128), _F32)),
        out_specs=(_SEM, _SEM, *[_HBM] * n, pl.BlockSpec(memory_space=pltpu.VMEM)),
        input_output_aliases={t: 2 + t for t in range(n)},
        compiler_params=pltpu.CompilerParams(has_side_effects=_EFFECT), **_KW,
    )(*[_in_hbm(f) for f in fulls])
    return out[0], out[1], list(out[2:2 + n]), out[-1]


def _pass_on_wait(fulls, send_sems, recv_sems, after, shs, name):
    n = len(fulls)

    def body(*refs):
        out, back = _pass_on_copies(refs[:n], shs, refs[n], refs[n + 1])
        for cp in out:
            cp.wait_send()
        for cp in back:
            cp.wait_recv()

    out = pl.pallas_call(
        body, name=name, in_specs=[*[_HBM] * n, _SEM, _SEM, _ANY], out_specs=[_HBM] * n,
        out_shape=[pltpu.HBM(f.shape, f.dtype) for f in fulls], input_output_aliases={t: t for t in range(n)},
        compiler_params=pltpu.CompilerParams(has_side_effects=_EFFECT), **_KW,
    )(*fulls, send_sems, recv_sems, after)
    return list(out)


def _rs_pair_copies(ins, outs, shs, send_sems, recv_sems):
    x, y, c, *_ = _where_am_i()
    sib = (x, y, 1 - c)
    cps = []
    for t in range(len(ins)):
        sh = shs[t]
        if sh.by_cols:
            cps.append(_remote(ins[t].at[pl.ds((1 - c) * sh.Rh, sh.Rh), :], outs[t], send_sems.at[4 * t], recv_sems.at[4 * t], sib))
        else:
            for j in range(4):
                cps.append(_remote(sh.full_piece(ins[t], j, 1 - c), sh.half_piece(outs[t], j),
                                   send_sems.at[4 * t + j], recv_sems.at[4 * t + j], sib))
    return cps


def _rs_pair_start(dws, shs, name):
    n = len(dws)
    lands = [lax.empty((sh.R // 2, sh.C), _WIRE) for sh in shs]

    def body(*refs):
        for cp in _rs_pair_copies(refs[:n], refs[n:2 * n], shs, refs[2 * n], refs[2 * n + 1]):
            cp.start()
        refs[-1][...] = jnp.zeros_like(refs[-1])

    out = pl.pallas_call(
        body, name=name, in_specs=[_HBM] * (2 * n),
        out_shape=(pltpu.SemaphoreType.DMA((4 * n,)), pltpu.SemaphoreType.DMA((4 * n,)),
                   *[pltpu.HBM(a.shape, a.dtype) for a in (*dws, *lands)], S((8, 128), _F32)),
        out_specs=(_SEM, _SEM, *[_HBM] * (2 * n), pl.BlockSpec(memory_space=pltpu.VMEM)),
        input_output_aliases={t: 2 + t for t in range(2 * n)},
        compiler_params=pltpu.CompilerParams(has_side_effects=_EFFECT), **_KW,
    )(*[_in_hbm(a) for a in (*dws, *lands)])
    return out[0], out[1], list(out[2:2 + n]), list(out[2 + n:2 + 2 * n]), out[-1]


def _rs_pair_wait(dws, lands, send_sems, recv_sems, after, shs, name):
    n = len(dws)

    def body(*refs):
        cps = _rs_pair_copies(refs[:n], refs[n:2 * n], shs, refs[2 * n], refs[2 * n + 1])
        for cp in cps:
            cp.wait_send()
        for cp in cps:
            cp.wait_recv()

    out = pl.pallas_call(
        body, name=name, in_specs=[*[_HBM] * (2 * n), _SEM, _SEM, _ANY], out_specs=[_HBM] * (2 * n),
        out_shape=[pltpu.HBM(a.shape, a.dtype) for a in (*dws, *lands)], input_output_aliases={t: t for t in range(2 * n)},
        compiler_params=pltpu.CompilerParams(has_side_effects=_EFFECT), **_KW,
    )(*dws, *lands, send_sems, recv_sems, after)
    return list(out[n:])


def _rs_pair_add(dw32, recv, sh, idx):
    tr, tc = _tile(sh.Q, 512, 16), _tile(sh.C, 2048)
    nb = sh.Q // tr

    def body(i_ref, c_ref, a_ref, b_ref, ow_ref):
        ow_ref[...] = (a_ref[...] + b_ref[...].astype(_F32)).astype(ow_ref.dtype)

    if sh.by_cols:
        a_map = lambda j, i, b, si, sc: (sc[0] * 4 * nb + j * nb + i, b)
    else:
        a_map = lambda j, i, b, si, sc: (j * 2 * nb + sc[0] * nb + i, b)
    h_spec = BS((tr, tc), lambda j, i, b, si, sc: (j * nb + i, b))
    return _call(
        body, "rs_pair_add",
        grid_spec=pltpu.PrefetchScalarGridSpec(num_scalar_prefetch=2, grid=(4, nb, sh.C // tc),
                                               in_specs=[BS((tr, tc), a_map), h_spec], out_specs=h_spec),
        out_shape=S((sh.R // 2, sh.C), _WIRE), compiler_params=_cp("parallel", "parallel", "parallel"),
    )(*idx, dw32, recv)


def _rs_chip_start(pws, shs, name):
    n = len(pws)
    lands = [lax.empty((3, sh.Rh, sh.Cs), _WIRE) for sh in shs]

    def body(*refs):
        ins, lnd = refs[:n], refs[n:2 * n]
        send_sems, recv_sems = refs[2 * n], refs[2 * n + 1]
        token = refs[-1]
        x, y, c, me, chips, chip_ids = _where_am_i()
        for t in range(n):
            for r in range(3):
                _remote(shs[t].half_piece(ins[t], chip_ids[r]), lnd[t].at[r], send_sems.at[3 * t + r], recv_sems.at[3 * t + r],
                        (*chips[r], c)).start()
        token[...] = jnp.zeros_like(token)

    out = pl.pallas_call(
        body, name=name, in_specs=[_HBM] * (2 * n),
        out_shape=(pltpu.SemaphoreType.DMA((3 * n,)), pltpu.SemaphoreType.DMA((3 * n,)),
                   *[pltpu.HBM(a.shape, a.dtype) for a in (*pws, *lands)], S((8, 128), _F32)),
        out_specs=(_SEM, _SEM, *[_HBM] * (2 * n), pl.BlockSpec(memory_space=pltpu.VMEM)),
        input_output_aliases={t: 2 + t for t in range(2 * n)},
        compiler_params=pltpu.CompilerParams(has_side_effects=_EFFECT), **_KW,
    )(*[_in_hbm(a) for a in (*pws, *lands)])
    return out[0], out[1], list(out[2:2 + n]), list(out[2 + n:2 + 2 * n]), out[-1]


def _rs_chip_wait(pws, lands, send_sems, recv_sems, after, shs, name):
    n = len(pws)
    after = list(after) if isinstance(after, (list, tuple)) else [after]

    def body(*refs):
        ins, lnd = refs[:n], refs[n:2 * n]
        ssem, rsem = refs[2 * n], refs[2 * n + 1]
        x, y, c, me, chips, chip_ids = _where_am_i()
        for t in range(n):
            for r in range(3):
                cp = _remote(shs[t].half_piece(ins[t], chip_ids[r]), lnd[t].at[r], ssem.at[3 * t + r], rsem.at[3 * t + r], (*chips[r], c))
                cp.wait_send()
        for t in range(n):
            for r in range(3):
                cp = _remote(shs[t].half_piece(ins[t], chip_ids[r]), lnd[t].at[r], ssem.at[3 * t + r], rsem.at[3 * t + r], (*chips[r], c))
                cp.wait_recv()

    out = pl.pallas_call(
        body, name=name, in_specs=[*[_HBM] * (2 * n), _SEM, _SEM, *[_ANY] * len(after)], out_specs=[_HBM] * (2 * n),
        out_shape=[pltpu.HBM(a.shape, a.dtype) for a in (*pws, *lands)], input_output_aliases={t: t for t in range(2 * n)},
        compiler_params=pltpu.CompilerParams(has_side_effects=_EFFECT), **_KW,
    )(*pws, *lands, send_sems, recv_sems, *after)
    return list(out[n:])


def _rs_chip_add(dw32, pair, recv, sh, idx, g_prev, l, L):
    tr, tc = _tile(sh.Rh, 512, 16), _tile(sh.Cs, 2048)
    nr, nc = sh.Rh // tr, sh.Cs // tc

    def body(i_ref, c_ref, d_ref, a_ref, b_ref, *rest):
        rest[-1][...] = ((d_ref[...] + a_ref[...].astype(_F32)) + b_ref[0].astype(_F32) + b_ref[1].astype(_F32)
                         + b_ref[2].astype(_F32))

    if sh.by_cols:
        d_map = lambda a, b, si, sc: (sc[0] * nr + a, si[0] * nc + b)
        a_map = lambda a, b, si, sc: (a, si[0] * nc + b)
    else:
        d_map = lambda a, b, si, sc: ((2 * si[0] + sc[0]) * nr + a, b)
        a_map = lambda a, b, si, sc: (si[0] * nr + a, b)
    in_specs = [BS((tr, tc), d_map), BS((tr, tc), a_map), BS((3, tr, tc), lambda a, b, si, sc: (0, a, b))]
    args = [*idx, dw32, pair, recv]
    if g_prev is not None:
        in_specs.append(_ANY)
        args.append(g_prev)
    return _call(
        body, "rs_chip_add",
        grid_spec=pltpu.PrefetchScalarGridSpec(num_scalar_prefetch=2, grid=(nr, nc), in_specs=in_specs,
                                               out_specs=BS((None, tr, tc), lambda a, b, si, sc: (l, sc[0] * nr + a, b))),
        out_shape=S((L, sh.Rs, sh.Cs), _F32), input_output_aliases={} if g_prev is None else {5: 0},
        compiler_params=_cp("parallel", "parallel"),
    )(*args)


def _rs_pair_share(gs, ls, shs, name):
    n = len(gs)

    def body(*refs):
        bufs = refs[n:2 * n]
        send_sems, recv_sems = refs[2 * n:]
        x, y, c, *_ = _where_am_i()
        sib = (x, y, 1 - c)
        cps = []
        for t in range(n):
            mine = shs[t].shard_half(bufs[t].at[ls[t]], c)
            cps.append(_remote(mine, mine, send_sems.at[t], recv_sems.at[t], sib))
        for cp in cps:
            cp.start()
        for t in range(n):
            other = shs[t].shard_half(bufs[t].at[ls[t]], 1 - c)
            _remote(other, other, send_sems.at[t], recv_sems.at[t], sib).wait_recv()
        for cp in cps:
            cp.wait_send()

    return _call(
        body, name, in_specs=[_ANY] * n, out_specs=[_ANY] * n, out_shape=[S(g.shape, g.dtype) for g in gs],
        input_output_aliases={t: t for t in range(n)},
        scratch_shapes=[pltpu.SemaphoreType.DMA((n,)), pltpu.SemaphoreType.DMA((n,))],
        compiler_params=pltpu.CompilerParams(has_side_effects=True),
    )(*gs)


def _all_reduce_small(xs, dep=None):
    M = xs.shape[0]
    deps = [] if dep is None else [dep]

    def body(x_ref, *rest):
        tot_ref, out_ref, send_sems, recv_sems, local_sem = rest[len(deps):]
        x, y, c, me, chips, chip_ids = _where_am_i()
        sib = (x, y, 1 - c)

        def rows(dev):
            return out_ref.at[pl.ds(_mo((4 * dev[0] + 2 * dev[1] + dev[2]) * M, 8), M), :]

        def copy(k, block, to, src=None):
            return _remote(rows(block) if src is None else src, rows(block), send_sems.at[k], recv_sems.at[k], to)

        mine = pltpu.make_async_copy(x_ref, rows((x, y, c)), local_sem)
        mine.start()
        first = [copy(0, (x, y, c), sib, src=x_ref)]
        first += [copy(1 + j, (x, y, c), (*chip, c), src=x_ref) for j, chip in enumerate(chips)]
        for cp in first:
            cp.start()
        passed = [copy(4 + j, (*chip, c), sib) for j, chip in enumerate(chips)]
        for j, chip in enumerate(chips):
            copy(1 + j, (*chip, c), (x, y, c)).wait_recv()
            passed[j].start()
        copy(0, sib, (x, y, c)).wait_recv()
        for j, chip in enumerate(chips):
            copy(4 + j, (*chip, 1 - c), (x, y, c)).wait_recv()
        for cp in first + passed:
            cp.wait_send()
        mine.wait()
        tot = out_ref[pl.ds(0, M), :]
        for d in range(1, 8):
            tot = tot + out_ref[pl.ds(d * M, M), :]
        tot_ref[...] = tot

    vm = pl.BlockSpec(memory_space=pltpu.VMEM)
    return _call(
        body, "all_reduce_small", in_specs=[vm] + [_ANY] * len(deps), out_specs=[vm, vm],
        out_shape=[S((M, 128), _F32), S((8 * M, 128), _F32)],
        scratch_shapes=[pltpu.SemaphoreType.DMA((7,)), pltpu.SemaphoreType.DMA((7,)), pltpu.SemaphoreType.DMA],
        compiler_params=_cp(has_side_effects=True),
    )(xs, *deps)[0]


def _reduce_scatter_begin(dws, shs, l):
    ssem, rsem, dww, lands, token = _rs_pair_start([d[1] for d in dws], shs, f"rs_pair_start_{l}")
    return ([d[0] for d in dws], dww, lands, ssem, rsem), token


def _reduce_scatter_middle(state, after, shs, idx, l):
    dw32s, dww, lands, ssem, rsem = state
    recv_a = _rs_pair_wait(dww, lands, ssem, rsem, after, shs, f"rs_pair_wait_{l}")
    pws = [_rs_pair_add(d32, ra, sh, idx) for d32, ra, sh in zip(dw32s, recv_a, shs)]
    ssem, rsem, pws, lands, token = _rs_chip_start(pws, shs, f"rs_chip_start_{l}")
    return (dw32s, recv_a, pws, lands, ssem, rsem), token


def _reduce_scatter_end(state, after, tensors, shs, gstack, idx, l):
    dw32s, recv_a, pws, lands, ssem, rsem = state
    recv_b = _rs_chip_wait(pws, lands, ssem, rsem, after, shs, f"rs_chip_wait_{l}")
    gs = [_rs_chip_add(d32, ra, rb, sh, idx, gstack[name], i, L)
          for d32, ra, rb, sh, (name, i, L) in zip(dw32s, recv_a, recv_b, shs, tensors)]
    gs = _rs_pair_share(gs, [i for _, i, _ in tensors], shs, "rs_pair_share")
    for (name, _, _), g in zip(tensors, gs):
        gstack[name] = g


def _pack(parts):
    out = []
    for p in parts:
        p2 = p.reshape(-1, 128)
        pad = (-p2.shape[0]) % 8
        out.append(jnp.pad(p2, ((0, pad), (0, 0))) if pad else p2)
    return jnp.concatenate(out, axis=0)


def _unpack(packed, like):
    out, at = [], 0
    for p in like:
        n = p.size // 128
        out.append(packed[at:at + n].reshape(p.shape))
        at += n + ((-n) % 8)
    return out


def kernel(x, mem, g_mix, g_ffn, w_in_a, g_v_a, w_spatial, b_spatial, w_in_b, g_q_b, g_k_b, g_mem, w_mem_kv, g_mq, g_mk, w_out, w_gate_up, w_down, loss_target, m_g_mix, m_g_ffn, m_w_in_a, m_g_v_a, m_w_spatial, m_b_spatial, m_w_in_b, m_g_q_b, m_g_k_b, m_g_mem, m_w_mem_kv, m_g_mq, m_g_mk, m_w_out, m_w_gate_up, m_w_down, v_g_mix, v_g_ffn, v_w_in_a, v_g_v_a, v_w_spatial, v_b_spatial, v_w_in_b, v_g_q_b, v_g_k_b, v_g_mem, v_w_mem_kv, v_g_mq, v_g_mk, v_w_out, v_w_gate_up, v_w_down):
    xs = x[0]
    mem2 = mem[0]
    target = loss_target[0]
    T, D = xs.shape
    depth = g_mix.shape[0]
    MEMW = w_mem_kv.shape[2] // 2
    TOK = D - MEMW
    KV = (w_in_b.shape[2] * 4 - TOK - MEMW) // 2
    QPK = TOK // KV
    F = w_gate_up.shape[2] * 4 // 2

    idx = ((2 * lax.axis_index("x") + lax.axis_index("y")).astype(jnp.int32).reshape(1), lax.axis_index("c").astype(jnp.int32).reshape(1))

    big = {
        "w_in_a": (w_in_a, _Shard(D, w_in_a.shape[2] * 4, True)),
        "w_in_b": (w_in_b, _Shard(D, w_in_b.shape[2] * 4, True)),
        "w_mem_kv": (w_mem_kv, _Shard(D, 2 * MEMW, False)),
        "w_out": (w_out, _Shard(D, D, False)),
        "w_gate_up": (w_gate_up, _Shard(D, 2 * F, True)),
        "w_down": (w_down, _Shard(F, D, False)),
    }

    def layer_tensors(l):
        n_in = "w_in_a" if l % 2 == 0 else "w_in_b"
        return [(n_in, l // 2, big[n_in][0].shape[0])] + [(n, l, depth) for n in ("w_mem_kv", "w_out", "w_gate_up", "w_down")]

    def layer_shards(l):
        return [big[n][1] for n, _, _ in layer_tensors(l)]

    full = {n: [None] * w.shape[0] for n, (w, _) in big.items()}
    flying = {}

    def start_layer(l, dep):
        tens = layer_tensors(l)
        token = dep
        for gi, group in enumerate([tens[:2], tens[2:3], tens[3:4], tens[4:]]):
            shs = [big[n][1] for n, _, _ in group]
            bufs = [_cast_into_full(big[n][0], i, big[n][1], idx, dep=token) for n, i, _ in group]
            ssem, rsem, bufs, token = _gather_start(bufs, shs, f"gather_start_{l}_{gi}")
            for n, i, _ in group:
                flying[(n, i)] = dict(group=group, shs=shs, state=(ssem, rsem, bufs), name=f"{l}_{gi}", passing=False)
        return token

    def land(members, after):
        for n, i in members:
            fl = flying[(n, i)]
            ssem, rsem, bufs = fl["state"]
            bufs = _gather_wait(bufs, ssem, rsem, after, fl["shs"], "gather_wait_" + fl["name"])
            ssem, rsem, bufs, after = _pass_on_start(bufs, fl["shs"], "pass_on_start_" + fl["name"])
            fl.update(state=(ssem, rsem, bufs), passing=True)
        return after

    def weight(n, i, after):
        if full[n][i] is None:
            fl = flying[(n, i)]
            ssem, rsem, bufs = fl["state"]
            if fl["passing"]:
                bufs = _pass_on_wait(bufs, ssem, rsem, after, fl["shs"], "pass_on_wait_" + fl["name"])
            else:
                bufs = _gather_wait(bufs, ssem, rsem, after, fl["shs"], "gather_wait_" + fl["name"])
                bufs = _gather_pass_on(bufs, fl["shs"], "gather_pass_on")
            for (m, j, _), b in zip(fl["group"], bufs):
                full[m][j] = b
        return full[n][i]

    after = None
    for l in range(depth):
        after = start_layer(l, after)
    tabs = _rope_tables(T)

    saved = []
    xc = xs
    for l in range(depth):
        is_a = l % 2 == 0
        li = l // 2
        w_in = weight("w_in_a" if is_a else "w_in_b", li, after)
        h = _rmsnorm_fwd(xc, g_mix[l], "rmsnorm_fwd")
        z = _mm_nn("mm_in", h, w_in, pm=2048, pn=512)
        st = dict(x=xc, h=h, z=z)
        if is_a:
            ws_m = w_spatial[li].astype(_MXU)
            st["ws_m"], st["wst_m"], st["b_t"] = ws_m, jnp.swapaxes(ws_m, 1, 2), b_spatial[li].T
            tok = _mixer_a_fwd(z, g_v_a[li], ws_m, st["b_t"], TOK)
            qblk = 2 * TOK // MEMW
        else:
            q, k, v = _qk_rope_fwd(z, g_q_b[li], g_k_b[li], tabs, TOK, KV)
            tok, stat = _attn_fwd(q, k, v, QPK)
            st["q"], st["k"], st["v"], st["stat"] = q, k, v, stat
            qblk = (TOK + 2 * KV) // MEMW
        mem_n = _rmsnorm_fwd(mem2, g_mem[l], "rmsnorm_mem")
        kv = _mm_nn("mm_memkv", mem_n, weight("w_mem_kv", l, z))
        mo = _mem_fwd(z, qblk, kv, g_mq[l], g_mk[l], MEMW)
        cat = jnp.concatenate([tok, mo], axis=1)
        token = land([("w_out", l), ("w_gate_up", l), ("w_down", l)], cat) if l > 0 else None
        x1 = _mm_nn("mm_out", cat, weight("w_out", l, cat), add=xc, dep=token)
        h2 = _rmsnorm_fwd(x1, g_ffn[l], "rmsnorm_fwd")
        act, gu = _ffn_gate_up(h2, weight("w_gate_up", l, h2))
        w_down_l = weight("w_down", l, act)
        token = land([layer_tensors(l + 1)[0][:2]], act) if l + 1 < depth else None
        xc = _mm_nn("mm_down", act, w_down_l, add=x1, pm=512, pn=512, pk=8192, dep=token)
        after = xc
        st.update(mem_n=mem_n, kv=kv, qblk=qblk, cat=cat, x1=x1, h2=h2, act=act, gu=gu)
        saved.append(st)

    dx, dxm, sq = _loss_head(xc, target)
    loss = lax.psum(sq[0, 0] * (0.5 / D), ("x", "y", "c"))

    gsm = {n: [None] * len(a) for n, a in dict(g_mix=g_mix, g_ffn=g_ffn, g_v_a=g_v_a, w_spatial=w_spatial, b_spatial=b_spatial,
                                                g_q_b=g_q_b, g_k_b=g_k_b, g_mem=g_mem, g_mq=g_mq, g_mk=g_mk).items()}
    gstack = {n: None for n in big}
    pairing, chipping, token = None, None, None

    def advance(after):
        nonlocal pairing, chipping
        state, tok = _reduce_scatter_middle(pairing[0], after, layer_shards(pairing[1]), idx, pairing[1])
        if chipping is not None:
            _reduce_scatter_end(chipping[0], tok, layer_tensors(chipping[1]), layer_shards(chipping[1]), gstack, idx, chipping[1])
        pairing, chipping = None, (state, pairing[1])
        return tok

    for l in reversed(range(depth)):
        st = saved[l]
        is_a = l % 2 == 0
        li = l // 2
        gbig = {}
        dgu = _ffn_dact(dxm, full["w_down"][l], st["gu"], dep=token)
        token = advance(dgu) if pairing is not None else None
        gbig["w_down"] = _mm_tn_dual("mm_dw_down", st["act"], dxm, pm=1408)
        dh2 = _ffn_dh(dgu, full["w_gate_up"][l], dep=token)
        gbig["w_gate_up"] = _ffn_dwgu(st["h2"], dgu)
        dx, dxm, dg = _rmsnorm_bwd(st["x1"], g_ffn[l], dh2, dx, "rmsnorm_bwd")
        gsm["g_ffn"][l] = dg[0]
        dcat = _mm_nt("mm_dcat", dxm, full["w_out"][l])
        gbig["w_out"] = _mm_tn_dual("mm_dw_out", st["cat"], dxm)
        dzq, dkn, dvm, dgq = _mem_bwd(st["z"], st["qblk"], st["kv"], g_mq[l], g_mk[l], dcat, TOK // MEMW, MEMW)
        dkv, dgk = _memkv_bwd(st["kv"], dkn, dvm, g_mk[l], MEMW)
        gsm["g_mq"][l], gsm["g_mk"][l] = dgq[0], dgk[0]
        gbig["w_mem_kv"] = _mm_tn_dual("mm_dw_memkv", st["mem_n"], dkv)
        dmem_n = _mm_nt("mm_dmemn", dkv, full["w_mem_kv"][l])
        gsm["g_mem"][l] = _rmsnorm_bwd(mem2, g_mem[l], dmem_n, None, "rmsnorm_bwd_mem")[2][0]
        if is_a:
            dz_tok, dws, dbs, dgv = _mixer_a_bwd(st["z"], dcat, g_v_a[li], st["ws_m"], st["wst_m"], st["b_t"], TOK)
            gsm["w_spatial"][li], gsm["b_spatial"][li], gsm["g_v_a"][li] = dws, dbs[:, :, 0], dgv[0]
            dz = jnp.concatenate([dz_tok, dzq], axis=1)
        else:
            dq, dk, dv = _attn_bwd(st["q"], st["k"], st["v"], dcat, st["cat"], st["stat"], QPK)
            dz_qk, dgq_b, dgk_b = _qk_rope_bwd(st["z"], dq, dk, g_q_b[li], g_k_b[li], tabs, TOK, KV)
            gsm["g_q_b"][li], gsm["g_k_b"][li] = dgq_b[0], dgk_b[0]
            dz = jnp.concatenate([dz_qk, dv, dzq], axis=1)
        n_in = "w_in_a" if is_a else "w_in_b"
        dh = _mm_nt("mm_dh", dz, full[n_in][li])
        gbig[n_in] = _mm_tn_dual("mm_dw_in", st["h"], dz, pm=2048, pn=512)
        dx, dxm, dg = _rmsnorm_bwd(st["x"], g_mix[l], dh, dx, "rmsnorm_bwd")
        gsm["g_mix"][l] = dg[0]
        state, token = _reduce_scatter_begin([gbig[n] for n, _, _ in layer_tensors(l)], layer_shards(l), l)
        pairing = (state, l)

    small = ["g_mix", "g_ffn", "g_v_a", "w_spatial", "b_spatial", "g_q_b", "g_k_b", "g_mem", "g_mq", "g_mk"]
    env = dict(g_mix=g_mix, g_ffn=g_ffn, g_v_a=g_v_a, w_spatial=w_spatial, b_spatial=b_spatial, g_q_b=g_q_b, g_k_b=g_k_b,
               g_mem=g_mem, g_mq=g_mq, g_mk=g_mk,
               m_g_mix=m_g_mix, m_g_ffn=m_g_ffn, m_g_v_a=m_g_v_a, m_w_spatial=m_w_spatial, m_b_spatial=m_b_spatial,
               m_g_q_b=m_g_q_b, m_g_k_b=m_g_k_b, m_g_mem=m_g_mem, m_g_mq=m_g_mq, m_g_mk=m_g_mk,
               v_g_mix=v_g_mix, v_g_ffn=v_g_ffn, v_g_v_a=v_g_v_a, v_w_spatial=v_w_spatial, v_b_spatial=v_b_spatial,
               v_g_q_b=v_g_q_b, v_g_k_b=v_g_k_b, v_g_mem=v_g_mem, v_g_mq=v_g_mq, v_g_mk=v_g_mk,
               m_w_in_a=m_w_in_a, m_w_in_b=m_w_in_b, m_w_mem_kv=m_w_mem_kv, m_w_out=m_w_out, m_w_gate_up=m_w_gate_up, m_w_down=m_w_down,
               v_w_in_a=v_w_in_a, v_w_in_b=v_w_in_b, v_w_mem_kv=v_w_mem_kv, v_w_out=v_w_out, v_w_gate_up=v_w_gate_up, v_w_down=v_w_down)
    like = [env[n] for n in small]
    g_small = _all_reduce_small(_pack([jnp.stack(gsm[n]) for n in small]), dep=token)

    res = {}
    outs = _adamw(_pack(like)[None], g_small[None], _pack([env["m_" + n] for n in small])[None],
                  _pack([env["v_" + n] for n in small])[None], "adamw_small")
    unpacked = [_unpack(o[0], like) for o in outs]
    for k, n in enumerate(small):
        res[n] = [u[k] for u in unpacked]
    advance(outs[1])
    pending = chipping
    last = {n: i for n, i, _ in layer_tensors(pending[1])}
    early = {}
    for n, (w, _) in big.items():
        L = w.shape[0]
        if n not in last:
            res[n] = _adamw(w, gstack[n], env["m_" + n], env["v_" + n], "adamw_" + n)
        elif L > 1:
            assert last[n] == 0
            early[n] = _adamw(w, gstack[n], env["m_" + n], env["v_" + n], "adamw_early_" + n, l0=1)
    done = [o[1] for o in early.values()] + [res[n][1] for n in big if n in res] + [res[small[0]][1]]
    _reduce_scatter_end(pending[0], done, layer_tensors(pending[1]), layer_shards(pending[1]), gstack, idx, pending[1])
    for n in last:
        res[n] = _adamw(big[n][0], gstack[n], env["m_" + n], env["v_" + n], "adamw_last_" + n, l0=0, l1=1, prev=early.get(n))

    order = ["g_mix", "g_ffn", "w_in_a", "g_v_a", "w_spatial", "b_spatial", "w_in_b", "g_q_b", "g_k_b", "g_mem", "w_mem_kv",
             "g_mq", "g_mk", "w_out", "w_gate_up", "w_down"]
    return (loss, dx.reshape(1, T, D), *[res[n][0] for n in order], *[res[n][1] for n in order],
            *[res[n][2] for n in order], *[res[n][3] for n in order])
```

```python
import jax
import jax.numpy as jnp
import numpy as np
from jax import lax
from jax.experimental import pallas as pl
from jax.experimental.pallas import tpu as pltpu

_F32 = jnp.float32
_MXU = jnp.bfloat16
_WIRE = jnp.bfloat16
_KW = {}

EPS = 1e-6
HEAD = 128
CHUNK = 128
GRID_W = 64
ROPE_THETA = 10000.0
ADAM_LR, ADAM_B1, ADAM_B2, ADAM_EPS, ADAM_WD, ADAM_STEP = 0.001, 0.9, 0.999, 1e-08, 0.01, 10
_SQRT_HALF = float(np.sqrt(0.5))
_INV_SQRT_2PI = float(1.0 / np.sqrt(2.0 * np.pi))
_VMEM_LIMIT = 56 * 1024 * 1024
_MESH = pl.DeviceIdType.MESH

_NN = (((1,), (0,)), ((), ()))
_NT = (((1,), (1,)), ((), ()))
_TN = (((0,), (0,)), ((), ()))

S = jax.ShapeDtypeStruct
BS = pl.BlockSpec
_ANY = pl.BlockSpec(memory_space=pl.ANY)


def _tile(n, pref, mult=128):
    if n <= pref:
        return n
    d = (pref // mult) * mult
    while d >= mult:
        if n % d == 0:
            return d
        d -= mult
    raise ValueError(f"no tile for {n} (pref {pref}, mult {mult})")


def _mo(v, m):
    return v if isinstance(v, int) else pl.multiple_of(v, m)


def _cp(*sem, **kw):
    return pltpu.CompilerParams(dimension_semantics=sem or None, vmem_limit_bytes=_VMEM_LIMIT, **kw)


def _call(body, name, **kw):
    return pl.pallas_call(body, name=name, **kw, **_KW)


def _dot(a, b, dn=_NN):
    return lax.dot_general(a, b, dn, preferred_element_type=_F32)


def _gelu(x):
    return 0.5 * x * (1.0 + lax.erf(x * _SQRT_HALF))


def _gelu_grad(x):
    return 0.5 * (1.0 + lax.erf(x * _SQRT_HALF)) + x * jnp.exp(-0.5 * x * x) * _INV_SQRT_2PI


def _rstd(x):
    return lax.rsqrt(jnp.mean(x * x, axis=-1, keepdims=True) + EPS)


def _norm_bwd(dout, xhat, r, g):
    dy = dout * g
    return r * (dy - xhat * jnp.mean(dy * xhat, axis=-1, keepdims=True))


def _softmax(s):
    e = jnp.exp(s - jnp.max(s, axis=-1, keepdims=True))
    return e * (1.0 / jnp.sum(e, axis=-1, keepdims=True))


def _mm(name, a, b, a_spec, b_spec, dn, grid, acc_shape, out_shape, out_specs, epilogue, extra=(), extra_specs=(), dep=None):
    nk = grid[2]
    n_ex = len(extra)
    deps = [] if dep is None else [dep]
    multi = isinstance(out_shape, (list, tuple))
    n_out = len(out_shape) if multi else 1

    def body(*refs):
        a_ref, b_ref = refs[0], refs[1]
        ex = refs[2:2 + n_ex]
        outs = refs[2 + n_ex + len(deps):2 + n_ex + len(deps) + n_out]

        def prod():
            return _dot(a_ref[...].astype(_MXU), b_ref[...].astype(_MXU), dn)

        if nk == 1:
            epilogue(prod(), ex, outs)
        else:
            acc = refs[-1]
            k = pl.program_id(2)

            @pl.when(k == 0)
            def _():
                acc[...] = jnp.zeros_like(acc)

            acc[...] += prod()

            @pl.when(k == nk - 1)
            def _():
                epilogue(acc[...], ex, outs)

    return _call(
        body, name, grid=grid, in_specs=[a_spec, b_spec, *extra_specs] + [_ANY] * len(deps), out_specs=out_specs, out_shape=out_shape,
        scratch_shapes=[] if nk == 1 else [pltpu.VMEM(acc_shape, _F32)],
        compiler_params=_cp("parallel", "parallel", "arbitrary"),
    )(a, b, *extra, *deps)


def _ep_store(acc, ex, outs):
    for o in outs:
        o[...] = acc.astype(o.dtype)


def _ep_add(acc, ex, outs):
    outs[0][...] = (acc + ex[0][...]).astype(outs[0].dtype)


def _mm_nn(name, a, b, out_dtype=_F32, add=None, pm=1024, pn=1024, pk=2048, dep=None):
    M, K = a.shape
    N = b.shape[1]
    tm, tn, tk = _tile(M, pm, 8), _tile(N, pn), _tile(K, pk)
    o_spec = BS((tm, tn), lambda i, j, k: (i, j))
    return _mm(name, a, b, BS((tm, tk), lambda i, j, k: (i, k)), BS((tk, tn), lambda i, j, k: (k, j)), _NN,
               (M // tm, N // tn, K // tk), (tm, tn), S((M, N), out_dtype), o_spec,
               _ep_store if add is None else _ep_add,
               extra=() if add is None else (add,), extra_specs=() if add is None else (o_spec,), dep=dep)


def _mm_nt(name, a, b, out_dtype=_F32, pm=1024, pn=1024, pk=4096):
    M, K = a.shape
    N = b.shape[0]
    tm, tn, tk = _tile(M, pm, 8), _tile(N, pn), _tile(K, pk)
    return _mm(name, a, b, BS((tm, tk), lambda i, j, k: (i, k)), BS((tn, tk), lambda i, j, k: (j, k)), _NT,
               (M // tm, N // tn, K // tk), (tm, tn), S((M, N), out_dtype), BS((tm, tn), lambda i, j, k: (i, j)), _ep_store)


def _mm_tn_dual(name, a, b, pm=1024, pn=1024, pk=2048, dep=None):
    K, M = a.shape
    N = b.shape[1]
    tm, tn, tk = _tile(M, pm), _tile(N, pn), _tile(K, pk, 16)
    o_spec = BS((tm, tn), lambda i, j, k: (i, j))
    return _mm(name, a, b, BS((tk, tm), lambda i, j, k: (k, i)), BS((tk, tn), lambda i, j, k: (k, j)), _TN,
               (M // tm, N // tn, K // tk), (tm, tn), [S((M, N), _F32), S((M, N), _WIRE)], [o_spec, o_spec], _ep_store, dep=dep)


def _ffn_gate_up(h2, wgu):
    T, D = h2.shape
    F = wgu.shape[1] // 2
    tm, tn = _tile(T, 1024, 8), _tile(F, 512)
    nj = F // tn

    def body(a_ref, bg_ref, bu_ref, act_ref, gu_ref):
        a = a_ref[...]
        g = _dot(a, bg_ref[...])
        u = _dot(a, bu_ref[...])
        sg = 1.0 / (1.0 + jnp.exp(-g))
        silu = g * sg
        gu_ref[0] = (u * (sg * (1.0 + g * (1.0 - sg)))).astype(gu_ref.dtype)
        gu_ref[1] = silu.astype(gu_ref.dtype)
        act_ref[...] = (silu * u).astype(act_ref.dtype)

    return _call(
        body, "ffn_gate_up", grid=(T // tm, nj),
        in_specs=[BS((tm, D), lambda i, j: (i, 0)), BS((D, tn), lambda i, j: (0, j)), BS((D, tn), lambda i, j: (0, j + nj))],
        out_specs=[BS((tm, tn), lambda i, j: (i, j)), BS((2, tm, tn), lambda i, j: (0, i, j))],
        out_shape=[S((T, F), _MXU), S((2, T, F), _MXU)],
        compiler_params=_cp("parallel", "parallel"),
    )(h2, wgu, wgu)


def _ffn_dact(dxm, wdown, gu, dep=None):
    T, D = dxm.shape
    F = wdown.shape[0]
    tm, tn = _tile(T, 2048, 8), _tile(F, 512)
    deps = [] if dep is None else [dep]

    def body(a_ref, b_ref, gu_ref, *rest):
        o_ref = rest[-1]
        d = _dot(a_ref[...], b_ref[...], _NT)
        o_ref[0] = (d * gu_ref[0].astype(_F32)).astype(o_ref.dtype)
        o_ref[1] = (d * gu_ref[1].astype(_F32)).astype(o_ref.dtype)

    return _call(
        body, "ffn_dact", grid=(T // tm, F // tn),
        in_specs=[BS((tm, D), lambda i, j: (i, 0)), BS((tn, D), lambda i, j: (j, 0)), BS((2, tm, tn), lambda i, j: (0, i, j))]
        + [_ANY] * len(deps),
        out_specs=BS((2, tm, tn), lambda i, j: (0, i, j)), out_shape=S((2, T, F), _MXU),
        compiler_params=_cp("parallel", "parallel"),
    )(dxm, wdown, gu, *deps)


def _ffn_dh(dgu, wgu, dep=None):
    _, T, F = dgu.shape
    D = wgu.shape[0]
    tm, tn, tk = _tile(T, 1024, 8), _tile(D, 2048), _tile(F, 2048)
    nkf = F // tk
    return _mm("ffn_dh", dgu, wgu, BS((None, tm, tk), lambda i, j, k: (k // nkf, i, k % nkf)),
               BS((tn, tk), lambda i, j, k: (j, k)), _NT, (T // tm, D // tn, 2 * nkf), (tm, tn),
               S((T, D), _F32), BS((tm, tn), lambda i, j, k: (i, j)), _ep_store, dep=dep)


def _ffn_dwgu(h2, dgu):
    _, T, F = dgu.shape
    D = h2.shape[1]
    tm, tn, tk = _tile(D, 1024), _tile(F, 1408), _tile(T, 2048, 16)
    njf = F // tn
    o_spec = BS((tm, tn), lambda i, j, k: (i, j))
    return _mm("ffn_dwgu", h2, dgu, BS((tk, tm), lambda i, j, k: (k, i)),
               BS((None, tk, tn), lambda i, j, k: (j // njf, k, j % njf)), _TN, (D // tm, 2 * njf, T // tk), (tm, tn),
               [S((D, 2 * F), _F32), S((D, 2 * F), _WIRE)], [o_spec, o_spec], _ep_store)


def _rmsnorm_fwd(x, g, name, dep=None):
    T, D = x.shape
    tr = _tile(T, 512, 8)

    def body(x_ref, g_ref, *rest):
        xv = x_ref[...]
        rest[-1][...] = (xv * _rstd(xv) * g_ref[...]).astype(rest[-1].dtype)

    row = BS((tr, D), lambda i: (i, 0))
    deps = [] if dep is None else [dep]
    return _call(body, name, grid=(T // tr,), in_specs=[row, BS((1, D), lambda i: (0, 0))] + [_ANY] * len(deps), out_specs=row,
                 out_shape=S((T, D), _MXU), compiler_params=_cp("parallel"))(x, g.reshape(1, D), *deps)


def _rmsnorm_bwd(x, g, dh, dres, name):
    T, D = x.shape
    tr = _tile(T, 256, 8)
    has_res = dres is not None

    def body(*refs):
        x_ref, g_ref, dh_ref = refs[:3]
        dx_ref, dxm_ref, dg_ref = refs[-3:]

        @pl.when(pl.program_id(0) == 0)
        def _():
            dg_ref[...] = jnp.zeros_like(dg_ref)

        xv = x_ref[...]
        r = _rstd(xv)
        xhat = xv * r
        dh_v = dh_ref[...]
        dg_ref[...] += jnp.sum(dh_v * xhat, axis=0, keepdims=True)
        dx = _norm_bwd(dh_v, xhat, r, g_ref[...])
        if has_res:
            dx = dx + refs[3][...]
        dx_ref[...] = dx
        dxm_ref[...] = dx.astype(dxm_ref.dtype)

    row = BS((tr, D), lambda i: (i, 0))
    vec = BS((1, D), lambda i: (0, 0))
    return _call(body, name, grid=(T // tr,), in_specs=[row, vec, row] + ([row] if has_res else []),
                 out_specs=[row, row, vec], out_shape=[S((T, D), _F32), S((T, D), _MXU), S((1, D), _F32)],
                 compiler_params=_cp("arbitrary"))(x, g.reshape(1, D), dh, *([dres] if has_res else []))


def _loss_head(y, target):
    T, D = y.shape
    tr = _tile(T, 256, 8)

    def body(y_ref, t_ref, dy_ref, dym_ref, acc_ref):
        @pl.when(pl.program_id(0) == 0)
        def _():
            acc_ref[...] = jnp.zeros_like(acc_ref)

        err = y_ref[...] - t_ref[...]
        acc_ref[...] += jnp.sum(jnp.sum(err * err, axis=-1, keepdims=True), axis=0, keepdims=True)
        dy = err * (1.0 / D)
        dy_ref[...] = dy
        dym_ref[...] = dy.astype(dym_ref.dtype)

    row = BS((tr, D), lambda i: (i, 0))
    return _call(body, "loss_head", grid=(T // tr,), in_specs=[row, row],
                 out_specs=[row, row, BS((1, 128), lambda i: (0, 0))],
                 out_shape=[S((T, D), _F32), S((T, D), _MXU), S((1, 128), _F32)],
                 compiler_params=_cp("arbitrary"))(y, target)


def _mixa_blocks(T):
    return 2 if T % (2 * CHUNK) == 0 else 1


def _mixer_a_fwd(z, gv, ws_m, b_t, TOK):
    T = z.shape[0]
    G = TOK // HEAD
    CB = _mixa_blocks(T)
    R = CB * CHUNK

    def body(z_ref, gv_ref, ws_ref, bt_ref, o_ref):
        u = _gelu(z_ref[:, :TOK])
        v = _gelu(z_ref[:, TOK:])
        vn = (v * _rstd(v) * gv_ref[...]).astype(_MXU)
        for c in range(CB):
            rows = slice(c * CHUNK, (c + 1) * CHUNK)
            for g in range(G):
                cols = slice(g * HEAD, (g + 1) * HEAD)
                s = _dot(ws_ref[g], vn[rows, cols]) + bt_ref[:, g:g + 1]
                o_ref[rows, cols] = (u[rows, cols] * s).astype(o_ref.dtype)

    return _call(
        body, "mixer_a_fwd", grid=(T // R,),
        in_specs=[BS((R, 2 * TOK), lambda i: (i, 0)), BS((1, TOK), lambda i: (0, 0)),
                  BS((G, CHUNK, CHUNK), lambda i: (0, 0, 0)), BS((CHUNK, G), lambda i: (0, 0))],
        out_specs=BS((R, TOK), lambda i: (i, 0)), out_shape=S((T, TOK), _MXU), compiler_params=_cp("parallel"),
    )(z, gv.reshape(1, TOK), ws_m, b_t)


def _mixer_a_bwd(z, dcat, gv, ws_m, wst_m, b_t, TOK):
    T = z.shape[0]
    G = TOK // HEAD
    CB = _mixa_blocks(T)
    R = CB * CHUNK
    n = T // R

    def body(z_ref, d_ref, gv_ref, ws_ref, wst_ref, bt_ref, dz_ref, dws_ref, db_ref, dgv_ref, dvn_scr):
        i = pl.program_id(0)

        @pl.when(i == 0)
        def _():
            dws_ref[...] = jnp.zeros_like(dws_ref)
            db_ref[...] = jnp.zeros_like(db_ref)
            dgv_ref[...] = jnp.zeros_like(dgv_ref)

        zu = z_ref[:, :TOK]
        zv = z_ref[:, TOK:]
        u = _gelu(zu)
        v = _gelu(zv)
        r = _rstd(v)
        vhat = v * r
        gvv = gv_ref[...]
        vn = (vhat * gvv).astype(_MXU)
        d = d_ref[...]
        gpu = _gelu_grad(zu)
        for c in range(CB):
            rows = slice(c * CHUNK, (c + 1) * CHUNK)
            for g in range(G):
                cols = slice(g * HEAD, (g + 1) * HEAD)
                vn_cg = vn[rows, cols]
                s = _dot(ws_ref[g], vn_cg) + bt_ref[:, g:g + 1]
                d_cg = d[rows, cols]
                dz_ref[rows, cols] = (d_cg * s * gpu[rows, cols]).astype(dz_ref.dtype)
                ds = d_cg * u[rows, cols]
                ds_m = ds.astype(_MXU)
                dvn_scr[rows, cols] = _dot(wst_ref[g], ds_m)
                dws_ref[g] += _dot(ds_m, vn_cg, _NT)
                db_ref[g] += ds
        dvn = dvn_scr[...]
        dgv_ref[...] += jnp.sum(dvn * vhat, axis=0, keepdims=True)
        dv = _norm_bwd(dvn, vhat, r, gvv)
        dz_ref[:, TOK:] = (dv * _gelu_grad(zv)).astype(dz_ref.dtype)

        @pl.when(i == n - 1)
        def _():
            for g in range(G):
                db_ref[g] = jnp.broadcast_to(jnp.sum(db_ref[g], axis=1, keepdims=True), (CHUNK, CHUNK))

    full3 = BS((G, CHUNK, CHUNK), lambda i: (0, 0, 0))
    return _call(
        body, "mixer_a_bwd", grid=(n,),
        in_specs=[BS((R, 2 * TOK), lambda i: (i, 0)), BS((R, TOK), lambda i: (i, 0)), BS((1, TOK), lambda i: (0, 0)),
                  full3, full3, BS((CHUNK, G), lambda i: (0, 0))],
        out_specs=[BS((R, 2 * TOK), lambda i: (i, 0)), full3, full3, BS((1, TOK), lambda i: (0, 0))],
        out_shape=[S((T, 2 * TOK), _MXU), S((G, CHUNK, CHUNK), _F32), S((G, CHUNK, CHUNK), _F32), S((1, TOK), _F32)],
        scratch_shapes=[pltpu.VMEM((R, TOK), _F32)], compiler_params=_cp("arbitrary"),
    )(z, dcat, gv.reshape(1, TOK), ws_m, wst_m, b_t)


def _rope_tables(T):
    n_rows = T // GRID_W
    rows = jnp.broadcast_to(jnp.arange(n_rows)[:, None], (n_rows, GRID_W)).reshape(T)
    cols = jnp.broadcast_to(jnp.arange(GRID_W)[None, :], (n_rows, GRID_W)).reshape(T)
    pairs = HEAD // 4
    freqs = ROPE_THETA ** (-jnp.arange(pairs, dtype=_F32) / pairs)
    ang_r = rows.astype(_F32)[:, None] * freqs
    ang_c = cols.astype(_F32)[:, None] * freqs
    ang = jnp.concatenate([ang_r, ang_r, ang_c, ang_c], axis=-1)
    cos, sin = jnp.cos(ang), jnp.sin(ang)
    first = (jnp.arange(HEAD) % (HEAD // 2)) < (HEAD // 4)
    return cos, jnp.where(first, -sin, 0.0), jnp.where(first, 0.0, sin)


def _rope(x, cs, sa, sb):
    return x * cs + pltpu.roll(x, 96, 1) * sa + pltpu.roll(x, 32, 1) * sb


def _qk_rope_fwd(z, gq, gk, tabs, TOK, KV):
    T = z.shape[0]
    R = _tile(T, 512, 8)
    W = TOK + 2 * KV

    def body(z_ref, gq_ref, gk_ref, cos_ref, sa_ref, sb_ref, q_ref, k_ref, v_ref):
        cs, sa, sb = cos_ref[...], sa_ref[...], sb_ref[...]
        for h in range((TOK + KV) // HEAD):
            cols = slice(h * HEAD, (h + 1) * HEAD)
            xv = z_ref[:, cols]
            xn = xv * _rstd(xv) * (gq_ref[...] if h < TOK // HEAD else gk_ref[...])
            out = _rope(xn, cs, sa, sb)
            if h < TOK // HEAD:
                q_ref[:, cols] = out.astype(q_ref.dtype)
            else:
                k_ref[:, h * HEAD - TOK:(h + 1) * HEAD - TOK] = out.astype(k_ref.dtype)
        v_ref[...] = z_ref[:, TOK + KV:].astype(v_ref.dtype)

    vec = BS((1, HEAD), lambda i: (0, 0))
    tab = BS((R, HEAD), lambda i: (i, 0))
    return _call(
        body, "qk_rope_fwd", grid=(T // R,), in_specs=[BS((R, W), lambda i: (i, 0)), vec, vec, tab, tab, tab],
        out_specs=[BS((R, TOK), lambda i: (i, 0)), BS((R, KV), lambda i: (i, 0)), BS((R, KV), lambda i: (i, 0))],
        out_shape=[S((T, TOK), _MXU), S((T, KV), _MXU), S((T, KV), _MXU)], compiler_params=_cp("parallel"),
    )(z, gq.reshape(1, HEAD), gk.reshape(1, HEAD), *tabs)


def _qk_rope_bwd(z, dq, dk, gq, gk, tabs, TOK, KV):
    T = z.shape[0]
    R = _tile(T, 512, 8)
    W = TOK + KV

    def body(z_ref, dq_ref, dk_ref, gq_ref, gk_ref, cos_ref, sa_ref, sb_ref, dz_ref, dgq_ref, dgk_ref):
        @pl.when(pl.program_id(0) == 0)
        def _():
            dgq_ref[...] = jnp.zeros_like(dgq_ref)
            dgk_ref[...] = jnp.zeros_like(dgk_ref)

        cs, sa, sb = cos_ref[...], sa_ref[...], sb_ref[...]
        for h in range(W // HEAD):
            cols = slice(h * HEAD, (h + 1) * HEAD)
            is_q = h < TOK // HEAD
            do = dq_ref[:, cols] if is_q else dk_ref[:, h * HEAD - TOK:(h + 1) * HEAD - TOK]
            dxn = do * cs - pltpu.roll(do, 96, 1) * sa - pltpu.roll(do, 32, 1) * sb
            xv = z_ref[:, cols]
            r = _rstd(xv)
            xhat = xv * r
            dg_ref = dgq_ref if is_q else dgk_ref
            dg_ref[...] += jnp.sum(dxn * xhat, axis=0, keepdims=True)
            dz_ref[:, cols] = _norm_bwd(dxn, xhat, r, gq_ref[...] if is_q else gk_ref[...]).astype(dz_ref.dtype)

    vec = BS((1, HEAD), lambda i: (0, 0))
    tab = BS((R, HEAD), lambda i: (i, 0))
    return _call(
        body, "qk_rope_bwd", grid=(T // R,),
        in_specs=[BS((R, W), lambda i: (i, 0)), BS((R, TOK), lambda i: (i, 0)), BS((R, KV), lambda i: (i, 0)), vec, vec, tab, tab, tab],
        out_specs=[BS((R, W), lambda i: (i, 0)), vec, vec],
        out_shape=[S((T, W), _MXU), S((1, HEAD), _F32), S((1, HEAD), _F32)], compiler_params=_cp("arbitrary"),
    )(z, dq, dk, gq.reshape(1, HEAD), gk.reshape(1, HEAD), *tabs)


_ATTN_C2 = float(HEAD ** -0.5 * np.log2(np.e))


def _attn_fwd(q, k, v, QPK):
    T, TOK = q.shape
    KVH = k.shape[1] // HEAD
    tq = _tile(T, 256, 8)
    W = QPK * HEAD

    def body(q_ref, k_ref, v_ref, o_ref, st_ref, vaug):
        @pl.when(pl.program_id(1) == 0)
        def _():
            vaug[:, :HEAD] = v_ref[...]
            vaug[:, HEAD:] = jnp.ones((T, HEAD), vaug.dtype)

        kk, va = k_ref[...], vaug[...]
        for g in range(QPK):
            cols = slice(g * HEAD, (g + 1) * HEAD)
            s = _dot(q_ref[:, cols], kk, _NT)
            m = jnp.max(s, axis=-1, keepdims=True)
            ov = _dot(jnp.exp2((s - m) * _ATTN_C2).astype(_MXU), va)
            l = ov[:, HEAD:HEAD + 1]
            o_ref[:, cols] = (ov[:, :HEAD] * (1.0 / l)).astype(o_ref.dtype)
            st_ref[:, g:g + 1] = m + jnp.log2(l) * (1.0 / _ATTN_C2)

    qs = BS((tq, W), lambda h, i: (i, h))
    ks = BS((T, HEAD), lambda h, i: (0, h))
    return _call(body, "attn_fwd", grid=(KVH, T // tq), in_specs=[qs, ks, ks],
                 out_specs=[qs, BS((None, tq, QPK), lambda h, i: (h, i, 0))],
                 out_shape=[S((T, TOK), _MXU), S((KVH, T, QPK), _F32)],
                 scratch_shapes=[pltpu.VMEM((T, 2 * HEAD), _MXU)],
                 compiler_params=_cp("parallel", "arbitrary"))(q, k, v)


def _attn_bwd(q, k, v, dcat, o, stat, QPK):
    T, TOK = q.shape
    KV = k.shape[1]
    KVH = KV // HEAD
    tq = _tile(T, 256, 8)
    nq = T // tq
    W = QPK * HEAD
    scale = HEAD ** -0.5

    def body(q_ref, k_ref, v_ref, do_ref, o_ref, st_ref, dq_ref, dk_ref, dv_ref, dk_acc, dv_acc, ds_all, p_all, q_all, do_all):
        i = pl.program_id(1)

        @pl.when(i == 0)
        def _():
            dk_acc[...] = jnp.zeros_like(dk_acc)
            dv_acc[...] = jnp.zeros_like(dv_acc)

        kk, vv = k_ref[...], v_ref[...]
        for g in range(QPK):
            cols = slice(g * HEAD, (g + 1) * HEAD)
            rows = slice(g * tq, (g + 1) * tq)
            qg = q_ref[:, cols]
            p = jnp.exp2((_dot(qg, kk, _NT) - st_ref[:, g:g + 1]) * _ATTN_C2)
            do32 = do_ref[:, cols]
            do = do32.astype(_MXU)
            delta = jnp.sum(do32 * o_ref[:, cols].astype(_F32), axis=-1, keepdims=True)
            ds = (p * (_dot(do, vv, _NT) - delta)).astype(_MXU)
            dq_ref[:, cols] = _dot(ds, kk) * scale
            ds_all[rows, :] = ds
            p_all[rows, :] = p.astype(_MXU)
            q_all[rows, :] = qg
            do_all[rows, :] = do
        dk_acc[...] += _dot(ds_all[...], q_all[...], _TN)
        dv_acc[...] += _dot(p_all[...], do_all[...], _TN)

        @pl.when(i == nq - 1)
        def _():
            dk_ref[...] = dk_acc[...] * scale
            dv_ref[...] = dv_acc[...].astype(dv_ref.dtype)

    qs = BS((tq, W), lambda h, i: (i, h))
    ks = BS((T, HEAD), lambda h, i: (0, h))
    return _call(
        body, "attn_bwd", grid=(KVH, nq), in_specs=[qs, ks, ks, qs, qs, BS((None, tq, QPK), lambda h, i: (h, i, 0))],
        out_specs=[qs, ks, ks], out_shape=[S((T, TOK), _F32), S((T, KV), _F32), S((T, KV), _MXU)],
        scratch_shapes=[pltpu.VMEM((T, HEAD), _F32), pltpu.VMEM((T, HEAD), _F32), pltpu.VMEM((QPK * tq, T), _MXU),
                        pltpu.VMEM((QPK * tq, T), _MXU), pltpu.VMEM((QPK * tq, HEAD), _MXU), pltpu.VMEM((QPK * tq, HEAD), _MXU)],
        compiler_params=_cp("parallel", "arbitrary"),
    )(q, k, v, dcat, o, stat)


def _mem_fwd(z, qblk, kv, gmq, gmk, MEMW):
    T = z.shape[0]
    NM = kv.shape[0]
    tq = _tile(T, 512, 8)
    scale = HEAD ** -0.5

    def body(q_ref, kv_ref, gq_ref, gk_ref, o_ref):
        for h in range(MEMW // HEAD):
            cols = slice(h * HEAD, (h + 1) * HEAD)
            kx = kv_ref[:, cols]
            kn = (kx * _rstd(kx) * gk_ref[...]).astype(_MXU)
            vv = kv_ref[:, MEMW + h * HEAD:MEMW + (h + 1) * HEAD].astype(_MXU)
            qx = q_ref[:, cols]
            qn = (qx * _rstd(qx) * gq_ref[...]).astype(_MXU)
            p = _softmax(_dot(qn, kn, _NT) * scale)
            o_ref[:, cols] = _dot(p.astype(_MXU), vv).astype(o_ref.dtype)

    vec = BS((1, HEAD), lambda i: (0, 0))
    return _call(
        body, "mem_fwd", grid=(T // tq,),
        in_specs=[BS((tq, MEMW), lambda i: (i, qblk)), BS((NM, 2 * MEMW), lambda i: (0, 0)), vec, vec],
        out_specs=BS((tq, MEMW), lambda i: (i, 0)), out_shape=S((T, MEMW), _MXU), compiler_params=_cp("parallel"),
    )(z, kv, gmq.reshape(1, HEAD), gmk.reshape(1, HEAD))


def _mem_bwd(z, qblk, kv, gmq, gmk, dcat, dblk, MEMW):
    T = z.shape[0]
    NM = kv.shape[0]
    tq = _tile(T, 512, 8)
    scale = HEAD ** -0.5

    def body(q_ref, kv_ref, gq_ref, gk_ref, do_ref, dz_ref, dkn_ref, dv_ref, dgq_ref):
        @pl.when(pl.program_id(0) == 0)
        def _():
            dkn_ref[...] = jnp.zeros_like(dkn_ref)
            dv_ref[...] = jnp.zeros_like(dv_ref)
            dgq_ref[...] = jnp.zeros_like(dgq_ref)

        for h in range(MEMW // HEAD):
            cols = slice(h * HEAD, (h + 1) * HEAD)
            kx = kv_ref[:, cols]
            kn = (kx * _rstd(kx) * gk_ref[...]).astype(_MXU)
            vv = kv_ref[:, MEMW + h * HEAD:MEMW + (h + 1) * HEAD].astype(_MXU)
            qx = q_ref[:, cols]
            rq = _rstd(qx)
            qhat = qx * rq
            qn = (qhat * gq_ref[...]).astype(_MXU)
            p = _softmax(_dot(qn, kn, _NT) * scale)
            do = do_ref[:, cols].astype(_MXU)
            dp = _dot(do, vv, _NT)
            ds = (p * (dp - jnp.sum(p * dp, axis=-1, keepdims=True)) * scale).astype(_MXU)
            dqn = _dot(ds, kn)
            dkn_ref[:, cols] += _dot(ds, qn, _TN)
            dv_ref[:, cols] += _dot(p.astype(_MXU), do, _TN)
            dgq_ref[...] += jnp.sum(dqn * qhat, axis=0, keepdims=True)
            dz_ref[:, cols] = _norm_bwd(dqn, qhat, rq, gq_ref[...]).astype(dz_ref.dtype)

    vec = BS((1, HEAD), lambda i: (0, 0))
    kvs = BS((NM, MEMW), lambda i: (0, 0))
    return _call(
        body, "mem_bwd", grid=(T // tq,),
        in_specs=[BS((tq, MEMW), lambda i: (i, qblk)), BS((NM, 2 * MEMW), lambda i: (0, 0)), vec, vec,
                  BS((tq, MEMW), lambda i: (i, dblk))],
        out_specs=[BS((tq, MEMW), lambda i: (i, 0)), kvs, kvs, vec],
        out_shape=[S((T, MEMW), _MXU), S((NM, MEMW), _F32), S((NM, MEMW), _F32), S((1, HEAD), _F32)],
        compiler_params=_cp("arbitrary"),
    )(z, kv, gmq.reshape(1, HEAD), gmk.reshape(1, HEAD), dcat)


def _memkv_bwd(kv, dkn, dv, gmk, MEMW):
    NM = kv.shape[0]

    def body(kv_ref, dkn_ref, dv_ref, gk_ref, dkv_ref, dgk_ref):
        dgk = jnp.zeros((1, HEAD), _F32)
        for h in range(MEMW // HEAD):
            cols = slice(h * HEAD, (h + 1) * HEAD)
            kx = kv_ref[:, cols]
            r = _rstd(kx)
            khat = kx * r
            dkn = dkn_ref[:, cols]
            dgk = dgk + jnp.sum(dkn * khat, axis=0, keepdims=True)
            dkv_ref[:, cols] = _norm_bwd(dkn, khat, r, gk_ref[...]).astype(dkv_ref.dtype)
        dgk_ref[...] = dgk
        dkv_ref[:, MEMW:] = dv_ref[...].astype(dkv_ref.dtype)

    return _call(body, "memkv_bwd", out_shape=[S((NM, 2 * MEMW), _MXU), S((1, HEAD), _F32)],
                 compiler_params=_cp())(kv, dkn, dv, gmk.reshape(1, HEAD))


def _cast_into_full(w, l, sh, idx, dep=None):
    tr, tc = _tile(sh.Rs, 512, 16), _tile(sh.Cs, 2048)
    nr, nc = sh.Rs // tr, sh.Cs // tc
    deps = [] if dep is None else [dep]

    def body(i_ref, c_ref, w_ref, *rest):
        rest[-1][...] = w_ref[...].astype(rest[-1].dtype)

    if sh.by_cols:
        o_map = lambda a, b, si, sc: (a, si[0] * nc + b)
    else:
        o_map = lambda a, b, si, sc: (si[0] * nr + a, b)
    return _call(
        body, "cast_into_full",
        grid_spec=pltpu.PrefetchScalarGridSpec(
            num_scalar_prefetch=2, grid=(nr, nc),
            in_specs=[BS((None, tr, tc), lambda a, b, si, sc: (l, a, b))] + [_ANY] * len(deps), out_specs=BS((tr, tc), o_map)),
        out_shape=S((sh.R, sh.C), _WIRE), compiler_params=_cp("parallel", "parallel"),
    )(*idx, w, *deps)


def _adamw(w, g, m, v, name, l0=0, l1=None, prev=None):
    L, R, C = w.shape
    l1 = L if l1 is None else l1
    tc = _tile(C, 2048)
    tr = _tile(R, max(8, (512 * 1024) // tc), 8)
    c_m = 1.0 / (1.0 - ADAM_B1 ** ADAM_STEP)
    c_v = 1.0 / (1.0 - ADAM_B2 ** ADAM_STEP)

    def body(w_ref, g_ref, m_ref, v_ref, *rest):
        go_ref, d_ref, mo_ref, vo_ref = rest[-4:]
        gv = g_ref[...]
        mn = ADAM_B1 * m_ref[...] + (1.0 - ADAM_B1) * gv
        vn = ADAM_B2 * v_ref[...] + (1.0 - ADAM_B2) * (gv * gv)
        go_ref[...] = gv
        mo_ref[...] = mn
        vo_ref[...] = vn
        d_ref[...] = -ADAM_LR * ((mn * c_m) / (jnp.sqrt(vn * c_v) + ADAM_EPS) + ADAM_WD * w_ref[...])

    blk = BS((None, tr, tc), lambda a, i, j: (l0 + a, i, j))
    prevs = [] if prev is None else list(prev)
    return _call(body, name, grid=(l1 - l0, R // tr, C // tc), in_specs=[blk] * 4 + [_ANY] * len(prevs), out_specs=[blk] * 4,
                 out_shape=[S((L, R, C), _F32)] * 4, input_output_aliases={4 + k: k for k in range(len(prevs))},
                 compiler_params=_cp("parallel", "parallel", "parallel"))(w, g, m, v, *prevs)


def _where_am_i():
    x, y, c = lax.axis_index("x"), lax.axis_index("y"), lax.axis_index("c")
    chips = [(1 - x, y), (x, 1 - y), (1 - x, 1 - y)]
    return x, y, c, 2 * x + y, chips, [2 * cx + cy for cx, cy in chips]


class _Shard:
    def __init__(self, R, C, by_cols):
        self.R, self.C, self.by_cols = R, C, by_cols
        self.Rs, self.Cs = (R, C // 4) if by_cols else (R // 4, C)
        self.Rh = self.Rs // 2
        self.Q = R // 8

    def full_piece(self, ref, j, cc):
        if self.by_cols:
            return ref.at[pl.ds(cc * self.Rh, self.Rh), pl.ds(_mo(j * self.Cs, 128), self.Cs)]
        return ref.at[pl.ds(_mo(j * self.Rs + cc * self.Rh, 16), self.Rh), :]

    def full_shard(self, ref, j):
        if self.by_cols:
            return ref.at[:, pl.ds(_mo(j * self.Cs, 128), self.Cs)]
        return ref.at[pl.ds(_mo(j * self.Rs, 16), self.Rs), :]

    def shard_half(self, ref, cc):
        return ref.at[pl.ds(_mo(cc * self.Rh, 16), self.Rh), :]

    def half_piece(self, ref, j):
        if self.by_cols:
            return ref.at[:, pl.ds(_mo(j * self.Cs, 128), self.Cs)]
        return ref.at[pl.ds(_mo(j * self.Rh, 16), self.Rh), :]


def _remote(src, dst, ssem, rsem, dev):
    return pltpu.make_async_remote_copy(src_ref=src, dst_ref=dst, send_sem=ssem, recv_sem=rsem, device_id=dev, device_id_type=_MESH)


_HBM = pl.BlockSpec(memory_space=pltpu.HBM)
_SEM = pl.BlockSpec(memory_space=pltpu.SEMAPHORE)
_EFFECT = pltpu.SideEffectType.DATAFLOW_SIDE_EFFECTING


def _in_hbm(a):
    return pltpu.with_memory_space_constraint(a, pltpu.HBM)


def _gather_start(fulls, shs, name):
    n = len(fulls)

    def body(*refs):
        bufs = refs[:n]
        send_sems, recv_sems = refs[n], refs[n + 1]
        token = refs[-1]
        x, y, c, me, chips, chip_ids = _where_am_i()
        for t in range(n):
            mine = shs[t].full_piece(bufs[t], me, c)
            for r in range(3):
                _remote(mine, mine, send_sems.at[3 * t + r], recv_sems.at[3 * t + r], (*chips[r], c)).start()
        token[...] = jnp.zeros_like(token)

    out = pl.pallas_call(
        body, name=name, in_specs=[_HBM] * n,
        out_shape=(pltpu.SemaphoreType.DMA((3 * n,)), pltpu.SemaphoreType.DMA((3 * n,)), *[pltpu.HBM(f.shape, f.dtype) for f in fulls],
                   S((8, 128), _F32)),
        out_specs=(_SEM, _SEM, *[_HBM] * n, pl.BlockSpec(memory_space=pltpu.VMEM)),
        input_output_aliases={t: 2 + t for t in range(n)},
        compiler_params=pltpu.CompilerParams(has_side_effects=_EFFECT), **_KW,
    )(*[_in_hbm(f) for f in fulls])
    return out[0], out[1], list(out[2:2 + n]), out[-1]


def _gather_wait(fulls, send_sems, recv_sems, after, shs, name):
    n = len(fulls)

    def body(*refs):
        bufs = refs[:n]
        ssem, rsem = refs[n], refs[n + 1]
        x, y, c, me, chips, chip_ids = _where_am_i()
        for t in range(n):
            mine = shs[t].full_piece(bufs[t], me, c)
            for r in range(3):
                _remote(mine, mine, ssem.at[3 * t + r], rsem.at[3 * t + r], (*chips[r], c)).wait_send()
        for t in range(n):
            for r in range(3):
                piece = shs[t].full_piece(bufs[t], chip_ids[r], c)
                _remote(piece, piece, ssem.at[3 * t + r], rsem.at[3 * t + r], (*chips[r], c)).wait_recv()

    out = pl.pallas_call(
        body, name=name, in_specs=[*[_HBM] * n, _SEM, _SEM, _ANY], out_specs=[_HBM] * n,
        out_shape=[pltpu.HBM(f.shape, f.dtype) for f in fulls], input_output_aliases={t: t for t in range(n)},
        compiler_params=pltpu.CompilerParams(has_side_effects=_EFFECT), **_KW,
    )(*fulls, send_sems, recv_sems, after)
    return list(out)


def _gather_pass_on(fulls, shs, name):
    n = len(fulls)

    def body(*refs):
        bufs = refs[n:2 * n]
        send_sems, recv_sems = refs[2 * n:]
        x, y, c, me, chips, chip_ids = _where_am_i()
        sib = (x, y, 1 - c)
        cps = []
        for t in range(n):
            for r in range(3):
                piece = shs[t].full_piece(bufs[t], chip_ids[r], c)
                cps.append(_remote(piece, piece, send_sems.at[t, r], recv_sems.at[t, r], sib))
        for cp in cps:
            cp.start()
        for t in range(n):
            for r in range(3):
                piece = shs[t].full_piece(bufs[t], chip_ids[r], 1 - c)
                _remote(piece, piece, send_sems.at[t, r], recv_sems.at[t, r], sib).wait_recv()
        for cp in cps:
            cp.wait_send()

    return _call(
        body, name, in_specs=[_ANY] * n, out_specs=[_ANY] * n, out_shape=[S(f.shape, f.dtype) for f in fulls],
        input_output_aliases={t: t for t in range(n)},
        scratch_shapes=[pltpu.SemaphoreType.DMA((n, 3)), pltpu.SemaphoreType.DMA((n, 3))],
        compiler_params=pltpu.CompilerParams(has_side_effects=True),
    )(*fulls)


def _pass_on_copies(bufs, shs, send_sems, recv_sems):
    x, y, c, me, chips, chip_ids = _where_am_i()
    sib = (x, y, 1 - c)
    out, back = [], []
    for t in range(len(bufs)):
        for r in range(3):
            piece = shs[t].full_piece(bufs[t], chip_ids[r], c)
            out.append(_remote(piece, piece, send_sems.at[3 * t + r], recv_sems.at[3 * t + r], sib))
            other = shs[t].full_piece(bufs[t], chip_ids[r], 1 - c)
            back.append(_remote(other, other, send_sems.at[3 * t + r], recv_sems.at[3 * t + r], sib))
    return out, back


def _pass_on_start(fulls, shs, name):
    n = len(fulls)

    def body(*refs):
        for cp in _pass_on_copies(refs[:n], shs, refs[n], refs[n + 1])[0]:
            cp.start()
        refs[-1][...] = jnp.zeros_like(refs[-1])

    out = pl.pallas_call(
        body, name=name, in_specs=[_HBM] * n,
        out_shape=(pltpu.SemaphoreType.DMA((3 * n,)), pltpu.SemaphoreType.DMA((3 * n,)), *[pltpu.HBM(f.shape, f.dtype) for f in fulls],
                   S((8, 128), _F32)),
        out_specs=(_SEM, _SEM, *[_HBM] * n, pl.BlockSpec(memory_space=pltpu.VMEM)),
        input_output_aliases={t: 2 + t for t in range(n)},
        compiler_params=pltpu.CompilerParams(has_side_effects=_EFFECT), **_KW,
    )(*[_in_hbm(f) for f in fulls])
    return out[0], out[1], list(out[2:2 + n]), out[-1]


def _pass_on_wait(fulls, send_sems, recv_sems, after, shs, name):
    n = len(fulls)

    def body(*refs):
        out, back = _pass_on_copies(refs[:n], shs, refs[n], refs[n + 1])
        for cp in out:
            cp.wait_send()
        for cp in back:
            cp.wait_recv()

    out = pl.pallas_call(
        body, name=name, in_specs=[*[_HBM] * n, _SEM, _SEM, _ANY], out_specs=[_HBM] * n,
        out_shape=[pltpu.HBM(f.shape, f.dtype) for f in fulls], input_output_aliases={t: t for t in range(n)},
        compiler_params=pltpu.CompilerParams(has_side_effects=_EFFECT), **_KW,
    )(*fulls, send_sems, recv_sems, after)
    return list(out)


def _rs_pair_copies(ins, outs, shs, send_sems, recv_sems):
    x, y, c, *_ = _where_am_i()
    sib = (x, y, 1 - c)
    cps = []
    for t in range(len(ins)):
        sh = shs[t]
        if sh.by_cols:
            cps.append(_remote(ins[t].at[pl.ds((1 - c) * sh.Rh, sh.Rh), :], outs[t], send_sems.at[4 * t], recv_sems.at[4 * t], sib))
        else:
            for j in range(4):
                cps.append(_remote(sh.full_piece(ins[t], j, 1 - c), sh.half_piece(outs[t], j),
                                   send_sems.at[4 * t + j], recv_sems.at[4 * t + j], sib))
    return cps


def _rs_pair_start(dws, shs, name):
    n = len(dws)
    lands = [lax.empty((sh.R // 2, sh.C), _WIRE) for sh in shs]

    def body(*refs):
        for cp in _rs_pair_copies(refs[:n], refs[n:2 * n], shs, refs[2 * n], refs[2 * n + 1]):
            cp.start()
        refs[-1][...] = jnp.zeros_like(refs[-1])

    out = pl.pallas_call(
        body, name=name, in_specs=[_HBM] * (2 * n),
        out_shape=(pltpu.SemaphoreType.DMA((4 * n,)), pltpu.SemaphoreType.DMA((4 * n,)),
                   *[pltpu.HBM(a.shape, a.dtype) for a in (*dws, *lands)], S((8, 128), _F32)),
        out_specs=(_SEM, _SEM, *[_HBM] * (2 * n), pl.BlockSpec(memory_space=pltpu.VMEM)),
        input_output_aliases={t: 2 + t for t in range(2 * n)},
        compiler_params=pltpu.CompilerParams(has_side_effects=_EFFECT), **_KW,
    )(*[_in_hbm(a) for a in (*dws, *lands)])
    return out[0], out[1], list(out[2:2 + n]), list(out[2 + n:2 + 2 * n]), out[-1]


def _rs_pair_wait(dws, lands, send_sems, recv_sems, after, shs, name):
    n = len(dws)

    def body(*refs):
        cps = _rs_pair_copies(refs[:n], refs[n:2 * n], shs, refs[2 * n], refs[2 * n + 1])
        for cp in cps:
            cp.wait_send()
        for cp in cps:
            cp.wait_recv()

    out = pl.pallas_call(
        body, name=name, in_specs=[*[_HBM] * (2 * n), _SEM, _SEM, _ANY], out_specs=[_HBM] * (2 * n),
        out_shape=[pltpu.HBM(a.shape, a.dtype) for a in (*dws, *lands)], input_output_aliases={t: t for t in range(2 * n)},
        compiler_params=pltpu.CompilerParams(has_side_effects=_EFFECT), **_KW,
    )(*dws, *lands, send_sems, recv_sems, after)
    return list(out[n:])


def _rs_pair_add(dw32, recv, sh, idx):
    tr, tc = _tile(sh.Q, 512, 16), _tile(sh.C, 2048)
    nb = sh.Q // tr

    def body(i_ref, c_ref, a_ref, b_ref, ow_ref):
        ow_ref[...] = (a_ref[...] + b_ref[...].astype(_F32)).astype(ow_ref.dtype)

    if sh.by_cols:
        a_map = lambda j, i, b, si, sc: (sc[0] * 4 * nb + j * nb + i, b)
    else:
        a_map = lambda j, i, b, si, sc: (j * 2 * nb + sc[0] * nb + i, b)
    h_spec = BS((tr, tc), lambda j, i, b, si, sc: (j * nb + i, b))
    return _call(
        body, "rs_pair_add",
        grid_spec=pltpu.PrefetchScalarGridSpec(num_scalar_prefetch=2, grid=(4, nb, sh.C // tc),
                                               in_specs=[BS((tr, tc), a_map), h_spec], out_specs=h_spec),
        out_shape=S((sh.R // 2, sh.C), _WIRE), compiler_params=_cp("parallel", "parallel", "parallel"),
    )(*idx, dw32, recv)


def _rs_chip_start(pws, shs, name):
    n = len(pws)
    lands = [lax.empty((3, sh.Rh, sh.Cs), _WIRE) for sh in shs]

    def body(*refs):
        ins, lnd = refs[:n], refs[n:2 * n]
        send_sems, recv_sems = refs[2 * n], refs[2 * n + 1]
        token = refs[-1]
        x, y, c, me, chips, chip_ids = _where_am_i()
        for t in range(n):
            for r in range(3):
                _remote(shs[t].half_piece(ins[t], chip_ids[r]), lnd[t].at[r], send_sems.at[3 * t + r], recv_sems.at[3 * t + r],
                        (*chips[r], c)).start()
        token[...] = jnp.zeros_like(token)

    out = pl.pallas_call(
        body, name=name, in_specs=[_HBM] * (2 * n),
        out_shape=(pltpu.SemaphoreType.DMA((3 * n,)), pltpu.SemaphoreType.DMA((3 * n,)),
                   *[pltpu.HBM(a.shape, a.dtype) for a in (*pws, *lands)], S((8, 128), _F32)),
        out_specs=(_SEM, _SEM, *[_HBM] * (2 * n), pl.BlockSpec(memory_space=pltpu.VMEM)),
        input_output_aliases={t: 2 + t for t in range(2 * n)},
        compiler_params=pltpu.CompilerParams(has_side_effects=_EFFECT), **_KW,
    )(*[_in_hbm(a) for a in (*pws, *lands)])
    return out[0], out[1], list(out[2:2 + n]), list(out[2 + n:2 + 2 * n]), out[-1]


def _rs_chip_wait(pws, lands, send_sems, recv_sems, after, shs, name):
    n = len(pws)
    after = list(after) if isinstance(after, (list, tuple)) else [after]

    def body(*refs):
        ins, lnd = refs[:n], refs[n:2 * n]
        ssem, rsem = refs[2 * n], refs[2 * n + 1]
        x, y, c, me, chips, chip_ids = _where_am_i()
        for t in range(n):
            for r in range(3):
                cp = _remote(shs[t].half_piece(ins[t], chip_ids[r]), lnd[t].at[r], ssem.at[3 * t + r], rsem.at[3 * t + r], (*chips[r], c))
                cp.wait_send()
        for t in range(n):
            for r in range(3):
                cp = _remote(shs[t].half_piece(ins[t], chip_ids[r]), lnd[t].at[r], ssem.at[3 * t + r], rsem.at[3 * t + r], (*chips[r], c))
                cp.wait_recv()

    out = pl.pallas_call(
        body, name=name, in_specs=[*[_HBM] * (2 * n), _SEM, _SEM, *[_ANY] * len(after)], out_specs=[_HBM] * (2 * n),
        out_shape=[pltpu.HBM(a.shape, a.dtype) for a in (*pws, *lands)], input_output_aliases={t: t for t in range(2 * n)},
        compiler_params=pltpu.CompilerParams(has_side_effects=_EFFECT), **_KW,
    )(*pws, *lands, send_sems, recv_sems, *after)
    return list(out[n:])


def _rs_chip_add(dw32, pair, recv, sh, idx, g_prev, l, L):
    tr, tc = _tile(sh.Rh, 512, 16), _tile(sh.Cs, 2048)
    nr, nc = sh.Rh // tr, sh.Cs // tc

    def body(i_ref, c_ref, d_ref, a_ref, b_ref, *rest):
        rest[-1][...] = ((d_ref[...] + a_ref[...].astype(_F32)) + b_ref[0].astype(_F32) + b_ref[1].astype(_F32)
                         + b_ref[2].astype(_F32))

    if sh.by_cols:
        d_map = lambda a, b, si, sc: (sc[0] * nr + a, si[0] * nc + b)
        a_map = lambda a, b, si, sc: (a, si[0] * nc + b)
    else:
        d_map = lambda a, b, si, sc: ((2 * si[0] + sc[0]) * nr + a, b)
        a_map = lambda a, b, si, sc: (si[0] * nr + a, b)
    in_specs = [BS((tr, tc), d_map), BS((tr, tc), a_map), BS((3, tr, tc), lambda a, b, si, sc: (0, a, b))]
    args = [*idx, dw32, pair, recv]
    if g_prev is not None:
        in_specs.append(_ANY)
        args.append(g_prev)
    return _call(
        body, "rs_chip_add",
        grid_spec=pltpu.PrefetchScalarGridSpec(num_scalar_prefetch=2, grid=(nr, nc), in_specs=in_specs,
                                               out_specs=BS((None, tr, tc), lambda a, b, si, sc: (l, sc[0] * nr + a, b))),
        out_shape=S((L, sh.Rs, sh.Cs), _F32), input_output_aliases={} if g_prev is None else {5: 0},
        compiler_params=_cp("parallel", "parallel"),
    )(*args)


def _rs_pair_share(gs, ls, shs, name):
    n = len(gs)

    def body(*refs):
        bufs = refs[n:2 * n]
        send_sems, recv_sems = refs[2 * n:]
        x, y, c, *_ = _where_am_i()
        sib = (x, y, 1 - c)
        cps = []
        for t in range(n):
            mine = shs[t].shard_half(bufs[t].at[ls[t]], c)
            cps.append(_remote(mine, mine, send_sems.at[t], recv_sems.at[t], sib))
        for cp in cps:
            cp.start()
        for t in range(n):
            other = shs[t].shard_half(bufs[t].at[ls[t]], 1 - c)
            _remote(other, other, send_sems.at[t], recv_sems.at[t], sib).wait_recv()
        for cp in cps:
            cp.wait_send()

    return _call(
        body, name, in_specs=[_ANY] * n, out_specs=[_ANY] * n, out_shape=[S(g.shape, g.dtype) for g in gs],
        input_output_aliases={t: t for t in range(n)},
        scratch_shapes=[pltpu.SemaphoreType.DMA((n,)), pltpu.SemaphoreType.DMA((n,))],
        compiler_params=pltpu.CompilerParams(has_side_effects=True),
    )(*gs)


def _all_reduce_small(xs, dep=None):
    M = xs.shape[0]
    deps = [] if dep is None else [dep]

    def body(x_ref, *rest):
        tot_ref, out_ref, send_sems, recv_sems, local_sem = rest[len(deps):]
        x, y, c, me, chips, chip_ids = _where_am_i()
        sib = (x, y, 1 - c)

        def rows(dev):
            return out_ref.at[pl.ds(_mo((4 * dev[0] + 2 * dev[1] + dev[2]) * M, 8), M), :]

        def copy(k, block, to, src=None):
            return _remote(rows(block) if src is None else src, rows(block), send_sems.at[k], recv_sems.at[k], to)

        mine = pltpu.make_async_copy(x_ref, rows((x, y, c)), local_sem)
        mine.start()
        first = [copy(0, (x, y, c), sib, src=x_ref)]
        first += [copy(1 + j, (x, y, c), (*chip, c), src=x_ref) for j, chip in enumerate(chips)]
        for cp in first:
            cp.start()
        passed = [copy(4 + j, (*chip, c), sib) for j, chip in enumerate(chips)]
        for j, chip in enumerate(chips):
            copy(1 + j, (*chip, c), (x, y, c)).wait_recv()
            passed[j].start()
        copy(0, sib, (x, y, c)).wait_recv()
        for j, chip in enumerate(chips):
            copy(4 + j, (*chip, 1 - c), (x, y, c)).wait_recv()
        for cp in first + passed:
            cp.wait_send()
        mine.wait()
        tot = out_ref[pl.ds(0, M), :]
        for d in range(1, 8):
            tot = tot + out_ref[pl.ds(d * M, M), :]
        tot_ref[...] = tot

    vm = pl.BlockSpec(memory_space=pltpu.VMEM)
    return _call(
        body, "all_reduce_small", in_specs=[vm] + [_ANY] * len(deps), out_specs=[vm, vm],
        out_shape=[S((M, 128), _F32), S((8 * M, 128), _F32)],
        scratch_shapes=[pltpu.SemaphoreType.DMA((7,)), pltpu.SemaphoreType.DMA((7,)), pltpu.SemaphoreType.DMA],
        compiler_params=_cp(has_side_effects=True),
    )(xs, *deps)[0]


def _reduce_scatter_begin(dws, shs, l):
    ssem, rsem, dww, lands, token = _rs_pair_start([d[1] for d in dws], shs, f"rs_pair_start_{l}")
    return ([d[0] for d in dws], dww, lands, ssem, rsem), token


def _reduce_scatter_middle(state, after, shs, idx, l):
    dw32s, dww, lands, ssem, rsem = state
    recv_a = _rs_pair_wait(dww, lands, ssem, rsem, after, shs, f"rs_pair_wait_{l}")
    pws = [_rs_pair_add(d32, ra, sh, idx) for d32, ra, sh in zip(dw32s, recv_a, shs)]
    ssem, rsem, pws, lands, token = _rs_chip_start(pws, shs, f"rs_chip_start_{l}")
    return (dw32s, recv_a, pws, lands, ssem, rsem), token


def _reduce_scatter_end(state, after, tensors, shs, gstack, idx, l):
    dw32s, recv_a, pws, lands, ssem, rsem = state
    recv_b = _rs_chip_wait(pws, lands, ssem, rsem, after, shs, f"rs_chip_wait_{l}")
    gs = [_rs_chip_add(d32, ra, rb, sh, idx, gstack[name], i, L)
          for d32, ra, rb, sh, (name, i, L) in zip(dw32s, recv_a, recv_b, shs, tensors)]
    gs = _rs_pair_share(gs, [i for _, i, _ in tensors], shs, "rs_pair_share")
    for (name, _, _), g in zip(tensors, gs):
        gstack[name] = g


def _pack(parts):
    out = []
    for p in parts:
        p2 = p.reshape(-1, 128)
        pad = (-p2.shape[0]) % 8
        out.append(jnp.pad(p2, ((0, pad), (0, 0))) if pad else p2)
    return jnp.concatenate(out, axis=0)


def _unpack(packed, like):
    out, at = [], 0
    for p in like:
        n = p.size // 128
        out.append(packed[at:at + n].reshape(p.shape))
        at += n + ((-n) % 8)
    return out


def kernel(x, mem, g_mix, g_ffn, w_in_a, g_v_a, w_spatial, b_spatial, w_in_b, g_q_b, g_k_b, g_mem, w_mem_kv, g_mq, g_mk, w_out, w_gate_up, w_down, loss_target, m_g_mix, m_g_ffn, m_w_in_a, m_g_v_a, m_w_spatial, m_b_spatial, m_w_in_b, m_g_q_b, m_g_k_b, m_g_mem, m_w_mem_kv, m_g_mq, m_g_mk, m_w_out, m_w_gate_up, m_w_down, v_g_mix, v_g_ffn, v_w_in_a, v_g_v_a, v_w_spatial, v_b_spatial, v_w_in_b, v_g_q_b, v_g_k_b, v_g_mem, v_w_mem_kv, v_g_mq, v_g_mk, v_w_out, v_w_gate_up, v_w_down):
    xs = x[0]
    mem2 = mem[0]
    target = loss_target[0]
    T, D = xs.shape
    depth = g_mix.shape[0]
    MEMW = w_mem_kv.shape[2] // 2
    TOK = D - MEMW
    KV = (w_in_b.shape[2] * 4 - TOK - MEMW) // 2
    QPK = TOK // KV
    F = w_gate_up.shape[2] * 4 // 2

    idx = ((2 * lax.axis_index("x") + lax.axis_index("y")).astype(jnp.int32).reshape(1), lax.axis_index("c").astype(jnp.int32).reshape(1))

    big = {
        "w_in_a": (w_in_a, _Shard(D, w_in_a.shape[2] * 4, True)),
        "w_in_b": (w_in_b, _Shard(D, w_in_b.shape[2] * 4, True)),
        "w_mem_kv": (w_mem_kv, _Shard(D, 2 * MEMW, False)),
        "w_out": (w_out, _Shard(D, D, False)),
        "w_gate_up": (w_gate_up, _Shard(D, 2 * F, True)),
        "w_down": (w_down, _Shard(F, D, False)),
    }

    def layer_tensors(l):
        n_in = "w_in_a" if l % 2 == 0 else "w_in_b"
        return [(n_in, l // 2, big[n_in][0].shape[0])] + [(n, l, depth) for n in ("w_mem_kv", "w_out", "w_gate_up", "w_down")]

    def layer_shards(l):
        return [big[n][1] for n, _, _ in layer_tensors(l)]

    full = {n: [None] * w.shape[0] for n, (w, _) in big.items()}
    flying = {}

    def start_layer(l, dep):
        tens = layer_tensors(l)
        token = dep
        for gi, group in enumerate([tens[:2], tens[2:3], tens[3:4], tens[4:]]):
            shs = [big[n][1] for n, _, _ in group]
            bufs = [_cast_into_full(big[n][0], i, big[n][1], idx, dep=token) for n, i, _ in group]
            ssem, rsem, bufs, token = _gather_start(bufs, shs, f"gather_start_{l}_{gi}")
            for n, i, _ in group:
                flying[(n, i)] = dict(group=group, shs=shs, state=(ssem, rsem, bufs), name=f"{l}_{gi}", passing=False)
        return token

    def land(members, after):
        for n, i in members:
            fl = flying[(n, i)]
            ssem, rsem, bufs = fl["state"]
            bufs = _gather_wait(bufs, ssem, rsem, after, fl["shs"], "gather_wait_" + fl["name"])
            ssem, rsem, bufs, after = _pass_on_start(bufs, fl["shs"], "pass_on_start_" + fl["name"])
            fl.update(state=(ssem, rsem, bufs), passing=True)
        return after

    def weight(n, i, after):
        if full[n][i] is None:
            fl = flying[(n, i)]
            ssem, rsem, bufs = fl["state"]
            if fl["passing"]:
                bufs = _pass_on_wait(bufs, ssem, rsem, after, fl["shs"], "pass_on_wait_" + fl["name"])
            else:
                bufs = _gather_wait(bufs, ssem, rsem, after, fl["shs"], "gather_wait_" + fl["name"])
                bufs = _gather_pass_on(bufs, fl["shs"], "gather_pass_on")
            for (m, j, _), b in zip(fl["group"], bufs):
                full[m][j] = b
        return full[n][i]

    after = None
    for l in range(depth):
        after = start_layer(l, after)
    tabs = _rope_tables(T)

    saved = []
    xc = xs
    for l in range(depth):
        is_a = l % 2 == 0
        li = l // 2
        w_in = weight("w_in_a" if is_a else "w_in_b", li, after)
        h = _rmsnorm_fwd(xc, g_mix[l], "rmsnorm_fwd")
        z = _mm_nn("mm_in", h, w_in, pm=2048, pn=512)
        st = dict(x=xc, h=h, z=z)
        if is_a:
            ws_m = w_spatial[li].astype(_MXU)
            st["ws_m"], st["wst_m"], st["b_t"] = ws_m, jnp.swapaxes(ws_m, 1, 2), b_spatial[li].T
            tok = _mixer_a_fwd(z, g_v_a[li], ws_m, st["b_t"], TOK)
            qblk = 2 * TOK // MEMW
        else:
            q, k, v = _qk_rope_fwd(z, g_q_b[li], g_k_b[li], tabs, TOK, KV)
            tok, stat = _attn_fwd(q, k, v, QPK)
            st["q"], st["k"], st["v"], st["stat"] = q, k, v, stat
            qblk = (TOK + 2 * KV) // MEMW
        mem_n = _rmsnorm_fwd(mem2, g_mem[l], "rmsnorm_mem")
        kv = _mm_nn("mm_memkv", mem_n, weight("w_mem_kv", l, z))
        mo = _mem_fwd(z, qblk, kv, g_mq[l], g_mk[l], MEMW)
        cat = jnp.concatenate([tok, mo], axis=1)
        token = land([("w_out", l), ("w_gate_up", l), ("w_down", l)], cat) if l > 0 else None
        x1 = _mm_nn("mm_out", cat, weight("w_out", l, cat), add=xc, dep=token)
        h2 = _rmsnorm_fwd(x1, g_ffn[l], "rmsnorm_fwd")
        act, gu = _ffn_gate_up(h2, weight("w_gate_up", l, h2))
        w_down_l = weight("w_down", l, act)
        token = land([layer_tensors(l + 1)[0][:2]], act) if l + 1 < depth else None
        xc = _mm_nn("mm_down", act, w_down_l, add=x1, pm=512, pn=512, pk=8192, dep=token)
        after = xc
        st.update(mem_n=mem_n, kv=kv, qblk=qblk, cat=cat, x1=x1, h2=h2, act=act, gu=gu)
        saved.append(st)

    dx, dxm, sq = _loss_head(xc, target)
    loss = lax.psum(sq[0, 0] * (0.5 / D), ("x", "y", "c"))

    gsm = {n: [None] * len(a) for n, a in dict(g_mix=g_mix, g_ffn=g_ffn, g_v_a=g_v_a, w_spatial=w_spatial, b_spatial=b_spatial,
                                                g_q_b=g_q_b, g_k_b=g_k_b, g_mem=g_mem, g_mq=g_mq, g_mk=g_mk).items()}
    gstack = {n: None for n in big}
    pairing, chipping, token = None, None, None

    def advance(after):
        nonlocal pairing, chipping
        state, tok = _reduce_scatter_middle(pairing[0], after, layer_shards(pairing[1]), idx, pairing[1])
        if chipping is not None:
            _reduce_scatter_end(chipping[0], tok, layer_tensors(chipping[1]), layer_shards(chipping[1]), gstack, idx, chipping[1])
        pairing, chipping = None, (state, pairing[1])
        return tok

    for l in reversed(range(depth)):
        st = saved[l]
        is_a = l % 2 == 0
        li = l // 2
        gbig = {}
        dgu = _ffn_dact(dxm, full["w_down"][l], st["gu"], dep=token)
        token = advance(dgu) if pairing is not None else None
        gbig["w_down"] = _mm_tn_dual("mm_dw_down", st["act"], dxm, pm=1408)
        dh2 = _ffn_dh(dgu, full["w_gate_up"][l], dep=token)
        gbig["w_gate_up"] = _ffn_dwgu(st["h2"], dgu)
        dx, dxm, dg = _rmsnorm_bwd(st["x1"], g_ffn[l], dh2, dx, "rmsnorm_bwd")
        gsm["g_ffn"][l] = dg[0]
        dcat = _mm_nt("mm_dcat", dxm, full["w_out"][l])
        gbig["w_out"] = _mm_tn_dual("mm_dw_out", st["cat"], dxm)
        dzq, dkn, dvm, dgq = _mem_bwd(st["z"], st["qblk"], st["kv"], g_mq[l], g_mk[l], dcat, TOK // MEMW, MEMW)
        dkv, dgk = _memkv_bwd(st["kv"], dkn, dvm, g_mk[l], MEMW)
        gsm["g_mq"][l], gsm["g_mk"][l] = dgq[0], dgk[0]
        gbig["w_mem_kv"] = _mm_tn_dual("mm_dw_memkv", st["mem_n"], dkv)
        dmem_n = _mm_nt("mm_dmemn", dkv, full["w_mem_kv"][l])
        gsm["g_mem"][l] = _rmsnorm_bwd(mem2, g_mem[l], dmem_n, None, "rmsnorm_bwd_mem")[2][0]
        if is_a:
            dz_tok, dws, dbs, dgv = _mixer_a_bwd(st["z"], dcat, g_v_a[li], st["ws_m"], st["wst_m"], st["b_t"], TOK)
            gsm["w_spatial"][li], gsm["b_spatial"][li], gsm["g_v_a"][li] = dws, dbs[:, :, 0], dgv[0]
            dz = jnp.concatenate([dz_tok, dzq], axis=1)
        else:
            dq, dk, dv = _attn_bwd(st["q"], st["k"], st["v"], dcat, st["cat"], st["stat"], QPK)
            dz_qk, dgq_b, dgk_b = _qk_rope_bwd(st["z"], dq, dk, g_q_b[li], g_k_b[li], tabs, TOK, KV)
            gsm["g_q_b"][li], gsm["g_k_b"][li] = dgq_b[0], dgk_b[0]
            dz = jnp.concatenate([dz_qk, dv, dzq], axis=1)
        n_in = "w_in_a" if is_a else "w_in_b"
        dh = _mm_nt("mm_dh", dz, full[n_in][li])
        gbig[n_in] = _mm_tn_dual("mm_dw_in", st["h"], dz, pm=2048, pn=512)
        dx, dxm, dg = _rmsnorm_bwd(st["x"], g_mix[l], dh, dx, "rmsnorm_bwd")
        gsm["g_mix"][l] = dg[0]
        state, token = _reduce_scatter_begin([gbig[n] for n, _, _ in layer_tensors(l)], layer_shards(l), l)
        pairing = (state, l)

    small = ["g_mix", "g_ffn", "g_v_a", "w_spatial", "b_spatial", "g_q_b", "g_k_b", "g_mem", "g_mq", "g_mk"]
    env = dict(g_mix=g_mix, g_ffn=g_ffn, g_v_a=g_v_a, w_spatial=w_spatial, b_spatial=b_spatial, g_q_b=g_q_b, g_k_b=g_k_b,
               g_mem=g_mem, g_mq=g_mq, g_mk=g_mk,
               m_g_mix=m_g_mix, m_g_ffn=m_g_ffn, m_g_v_a=m_g_v_a, m_w_spatial=m_w_spatial, m_b_spatial=m_b_spatial,
               m_g_q_b=m_g_q_b, m_g_k_b=m_g_k_b, m_g_mem=m_g_mem, m_g_mq=m_g_mq, m_g_mk=m_g_mk,
               v_g_mix=v_g_mix, v_g_ffn=v_g_ffn, v_g_v_a=v_g_v_a, v_w_spatial=v_w_spatial, v_b_spatial=v_b_spatial,
               v_g_q_b=v_g_q_b, v_g_k_b=v_g_k_b, v_g_mem=v_g_mem, v_g_mq=v_g_mq, v_g_mk=v_g_mk,
               m_w_in_a=m_w_in_a, m_w_in_b=m_w_in_b, m_w_mem_kv=m_w_mem_kv, m_w_out=m_w_out, m_w_gate_up=m_w_gate_up, m_w_down=m_w_down,
               v_w_in_a=v_w_in_a, v_w_in_b=v_w_in_b, v_w_mem_kv=v_w_mem_kv, v_w_out=v_w_out, v_w_gate_up=v_w_gate_up, v_w_down=v_w_down)
    like = [env[n] for n in small]
    g_small = _all_reduce_small(_pack([jnp.stack(gsm[n]) for n in small]), dep=token)

    res = {}
    outs = _adamw(_pack(like)[None], g_small[None], _pack([env["m_" + n] for n in small])[None],
                  _pack([env["v_" + n] for n in small])[None], "adamw_small")
    unpacked = [_unpack(o[0], like) for o in outs]
    for k, n in enumerate(small):
        res[n] = [u[k] for u in unpacked]
    advance(outs[1])
    pending = chipping
    last = {n: i for n, i, _ in layer_tensors(pending[1])}
    early = {}
    for n, (w, _) in big.items():
        L = w.shape[0]
        if n not in last:
            res[n] = _adamw(w, gstack[n], env["m_" + n], env["v_" + n], "adamw_" + n)
        elif L > 1:
            assert last[n] == 0
            early[n] = _adamw(w, gstack[n], env["m_" + n], env["v_" + n], "adamw_early_" + n, l0=1)
    done = [o[1] for o in early.values()] + [res[n][1] for n in big if n in res] + [res[small[0]][1]]
    _reduce_scatter_end(pending[0], done, layer_tensors(pending[1]), layer_shards(pending[1]), gstack, idx, pending[1])
    for n in last:
        res[n] = _adamw(big[n][0], gstack[n], env["m_" + n], env["v_" + n], "adamw_last_" + n, l0=0, l1=1, prev=early.get(n))

    order = ["g_mix", "g_ffn", "w_in_a", "g_v_a", "w_spatial", "b_spatial", "w_in_b", "g_q_b", "g_k_b", "g_mem", "w_mem_kv",
             "g_mq", "g_mk", "w_out", "w_gate_up", "w_down"]
    return (loss, dx.reshape(1, T, D), *[res[n][0] for n in order], *[res[n][1] for n in order],
            *[res[n][2] for n in order], *[res[n][3] for n in order])
```

```python
import jax
import jax.numpy as jnp
import numpy as np
from jax import lax
from jax.experimental import pallas as pl
from jax.experimental.pallas import tpu as pltpu

_F32 = jnp.float32
_MXU = jnp.bfloat16
_WIRE = jnp.bfloat16
_KW = {}

EPS = 1e-6
HEAD = 128
CHUNK = 128
GRID_W = 64
ROPE_THETA = 10000.0
ADAM_LR, ADAM_B1, ADAM_B2, ADAM_EPS, ADAM_WD, ADAM_STEP = 0.001, 0.9, 0.999, 1e-08, 0.01, 10
_SQRT_HALF = float(np.sqrt(0.5))
_INV_SQRT_2PI = float(1.0 / np.sqrt(2.0 * np.pi))
_VMEM_LIMIT = 56 * 1024 * 1024
_MESH = pl.DeviceIdType.MESH

_NN = (((1,), (0,)), ((), ()))
_NT = (((1,), (1,)), ((), ()))
_TN = (((0,), (0,)), ((), ()))

S = jax.ShapeDtypeStruct
BS = pl.BlockSpec
_ANY = pl.BlockSpec(memory_space=pl.ANY)


def _tile(n, pref, mult=128):
    if n <= pref:
        return n
    d = (pref // mult) * mult
    while d >= mult:
        if n % d == 0:
            return d
        d -= mult
    raise ValueError(f"no tile for {n} (pref {pref}, mult {mult})")


def _mo(v, m):
    return v if isinstance(v, int) else pl.multiple_of(v, m)


def _cp(*sem, **kw):
    return pltpu.CompilerParams(dimension_semantics=sem or None, vmem_limit_bytes=_VMEM_LIMIT, **kw)


def _call(body, name, **kw):
    return pl.pallas_call(body, name=name, **kw, **_KW)


def _dot(a, b, dn=_NN):
    return lax.dot_general(a, b, dn, preferred_element_type=_F32)


def _gelu(x):
    return 0.5 * x * (1.0 + lax.erf(x * _SQRT_HALF))


def _gelu_grad(x):
    return 0.5 * (1.0 + lax.erf(x * _SQRT_HALF)) + x * jnp.exp(-0.5 * x * x) * _INV_SQRT_2PI


def _rstd(x):
    return lax.rsqrt(jnp.mean(x * x, axis=-1, keepdims=True) + EPS)


def _norm_bwd(dout, xhat, r, g):
    dy = dout * g
    return r * (dy - xhat * jnp.mean(dy * xhat, axis=-1, keepdims=True))


def _softmax(s):
    e = jnp.exp(s - jnp.max(s, axis=-1, keepdims=True))
    return e * (1.0 / jnp.sum(e, axis=-1, keepdims=True))


def _mm(name, a, b, a_spec, b_spec, dn, grid, acc_shape, out_shape, out_specs, epilogue, extra=(), extra_specs=(), dep=None):
    nk = grid[2]
    n_ex = len(extra)
    deps = [] if dep is None else [dep]
    multi = isinstance(out_shape, (list, tuple))
    n_out = len(out_shape) if multi else 1

    def body(*refs):
        a_ref, b_ref = refs[0], refs[1]
        ex = refs[2:2 + n_ex]
        outs = refs[2 + n_ex + len(deps):2 + n_ex + len(deps) + n_out]

        def prod():
            return _dot(a_ref[...].astype(_MXU), b_ref[...].astype(_MXU), dn)

        if nk == 1:
            epilogue(prod(), ex, outs)
        else:
            acc = refs[-1]
            k = pl.program_id(2)

            @pl.when(k == 0)
            def _():
                acc[...] = jnp.zeros_like(acc)

            acc[...] += prod()

            @pl.when(k == nk - 1)
            def _():
                epilogue(acc[...], ex, outs)

    return _call(
        body, name, grid=grid, in_specs=[a_spec, b_spec, *extra_specs] + [_ANY] * len(deps), out_specs=out_specs, out_shape=out_shape,
        scratch_shapes=[] if nk == 1 else [pltpu.VMEM(acc_shape, _F32)],
        compiler_params=_cp("parallel", "parallel", "arbitrary"),
    )(a, b, *extra, *deps)


def _ep_store(acc, ex, outs):
    for o in outs:
        o[...] = acc.astype(o.dtype)


def _ep_add(acc, ex, outs):
    outs[0][...] = (acc + ex[0][...]).astype(outs[0].dtype)


def _mm_nn(name, a, b, out_dtype=_F32, add=None, pm=1024, pn=1024, pk=2048, dep=None):
    M, K = a.shape
    N = b.shape[1]
    tm, tn, tk = _tile(M, pm, 8), _tile(N, pn), _tile(K, pk)
    o_spec = BS((tm, tn), lambda i, j, k: (i, j))
    return _mm(name, a, b, BS((tm, tk), lambda i, j, k: (i, k)), BS((tk, tn), lambda i, j, k: (k, j)), _NN,
               (M // tm, N // tn, K // tk), (tm, tn), S((M, N), out_dtype), o_spec,
               _ep_store if add is None else _ep_add,
               extra=() if add is None else (add,), extra_specs=() if add is None else (o_spec,), dep=dep)


def _mm_nt(name, a, b, out_dtype=_F32, pm=1024, pn=1024, pk=4096):
    M, K = a.shape
    N = b.shape[0]
    tm, tn, tk = _tile(M, pm, 8), _tile(N, pn), _tile(K, pk)
    return _mm(name, a, b, BS((tm, tk), lambda i, j, k: (i, k)), BS((tn, tk), lambda i, j, k: (j, k)), _NT,
               (M // tm, N // tn, K // tk), (tm, tn), S((M, N), out_dtype), BS((tm, tn), lambda i, j, k: (i, j)), _ep_store)


def _mm_tn_dual(name, a, b, pm=1024, pn=1024, pk=2048, dep=None):
    K, M = a.shape
    N = b.shape[1]
    tm, tn, tk = _tile(M, pm), _tile(N, pn), _tile(K, pk, 16)
    o_spec = BS((tm, tn), lambda i, j, k: (i, j))
    return _mm(name, a, b, BS((tk, tm), lambda i, j, k: (k, i)), BS((tk, tn), lambda i, j, k: (k, j)), _TN,
               (M // tm, N // tn, K // tk), (tm, tn), [S((M, N), _F32), S((M, N), _WIRE)], [o_spec, o_spec], _ep_store, dep=dep)


def _ffn_gate_up(h2, wgu):
    T, D = h2.shape
    F = wgu.shape[1] // 2
    tm, tn = _tile(T, 1024, 8), _tile(F, 512)
    nj = F // tn

    def body(a_ref, bg_ref, bu_ref, act_ref, gu_ref):
        a = a_ref[...]
        g = _dot(a, bg_ref[...])
        u = _dot(a, bu_ref[...])
        sg = 1.0 / (1.0 + jnp.exp(-g))
        silu = g * sg
        gu_ref[0] = (u * (sg * (1.0 + g * (1.0 - sg)))).astype(gu_ref.dtype)
        gu_ref[1] = silu.astype(gu_ref.dtype)
        act_ref[...] = (silu * u).astype(act_ref.dtype)

    return _call(
        body, "ffn_gate_up", grid=(T // tm, nj),
        in_specs=[BS((tm, D), lambda i, j: (i, 0)), BS((D, tn), lambda i, j: (0, j)), BS((D, tn), lambda i, j: (0, j + nj))],
        out_specs=[BS((tm, tn), lambda i, j: (i, j)), BS((2, tm, tn), lambda i, j: (0, i, j))],
        out_shape=[S((T, F), _MXU), S((2, T, F), _MXU)],
        compiler_params=_cp("parallel", "parallel"),
    )(h2, wgu, wgu)


def _ffn_dact(dxm, wdown, gu, dep=None):
    T, D = dxm.shape
    F = wdown.shape[0]
    tm, tn = _tile(T, 2048, 8), _tile(F, 512)
    deps = [] if dep is None else [dep]

    def body(a_ref, b_ref, gu_ref, *rest):
        o_ref = rest[-1]
        d = _dot(a_ref[...], b_ref[...], _NT)
        o_ref[0] = (d * gu_ref[0].astype(_F32)).astype(o_ref.dtype)
        o_ref[1] = (d * gu_ref[1].astype(_F32)).astype(o_ref.dtype)

    return _call(
        body, "ffn_dact", grid=(T // tm, F // tn),
        in_specs=[BS((tm, D), lambda i, j: (i, 0)), BS((tn, D), lambda i, j: (j, 0)), BS((2, tm, tn), lambda i, j: (0, i, j))]
        + [_ANY] * len(deps),
        out_specs=BS((2, tm, tn), lambda i, j: (0, i, j)), out_shape=S((2, T, F), _MXU),
        compiler_params=_cp("parallel", "parallel"),
    )(dxm, wdown, gu, *deps)


def _ffn_dh(dgu, wgu, dep=None):
    _, T, F = dgu.shape
    D = wgu.shape[0]
    tm, tn, tk = _tile(T, 512, 8), _tile(D, 2048), _tile(F, 2816, 256)
    nkf = F // tk
    return _mm("ffn_dh", dgu, wgu, BS((None, tm, tk), lambda i, j, k: (k // nkf, i, k % nkf)),
               BS((tn, tk), lambda i, j, k: (j, k)), _NT, (T // tm, D // tn, 2 * nkf), (tm, tn),
               S((T, D), _F32), BS((tm, tn), lambda i, j, k: (i, j)), _ep_store, dep=dep)


def _ffn_dwgu(h2t, dgu):
    _, T, F = dgu.shape
    D = h2t.shape[0]
    tm, tn, tk = _tile(D, 1024), _tile(F, 512), _tile(T, 4096)
    njf = F // tn
    o_spec = BS((tm, tn), lambda i, j, k: (i, j))
    return _mm("ffn_dwgu", h2t, dgu, BS((tm, tk), lambda i, j, k: (i, k)),
               BS((None, tk, tn), lambda i, j, k: (j // njf, k, j % njf)), _NN, (D // tm, 2 * njf, T // tk), (tm, tn),
               [S((D, 2 * F), _F32), S((D, 2 * F), _WIRE)], [o_spec, o_spec], _ep_store)


def _mm_nn_dual(name, a, b, pm=1024, pn=1024, pk=2048):
    M, K = a.shape
    N = b.shape[1]
    tm, tn, tk = _tile(M, pm), _tile(N, pn), _tile(K, pk)
    o_spec = BS((tm, tn), lambda i, j, k: (i, j))
    return _mm(name, a, b, BS((tm, tk), lambda i, j, k: (i, k)), BS((tk, tn), lambda i, j, k: (k, j)), _NN,
               (M // tm, N // tn, K // tk), (tm, tn), [S((M, N), _F32), S((M, N), _WIRE)], [o_spec, o_spec], _ep_store)


def _rmsnorm_fwd(x, g, name, dep=None, transposed=False):
    T, D = x.shape
    tr = _tile(T, 512, 128)
    n_out = 2 if transposed else 1

    def body(x_ref, g_ref, *rest):
        outs = rest[-n_out:]
        xv = x_ref[...]
        h = (xv * _rstd(xv) * g_ref[...]).astype(outs[0].dtype)
        outs[0][...] = h
        if transposed:
            outs[1][...] = h.T

    row = BS((tr, D), lambda i: (i, 0))
    deps = [] if dep is None else [dep]
    return _call(body, name, grid=(T // tr,), in_specs=[row, BS((1, D), lambda i: (0, 0))] + [_ANY] * len(deps),
                 out_specs=[row, BS((D, tr), lambda i: (0, i))] if transposed else row,
                 out_shape=[S((T, D), _MXU), S((D, T), _MXU)] if transposed else S((T, D), _MXU),
                 compiler_params=_cp("parallel"))(x, g.reshape(1, D), *deps)


def _rmsnorm_bwd(x, g, dh, dres, name):
    T, D = x.shape
    tr = _tile(T, 256, 8)
    has_res = dres is not None

    def body(*refs):
        x_ref, g_ref, dh_ref = refs[:3]
        dx_ref, dxm_ref, dg_ref = refs[-3:]

        @pl.when(pl.program_id(0) == 0)
        def _():
            dg_ref[...] = jnp.zeros_like(dg_ref)

        xv = x_ref[...]
        r = _rstd(xv)
        xhat = xv * r
        dh_v = dh_ref[...]
        dg_ref[...] += jnp.sum(dh_v * xhat, axis=0, keepdims=True)
        dx = _norm_bwd(dh_v, xhat, r, g_ref[...])
        if has_res:
            dx = dx + refs[3][...]
        dx_ref[...] = dx
        dxm_ref[...] = dx.astype(dxm_ref.dtype)

    row = BS((tr, D), lambda i: (i, 0))
    vec = BS((1, D), lambda i: (0, 0))
    return _call(body, name, grid=(T // tr,), in_specs=[row, vec, row] + ([row] if has_res else []),
                 out_specs=[row, row, vec], out_shape=[S((T, D), _F32), S((T, D), _MXU), S((1, D), _F32)],
                 compiler_params=_cp("arbitrary"))(x, g.reshape(1, D), dh, *([dres] if has_res else []))


def _loss_head(y, target):
    T, D = y.shape
    tr = _tile(T, 256, 8)

    def body(y_ref, t_ref, dy_ref, dym_ref, acc_ref):
        @pl.when(pl.program_id(0) == 0)
        def _():
            acc_ref[...] = jnp.zeros_like(acc_ref)

        err = y_ref[...] - t_ref[...]
        acc_ref[...] += jnp.sum(jnp.sum(err * err, axis=-1, keepdims=True), axis=0, keepdims=True)
        dy = err * (1.0 / D)
        dy_ref[...] = dy
        dym_ref[...] = dy.astype(dym_ref.dtype)

    row = BS((tr, D), lambda i: (i, 0))
    return _call(body, "loss_head", grid=(T // tr,), in_specs=[row, row],
                 out_specs=[row, row, BS((1, 128), lambda i: (0, 0))],
                 out_shape=[S((T, D), _F32), S((T, D), _MXU), S((1, 128), _F32)],
                 compiler_params=_cp("arbitrary"))(y, target)


def _mixa_blocks(T):
    return 2 if T % (2 * CHUNK) == 0 else 1


def _mixer_a_fwd(z, gv, ws_m, b_t, TOK):
    T = z.shape[0]
    G = TOK // HEAD
    CB = _mixa_blocks(T)
    R = CB * CHUNK

    def body(z_ref, gv_ref, ws_ref, bt_ref, o_ref):
        u = _gelu(z_ref[:, :TOK])
        v = _gelu(z_ref[:, TOK:])
        vn = (v * _rstd(v) * gv_ref[...]).astype(_MXU)
        for c in range(CB):
            rows = slice(c * CHUNK, (c + 1) * CHUNK)
            for g in range(G):
                cols = slice(g * HEAD, (g + 1) * HEAD)
                s = _dot(ws_ref[g], vn[rows, cols]) + bt_ref[:, g:g + 1]
                o_ref[rows, cols] = (u[rows, cols] * s).astype(o_ref.dtype)

    return _call(
        body, "mixer_a_fwd", grid=(T // R,),
        in_specs=[BS((R, 2 * TOK), lambda i: (i, 0)), BS((1, TOK), lambda i: (0, 0)),
                  BS((G, CHUNK, CHUNK), lambda i: (0, 0, 0)), BS((CHUNK, G), lambda i: (0, 0))],
        out_specs=BS((R, TOK), lambda i: (i, 0)), out_shape=S((T, TOK), _MXU), compiler_params=_cp("parallel"),
    )(z, gv.reshape(1, TOK), ws_m, b_t)


def _mixer_a_bwd(z, dcat, gv, ws_m, wst_m, b_t, TOK):
    T = z.shape[0]
    G = TOK // HEAD
    CB = _mixa_blocks(T)
    R = CB * CHUNK
    n = T // R

    def body(z_ref, d_ref, gv_ref, ws_ref, wst_ref, bt_ref, dz_ref, dws_ref, db_ref, dgv_ref, dvn_scr):
        i = pl.program_id(0)

        @pl.when(i == 0)
        def _():
            dws_ref[...] = jnp.zeros_like(dws_ref)
            db_ref[...] = jnp.zeros_like(db_ref)
            dgv_ref[...] = jnp.zeros_like(dgv_ref)

        zu = z_ref[:, :TOK]
        zv = z_ref[:, TOK:]
        u = _gelu(zu)
        v = _gelu(zv)
        r = _rstd(v)
        vhat = v * r
        gvv = gv_ref[...]
        vn = (vhat * gvv).astype(_MXU)
        d = d_ref[...]
        gpu = _gelu_grad(zu)
        for c in range(CB):
            rows = slice(c * CHUNK, (c + 1) * CHUNK)
            for g in range(G):
                cols = slice(g * HEAD, (g + 1) * HEAD)
                vn_cg = vn[rows, cols]
                s = _dot(ws_ref[g], vn_cg) + bt_ref[:, g:g + 1]
                d_cg = d[rows, cols]
                dz_ref[rows, cols] = (d_cg * s * gpu[rows, cols]).astype(dz_ref.dtype)
                ds = d_cg * u[rows, cols]
                ds_m = ds.astype(_MXU)
                dvn_scr[rows, cols] = _dot(wst_ref[g], ds_m)
                dws_ref[g] += _dot(ds_m, vn_cg, _NT)
                db_ref[g] += ds
        dvn = dvn_scr[...]
        dgv_ref[...] += jnp.sum(dvn * vhat, axis=0, keepdims=True)
        dv = _norm_bwd(dvn, vhat, r, gvv)
        dz_ref[:, TOK:] = (dv * _gelu_grad(zv)).astype(dz_ref.dtype)

        @pl.when(i == n - 1)
        def _():
            for g in range(G):
                db_ref[g] = jnp.broadcast_to(jnp.sum(db_ref[g], axis=1, keepdims=True), (CHUNK, CHUNK))

    full3 = BS((G, CHUNK, CHUNK), lambda i: (0, 0, 0))
    return _call(
        body, "mixer_a_bwd", grid=(n,),
        in_specs=[BS((R, 2 * TOK), lambda i: (i, 0)), BS((R, TOK), lambda i: (i, 0)), BS((1, TOK), lambda i: (0, 0)),
                  full3, full3, BS((CHUNK, G), lambda i: (0, 0))],
        out_specs=[BS((R, 2 * TOK), lambda i: (i, 0)), full3, full3, BS((1, TOK), lambda i: (0, 0))],
        out_shape=[S((T, 2 * TOK), _MXU), S((G, CHUNK, CHUNK), _F32), S((G, CHUNK, CHUNK), _F32), S((1, TOK), _F32)],
        scratch_shapes=[pltpu.VMEM((R, TOK), _F32)], compiler_params=_cp("arbitrary"),
    )(z, dcat, gv.reshape(1, TOK), ws_m, wst_m, b_t)


def _rope_tables(T):
    n_rows = T // GRID_W
    rows = jnp.broadcast_to(jnp.arange(n_rows)[:, None], (n_rows, GRID_W)).reshape(T)
    cols = jnp.broadcast_to(jnp.arange(GRID_W)[None, :], (n_rows, GRID_W)).reshape(T)
    pairs = HEAD // 4
    freqs = ROPE_THETA ** (-jnp.arange(pairs, dtype=_F32) / pairs)
    ang_r = rows.astype(_F32)[:, None] * freqs
    ang_c = cols.astype(_F32)[:, None] * freqs
    ang = jnp.concatenate([ang_r, ang_r, ang_c, ang_c], axis=-1)
    cos, sin = jnp.cos(ang), jnp.sin(ang)
    first = (jnp.arange(HEAD) % (HEAD // 2)) < (HEAD // 4)
    return cos, jnp.where(first, -sin, 0.0), jnp.where(first, 0.0, sin)


def _rope(x, cs, sa, sb):
    return x * cs + pltpu.roll(x, 96, 1) * sa + pltpu.roll(x, 32, 1) * sb


def _qk_rope_fwd(z, gq, gk, tabs, TOK, KV):
    T = z.shape[0]
    R = _tile(T, 512, 8)
    W = TOK + 2 * KV

    def body(z_ref, gq_ref, gk_ref, cos_ref, sa_ref, sb_ref, q_ref, k_ref, v_ref):
        cs, sa, sb = cos_ref[...], sa_ref[...], sb_ref[...]
        for h in range((TOK + KV) // HEAD):
            cols = slice(h * HEAD, (h + 1) * HEAD)
            xv = z_ref[:, cols]
            xn = xv * _rstd(xv) * (gq_ref[...] if h < TOK // HEAD else gk_ref[...])
            out = _rope(xn, cs, sa, sb)
            if h < TOK // HEAD:
                q_ref[:, cols] = out.astype(q_ref.dtype)
            else:
                k_ref[:, h * HEAD - TOK:(h + 1) * HEAD - TOK] = out.astype(k_ref.dtype)
        v_ref[...] = z_ref[:, TOK + KV:].astype(v_ref.dtype)

    vec = BS((1, HEAD), lambda i: (0, 0))
    tab = BS((R, HEAD), lambda i: (i, 0))
    return _call(
        body, "qk_rope_fwd", grid=(T // R,), in_specs=[BS((R, W), lambda i: (i, 0)), vec, vec, tab, tab, tab],
        out_specs=[BS((R, TOK), lambda i: (i, 0)), BS((R, KV), lambda i: (i, 0)), BS((R, KV), lambda i: (i, 0))],
        out_shape=[S((T, TOK), _MXU), S((T, KV), _MXU), S((T, KV), _MXU)], compiler_params=_cp("parallel"),
    )(z, gq.reshape(1, HEAD), gk.reshape(1, HEAD), *tabs)


def _qk_rope_bwd(z, dq, dk, gq, gk, tabs, TOK, KV):
    T = z.shape[0]
    R = _tile(T, 512, 8)
    W = TOK + KV

    def body(z_ref, dq_ref, dk_ref, gq_ref, gk_ref, cos_ref, sa_ref, sb_ref, dz_ref, dgq_ref, dgk_ref):
        @pl.when(pl.program_id(0) == 0)
        def _():
            dgq_ref[...] = jnp.zeros_like(dgq_ref)
            dgk_ref[...] = jnp.zeros_like(dgk_ref)

        cs, sa, sb = cos_ref[...], sa_ref[...], sb_ref[...]
        for h in range(W // HEAD):
            cols = slice(h * HEAD, (h + 1) * HEAD)
            is_q = h < TOK // HEAD
            do = dq_ref[:, cols] if is_q else dk_ref[:, h * HEAD - TOK:(h + 1) * HEAD - TOK]
            dxn = do * cs - pltpu.roll(do, 96, 1) * sa - pltpu.roll(do, 32, 1) * sb
            xv = z_ref[:, cols]
            r = _rstd(xv)
            xhat = xv * r
            dg_ref = dgq_ref if is_q else dgk_ref
            dg_ref[...] += jnp.sum(dxn * xhat, axis=0, keepdims=True)
            dz_ref[:, cols] = _norm_bwd(dxn, xhat, r, gq_ref[...] if is_q else gk_ref[...]).astype(dz_ref.dtype)

    vec = BS((1, HEAD), lambda i: (0, 0))
    tab = BS((R, HEAD), lambda i: (i, 0))
    return _call(
        body, "qk_rope_bwd", grid=(T // R,),
        in_specs=[BS((R, W), lambda i: (i, 0)), BS((R, TOK), lambda i: (i, 0)), BS((R, KV), lambda i: (i, 0)), vec, vec, tab, tab, tab],
        out_specs=[BS((R, W), lambda i: (i, 0)), vec, vec],
        out_shape=[S((T, W), _MXU), S((1, HEAD), _F32), S((1, HEAD), _F32)], compiler_params=_cp("arbitrary"),
    )(z, dq, dk, gq.reshape(1, HEAD), gk.reshape(1, HEAD), *tabs)


_ATTN_C2 = float(HEAD ** -0.5 * np.log2(np.e))


def _attn_fwd(q, k, v, QPK):
    T, TOK = q.shape
    KVH = k.shape[1] // HEAD
    tq = _tile(T, 256, 8)
    W = QPK * HEAD

    def body(q_ref, k_ref, v_ref, o_ref, st_ref, vaug):
        @pl.when(pl.program_id(1) == 0)
        def _():
            vaug[:, :HEAD] = v_ref[...]
            vaug[:, HEAD:] = jnp.ones((T, HEAD), vaug.dtype)

        kk, va = k_ref[...], vaug[...]
        for g in range(QPK):
            cols = slice(g * HEAD, (g + 1) * HEAD)
            s = _dot(q_ref[:, cols], kk, _NT)
            m = jnp.max(s, axis=-1, keepdims=True)
            ov = _dot(jnp.exp2((s - m) * _ATTN_C2).astype(_MXU), va)
            l = ov[:, HEAD:HEAD + 1]
            o_ref[:, cols] = (ov[:, :HEAD] * (1.0 / l)).astype(o_ref.dtype)
            st_ref[:, g:g + 1] = m + jnp.log2(l) * (1.0 / _ATTN_C2)

    qs = BS((tq, W), lambda h, i: (i, h))
    ks = BS((T, HEAD), lambda h, i: (0, h))
    return _call(body, "attn_fwd", grid=(KVH, T // tq), in_specs=[qs, ks, ks],
                 out_specs=[qs, BS((None, tq, QPK), lambda h, i: (h, i, 0))],
                 out_shape=[S((T, TOK), _MXU), S((KVH, T, QPK), _F32)],
                 scratch_shapes=[pltpu.VMEM((T, 2 * HEAD), _MXU)],
                 compiler_params=_cp("parallel", "arbitrary"))(q, k, v)


def _attn_bwd(q, k, v, dcat, o, stat, QPK):
    T, TOK = q.shape
    KV = k.shape[1]
    KVH = KV // HEAD
    tq = _tile(T, 256, 8)
    nq = T // tq
    W = QPK * HEAD
    scale = HEAD ** -0.5

    def body(q_ref, k_ref, v_ref, do_ref, o_ref, st_ref, dq_ref, dk_ref, dv_ref, dk_acc, dv_acc, ds_all, p_all, q_all, do_all):
        i = pl.program_id(1)

        @pl.when(i == 0)
        def _():
            dk_acc[...] = jnp.zeros_like(dk_acc)
            dv_acc[...] = jnp.zeros_like(dv_acc)

        kk, vv = k_ref[...], v_ref[...]
        for g in range(QPK):
            cols = slice(g * HEAD, (g + 1) * HEAD)
            rows = slice(g * tq, (g + 1) * tq)
            qg = q_ref[:, cols]
            p = jnp.exp2((_dot(qg, kk, _NT) - st_ref[:, g:g + 1]) * _ATTN_C2)
            do32 = do_ref[:, cols]
            do = do32.astype(_MXU)
            delta = jnp.sum(do32 * o_ref[:, cols].astype(_F32), axis=-1, keepdims=True)
            ds = (p * (_dot(do, vv, _NT) - delta)).astype(_MXU)
            dq_ref[:, cols] = _dot(ds, kk) * scale
            ds_all[rows, :] = ds
            p_all[rows, :] = p.astype(_MXU)
            q_all[rows, :] = qg
            do_all[rows, :] = do
        dk_acc[...] += _dot(ds_all[...], q_all[...], _TN)
        dv_acc[...] += _dot(p_all[...], do_all[...], _TN)

        @pl.when(i == nq - 1)
        def _():
            dk_ref[...] = dk_acc[...] * scale
            dv_ref[...] = dv_acc[...].astype(dv_ref.dtype)

    qs = BS((tq, W), lambda h, i: (i, h))
    ks = BS((T, HEAD), lambda h, i: (0, h))
    return _call(
        body, "attn_bwd", grid=(KVH, nq), in_specs=[qs, ks, ks, qs, qs, BS((None, tq, QPK), lambda h, i: (h, i, 0))],
        out_specs=[qs, ks, ks], out_shape=[S((T, TOK), _F32), S((T, KV), _F32), S((T, KV), _MXU)],
        scratch_shapes=[pltpu.VMEM((T, HEAD), _F32), pltpu.VMEM((T, HEAD), _F32), pltpu.VMEM((QPK * tq, T), _MXU),
                        pltpu.VMEM((QPK * tq, T), _MXU), pltpu.VMEM((QPK * tq, HEAD), _MXU), pltpu.VMEM((QPK * tq, HEAD), _MXU)],
        compiler_params=_cp("parallel", "arbitrary"),
    )(q, k, v, dcat, o, stat)


def _mem_fwd(z, qblk, kv, gmq, gmk, MEMW):
    T = z.shape[0]
    NM = kv.shape[0]
    tq = _tile(T, 512, 8)
    scale = HEAD ** -0.5

    def body(q_ref, kv_ref, gq_ref, gk_ref, o_ref):
        for h in range(MEMW // HEAD):
            cols = slice(h * HEAD, (h + 1) * HEAD)
            kx = kv_ref[:, cols]
            kn = (kx * _rstd(kx) * gk_ref[...]).astype(_MXU)
            vv = kv_ref[:, MEMW + h * HEAD:MEMW + (h + 1) * HEAD].astype(_MXU)
            qx = q_ref[:, cols]
            qn = (qx * _rstd(qx) * gq_ref[...]).astype(_MXU)
            p = _softmax(_dot(qn, kn, _NT) * scale)
            o_ref[:, cols] = _dot(p.astype(_MXU), vv).astype(o_ref.dtype)

    vec = BS((1, HEAD), lambda i: (0, 0))
    return _call(
        body, "mem_fwd", grid=(T // tq,),
        in_specs=[BS((tq, MEMW), lambda i: (i, qblk)), BS((NM, 2 * MEMW), lambda i: (0, 0)), vec, vec],
        out_specs=BS((tq, MEMW), lambda i: (i, 0)), out_shape=S((T, MEMW), _MXU), compiler_params=_cp("parallel"),
    )(z, kv, gmq.reshape(1, HEAD), gmk.reshape(1, HEAD))


def _mem_bwd(z, qblk, kv, gmq, gmk, dcat, dblk, MEMW):
    T = z.shape[0]
    NM = kv.shape[0]
    tq = _tile(T, 512, 8)
    scale = HEAD ** -0.5

    def body(q_ref, kv_ref, gq_ref, gk_ref, do_ref, dz_ref, dkn_ref, dv_ref, dgq_ref):
        @pl.when(pl.program_id(0) == 0)
        def _():
            dkn_ref[...] = jnp.zeros_like(dkn_ref)
            dv_ref[...] = jnp.zeros_like(dv_ref)
            dgq_ref[...] = jnp.zeros_like(dgq_ref)

        for h in range(MEMW // HEAD):
            cols = slice(h * HEAD, (h + 1) * HEAD)
            kx = kv_ref[:, cols]
            kn = (kx * _rstd(kx) * gk_ref[...]).astype(_MXU)
            vv = kv_ref[:, MEMW + h * HEAD:MEMW + (h + 1) * HEAD].astype(_MXU)
            qx = q_ref[:, cols]
            rq = _rstd(qx)
            qhat = qx * rq
            qn = (qhat * gq_ref[...]).astype(_MXU)
            p = _softmax(_dot(qn, kn, _NT) * scale)
            do = do_ref[:, cols].astype(_MXU)
            dp = _dot(do, vv, _NT)
            ds = (p * (dp - jnp.sum(p * dp, axis=-1, keepdims=True)) * scale).astype(_MXU)
            dqn = _dot(ds, kn)
            dkn_ref[:, cols] += _dot(ds, qn, _TN)
            dv_ref[:, cols] += _dot(p.astype(_MXU), do, _TN)
            dgq_ref[...] += jnp.sum(dqn * qhat, axis=0, keepdims=True)
            dz_ref[:, cols] = _norm_bwd(dqn, qhat, rq, gq_ref[...]).astype(dz_ref.dtype)

    vec = BS((1, HEAD), lambda i: (0, 0))
    kvs = BS((NM, MEMW), lambda i: (0, 0))
    return _call(
        body, "mem_bwd", grid=(T // tq,),
        in_specs=[BS((tq, MEMW), lambda i: (i, qblk)), BS((NM, 2 * MEMW), lambda i: (0, 0)), vec, vec,
                  BS((tq, MEMW), lambda i: (i, dblk))],
        out_specs=[BS((tq, MEMW), lambda i: (i, 0)), kvs, kvs, vec],
        out_shape=[S((T, MEMW), _MXU), S((NM, MEMW), _F32), S((NM, MEMW), _F32), S((1, HEAD), _F32)],
        compiler_params=_cp("arbitrary"),
    )(z, kv, gmq.reshape(1, HEAD), gmk.reshape(1, HEAD), dcat)


def _memkv_bwd(kv, dkn, dv, gmk, MEMW):
    NM = kv.shape[0]

    def body(kv_ref, dkn_ref, dv_ref, gk_ref, dkv_ref, dgk_ref):
        dgk = jnp.zeros((1, HEAD), _F32)
        for h in range(MEMW // HEAD):
            cols = slice(h * HEAD, (h + 1) * HEAD)
            kx = kv_ref[:, cols]
            r = _rstd(kx)
            khat = kx * r
            dkn = dkn_ref[:, cols]
            dgk = dgk + jnp.sum(dkn * khat, axis=0, keepdims=True)
            dkv_ref[:, cols] = _norm_bwd(dkn, khat, r, gk_ref[...]).astype(dkv_ref.dtype)
        dgk_ref[...] = dgk
        dkv_ref[:, MEMW:] = dv_ref[...].astype(dkv_ref.dtype)

    return _call(body, "memkv_bwd", out_shape=[S((NM, 2 * MEMW), _MXU), S((1, HEAD), _F32)],
                 compiler_params=_cp())(kv, dkn, dv, gmk.reshape(1, HEAD))


def _cast_into_full(w, l, sh, idx, dep=None):
    tr, tc = _tile(sh.Rs, 512, 16), _tile(sh.Cs, 2048)
    nr, nc = sh.Rs // tr, sh.Cs // tc
    deps = [] if dep is None else [dep]

    def body(i_ref, c_ref, w_ref, *rest):
        rest[-1][...] = w_ref[...].astype(rest[-1].dtype)

    if sh.by_cols:
        o_map = lambda a, b, si, sc: (a, si[0] * nc + b)
    else:
        o_map = lambda a, b, si, sc: (si[0] * nr + a, b)
    return _call(
        body, "cast_into_full",
        grid_spec=pltpu.PrefetchScalarGridSpec(
            num_scalar_prefetch=2, grid=(nr, nc),
            in_specs=[BS((None, tr, tc), lambda a, b, si, sc: (l, a, b))] + [_ANY] * len(deps), out_specs=BS((tr, tc), o_map)),
        out_shape=S((sh.R, sh.C), _WIRE), compiler_params=_cp("parallel", "parallel"),
    )(*idx, w, *deps)


def _adamw(w, g, m, v, name, l0=0, l1=None, prev=None):
    L, R, C = w.shape
    l1 = L if l1 is None else l1
    tc = _tile(C, 2048)
    tr = _tile(R, max(8, (512 * 1024) // tc), 8)
    c_m = 1.0 / (1.0 - ADAM_B1 ** ADAM_STEP)
    c_v = 1.0 / (1.0 - ADAM_B2 ** ADAM_STEP)

    def body(w_ref, g_ref, m_ref, v_ref, *rest):
        go_ref, d_ref, mo_ref, vo_ref = rest[-4:]
        gv = g_ref[...]
        mn = ADAM_B1 * m_ref[...] + (1.0 - ADAM_B1) * gv
        vn = ADAM_B2 * v_ref[...] + (1.0 - ADAM_B2) * (gv * gv)
        go_ref[...] = gv
        mo_ref[...] = mn
        vo_ref[...] = vn
        d_ref[...] = -ADAM_LR * ((mn * c_m) / (jnp.sqrt(vn * c_v) + ADAM_EPS) + ADAM_WD * w_ref[...])

    blk = BS((None, tr, tc), lambda a, i, j: (l0 + a, i, j))
    prevs = [] if prev is None else list(prev)
    return _call(body, name, grid=(l1 - l0, R // tr, C // tc), in_specs=[blk] * 4 + [_ANY] * len(prevs), out_specs=[blk] * 4,
                 out_shape=[S((L, R, C), _F32)] * 4, input_output_aliases={4 + k: k for k in range(len(prevs))},
                 compiler_params=_cp("parallel", "parallel", "parallel"))(w, g, m, v, *prevs)


def _where_am_i():
    x, y, c = lax.axis_index("x"), lax.axis_index("y"), lax.axis_index("c")
    chips = [(1 - x, y), (x, 1 - y), (1 - x, 1 - y)]
    return x, y, c, 2 * x + y, chips, [2 * cx + cy for cx, cy in chips]


class _Shard:
    def __init__(self, R, C, by_cols):
        self.R, self.C, self.by_cols = R, C, by_cols
        self.Rs, self.Cs = (R, C // 4) if by_cols else (R // 4, C)
        self.Rh = self.Rs // 2
        self.Q = R // 8

    def full_piece(self, ref, j, cc):
        if self.by_cols:
            return ref.at[pl.ds(cc * self.Rh, self.Rh), pl.ds(_mo(j * self.Cs, 128), self.Cs)]
        return ref.at[pl.ds(_mo(j * self.Rs + cc * self.Rh, 16), self.Rh), :]

    def full_shard(self, ref, j):
        if self.by_cols:
            return ref.at[:, pl.ds(_mo(j * self.Cs, 128), self.Cs)]
        return ref.at[pl.ds(_mo(j * self.Rs, 16), self.Rs), :]

    def shard_half(self, ref, cc):
        return ref.at[pl.ds(_mo(cc * self.Rh, 16), self.Rh), :]

    def half_piece(self, ref, j):
        if self.by_cols:
            return ref.at[:, pl.ds(_mo(j * self.Cs, 128), self.Cs)]
        return ref.at[pl.ds(_mo(j * self.Rh, 16), self.Rh), :]


def _remote(src, dst, ssem, rsem, dev):
    return pltpu.make_async_remote_copy(src_ref=src, dst_ref=dst, send_sem=ssem, recv_sem=rsem, device_id=dev, device_id_type=_MESH)


_HBM = pl.BlockSpec(memory_space=pltpu.HBM)
_SEM = pl.BlockSpec(memory_space=pltpu.SEMAPHORE)
_EFFECT = pltpu.SideEffectType.DATAFLOW_SIDE_EFFECTING


def _in_hbm(a):
    return pltpu.with_memory_space_constraint(a, pltpu.HBM)


def _gather_start(fulls, shs, name):
    n = len(fulls)

    def body(*refs):
        bufs = refs[:n]
        send_sems, recv_sems = refs[n], refs[n + 1]
        token = refs[-1]
        x, y, c, me, chips, chip_ids = _where_am_i()
        for t in range(n):
            mine = shs[t].full_piece(bufs[t], me, c)
            for r in range(3):
                _remote(mine, mine, send_sems.at[3 * t + r], recv_sems.at[3 * t + r], (*chips[r], c)).start()
        token[...] = jnp.zeros_like(token)

    out = pl.pallas_call(
        body, name=name, in_specs=[_HBM] * n,
        out_shape=(pltpu.SemaphoreType.DMA((3 * n,)), pltpu.SemaphoreType.DMA((3 * n,)), *[pltpu.HBM(f.shape, f.dtype) for f in fulls],
                   S((8, 128), _F32)),
        out_specs=(_SEM, _SEM, *[_HBM] * n, pl.BlockSpec(memory_space=pltpu.VMEM)),
        input_output_aliases={t: 2 + t for t in range(n)},
        compiler_params=pltpu.CompilerParams(has_side_effects=_EFFECT), **_KW,
    )(*[_in_hbm(f) for f in fulls])
    return out[0], out[1], list(out[2:2 + n]), out[-1]


def _gather_wait(fulls, send_sems, recv_sems, after, shs, name):
    n = len(fulls)

    def body(*refs):
        bufs = refs[:n]
        ssem, rsem = refs[n], refs[n + 1]
        x, y, c, me, chips, chip_ids = _where_am_i()
        for t in range(n):
            mine = shs[t].full_piece(bufs[t], me, c)
            for r in range(3):
                _remote(mine, mine, ssem.at[3 * t + r], rsem.at[3 * t + r], (*chips[r], c)).wait_send()
        for t in range(n):
            for r in range(3):
                piece = shs[t].full_piece(bufs[t], chip_ids[r], c)
                _remote(piece, piece, ssem.at[3 * t + r], rsem.at[3 * t + r], (*chips[r], c)).wait_recv()

    out = pl.pallas_call(
        body, name=name, in_specs=[*[_HBM] * n, _SEM, _SEM, _ANY], out_specs=[_HBM] * n,
        out_shape=[pltpu.HBM(f.shape, f.dtype) for f in fulls], input_output_aliases={t: t for t in range(n)},
        compiler_params=pltpu.CompilerParams(has_side_effects=_EFFECT), **_KW,
    )(*fulls, send_sems, recv_sems, after)
    return list(out)


def _gather_pass_on(fulls, shs, name):
    n = len(fulls)

    def body(*refs):
        bufs = refs[n:2 * n]
        send_sems, recv_sems = refs[2 * n:]
        x, y, c, me, chips, chip_ids = _where_am_i()
        sib = (x, y, 1 - c)
        cps = []
        for t in range(n):
            for r in range(3):
                piece = shs[t].full_piece(bufs[t], chip_ids[r], c)
                cps.append(_remote(piece, piece, send_sems.at[t, r], recv_sems.at[t, r], sib))
        for cp in cps:
            cp.start()
        for t in range(n):
            for r in range(3):
                piece = shs[t].full_piece(bufs[t], chip_ids[r], 1 - c)
                _remote(piece, piece, send_sems.at[t, r], recv_sems.at[t, r], sib).wait_recv()
        for cp in cps:
            cp.wait_send()

    return _call(
        body, name, in_specs=[_ANY] * n, out_specs=[_ANY] * n, out_shape=[S(f.shape, f.dtype) for f in fulls],
        input_output_aliases={t: t for t in range(n)},
        scratch_shapes=[pltpu.SemaphoreType.DMA((n, 3)), pltpu.SemaphoreType.DMA((n, 3))],
        compiler_params=pltpu.CompilerParams(has_side_effects=True),
    )(*fulls)


def _pass_on_copies(bufs, shs, send_sems, recv_sems):
    x, y, c, me, chips, chip_ids = _where_am_i()
    sib = (x, y, 1 - c)
    out, back = [], []
    for t in range(len(bufs)):
        for r in range(3):
            piece = shs[t].full_piece(bufs[t], chip_ids[r], c)
            out.append(_remote(piece, piece, send_sems.at[3 * t + r], recv_sems.at[3 * t + r], sib))
            other = shs[t].full_piece(bufs[t], chip_ids[r], 1 - c)
            back.append(_remote(other, other, send_sems.at[3 * t + r], recv_sems.at[3 * t + r], sib))
    return out, back


def _pass_on_start(fulls, shs, name):
    n = len(fulls)

    def body(*refs):
        for cp in _pass_on_copies(refs[:n], shs, refs[n], refs[n + 1])[0]:
            cp.start()
        refs[-1][...] = jnp.zeros_like(refs[-1])

    out = pl.pallas_call(
        body, name=name, in_specs=[_HBM] * n,
        out_shape=(pltpu.SemaphoreType.DMA((3 * n,)), pltpu.SemaphoreType.DMA((3 * n,)), *[pltpu.HBM(f.shape, f.dtype) for f in fulls],
                   S((8, 128), _F32)),
        out_specs=(_SEM, _SEM, *[_HBM] * n, pl.BlockSpec(memory_space=pltpu.VMEM)),
        input_output_aliases={t: 2 + t for t in range(n)},
        compiler_params=pltpu.CompilerParams(has_side_effects=_EFFECT), **_KW,
    )(*[_in_hbm(f) for f in fulls])
    return out[0], out[1], list(out[2:2 + n]), out[-1]


def _pass_on_wait(fulls, send_sems, recv_sems, after, shs, name):
    n = len(fulls)

    def body(*refs):
        out, back = _pass_on_copies(refs[:n], shs, refs[n], refs[n + 1])
        for cp in out:
            cp.wait_send()
        for cp in back:
            cp.wait_recv()

    out = pl.pallas_call(
        body, name=name, in_specs=[*[_HBM] * n, _SEM, _SEM, _ANY], out_specs=[_HBM] * n,
        out_shape=[pltpu.HBM(f.shape, f.dtype) for f in fulls], input_output_aliases={t: t for t in range(n)},
        compiler_params=pltpu.CompilerParams(has_side_effects=_EFFECT), **_KW,
    )(*fulls, send_sems, recv_sems, after)
    return list(out)


def _rs_pair_copies(ins, outs, shs, send_sems, recv_sems):
    x, y, c, *_ = _where_am_i()
    sib = (x, y, 1 - c)
    cps = []
    for t in range(len(ins)):
        sh = shs[t]
        if sh.by_cols:
            cps.append(_remote(ins[t].at[pl.ds((1 - c) * sh.Rh, sh.Rh), :], outs[t], send_sems.at[4 * t], recv_sems.at[4 * t], sib))
        else:
            for j in range(4):
                cps.append(_remote(sh.full_piece(ins[t], j, 1 - c), sh.half_piece(outs[t], j),
                                   send_sems.at[4 * t + j], recv_sems.at[4 * t + j], sib))
    return cps


def _rs_pair_start(dws, shs, name):
    n = len(dws)
    lands = [lax.empty((sh.R // 2, sh.C), _WIRE) for sh in shs]

    def body(*refs):
        for cp in _rs_pair_copies(refs[:n], refs[n:2 * n], shs, refs[2 * n], refs[2 * n + 1]):
            cp.start()
        refs[-1][...] = jnp.zeros_like(refs[-1])

    out = pl.pallas_call(
        body, name=name, in_specs=[_HBM] * (2 * n),
        out_shape=(pltpu.SemaphoreType.DMA((4 * n,)), pltpu.SemaphoreType.DMA((4 * n,)),
                   *[pltpu.HBM(a.shape, a.dtype) for a in (*dws, *lands)], S((8, 128), _F32)),
        out_specs=(_SEM, _SEM, *[_HBM] * (2 * n), pl.BlockSpec(memory_space=pltpu.VMEM)),
        input_output_aliases={t: 2 + t for t in range(2 * n)},
        compiler_params=pltpu.CompilerParams(has_side_effects=_EFFECT), **_KW,
    )(*[_in_hbm(a) for a in (*dws, *lands)])
    return out[0], out[1], list(out[2:2 + n]), list(out[2 + n:2 + 2 * n]), out[-1]


def _rs_pair_wait(dws, lands, send_sems, recv_sems, after, shs, name):
    n = len(dws)

    def body(*refs):
        cps = _rs_pair_copies(refs[:n], refs[n:2 * n], shs, refs[2 * n], refs[2 * n + 1])
        for cp in cps:
            cp.wait_send()
        for cp in cps:
            cp.wait_recv()

    out = pl.pallas_call(
        body, name=name, in_specs=[*[_HBM] * (2 * n), _SEM, _SEM, _ANY], out_specs=[_HBM] * (2 * n),
        out_shape=[pltpu.HBM(a.shape, a.dtype) for a in (*dws, *lands)], input_output_aliases={t: t for t in range(2 * n)},
        compiler_params=pltpu.CompilerParams(has_side_effects=_EFFECT), **_KW,
    )(*dws, *lands, send_sems, recv_sems, after)
    return list(out[n:])


def _rs_pair_add(dw32, recv, sh, idx):
    tr, tc = _tile(sh.Q, 512, 16), _tile(sh.C, 2048)
    nb = sh.Q // tr

    def body(i_ref, c_ref, a_ref, b_ref, ow_ref):
        ow_ref[...] = (a_ref[...] + b_ref[...].astype(_F32)).astype(ow_ref.dtype)

    if sh.by_cols:
        a_map = lambda j, i, b, si, sc: (sc[0] * 4 * nb + j * nb + i, b)
    else:
        a_map = lambda j, i, b, si, sc: (j * 2 * nb + sc[0] * nb + i, b)
    h_spec = BS((tr, tc), lambda j, i, b, si, sc: (j * nb + i, b))
    return _call(
        body, "rs_pair_add",
        grid_spec=pltpu.PrefetchScalarGridSpec(num_scalar_prefetch=2, grid=(4, nb, sh.C // tc),
                                               in_specs=[BS((tr, tc), a_map), h_spec], out_specs=h_spec),
        out_shape=S((sh.R // 2, sh.C), _WIRE), compiler_params=_cp("parallel", "parallel", "parallel"),
    )(*idx, dw32, recv)


def _rs_chip_start(pws, shs, name):
    n = len(pws)
    lands = [lax.empty((3, sh.Rh, sh.Cs), _WIRE) for sh in shs]

    def body(*refs):
        ins, lnd = refs[:n], refs[n:2 * n]
        send_sems, recv_sems = refs[2 * n], refs[2 * n + 1]
        token = refs[-1]
        x, y, c, me, chips, chip_ids = _where_am_i()
        for t in range(n):
            for r in range(3):
                _remote(shs[t].half_piece(ins[t], chip_ids[r]), lnd[t].at[r], send_sems.at[3 * t + r], recv_sems.at[3 * t + r],
                        (*chips[r], c)).start()
        token[...] = jnp.zeros_like(token)

    out = pl.pallas_call(
        body, name=name, in_specs=[_HBM] * (2 * n),
        out_shape=(pltpu.SemaphoreType.DMA((3 * n,)), pltpu.SemaphoreType.DMA((3 * n,)),
                   *[pltpu.HBM(a.shape, a.dtype) for a in (*pws, *lands)], S((8, 128), _F32)),
        out_specs=(_SEM, _SEM, *[_HBM] * (2 * n), pl.BlockSpec(memory_space=pltpu.VMEM)),
        input_output_aliases={t: 2 + t for t in range(2 * n)},
        compiler_params=pltpu.CompilerParams(has_side_effects=_EFFECT), **_KW,
    )(*[_in_hbm(a) for a in (*pws, *lands)])
    return out[0], out[1], list(out[2:2 + n]), list(out[2 + n:2 + 2 * n]), out[-1]


def _rs_chip_wait(pws, lands, send_sems, recv_sems, after, shs, name):
    n = len(pws)
    after = list(after) if isinstance(after, (list, tuple)) else [after]

    def body(*refs):
        ins, lnd = refs[:n], refs[n:2 * n]
        ssem, rsem = refs[2 * n], refs[2 * n + 1]
        x, y, c, me, chips, chip_ids = _where_am_i()
        for t in range(n):
            for r in range(3):
                cp = _remote(shs[t].half_piece(ins[t], chip_ids[r]), lnd[t].at[r], ssem.at[3 * t + r], rsem.at[3 * t + r], (*chips[r], c))
                cp.wait_send()
        for t in range(n):
            for r in range(3):
                cp = _remote(shs[t].half_piece(ins[t], chip_ids[r]), lnd[t].at[r], ssem.at[3 * t + r], rsem.at[3 * t + r], (*chips[r], c))
                cp.wait_recv()

    out = pl.pallas_call(
        body, name=name, in_specs=[*[_HBM] * (2 * n), _SEM, _SEM, *[_ANY] * len(after)], out_specs=[_HBM] * (2 * n),
        out_shape=[pltpu.HBM(a.shape, a.dtype) for a in (*pws, *lands)], input_output_aliases={t: t for t in range(2 * n)},
        compiler_params=pltpu.CompilerParams(has_side_effects=_EFFECT), **_KW,
    )(*pws, *lands, send_sems, recv_sems, *after)
    return list(out[n:])


def _rs_chip_add(dw32, pair, recv, sh, idx, g_prev, l, L):
    tr, tc = _tile(sh.Rh, 512, 16), _tile(sh.Cs, 2048)
    nr, nc = sh.Rh // tr, sh.Cs // tc

    def body(i_ref, c_ref, d_ref, a_ref, b_ref, *rest):
        rest[-1][...] = ((d_ref[...] + a_ref[...].astype(_F32)) + b_ref[0].astype(_F32) + b_ref[1].astype(_F32)
                         + b_ref[2].astype(_F32))

    if sh.by_cols:
        d_map = lambda a, b, si, sc: (sc[0] * nr + a, si[0] * nc + b)
        a_map = lambda a, b, si, sc: (a, si[0] * nc + b)
    else:
        d_map = lambda a, b, si, sc: ((2 * si[0] + sc[0]) * nr + a, b)
        a_map = lambda a, b, si, sc: (si[0] * nr + a, b)
    in_specs = [BS((tr, tc), d_map), BS((tr, tc), a_map), BS((3, tr, tc), lambda a, b, si, sc: (0, a, b))]
    args = [*idx, dw32, pair, recv]
    if g_prev is not None:
        in_specs.append(_ANY)
        args.append(g_prev)
    return _call(
        body, "rs_chip_add",
        grid_spec=pltpu.PrefetchScalarGridSpec(num_scalar_prefetch=2, grid=(nr, nc), in_specs=in_specs,
                                               out_specs=BS((None, tr, tc), lambda a, b, si, sc: (l, sc[0] * nr + a, b))),
        out_shape=S((L, sh.Rs, sh.Cs), _F32), input_output_aliases={} if g_prev is None else {5: 0},
        compiler_params=_cp("parallel", "parallel"),
    )(*args)


def _rs_pair_share(gs, ls, shs, name):
    n = len(gs)

    def body(*refs):
        bufs = refs[n:2 * n]
        send_sems, recv_sems = refs[2 * n:]
        x, y, c, *_ = _where_am_i()
        sib = (x, y, 1 - c)
        cps = []
        for t in range(n):
            mine = shs[t].shard_half(bufs[t].at[ls[t]], c)
            cps.append(_remote(mine, mine, send_sems.at[t], recv_sems.at[t], sib))
        for cp in cps:
            cp.start()
        for t in range(n):
            other = shs[t].shard_half(bufs[t].at[ls[t]], 1 - c)
            _remote(other, other, send_sems.at[t], recv_sems.at[t], sib).wait_recv()
        for cp in cps:
            cp.wait_send()

    return _call(
        body, name, in_specs=[_ANY] * n, out_specs=[_ANY] * n, out_shape=[S(g.shape, g.dtype) for g in gs],
        input_output_aliases={t: t for t in range(n)},
        scratch_shapes=[pltpu.SemaphoreType.DMA((n,)), pltpu.SemaphoreType.DMA((n,))],
        compiler_params=pltpu.CompilerParams(has_side_effects=True),
    )(*gs)


def _all_reduce_small(xs, dep=None):
    M = xs.shape[0]
    deps = [] if dep is None else [dep]

    def body(x_ref, *rest):
        tot_ref, out_ref, send_sems, recv_sems, local_sem = rest[len(deps):]
        x, y, c, me, chips, chip_ids = _where_am_i()
        sib = (x, y, 1 - c)

        def rows(dev):
            return out_ref.at[pl.ds(_mo((4 * dev[0] + 2 * dev[1] + dev[2]) * M, 8), M), :]

        def copy(k, block, to, src=None):
            return _remote(rows(block) if src is None else src, rows(block), send_sems.at[k], recv_sems.at[k], to)

        mine = pltpu.make_async_copy(x_ref, rows((x, y, c)), local_sem)
        mine.start()
        first = [copy(0, (x, y, c), sib, src=x_ref)]
        first += [copy(1 + j, (x, y, c), (*chip, c), src=x_ref) for j, chip in enumerate(chips)]
        for cp in first:
            cp.start()
        passed = [copy(4 + j, (*chip, c), sib) for j, chip in enumerate(chips)]
        for j, chip in enumerate(chips):
            copy(1 + j, (*chip, c), (x, y, c)).wait_recv()
            passed[j].start()
        copy(0, sib, (x, y, c)).wait_recv()
        for j, chip in enumerate(chips):
            copy(4 + j, (*chip, 1 - c), (x, y, c)).wait_recv()
        for cp in first + passed:
            cp.wait_send()
        mine.wait()
        tot = out_ref[pl.ds(0, M), :]
        for d in range(1, 8):
            tot = tot + out_ref[pl.ds(d * M, M), :]
        tot_ref[...] = tot

    vm = pl.BlockSpec(memory_space=pltpu.VMEM)
    return _call(
        body, "all_reduce_small", in_specs=[vm] + [_ANY] * len(deps), out_specs=[vm, vm],
        out_shape=[S((M, 128), _F32), S((8 * M, 128), _F32)],
        scratch_shapes=[pltpu.SemaphoreType.DMA((7,)), pltpu.SemaphoreType.DMA((7,)), pltpu.SemaphoreType.DMA],
        compiler_params=_cp(has_side_effects=True),
    )(xs, *deps)[0]


def _reduce_scatter_begin(dws, shs, l):
    ssem, rsem, dww, lands, token = _rs_pair_start([d[1] for d in dws], shs, f"rs_pair_start_{l}")
    return ([d[0] for d in dws], dww, lands, ssem, rsem), token


def _reduce_scatter_middle(state, after, shs, idx, l):
    dw32s, dww, lands, ssem, rsem = state
    recv_a = _rs_pair_wait(dww, lands, ssem, rsem, after, shs, f"rs_pair_wait_{l}")
    pws = [_rs_pair_add(d32, ra, sh, idx) for d32, ra, sh in zip(dw32s, recv_a, shs)]
    ssem, rsem, pws, lands, token = _rs_chip_start(pws, shs, f"rs_chip_start_{l}")
    return (dw32s, recv_a, pws, lands, ssem, rsem), token


def _reduce_scatter_end(state, after, tensors, shs, gstack, idx, l):
    dw32s, recv_a, pws, lands, ssem, rsem = state
    recv_b = _rs_chip_wait(pws, lands, ssem, rsem, after, shs, f"rs_chip_wait_{l}")
    gs = [_rs_chip_add(d32, ra, rb, sh, idx, gstack[name], i, L)
          for d32, ra, rb, sh, (name, i, L) in zip(dw32s, recv_a, recv_b, shs, tensors)]
    gs = _rs_pair_share(gs, [i for _, i, _ in tensors], shs, "rs_pair_share")
    for (name, _, _), g in zip(tensors, gs):
        gstack[name] = g


def _pack(parts):
    out = []
    for p in parts:
        p2 = p.reshape(-1, 128)
        pad = (-p2.shape[0]) % 8
        out.append(jnp.pad(p2, ((0, pad), (0, 0))) if pad else p2)
    return jnp.concatenate(out, axis=0)


def _unpack(packed, like):
    out, at = [], 0
    for p in like:
        n = p.size // 128
        out.append(packed[at:at + n].reshape(p.shape))
        at += n + ((-n) % 8)
    return out


def kernel(x, mem, g_mix, g_ffn, w_in_a, g_v_a, w_spatial, b_spatial, w_in_b, g_q_b, g_k_b, g_mem, w_mem_kv, g_mq, g_mk, w_out, w_gate_up, w_down, loss_target, m_g_mix, m_g_ffn, m_w_in_a, m_g_v_a, m_w_spatial, m_b_spatial, m_w_in_b, m_g_q_b, m_g_k_b, m_g_mem, m_w_mem_kv, m_g_mq, m_g_mk, m_w_out, m_w_gate_up, m_w_down, v_g_mix, v_g_ffn, v_w_in_a, v_g_v_a, v_w_spatial, v_b_spatial, v_w_in_b, v_g_q_b, v_g_k_b, v_g_mem, v_w_mem_kv, v_g_mq, v_g_mk, v_w_out, v_w_gate_up, v_w_down):
    xs = x[0]
    mem2 = mem[0]
    target = loss_target[0]
    T, D = xs.shape
    depth = g_mix.shape[0]
    MEMW = w_mem_kv.shape[2] // 2
    TOK = D - MEMW
    KV = (w_in_b.shape[2] * 4 - TOK - MEMW) // 2
    QPK = TOK // KV
    F = w_gate_up.shape[2] * 4 // 2

    idx = ((2 * lax.axis_index("x") + lax.axis_index("y")).astype(jnp.int32).reshape(1), lax.axis_index("c").astype(jnp.int32).reshape(1))

    big = {
        "w_in_a": (w_in_a, _Shard(D, w_in_a.shape[2] * 4, True)),
        "w_in_b": (w_in_b, _Shard(D, w_in_b.shape[2] * 4, True)),
        "w_mem_kv": (w_mem_kv, _Shard(D, 2 * MEMW, False)),
        "w_out": (w_out, _Shard(D, D, False)),
        "w_gate_up": (w_gate_up, _Shard(D, 2 * F, True)),
        "w_down": (w_down, _Shard(F, D, False)),
    }

    def layer_tensors(l):
        n_in = "w_in_a" if l % 2 == 0 else "w_in_b"
        return [(n_in, l // 2, big[n_in][0].shape[0])] + [(n, l, depth) for n in ("w_mem_kv", "w_out", "w_gate_up", "w_down")]

    def layer_shards(l):
        return [big[n][1] for n, _, _ in layer_tensors(l)]

    full = {n: [None] * w.shape[0] for n, (w, _) in big.items()}
    flying = {}

    def start_layer(l, dep):
        tens = layer_tensors(l)
        token = dep
        for gi, group in enumerate([tens[:2], tens[2:3], tens[3:4], tens[4:]]):
            shs = [big[n][1] for n, _, _ in group]
            bufs = [_cast_into_full(big[n][0], i, big[n][1], idx, dep=token) for n, i, _ in group]
            ssem, rsem, bufs, token = _gather_start(bufs, shs, f"gather_start_{l}_{gi}")
            for n, i, _ in group:
                flying[(n, i)] = dict(group=group, shs=shs, state=(ssem, rsem, bufs), name=f"{l}_{gi}", passing=False)
        return token

    def land(members, after):
        for n, i in members:
            fl = flying[(n, i)]
            ssem, rsem, bufs = fl["state"]
            bufs = _gather_wait(bufs, ssem, rsem, after, fl["shs"], "gather_wait_" + fl["name"])
            ssem, rsem, bufs, after = _pass_on_start(bufs, fl["shs"], "pass_on_start_" + fl["name"])
            fl.update(state=(ssem, rsem, bufs), passing=True)
        return after

    def weight(n, i, after):
        if full[n][i] is None:
            fl = flying[(n, i)]
            ssem, rsem, bufs = fl["state"]
            if fl["passing"]:
                bufs = _pass_on_wait(bufs, ssem, rsem, after, fl["shs"], "pass_on_wait_" + fl["name"])
            else:
                bufs = _gather_wait(bufs, ssem, rsem, after, fl["shs"], "gather_wait_" + fl["name"])
                bufs = _gather_pass_on(bufs, fl["shs"], "gather_pass_on")
            for (m, j, _), b in zip(fl["group"], bufs):
                full[m][j] = b
        return full[n][i]

    after = None
    for l in range(depth):
        after = start_layer(l, after)
    tabs = _rope_tables(T)

    saved = []
    xc = xs
    for l in range(depth):
        is_a = l % 2 == 0
        li = l // 2
        w_in = weight("w_in_a" if is_a else "w_in_b", li, after)
        h, ht = _rmsnorm_fwd(xc, g_mix[l], "rmsnorm_fwd", transposed=True)
        z = _mm_nn("mm_in", h, w_in, pm=2048, pn=512)
        st = dict(x=xc, ht=ht, z=z)
        if is_a:
            ws_m = w_spatial[li].astype(_MXU)
            st["ws_m"], st["wst_m"], st["b_t"] = ws_m, jnp.swapaxes(ws_m, 1, 2), b_spatial[li].T
            tok = _mixer_a_fwd(z, g_v_a[li], ws_m, st["b_t"], TOK)
            qblk = 2 * TOK // MEMW
        else:
            q, k, v = _qk_rope_fwd(z, g_q_b[li], g_k_b[li], tabs, TOK, KV)
            tok, stat = _attn_fwd(q, k, v, QPK)
            st["q"], st["k"], st["v"], st["stat"] = q, k, v, stat
            qblk = (TOK + 2 * KV) // MEMW
        mem_n = _rmsnorm_fwd(mem2, g_mem[l], "rmsnorm_mem")
        kv = _mm_nn("mm_memkv", mem_n, weight("w_mem_kv", l, z))
        mo = _mem_fwd(z, qblk, kv, g_mq[l], g_mk[l], MEMW)
        cat = jnp.concatenate([tok, mo], axis=1)
        token = land([("w_out", l), ("w_gate_up", l), ("w_down", l)], cat) if l > 0 else None
        x1 = _mm_nn("mm_out", cat, weight("w_out", l, cat), add=xc, dep=token)
        h2, h2t = _rmsnorm_fwd(x1, g_ffn[l], "rmsnorm_fwd", transposed=True)
        act, gu = _ffn_gate_up(h2, weight("w_gate_up", l, h2))
        w_down_l = weight("w_down", l, act)
        token = land([layer_tensors(l + 1)[0][:2]], act) if l + 1 < depth else None
        xc = _mm_nn("mm_down", act, w_down_l, add=x1, pm=512, pn=1024, pk=8192, dep=token)
        after = xc
        st.update(mem_n=mem_n, kv=kv, qblk=qblk, cat=cat, x1=x1, h2t=h2t, act=act, gu=gu)
        saved.append(st)

    dx, dxm, sq = _loss_head(xc, target)
    loss = lax.psum(sq[0, 0] * (0.5 / D), ("x", "y", "c"))

    gsm = {n: [None] * len(a) for n, a in dict(g_mix=g_mix, g_ffn=g_ffn, g_v_a=g_v_a, w_spatial=w_spatial, b_spatial=b_spatial,
                                                g_q_b=g_q_b, g_k_b=g_k_b, g_mem=g_mem, g_mq=g_mq, g_mk=g_mk).items()}
    gstack = {n: None for n in big}
    pairing, chipping, token = None, None, None

    def advance(after):
        nonlocal pairing, chipping
        state, tok = _reduce_scatter_middle(pairing[0], after, layer_shards(pairing[1]), idx, pairing[1])
        if chipping is not None:
            _reduce_scatter_end(chipping[0], tok, layer_tensors(chipping[1]), layer_shards(chipping[1]), gstack, idx, chipping[1])
        pairing, chipping = None, (state, pairing[1])
        return tok

    for l in reversed(range(depth)):
        st = saved[l]
        is_a = l % 2 == 0
        li = l // 2
        gbig = {}
        dgu = _ffn_dact(dxm, full["w_down"][l], st["gu"], dep=token)
        token = advance(dgu) if pairing is not None else None
        gbig["w_down"] = _mm_tn_dual("mm_dw_down", st["act"], dxm, pm=512, pn=1024, pk=4096)
        dh2 = _ffn_dh(dgu, full["w_gate_up"][l], dep=token)
        gbig["w_gate_up"] = _ffn_dwgu(st["h2t"], dgu)
        dx, dxm, dg = _rmsnorm_bwd(st["x1"], g_ffn[l], dh2, dx, "rmsnorm_bwd")
        gsm["g_ffn"][l] = dg[0]
        dcat = _mm_nt("mm_dcat", dxm, full["w_out"][l])
        gbig["w_out"] = _mm_tn_dual("mm_dw_out", st["cat"], dxm, pm=512, pn=1024, pk=4096)
        dzq, dkn, dvm, dgq = _mem_bwd(st["z"], st["qblk"], st["kv"], g_mq[l], g_mk[l], dcat, TOK // MEMW, MEMW)
        dkv, dgk = _memkv_bwd(st["kv"], dkn, dvm, g_mk[l], MEMW)
        gsm["g_mq"][l], gsm["g_mk"][l] = dgq[0], dgk[0]
        gbig["w_mem_kv"] = _mm_tn_dual("mm_dw_memkv", st["mem_n"], dkv)
        dmem_n = _mm_nt("mm_dmemn", dkv, full["w_mem_kv"][l])
        gsm["g_mem"][l] = _rmsnorm_bwd(mem2, g_mem[l], dmem_n, None, "rmsnorm_bwd_mem")[2][0]
        if is_a:
            dz_tok, dws, dbs, dgv = _mixer_a_bwd(st["z"], dcat, g_v_a[li], st["ws_m"], st["wst_m"], st["b_t"], TOK)
            gsm["w_spatial"][li], gsm["b_spatial"][li], gsm["g_v_a"][li] = dws, dbs[:, :, 0], dgv[0]
            dz = jnp.concatenate([dz_tok, dzq], axis=1)
        else:
            dq, dk, dv = _attn_bwd(st["q"], st["k"], st["v"], dcat, st["cat"], st["stat"], QPK)
            dz_qk, dgq_b, dgk_b = _qk_rope_bwd(st["z"], dq, dk, g_q_b[li], g_k_b[li], tabs, TOK, KV)
            gsm["g_q_b"][li], gsm["g_k_b"][li] = dgq_b[0], dgk_b[0]
            dz = jnp.concatenate([dz_qk, dv, dzq], axis=1)
        n_in = "w_in_a" if is_a else "w_in_b"
        dh = _mm_nt("mm_dh", dz, full[n_in][li])
        gbig[n_in] = _mm_nn_dual("mm_dw_in", st["ht"], dz, pm=1024, pn=512, pk=4096)
        dx, dxm, dg = _rmsnorm_bwd(st["x"], g_mix[l], dh, dx, "rmsnorm_bwd")
        gsm["g_mix"][l] = dg[0]
        state, token = _reduce_scatter_begin([gbig[n] for n, _, _ in layer_tensors(l)], layer_shards(l), l)
        pairing = (state, l)

    small = ["g_mix", "g_ffn", "g_v_a", "w_spatial", "b_spatial", "g_q_b", "g_k_b", "g_mem", "g_mq", "g_mk"]
    env = dict(g_mix=g_mix, g_ffn=g_ffn, g_v_a=g_v_a, w_spatial=w_spatial, b_spatial=b_spatial, g_q_b=g_q_b, g_k_b=g_k_b,
               g_mem=g_mem, g_mq=g_mq, g_mk=g_mk,
               m_g_mix=m_g_mix, m_g_ffn=m_g_ffn, m_g_v_a=m_g_v_a, m_w_spatial=m_w_spatial, m_b_spatial=m_b_spatial,
               m_g_q_b=m_g_q_b, m_g_k_b=m_g_k_b, m_g_mem=m_g_mem, m_g_mq=m_g_mq, m_g_mk=m_g_mk,
               v_g_mix=v_g_mix, v_g_ffn=v_g_ffn, v_g_v_a=v_g_v_a, v_w_spatial=v_w_spatial, v_b_spatial=v_b_spatial,
               v_g_q_b=v_g_q_b, v_g_k_b=v_g_k_b, v_g_mem=v_g_mem, v_g_mq=v_g_mq, v_g_mk=v_g_mk,
               m_w_in_a=m_w_in_a, m_w_in_b=m_w_in_b, m_w_mem_kv=m_w_mem_kv, m_w_out=m_w_out, m_w_gate_up=m_w_gate_up, m_w_down=m_w_down,
               v_w_in_a=v_w_in_a, v_w_in_b=v_w_in_b, v_w_mem_kv=v_w_mem_kv, v_w_out=v_w_out, v_w_gate_up=v_w_gate_up, v_w_down=v_w_down)
    like = [env[n] for n in small]
    g_small = _all_reduce_small(_pack([jnp.stack(gsm[n]) for n in small]), dep=token)

    res = {}
    outs = _adamw(_pack(like)[None], g_small[None], _pack([env["m_" + n] for n in small])[None],
                  _pack([env["v_" + n] for n in small])[None], "adamw_small")
    unpacked = [_unpack(o[0], like) for o in outs]
    for k, n in enumerate(small):
        res[n] = [u[k] for u in unpacked]
    advance(outs[1])
    pending = chipping
    last = {n: i for n, i, _ in layer_tensors(pending[1])}
    early = {}
    for n, (w, _) in big.items():
        L = w.shape[0]
        if n not in last:
            res[n] = _adamw(w, gstack[n], env["m_" + n], env["v_" + n], "adamw_" + n)
        elif L > 1:
            assert last[n] == 0
            early[n] = _adamw(w, gstack[n], env["m_" + n], env["v_" + n], "adamw_early_" + n, l0=1)
    done = [o[1] for o in early.values()] + [res[n][1] for n in big if n in res] + [res[small[0]][1]]
    _reduce_scatter_end(pending[0], done, layer_tensors(pending[1]), layer_shards(pending[1]), gstack, idx, pending[1])
    for n in last:
        res[n] = _adamw(big[n][0], gstack[n], env["m_" + n], env["v_" + n], "adamw_last_" + n, l0=0, l1=1, prev=early.get(n))

    order = ["g_mix", "g_ffn", "w_in_a", "g_v_a", "w_spatial", "b_spatial", "w_in_b", "g_q_b", "g_k_b", "g_mem", "w_mem_kv",
             "g_mq", "g_mk", "w_out", "w_gate_up", "w_down"]
    return (loss, dx.reshape(1, T, D), *[res[n][0] for n in order], *[res[n][1] for n in order],
            *[res[n][2] for n in order], *[res[n][3] for n in order])
```

```python
import jax
import jax.numpy as jnp
import numpy as np
from jax import lax
from jax.experimental import pallas as pl
from jax.experimental.pallas import tpu as pltpu

_F32 = jnp.float32
_MXU = jnp.bfloat16
_WIRE = jnp.bfloat16
_KW = {}

EPS = 1e-6
HEAD = 128
CHUNK = 128
GRID_W = 64
ROPE_THETA = 10000.0
ADAM_LR, ADAM_B1, ADAM_B2, ADAM_EPS, ADAM_WD, ADAM_STEP = 0.001, 0.9, 0.999, 1e-08, 0.01, 10
_SQRT_HALF = float(np.sqrt(0.5))
_INV_SQRT_2PI = float(1.0 / np.sqrt(2.0 * np.pi))
_VMEM_LIMIT = 56 * 1024 * 1024
_MESH = pl.DeviceIdType.MESH

_NN = (((1,), (0,)), ((), ()))
_NT = (((1,), (1,)), ((), ()))
_TN = (((0,), (0,)), ((), ()))

S = jax.ShapeDtypeStruct
BS = pl.BlockSpec
_ANY = pl.BlockSpec(memory_space=pl.ANY)


def _tile(n, pref, mult=128):
    if n <= pref:
        return n
    d = (pref // mult) * mult
    while d >= mult:
        if n % d == 0:
            return d
        d -= mult
    raise ValueError(f"no tile for {n} (pref {pref}, mult {mult})")


def _mo(v, m):
    return v if isinstance(v, int) else pl.multiple_of(v, m)


def _cp(*sem, **kw):
    return pltpu.CompilerParams(dimension_semantics=sem or None, vmem_limit_bytes=_VMEM_LIMIT, **kw)


def _call(body, name, **kw):
    return pl.pallas_call(body, name=name, **kw, **_KW)


def _dot(a, b, dn=_NN):
    return lax.dot_general(a, b, dn, preferred_element_type=_F32)


def _gelu(x):
    return 0.5 * x * (1.0 + lax.erf(x * _SQRT_HALF))


def _gelu_grad(x):
    return 0.5 * (1.0 + lax.erf(x * _SQRT_HALF)) + x * jnp.exp(-0.5 * x * x) * _INV_SQRT_2PI


def _rstd(x):
    return lax.rsqrt(jnp.mean(x * x, axis=-1, keepdims=True) + EPS)


def _norm_bwd(dout, xhat, r, g):
    dy = dout * g
    return r * (dy - xhat * jnp.mean(dy * xhat, axis=-1, keepdims=True))


def _softmax(s):
    e = jnp.exp(s - jnp.max(s, axis=-1, keepdims=True))
    return e * (1.0 / jnp.sum(e, axis=-1, keepdims=True))


def _mm(name, a, b, a_spec, b_spec, dn, grid, acc_shape, out_shape, out_specs, epilogue, extra=(), extra_specs=(), dep=None):
    nk = grid[2]
    n_ex = len(extra)
    deps = [] if dep is None else [dep]
    multi = isinstance(out_shape, (list, tuple))
    n_out = len(out_shape) if multi else 1

    def body(*refs):
        a_ref, b_ref = refs[0], refs[1]
        ex = refs[2:2 + n_ex]
        outs = refs[2 + n_ex + len(deps):2 + n_ex + len(deps) + n_out]

        def prod():
            return _dot(a_ref[...].astype(_MXU), b_ref[...].astype(_MXU), dn)

        if nk == 1:
            epilogue(prod(), ex, outs)
        else:
            acc = refs[-1]
            k = pl.program_id(2)

            @pl.when(k == 0)
            def _():
                acc[...] = jnp.zeros_like(acc)

            acc[...] += prod()

            @pl.when(k == nk - 1)
            def _():
                epilogue(acc[...], ex, outs)

    return _call(
        body, name, grid=grid, in_specs=[a_spec, b_spec, *extra_specs] + [_ANY] * len(deps), out_specs=out_specs, out_shape=out_shape,
        scratch_shapes=[] if nk == 1 else [pltpu.VMEM(acc_shape, _F32)],
        compiler_params=_cp("parallel", "parallel", "arbitrary"),
    )(a, b, *extra, *deps)


def _ep_store(acc, ex, outs):
    for o in outs:
        o[...] = acc.astype(o.dtype)


def _ep_add(acc, ex, outs):
    outs[0][...] = (acc + ex[0][...]).astype(outs[0].dtype)


def _mm_nn(name, a, b, out_dtype=_F32, add=None, pm=1024, pn=1024, pk=2048, dep=None):
    M, K = a.shape
    N = b.shape[1]
    tm, tn, tk = _tile(M, pm, 8), _tile(N, pn), _tile(K, pk)
    o_spec = BS((tm, tn), lambda i, j, k: (i, j))
    return _mm(name, a, b, BS((tm, tk), lambda i, j, k: (i, k)), BS((tk, tn), lambda i, j, k: (k, j)), _NN,
               (M // tm, N // tn, K // tk), (tm, tn), S((M, N), out_dtype), o_spec,
               _ep_store if add is None else _ep_add,
               extra=() if add is None else (add,), extra_specs=() if add is None else (o_spec,), dep=dep)


def _mm_nt(name, a, b, out_dtype=_F32, pm=1024, pn=1024, pk=4096):
    M, K = a.shape
    N = b.shape[0]
    tm, tn, tk = _tile(M, pm, 8), _tile(N, pn), _tile(K, pk)
    return _mm(name, a, b, BS((tm, tk), lambda i, j, k: (i, k)), BS((tn, tk), lambda i, j, k: (j, k)), _NT,
               (M // tm, N // tn, K // tk), (tm, tn), S((M, N), out_dtype), BS((tm, tn), lambda i, j, k: (i, j)), _ep_store)


def _mm_tn_dual(name, a, b, pm=1024, pn=1024, pk=2048, dep=None):
    K, M = a.shape
    N = b.shape[1]
    tm, tn, tk = _tile(M, pm), _tile(N, pn), _tile(K, pk, 16)
    o_spec = BS((tm, tn), lambda i, j, k: (i, j))
    return _mm(name, a, b, BS((tk, tm), lambda i, j, k: (k, i)), BS((tk, tn), lambda i, j, k: (k, j)), _TN,
               (M // tm, N // tn, K // tk), (tm, tn), [S((M, N), _F32), S((M, N), _WIRE)], [o_spec, o_spec], _ep_store, dep=dep)


def _ffn_gate_up(h2, wgu):
    T, D = h2.shape
    F = wgu.shape[1] // 2
    tm, tn = _tile(T, 1024, 8), _tile(F, 512)
    nj = F // tn

    def body(a_ref, bg_ref, bu_ref, act_ref, gu_ref):
        a = a_ref[...]
        g = _dot(a, bg_ref[...])
        u = _dot(a, bu_ref[...])
        sg = 1.0 / (1.0 + jnp.exp(-g))
        silu = g * sg
        gu_ref[0] = (u * (sg * (1.0 + g * (1.0 - sg)))).astype(gu_ref.dtype)
        gu_ref[1] = silu.astype(gu_ref.dtype)
        act_ref[...] = (silu * u).astype(act_ref.dtype)

    return _call(
        body, "ffn_gate_up", grid=(T // tm, nj),
        in_specs=[BS((tm, D), lambda i, j: (i, 0)), BS((D, tn), lambda i, j: (0, j)), BS((D, tn), lambda i, j: (0, j + nj))],
        out_specs=[BS((tm, tn), lambda i, j: (i, j)), BS((2, tm, tn), lambda i, j: (0, i, j))],
        out_shape=[S((T, F), _MXU), S((2, T, F), _MXU)],
        compiler_params=_cp("parallel", "parallel"),
    )(h2, wgu, wgu)


def _ffn_dact(dxm, wdown, gu, dep=None):
    T, D = dxm.shape
    F = wdown.shape[0]
    tm, tn = _tile(T, 2048, 8), _tile(F, 512)
    deps = [] if dep is None else [dep]

    def body(a_ref, b_ref, gu_ref, *rest):
        o_ref = rest[-1]
        d = _dot(a_ref[...], b_ref[...], _NT)
        o_ref[0] = (d * gu_ref[0].astype(_F32)).astype(o_ref.dtype)
        o_ref[1] = (d * gu_ref[1].astype(_F32)).astype(o_ref.dtype)

    return _call(
        body, "ffn_dact", grid=(T // tm, F // tn),
        in_specs=[BS((tm, D), lambda i, j: (i, 0)), BS((tn, D), lambda i, j: (j, 0)), BS((2, tm, tn), lambda i, j: (0, i, j))]
        + [_ANY] * len(deps),
        out_specs=BS((2, tm, tn), lambda i, j: (0, i, j)), out_shape=S((2, T, F), _MXU),
        compiler_params=_cp("parallel", "parallel"),
    )(dxm, wdown, gu, *deps)


def _ffn_dh(dgu, wgu, dep=None):
    _, T, F = dgu.shape
    D = wgu.shape[0]
    tm, tn, tk = _tile(T, 512, 8), _tile(D, 2048), _tile(F, 2816, 256)
    nkf = F // tk
    return _mm("ffn_dh", dgu, wgu, BS((None, tm, tk), lambda i, j, k: (k // nkf, i, k % nkf)),
               BS((tn, tk), lambda i, j, k: (j, k)), _NT, (T // tm, D // tn, 2 * nkf), (tm, tn),
               S((T, D), _F32), BS((tm, tn), lambda i, j, k: (i, j)), _ep_store, dep=dep)


def _ffn_dwgu(h2t, dgu):
    _, T, F = dgu.shape
    D = h2t.shape[0]
    tm, tn, tk = _tile(D, 1024), _tile(F, 512), _tile(T, 4096)
    njf = F // tn
    o_spec = BS((tm, tn), lambda i, j, k: (i, j))
    return _mm("ffn_dwgu", h2t, dgu, BS((tm, tk), lambda i, j, k: (i, k)),
               BS((None, tk, tn), lambda i, j, k: (j // njf, k, j % njf)), _NN, (D // tm, 2 * njf, T // tk), (tm, tn),
               [S((D, 2 * F), _F32), S((D, 2 * F), _WIRE)], [o_spec, o_spec], _ep_store)


def _mm_nn_dual(name, a, b, pm=1024, pn=1024, pk=2048):
    M, K = a.shape
    N = b.shape[1]
    tm, tn, tk = _tile(M, pm), _tile(N, pn), _tile(K, pk)
    o_spec = BS((tm, tn), lambda i, j, k: (i, j))
    return _mm(name, a, b, BS((tm, tk), lambda i, j, k: (i, k)), BS((tk, tn), lambda i, j, k: (k, j)), _NN,
               (M // tm, N // tn, K // tk), (tm, tn), [S((M, N), _F32), S((M, N), _WIRE)], [o_spec, o_spec], _ep_store)


def _rmsnorm_fwd(x, g, name, dep=None, transposed=False):
    T, D = x.shape
    tr = _tile(T, 512, 128)
    n_out = 2 if transposed else 1

    def body(x_ref, g_ref, *rest):
        outs = rest[-n_out:]
        xv = x_ref[...]
        h = (xv * _rstd(xv) * g_ref[...]).astype(outs[0].dtype)
        outs[0][...] = h
        if transposed:
            outs[1][...] = h.T

    row = BS((tr, D), lambda i: (i, 0))
    deps = [] if dep is None else [dep]
    return _call(body, name, grid=(T // tr,), in_specs=[row, BS((1, D), lambda i: (0, 0))] + [_ANY] * len(deps),
                 out_specs=[row, BS((D, tr), lambda i: (0, i))] if transposed else row,
                 out_shape=[S((T, D), _MXU), S((D, T), _MXU)] if transposed else S((T, D), _MXU),
                 compiler_params=_cp("parallel"))(x, g.reshape(1, D), *deps)


def _rmsnorm_bwd(x, g, dh, dres, name):
    T, D = x.shape
    tr = _tile(T, 256, 8)
    has_res = dres is not None

    def body(*refs):
        x_ref, g_ref, dh_ref = refs[:3]
        dx_ref, dxm_ref, dg_ref = refs[-3:]

        @pl.when(pl.program_id(0) == 0)
        def _():
            dg_ref[...] = jnp.zeros_like(dg_ref)

        xv = x_ref[...]
        r = _rstd(xv)
        xhat = xv * r
        dh_v = dh_ref[...]
        dg_ref[...] += jnp.sum(dh_v * xhat, axis=0, keepdims=True)
        dx = _norm_bwd(dh_v, xhat, r, g_ref[...])
        if has_res:
            dx = dx + refs[3][...]
        dx_ref[...] = dx
        dxm_ref[...] = dx.astype(dxm_ref.dtype)

    row = BS((tr, D), lambda i: (i, 0))
    vec = BS((1, D), lambda i: (0, 0))
    return _call(body, name, grid=(T // tr,), in_specs=[row, vec, row] + ([row] if has_res else []),
                 out_specs=[row, row, vec], out_shape=[S((T, D), _F32), S((T, D), _MXU), S((1, D), _F32)],
                 compiler_params=_cp("arbitrary"))(x, g.reshape(1, D), dh, *([dres] if has_res else []))


def _loss_head(y, target):
    T, D = y.shape
    tr = _tile(T, 256, 8)

    def body(y_ref, t_ref, dy_ref, dym_ref, acc_ref):
        @pl.when(pl.program_id(0) == 0)
        def _():
            acc_ref[...] = jnp.zeros_like(acc_ref)

        err = y_ref[...] - t_ref[...]
        acc_ref[...] += jnp.sum(jnp.sum(err * err, axis=-1, keepdims=True), axis=0, keepdims=True)
        dy = err * (1.0 / D)
        dy_ref[...] = dy
        dym_ref[...] = dy.astype(dym_ref.dtype)

    row = BS((tr, D), lambda i: (i, 0))
    return _call(body, "loss_head", grid=(T // tr,), in_specs=[row, row],
                 out_specs=[row, row, BS((1, 128), lambda i: (0, 0))],
                 out_shape=[S((T, D), _F32), S((T, D), _MXU), S((1, 128), _F32)],
                 compiler_params=_cp("arbitrary"))(y, target)


def _mixa_blocks(T):
    return 2 if T % (2 * CHUNK) == 0 else 1


def _mixer_a_fwd(z, gv, ws_m, b_t, TOK):
    T = z.shape[0]
    G = TOK // HEAD
    CB = _mixa_blocks(T)
    R = CB * CHUNK

    def body(z_ref, gv_ref, ws_ref, bt_ref, o_ref):
        u = _gelu(z_ref[:, :TOK])
        v = _gelu(z_ref[:, TOK:])
        vn = (v * _rstd(v) * gv_ref[...]).astype(_MXU)
        for c in range(CB):
            rows = slice(c * CHUNK, (c + 1) * CHUNK)
            for g in range(G):
                cols = slice(g * HEAD, (g + 1) * HEAD)
                s = _dot(ws_ref[g], vn[rows, cols]) + bt_ref[:, g:g + 1]
                o_ref[rows, cols] = (u[rows, cols] * s).astype(o_ref.dtype)

    return _call(
        body, "mixer_a_fwd", grid=(T // R,),
        in_specs=[BS((R, 2 * TOK), lambda i: (i, 0)), BS((1, TOK), lambda i: (0, 0)),
                  BS((G, CHUNK, CHUNK), lambda i: (0, 0, 0)), BS((CHUNK, G), lambda i: (0, 0))],
        out_specs=BS((R, TOK), lambda i: (i, 0)), out_shape=S((T, TOK), _MXU), compiler_params=_cp("parallel"),
    )(z, gv.reshape(1, TOK), ws_m, b_t)


def _mixer_a_bwd(z, dcat, gv, ws_m, wst_m, b_t, TOK):
    T = z.shape[0]
    G = TOK // HEAD
    CB = _mixa_blocks(T)
    R = CB * CHUNK
    n = T // R

    def body(z_ref, d_ref, gv_ref, ws_ref, wst_ref, bt_ref, dz_ref, dws_ref, db_ref, dgv_ref, dvn_scr):
        i = pl.program_id(0)

        @pl.when(i == 0)
        def _():
            dws_ref[...] = jnp.zeros_like(dws_ref)
            db_ref[...] = jnp.zeros_like(db_ref)
            dgv_ref[...] = jnp.zeros_like(dgv_ref)

        zu = z_ref[:, :TOK]
        zv = z_ref[:, TOK:]
        u = _gelu(zu)
        v = _gelu(zv)
        r = _rstd(v)
        vhat = v * r
        gvv = gv_ref[...]
        vn = (vhat * gvv).astype(_MXU)
        d = d_ref[...]
        gpu = _gelu_grad(zu)
        for c in range(CB):
            rows = slice(c * CHUNK, (c + 1) * CHUNK)
            for g in range(G):
                cols = slice(g * HEAD, (g + 1) * HEAD)
                vn_cg = vn[rows, cols]
                s = _dot(ws_ref[g], vn_cg) + bt_ref[:, g:g + 1]
                d_cg = d[rows, cols]
                dz_ref[rows, cols] = (d_cg * s * gpu[rows, cols]).astype(dz_ref.dtype)
                ds = d_cg * u[rows, cols]
                ds_m = ds.astype(_MXU)
                dvn_scr[rows, cols] = _dot(wst_ref[g], ds_m)
                dws_ref[g] += _dot(ds_m, vn_cg, _NT)
                db_ref[g] += ds
        dvn = dvn_scr[...]
        dgv_ref[...] += jnp.sum(dvn * vhat, axis=0, keepdims=True)
        dv = _norm_bwd(dvn, vhat, r, gvv)
        dz_ref[:, TOK:] = (dv * _gelu_grad(zv)).astype(dz_ref.dtype)

        @pl.when(i == n - 1)
        def _():
            for g in range(G):
                db_ref[g] = jnp.broadcast_to(jnp.sum(db_ref[g], axis=1, keepdims=True), (CHUNK, CHUNK))

    full3 = BS((G, CHUNK, CHUNK), lambda i: (0, 0, 0))
    return _call(
        body, "mixer_a_bwd", grid=(n,),
        in_specs=[BS((R, 2 * TOK), lambda i: (i, 0)), BS((R, TOK), lambda i: (i, 0)), BS((1, TOK), lambda i: (0, 0)),
                  full3, full3, BS((CHUNK, G), lambda i: (0, 0))],
        out_specs=[BS((R, 2 * TOK), lambda i: (i, 0)), full3, full3, BS((1, TOK), lambda i: (0, 0))],
        out_shape=[S((T, 2 * TOK), _MXU), S((G, CHUNK, CHUNK), _F32), S((G, CHUNK, CHUNK), _F32), S((1, TOK), _F32)],
        scratch_shapes=[pltpu.VMEM((R, TOK), _F32)], compiler_params=_cp("arbitrary"),
    )(z, dcat, gv.reshape(1, TOK), ws_m, wst_m, b_t)


def _rope_tables(T):
    n_rows = T // GRID_W
    rows = jnp.broadcast_to(jnp.arange(n_rows)[:, None], (n_rows, GRID_W)).reshape(T)
    cols = jnp.broadcast_to(jnp.arange(GRID_W)[None, :], (n_rows, GRID_W)).reshape(T)
    pairs = HEAD // 4
    freqs = ROPE_THETA ** (-jnp.arange(pairs, dtype=_F32) / pairs)
    ang_r = rows.astype(_F32)[:, None] * freqs
    ang_c = cols.astype(_F32)[:, None] * freqs
    ang = jnp.concatenate([ang_r, ang_r, ang_c, ang_c], axis=-1)
    cos, sin = jnp.cos(ang), jnp.sin(ang)
    first = (jnp.arange(HEAD) % (HEAD // 2)) < (HEAD // 4)
    return cos, jnp.where(first, -sin, 0.0), jnp.where(first, 0.0, sin)


def _rope(x, cs, sa, sb):
    return x * cs + pltpu.roll(x, 96, 1) * sa + pltpu.roll(x, 32, 1) * sb


def _qk_rope_fwd(z, gq, gk, tabs, TOK, KV):
    T = z.shape[0]
    R = _tile(T, 512, 8)
    W = TOK + 2 * KV

    def body(z_ref, gq_ref, gk_ref, cos_ref, sa_ref, sb_ref, q_ref, k_ref, v_ref):
        cs, sa, sb = cos_ref[...], sa_ref[...], sb_ref[...]
        for h in range((TOK + KV) // HEAD):
            cols = slice(h * HEAD, (h + 1) * HEAD)
            xv = z_ref[:, cols]
            xn = xv * _rstd(xv) * (gq_ref[...] if h < TOK // HEAD else gk_ref[...])
            out = _rope(xn, cs, sa, sb)
            if h < TOK // HEAD:
                q_ref[:, cols] = out.astype(q_ref.dtype)
            else:
                k_ref[:, h * HEAD - TOK:(h + 1) * HEAD - TOK] = out.astype(k_ref.dtype)
        v_ref[...] = z_ref[:, TOK + KV:].astype(v_ref.dtype)

    vec = BS((1, HEAD), lambda i: (0, 0))
    tab = BS((R, HEAD), lambda i: (i, 0))
    return _call(
        body, "qk_rope_fwd", grid=(T // R,), in_specs=[BS((R, W), lambda i: (i, 0)), vec, vec, tab, tab, tab],
        out_specs=[BS((R, TOK), lambda i: (i, 0)), BS((R, KV), lambda i: (i, 0)), BS((R, KV), lambda i: (i, 0))],
        out_shape=[S((T, TOK), _MXU), S((T, KV), _MXU), S((T, KV), _MXU)], compiler_params=_cp("parallel"),
    )(z, gq.reshape(1, HEAD), gk.reshape(1, HEAD), *tabs)


def _qk_rope_bwd(z, dq, dk, gq, gk, tabs, TOK, KV):
    T = z.shape[0]
    R = _tile(T, 512, 8)
    W = TOK + KV

    def body(z_ref, dq_ref, dk_ref, gq_ref, gk_ref, cos_ref, sa_ref, sb_ref, dz_ref, dgq_ref, dgk_ref):
        @pl.when(pl.program_id(0) == 0)
        def _():
            dgq_ref[...] = jnp.zeros_like(dgq_ref)
            dgk_ref[...] = jnp.zeros_like(dgk_ref)

        cs, sa, sb = cos_ref[...], sa_ref[...], sb_ref[...]
        for h in range(W // HEAD):
            cols = slice(h * HEAD, (h + 1) * HEAD)
            is_q = h < TOK // HEAD
            do = dq_ref[:, cols] if is_q else dk_ref[:, h * HEAD - TOK:(h + 1) * HEAD - TOK]
            dxn = do * cs - pltpu.roll(do, 96, 1) * sa - pltpu.roll(do, 32, 1) * sb
            xv = z_ref[:, cols]
            r = _rstd(xv)
            xhat = xv * r
            dg_ref = dgq_ref if is_q else dgk_ref
            dg_ref[...] += jnp.sum(dxn * xhat, axis=0, keepdims=True)
            dz_ref[:, cols] = _norm_bwd(dxn, xhat, r, gq_ref[...] if is_q else gk_ref[...]).astype(dz_ref.dtype)

    vec = BS((1, HEAD), lambda i: (0, 0))
    tab = BS((R, HEAD), lambda i: (i, 0))
    return _call(
        body, "qk_rope_bwd", grid=(T // R,),
        in_specs=[BS((R, W), lambda i: (i, 0)), BS((R, TOK), lambda i: (i, 0)), BS((R, KV), lambda i: (i, 0)), vec, vec, tab, tab, tab],
        out_specs=[BS((R, W), lambda i: (i, 0)), vec, vec],
        out_shape=[S((T, W), _MXU), S((1, HEAD), _F32), S((1, HEAD), _F32)], compiler_params=_cp("arbitrary"),
    )(z, dq, dk, gq.reshape(1, HEAD), gk.reshape(1, HEAD), *tabs)


_ATTN_C2 = float(HEAD ** -0.5 * np.log2(np.e))


def _attn_fwd(q, k, v, QPK):
    T, TOK = q.shape
    KVH = k.shape[1] // HEAD
    tq = _tile(T, 1024, 8)
    ts = 256 if tq % 256 == 0 else tq
    W = QPK * HEAD

    def body(q_ref, k_ref, v_ref, o_ref, st_ref, vaug):
        @pl.when(pl.program_id(1) == 0)
        def _():
            vaug[:, :HEAD] = v_ref[...]
            vaug[:, HEAD:] = jnp.ones((T, HEAD), vaug.dtype)

        kk, va = k_ref[...], vaug[...]
        for g in range(QPK):
            cols = slice(g * HEAD, (g + 1) * HEAD)
            for r in range(tq // ts):
                rows = slice(r * ts, (r + 1) * ts)
                s = _dot(q_ref[rows, cols], kk, _NT)
                m = jnp.max(s, axis=-1, keepdims=True)
                ov = _dot(jnp.exp2((s - m) * _ATTN_C2).astype(_MXU), va)
                l = ov[:, HEAD:HEAD + 1]
                o_ref[rows, cols] = (ov[:, :HEAD] * (1.0 / l)).astype(o_ref.dtype)
                st_ref[rows, g:g + 1] = m + jnp.log2(l) * (1.0 / _ATTN_C2)

    qs = BS((tq, W), lambda h, i: (i, h))
    ks = BS((T, HEAD), lambda h, i: (0, h))
    return _call(body, "attn_fwd", grid=(KVH, T // tq), in_specs=[qs, ks, ks],
                 out_specs=[qs, BS((None, tq, QPK), lambda h, i: (h, i, 0))],
                 out_shape=[S((T, TOK), _MXU), S((KVH, T, QPK), _F32)],
                 scratch_shapes=[pltpu.VMEM((T, 2 * HEAD), _MXU)],
                 compiler_params=_cp("parallel", "arbitrary"))(q, k, v)


def _attn_bwd(q, k, v, dcat, o, stat, QPK):
    T, TOK = q.shape
    KV = k.shape[1]
    KVH = KV // HEAD
    tq = _tile(T, 512, 8)
    ts = 256 if tq % 256 == 0 else tq
    nq = T // tq
    W = QPK * HEAD
    scale = HEAD ** -0.5

    def body(q_ref, k_ref, v_ref, do_ref, o_ref, st_ref, dq_ref, dk_ref, dv_ref, dk_acc, dv_acc, ds_all, p_all, q_all, do_all):
        i = pl.program_id(1)

        @pl.when(i == 0)
        def _():
            dk_acc[...] = jnp.zeros_like(dk_acc)
            dv_acc[...] = jnp.zeros_like(dv_acc)

        kk, vv = k_ref[...], v_ref[...]
        for r in range(tq // ts):
            rows = slice(r * ts, (r + 1) * ts)
            for g in range(QPK):
                cols = slice(g * HEAD, (g + 1) * HEAD)
                stack = slice(g * ts, (g + 1) * ts)
                qg = q_ref[rows, cols]
                p = jnp.exp2((_dot(qg, kk, _NT) - st_ref[rows, g:g + 1]) * _ATTN_C2)
                do32 = do_ref[rows, cols]
                do = do32.astype(_MXU)
                delta = jnp.sum(do32 * o_ref[rows, cols].astype(_F32), axis=-1, keepdims=True)
                ds = (p * (_dot(do, vv, _NT) - delta)).astype(_MXU)
                dq_ref[rows, cols] = _dot(ds, kk) * scale
                ds_all[stack, :] = ds
                p_all[stack, :] = p.astype(_MXU)
                q_all[stack, :] = qg
                do_all[stack, :] = do
            dk_acc[...] += _dot(ds_all[...], q_all[...], _TN)
            dv_acc[...] += _dot(p_all[...], do_all[...], _TN)

        @pl.when(i == nq - 1)
        def _():
            dk_ref[...] = dk_acc[...] * scale
            dv_ref[...] = dv_acc[...].astype(dv_ref.dtype)

    qs = BS((tq, W), lambda h, i: (i, h))
    ks = BS((T, HEAD), lambda h, i: (0, h))
    return _call(
        body, "attn_bwd", grid=(KVH, nq), in_specs=[qs, ks, ks, qs, qs, BS((None, tq, QPK), lambda h, i: (h, i, 0))],
        out_specs=[qs, ks, ks], out_shape=[S((T, TOK), _F32), S((T, KV), _F32), S((T, KV), _MXU)],
        scratch_shapes=[pltpu.VMEM((T, HEAD), _F32), pltpu.VMEM((T, HEAD), _F32), pltpu.VMEM((QPK * ts, T), _MXU),
                        pltpu.VMEM((QPK * ts, T), _MXU), pltpu.VMEM((QPK * ts, HEAD), _MXU), pltpu.VMEM((QPK * ts, HEAD), _MXU)],
        compiler_params=_cp("parallel", "arbitrary"),
    )(q, k, v, dcat, o, stat)


def _mem_fwd(z, qblk, kv, gmq, gmk, MEMW):
    T = z.shape[0]
    NM = kv.shape[0]
    tq = _tile(T, 512, 8)
    scale = HEAD ** -0.5

    def body(q_ref, kv_ref, gq_ref, gk_ref, o_ref):
        for h in range(MEMW // HEAD):
            cols = slice(h * HEAD, (h + 1) * HEAD)
            kx = kv_ref[:, cols]
            kn = (kx * _rstd(kx) * gk_ref[...]).astype(_MXU)
            vv = kv_ref[:, MEMW + h * HEAD:MEMW + (h + 1) * HEAD].astype(_MXU)
            qx = q_ref[:, cols]
            qn = (qx * _rstd(qx) * gq_ref[...]).astype(_MXU)
            p = _softmax(_dot(qn, kn, _NT) * scale)
            o_ref[:, cols] = _dot(p.astype(_MXU), vv).astype(o_ref.dtype)

    vec = BS((1, HEAD), lambda i: (0, 0))
    return _call(
        body, "mem_fwd", grid=(T // tq,),
        in_specs=[BS((tq, MEMW), lambda i: (i, qblk)), BS((NM, 2 * MEMW), lambda i: (0, 0)), vec, vec],
        out_specs=BS((tq, MEMW), lambda i: (i, 0)), out_shape=S((T, MEMW), _MXU), compiler_params=_cp("parallel"),
    )(z, kv, gmq.reshape(1, HEAD), gmk.reshape(1, HEAD))


def _mem_bwd(z, qblk, kv, gmq, gmk, dcat, dblk, MEMW):
    T = z.shape[0]
    NM = kv.shape[0]
    tq = _tile(T, 512, 8)
    scale = HEAD ** -0.5

    def body(q_ref, kv_ref, gq_ref, gk_ref, do_ref, dz_ref, dkn_ref, dv_ref, dgq_ref):
        @pl.when(pl.program_id(0) == 0)
        def _():
            dkn_ref[...] = jnp.zeros_like(dkn_ref)
            dv_ref[...] = jnp.zeros_like(dv_ref)
            dgq_ref[...] = jnp.zeros_like(dgq_ref)

        for h in range(MEMW // HEAD):
            cols = slice(h * HEAD, (h + 1) * HEAD)
            kx = kv_ref[:, cols]
            kn = (kx * _rstd(kx) * gk_ref[...]).astype(_MXU)
            vv = kv_ref[:, MEMW + h * HEAD:MEMW + (h + 1) * HEAD].astype(_MXU)
            qx = q_ref[:, cols]
            rq = _rstd(qx)
            qhat = qx * rq
            qn = (qhat * gq_ref[...]).astype(_MXU)
            p = _softmax(_dot(qn, kn, _NT) * scale)
            do = do_ref[:, cols].astype(_MXU)
            dp = _dot(do, vv, _NT)
            ds = (p * (dp - jnp.sum(p * dp, axis=-1, keepdims=True)) * scale).astype(_MXU)
            dqn = _dot(ds, kn)
            dkn_ref[:, cols] += _dot(ds, qn, _TN)
            dv_ref[:, cols] += _dot(p.astype(_MXU), do, _TN)
            dgq_ref[...] += jnp.sum(dqn * qhat, axis=0, keepdims=True)
            dz_ref[:, cols] = _norm_bwd(dqn, qhat, rq, gq_ref[...]).astype(dz_ref.dtype)

    vec = BS((1, HEAD), lambda i: (0, 0))
    kvs = BS((NM, MEMW), lambda i: (0, 0))
    return _call(
        body, "mem_bwd", grid=(T // tq,),
        in_specs=[BS((tq, MEMW), lambda i: (i, qblk)), BS((NM, 2 * MEMW), lambda i: (0, 0)), vec, vec,
                  BS((tq, MEMW), lambda i: (i, dblk))],
        out_specs=[BS((tq, MEMW), lambda i: (i, 0)), kvs, kvs, vec],
        out_shape=[S((T, MEMW), _MXU), S((NM, MEMW), _F32), S((NM, MEMW), _F32), S((1, HEAD), _F32)],
        compiler_params=_cp("arbitrary"),
    )(z, kv, gmq.reshape(1, HEAD), gmk.reshape(1, HEAD), dcat)


def _memkv_bwd(kv, dkn, dv, gmk, MEMW):
    NM = kv.shape[0]

    def body(kv_ref, dkn_ref, dv_ref, gk_ref, dkv_ref, dgk_ref):
        dgk = jnp.zeros((1, HEAD), _F32)
        for h in range(MEMW // HEAD):
            cols = slice(h * HEAD, (h + 1) * HEAD)
            kx = kv_ref[:, cols]
            r = _rstd(kx)
            khat = kx * r
            dkn = dkn_ref[:, cols]
            dgk = dgk + jnp.sum(dkn * khat, axis=0, keepdims=True)
            dkv_ref[:, cols] = _norm_bwd(dkn, khat, r, gk_ref[...]).astype(dkv_ref.dtype)
        dgk_ref[...] = dgk
        dkv_ref[:, MEMW:] = dv_ref[...].astype(dkv_ref.dtype)

    return _call(body, "memkv_bwd", out_shape=[S((NM, 2 * MEMW), _MXU), S((1, HEAD), _F32)],
                 compiler_params=_cp())(kv, dkn, dv, gmk.reshape(1, HEAD))


def _cast_into_full(w, l, sh, idx, dep=None):
    tr, tc = _tile(sh.Rs, 512, 16), _tile(sh.Cs, 2048)
    nr, nc = sh.Rs // tr, sh.Cs // tc
    deps = [] if dep is None else [dep]

    def body(i_ref, c_ref, w_ref, *rest):
        rest[-1][...] = w_ref[...].astype(rest[-1].dtype)

    if sh.by_cols:
        o_map = lambda a, b, si, sc: (a, si[0] * nc + b)
    else:
        o_map = lambda a, b, si, sc: (si[0] * nr + a, b)
    return _call(
        body, "cast_into_full",
        grid_spec=pltpu.PrefetchScalarGridSpec(
            num_scalar_prefetch=2, grid=(nr, nc),
            in_specs=[BS((None, tr, tc), lambda a, b, si, sc: (l, a, b))] + [_ANY] * len(deps), out_specs=BS((tr, tc), o_map)),
        out_shape=S((sh.R, sh.C), _WIRE), compiler_params=_cp("parallel", "parallel"),
    )(*idx, w, *deps)


def _adamw(w, g, m, v, name, l0=0, l1=None, prev=None):
    L, R, C = w.shape
    l1 = L if l1 is None else l1
    tc = _tile(C, 2048)
    tr = _tile(R, max(8, (512 * 1024) // tc), 8)
    c_m = 1.0 / (1.0 - ADAM_B1 ** ADAM_STEP)
    c_v = 1.0 / (1.0 - ADAM_B2 ** ADAM_STEP)

    def body(w_ref, g_ref, m_ref, v_ref, *rest):
        go_ref, d_ref, mo_ref, vo_ref = rest[-4:]
        gv = g_ref[...]
        mn = ADAM_B1 * m_ref[...] + (1.0 - ADAM_B1) * gv
        vn = ADAM_B2 * v_ref[...] + (1.0 - ADAM_B2) * (gv * gv)
        go_ref[...] = gv
        mo_ref[...] = mn
        vo_ref[...] = vn
        d_ref[...] = -ADAM_LR * ((mn * c_m) / (jnp.sqrt(vn * c_v) + ADAM_EPS) + ADAM_WD * w_ref[...])

    blk = BS((None, tr, tc), lambda a, i, j: (l0 + a, i, j))
    prevs = [] if prev is None else list(prev)
    return _call(body, name, grid=(l1 - l0, R // tr, C // tc), in_specs=[blk] * 4 + [_ANY] * len(prevs), out_specs=[blk] * 4,
                 out_shape=[S((L, R, C), _F32)] * 4, input_output_aliases={4 + k: k for k in range(len(prevs))},
                 compiler_params=_cp("parallel", "parallel", "parallel"))(w, g, m, v, *prevs)


def _where_am_i():
    x, y, c = lax.axis_index("x"), lax.axis_index("y"), lax.axis_index("c")
    chips = [(1 - x, y), (x, 1 - y), (1 - x, 1 - y)]
    return x, y, c, 2 * x + y, chips, [2 * cx + cy for cx, cy in chips]


class _Shard:
    def __init__(self, R, C, by_cols):
        self.R, self.C, self.by_cols = R, C, by_cols
        self.Rs, self.Cs = (R, C // 4) if by_cols else (R // 4, C)
        self.Rh = self.Rs // 2
        self.Q = R // 8

    def full_piece(self, ref, j, cc):
        if self.by_cols:
            return ref.at[pl.ds(cc * self.Rh, self.Rh), pl.ds(_mo(j * self.Cs, 128), self.Cs)]
        return ref.at[pl.ds(_mo(j * self.Rs + cc * self.Rh, 16), self.Rh), :]

    def full_shard(self, ref, j):
        if self.by_cols:
            return ref.at[:, pl.ds(_mo(j * self.Cs, 128), self.Cs)]
        return ref.at[pl.ds(_mo(j * self.Rs, 16), self.Rs), :]

    def shard_half(self, ref, cc):
        return ref.at[pl.ds(_mo(cc * self.Rh, 16), self.Rh), :]

    def half_piece(self, ref, j):
        if self.by_cols:
            return ref.at[:, pl.ds(_mo(j * self.Cs, 128), self.Cs)]
        return ref.at[pl.ds(_mo(j * self.Rh, 16), self.Rh), :]


def _remote(src, dst, ssem, rsem, dev):
    return pltpu.make_async_remote_copy(src_ref=src, dst_ref=dst, send_sem=ssem, recv_sem=rsem, device_id=dev, device_id_type=_MESH)


_HBM = pl.BlockSpec(memory_space=pltpu.HBM)
_SEM = pl.BlockSpec(memory_space=pltpu.SEMAPHORE)
_EFFECT = pltpu.SideEffectType.DATAFLOW_SIDE_EFFECTING


def _in_hbm(a):
    return pltpu.with_memory_space_constraint(a, pltpu.HBM)


def _gather_start(fulls, shs, name):
    n = len(fulls)

    def body(*refs):
        bufs = refs[:n]
        send_sems, recv_sems = refs[n], refs[n + 1]
        token = refs[-1]
        x, y, c, me, chips, chip_ids = _where_am_i()
        for t in range(n):
            mine = shs[t].full_piece(bufs[t], me, c)
            for r in range(3):
                _remote(mine, mine, send_sems.at[3 * t + r], recv_sems.at[3 * t + r], (*chips[r], c)).start()
        token[...] = jnp.zeros_like(token)

    out = pl.pallas_call(
        body, name=name, in_specs=[_HBM] * n,
        out_shape=(pltpu.SemaphoreType.DMA((3 * n,)), pltpu.SemaphoreType.DMA((3 * n,)), *[pltpu.HBM(f.shape, f.dtype) for f in fulls],
                   S((8, 128), _F32)),
        out_specs=(_SEM, _SEM, *[_HBM] * n, pl.BlockSpec(memory_space=pltpu.VMEM)),
        input_output_aliases={t: 2 + t for t in range(n)},
        compiler_params=pltpu.CompilerParams(has_side_effects=_EFFECT), **_KW,
    )(*[_in_hbm(f) for f in fulls])
    return out[0], out[1], list(out[2:2 + n]), out[-1]


def _gather_wait(fulls, send_sems, recv_sems, after, shs, name):
    n = len(fulls)

    def body(*refs):
        bufs = refs[:n]
        ssem, rsem = refs[n], refs[n + 1]
        x, y, c, me, chips, chip_ids = _where_am_i()
        for t in range(n):
            mine = shs[t].full_piece(bufs[t], me, c)
            for r in range(3):
                _remote(mine, mine, ssem.at[3 * t + r], rsem.at[3 * t + r], (*chips[r], c)).wait_send()
        for t in range(n):
            for r in range(3):
                piece = shs[t].full_piece(bufs[t], chip_ids[r], c)
                _remote(piece, piece, ssem.at[3 * t + r], rsem.at[3 * t + r], (*chips[r], c)).wait_recv()

    out = pl.pallas_call(
        body, name=name, in_specs=[*[_HBM] * n, _SEM, _SEM, _ANY], out_specs=[_HBM] * n,
        out_shape=[pltpu.HBM(f.shape, f.dtype) for f in fulls], input_output_aliases={t: t for t in range(n)},
        compiler_params=pltpu.CompilerParams(has_side_effects=_EFFECT), **_KW,
    )(*fulls, send_sems, recv_sems, after)
    return list(out)


def _gather_pass_on(fulls, shs, name):
    n = len(fulls)

    def body(*refs):
        bufs = refs[n:2 * n]
        send_sems, recv_sems = refs[2 * n:]
        x, y, c, me, chips, chip_ids = _where_am_i()
        sib = (x, y, 1 - c)
        cps = []
        for t in range(n):
            for r in range(3):
                piece = shs[t].full_piece(bufs[t], chip_ids[r], c)
                cps.append(_remote(piece, piece, send_sems.at[t, r], recv_sems.at[t, r], sib))
        for cp in cps:
            cp.start()
        for t in range(n):
            for r in range(3):
                piece = shs[t].full_piece(bufs[t], chip_ids[r], 1 - c)
                _remote(piece, piece, send_sems.at[t, r], recv_sems.at[t, r], sib).wait_recv()
        for cp in cps:
            cp.wait_send()

    return _call(
        body, name, in_specs=[_ANY] * n, out_specs=[_ANY] * n, out_shape=[S(f.shape, f.dtype) for f in fulls],
        input_output_aliases={t: t for t in range(n)},
        scratch_shapes=[pltpu.SemaphoreType.DMA((n, 3)), pltpu.SemaphoreType.DMA((n, 3))],
        compiler_params=pltpu.CompilerParams(has_side_effects=True),
    )(*fulls)


def _pass_on_copies(bufs, shs, send_sems, recv_sems):
    x, y, c, me, chips, chip_ids = _where_am_i()
    sib = (x, y, 1 - c)
    out, back = [], []
    for t in range(len(bufs)):
        for r in range(3):
            piece = shs[t].full_piece(bufs[t], chip_ids[r], c)
            out.append(_remote(piece, piece, send_sems.at[3 * t + r], recv_sems.at[3 * t + r], sib))
            other = shs[t].full_piece(bufs[t], chip_ids[r], 1 - c)
            back.append(_remote(other, other, send_sems.at[3 * t + r], recv_sems.at[3 * t + r], sib))
    return out, back


def _pass_on_start(fulls, shs, name):
    n = len(fulls)

    def body(*refs):
        for cp in _pass_on_copies(refs[:n], shs, refs[n], refs[n + 1])[0]:
            cp.start()
        refs[-1][...] = jnp.zeros_like(refs[-1])

    out = pl.pallas_call(
        body, name=name, in_specs=[_HBM] * n,
        out_shape=(pltpu.SemaphoreType.DMA((3 * n,)), pltpu.SemaphoreType.DMA((3 * n,)), *[pltpu.HBM(f.shape, f.dtype) for f in fulls],
                   S((8, 128), _F32)),
        out_specs=(_SEM, _SEM, *[_HBM] * n, pl.BlockSpec(memory_space=pltpu.VMEM)),
        input_output_aliases={t: 2 + t for t in range(n)},
        compiler_params=pltpu.CompilerParams(has_side_effects=_EFFECT), **_KW,
    )(*[_in_hbm(f) for f in fulls])
    return out[0], out[1], list(out[2:2 + n]), out[-1]


def _pass_on_wait(fulls, send_sems, recv_sems, after, shs, name):
    n = len(fulls)

    def body(*refs):
        out, back = _pass_on_copies(refs[:n], shs, refs[n], refs[n + 1])
        for cp in out:
            cp.wait_send()
        for cp in back:
            cp.wait_recv()

    out = pl.pallas_call(
        body, name=name, in_specs=[*[_HBM] * n, _SEM, _SEM, _ANY], out_specs=[_HBM] * n,
        out_shape=[pltpu.HBM(f.shape, f.dtype) for f in fulls], input_output_aliases={t: t for t in range(n)},
        compiler_params=pltpu.CompilerParams(has_side_effects=_EFFECT), **_KW,
    )(*fulls, send_sems, recv_sems, after)
    return list(out)


def _rs_pair_copies(ins, outs, shs, send_sems, recv_sems):
    x, y, c, *_ = _where_am_i()
    sib = (x, y, 1 - c)
    cps = []
    for t in range(len(ins)):
        sh = shs[t]
        if sh.by_cols:
            cps.append(_remote(ins[t].at[pl.ds((1 - c) * sh.Rh, sh.Rh), :], outs[t], send_sems.at[4 * t], recv_sems.at[4 * t], sib))
        else:
            for j in range(4):
                cps.append(_remote(sh.full_piece(ins[t], j, 1 - c), sh.half_piece(outs[t], j),
                                   send_sems.at[4 * t + j], recv_sems.at[4 * t + j], sib))
    return cps


def _rs_pair_start(dws, shs, name):
    n = len(dws)
    lands = [lax.empty((sh.R // 2, sh.C), _WIRE) for sh in shs]

    def body(*refs):
        for cp in _rs_pair_copies(refs[:n], refs[n:2 * n], shs, refs[2 * n], refs[2 * n + 1]):
            cp.start()
        refs[-1][...] = jnp.zeros_like(refs[-1])

    out = pl.pallas_call(
        body, name=name, in_specs=[_HBM] * (2 * n),
        out_shape=(pltpu.SemaphoreType.DMA((4 * n,)), pltpu.SemaphoreType.DMA((4 * n,)),
                   *[pltpu.HBM(a.shape, a.dtype) for a in (*dws, *lands)], S((8, 128), _F32)),
        out_specs=(_SEM, _SEM, *[_HBM] * (2 * n), pl.BlockSpec(memory_space=pltpu.VMEM)),
        input_output_aliases={t: 2 + t for t in range(2 * n)},
        compiler_params=pltpu.CompilerParams(has_side_effects=_EFFECT), **_KW,
    )(*[_in_hbm(a) for a in (*dws, *lands)])
    return out[0], out[1], list(out[2:2 + n]), list(out[2 + n:2 + 2 * n]), out[-1]


def _rs_pair_wait(dws, lands, send_sems, recv_sems, after, shs, name):
    n = len(dws)

    def body(*refs):
        cps = _rs_pair_copies(refs[:n], refs[n:2 * n], shs, refs[2 * n], refs[2 * n + 1])
        for cp in cps:
            cp.wait_send()
        for cp in cps:
            cp.wait_recv()

    out = pl.pallas_call(
        body, name=name, in_specs=[*[_HBM] * (2 * n), _SEM, _SEM, _ANY], out_specs=[_HBM] * (2 * n),
        out_shape=[pltpu.HBM(a.shape, a.dtype) for a in (*dws, *lands)], input_output_aliases={t: t for t in range(2 * n)},
        compiler_params=pltpu.CompilerParams(has_side_effects=_EFFECT), **_KW,
    )(*dws, *lands, send_sems, recv_sems, after)
    return list(out[n:])


def _rs_pair_add(dw32, recv, sh, idx):
    tr, tc = _tile(sh.Q, 512, 16), _tile(sh.C, 2048)
    nb = sh.Q // tr

    def body(i_ref, c_ref, a_ref, b_ref, ow_ref):
        ow_ref[...] = (a_ref[...] + b_ref[...].astype(_F32)).astype(ow_ref.dtype)

    if sh.by_cols:
        a_map = lambda j, i, b, si, sc: (sc[0] * 4 * nb + j * nb + i, b)
    else:
        a_map = lambda j, i, b, si, sc: (j * 2 * nb + sc[0] * nb + i, b)
    h_spec = BS((tr, tc), lambda j, i, b, si, sc: (j * nb + i, b))
    return _call(
        body, "rs_pair_add",
        grid_spec=pltpu.PrefetchScalarGridSpec(num_scalar_prefetch=2, grid=(4, nb, sh.C // tc),
                                               in_specs=[BS((tr, tc), a_map), h_spec], out_specs=h_spec),
        out_shape=S((sh.R // 2, sh.C), _WIRE), compiler_params=_cp("parallel", "parallel", "parallel"),
    )(*idx, dw32, recv)


def _rs_chip_start(pws, shs, name):
    n = len(pws)
    lands = [lax.empty((3, sh.Rh, sh.Cs), _WIRE) for sh in shs]

    def body(*refs):
        ins, lnd = refs[:n], refs[n:2 * n]
        send_sems, recv_sems = refs[2 * n], refs[2 * n + 1]
        token = refs[-1]
        x, y, c, me, chips, chip_ids = _where_am_i()
        for t in range(n):
            for r in range(3):
                _remote(shs[t].half_piece(ins[t], chip_ids[r]), lnd[t].at[r], send_sems.at[3 * t + r], recv_sems.at[3 * t + r],
                        (*chips[r], c)).start()
        token[...] = jnp.zeros_like(token)

    out = pl.pallas_call(
        body, name=name, in_specs=[_HBM] * (2 * n),
        out_shape=(pltpu.SemaphoreType.DMA((3 * n,)), pltpu.SemaphoreType.DMA((3 * n,)),
                   *[pltpu.HBM(a.shape, a.dtype) for a in (*pws, *lands)], S((8, 128), _F32)),
        out_specs=(_SEM, _SEM, *[_HBM] * (2 * n), pl.BlockSpec(memory_space=pltpu.VMEM)),
        input_output_aliases={t: 2 + t for t in range(2 * n)},
        compiler_params=pltpu.CompilerParams(has_side_effects=_EFFECT), **_KW,
    )(*[_in_hbm(a) for a in (*pws, *lands)])
    return out[0], out[1], list(out[2:2 + n]), list(out[2 + n:2 + 2 * n]), out[-1]


def _rs_chip_wait(pws, lands, send_sems, recv_sems, after, shs, name):
    n = len(pws)
    after = list(after) if isinstance(after, (list, tuple)) else [after]

    def body(*refs):
        ins, lnd = refs[:n], refs[n:2 * n]
        ssem, rsem = refs[2 * n], refs[2 * n + 1]
        x, y, c, me, chips, chip_ids = _where_am_i()
        for t in range(n):
            for r in range(3):
                cp = _remote(shs[t].half_piece(ins[t], chip_ids[r]), lnd[t].at[r], ssem.at[3 * t + r], rsem.at[3 * t + r], (*chips[r], c))
                cp.wait_send()
        for t in range(n):
            for r in range(3):
                cp = _remote(shs[t].half_piece(ins[t], chip_ids[r]), lnd[t].at[r], ssem.at[3 * t + r], rsem.at[3 * t + r], (*chips[r], c))
                cp.wait_recv()

    out = pl.pallas_call(
        body, name=name, in_specs=[*[_HBM] * (2 * n), _SEM, _SEM, *[_ANY] * len(after)], out_specs=[_HBM] * (2 * n),
        out_shape=[pltpu.HBM(a.shape, a.dtype) for a in (*pws, *lands)], input_output_aliases={t: t for t in range(2 * n)},
        compiler_params=pltpu.CompilerParams(has_side_effects=_EFFECT), **_KW,
    )(*pws, *lands, send_sems, recv_sems, *after)
    return list(out[n:])


def _rs_chip_add(dw32, pair, recv, sh, idx, g_prev, l, L):
    tr, tc = _tile(sh.Rh, 512, 16), _tile(sh.Cs, 2048)
    nr, nc = sh.Rh // tr, sh.Cs // tc

    def body(i_ref, c_ref, d_ref, a_ref, b_ref, *rest):
        rest[-1][...] = ((d_ref[...] + a_ref[...].astype(_F32)) + b_ref[0].astype(_F32) + b_ref[1].astype(_F32)
                         + b_ref[2].astype(_F32))

    if sh.by_cols:
        d_map = lambda a, b, si, sc: (sc[0] * nr + a, si[0] * nc + b)
        a_map = lambda a, b, si, sc: (a, si[0] * nc + b)
    else:
        d_map = lambda a, b, si, sc: ((2 * si[0] + sc[0]) * nr + a, b)
        a_map = lambda a, b, si, sc: (si[0] * nr + a, b)
    in_specs = [BS((tr, tc), d_map), BS((tr, tc), a_map), BS((3, tr, tc), lambda a, b, si, sc: (0, a, b))]
    args = [*idx, dw32, pair, recv]
    if g_prev is not None:
        in_specs.append(_ANY)
        args.append(g_prev)
    return _call(
        body, "rs_chip_add",
        grid_spec=pltpu.PrefetchScalarGridSpec(num_scalar_prefetch=2, grid=(nr, nc), in_specs=in_specs,
                                               out_specs=BS((None, tr, tc), lambda a, b, si, sc: (l, sc[0] * nr + a, b))),
        out_shape=S((L, sh.Rs, sh.Cs), _F32), input_output_aliases={} if g_prev is None else {5: 0},
        compiler_params=_cp("parallel", "parallel"),
    )(*args)


def _rs_pair_share(gs, ls, shs, name):
    n = len(gs)

    def body(*refs):
        bufs = refs[n:2 * n]
        send_sems, recv_sems = refs[2 * n:]
        x, y, c, *_ = _where_am_i()
        sib = (x, y, 1 - c)
        cps = []
        for t in range(n):
            mine = shs[t].shard_half(bufs[t].at[ls[t]], c)
            cps.append(_remote(mine, mine, send_sems.at[t], recv_sems.at[t], sib))
        for cp in cps:
            cp.start()
        for t in range(n):
            other = shs[t].shard_half(bufs[t].at[ls[t]], 1 - c)
            _remote(other, other, send_sems.at[t], recv_sems.at[t], sib).wait_recv()
        for cp in cps:
            cp.wait_send()

    return _call(
        body, name, in_specs=[_ANY] * n, out_specs=[_ANY] * n, out_shape=[S(g.shape, g.dtype) for g in gs],
        input_output_aliases={t: t for t in range(n)},
        scratch_shapes=[pltpu.SemaphoreType.DMA((n,)), pltpu.SemaphoreType.DMA((n,))],
        compiler_params=pltpu.CompilerParams(has_side_effects=True),
    )(*gs)


def _all_reduce_small(xs, dep=None):
    M = xs.shape[0]
    deps = [] if dep is None else [dep]

    def body(x_ref, *rest):
        tot_ref, out_ref, send_sems, recv_sems, local_sem = rest[len(deps):]
        x, y, c, me, chips, chip_ids = _where_am_i()
        sib = (x, y, 1 - c)

        def rows(dev):
            return out_ref.at[pl.ds(_mo((4 * dev[0] + 2 * dev[1] + dev[2]) * M, 8), M), :]

        def copy(k, block, to, src=None):
            return _remote(rows(block) if src is None else src, rows(block), send_sems.at[k], recv_sems.at[k], to)

        mine = pltpu.make_async_copy(x_ref, rows((x, y, c)), local_sem)
        mine.start()
        first = [copy(0, (x, y, c), sib, src=x_ref)]
        first += [copy(1 + j, (x, y, c), (*chip, c), src=x_ref) for j, chip in enumerate(chips)]
        for cp in first:
            cp.start()
        passed = [copy(4 + j, (*chip, c), sib) for j, chip in enumerate(chips)]
        for j, chip in enumerate(chips):
            copy(1 + j, (*chip, c), (x, y, c)).wait_recv()
            passed[j].start()
        copy(0, sib, (x, y, c)).wait_recv()
        for j, chip in enumerate(chips):
            copy(4 + j, (*chip, 1 - c), (x, y, c)).wait_recv()
        for cp in first + passed:
            cp.wait_send()
        mine.wait()
        tot = out_ref[pl.ds(0, M), :]
        for d in range(1, 8):
            tot = tot + out_ref[pl.ds(d * M, M), :]
        tot_ref[...] = tot

    vm = pl.BlockSpec(memory_space=pltpu.VMEM)
    return _call(
        body, "all_reduce_small", in_specs=[vm] + [_ANY] * len(deps), out_specs=[vm, vm],
        out_shape=[S((M, 128), _F32), S((8 * M, 128), _F32)],
        scratch_shapes=[pltpu.SemaphoreType.DMA((7,)), pltpu.SemaphoreType.DMA((7,)), pltpu.SemaphoreType.DMA],
        compiler_params=_cp(has_side_effects=True),
    )(xs, *deps)[0]


def _reduce_scatter_begin(dws, shs, l):
    ssem, rsem, dww, lands, token = _rs_pair_start([d[1] for d in dws], shs, f"rs_pair_start_{l}")
    return ([d[0] for d in dws], dww, lands, ssem, rsem), token


def _reduce_scatter_middle(state, after, shs, idx, l):
    dw32s, dww, lands, ssem, rsem = state
    recv_a = _rs_pair_wait(dww, lands, ssem, rsem, after, shs, f"rs_pair_wait_{l}")
    pws = [_rs_pair_add(d32, ra, sh, idx) for d32, ra, sh in zip(dw32s, recv_a, shs)]
    ssem, rsem, pws, lands, token = _rs_chip_start(pws, shs, f"rs_chip_start_{l}")
    return (dw32s, recv_a, pws, lands, ssem, rsem), token


def _reduce_scatter_end(state, after, tensors, shs, gstack, idx, l):
    dw32s, recv_a, pws, lands, ssem, rsem = state
    recv_b = _rs_chip_wait(pws, lands, ssem, rsem, after, shs, f"rs_chip_wait_{l}")
    gs = [_rs_chip_add(d32, ra, rb, sh, idx, gstack[name], i, L)
          for d32, ra, rb, sh, (name, i, L) in zip(dw32s, recv_a, recv_b, shs, tensors)]
    gs = _rs_pair_share(gs, [i for _, i, _ in tensors], shs, "rs_pair_share")
    for (name, _, _), g in zip(tensors, gs):
        gstack[name] = g


def _pack(parts):
    out = []
    for p in parts:
        p2 = p.reshape(-1, 128)
        pad = (-p2.shape[0]) % 8
        out.append(jnp.pad(p2, ((0, pad), (0, 0))) if pad else p2)
    return jnp.concatenate(out, axis=0)


def _unpack(packed, like):
    out, at = [], 0
    for p in like:
        n = p.size // 128
        out.append(packed[at:at + n].reshape(p.shape))
        at += n + ((-n) % 8)
    return out


def kernel(x, mem, g_mix, g_ffn, w_in_a, g_v_a, w_spatial, b_spatial, w_in_b, g_q_b, g_k_b, g_mem, w_mem_kv, g_mq, g_mk, w_out, w_gate_up, w_down, loss_target, m_g_mix, m_g_ffn, m_w_in_a, m_g_v_a, m_w_spatial, m_b_spatial, m_w_in_b, m_g_q_b, m_g_k_b, m_g_mem, m_w_mem_kv, m_g_mq, m_g_mk, m_w_out, m_w_gate_up, m_w_down, v_g_mix, v_g_ffn, v_w_in_a, v_g_v_a, v_w_spatial, v_b_spatial, v_w_in_b, v_g_q_b, v_g_k_b, v_g_mem, v_w_mem_kv, v_g_mq, v_g_mk, v_w_out, v_w_gate_up, v_w_down):
    xs = x[0]
    mem2 = mem[0]
    target = loss_target[0]
    T, D = xs.shape
    depth = g_mix.shape[0]
    MEMW = w_mem_kv.shape[2] // 2
    TOK = D - MEMW
    KV = (w_in_b.shape[2] * 4 - TOK - MEMW) // 2
    QPK = TOK // KV
    F = w_gate_up.shape[2] * 4 // 2

    idx = ((2 * lax.axis_index("x") + lax.axis_index("y")).astype(jnp.int32).reshape(1), lax.axis_index("c").astype(jnp.int32).reshape(1))

    big = {
        "w_in_a": (w_in_a, _Shard(D, w_in_a.shape[2] * 4, True)),
        "w_in_b": (w_in_b, _Shard(D, w_in_b.shape[2] * 4, True)),
        "w_mem_kv": (w_mem_kv, _Shard(D, 2 * MEMW, False)),
        "w_out": (w_out, _Shard(D, D, False)),
        "w_gate_up": (w_gate_up, _Shard(D, 2 * F, True)),
        "w_down": (w_down, _Shard(F, D, False)),
    }

    def layer_tensors(l):
        n_in = "w_in_a" if l % 2 == 0 else "w_in_b"
        return [(n_in, l // 2, big[n_in][0].shape[0])] + [(n, l, depth) for n in ("w_mem_kv", "w_out", "w_gate_up", "w_down")]

    def layer_shards(l):
        return [big[n][1] for n, _, _ in layer_tensors(l)]

    full = {n: [None] * w.shape[0] for n, (w, _) in big.items()}
    flying = {}

    def start_layer(l, dep):
        tens = layer_tensors(l)
        token = dep
        for gi, group in enumerate([tens[:2], tens[2:3], tens[3:4], tens[4:]]):
            shs = [big[n][1] for n, _, _ in group]
            bufs = [_cast_into_full(big[n][0], i, big[n][1], idx, dep=token) for n, i, _ in group]
            ssem, rsem, bufs, token = _gather_start(bufs, shs, f"gather_start_{l}_{gi}")
            for n, i, _ in group:
                flying[(n, i)] = dict(group=group, shs=shs, state=(ssem, rsem, bufs), name=f"{l}_{gi}", passing=False)
        return token

    def land(members, after):
        for n, i in members:
            fl = flying[(n, i)]
            ssem, rsem, bufs = fl["state"]
            bufs = _gather_wait(bufs, ssem, rsem, after, fl["shs"], "gather_wait_" + fl["name"])
            ssem, rsem, bufs, after = _pass_on_start(bufs, fl["shs"], "pass_on_start_" + fl["name"])
            fl.update(state=(ssem, rsem, bufs), passing=True)
        return after

    def weight(n, i, after):
        if full[n][i] is None:
            fl = flying[(n, i)]
            ssem, rsem, bufs = fl["state"]
            if fl["passing"]:
                bufs = _pass_on_wait(bufs, ssem, rsem, after, fl["shs"], "pass_on_wait_" + fl["name"])
            else:
                bufs = _gather_wait(bufs, ssem, rsem, after, fl["shs"], "gather_wait_" + fl["name"])
                bufs = _gather_pass_on(bufs, fl["shs"], "gather_pass_on")
            for (m, j, _), b in zip(fl["group"], bufs):
                full[m][j] = b
        return full[n][i]

    after = None
    for l in range(depth):
        after = start_layer(l, after)
    tabs = _rope_tables(T)

    saved = []
    xc = xs
    for l in range(depth):
        is_a = l % 2 == 0
        li = l // 2
        w_in = weight("w_in_a" if is_a else "w_in_b", li, after)
        h, ht = _rmsnorm_fwd(xc, g_mix[l], "rmsnorm_fwd", transposed=True)
        z = _mm_nn("mm_in", h, w_in, pm=2048, pn=512)
        st = dict(x=xc, ht=ht, z=z)
        if is_a:
            ws_m = w_spatial[li].astype(_MXU)
            st["ws_m"], st["wst_m"], st["b_t"] = ws_m, jnp.swapaxes(ws_m, 1, 2), b_spatial[li].T
            tok = _mixer_a_fwd(z, g_v_a[li], ws_m, st["b_t"], TOK)
            qblk = 2 * TOK // MEMW
        else:
            q, k, v = _qk_rope_fwd(z, g_q_b[li], g_k_b[li], tabs, TOK, KV)
            tok, stat = _attn_fwd(q, k, v, QPK)
            st["q"], st["k"], st["v"], st["stat"] = q, k, v, stat
            qblk = (TOK + 2 * KV) // MEMW
        mem_n = _rmsnorm_fwd(mem2, g_mem[l], "rmsnorm_mem")
        kv = _mm_nn("mm_memkv", mem_n, weight("w_mem_kv", l, z))
        mo = _mem_fwd(z, qblk, kv, g_mq[l], g_mk[l], MEMW)
        cat = jnp.concatenate([tok, mo], axis=1)
        token = land([("w_out", l), ("w_gate_up", l), ("w_down", l)], cat) if l > 0 else None
        x1 = _mm_nn("mm_out", cat, weight("w_out", l, cat), add=xc, dep=token)
        h2, h2t = _rmsnorm_fwd(x1, g_ffn[l], "rmsnorm_fwd", transposed=True)
        act, gu = _ffn_gate_up(h2, weight("w_gate_up", l, h2))
        w_down_l = weight("w_down", l, act)
        token = land([layer_tensors(l + 1)[0][:2]], act) if l + 1 < depth else None
        xc = _mm_nn("mm_down", act, w_down_l, add=x1, pm=512, pn=1024, pk=8192, dep=token)
        after = xc
        st.update(mem_n=mem_n, kv=kv, qblk=qblk, cat=cat, x1=x1, h2t=h2t, act=act, gu=gu)
        saved.append(st)

    dx, dxm, sq = _loss_head(xc, target)
    loss = lax.psum(sq[0, 0] * (0.5 / D), ("x", "y", "c"))

    gsm = {n: [None] * len(a) for n, a in dict(g_mix=g_mix, g_ffn=g_ffn, g_v_a=g_v_a, w_spatial=w_spatial, b_spatial=b_spatial,
                                                g_q_b=g_q_b, g_k_b=g_k_b, g_mem=g_mem, g_mq=g_mq, g_mk=g_mk).items()}
    gstack = {n: None for n in big}
    pairing, chipping, token = None, None, None

    def advance(after):
        nonlocal pairing, chipping
        state, tok = _reduce_scatter_middle(pairing[0], after, layer_shards(pairing[1]), idx, pairing[1])
        if chipping is not None:
            _reduce_scatter_end(chipping[0], tok, layer_tensors(chipping[1]), layer_shards(chipping[1]), gstack, idx, chipping[1])
        pairing, chipping = None, (state, pairing[1])
        return tok

    for l in reversed(range(depth)):
        st = saved[l]
        is_a = l % 2 == 0
        li = l // 2
        gbig = {}
        dgu = _ffn_dact(dxm, full["w_down"][l], st["gu"], dep=token)
        token = advance(dgu) if pairing is not None else None
        gbig["w_down"] = _mm_tn_dual("mm_dw_down", st["act"], dxm, pm=512, pn=1024, pk=4096)
        dh2 = _ffn_dh(dgu, full["w_gate_up"][l], dep=token)
        gbig["w_gate_up"] = _ffn_dwgu(st["h2t"], dgu)
        dx, dxm, dg = _rmsnorm_bwd(st["x1"], g_ffn[l], dh2, dx, "rmsnorm_bwd")
        gsm["g_ffn"][l] = dg[0]
        dcat = _mm_nt("mm_dcat", dxm, full["w_out"][l])
        gbig["w_out"] = _mm_tn_dual("mm_dw_out", st["cat"], dxm, pm=512, pn=1024, pk=4096)
        dzq, dkn, dvm, dgq = _mem_bwd(st["z"], st["qblk"], st["kv"], g_mq[l], g_mk[l], dcat, TOK // MEMW, MEMW)
        dkv, dgk = _memkv_bwd(st["kv"], dkn, dvm, g_mk[l], MEMW)
        gsm["g_mq"][l], gsm["g_mk"][l] = dgq[0], dgk[0]
        gbig["w_mem_kv"] = _mm_tn_dual("mm_dw_memkv", st["mem_n"], dkv)
        dmem_n = _mm_nt("mm_dmemn", dkv, full["w_mem_kv"][l])
        gsm["g_mem"][l] = _rmsnorm_bwd(mem2, g_mem[l], dmem_n, None, "rmsnorm_bwd_mem")[2][0]
        if is_a:
            dz_tok, dws, dbs, dgv = _mixer_a_bwd(st["z"], dcat, g_v_a[li], st["ws_m"], st["wst_m"], st["b_t"], TOK)
            gsm["w_spatial"][li], gsm["b_spatial"][li], gsm["g_v_a"][li] = dws, dbs[:, :, 0], dgv[0]
            dz = jnp.concatenate([dz_tok, dzq], axis=1)
        else:
            dq, dk, dv = _attn_bwd(st["q"], st["k"], st["v"], dcat, st["cat"], st["stat"], QPK)
            dz_qk, dgq_b, dgk_b = _qk_rope_bwd(st["z"], dq, dk, g_q_b[li], g_k_b[li], tabs, TOK, KV)
            gsm["g_q_b"][li], gsm["g_k_b"][li] = dgq_b[0], dgk_b[0]
            dz = jnp.concatenate([dz_qk, dv, dzq], axis=1)
        n_in = "w_in_a" if is_a else "w_in_b"
        dh = _mm_nt("mm_dh", dz, full[n_in][li])
        gbig[n_in] = _mm_nn_dual("mm_dw_in", st["ht"], dz, pm=1024, pn=512, pk=4096)
        dx, dxm, dg = _rmsnorm_bwd(st["x"], g_mix[l], dh, dx, "rmsnorm_bwd")
        gsm["g_mix"][l] = dg[0]
        state, token = _reduce_scatter_begin([gbig[n] for n, _, _ in layer_tensors(l)], layer_shards(l), l)
        pairing = (state, l)

    small = ["g_mix", "g_ffn", "g_v_a", "w_spatial", "b_spatial", "g_q_b", "g_k_b", "g_mem", "g_mq", "g_mk"]
    env = dict(g_mix=g_mix, g_ffn=g_ffn, g_v_a=g_v_a, w_spatial=w_spatial, b_spatial=b_spatial, g_q_b=g_q_b, g_k_b=g_k_b,
               g_mem=g_mem, g_mq=g_mq, g_mk=g_mk,
               m_g_mix=m_g_mix, m_g_ffn=m_g_ffn, m_g_v_a=m_g_v_a, m_w_spatial=m_w_spatial, m_b_spatial=m_b_spatial,
               m_g_q_b=m_g_q_b, m_g_k_b=m_g_k_b, m_g_mem=m_g_mem, m_g_mq=m_g_mq, m_g_mk=m_g_mk,
               v_g_mix=v_g_mix, v_g_ffn=v_g_ffn, v_g_v_a=v_g_v_a, v_w_spatial=v_w_spatial, v_b_spatial=v_b_spatial,
               v_g_q_b=v_g_q_b, v_g_k_b=v_g_k_b, v_g_mem=v_g_mem, v_g_mq=v_g_mq, v_g_mk=v_g_mk,
               m_w_in_a=m_w_in_a, m_w_in_b=m_w_in_b, m_w_mem_kv=m_w_mem_kv, m_w_out=m_w_out, m_w_gate_up=m_w_gate_up, m_w_down=m_w_down,
               v_w_in_a=v_w_in_a, v_w_in_b=v_w_in_b, v_w_mem_kv=v_w_mem_kv, v_w_out=v_w_out, v_w_gate_up=v_w_gate_up, v_w_down=v_w_down)
    like = [env[n] for n in small]
    g_small = _all_reduce_small(_pack([jnp.stack(gsm[n]) for n in small]), dep=token)

    res = {}
    outs = _adamw(_pack(like)[None], g_small[None], _pack([env["m_" + n] for n in small])[None],
                  _pack([env["v_" + n] for n in small])[None], "adamw_small")
    unpacked = [_unpack(o[0], like) for o in outs]
    for k, n in enumerate(small):
        res[n] = [u[k] for u in unpacked]
    advance(outs[1])
    pending = chipping
    last = {n: i for n, i, _ in layer_tensors(pending[1])}
    early = {}
    for n, (w, _) in big.items():
        L = w.shape[0]
        if n not in last:
            res[n] = _adamw(w, gstack[n], env["m_" + n], env["v_" + n], "adamw_" + n)
        elif L > 1:
            assert last[n] == 0
            early[n] = _adamw(w, gstack[n], env["m_" + n], env["v_" + n], "adamw_early_" + n, l0=1)
    done = [o[1] for o in early.values()] + [res[n][1] for n in big if n in res] + [res[small[0]][1]]
    _reduce_scatter_end(pending[0], done, layer_tensors(pending[1]), layer_shards(pending[1]), gstack, idx, pending[1])
    for n in last:
        res[n] = _adamw(big[n][0], gstack[n], env["m_" + n], env["v_" + n], "adamw_last_" + n, l0=0, l1=1, prev=early.get(n))

    order = ["g_mix", "g_ffn", "w_in_a", "g_v_a", "w_spatial", "b_spatial", "w_in_b", "g_q_b", "g_k_b", "g_mem", "w_mem_kv",
             "g_mq", "g_mk", "w_out", "w_gate_up", "w_down"]
    return (loss, dx.reshape(1, T, D), *[res[n][0] for n in order], *[res[n][1] for n in order],
            *[res[n][2] for n in order], *[res[n][3] for n in order])
```

```python
import jax
import jax.numpy as jnp
import numpy as np
from jax import lax
from jax.experimental import pallas as pl
from jax.experimental.pallas import tpu as pltpu

_F32 = jnp.float32
_MXU = jnp.bfloat16
_WIRE = jnp.bfloat16
_KW = {}

EPS = 1e-6
HEAD = 128
CHUNK = 128
GRID_W = 64
ROPE_THETA = 10000.0
ADAM_LR, ADAM_B1, ADAM_B2, ADAM_EPS, ADAM_WD, ADAM_STEP = 0.001, 0.9, 0.999, 1e-08, 0.01, 10
_SQRT_HALF = float(np.sqrt(0.5))
_INV_SQRT_2PI = float(1.0 / np.sqrt(2.0 * np.pi))
_VMEM_LIMIT = 56 * 1024 * 1024
_MESH = pl.DeviceIdType.MESH

_NN = (((1,), (0,)), ((), ()))
_NT = (((1,), (1,)), ((), ()))
_TN = (((0,), (0,)), ((), ()))

S = jax.ShapeDtypeStruct
BS = pl.BlockSpec
_ANY = pl.BlockSpec(memory_space=pl.ANY)


def _tile(n, pref, mult=128):
    if n <= pref:
        return n
    d = (pref // mult) * mult
    while d >= mult:
        if n % d == 0:
            return d
        d -= mult
    raise ValueError(f"no tile for {n} (pref {pref}, mult {mult})")


def _mo(v, m):
    return v if isinstance(v, int) else pl.multiple_of(v, m)


def _cp(*sem, **kw):
    return pltpu.CompilerParams(dimension_semantics=sem or None, vmem_limit_bytes=_VMEM_LIMIT, **kw)


def _call(body, name, **kw):
    return pl.pallas_call(body, name=name, **kw, **_KW)


def _dot(a, b, dn=_NN):
    return lax.dot_general(a, b, dn, preferred_element_type=_F32)


def _gelu(x):
    return 0.5 * x * (1.0 + lax.erf(x * _SQRT_HALF))


def _gelu_grad(x):
    return 0.5 * (1.0 + lax.erf(x * _SQRT_HALF)) + x * jnp.exp(-0.5 * x * x) * _INV_SQRT_2PI


def _rstd(x):
    return lax.rsqrt(jnp.mean(x * x, axis=-1, keepdims=True) + EPS)


def _norm_bwd(dout, xhat, r, g):
    dy = dout * g
    return r * (dy - xhat * jnp.mean(dy * xhat, axis=-1, keepdims=True))


def _softmax(s):
    e = jnp.exp(s - jnp.max(s, axis=-1, keepdims=True))
    return e * (1.0 / jnp.sum(e, axis=-1, keepdims=True))


def _mm(name, a, b, a_spec, b_spec, dn, grid, acc_shape, out_shape, out_specs, epilogue, extra=(), extra_specs=(), dep=None):
    nk = grid[2]
    n_ex = len(extra)
    deps = [] if dep is None else [dep]
    multi = isinstance(out_shape, (list, tuple))
    n_out = len(out_shape) if multi else 1

    def body(*refs):
        a_ref, b_ref = refs[0], refs[1]
        ex = refs[2:2 + n_ex]
        outs = refs[2 + n_ex + len(deps):2 + n_ex + len(deps) + n_out]

        def prod():
            return _dot(a_ref[...].astype(_MXU), b_ref[...].astype(_MXU), dn)

        if nk == 1:
            epilogue(prod(), ex, outs)
        else:
            acc = refs[-1]
            k = pl.program_id(2)

            @pl.when(k == 0)
            def _():
                acc[...] = jnp.zeros_like(acc)

            acc[...] += prod()

            @pl.when(k == nk - 1)
            def _():
                epilogue(acc[...], ex, outs)

    return _call(
        body, name, grid=grid, in_specs=[a_spec, b_spec, *extra_specs] + [_ANY] * len(deps), out_specs=out_specs, out_shape=out_shape,
        scratch_shapes=[] if nk == 1 else [pltpu.VMEM(acc_shape, _F32)],
        compiler_params=_cp("parallel", "parallel", "arbitrary"),
    )(a, b, *extra, *deps)


def _ep_store(acc, ex, outs):
    for o in outs:
        o[...] = acc.astype(o.dtype)


def _ep_add(acc, ex, outs):
    outs[0][...] = (acc + ex[0][...]).astype(outs[0].dtype)


def _mm_nn(name, a, b, out_dtype=_F32, add=None, pm=1024, pn=1024, pk=2048, dep=None):
    M, K = a.shape
    N = b.shape[1]
    tm, tn, tk = _tile(M, pm, 8), _tile(N, pn), _tile(K, pk)
    o_spec = BS((tm, tn), lambda i, j, k: (i, j))
    return _mm(name, a, b, BS((tm, tk), lambda i, j, k: (i, k)), BS((tk, tn), lambda i, j, k: (k, j)), _NN,
               (M // tm, N // tn, K // tk), (tm, tn), S((M, N), out_dtype), o_spec,
               _ep_store if add is None else _ep_add,
               extra=() if add is None else (add,), extra_specs=() if add is None else (o_spec,), dep=dep)


def _mm_nt(name, a, b, out_dtype=_F32, pm=1024, pn=1024, pk=4096):
    M, K = a.shape
    N = b.shape[0]
    tm, tn, tk = _tile(M, pm, 8), _tile(N, pn), _tile(K, pk)
    return _mm(name, a, b, BS((tm, tk), lambda i, j, k: (i, k)), BS((tn, tk), lambda i, j, k: (j, k)), _NT,
               (M // tm, N // tn, K // tk), (tm, tn), S((M, N), out_dtype), BS((tm, tn), lambda i, j, k: (i, j)), _ep_store)


def _mm_tn_dual(name, a, b, pm=1024, pn=1024, pk=2048, dep=None):
    K, M = a.shape
    N = b.shape[1]
    tm, tn, tk = _tile(M, pm), _tile(N, pn), _tile(K, pk, 16)
    o_spec = BS((tm, tn), lambda i, j, k: (i, j))
    return _mm(name, a, b, BS((tk, tm), lambda i, j, k: (k, i)), BS((tk, tn), lambda i, j, k: (k, j)), _TN,
               (M // tm, N // tn, K // tk), (tm, tn), [S((M, N), _F32), S((M, N), _WIRE)], [o_spec, o_spec], _ep_store, dep=dep)


def _ffn_gate_up(h2, wgu):
    T, D = h2.shape
    F = wgu.shape[1] // 2
    tm, tn = _tile(T, 1024, 8), _tile(F, 512)
    nj = F // tn

    def body(a_ref, bg_ref, bu_ref, act_ref, gu_ref):
        a = a_ref[...]
        g = _dot(a, bg_ref[...])
        u = _dot(a, bu_ref[...])
        sg = 1.0 / (1.0 + jnp.exp(-g))
        silu = g * sg
        gu_ref[0] = (u * (sg * (1.0 + g * (1.0 - sg)))).astype(gu_ref.dtype)
        gu_ref[1] = silu.astype(gu_ref.dtype)
        act_ref[...] = (silu * u).astype(act_ref.dtype)

    return _call(
        body, "ffn_gate_up", grid=(T // tm, nj),
        in_specs=[BS((tm, D), lambda i, j: (i, 0)), BS((D, tn), lambda i, j: (0, j)), BS((D, tn), lambda i, j: (0, j + nj))],
        out_specs=[BS((tm, tn), lambda i, j: (i, j)), BS((2, tm, tn), lambda i, j: (0, i, j))],
        out_shape=[S((T, F), _MXU), S((2, T, F), _MXU)],
        compiler_params=_cp("parallel", "parallel"),
    )(h2, wgu, wgu)


def _ffn_dact(dxm, wdown, gu, dep=None):
    T, D = dxm.shape
    F = wdown.shape[0]
    tm, tn = _tile(T, 2048, 8), _tile(F, 512)
    deps = [] if dep is None else [dep]

    def body(a_ref, b_ref, gu_ref, *rest):
        o_ref = rest[-1]
        d = _dot(a_ref[...], b_ref[...], _NT)
        o_ref[0] = (d * gu_ref[0].astype(_F32)).astype(o_ref.dtype)
        o_ref[1] = (d * gu_ref[1].astype(_F32)).astype(o_ref.dtype)

    return _call(
        body, "ffn_dact", grid=(T // tm, F // tn),
        in_specs=[BS((tm, D), lambda i, j: (i, 0)), BS((tn, D), lambda i, j: (j, 0)), BS((2, tm, tn), lambda i, j: (0, i, j))]
        + [_ANY] * len(deps),
        out_specs=BS((2, tm, tn), lambda i, j: (0, i, j)), out_shape=S((2, T, F), _MXU),
        compiler_params=_cp("parallel", "parallel"),
    )(dxm, wdown, gu, *deps)


def _ffn_dh(dgu, wgu, dep=None):
    _, T, F = dgu.shape
    D = wgu.shape[0]
    tm, tn, tk = _tile(T, 512, 8), _tile(D, 2048), _tile(F, 2816, 256)
    nkf = F // tk
    return _mm("ffn_dh", dgu, wgu, BS((None, tm, tk), lambda i, j, k: (k // nkf, i, k % nkf)),
               BS((tn, tk), lambda i, j, k: (j, k)), _NT, (T // tm, D // tn, 2 * nkf), (tm, tn),
               S((T, D), _F32), BS((tm, tn), lambda i, j, k: (i, j)), _ep_store, dep=dep)


def _ffn_dwgu(h2t, dgu):
    _, T, F = dgu.shape
    D = h2t.shape[0]
    tm, tn, tk = _tile(D, 1024), _tile(F, 512), _tile(T, 4096)
    njf = F // tn
    o_spec = BS((tm, tn), lambda i, j, k: (i, j))
    return _mm("ffn_dwgu", h2t, dgu, BS((tm, tk), lambda i, j, k: (i, k)),
               BS((None, tk, tn), lambda i, j, k: (j // njf, k, j % njf)), _NN, (D // tm, 2 * njf, T // tk), (tm, tn),
               [S((D, 2 * F), _F32), S((D, 2 * F), _WIRE)], [o_spec, o_spec], _ep_store)


def _mm_nn_dual(name, a, b, pm=1024, pn=1024, pk=2048):
    M, K = a.shape
    N = b.shape[1]
    tm, tn, tk = _tile(M, pm), _tile(N, pn), _tile(K, pk)
    o_spec = BS((tm, tn), lambda i, j, k: (i, j))
    return _mm(name, a, b, BS((tm, tk), lambda i, j, k: (i, k)), BS((tk, tn), lambda i, j, k: (k, j)), _NN,
               (M // tm, N // tn, K // tk), (tm, tn), [S((M, N), _F32), S((M, N), _WIRE)], [o_spec, o_spec], _ep_store)


def _rmsnorm_fwd(x, g, name, dep=None, transposed=False):
    T, D = x.shape
    tr = _tile(T, 512, 128)
    n_out = 3 if transposed else 1

    def body(x_ref, g_ref, *rest):
        outs = rest[-n_out:]
        xv = x_ref[...]
        r = _rstd(xv)
        h = (xv * r * g_ref[...]).astype(outs[0].dtype)
        outs[0][...] = h
        if transposed:
            outs[1][...] = h.T
            outs[2][...] = r

    row = BS((tr, D), lambda i: (i, 0))
    deps = [] if dep is None else [dep]
    return _call(body, name, grid=(T // tr,), in_specs=[row, BS((1, D), lambda i: (0, 0))] + [_ANY] * len(deps),
                 out_specs=[row, BS((D, tr), lambda i: (0, i)), BS((tr, 1), lambda i: (i, 0))] if transposed else row,
                 out_shape=[S((T, D), _MXU), S((D, T), _MXU), S((T, 1), _F32)] if transposed else S((T, D), _MXU),
                 compiler_params=_cp("parallel"))(x, g.reshape(1, D), *deps)


def _rmsnorm_bwd(x, g, dh, dres, name, rstd=None):
    T, D = x.shape
    tr = _tile(T, 256, 8)
    has_res = dres is not None
    has_r = rstd is not None

    def body(*refs):
        x_ref, g_ref, dh_ref = refs[:3]
        dx_ref, dxm_ref, dg_ref = refs[-3:]

        @pl.when(pl.program_id(0) == 0)
        def _():
            dg_ref[...] = jnp.zeros_like(dg_ref)

        xv = x_ref[...]
        r = refs[3 + has_res][...] if has_r else _rstd(xv)
        xhat = xv * r
        dh_v = dh_ref[...]
        dg_ref[...] += jnp.sum(dh_v * xhat, axis=0, keepdims=True)
        dx = _norm_bwd(dh_v, xhat, r, g_ref[...])
        if has_res:
            dx = dx + refs[3][...]
        dx_ref[...] = dx
        dxm_ref[...] = dx.astype(dxm_ref.dtype)

    row = BS((tr, D), lambda i: (i, 0))
    vec = BS((1, D), lambda i: (0, 0))
    extra = ([dres] if has_res else []) + ([rstd] if has_r else [])
    return _call(body, name, grid=(T // tr,),
                 in_specs=[row, vec, row] + ([row] if has_res else []) + ([BS((tr, 1), lambda i: (i, 0))] if has_r else []),
                 out_specs=[row, row, vec], out_shape=[S((T, D), _F32), S((T, D), _MXU), S((1, D), _F32)],
                 compiler_params=_cp("arbitrary"))(x, g.reshape(1, D), dh, *extra)


def _loss_head(y, target):
    T, D = y.shape
    tr = _tile(T, 256, 8)

    def body(y_ref, t_ref, dy_ref, dym_ref, acc_ref):
        @pl.when(pl.program_id(0) == 0)
        def _():
            acc_ref[...] = jnp.zeros_like(acc_ref)

        err = y_ref[...] - t_ref[...]
        acc_ref[...] += jnp.sum(jnp.sum(err * err, axis=-1, keepdims=True), axis=0, keepdims=True)
        dy = err * (1.0 / D)
        dy_ref[...] = dy
        dym_ref[...] = dy.astype(dym_ref.dtype)

    row = BS((tr, D), lambda i: (i, 0))
    return _call(body, "loss_head", grid=(T // tr,), in_specs=[row, row],
                 out_specs=[row, row, BS((1, 128), lambda i: (0, 0))],
                 out_shape=[S((T, D), _F32), S((T, D), _MXU), S((1, 128), _F32)],
                 compiler_params=_cp("arbitrary"))(y, target)


def _mixa_blocks(T):
    return 2 if T % (2 * CHUNK) == 0 else 1


def _mixer_a_fwd(z, gv, ws_m, b_t, TOK, width):
    T = z.shape[0]
    G = TOK // HEAD
    CB = _mixa_blocks(T)
    R = CB * CHUNK

    def body(z_ref, gv_ref, ws_ref, bt_ref, o_ref):
        u = _gelu(z_ref[:, :TOK])
        v = _gelu(z_ref[:, TOK:])
        vn = (v * _rstd(v) * gv_ref[...]).astype(_MXU)
        for c in range(CB):
            rows = slice(c * CHUNK, (c + 1) * CHUNK)
            for g in range(G):
                cols = slice(g * HEAD, (g + 1) * HEAD)
                s = _dot(ws_ref[g], vn[rows, cols]) + bt_ref[:, g:g + 1]
                o_ref[rows, cols] = (u[rows, cols] * s).astype(o_ref.dtype)

    return _call(
        body, "mixer_a_fwd", grid=(T // R,),
        in_specs=[BS((R, 2 * TOK), lambda i: (i, 0)), BS((1, TOK), lambda i: (0, 0)),
                  BS((G, CHUNK, CHUNK), lambda i: (0, 0, 0)), BS((CHUNK, G), lambda i: (0, 0))],
        out_specs=BS((R, TOK), lambda i: (i, 0)), out_shape=S((T, width), _MXU), compiler_params=_cp("parallel"),
    )(z, gv.reshape(1, TOK), ws_m, b_t)


def _mixer_a_bwd(z, dcat, gv, ws_m, wst_m, b_t, TOK):
    T = z.shape[0]
    G = TOK // HEAD
    CB = _mixa_blocks(T)
    R = CB * CHUNK
    n = T // R

    def body(z_ref, d_ref, gv_ref, ws_ref, wst_ref, bt_ref, dz_ref, dws_ref, db_ref, dgv_ref, dvn_scr):
        i = pl.program_id(0)

        @pl.when(i == 0)
        def _():
            dws_ref[...] = jnp.zeros_like(dws_ref)
            db_ref[...] = jnp.zeros_like(db_ref)
            dgv_ref[...] = jnp.zeros_like(dgv_ref)

        zu = z_ref[:, :TOK]
        zv = z_ref[:, TOK:]
        u = _gelu(zu)
        v = _gelu(zv)
        r = _rstd(v)
        vhat = v * r
        gvv = gv_ref[...]
        vn = (vhat * gvv).astype(_MXU)
        d = d_ref[...]
        gpu = _gelu_grad(zu)
        for c in range(CB):
            rows = slice(c * CHUNK, (c + 1) * CHUNK)
            for g in range(G):
                cols = slice(g * HEAD, (g + 1) * HEAD)
                vn_cg = vn[rows, cols]
                s = _dot(ws_ref[g], vn_cg) + bt_ref[:, g:g + 1]
                d_cg = d[rows, cols]
                dz_ref[rows, cols] = (d_cg * s * gpu[rows, cols]).astype(dz_ref.dtype)
                ds = d_cg * u[rows, cols]
                ds_m = ds.astype(_MXU)
                dvn_scr[rows, cols] = _dot(wst_ref[g], ds_m)
                dws_ref[g] += _dot(ds_m, vn_cg, _NT)
                db_ref[g] += ds
        dvn = dvn_scr[...]
        dgv_ref[...] += jnp.sum(dvn * vhat, axis=0, keepdims=True)
        dv = _norm_bwd(dvn, vhat, r, gvv)
        dz_ref[:, TOK:] = (dv * _gelu_grad(zv)).astype(dz_ref.dtype)

        @pl.when(i == n - 1)
        def _():
            for g in range(G):
                db_ref[g] = jnp.broadcast_to(jnp.sum(db_ref[g], axis=1, keepdims=True), (CHUNK, CHUNK))

    full3 = BS((G, CHUNK, CHUNK), lambda i: (0, 0, 0))
    return _call(
        body, "mixer_a_bwd", grid=(n,),
        in_specs=[BS((R, 2 * TOK), lambda i: (i, 0)), BS((R, TOK), lambda i: (i, 0)), BS((1, TOK), lambda i: (0, 0)),
                  full3, full3, BS((CHUNK, G), lambda i: (0, 0))],
        out_specs=[BS((R, 2 * TOK), lambda i: (i, 0)), full3, full3, BS((1, TOK), lambda i: (0, 0))],
        out_shape=[S((T, 2 * TOK), _MXU), S((G, CHUNK, CHUNK), _F32), S((G, CHUNK, CHUNK), _F32), S((1, TOK), _F32)],
        scratch_shapes=[pltpu.VMEM((R, TOK), _F32)], compiler_params=_cp("arbitrary"),
    )(z, dcat, gv.reshape(1, TOK), ws_m, wst_m, b_t)


def _rope_tables(T):
    n_rows = T // GRID_W
    rows = jnp.broadcast_to(jnp.arange(n_rows)[:, None], (n_rows, GRID_W)).reshape(T)
    cols = jnp.broadcast_to(jnp.arange(GRID_W)[None, :], (n_rows, GRID_W)).reshape(T)
    pairs = HEAD // 4
    freqs = ROPE_THETA ** (-jnp.arange(pairs, dtype=_F32) / pairs)
    ang_r = rows.astype(_F32)[:, None] * freqs
    ang_c = cols.astype(_F32)[:, None] * freqs
    ang = jnp.concatenate([ang_r, ang_r, ang_c, ang_c], axis=-1)
    cos, sin = jnp.cos(ang), jnp.sin(ang)
    first = (jnp.arange(HEAD) % (HEAD // 2)) < (HEAD // 4)
    return cos, jnp.where(first, -sin, 0.0), jnp.where(first, 0.0, sin)


def _rope(x, cs, sa, sb):
    return x * cs + pltpu.roll(x, 96, 1) * sa + pltpu.roll(x, 32, 1) * sb


def _qk_rope_fwd(z, gq, gk, tabs, TOK, KV):
    T = z.shape[0]
    R = _tile(T, 512, 8)
    W = TOK + 2 * KV

    def body(z_ref, gq_ref, gk_ref, cos_ref, sa_ref, sb_ref, q_ref, k_ref, v_ref):
        cs, sa, sb = cos_ref[...], sa_ref[...], sb_ref[...]
        for h in range((TOK + KV) // HEAD):
            cols = slice(h * HEAD, (h + 1) * HEAD)
            xv = z_ref[:, cols]
            xn = xv * _rstd(xv) * (gq_ref[...] if h < TOK // HEAD else gk_ref[...])
            out = _rope(xn, cs, sa, sb)
            if h < TOK // HEAD:
                q_ref[:, cols] = out.astype(q_ref.dtype)
            else:
                k_ref[:, h * HEAD - TOK:(h + 1) * HEAD - TOK] = out.astype(k_ref.dtype)
        v_ref[...] = z_ref[:, TOK + KV:].astype(v_ref.dtype)

    vec = BS((1, HEAD), lambda i: (0, 0))
    tab = BS((R, HEAD), lambda i: (i, 0))
    return _call(
        body, "qk_rope_fwd", grid=(T // R,), in_specs=[BS((R, W), lambda i: (i, 0)), vec, vec, tab, tab, tab],
        out_specs=[BS((R, TOK), lambda i: (i, 0)), BS((R, KV), lambda i: (i, 0)), BS((R, KV), lambda i: (i, 0))],
        out_shape=[S((T, TOK), _MXU), S((T, KV), _MXU), S((T, KV), _MXU)], compiler_params=_cp("parallel"),
    )(z, gq.reshape(1, HEAD), gk.reshape(1, HEAD), *tabs)


def _qk_rope_bwd(z, dq, dk, gq, gk, tabs, TOK, KV):
    T = z.shape[0]
    R = _tile(T, 512, 8)
    W = TOK + KV

    def body(z_ref, dq_ref, dk_ref, gq_ref, gk_ref, cos_ref, sa_ref, sb_ref, dz_ref, dgq_ref, dgk_ref):
        @pl.when(pl.program_id(0) == 0)
        def _():
            dgq_ref[...] = jnp.zeros_like(dgq_ref)
            dgk_ref[...] = jnp.zeros_like(dgk_ref)

        cs, sa, sb = cos_ref[...], sa_ref[...], sb_ref[...]
        for h in range(W // HEAD):
            cols = slice(h * HEAD, (h + 1) * HEAD)
            is_q = h < TOK // HEAD
            do = dq_ref[:, cols] if is_q else dk_ref[:, h * HEAD - TOK:(h + 1) * HEAD - TOK]
            dxn = do * cs - pltpu.roll(do, 96, 1) * sa - pltpu.roll(do, 32, 1) * sb
            xv = z_ref[:, cols]
            r = _rstd(xv)
            xhat = xv * r
            dg_ref = dgq_ref if is_q else dgk_ref
            dg_ref[...] += jnp.sum(dxn * xhat, axis=0, keepdims=True)
            dz_ref[:, cols] = _norm_bwd(dxn, xhat, r, gq_ref[...] if is_q else gk_ref[...]).astype(dz_ref.dtype)

    vec = BS((1, HEAD), lambda i: (0, 0))
    tab = BS((R, HEAD), lambda i: (i, 0))
    return _call(
        body, "qk_rope_bwd", grid=(T // R,),
        in_specs=[BS((R, W), lambda i: (i, 0)), BS((R, TOK), lambda i: (i, 0)), BS((R, KV), lambda i: (i, 0)), vec, vec, tab, tab, tab],
        out_specs=[BS((R, W), lambda i: (i, 0)), vec, vec],
        out_shape=[S((T, W), _MXU), S((1, HEAD), _F32), S((1, HEAD), _F32)], compiler_params=_cp("arbitrary"),
    )(z, dq, dk, gq.reshape(1, HEAD), gk.reshape(1, HEAD), *tabs)


_ATTN_C2 = float(HEAD ** -0.5 * np.log2(np.e))


def _attn_fwd(q, k, v, QPK, width):
    T, TOK = q.shape
    KVH = k.shape[1] // HEAD
    tq = _tile(T, 1024, 8)
    ts = 256 if tq % 256 == 0 else tq
    W = QPK * HEAD

    def body(q_ref, k_ref, v_ref, o_ref, st_ref, vaug):
        @pl.when(pl.program_id(1) == 0)
        def _():
            vaug[:, :HEAD] = v_ref[...]
            vaug[:, HEAD:] = jnp.ones((T, HEAD), vaug.dtype)

        kk, va = k_ref[...], vaug[...]
        for g in range(QPK):
            cols = slice(g * HEAD, (g + 1) * HEAD)
            for r in range(tq // ts):
                rows = slice(r * ts, (r + 1) * ts)
                s = _dot(q_ref[rows, cols], kk, _NT)
                m = jnp.max(s, axis=-1, keepdims=True)
                ov = _dot(jnp.exp2((s - m) * _ATTN_C2).astype(_MXU), va)
                l = ov[:, HEAD:HEAD + 1]
                o_ref[rows, cols] = (ov[:, :HEAD] * (1.0 / l)).astype(o_ref.dtype)
                st_ref[rows, g:g + 1] = m + jnp.log2(l) * (1.0 / _ATTN_C2)

    qs = BS((tq, W), lambda h, i: (i, h))
    ks = BS((T, HEAD), lambda h, i: (0, h))
    return _call(body, "attn_fwd", grid=(KVH, T // tq), in_specs=[qs, ks, ks],
                 out_specs=[qs, BS((None, tq, QPK), lambda h, i: (h, i, 0))],
                 out_shape=[S((T, width), _MXU), S((KVH, T, QPK), _F32)],
                 scratch_shapes=[pltpu.VMEM((T, 2 * HEAD), _MXU)],
                 compiler_params=_cp("parallel", "arbitrary"))(q, k, v)


def _attn_bwd(q, k, v, dcat, o, stat, QPK):
    T, TOK = q.shape
    KV = k.shape[1]
    KVH = KV // HEAD
    tq = _tile(T, 512, 8)
    ts = 256 if tq % 256 == 0 else tq
    nq = T // tq
    W = QPK * HEAD
    scale = HEAD ** -0.5

    def body(q_ref, k_ref, v_ref, do_ref, o_ref, st_ref, dq_ref, dk_ref, dv_ref, dk_acc, dv_acc, ds_all, p_all, q_all, do_all):
        i = pl.program_id(1)

        @pl.when(i == 0)
        def _():
            dk_acc[...] = jnp.zeros_like(dk_acc)
            dv_acc[...] = jnp.zeros_like(dv_acc)

        kk, vv = k_ref[...], v_ref[...]
        for r in range(tq // ts):
            rows = slice(r * ts, (r + 1) * ts)
            for g in range(QPK):
                cols = slice(g * HEAD, (g + 1) * HEAD)
                stack = slice(g * ts, (g + 1) * ts)
                qg = q_ref[rows, cols]
                p = jnp.exp2((_dot(qg, kk, _NT) - st_ref[rows, g:g + 1]) * _ATTN_C2)
                do32 = do_ref[rows, cols]
                do = do32.astype(_MXU)
                delta = jnp.sum(do32 * o_ref[rows, cols].astype(_F32), axis=-1, keepdims=True)
                ds = (p * (_dot(do, vv, _NT) - delta)).astype(_MXU)
                dq_ref[rows, cols] = _dot(ds, kk) * scale
                ds_all[stack, :] = ds
                p_all[stack, :] = p.astype(_MXU)
                q_all[stack, :] = qg
                do_all[stack, :] = do
            dk_acc[...] += _dot(ds_all[...], q_all[...], _TN)
            dv_acc[...] += _dot(p_all[...], do_all[...], _TN)

        @pl.when(i == nq - 1)
        def _():
            dk_ref[...] = dk_acc[...] * scale
            dv_ref[...] = dv_acc[...].astype(dv_ref.dtype)

    qs = BS((tq, W), lambda h, i: (i, h))
    ks = BS((T, HEAD), lambda h, i: (0, h))
    return _call(
        body, "attn_bwd", grid=(KVH, nq), in_specs=[qs, ks, ks, qs, qs, BS((None, tq, QPK), lambda h, i: (h, i, 0))],
        out_specs=[qs, ks, ks], out_shape=[S((T, TOK), _F32), S((T, KV), _F32), S((T, KV), _MXU)],
        scratch_shapes=[pltpu.VMEM((T, HEAD), _F32), pltpu.VMEM((T, HEAD), _F32), pltpu.VMEM((QPK * ts, T), _MXU),
                        pltpu.VMEM((QPK * ts, T), _MXU), pltpu.VMEM((QPK * ts, HEAD), _MXU), pltpu.VMEM((QPK * ts, HEAD), _MXU)],
        compiler_params=_cp("parallel", "arbitrary"),
    )(q, k, v, dcat, o, stat)


def _mem_fwd(z, qblk, kv, gmq, gmk, MEMW, into):
    T = z.shape[0]
    NM = kv.shape[0]
    tq = _tile(T, 512, 8)
    scale = HEAD ** -0.5
    oblk = into.shape[1] // MEMW - 1

    def body(q_ref, kv_ref, gq_ref, gk_ref, into_ref, o_ref):
        for h in range(MEMW // HEAD):
            cols = slice(h * HEAD, (h + 1) * HEAD)
            kx = kv_ref[:, cols]
            kn = (kx * _rstd(kx) * gk_ref[...]).astype(_MXU)
            vv = kv_ref[:, MEMW + h * HEAD:MEMW + (h + 1) * HEAD].astype(_MXU)
            qx = q_ref[:, cols]
            qn = (qx * _rstd(qx) * gq_ref[...]).astype(_MXU)
            p = _softmax(_dot(qn, kn, _NT) * scale)
            o_ref[:, cols] = _dot(p.astype(_MXU), vv).astype(o_ref.dtype)

    vec = BS((1, HEAD), lambda i: (0, 0))
    return _call(
        body, "mem_fwd", grid=(T // tq,),
        in_specs=[BS((tq, MEMW), lambda i: (i, qblk)), BS((NM, 2 * MEMW), lambda i: (0, 0)), vec, vec, _ANY],
        out_specs=BS((tq, MEMW), lambda i: (i, oblk)), out_shape=S(into.shape, into.dtype), input_output_aliases={4: 0},
        compiler_params=_cp("parallel"),
    )(z, kv, gmq.reshape(1, HEAD), gmk.reshape(1, HEAD), into)


def _mem_bwd(z, qblk, kv, gmq, gmk, dcat, dblk, MEMW):
    T = z.shape[0]
    NM = kv.shape[0]
    tq = _tile(T, 512, 8)
    scale = HEAD ** -0.5

    def body(q_ref, kv_ref, gq_ref, gk_ref, do_ref, dz_ref, dkn_ref, dv_ref, dgq_ref):
        @pl.when(pl.program_id(0) == 0)
        def _():
            dkn_ref[...] = jnp.zeros_like(dkn_ref)
            dv_ref[...] = jnp.zeros_like(dv_ref)
            dgq_ref[...] = jnp.zeros_like(dgq_ref)

        for h in range(MEMW // HEAD):
            cols = slice(h * HEAD, (h + 1) * HEAD)
            kx = kv_ref[:, cols]
            kn = (kx * _rstd(kx) * gk_ref[...]).astype(_MXU)
            vv = kv_ref[:, MEMW + h * HEAD:MEMW + (h + 1) * HEAD].astype(_MXU)
            qx = q_ref[:, cols]
            rq = _rstd(qx)
            qhat = qx * rq
            qn = (qhat * gq_ref[...]).astype(_MXU)
            p = _softmax(_dot(qn, kn, _NT) * scale)
            do = do_ref[:, cols].astype(_MXU)
            dp = _dot(do, vv, _NT)
            ds = (p * (dp - jnp.sum(p * dp, axis=-1, keepdims=True)) * scale).astype(_MXU)
            dqn = _dot(ds, kn)
            dkn_ref[:, cols] += _dot(ds, qn, _TN)
            dv_ref[:, cols] += _dot(p.astype(_MXU), do, _TN)
            dgq_ref[...] += jnp.sum(dqn * qhat, axis=0, keepdims=True)
            dz_ref[:, cols] = _norm_bwd(dqn, qhat, rq, gq_ref[...]).astype(dz_ref.dtype)

    vec = BS((1, HEAD), lambda i: (0, 0))
    kvs = BS((NM, MEMW), lambda i: (0, 0))
    return _call(
        body, "mem_bwd", grid=(T // tq,),
        in_specs=[BS((tq, MEMW), lambda i: (i, qblk)), BS((NM, 2 * MEMW), lambda i: (0, 0)), vec, vec,
                  BS((tq, MEMW), lambda i: (i, dblk))],
        out_specs=[BS((tq, MEMW), lambda i: (i, 0)), kvs, kvs, vec],
        out_shape=[S((T, MEMW), _MXU), S((NM, MEMW), _F32), S((NM, MEMW), _F32), S((1, HEAD), _F32)],
        compiler_params=_cp("arbitrary"),
    )(z, kv, gmq.reshape(1, HEAD), gmk.reshape(1, HEAD), dcat)


def _memkv_bwd(kv, dkn, dv, gmk, MEMW):
    NM = kv.shape[0]

    def body(kv_ref, dkn_ref, dv_ref, gk_ref, dkv_ref, dgk_ref):
        dgk = jnp.zeros((1, HEAD), _F32)
        for h in range(MEMW // HEAD):
            cols = slice(h * HEAD, (h + 1) * HEAD)
            kx = kv_ref[:, cols]
            r = _rstd(kx)
            khat = kx * r
            dkn = dkn_ref[:, cols]
            dgk = dgk + jnp.sum(dkn * khat, axis=0, keepdims=True)
            dkv_ref[:, cols] = _norm_bwd(dkn, khat, r, gk_ref[...]).astype(dkv_ref.dtype)
        dgk_ref[...] = dgk
        dkv_ref[:, MEMW:] = dv_ref[...].astype(dkv_ref.dtype)

    return _call(body, "memkv_bwd", out_shape=[S((NM, 2 * MEMW), _MXU), S((1, HEAD), _F32)],
                 compiler_params=_cp())(kv, dkn, dv, gmk.reshape(1, HEAD))


def _cast_into_full(w, l, sh, idx, dep=None):
    tr, tc = _tile(sh.Rs, 512, 16), _tile(sh.Cs, 2048)
    nr, nc = sh.Rs // tr, sh.Cs // tc
    deps = [] if dep is None else [dep]

    def body(i_ref, c_ref, w_ref, *rest):
        rest[-1][...] = w_ref[...].astype(rest[-1].dtype)

    if sh.by_cols:
        o_map = lambda a, b, si, sc: (a, si[0] * nc + b)
    else:
        o_map = lambda a, b, si, sc: (si[0] * nr + a, b)
    return _call(
        body, "cast_into_full",
        grid_spec=pltpu.PrefetchScalarGridSpec(
            num_scalar_prefetch=2, grid=(nr, nc),
            in_specs=[BS((None, tr, tc), lambda a, b, si, sc: (l, a, b))] + [_ANY] * len(deps), out_specs=BS((tr, tc), o_map)),
        out_shape=S((sh.R, sh.C), _WIRE), compiler_params=_cp("parallel", "parallel"),
    )(*idx, w, *deps)


def _adamw(w, g, m, v, name, l0=0, l1=None, prev=None):
    L, R, C = w.shape
    l1 = L if l1 is None else l1
    tc = _tile(C, 2048)
    tr = _tile(R, max(8, (512 * 1024) // tc), 8)
    c_m = 1.0 / (1.0 - ADAM_B1 ** ADAM_STEP)
    c_v = 1.0 / (1.0 - ADAM_B2 ** ADAM_STEP)

    def body(w_ref, g_ref, m_ref, v_ref, *rest):
        go_ref, d_ref, mo_ref, vo_ref = rest[-4:]
        gv = g_ref[...]
        mn = ADAM_B1 * m_ref[...] + (1.0 - ADAM_B1) * gv
        vn = ADAM_B2 * v_ref[...] + (1.0 - ADAM_B2) * (gv * gv)
        go_ref[...] = gv
        mo_ref[...] = mn
        vo_ref[...] = vn
        d_ref[...] = -ADAM_LR * ((mn * c_m) / (jnp.sqrt(vn * c_v) + ADAM_EPS) + ADAM_WD * w_ref[...])

    blk = BS((None, tr, tc), lambda a, i, j: (l0 + a, i, j))
    prevs = [] if prev is None else list(prev)
    return _call(body, name, grid=(l1 - l0, R // tr, C // tc), in_specs=[blk] * 4 + [_ANY] * len(prevs), out_specs=[blk] * 4,
                 out_shape=[S((L, R, C), _F32)] * 4, input_output_aliases={4 + k: k for k in range(len(prevs))},
                 compiler_params=_cp("parallel", "parallel", "parallel"))(w, g, m, v, *prevs)


def _where_am_i():
    x, y, c = lax.axis_index("x"), lax.axis_index("y"), lax.axis_index("c")
    chips = [(1 - x, y), (x, 1 - y), (1 - x, 1 - y)]
    return x, y, c, 2 * x + y, chips, [2 * cx + cy for cx, cy in chips]


class _Shard:
    def __init__(self, R, C, by_cols):
        self.R, self.C, self.by_cols = R, C, by_cols
        self.Rs, self.Cs = (R, C // 4) if by_cols else (R // 4, C)
        self.Rh = self.Rs // 2
        self.Q = R // 8

    def full_piece(self, ref, j, cc):
        if self.by_cols:
            return ref.at[pl.ds(cc * self.Rh, self.Rh), pl.ds(_mo(j * self.Cs, 128), self.Cs)]
        return ref.at[pl.ds(_mo(j * self.Rs + cc * self.Rh, 16), self.Rh), :]

    def full_shard(self, ref, j):
        if self.by_cols:
            return ref.at[:, pl.ds(_mo(j * self.Cs, 128), self.Cs)]
        return ref.at[pl.ds(_mo(j * self.Rs, 16), self.Rs), :]

    def shard_half(self, ref, cc):
        return ref.at[pl.ds(_mo(cc * self.Rh, 16), self.Rh), :]

    def half_piece(self, ref, j):
        if self.by_cols:
            return ref.at[:, pl.ds(_mo(j * self.Cs, 128), self.Cs)]
        return ref.at[pl.ds(_mo(j * self.Rh, 16), self.Rh), :]


def _remote(src, dst, ssem, rsem, dev):
    return pltpu.make_async_remote_copy(src_ref=src, dst_ref=dst, send_sem=ssem, recv_sem=rsem, device_id=dev, device_id_type=_MESH)


_HBM = pl.BlockSpec(memory_space=pltpu.HBM)
_SEM = pl.BlockSpec(memory_space=pltpu.SEMAPHORE)
_EFFECT = pltpu.SideEffectType.DATAFLOW_SIDE_EFFECTING


def _in_hbm(a):
    return pltpu.with_memory_space_constraint(a, pltpu.HBM)


def _gather_start(fulls, shs, name):
    n = len(fulls)

    def body(*refs):
        bufs = refs[:n]
        send_sems, recv_sems = refs[n], refs[n + 1]
        token = refs[-1]
        x, y, c, me, chips, chip_ids = _where_am_i()
        for t in range(n):
            mine = shs[t].full_piece(bufs[t], me, c)
            for r in range(3):
                _remote(mine, mine, send_sems.at[3 * t + r], recv_sems.at[3 * t + r], (*chips[r], c)).start()
        token[...] = jnp.zeros_like(token)

    out = pl.pallas_call(
        body, name=name, in_specs=[_HBM] * n,
        out_shape=(pltpu.SemaphoreType.DMA((3 * n,)), pltpu.SemaphoreType.DMA((3 * n,)), *[pltpu.HBM(f.shape, f.dtype) for f in fulls],
                   S((8, 128), _F32)),
        out_specs=(_SEM, _SEM, *[_HBM] * n, pl.BlockSpec(memory_space=pltpu.VMEM)),
        input_output_aliases={t: 2 + t for t in range(n)},
        compiler_params=pltpu.CompilerParams(has_side_effects=_EFFECT), **_KW,
    )(*[_in_hbm(f) for f in fulls])
    return out[0], out[1], list(out[2:2 + n]), out[-1]


def _gather_wait(fulls, send_sems, recv_sems, after, shs, name):
    n = len(fulls)

    def body(*refs):
        bufs = refs[:n]
        ssem, rsem = refs[n], refs[n + 1]
        x, y, c, me, chips, chip_ids = _where_am_i()
        for t in range(n):
            mine = shs[t].full_piece(bufs[t], me, c)
            for r in range(3):
                _remote(mine, mine, ssem.at[3 * t + r], rsem.at[3 * t + r], (*chips[r], c)).wait_send()
        for t in range(n):
            for r in range(3):
                piece = shs[t].full_piece(bufs[t], chip_ids[r], c)
                _remote(piece, piece, ssem.at[3 * t + r], rsem.at[3 * t + r], (*chips[r], c)).wait_recv()

    out = pl.pallas_call(
        body, name=name, in_specs=[*[_HBM] * n, _SEM, _SEM, _ANY], out_specs=[_HBM] * n,
        out_shape=[pltpu.HBM(f.shape, f.dtype) for f in fulls], input_output_aliases={t: t for t in range(n)},
        compiler_params=pltpu.CompilerParams(has_side_effects=_EFFECT), **_KW,
    )(*fulls, send_sems, recv_sems, after)
    return list(out)


def _gather_pass_on(fulls, shs, name):
    n = len(fulls)

    def body(*refs):
        bufs = refs[n:2 * n]
        send_sems, recv_sems = refs[2 * n:]
        x, y, c, me, chips, chip_ids = _where_am_i()
        sib = (x, y, 1 - c)
        cps = []
        for t in range(n):
            for r in range(3):
                piece = shs[t].full_piece(bufs[t], chip_ids[r], c)
                cps.append(_remote(piece, piece, send_sems.at[t, r], recv_sems.at[t, r], sib))
        for cp in cps:
            cp.start()
        for t in range(n):
            for r in range(3):
                piece = shs[t].full_piece(bufs[t], chip_ids[r], 1 - c)
                _remote(piece, piece, send_sems.at[t, r], recv_sems.at[t, r], sib).wait_recv()
        for cp in cps:
            cp.wait_send()

    return _call(
        body, name, in_specs=[_ANY] * n, out_specs=[_ANY] * n, out_shape=[S(f.shape, f.dtype) for f in fulls],
        input_output_aliases={t: t for t in range(n)},
        scratch_shapes=[pltpu.SemaphoreType.DMA((n, 3)), pltpu.SemaphoreType.DMA((n, 3))],
        compiler_params=pltpu.CompilerParams(has_side_effects=True),
    )(*fulls)


def _pass_on_copies(bufs, shs, send_sems, recv_sems):
    x, y, c, me, chips, chip_ids = _where_am_i()
    sib = (x, y, 1 - c)
    out, back = [], []
    for t in range(len(bufs)):
        for r in range(3):
            piece = shs[t].full_piece(bufs[t], chip_ids[r], c)
            out.append(_remote(piece, piece, send_sems.at[3 * t + r], recv_sems.at[3 * t + r], sib))
            other = shs[t].full_piece(bufs[t], chip_ids[r], 1 - c)
            back.append(_remote(other, other, send_sems.at[3 * t + r], recv_sems.at[3 * t + r], sib))
    return out, back


def _pass_on_start(fulls, shs, name):
    n = len(fulls)

    def body(*refs):
        for cp in _pass_on_copies(refs[:n], shs, refs[n], refs[n + 1])[0]:
            cp.start()
        refs[-1][...] = jnp.zeros_like(refs[-1])

    out = pl.pallas_call(
        body, name=name, in_specs=[_HBM] * n,
        out_shape=(pltpu.SemaphoreType.DMA((3 * n,)), pltpu.SemaphoreType.DMA((3 * n,)), *[pltpu.HBM(f.shape, f.dtype) for f in fulls],
                   S((8, 128), _F32)),
        out_specs=(_SEM, _SEM, *[_HBM] * n, pl.BlockSpec(memory_space=pltpu.VMEM)),
        input_output_aliases={t: 2 + t for t in range(n)},
        compiler_params=pltpu.CompilerParams(has_side_effects=_EFFECT), **_KW,
    )(*[_in_hbm(f) for f in fulls])
    return out[0], out[1], list(out[2:2 + n]), out[-1]


def _pass_on_wait(fulls, send_sems, recv_sems, after, shs, name):
    n = len(fulls)

    def body(*refs):
        out, back = _pass_on_copies(refs[:n], shs, refs[n], refs[n + 1])
        for cp in out:
            cp.wait_send()
        for cp in back:
            cp.wait_recv()

    out = pl.pallas_call(
        body, name=name, in_specs=[*[_HBM] * n, _SEM, _SEM, _ANY], out_specs=[_HBM] * n,
        out_shape=[pltpu.HBM(f.shape, f.dtype) for f in fulls], input_output_aliases={t: t for t in range(n)},
        compiler_params=pltpu.CompilerParams(has_side_effects=_EFFECT), **_KW,
    )(*fulls, send_sems, recv_sems, after)
    return list(out)


def _rs_pair_copies(ins, outs, shs, send_sems, recv_sems):
    x, y, c, *_ = _where_am_i()
    sib = (x, y, 1 - c)
    cps = []
    for t in range(len(ins)):
        sh = shs[t]
        if sh.by_cols:
            cps.append(_remote(ins[t].at[pl.ds((1 - c) * sh.Rh, sh.Rh), :], outs[t], send_sems.at[4 * t], recv_sems.at[4 * t], sib))
        else:
            for j in range(4):
                cps.append(_remote(sh.full_piece(ins[t], j, 1 - c), sh.half_piece(outs[t], j),
                                   send_sems.at[4 * t + j], recv_sems.at[4 * t + j], sib))
    return cps


def _rs_pair_start(dws, shs, name):
    n = len(dws)
    lands = [lax.empty((sh.R // 2, sh.C), _WIRE) for sh in shs]

    def body(*refs):
        for cp in _rs_pair_copies(refs[:n], refs[n:2 * n], shs, refs[2 * n], refs[2 * n + 1]):
            cp.start()
        refs[-1][...] = jnp.zeros_like(refs[-1])

    out = pl.pallas_call(
        body, name=name, in_specs=[_HBM] * (2 * n),
        out_shape=(pltpu.SemaphoreType.DMA((4 * n,)), pltpu.SemaphoreType.DMA((4 * n,)),
                   *[pltpu.HBM(a.shape, a.dtype) for a in (*dws, *lands)], S((8, 128), _F32)),
        out_specs=(_SEM, _SEM, *[_HBM] * (2 * n), pl.BlockSpec(memory_space=pltpu.VMEM)),
        input_output_aliases={t: 2 + t for t in range(2 * n)},
        compiler_params=pltpu.CompilerParams(has_side_effects=_EFFECT), **_KW,
    )(*[_in_hbm(a) for a in (*dws, *lands)])
    return out[0], out[1], list(out[2:2 + n]), list(out[2 + n:2 + 2 * n]), out[-1]


def _rs_pair_wait(dws, lands, send_sems, recv_sems, after, shs, name):
    n = len(dws)

    def body(*refs):
        cps = _rs_pair_copies(refs[:n], refs[n:2 * n], shs, refs[2 * n], refs[2 * n + 1])
        for cp in cps:
            cp.wait_send()
        for cp in cps:
            cp.wait_recv()

    out = pl.pallas_call(
        body, name=name, in_specs=[*[_HBM] * (2 * n), _SEM, _SEM, _ANY], out_specs=[_HBM] * (2 * n),
        out_shape=[pltpu.HBM(a.shape, a.dtype) for a in (*dws, *lands)], input_output_aliases={t: t for t in range(2 * n)},
        compiler_params=pltpu.CompilerParams(has_side_effects=_EFFECT), **_KW,
    )(*dws, *lands, send_sems, recv_sems, after)
    return list(out[n:])


def _rs_pair_add(dw32, recv, sh, idx):
    tr, tc = _tile(sh.Q, 512, 16), _tile(sh.C, 2048)
    nb = sh.Q // tr

    def body(i_ref, c_ref, a_ref, b_ref, ow_ref):
        ow_ref[...] = (a_ref[...] + b_ref[...].astype(_F32)).astype(ow_ref.dtype)

    if sh.by_cols:
        a_map = lambda j, i, b, si, sc: (sc[0] * 4 * nb + j * nb + i, b)
    else:
        a_map = lambda j, i, b, si, sc: (j * 2 * nb + sc[0] * nb + i, b)
    h_spec = BS((tr, tc), lambda j, i, b, si, sc: (j * nb + i, b))
    return _call(
        body, "rs_pair_add",
        grid_spec=pltpu.PrefetchScalarGridSpec(num_scalar_prefetch=2, grid=(4, nb, sh.C // tc),
                                               in_specs=[BS((tr, tc), a_map), h_spec], out_specs=h_spec),
        out_shape=S((sh.R // 2, sh.C), _WIRE), compiler_params=_cp("parallel", "parallel", "parallel"),
    )(*idx, dw32, recv)


def _rs_chip_start(pws, shs, name):
    n = len(pws)
    lands = [lax.empty((3, sh.Rh, sh.Cs), _WIRE) for sh in shs]

    def body(*refs):
        ins, lnd = refs[:n], refs[n:2 * n]
        send_sems, recv_sems = refs[2 * n], refs[2 * n + 1]
        token = refs[-1]
        x, y, c, me, chips, chip_ids = _where_am_i()
        for t in range(n):
            for r in range(3):
                _remote(shs[t].half_piece(ins[t], chip_ids[r]), lnd[t].at[r], send_sems.at[3 * t + r], recv_sems.at[3 * t + r],
                        (*chips[r], c)).start()
        token[...] = jnp.zeros_like(token)

    out = pl.pallas_call(
        body, name=name, in_specs=[_HBM] * (2 * n),
        out_shape=(pltpu.SemaphoreType.DMA((3 * n,)), pltpu.SemaphoreType.DMA((3 * n,)),
                   *[pltpu.HBM(a.shape, a.dtype) for a in (*pws, *lands)], S((8, 128), _F32)),
        out_specs=(_SEM, _SEM, *[_HBM] * (2 * n), pl.BlockSpec(memory_space=pltpu.VMEM)),
        input_output_aliases={t: 2 + t for t in range(2 * n)},
        compiler_params=pltpu.CompilerParams(has_side_effects=_EFFECT), **_KW,
    )(*[_in_hbm(a) for a in (*pws, *lands)])
    return out[0], out[1], list(out[2:2 + n]), list(out[2 + n:2 + 2 * n]), out[-1]


def _rs_chip_wait(pws, lands, send_sems, recv_sems, after, shs, name):
    n = len(pws)
    after = list(after) if isinstance(after, (list, tuple)) else [after]

    def body(*refs):
        ins, lnd = refs[:n], refs[n:2 * n]
        ssem, rsem = refs[2 * n], refs[2 * n + 1]
        x, y, c, me, chips, chip_ids = _where_am_i()
        for t in range(n):
            for r in range(3):
                cp = _remote(shs[t].half_piece(ins[t], chip_ids[r]), lnd[t].at[r], ssem.at[3 * t + r], rsem.at[3 * t + r], (*chips[r], c))
                cp.wait_send()
        for t in range(n):
            for r in range(3):
                cp = _remote(shs[t].half_piece(ins[t], chip_ids[r]), lnd[t].at[r], ssem.at[3 * t + r], rsem.at[3 * t + r], (*chips[r], c))
                cp.wait_recv()

    out = pl.pallas_call(
        body, name=name, in_specs=[*[_HBM] * (2 * n), _SEM, _SEM, *[_ANY] * len(after)], out_specs=[_HBM] * (2 * n),
        out_shape=[pltpu.HBM(a.shape, a.dtype) for a in (*pws, *lands)], input_output_aliases={t: t for t in range(2 * n)},
        compiler_params=pltpu.CompilerParams(has_side_effects=_EFFECT), **_KW,
    )(*pws, *lands, send_sems, recv_sems, *after)
    return list(out[n:])


def _rs_chip_add(dw32, pair, recv, sh, idx, g_prev, l, L):
    tr, tc = _tile(sh.Rh, 512, 16), _tile(sh.Cs, 2048)
    nr, nc = sh.Rh // tr, sh.Cs // tc

    def body(i_ref, c_ref, d_ref, a_ref, b_ref, *rest):
        rest[-1][...] = ((d_ref[...] + a_ref[...].astype(_F32)) + b_ref[0].astype(_F32) + b_ref[1].astype(_F32)
                         + b_ref[2].astype(_F32))

    if sh.by_cols:
        d_map = lambda a, b, si, sc: (sc[0] * nr + a, si[0] * nc + b)
        a_map = lambda a, b, si, sc: (a, si[0] * nc + b)
    else:
        d_map = lambda a, b, si, sc: ((2 * si[0] + sc[0]) * nr + a, b)
        a_map = lambda a, b, si, sc: (si[0] * nr + a, b)
    in_specs = [BS((tr, tc), d_map), BS((tr, tc), a_map), BS((3, tr, tc), lambda a, b, si, sc: (0, a, b))]
    args = [*idx, dw32, pair, recv]
    if g_prev is not None:
        in_specs.append(_ANY)
        args.append(g_prev)
    return _call(
        body, "rs_chip_add",
        grid_spec=pltpu.PrefetchScalarGridSpec(num_scalar_prefetch=2, grid=(nr, nc), in_specs=in_specs,
                                               out_specs=BS((None, tr, tc), lambda a, b, si, sc: (l, sc[0] * nr + a, b))),
        out_shape=S((L, sh.Rs, sh.Cs), _F32), input_output_aliases={} if g_prev is None else {5: 0},
        compiler_params=_cp("parallel", "parallel"),
    )(*args)


def _rs_pair_share(gs, ls, shs, name):
    n = len(gs)

    def body(*refs):
        bufs = refs[n:2 * n]
        send_sems, recv_sems = refs[2 * n:]
        x, y, c, *_ = _where_am_i()
        sib = (x, y, 1 - c)
        cps = []
        for t in range(n):
            mine = shs[t].shard_half(bufs[t].at[ls[t]], c)
            cps.append(_remote(mine, mine, send_sems.at[t], recv_sems.at[t], sib))
        for cp in cps:
            cp.start()
        for t in range(n):
            other = shs[t].shard_half(bufs[t].at[ls[t]], 1 - c)
            _remote(other, other, send_sems.at[t], recv_sems.at[t], sib).wait_recv()
        for cp in cps:
            cp.wait_send()

    return _call(
        body, name, in_specs=[_ANY] * n, out_specs=[_ANY] * n, out_shape=[S(g.shape, g.dtype) for g in gs],
        input_output_aliases={t: t for t in range(n)},
        scratch_shapes=[pltpu.SemaphoreType.DMA((n,)), pltpu.SemaphoreType.DMA((n,))],
        compiler_params=pltpu.CompilerParams(has_side_effects=True),
    )(*gs)


def _all_reduce_small(xs, dep=None):
    M = xs.shape[0]
    deps = [] if dep is None else [dep]

    def body(x_ref, *rest):
        tot_ref, out_ref, send_sems, recv_sems, local_sem = rest[len(deps):]
        x, y, c, me, chips, chip_ids = _where_am_i()
        sib = (x, y, 1 - c)

        def rows(dev):
            return out_ref.at[pl.ds(_mo((4 * dev[0] + 2 * dev[1] + dev[2]) * M, 8), M), :]

        def copy(k, block, to, src=None):
            return _remote(rows(block) if src is None else src, rows(block), send_sems.at[k], recv_sems.at[k], to)

        mine = pltpu.make_async_copy(x_ref, rows((x, y, c)), local_sem)
        mine.start()
        first = [copy(0, (x, y, c), sib, src=x_ref)]
        first += [copy(1 + j, (x, y, c), (*chip, c), src=x_ref) for j, chip in enumerate(chips)]
        for cp in first:
            cp.start()
        passed = [copy(4 + j, (*chip, c), sib) for j, chip in enumerate(chips)]
        for j, chip in enumerate(chips):
            copy(1 + j, (*chip, c), (x, y, c)).wait_recv()
            passed[j].start()
        copy(0, sib, (x, y, c)).wait_recv()
        for j, chip in enumerate(chips):
            copy(4 + j, (*chip, 1 - c), (x, y, c)).wait_recv()
        for cp in first + passed:
            cp.wait_send()
        mine.wait()
        tot = out_ref[pl.ds(0, M), :]
        for d in range(1, 8):
            tot = tot + out_ref[pl.ds(d * M, M), :]
        tot_ref[...] = tot

    vm = pl.BlockSpec(memory_space=pltpu.VMEM)
    return _call(
        body, "all_reduce_small", in_specs=[vm] + [_ANY] * len(deps), out_specs=[vm, vm],
        out_shape=[S((M, 128), _F32), S((8 * M, 128), _F32)],
        scratch_shapes=[pltpu.SemaphoreType.DMA((7,)), pltpu.SemaphoreType.DMA((7,)), pltpu.SemaphoreType.DMA],
        compiler_params=_cp(has_side_effects=True),
    )(xs, *deps)[0]


def _reduce_scatter_begin(dws, shs, l):
    ssem, rsem, dww, lands, token = _rs_pair_start([d[1] for d in dws], shs, f"rs_pair_start_{l}")
    return ([d[0] for d in dws], dww, lands, ssem, rsem), token


def _reduce_scatter_middle(state, after, shs, idx, l):
    dw32s, dww, lands, ssem, rsem = state
    recv_a = _rs_pair_wait(dww, lands, ssem, rsem, after, shs, f"rs_pair_wait_{l}")
    pws = [_rs_pair_add(d32, ra, sh, idx) for d32, ra, sh in zip(dw32s, recv_a, shs)]
    ssem, rsem, pws, lands, token = _rs_chip_start(pws, shs, f"rs_chip_start_{l}")
    return (dw32s, recv_a, pws, lands, ssem, rsem), token


def _reduce_scatter_end(state, after, tensors, shs, gstack, idx, l):
    dw32s, recv_a, pws, lands, ssem, rsem = state
    recv_b = _rs_chip_wait(pws, lands, ssem, rsem, after, shs, f"rs_chip_wait_{l}")
    gs = [_rs_chip_add(d32, ra, rb, sh, idx, gstack[name], i, L)
          for d32, ra, rb, sh, (name, i, L) in zip(dw32s, recv_a, recv_b, shs, tensors)]
    gs = _rs_pair_share(gs, [i for _, i, _ in tensors], shs, "rs_pair_share")
    for (name, _, _), g in zip(tensors, gs):
        gstack[name] = g


def _pack(parts):
    out = []
    for p in parts:
        p2 = p.reshape(-1, 128)
        pad = (-p2.shape[0]) % 8
        out.append(jnp.pad(p2, ((0, pad), (0, 0))) if pad else p2)
    return jnp.concatenate(out, axis=0)


def _unpack(packed, like):
    out, at = [], 0
    for p in like:
        n = p.size // 128
        out.append(packed[at:at + n].reshape(p.shape))
        at += n + ((-n) % 8)
    return out


def kernel(x, mem, g_mix, g_ffn, w_in_a, g_v_a, w_spatial, b_spatial, w_in_b, g_q_b, g_k_b, g_mem, w_mem_kv, g_mq, g_mk, w_out, w_gate_up, w_down, loss_target, m_g_mix, m_g_ffn, m_w_in_a, m_g_v_a, m_w_spatial, m_b_spatial, m_w_in_b, m_g_q_b, m_g_k_b, m_g_mem, m_w_mem_kv, m_g_mq, m_g_mk, m_w_out, m_w_gate_up, m_w_down, v_g_mix, v_g_ffn, v_w_in_a, v_g_v_a, v_w_spatial, v_b_spatial, v_w_in_b, v_g_q_b, v_g_k_b, v_g_mem, v_w_mem_kv, v_g_mq, v_g_mk, v_w_out, v_w_gate_up, v_w_down):
    xs = x[0]
    mem2 = mem[0]
    target = loss_target[0]
    T, D = xs.shape
    depth = g_mix.shape[0]
    MEMW = w_mem_kv.shape[2] // 2
    TOK = D - MEMW
    KV = (w_in_b.shape[2] * 4 - TOK - MEMW) // 2
    QPK = TOK // KV
    F = w_gate_up.shape[2] * 4 // 2

    idx = ((2 * lax.axis_index("x") + lax.axis_index("y")).astype(jnp.int32).reshape(1), lax.axis_index("c").astype(jnp.int32).reshape(1))

    big = {
        "w_in_a": (w_in_a, _Shard(D, w_in_a.shape[2] * 4, True)),
        "w_in_b": (w_in_b, _Shard(D, w_in_b.shape[2] * 4, True)),
        "w_mem_kv": (w_mem_kv, _Shard(D, 2 * MEMW, False)),
        "w_out": (w_out, _Shard(D, D, False)),
        "w_gate_up": (w_gate_up, _Shard(D, 2 * F, True)),
        "w_down": (w_down, _Shard(F, D, False)),
    }

    def layer_tensors(l):
        n_in = "w_in_a" if l % 2 == 0 else "w_in_b"
        return [(n_in, l // 2, big[n_in][0].shape[0])] + [(n, l, depth) for n in ("w_mem_kv", "w_out", "w_gate_up", "w_down")]

    def layer_shards(l):
        return [big[n][1] for n, _, _ in layer_tensors(l)]

    full = {n: [None] * w.shape[0] for n, (w, _) in big.items()}
    flying = {}

    def start_layer(l, dep):
        tens = layer_tensors(l)
        token = dep
        for gi, group in enumerate([tens[:2], tens[2:3], tens[3:4], tens[4:]]):
            shs = [big[n][1] for n, _, _ in group]
            bufs = [_cast_into_full(big[n][0], i, big[n][1], idx, dep=token) for n, i, _ in group]
            ssem, rsem, bufs, token = _gather_start(bufs, shs, f"gather_start_{l}_{gi}")
            for n, i, _ in group:
                flying[(n, i)] = dict(group=group, shs=shs, state=(ssem, rsem, bufs), name=f"{l}_{gi}", passing=False)
        return token

    def land(members, after):
        for n, i in members:
            fl = flying[(n, i)]
            ssem, rsem, bufs = fl["state"]
            bufs = _gather_wait(bufs, ssem, rsem, after, fl["shs"], "gather_wait_" + fl["name"])
            ssem, rsem, bufs, after = _pass_on_start(bufs, fl["shs"], "pass_on_start_" + fl["name"])
            fl.update(state=(ssem, rsem, bufs), passing=True)
        return after

    def weight(n, i, after):
        if full[n][i] is None:
            fl = flying[(n, i)]
            ssem, rsem, bufs = fl["state"]
            if fl["passing"]:
                bufs = _pass_on_wait(bufs, ssem, rsem, after, fl["shs"], "pass_on_wait_" + fl["name"])
            else:
                bufs = _gather_wait(bufs, ssem, rsem, after, fl["shs"], "gather_wait_" + fl["name"])
                bufs = _gather_pass_on(bufs, fl["shs"], "gather_pass_on")
            for (m, j, _), b in zip(fl["group"], bufs):
                full[m][j] = b
        return full[n][i]

    after = None
    for l in range(depth):
        after = start_layer(l, after)
    tabs = _rope_tables(T)

    saved = []
    xc = xs
    for l in range(depth):
        is_a = l % 2 == 0
        li = l // 2
        w_in = weight("w_in_a" if is_a else "w_in_b", li, after)
        h, ht, r1 = _rmsnorm_fwd(xc, g_mix[l], "rmsnorm_fwd", transposed=True)
        z = _mm_nn("mm_in", h, w_in, pm=2048, pn=512)
        st = dict(x=xc, ht=ht, r1=r1, z=z)
        if is_a:
            ws_m = w_spatial[li].astype(_MXU)
            st["ws_m"], st["wst_m"], st["b_t"] = ws_m, jnp.swapaxes(ws_m, 1, 2), b_spatial[li].T
            tok = _mixer_a_fwd(z, g_v_a[li], ws_m, st["b_t"], TOK, D)
            qblk = 2 * TOK // MEMW
        else:
            q, k, v = _qk_rope_fwd(z, g_q_b[li], g_k_b[li], tabs, TOK, KV)
            tok, stat = _attn_fwd(q, k, v, QPK, D)
            st["q"], st["k"], st["v"], st["stat"] = q, k, v, stat
            qblk = (TOK + 2 * KV) // MEMW
        mem_n = _rmsnorm_fwd(mem2, g_mem[l], "rmsnorm_mem")
        kv = _mm_nn("mm_memkv", mem_n, weight("w_mem_kv", l, z))
        cat = _mem_fwd(z, qblk, kv, g_mq[l], g_mk[l], MEMW, tok)
        token = land([("w_out", l), ("w_gate_up", l), ("w_down", l)], cat) if l > 0 else None
        x1 = _mm_nn("mm_out", cat, weight("w_out", l, cat), add=xc, dep=token)
        h2, h2t, r2 = _rmsnorm_fwd(x1, g_ffn[l], "rmsnorm_fwd", transposed=True)
        st["r2"] = r2
        act, gu = _ffn_gate_up(h2, weight("w_gate_up", l, h2))
        w_down_l = weight("w_down", l, act)
        token = land([layer_tensors(l + 1)[0][:2]], act) if l + 1 < depth else None
        xc = _mm_nn("mm_down", act, w_down_l, add=x1, pm=512, pn=1024, pk=8192, dep=token)
        after = xc
        st.update(mem_n=mem_n, kv=kv, qblk=qblk, cat=cat, x1=x1, h2t=h2t, act=act, gu=gu)
        saved.append(st)

    dx, dxm, sq = _loss_head(xc, target)
    loss = lax.psum(sq[0, 0] * (0.5 / D), ("x", "y", "c"))

    gsm = {n: [None] * len(a) for n, a in dict(g_mix=g_mix, g_ffn=g_ffn, g_v_a=g_v_a, w_spatial=w_spatial, b_spatial=b_spatial,
                                                g_q_b=g_q_b, g_k_b=g_k_b, g_mem=g_mem, g_mq=g_mq, g_mk=g_mk).items()}
    gstack = {n: None for n in big}
    pairing, chipping, token = None, None, None

    def advance(after):
        nonlocal pairing, chipping
        state, tok = _reduce_scatter_middle(pairing[0], after, layer_shards(pairing[1]), idx, pairing[1])
        if chipping is not None:
            _reduce_scatter_end(chipping[0], tok, layer_tensors(chipping[1]), layer_shards(chipping[1]), gstack, idx, chipping[1])
        pairing, chipping = None, (state, pairing[1])
        return tok

    for l in reversed(range(depth)):
        st = saved[l]
        is_a = l % 2 == 0
        li = l // 2
        gbig = {}
        dgu = _ffn_dact(dxm, full["w_down"][l], st["gu"], dep=token)
        token = advance(dgu) if pairing is not None else None
        gbig["w_down"] = _mm_tn_dual("mm_dw_down", st["act"], dxm, pm=512, pn=1024, pk=4096)
        dh2 = _ffn_dh(dgu, full["w_gate_up"][l], dep=token)
        gbig["w_gate_up"] = _ffn_dwgu(st["h2t"], dgu)
        dx, dxm, dg = _rmsnorm_bwd(st["x1"], g_ffn[l], dh2, dx, "rmsnorm_bwd", rstd=st["r2"])
        gsm["g_ffn"][l] = dg[0]
        dcat = _mm_nt("mm_dcat", dxm, full["w_out"][l])
        gbig["w_out"] = _mm_tn_dual("mm_dw_out", st["cat"], dxm, pm=512, pn=1024, pk=4096)
        dzq, dkn, dvm, dgq = _mem_bwd(st["z"], st["qblk"], st["kv"], g_mq[l], g_mk[l], dcat, TOK // MEMW, MEMW)
        dkv, dgk = _memkv_bwd(st["kv"], dkn, dvm, g_mk[l], MEMW)
        gsm["g_mq"][l], gsm["g_mk"][l] = dgq[0], dgk[0]
        gbig["w_mem_kv"] = _mm_tn_dual("mm_dw_memkv", st["mem_n"], dkv)
        dmem_n = _mm_nt("mm_dmemn", dkv, full["w_mem_kv"][l])
        gsm["g_mem"][l] = _rmsnorm_bwd(mem2, g_mem[l], dmem_n, None, "rmsnorm_bwd_mem")[2][0]
        if is_a:
            dz_tok, dws, dbs, dgv = _mixer_a_bwd(st["z"], dcat, g_v_a[li], st["ws_m"], st["wst_m"], st["b_t"], TOK)
            gsm["w_spatial"][li], gsm["b_spatial"][li], gsm["g_v_a"][li] = dws, dbs[:, :, 0], dgv[0]
            dz = jnp.concatenate([dz_tok, dzq], axis=1)
        else:
            dq, dk, dv = _attn_bwd(st["q"], st["k"], st["v"], dcat, st["cat"], st["stat"], QPK)
            dz_qk, dgq_b, dgk_b = _qk_rope_bwd(st["z"], dq, dk, g_q_b[li], g_k_b[li], tabs, TOK, KV)
            gsm["g_q_b"][li], gsm["g_k_b"][li] = dgq_b[0], dgk_b[0]
            dz = jnp.concatenate([dz_qk, dv, dzq], axis=1)
        n_in = "w_in_a" if is_a else "w_in_b"
        dh = _mm_nt("mm_dh", dz, full[n_in][li])
        gbig[n_in] = _mm_nn_dual("mm_dw_in", st["ht"], dz, pm=1024, pn=512, pk=4096)
        dx, dxm, dg = _rmsnorm_bwd(st["x"], g_mix[l], dh, dx, "rmsnorm_bwd", rstd=st["r1"])
        gsm["g_mix"][l] = dg[0]
        state, token = _reduce_scatter_begin([gbig[n] for n, _, _ in layer_tensors(l)], layer_shards(l), l)
        pairing = (state, l)

    small = ["g_mix", "g_ffn", "g_v_a", "w_spatial", "b_spatial", "g_q_b", "g_k_b", "g_mem", "g_mq", "g_mk"]
    env = dict(g_mix=g_mix, g_ffn=g_ffn, g_v_a=g_v_a, w_spatial=w_spatial, b_spatial=b_spatial, g_q_b=g_q_b, g_k_b=g_k_b,
               g_mem=g_mem, g_mq=g_mq, g_mk=g_mk,
               m_g_mix=m_g_mix, m_g_ffn=m_g_ffn, m_g_v_a=m_g_v_a, m_w_spatial=m_w_spatial, m_b_spatial=m_b_spatial,
               m_g_q_b=m_g_q_b, m_g_k_b=m_g_k_b, m_g_mem=m_g_mem, m_g_mq=m_g_mq, m_g_mk=m_g_mk,
               v_g_mix=v_g_mix, v_g_ffn=v_g_ffn, v_g_v_a=v_g_v_a, v_w_spatial=v_w_spatial, v_b_spatial=v_b_spatial,
               v_g_q_b=v_g_q_b, v_g_k_b=v_g_k_b, v_g_mem=v_g_mem, v_g_mq=v_g_mq, v_g_mk=v_g_mk,
               m_w_in_a=m_w_in_a, m_w_in_b=m_w_in_b, m_w_mem_kv=m_w_mem_kv, m_w_out=m_w_out, m_w_gate_up=m_w_gate_up, m_w_down=m_w_down,
               v_w_in_a=v_w_in_a, v_w_in_b=v_w_in_b, v_w_mem_kv=v_w_mem_kv, v_w_out=v_w_out, v_w_gate_up=v_w_gate_up, v_w_down=v_w_down)
    like = [env[n] for n in small]
    g_small = _all_reduce_small(_pack([jnp.stack(gsm[n]) for n in small]), dep=token)

    res = {}
    outs = _adamw(_pack(like)[None], g_small[None], _pack([env["m_" + n] for n in small])[None],
                  _pack([env["v_" + n] for n in small])[None], "adamw_small")
    unpacked = [_unpack(o[0], like) for o in outs]
    for k, n in enumerate(small):
        res[n] = [u[k] for u in unpacked]
    advance(outs[1])
    pending = chipping
    last = {n: i for n, i, _ in layer_tensors(pending[1])}
    early = {}
    for n, (w, _) in big.items():
        L = w.shape[0]
        if n not in last:
            res[n] = _adamw(w, gstack[n], env["m_" + n], env["v_" + n], "adamw_" + n)
        elif L > 1:
            assert last[n] == 0
            early[n] = _adamw(w, gstack[n], env["m_" + n], env["v_" + n], "adamw_early_" + n, l0=1)
    done = [o[1] for o in early.values()] + [res[n][1] for n in big if n in res] + [res[small[0]][1]]
    _reduce_scatter_end(pending[0], done, layer_tensors(pending[1]), layer_shards(pending[1]), gstack, idx, pending[1])
    for n in last:
        res[n] = _adamw(big[n][0], gstack[n], env["m_" + n], env["v_" + n], "adamw_last_" + n, l0=0, l1=1, prev=early.get(n))

    order = ["g_mix", "g_ffn", "w_in_a", "g_v_a", "w_spatial", "b_spatial", "w_in_b", "g_q_b", "g_k_b", "g_mem", "w_mem_kv",
             "g_mq", "g_mk", "w_out", "w_gate_up", "w_down"]
    return (loss, dx.reshape(1, T, D), *[res[n][0] for n in order], *[res[n][1] for n in order],
            *[res[n][2] for n in order], *[res[n][3] for n in order])
```

```python
import jax
import jax.numpy as jnp
import numpy as np
from jax import lax
from jax.experimental import pallas as pl
from jax.experimental.pallas import tpu as pltpu

_F32 = jnp.float32
_MXU = jnp.bfloat16
_WIRE = jnp.bfloat16
_KW = {}

EPS = 1e-6
HEAD = 128
CHUNK = 128
GRID_W = 64
ROPE_THETA = 10000.0
ADAM_LR, ADAM_B1, ADAM_B2, ADAM_EPS, ADAM_WD, ADAM_STEP = 0.001, 0.9, 0.999, 1e-08, 0.01, 10
_SQRT_HALF = float(np.sqrt(0.5))
_INV_SQRT_2PI = float(1.0 / np.sqrt(2.0 * np.pi))
_VMEM_LIMIT = 56 * 1024 * 1024
_MESH = pl.DeviceIdType.MESH

_NN = (((1,), (0,)), ((), ()))
_NT = (((1,), (1,)), ((), ()))
_TN = (((0,), (0,)), ((), ()))

S = jax.ShapeDtypeStruct
BS = pl.BlockSpec
_ANY = pl.BlockSpec(memory_space=pl.ANY)


def _tile(n, pref, mult=128):
    if n <= pref:
        return n
    d = (pref // mult) * mult
    while d >= mult:
        if n % d == 0:
            return d
        d -= mult
    raise ValueError(f"no tile for {n} (pref {pref}, mult {mult})")


def _mo(v, m):
    return v if isinstance(v, int) else pl.multiple_of(v, m)


def _cp(*sem, **kw):
    return pltpu.CompilerParams(dimension_semantics=sem or None, vmem_limit_bytes=_VMEM_LIMIT, **kw)


def _call(body, name, **kw):
    return pl.pallas_call(body, name=name, **kw, **_KW)


def _dot(a, b, dn=_NN):
    return lax.dot_general(a, b, dn, preferred_element_type=_F32)


def _gelu(x):
    return 0.5 * x * (1.0 + lax.erf(x * _SQRT_HALF))


def _gelu_grad(x):
    return 0.5 * (1.0 + lax.erf(x * _SQRT_HALF)) + x * jnp.exp(-0.5 * x * x) * _INV_SQRT_2PI


def _rstd(x):
    return lax.rsqrt(jnp.mean(x * x, axis=-1, keepdims=True) + EPS)


def _norm_bwd(dout, xhat, r, g):
    dy = dout * g
    return r * (dy - xhat * jnp.mean(dy * xhat, axis=-1, keepdims=True))


def _softmax(s):
    e = jnp.exp(s - jnp.max(s, axis=-1, keepdims=True))
    return e * (1.0 / jnp.sum(e, axis=-1, keepdims=True))


def _mm(name, a, b, a_spec, b_spec, dn, grid, acc_shape, out_shape, out_specs, epilogue, extra=(), extra_specs=(), dep=None):
    nk = grid[2]
    n_ex = len(extra)
    deps = [] if dep is None else [dep]
    multi = isinstance(out_shape, (list, tuple))
    n_out = len(out_shape) if multi else 1

    def body(*refs):
        a_ref, b_ref = refs[0], refs[1]
        ex = refs[2:2 + n_ex]
        outs = refs[2 + n_ex + len(deps):2 + n_ex + len(deps) + n_out]

        def prod():
            return _dot(a_ref[...].astype(_MXU), b_ref[...].astype(_MXU), dn)

        if nk == 1:
            epilogue(prod(), ex, outs)
        else:
            acc = refs[-1]
            k = pl.program_id(2)

            @pl.when(k == 0)
            def _():
                acc[...] = jnp.zeros_like(acc)

            acc[...] += prod()

            @pl.when(k == nk - 1)
            def _():
                epilogue(acc[...], ex, outs)

    return _call(
        body, name, grid=grid, in_specs=[a_spec, b_spec, *extra_specs] + [_ANY] * len(deps), out_specs=out_specs, out_shape=out_shape,
        scratch_shapes=[] if nk == 1 else [pltpu.VMEM(acc_shape, _F32)],
        compiler_params=_cp("parallel", "parallel", "arbitrary"),
    )(a, b, *extra, *deps)


def _ep_store(acc, ex, outs):
    for o in outs:
        o[...] = acc.astype(o.dtype)


def _ep_add(acc, ex, outs):
    outs[0][...] = (acc + ex[0][...]).astype(outs[0].dtype)


def _mm_nn(name, a, b, out_dtype=_F32, add=None, pm=1024, pn=1024, pk=2048, dep=None):
    M, K = a.shape
    N = b.shape[1]
    tm, tn, tk = _tile(M, pm, 8), _tile(N, pn), _tile(K, pk)
    o_spec = BS((tm, tn), lambda i, j, k: (i, j))
    return _mm(name, a, b, BS((tm, tk), lambda i, j, k: (i, k)), BS((tk, tn), lambda i, j, k: (k, j)), _NN,
               (M // tm, N // tn, K // tk), (tm, tn), S((M, N), out_dtype), o_spec,
               _ep_store if add is None else _ep_add,
               extra=() if add is None else (add,), extra_specs=() if add is None else (o_spec,), dep=dep)


def _mm_nt(name, a, b, out_dtype=_F32, pm=1024, pn=1024, pk=4096):
    M, K = a.shape
    N = b.shape[0]
    tm, tn, tk = _tile(M, pm, 8), _tile(N, pn), _tile(K, pk)
    return _mm(name, a, b, BS((tm, tk), lambda i, j, k: (i, k)), BS((tn, tk), lambda i, j, k: (j, k)), _NT,
               (M // tm, N // tn, K // tk), (tm, tn), S((M, N), out_dtype), BS((tm, tn), lambda i, j, k: (i, j)), _ep_store)


def _mm_tn_dual(name, a, b, pm=1024, pn=1024, pk=2048, dep=None):
    K, M = a.shape
    N = b.shape[1]
    tm, tn, tk = _tile(M, pm), _tile(N, pn), _tile(K, pk, 16)
    o_spec = BS((tm, tn), lambda i, j, k: (i, j))
    return _mm(name, a, b, BS((tk, tm), lambda i, j, k: (k, i)), BS((tk, tn), lambda i, j, k: (k, j)), _TN,
               (M // tm, N // tn, K // tk), (tm, tn), [S((M, N), _F32), S((M, N), _WIRE)], [o_spec, o_spec], _ep_store, dep=dep)


def _ffn_gate_up(h2, wgu):
    T, D = h2.shape
    F = wgu.shape[1] // 2
    tm, tn = _tile(T, 1024, 8), _tile(F, 512)
    nj = F // tn

    def body(a_ref, bg_ref, bu_ref, act_ref, gu_ref):
        a = a_ref[...]
        g = _dot(a, bg_ref[...])
        u = _dot(a, bu_ref[...])
        sg = 1.0 / (1.0 + jnp.exp(-g))
        silu = g * sg
        gu_ref[0] = (u * (sg * (1.0 + g * (1.0 - sg)))).astype(gu_ref.dtype)
        gu_ref[1] = silu.astype(gu_ref.dtype)
        act_ref[...] = (silu * u).astype(act_ref.dtype)

    return _call(
        body, "ffn_gate_up", grid=(T // tm, nj),
        in_specs=[BS((tm, D), lambda i, j: (i, 0)), BS((D, tn), lambda i, j: (0, j)), BS((D, tn), lambda i, j: (0, j + nj))],
        out_specs=[BS((tm, tn), lambda i, j: (i, j)), BS((2, tm, tn), lambda i, j: (0, i, j))],
        out_shape=[S((T, F), _MXU), S((2, T, F), _MXU)],
        compiler_params=_cp("parallel", "parallel"),
    )(h2, wgu, wgu)


def _ffn_dact(dxm, wdown, gu, dep=None):
    T, D = dxm.shape
    F = wdown.shape[0]
    tm, tn = _tile(T, 2048, 8), _tile(F, 512)
    deps = [] if dep is None else [dep]

    def body(a_ref, b_ref, gu_ref, *rest):
        o_ref = rest[-1]
        d = _dot(a_ref[...], b_ref[...], _NT)
        o_ref[0] = (d * gu_ref[0].astype(_F32)).astype(o_ref.dtype)
        o_ref[1] = (d * gu_ref[1].astype(_F32)).astype(o_ref.dtype)

    return _call(
        body, "ffn_dact", grid=(T // tm, F // tn),
        in_specs=[BS((tm, D), lambda i, j: (i, 0)), BS((tn, D), lambda i, j: (j, 0)), BS((2, tm, tn), lambda i, j: (0, i, j))]
        + [_ANY] * len(deps),
        out_specs=BS((2, tm, tn), lambda i, j: (0, i, j)), out_shape=S((2, T, F), _MXU),
        compiler_params=_cp("parallel", "parallel"),
    )(dxm, wdown, gu, *deps)


def _ffn_dh(dgu, wgu, dep=None):
    _, T, F = dgu.shape
    D = wgu.shape[0]
    tm, tn, tk = _tile(T, 512, 8), _tile(D, 2048), _tile(F, 2816, 256)
    nkf = F // tk
    return _mm("ffn_dh", dgu, wgu, BS((None, tm, tk), lambda i, j, k: (k // nkf, i, k % nkf)),
               BS((tn, tk), lambda i, j, k: (j, k)), _NT, (T // tm, D // tn, 2 * nkf), (tm, tn),
               S((T, D), _F32), BS((tm, tn), lambda i, j, k: (i, j)), _ep_store, dep=dep)


def _ffn_dwgu(h2t, dgu):
    _, T, F = dgu.shape
    D = h2t.shape[0]
    tm, tn, tk = _tile(D, 1024), _tile(F, 512), _tile(T, 4096)
    njf = F // tn
    o_spec = BS((tm, tn), lambda i, j, k: (i, j))
    return _mm("ffn_dwgu", h2t, dgu, BS((tm, tk), lambda i, j, k: (i, k)),
               BS((None, tk, tn), lambda i, j, k: (j // njf, k, j % njf)), _NN, (D // tm, 2 * njf, T // tk), (tm, tn),
               [S((D, 2 * F), _F32), S((D, 2 * F), _WIRE)], [o_spec, o_spec], _ep_store)


def _mm_nn_dual(name, a, b, pm=1024, pn=1024, pk=2048):
    M, K = a.shape
    N = b.shape[1]
    tm, tn, tk = _tile(M, pm), _tile(N, pn), _tile(K, pk)
    o_spec = BS((tm, tn), lambda i, j, k: (i, j))
    return _mm(name, a, b, BS((tm, tk), lambda i, j, k: (i, k)), BS((tk, tn), lambda i, j, k: (k, j)), _NN,
               (M // tm, N // tn, K // tk), (tm, tn), [S((M, N), _F32), S((M, N), _WIRE)], [o_spec, o_spec], _ep_store)


def _rmsnorm_fwd(x, g, name, dep=None, transposed=False):
    T, D = x.shape
    tr = _tile(T, 512, 128)
    n_out = 3 if transposed else 1

    def body(x_ref, g_ref, *rest):
        outs = rest[-n_out:]
        xv = x_ref[...]
        r = _rstd(xv)
        h = (xv * r * g_ref[...]).astype(outs[0].dtype)
        outs[0][...] = h
        if transposed:
            outs[1][...] = h.T
            outs[2][...] = r

    row = BS((tr, D), lambda i: (i, 0))
    deps = [] if dep is None else [dep]
    return _call(body, name, grid=(T // tr,), in_specs=[row, BS((1, D), lambda i: (0, 0))] + [_ANY] * len(deps),
                 out_specs=[row, BS((D, tr), lambda i: (0, i)), BS((tr, 1), lambda i: (i, 0))] if transposed else row,
                 out_shape=[S((T, D), _MXU), S((D, T), _MXU), S((T, 1), _F32)] if transposed else S((T, D), _MXU),
                 compiler_params=_cp("parallel"))(x, g.reshape(1, D), *deps)


def _rmsnorm_bwd(x, g, dh, dres, name, rstd=None):
    T, D = x.shape
    tr = _tile(T, 256, 8)
    has_res = dres is not None
    has_r = rstd is not None

    def body(*refs):
        x_ref, g_ref, dh_ref = refs[:3]
        dx_ref, dxm_ref, dg_ref = refs[-3:]

        @pl.when(pl.program_id(0) == 0)
        def _():
            dg_ref[...] = jnp.zeros_like(dg_ref)

        xv = x_ref[...]
        r = refs[3 + has_res][...] if has_r else _rstd(xv)
        xhat = xv * r
        dh_v = dh_ref[...]
        dg_ref[...] += jnp.sum(dh_v * xhat, axis=0, keepdims=True)
        dx = _norm_bwd(dh_v, xhat, r, g_ref[...])
        if has_res:
            dx = dx + refs[3][...]
        dx_ref[...] = dx
        dxm_ref[...] = dx.astype(dxm_ref.dtype)

    row = BS((tr, D), lambda i: (i, 0))
    vec = BS((1, D), lambda i: (0, 0))
    extra = ([dres] if has_res else []) + ([rstd] if has_r else [])
    return _call(body, name, grid=(T // tr,),
                 in_specs=[row, vec, row] + ([row] if has_res else []) + ([BS((tr, 1), lambda i: (i, 0))] if has_r else []),
                 out_specs=[row, row, vec], out_shape=[S((T, D), _F32), S((T, D), _MXU), S((1, D), _F32)],
                 compiler_params=_cp("arbitrary"))(x, g.reshape(1, D), dh, *extra)


def _loss_head(y, target):
    T, D = y.shape
    tr = _tile(T, 256, 8)

    def body(y_ref, t_ref, dy_ref, dym_ref, acc_ref):
        @pl.when(pl.program_id(0) == 0)
        def _():
            acc_ref[...] = jnp.zeros_like(acc_ref)

        err = y_ref[...] - t_ref[...]
        acc_ref[...] += jnp.sum(jnp.sum(err * err, axis=-1, keepdims=True), axis=0, keepdims=True)
        dy = err * (1.0 / D)
        dy_ref[...] = dy
        dym_ref[...] = dy.astype(dym_ref.dtype)

    row = BS((tr, D), lambda i: (i, 0))
    return _call(body, "loss_head", grid=(T // tr,), in_specs=[row, row],
                 out_specs=[row, row, BS((1, 128), lambda i: (0, 0))],
                 out_shape=[S((T, D), _F32), S((T, D), _MXU), S((1, 128), _F32)],
                 compiler_params=_cp("arbitrary"))(y, target)


def _mixa_blocks(T):
    return 2 if T % (2 * CHUNK) == 0 else 1


def _mixer_a_fwd(z, gv, ws_m, b_t, TOK, width):
    T = z.shape[0]
    G = TOK // HEAD
    CB = _mixa_blocks(T)
    R = CB * CHUNK

    def body(z_ref, gv_ref, ws_ref, bt_ref, o_ref):
        u = _gelu(z_ref[:, :TOK])
        v = _gelu(z_ref[:, TOK:])
        vn = (v * _rstd(v) * gv_ref[...]).astype(_MXU)
        for c in range(CB):
            rows = slice(c * CHUNK, (c + 1) * CHUNK)
            for g in range(G):
                cols = slice(g * HEAD, (g + 1) * HEAD)
                s = _dot(ws_ref[g], vn[rows, cols]) + bt_ref[:, g:g + 1]
                o_ref[rows, cols] = (u[rows, cols] * s).astype(o_ref.dtype)

    return _call(
        body, "mixer_a_fwd", grid=(T // R,),
        in_specs=[BS((R, 2 * TOK), lambda i: (i, 0)), BS((1, TOK), lambda i: (0, 0)),
                  BS((G, CHUNK, CHUNK), lambda i: (0, 0, 0)), BS((CHUNK, G), lambda i: (0, 0))],
        out_specs=BS((R, TOK), lambda i: (i, 0)), out_shape=S((T, width), _MXU), compiler_params=_cp("parallel"),
    )(z, gv.reshape(1, TOK), ws_m, b_t)


def _mixer_a_bwd(z, dcat, gv, ws_m, wst_m, b_t, TOK):
    T = z.shape[0]
    G = TOK // HEAD
    CB = _mixa_blocks(T)
    R = CB * CHUNK
    n = T // R

    def body(z_ref, d_ref, gv_ref, ws_ref, wst_ref, bt_ref, dz_ref, dws_ref, db_ref, dgv_ref, dvn_scr):
        i = pl.program_id(0)

        @pl.when(i == 0)
        def _():
            dws_ref[...] = jnp.zeros_like(dws_ref)
            db_ref[...] = jnp.zeros_like(db_ref)
            dgv_ref[...] = jnp.zeros_like(dgv_ref)

        zu = z_ref[:, :TOK]
        zv = z_ref[:, TOK:]
        u = _gelu(zu)
        v = _gelu(zv)
        r = _rstd(v)
        vhat = v * r
        gvv = gv_ref[...]
        vn = (vhat * gvv).astype(_MXU)
        d = d_ref[...]
        gpu = _gelu_grad(zu)
        for c in range(CB):
            rows = slice(c * CHUNK, (c + 1) * CHUNK)
            for g in range(G):
                cols = slice(g * HEAD, (g + 1) * HEAD)
                vn_cg = vn[rows, cols]
                s = _dot(ws_ref[g], vn_cg) + bt_ref[:, g:g + 1]
                d_cg = d[rows, cols]
                dz_ref[rows, cols] = (d_cg * s * gpu[rows, cols]).astype(dz_ref.dtype)
                ds = d_cg * u[rows, cols]
                ds_m = ds.astype(_MXU)
                dvn_scr[rows, cols] = _dot(wst_ref[g], ds_m)
                dws_ref[g] += _dot(ds_m, vn_cg, _NT)
                db_ref[g] += ds
        dvn = dvn_scr[...]
        dgv_ref[...] += jnp.sum(dvn * vhat, axis=0, keepdims=True)
        dv = _norm_bwd(dvn, vhat, r, gvv)
        dz_ref[:, TOK:] = (dv * _gelu_grad(zv)).astype(dz_ref.dtype)

        @pl.when(i == n - 1)
        def _():
            for g in range(G):
                db_ref[g] = jnp.broadcast_to(jnp.sum(db_ref[g], axis=1, keepdims=True), (CHUNK, CHUNK))

    full3 = BS((G, CHUNK, CHUNK), lambda i: (0, 0, 0))
    return _call(
        body, "mixer_a_bwd", grid=(n,),
        in_specs=[BS((R, 2 * TOK), lambda i: (i, 0)), BS((R, TOK), lambda i: (i, 0)), BS((1, TOK), lambda i: (0, 0)),
                  full3, full3, BS((CHUNK, G), lambda i: (0, 0))],
        out_specs=[BS((R, 2 * TOK), lambda i: (i, 0)), full3, full3, BS((1, TOK), lambda i: (0, 0))],
        out_shape=[S((T, z.shape[1]), _MXU), S((G, CHUNK, CHUNK), _F32), S((G, CHUNK, CHUNK), _F32), S((1, TOK), _F32)],
        scratch_shapes=[pltpu.VMEM((R, TOK), _F32)], compiler_params=_cp("arbitrary"),
    )(z, dcat, gv.reshape(1, TOK), ws_m, wst_m, b_t)


def _rope_tables(T):
    n_rows = T // GRID_W
    rows = jnp.broadcast_to(jnp.arange(n_rows)[:, None], (n_rows, GRID_W)).reshape(T)
    cols = jnp.broadcast_to(jnp.arange(GRID_W)[None, :], (n_rows, GRID_W)).reshape(T)
    pairs = HEAD // 4
    freqs = ROPE_THETA ** (-jnp.arange(pairs, dtype=_F32) / pairs)
    ang_r = rows.astype(_F32)[:, None] * freqs
    ang_c = cols.astype(_F32)[:, None] * freqs
    ang = jnp.concatenate([ang_r, ang_r, ang_c, ang_c], axis=-1)
    cos, sin = jnp.cos(ang), jnp.sin(ang)
    first = (jnp.arange(HEAD) % (HEAD // 2)) < (HEAD // 4)
    return cos, jnp.where(first, -sin, 0.0), jnp.where(first, 0.0, sin)


def _rope(x, cs, sa, sb):
    return x * cs + pltpu.roll(x, 96, 1) * sa + pltpu.roll(x, 32, 1) * sb


def _qk_rope_fwd(z, gq, gk, tabs, TOK, KV):
    T = z.shape[0]
    R = _tile(T, 512, 8)
    W = TOK + 2 * KV

    def body(z_ref, gq_ref, gk_ref, cos_ref, sa_ref, sb_ref, q_ref, k_ref, v_ref):
        cs, sa, sb = cos_ref[...], sa_ref[...], sb_ref[...]
        for h in range((TOK + KV) // HEAD):
            cols = slice(h * HEAD, (h + 1) * HEAD)
            xv = z_ref[:, cols]
            xn = xv * _rstd(xv) * (gq_ref[...] if h < TOK // HEAD else gk_ref[...])
            out = _rope(xn, cs, sa, sb)
            if h < TOK // HEAD:
                q_ref[:, cols] = out.astype(q_ref.dtype)
            else:
                k_ref[:, h * HEAD - TOK:(h + 1) * HEAD - TOK] = out.astype(k_ref.dtype)
        v_ref[...] = z_ref[:, TOK + KV:].astype(v_ref.dtype)

    vec = BS((1, HEAD), lambda i: (0, 0))
    tab = BS((R, HEAD), lambda i: (i, 0))
    return _call(
        body, "qk_rope_fwd", grid=(T // R,), in_specs=[BS((R, W), lambda i: (i, 0)), vec, vec, tab, tab, tab],
        out_specs=[BS((R, TOK), lambda i: (i, 0)), BS((R, KV), lambda i: (i, 0)), BS((R, KV), lambda i: (i, 0))],
        out_shape=[S((T, TOK), _MXU), S((T, KV), _MXU), S((T, KV), _MXU)], compiler_params=_cp("parallel"),
    )(z, gq.reshape(1, HEAD), gk.reshape(1, HEAD), *tabs)


def _qk_rope_bwd(z, dq, dk, gq, gk, tabs, TOK, KV, into):
    T = z.shape[0]
    R = _tile(T, 512, 8)
    W = TOK + KV

    def body(z_ref, dq_ref, dk_ref, gq_ref, gk_ref, cos_ref, sa_ref, sb_ref, into_ref, dz_ref, dgq_ref, dgk_ref):
        @pl.when(pl.program_id(0) == 0)
        def _():
            dgq_ref[...] = jnp.zeros_like(dgq_ref)
            dgk_ref[...] = jnp.zeros_like(dgk_ref)

        cs, sa, sb = cos_ref[...], sa_ref[...], sb_ref[...]
        for h in range(W // HEAD):
            cols = slice(h * HEAD, (h + 1) * HEAD)
            is_q = h < TOK // HEAD
            do = dq_ref[:, cols] if is_q else dk_ref[:, h * HEAD - TOK:(h + 1) * HEAD - TOK]
            dxn = do * cs - pltpu.roll(do, 96, 1) * sa - pltpu.roll(do, 32, 1) * sb
            xv = z_ref[:, cols]
            r = _rstd(xv)
            xhat = xv * r
            dg_ref = dgq_ref if is_q else dgk_ref
            dg_ref[...] += jnp.sum(dxn * xhat, axis=0, keepdims=True)
            dz_ref[:, cols] = _norm_bwd(dxn, xhat, r, gq_ref[...] if is_q else gk_ref[...]).astype(dz_ref.dtype)

    vec = BS((1, HEAD), lambda i: (0, 0))
    tab = BS((R, HEAD), lambda i: (i, 0))
    return _call(
        body, "qk_rope_bwd", grid=(T // R,),
        in_specs=[BS((R, W), lambda i: (i, 0)), BS((R, TOK), lambda i: (i, 0)), BS((R, KV), lambda i: (i, 0)), vec, vec, tab, tab, tab,
                  _ANY],
        out_specs=[BS((R, W), lambda i: (i, 0)), vec, vec],
        out_shape=[S(into.shape, into.dtype), S((1, HEAD), _F32), S((1, HEAD), _F32)], input_output_aliases={8: 0},
        compiler_params=_cp("arbitrary"),
    )(z, dq, dk, gq.reshape(1, HEAD), gk.reshape(1, HEAD), *tabs, into)


_ATTN_C2 = float(HEAD ** -0.5 * np.log2(np.e))


def _attn_fwd(q, k, v, QPK, width):
    T, TOK = q.shape
    KVH = k.shape[1] // HEAD
    tq = _tile(T, 1024, 8)
    ts = 256 if tq % 256 == 0 else tq
    W = QPK * HEAD

    def body(q_ref, k_ref, v_ref, o_ref, st_ref, vaug):
        @pl.when(pl.program_id(1) == 0)
        def _():
            vaug[:, :HEAD] = v_ref[...]
            vaug[:, HEAD:] = jnp.ones((T, HEAD), vaug.dtype)

        kk, va = k_ref[...], vaug[...]
        for g in range(QPK):
            cols = slice(g * HEAD, (g + 1) * HEAD)
            for r in range(tq // ts):
                rows = slice(r * ts, (r + 1) * ts)
                s = _dot(q_ref[rows, cols], kk, _NT)
                m = jnp.max(s, axis=-1, keepdims=True)
                ov = _dot(jnp.exp2((s - m) * _ATTN_C2).astype(_MXU), va)
                l = ov[:, HEAD:HEAD + 1]
                o_ref[rows, cols] = (ov[:, :HEAD] * (1.0 / l)).astype(o_ref.dtype)
                st_ref[rows, g:g + 1] = m + jnp.log2(l) * (1.0 / _ATTN_C2)

    qs = BS((tq, W), lambda h, i: (i, h))
    ks = BS((T, HEAD), lambda h, i: (0, h))
    return _call(body, "attn_fwd", grid=(KVH, T // tq), in_specs=[qs, ks, ks],
                 out_specs=[qs, BS((None, tq, QPK), lambda h, i: (h, i, 0))],
                 out_shape=[S((T, width), _MXU), S((KVH, T, QPK), _F32)],
                 scratch_shapes=[pltpu.VMEM((T, 2 * HEAD), _MXU)],
                 compiler_params=_cp("parallel", "arbitrary"))(q, k, v)


def _attn_bwd(q, k, v, dcat, o, stat, QPK, width):
    T, TOK = q.shape
    KV = k.shape[1]
    KVH = KV // HEAD
    tq = _tile(T, 512, 8)
    ts = 256 if tq % 256 == 0 else tq
    nq = T // tq
    W = QPK * HEAD
    scale = HEAD ** -0.5

    def body(q_ref, k_ref, v_ref, do_ref, o_ref, st_ref, dq_ref, dk_ref, dv_ref, dk_acc, dv_acc, ds_all, p_all, q_all, do_all):
        i = pl.program_id(1)

        @pl.when(i == 0)
        def _():
            dk_acc[...] = jnp.zeros_like(dk_acc)
            dv_acc[...] = jnp.zeros_like(dv_acc)

        kk, vv = k_ref[...], v_ref[...]
        for r in range(tq // ts):
            rows = slice(r * ts, (r + 1) * ts)
            for g in range(QPK):
                cols = slice(g * HEAD, (g + 1) * HEAD)
                stack = slice(g * ts, (g + 1) * ts)
                qg = q_ref[rows, cols]
                p = jnp.exp2((_dot(qg, kk, _NT) - st_ref[rows, g:g + 1]) * _ATTN_C2)
                do32 = do_ref[rows, cols]
                do = do32.astype(_MXU)
                delta = jnp.sum(do32 * o_ref[rows, cols].astype(_F32), axis=-1, keepdims=True)
                ds = (p * (_dot(do, vv, _NT) - delta)).astype(_MXU)
                dq_ref[rows, cols] = _dot(ds, kk) * scale
                ds_all[stack, :] = ds
                p_all[stack, :] = p.astype(_MXU)
                q_all[stack, :] = qg
                do_all[stack, :] = do
            dk_acc[...] += _dot(ds_all[...], q_all[...], _TN)
            dv_acc[...] += _dot(p_all[...], do_all[...], _TN)

        @pl.when(i == nq - 1)
        def _():
            dk_ref[...] = dk_acc[...] * scale
            dv_ref[...] = dv_acc[...].astype(dv_ref.dtype)

    qs = BS((tq, W), lambda h, i: (i, h))
    ks = BS((T, HEAD), lambda h, i: (0, h))
    return _call(
        body, "attn_bwd", grid=(KVH, nq), in_specs=[qs, ks, ks, qs, qs, BS((None, tq, QPK), lambda h, i: (h, i, 0))],
        out_specs=[qs, ks, BS((T, HEAD), lambda h, i: (0, (TOK + KV) // HEAD + h))],
        out_shape=[S((T, TOK), _F32), S((T, KV), _F32), S((T, width), _MXU)],
        scratch_shapes=[pltpu.VMEM((T, HEAD), _F32), pltpu.VMEM((T, HEAD), _F32), pltpu.VMEM((QPK * ts, T), _MXU),
                        pltpu.VMEM((QPK * ts, T), _MXU), pltpu.VMEM((QPK * ts, HEAD), _MXU), pltpu.VMEM((QPK * ts, HEAD), _MXU)],
        compiler_params=_cp("parallel", "arbitrary"),
    )(q, k, v, dcat, o, stat)


def _mem_fwd(z, qblk, kv, gmq, gmk, MEMW, into):
    T = z.shape[0]
    NM = kv.shape[0]
    tq = _tile(T, 512, 8)
    scale = HEAD ** -0.5
    oblk = into.shape[1] // MEMW - 1

    def body(q_ref, kv_ref, gq_ref, gk_ref, into_ref, o_ref):
        for h in range(MEMW // HEAD):
            cols = slice(h * HEAD, (h + 1) * HEAD)
            kx = kv_ref[:, cols]
            kn = (kx * _rstd(kx) * gk_ref[...]).astype(_MXU)
            vv = kv_ref[:, MEMW + h * HEAD:MEMW + (h + 1) * HEAD].astype(_MXU)
            qx = q_ref[:, cols]
            qn = (qx * _rstd(qx) * gq_ref[...]).astype(_MXU)
            p = _softmax(_dot(qn, kn, _NT) * scale)
            o_ref[:, cols] = _dot(p.astype(_MXU), vv).astype(o_ref.dtype)

    vec = BS((1, HEAD), lambda i: (0, 0))
    return _call(
        body, "mem_fwd", grid=(T // tq,),
        in_specs=[BS((tq, MEMW), lambda i: (i, qblk)), BS((NM, 2 * MEMW), lambda i: (0, 0)), vec, vec, _ANY],
        out_specs=BS((tq, MEMW), lambda i: (i, oblk)), out_shape=S(into.shape, into.dtype), input_output_aliases={4: 0},
        compiler_params=_cp("parallel"),
    )(z, kv, gmq.reshape(1, HEAD), gmk.reshape(1, HEAD), into)


def _mem_bwd(z, qblk, kv, gmq, gmk, dcat, dblk, MEMW, into):
    T = z.shape[0]
    NM = kv.shape[0]
    tq = _tile(T, 512, 8)
    scale = HEAD ** -0.5

    def body(q_ref, kv_ref, gq_ref, gk_ref, do_ref, into_ref, dz_ref, dkn_ref, dv_ref, dgq_ref):
        @pl.when(pl.program_id(0) == 0)
        def _():
            dkn_ref[...] = jnp.zeros_like(dkn_ref)
            dv_ref[...] = jnp.zeros_like(dv_ref)
            dgq_ref[...] = jnp.zeros_like(dgq_ref)

        for h in range(MEMW // HEAD):
            cols = slice(h * HEAD, (h + 1) * HEAD)
            kx = kv_ref[:, cols]
            kn = (kx * _rstd(kx) * gk_ref[...]).astype(_MXU)
            vv = kv_ref[:, MEMW + h * HEAD:MEMW + (h + 1) * HEAD].astype(_MXU)
            qx = q_ref[:, cols]
            rq = _rstd(qx)
            qhat = qx * rq
            qn = (qhat * gq_ref[...]).astype(_MXU)
            p = _softmax(_dot(qn, kn, _NT) * scale)
            do = do_ref[:, cols].astype(_MXU)
            dp = _dot(do, vv, _NT)
            ds = (p * (dp - jnp.sum(p * dp, axis=-1, keepdims=True)) * scale).astype(_MXU)
            dqn = _dot(ds, kn)
            dkn_ref[:, cols] += _dot(ds, qn, _TN)
            dv_ref[:, cols] += _dot(p.astype(_MXU), do, _TN)
            dgq_ref[...] += jnp.sum(dqn * qhat, axis=0, keepdims=True)
            dz_ref[:, cols] = _norm_bwd(dqn, qhat, rq, gq_ref[...]).astype(dz_ref.dtype)

    vec = BS((1, HEAD), lambda i: (0, 0))
    kvs = BS((NM, MEMW), lambda i: (0, 0))
    return _call(
        body, "mem_bwd", grid=(T // tq,),
        in_specs=[BS((tq, MEMW), lambda i: (i, qblk)), BS((NM, 2 * MEMW), lambda i: (0, 0)), vec, vec,
                  BS((tq, MEMW), lambda i: (i, dblk)), _ANY],
        out_specs=[BS((tq, MEMW), lambda i: (i, qblk)), kvs, kvs, vec],
        out_shape=[S(into.shape, into.dtype), S((NM, MEMW), _F32), S((NM, MEMW), _F32), S((1, HEAD), _F32)],
        input_output_aliases={5: 0}, compiler_params=_cp("arbitrary"),
    )(z, kv, gmq.reshape(1, HEAD), gmk.reshape(1, HEAD), dcat, into)


def _memkv_bwd(kv, dkn, dv, gmk, MEMW):
    NM = kv.shape[0]

    def body(kv_ref, dkn_ref, dv_ref, gk_ref, dkv_ref, dgk_ref):
        dgk = jnp.zeros((1, HEAD), _F32)
        for h in range(MEMW // HEAD):
            cols = slice(h * HEAD, (h + 1) * HEAD)
            kx = kv_ref[:, cols]
            r = _rstd(kx)
            khat = kx * r
            dkn = dkn_ref[:, cols]
            dgk = dgk + jnp.sum(dkn * khat, axis=0, keepdims=True)
            dkv_ref[:, cols] = _norm_bwd(dkn, khat, r, gk_ref[...]).astype(dkv_ref.dtype)
        dgk_ref[...] = dgk
        dkv_ref[:, MEMW:] = dv_ref[...].astype(dkv_ref.dtype)

    return _call(body, "memkv_bwd", out_shape=[S((NM, 2 * MEMW), _MXU), S((1, HEAD), _F32)],
                 compiler_params=_cp())(kv, dkn, dv, gmk.reshape(1, HEAD))


def _cast_into_full(w, l, sh, idx, dep=None):
    tr, tc = _tile(sh.Rs, 512, 16), _tile(sh.Cs, 2048)
    nr, nc = sh.Rs // tr, sh.Cs // tc
    deps = [] if dep is None else [dep]

    def body(i_ref, c_ref, w_ref, *rest):
        rest[-1][...] = w_ref[...].astype(rest[-1].dtype)

    if sh.by_cols:
        o_map = lambda a, b, si, sc: (a, si[0] * nc + b)
    else:
        o_map = lambda a, b, si, sc: (si[0] * nr + a, b)
    return _call(
        body, "cast_into_full",
        grid_spec=pltpu.PrefetchScalarGridSpec(
            num_scalar_prefetch=2, grid=(nr, nc),
            in_specs=[BS((None, tr, tc), lambda a, b, si, sc: (l, a, b))] + [_ANY] * len(deps), out_specs=BS((tr, tc), o_map)),
        out_shape=S((sh.R, sh.C), _WIRE), compiler_params=_cp("parallel", "parallel"),
    )(*idx, w, *deps)


def _adamw(w, g, m, v, name, l0=0, l1=None, prev=None):
    L, R, C = w.shape
    l1 = L if l1 is None else l1
    tc = _tile(C, 2048)
    tr = _tile(R, max(8, (512 * 1024) // tc), 8)
    c_m = 1.0 / (1.0 - ADAM_B1 ** ADAM_STEP)
    c_v = 1.0 / (1.0 - ADAM_B2 ** ADAM_STEP)

    def body(w_ref, g_ref, m_ref, v_ref, *rest):
        go_ref, d_ref, mo_ref, vo_ref = rest[-4:]
        gv = g_ref[...]
        mn = ADAM_B1 * m_ref[...] + (1.0 - ADAM_B1) * gv
        vn = ADAM_B2 * v_ref[...] + (1.0 - ADAM_B2) * (gv * gv)
        go_ref[...] = gv
        mo_ref[...] = mn
        vo_ref[...] = vn
        d_ref[...] = -ADAM_LR * ((mn * c_m) / (jnp.sqrt(vn * c_v) + ADAM_EPS) + ADAM_WD * w_ref[...])

    blk = BS((None, tr, tc), lambda a, i, j: (l0 + a, i, j))
    prevs = [] if prev is None else list(prev)
    return _call(body, name, grid=(l1 - l0, R // tr, C // tc), in_specs=[blk] * 4 + [_ANY] * len(prevs), out_specs=[blk] * 4,
                 out_shape=[S((L, R, C), _F32)] * 4, input_output_aliases={4 + k: k for k in range(len(prevs))},
                 compiler_params=_cp("parallel", "parallel", "parallel"))(w, g, m, v, *prevs)


def _where_am_i():
    x, y, c = lax.axis_index("x"), lax.axis_index("y"), lax.axis_index("c")
    chips = [(1 - x, y), (x, 1 - y), (1 - x, 1 - y)]
    return x, y, c, 2 * x + y, chips, [2 * cx + cy for cx, cy in chips]


class _Shard:
    def __init__(self, R, C, by_cols):
        self.R, self.C, self.by_cols = R, C, by_cols
        self.Rs, self.Cs = (R, C // 4) if by_cols else (R // 4, C)
        self.Rh = self.Rs // 2
        self.Q = R // 8

    def full_piece(self, ref, j, cc):
        if self.by_cols:
            return ref.at[pl.ds(cc * self.Rh, self.Rh), pl.ds(_mo(j * self.Cs, 128), self.Cs)]
        return ref.at[pl.ds(_mo(j * self.Rs + cc * self.Rh, 16), self.Rh), :]

    def full_shard(self, ref, j):
        if self.by_cols:
            return ref.at[:, pl.ds(_mo(j * self.Cs, 128), self.Cs)]
        return ref.at[pl.ds(_mo(j * self.Rs, 16), self.Rs), :]

    def shard_half(self, ref, cc):
        return ref.at[pl.ds(_mo(cc * self.Rh, 16), self.Rh), :]

    def half_piece(self, ref, j):
        if self.by_cols:
            return ref.at[:, pl.ds(_mo(j * self.Cs, 128), self.Cs)]
        return ref.at[pl.ds(_mo(j * self.Rh, 16), self.Rh), :]


def _remote(src, dst, ssem, rsem, dev):
    return pltpu.make_async_remote_copy(src_ref=src, dst_ref=dst, send_sem=ssem, recv_sem=rsem, device_id=dev, device_id_type=_MESH)


_HBM = pl.BlockSpec(memory_space=pltpu.HBM)
_SEM = pl.BlockSpec(memory_space=pltpu.SEMAPHORE)
_EFFECT = pltpu.SideEffectType.DATAFLOW_SIDE_EFFECTING


def _in_hbm(a):
    return pltpu.with_memory_space_constraint(a, pltpu.HBM)


def _gather_start(fulls, shs, name):
    n = len(fulls)

    def body(*refs):
        bufs = refs[:n]
        send_sems, recv_sems = refs[n], refs[n + 1]
        token = refs[-1]
        x, y, c, me, chips, chip_ids = _where_am_i()
        for t in range(n):
            mine = shs[t].full_piece(bufs[t], me, c)
            for r in range(3):
                _remote(mine, mine, send_sems.at[3 * t + r], recv_sems.at[3 * t + r], (*chips[r], c)).start()
        token[...] = jnp.zeros_like(token)

    out = pl.pallas_call(
        body, name=name, in_specs=[_HBM] * n,
        out_shape=(pltpu.SemaphoreType.DMA((3 * n,)), pltpu.SemaphoreType.DMA((3 * n,)), *[pltpu.HBM(f.shape, f.dtype) for f in fulls],
                   S((8, 128), _F32)),
        out_specs=(_SEM, _SEM, *[_HBM] * n, pl.BlockSpec(memory_space=pltpu.VMEM)),
        input_output_aliases={t: 2 + t for t in range(n)},
        compiler_params=pltpu.CompilerParams(has_side_effects=_EFFECT), **_KW,
    )(*[_in_hbm(f) for f in fulls])
    return out[0], out[1], list(out[2:2 + n]), out[-1]


def _gather_wait(fulls, send_sems, recv_sems, after, shs, name):
    n = len(fulls)

    def body(*refs):
        bufs = refs[:n]
        ssem, rsem = refs[n], refs[n + 1]
        x, y, c, me, chips, chip_ids = _where_am_i()
        for t in range(n):
            mine = shs[t].full_piece(bufs[t], me, c)
            for r in range(3):
                _remote(mine, mine, ssem.at[3 * t + r], rsem.at[3 * t + r], (*chips[r], c)).wait_send()
        for t in range(n):
            for r in range(3):
                piece = shs[t].full_piece(bufs[t], chip_ids[r], c)
                _remote(piece, piece, ssem.at[3 * t + r], rsem.at[3 * t + r], (*chips[r], c)).wait_recv()

    out = pl.pallas_call(
        body, name=name, in_specs=[*[_HBM] * n, _SEM, _SEM, _ANY], out_specs=[_HBM] * n,
        out_shape=[pltpu.HBM(f.shape, f.dtype) for f in fulls], input_output_aliases={t: t for t in range(n)},
        compiler_params=pltpu.CompilerParams(has_side_effects=_EFFECT), **_KW,
    )(*fulls, send_sems, recv_sems, after)
    return list(out)


def _gather_pass_on(fulls, shs, name):
    n = len(fulls)

    def body(*refs):
        bufs = refs[n:2 * n]
        send_sems, recv_sems = refs[2 * n:]
        x, y, c, me, chips, chip_ids = _where_am_i()
        sib = (x, y, 1 - c)
        cps = []
        for t in range(n):
            for r in range(3):
                piece = shs[t].full_piece(bufs[t], chip_ids[r], c)
                cps.append(_remote(piece, piece, send_sems.at[t, r], recv_sems.at[t, r], sib))
        for cp in cps:
            cp.start()
        for t in range(n):
            for r in range(3):
                piece = shs[t].full_piece(bufs[t], chip_ids[r], 1 - c)
                _remote(piece, piece, send_sems.at[t, r], recv_sems.at[t, r], sib).wait_recv()
        for cp in cps:
            cp.wait_send()

    return _call(
        body, name, in_specs=[_ANY] * n, out_specs=[_ANY] * n, out_shape=[S(f.shape, f.dtype) for f in fulls],
        input_output_aliases={t: t for t in range(n)},
        scratch_shapes=[pltpu.SemaphoreType.DMA((n, 3)), pltpu.SemaphoreType.DMA((n, 3))],
        compiler_params=pltpu.CompilerParams(has_side_effects=True),
    )(*fulls)


def _pass_on_copies(bufs, shs, send_sems, recv_sems):
    x, y, c, me, chips, chip_ids = _where_am_i()
    sib = (x, y, 1 - c)
    out, back = [], []
    for t in range(len(bufs)):
        for r in range(3):
            piece = shs[t].full_piece(bufs[t], chip_ids[r], c)
            out.append(_remote(piece, piece, send_sems.at[3 * t + r], recv_sems.at[3 * t + r], sib))
            other = shs[t].full_piece(bufs[t], chip_ids[r], 1 - c)
            back.append(_remote(other, other, send_sems.at[3 * t + r], recv_sems.at[3 * t + r], sib))
    return out, back


def _pass_on_start(fulls, shs, name):
    n = len(fulls)

    def body(*refs):
        for cp in _pass_on_copies(refs[:n], shs, refs[n], refs[n + 1])[0]:
            cp.start()
        refs[-1][...] = jnp.zeros_like(refs[-1])

    out = pl.pallas_call(
        body, name=name, in_specs=[_HBM] * n,
        out_shape=(pltpu.SemaphoreType.DMA((3 * n,)), pltpu.SemaphoreType.DMA((3 * n,)), *[pltpu.HBM(f.shape, f.dtype) for f in fulls],
                   S((8, 128), _F32)),
        out_specs=(_SEM, _SEM, *[_HBM] * n, pl.BlockSpec(memory_space=pltpu.VMEM)),
        input_output_aliases={t: 2 + t for t in range(n)},
        compiler_params=pltpu.CompilerParams(has_side_effects=_EFFECT), **_KW,
    )(*[_in_hbm(f) for f in fulls])
    return out[0], out[1], list(out[2:2 + n]), out[-1]


def _pass_on_wait(fulls, send_sems, recv_sems, after, shs, name):
    n = len(fulls)

    def body(*refs):
        out, back = _pass_on_copies(refs[:n], shs, refs[n], refs[n + 1])
        for cp in out:
            cp.wait_send()
        for cp in back:
            cp.wait_recv()

    out = pl.pallas_call(
        body, name=name, in_specs=[*[_HBM] * n, _SEM, _SEM, _ANY], out_specs=[_HBM] * n,
        out_shape=[pltpu.HBM(f.shape, f.dtype) for f in fulls], input_output_aliases={t: t for t in range(n)},
        compiler_params=pltpu.CompilerParams(has_side_effects=_EFFECT), **_KW,
    )(*fulls, send_sems, recv_sems, after)
    return list(out)


def _rs_pair_copies(ins, outs, shs, send_sems, recv_sems):
    x, y, c, *_ = _where_am_i()
    sib = (x, y, 1 - c)
    cps = []
    for t in range(len(ins)):
        sh = shs[t]
        if sh.by_cols:
            cps.append(_remote(ins[t].at[pl.ds((1 - c) * sh.Rh, sh.Rh), :], outs[t], send_sems.at[4 * t], recv_sems.at[4 * t], sib))
        else:
            for j in range(4):
                cps.append(_remote(sh.full_piece(ins[t], j, 1 - c), sh.half_piece(outs[t], j),
                                   send_sems.at[4 * t + j], recv_sems.at[4 * t + j], sib))
    return cps


def _rs_pair_start(dws, shs, name):
    n = len(dws)
    lands = [lax.empty((sh.R // 2, sh.C), _WIRE) for sh in shs]

    def body(*refs):
        for cp in _rs_pair_copies(refs[:n], refs[n:2 * n], shs, refs[2 * n], refs[2 * n + 1]):
            cp.start()
        refs[-1][...] = jnp.zeros_like(refs[-1])

    out = pl.pallas_call(
        body, name=name, in_specs=[_HBM] * (2 * n),
        out_shape=(pltpu.SemaphoreType.DMA((4 * n,)), pltpu.SemaphoreType.DMA((4 * n,)),
                   *[pltpu.HBM(a.shape, a.dtype) for a in (*dws, *lands)], S((8, 128), _F32)),
        out_specs=(_SEM, _SEM, *[_HBM] * (2 * n), pl.BlockSpec(memory_space=pltpu.VMEM)),
        input_output_aliases={t: 2 + t for t in range(2 * n)},
        compiler_params=pltpu.CompilerParams(has_side_effects=_EFFECT), **_KW,
    )(*[_in_hbm(a) for a in (*dws, *lands)])
    return out[0], out[1], list(out[2:2 + n]), list(out[2 + n:2 + 2 * n]), out[-1]


def _rs_pair_wait(dws, lands, send_sems, recv_sems, after, shs, name):
    n = len(dws)

    def body(*refs):
        cps = _rs_pair_copies(refs[:n], refs[n:2 * n], shs, refs[2 * n], refs[2 * n + 1])
        for cp in cps:
            cp.wait_send()
        for cp in cps:
            cp.wait_recv()

    out = pl.pallas_call(
        body, name=name, in_specs=[*[_HBM] * (2 * n), _SEM, _SEM, _ANY], out_specs=[_HBM] * (2 * n),
        out_shape=[pltpu.HBM(a.shape, a.dtype) for a in (*dws, *lands)], input_output_aliases={t: t for t in range(2 * n)},
        compiler_params=pltpu.CompilerParams(has_side_effects=_EFFECT), **_KW,
    )(*dws, *lands, send_sems, recv_sems, after)
    return list(out[n:])


def _rs_pair_add(dw32, recv, sh, idx):
    tr, tc = _tile(sh.Q, 512, 16), _tile(sh.C, 2048)
    nb = sh.Q // tr

    def body(i_ref, c_ref, a_ref, b_ref, ow_ref):
        ow_ref[...] = (a_ref[...] + b_ref[...].astype(_F32)).astype(ow_ref.dtype)

    if sh.by_cols:
        a_map = lambda j, i, b, si, sc: (sc[0] * 4 * nb + j * nb + i, b)
    else:
        a_map = lambda j, i, b, si, sc: (j * 2 * nb + sc[0] * nb + i, b)
    h_spec = BS((tr, tc), lambda j, i, b, si, sc: (j * nb + i, b))
    return _call(
        body, "rs_pair_add",
        grid_spec=pltpu.PrefetchScalarGridSpec(num_scalar_prefetch=2, grid=(4, nb, sh.C // tc),
                                               in_specs=[BS((tr, tc), a_map), h_spec], out_specs=h_spec),
        out_shape=S((sh.R // 2, sh.C), _WIRE), compiler_params=_cp("parallel", "parallel", "parallel"),
    )(*idx, dw32, recv)


def _rs_chip_start(pws, shs, name):
    n = len(pws)
    lands = [lax.empty((3, sh.Rh, sh.Cs), _WIRE) for sh in shs]

    def body(*refs):
        ins, lnd = refs[:n], refs[n:2 * n]
        send_sems, recv_sems = refs[2 * n], refs[2 * n + 1]
        token = refs[-1]
        x, y, c, me, chips, chip_ids = _where_am_i()
        for t in range(n):
            for r in range(3):
                _remote(shs[t].half_piece(ins[t], chip_ids[r]), lnd[t].at[r], send_sems.at[3 * t + r], recv_sems.at[3 * t + r],
                        (*chips[r], c)).start()
        token[...] = jnp.zeros_like(token)

    out = pl.pallas_call(
        body, name=name, in_specs=[_HBM] * (2 * n),
        out_shape=(pltpu.SemaphoreType.DMA((3 * n,)), pltpu.SemaphoreType.DMA((3 * n,)),
                   *[pltpu.HBM(a.shape, a.dtype) for a in (*pws, *lands)], S((8, 128), _F32)),
        out_specs=(_SEM, _SEM, *[_HBM] * (2 * n), pl.BlockSpec(memory_space=pltpu.VMEM)),
        input_output_aliases={t: 2 + t for t in range(2 * n)},
        compiler_params=pltpu.CompilerParams(has_side_effects=_EFFECT), **_KW,
    )(*[_in_hbm(a) for a in (*pws, *lands)])
    return out[0], out[1], list(out[2:2 + n]), list(out[2 + n:2 + 2 * n]), out[-1]


def _rs_chip_wait(pws, lands, send_sems, recv_sems, after, shs, name):
    n = len(pws)
    after = list(after) if isinstance(after, (list, tuple)) else [after]

    def body(*refs):
        ins, lnd = refs[:n], refs[n:2 * n]
        ssem, rsem = refs[2 * n], refs[2 * n + 1]
        x, y, c, me, chips, chip_ids = _where_am_i()
        for t in range(n):
            for r in range(3):
                cp = _remote(shs[t].half_piece(ins[t], chip_ids[r]), lnd[t].at[r], ssem.at[3 * t + r], rsem.at[3 * t + r], (*chips[r], c))
                cp.wait_send()
        for t in range(n):
            for r in range(3):
                cp = _remote(shs[t].half_piece(ins[t], chip_ids[r]), lnd[t].at[r], ssem.at[3 * t + r], rsem.at[3 * t + r], (*chips[r], c))
                cp.wait_recv()

    out = pl.pallas_call(
        body, name=name, in_specs=[*[_HBM] * (2 * n), _SEM, _SEM, *[_ANY] * len(after)], out_specs=[_HBM] * (2 * n),
        out_shape=[pltpu.HBM(a.shape, a.dtype) for a in (*pws, *lands)], input_output_aliases={t: t for t in range(2 * n)},
        compiler_params=pltpu.CompilerParams(has_side_effects=_EFFECT), **_KW,
    )(*pws, *lands, send_sems, recv_sems, *after)
    return list(out[n:])


def _rs_chip_add(dw32, pair, recv, sh, idx, g_prev, l, L):
    tr, tc = _tile(sh.Rh, 512, 16), _tile(sh.Cs, 2048)
    nr, nc = sh.Rh // tr, sh.Cs // tc

    def body(i_ref, c_ref, d_ref, a_ref, b_ref, *rest):
        rest[-1][...] = ((d_ref[...] + a_ref[...].astype(_F32)) + b_ref[0].astype(_F32) + b_ref[1].astype(_F32)
                         + b_ref[2].astype(_F32))

    if sh.by_cols:
        d_map = lambda a, b, si, sc: (sc[0] * nr + a, si[0] * nc + b)
        a_map = lambda a, b, si, sc: (a, si[0] * nc + b)
    else:
        d_map = lambda a, b, si, sc: ((2 * si[0] + sc[0]) * nr + a, b)
        a_map = lambda a, b, si, sc: (si[0] * nr + a, b)
    in_specs = [BS((tr, tc), d_map), BS((tr, tc), a_map), BS((3, tr, tc), lambda a, b, si, sc: (0, a, b))]
    args = [*idx, dw32, pair, recv]
    if g_prev is not None:
        in_specs.append(_ANY)
        args.append(g_prev)
    return _call(
        body, "rs_chip_add",
        grid_spec=pltpu.PrefetchScalarGridSpec(num_scalar_prefetch=2, grid=(nr, nc), in_specs=in_specs,
                                               out_specs=BS((None, tr, tc), lambda a, b, si, sc: (l, sc[0] * nr + a, b))),
        out_shape=S((L, sh.Rs, sh.Cs), _F32), input_output_aliases={} if g_prev is None else {5: 0},
        compiler_params=_cp("parallel", "parallel"),
    )(*args)


def _rs_pair_share(gs, ls, shs, name):
    n = len(gs)

    def body(*refs):
        bufs = refs[n:2 * n]
        send_sems, recv_sems = refs[2 * n:]
        x, y, c, *_ = _where_am_i()
        sib = (x, y, 1 - c)
        cps = []
        for t in range(n):
            mine = shs[t].shard_half(bufs[t].at[ls[t]], c)
            cps.append(_remote(mine, mine, send_sems.at[t], recv_sems.at[t], sib))
        for cp in cps:
            cp.start()
        for t in range(n):
            other = shs[t].shard_half(bufs[t].at[ls[t]], 1 - c)
            _remote(other, other, send_sems.at[t], recv_sems.at[t], sib).wait_recv()
        for cp in cps:
            cp.wait_send()

    return _call(
        body, name, in_specs=[_ANY] * n, out_specs=[_ANY] * n, out_shape=[S(g.shape, g.dtype) for g in gs],
        input_output_aliases={t: t for t in range(n)},
        scratch_shapes=[pltpu.SemaphoreType.DMA((n,)), pltpu.SemaphoreType.DMA((n,))],
        compiler_params=pltpu.CompilerParams(has_side_effects=True),
    )(*gs)


def _all_reduce_small(xs, dep=None):
    M = xs.shape[0]
    deps = [] if dep is None else [dep]

    def body(x_ref, *rest):
        tot_ref, out_ref, send_sems, recv_sems, local_sem = rest[len(deps):]
        x, y, c, me, chips, chip_ids = _where_am_i()
        sib = (x, y, 1 - c)

        def rows(dev):
            return out_ref.at[pl.ds(_mo((4 * dev[0] + 2 * dev[1] + dev[2]) * M, 8), M), :]

        def copy(k, block, to, src=None):
            return _remote(rows(block) if src is None else src, rows(block), send_sems.at[k], recv_sems.at[k], to)

        mine = pltpu.make_async_copy(x_ref, rows((x, y, c)), local_sem)
        mine.start()
        first = [copy(0, (x, y, c), sib, src=x_ref)]
        first += [copy(1 + j, (x, y, c), (*chip, c), src=x_ref) for j, chip in enumerate(chips)]
        for cp in first:
            cp.start()
        passed = [copy(4 + j, (*chip, c), sib) for j, chip in enumerate(chips)]
        for j, chip in enumerate(chips):
            copy(1 + j, (*chip, c), (x, y, c)).wait_recv()
            passed[j].start()
        copy(0, sib, (x, y, c)).wait_recv()
        for j, chip in enumerate(chips):
            copy(4 + j, (*chip, 1 - c), (x, y, c)).wait_recv()
        for cp in first + passed:
            cp.wait_send()
        mine.wait()
        tot = out_ref[pl.ds(0, M), :]
        for d in range(1, 8):
            tot = tot + out_ref[pl.ds(d * M, M), :]
        tot_ref[...] = tot

    vm = pl.BlockSpec(memory_space=pltpu.VMEM)
    return _call(
        body, "all_reduce_small", in_specs=[vm] + [_ANY] * len(deps), out_specs=[vm, vm],
        out_shape=[S((M, 128), _F32), S((8 * M, 128), _F32)],
        scratch_shapes=[pltpu.SemaphoreType.DMA((7,)), pltpu.SemaphoreType.DMA((7,)), pltpu.SemaphoreType.DMA],
        compiler_params=_cp(has_side_effects=True),
    )(xs, *deps)[0]


def _reduce_scatter_begin(dws, shs, l):
    ssem, rsem, dww, lands, token = _rs_pair_start([d[1] for d in dws], shs, f"rs_pair_start_{l}")
    return ([d[0] for d in dws], dww, lands, ssem, rsem), token


def _reduce_scatter_middle(state, after, shs, idx, l):
    dw32s, dww, lands, ssem, rsem = state
    recv_a = _rs_pair_wait(dww, lands, ssem, rsem, after, shs, f"rs_pair_wait_{l}")
    pws = [_rs_pair_add(d32, ra, sh, idx) for d32, ra, sh in zip(dw32s, recv_a, shs)]
    ssem, rsem, pws, lands, token = _rs_chip_start(pws, shs, f"rs_chip_start_{l}")
    return (dw32s, recv_a, pws, lands, ssem, rsem), token


def _reduce_scatter_end(state, after, tensors, shs, gstack, idx, l):
    dw32s, recv_a, pws, lands, ssem, rsem = state
    recv_b = _rs_chip_wait(pws, lands, ssem, rsem, after, shs, f"rs_chip_wait_{l}")
    gs = [_rs_chip_add(d32, ra, rb, sh, idx, gstack[name], i, L)
          for d32, ra, rb, sh, (name, i, L) in zip(dw32s, recv_a, recv_b, shs, tensors)]
    gs = _rs_pair_share(gs, [i for _, i, _ in tensors], shs, "rs_pair_share")
    for (name, _, _), g in zip(tensors, gs):
        gstack[name] = g


def _pack(parts):
    out = []
    for p in parts:
        p2 = p.reshape(-1, 128)
        pad = (-p2.shape[0]) % 8
        out.append(jnp.pad(p2, ((0, pad), (0, 0))) if pad else p2)
    return jnp.concatenate(out, axis=0)


def _unpack(packed, like):
    out, at = [], 0
    for p in like:
        n = p.size // 128
        out.append(packed[at:at + n].reshape(p.shape))
        at += n + ((-n) % 8)
    return out


def kernel(x, mem, g_mix, g_ffn, w_in_a, g_v_a, w_spatial, b_spatial, w_in_b, g_q_b, g_k_b, g_mem, w_mem_kv, g_mq, g_mk, w_out, w_gate_up, w_down, loss_target, m_g_mix, m_g_ffn, m_w_in_a, m_g_v_a, m_w_spatial, m_b_spatial, m_w_in_b, m_g_q_b, m_g_k_b, m_g_mem, m_w_mem_kv, m_g_mq, m_g_mk, m_w_out, m_w_gate_up, m_w_down, v_g_mix, v_g_ffn, v_w_in_a, v_g_v_a, v_w_spatial, v_b_spatial, v_w_in_b, v_g_q_b, v_g_k_b, v_g_mem, v_w_mem_kv, v_g_mq, v_g_mk, v_w_out, v_w_gate_up, v_w_down):
    xs = x[0]
    mem2 = mem[0]
    target = loss_target[0]
    T, D = xs.shape
    depth = g_mix.shape[0]
    MEMW = w_mem_kv.shape[2] // 2
    TOK = D - MEMW
    KV = (w_in_b.shape[2] * 4 - TOK - MEMW) // 2
    QPK = TOK // KV
    F = w_gate_up.shape[2] * 4 // 2

    idx = ((2 * lax.axis_index("x") + lax.axis_index("y")).astype(jnp.int32).reshape(1), lax.axis_index("c").astype(jnp.int32).reshape(1))

    big = {
        "w_in_a": (w_in_a, _Shard(D, w_in_a.shape[2] * 4, True)),
        "w_in_b": (w_in_b, _Shard(D, w_in_b.shape[2] * 4, True)),
        "w_mem_kv": (w_mem_kv, _Shard(D, 2 * MEMW, False)),
        "w_out": (w_out, _Shard(D, D, False)),
        "w_gate_up": (w_gate_up, _Shard(D, 2 * F, True)),
        "w_down": (w_down, _Shard(F, D, False)),
    }

    def layer_tensors(l):
        n_in = "w_in_a" if l % 2 == 0 else "w_in_b"
        return [(n_in, l // 2, big[n_in][0].shape[0])] + [(n, l, depth) for n in ("w_mem_kv", "w_out", "w_gate_up", "w_down")]

    def layer_shards(l):
        return [big[n][1] for n, _, _ in layer_tensors(l)]

    full = {n: [None] * w.shape[0] for n, (w, _) in big.items()}
    flying = {}

    def start_layer(l, dep):
        tens = layer_tensors(l)
        token = dep
        for gi, group in enumerate([tens[:2], tens[2:3], tens[3:4], tens[4:]]):
            shs = [big[n][1] for n, _, _ in group]
            bufs = [_cast_into_full(big[n][0], i, big[n][1], idx, dep=token) for n, i, _ in group]
            ssem, rsem, bufs, token = _gather_start(bufs, shs, f"gather_start_{l}_{gi}")
            for n, i, _ in group:
                flying[(n, i)] = dict(group=group, shs=shs, state=(ssem, rsem, bufs), name=f"{l}_{gi}", passing=False)
        return token

    def land(members, after):
        for n, i in members:
            fl = flying[(n, i)]
            ssem, rsem, bufs = fl["state"]
            bufs = _gather_wait(bufs, ssem, rsem, after, fl["shs"], "gather_wait_" + fl["name"])
            ssem, rsem, bufs, after = _pass_on_start(bufs, fl["shs"], "pass_on_start_" + fl["name"])
            fl.update(state=(ssem, rsem, bufs), passing=True)
        return after

    def weight(n, i, after):
        if full[n][i] is None:
            fl = flying[(n, i)]
            ssem, rsem, bufs = fl["state"]
            if fl["passing"]:
                bufs = _pass_on_wait(bufs, ssem, rsem, after, fl["shs"], "pass_on_wait_" + fl["name"])
            else:
                bufs = _gather_wait(bufs, ssem, rsem, after, fl["shs"], "gather_wait_" + fl["name"])
                bufs = _gather_pass_on(bufs, fl["shs"], "gather_pass_on")
            for (m, j, _), b in zip(fl["group"], bufs):
                full[m][j] = b
        return full[n][i]

    after = None
    for l in range(depth):
        after = start_layer(l, after)
    tabs = _rope_tables(T)

    saved = []
    xc = xs
    for l in range(depth):
        is_a = l % 2 == 0
        li = l // 2
        w_in = weight("w_in_a" if is_a else "w_in_b", li, after)
        h, ht, r1 = _rmsnorm_fwd(xc, g_mix[l], "rmsnorm_fwd", transposed=True)
        z = _mm_nn("mm_in", h, w_in, pm=2048, pn=512)
        st = dict(x=xc, ht=ht, r1=r1, z=z)
        if is_a:
            ws_m = w_spatial[li].astype(_MXU)
            st["ws_m"], st["wst_m"], st["b_t"] = ws_m, jnp.swapaxes(ws_m, 1, 2), b_spatial[li].T
            tok = _mixer_a_fwd(z, g_v_a[li], ws_m, st["b_t"], TOK, D)
            qblk = 2 * TOK // MEMW
        else:
            q, k, v = _qk_rope_fwd(z, g_q_b[li], g_k_b[li], tabs, TOK, KV)
            tok, stat = _attn_fwd(q, k, v, QPK, D)
            st["q"], st["k"], st["v"], st["stat"] = q, k, v, stat
            qblk = (TOK + 2 * KV) // MEMW
        mem_n = _rmsnorm_fwd(mem2, g_mem[l], "rmsnorm_mem")
        kv = _mm_nn("mm_memkv", mem_n, weight("w_mem_kv", l, z))
        cat = _mem_fwd(z, qblk, kv, g_mq[l], g_mk[l], MEMW, tok)
        token = land([("w_out", l), ("w_gate_up", l), ("w_down", l)], cat) if l > 0 else None
        x1 = _mm_nn("mm_out", cat, weight("w_out", l, cat), add=xc, dep=token)
        h2, h2t, r2 = _rmsnorm_fwd(x1, g_ffn[l], "rmsnorm_fwd", transposed=True)
        st["r2"] = r2
        act, gu = _ffn_gate_up(h2, weight("w_gate_up", l, h2))
        w_down_l = weight("w_down", l, act)
        token = land([layer_tensors(l + 1)[0][:2]], act) if l + 1 < depth else None
        xc = _mm_nn("mm_down", act, w_down_l, add=x1, pm=512, pn=1024, pk=8192, dep=token)
        after = xc
        st.update(mem_n=mem_n, kv=kv, qblk=qblk, cat=cat, x1=x1, h2t=h2t, act=act, gu=gu)
        saved.append(st)

    dx, dxm, sq = _loss_head(xc, target)
    loss = lax.psum(sq[0, 0] * (0.5 / D), ("x", "y", "c"))

    gsm = {n: [None] * len(a) for n, a in dict(g_mix=g_mix, g_ffn=g_ffn, g_v_a=g_v_a, w_spatial=w_spatial, b_spatial=b_spatial,
                                                g_q_b=g_q_b, g_k_b=g_k_b, g_mem=g_mem, g_mq=g_mq, g_mk=g_mk).items()}
    gstack = {n: None for n in big}
    pairing, chipping, token = None, None, None

    def advance(after):
        nonlocal pairing, chipping
        state, tok = _reduce_scatter_middle(pairing[0], after, layer_shards(pairing[1]), idx, pairing[1])
        if chipping is not None:
            _reduce_scatter_end(chipping[0], tok, layer_tensors(chipping[1]), layer_shards(chipping[1]), gstack, idx, chipping[1])
        pairing, chipping = None, (state, pairing[1])
        return tok

    for l in reversed(range(depth)):
        st = saved[l]
        is_a = l % 2 == 0
        li = l // 2
        gbig = {}
        dgu = _ffn_dact(dxm, full["w_down"][l], st["gu"], dep=token)
        token = advance(dgu) if pairing is not None else None
        gbig["w_down"] = _mm_tn_dual("mm_dw_down", st["act"], dxm, pm=512, pn=1024, pk=4096)
        dh2 = _ffn_dh(dgu, full["w_gate_up"][l], dep=token)
        gbig["w_gate_up"] = _ffn_dwgu(st["h2t"], dgu)
        dx, dxm, dg = _rmsnorm_bwd(st["x1"], g_ffn[l], dh2, dx, "rmsnorm_bwd", rstd=st["r2"])
        gsm["g_ffn"][l] = dg[0]
        dcat = _mm_nt("mm_dcat", dxm, full["w_out"][l])
        gbig["w_out"] = _mm_tn_dual("mm_dw_out", st["cat"], dxm, pm=512, pn=1024, pk=4096)
        if is_a:
            dz, dws, dbs, dgv = _mixer_a_bwd(st["z"], dcat, g_v_a[li], st["ws_m"], st["wst_m"], st["b_t"], TOK)
            gsm["w_spatial"][li], gsm["b_spatial"][li], gsm["g_v_a"][li] = dws, dbs[:, :, 0], dgv[0]
        else:
            dq, dk, dz = _attn_bwd(st["q"], st["k"], st["v"], dcat, st["cat"], st["stat"], QPK, st["z"].shape[1])
            dz, dgq_b, dgk_b = _qk_rope_bwd(st["z"], dq, dk, g_q_b[li], g_k_b[li], tabs, TOK, KV, dz)
            gsm["g_q_b"][li], gsm["g_k_b"][li] = dgq_b[0], dgk_b[0]
        dz, dkn, dvm, dgq = _mem_bwd(st["z"], st["qblk"], st["kv"], g_mq[l], g_mk[l], dcat, TOK // MEMW, MEMW, dz)
        dkv, dgk = _memkv_bwd(st["kv"], dkn, dvm, g_mk[l], MEMW)
        gsm["g_mq"][l], gsm["g_mk"][l] = dgq[0], dgk[0]
        gbig["w_mem_kv"] = _mm_tn_dual("mm_dw_memkv", st["mem_n"], dkv)
        dmem_n = _mm_nt("mm_dmemn", dkv, full["w_mem_kv"][l])
        gsm["g_mem"][l] = _rmsnorm_bwd(mem2, g_mem[l], dmem_n, None, "rmsnorm_bwd_mem")[2][0]
        n_in = "w_in_a" if is_a else "w_in_b"
        dh = _mm_nt("mm_dh", dz, full[n_in][li])
        gbig[n_in] = _mm_nn_dual("mm_dw_in", st["ht"], dz, pm=1024, pn=512, pk=4096)
        dx, dxm, dg = _rmsnorm_bwd(st["x"], g_mix[l], dh, dx, "rmsnorm_bwd", rstd=st["r1"])
        gsm["g_mix"][l] = dg[0]
        state, token = _reduce_scatter_begin([gbig[n] for n, _, _ in layer_tensors(l)], layer_shards(l), l)
        pairing = (state, l)

    small = ["g_mix", "g_ffn", "g_v_a", "w_spatial", "b_spatial", "g_q_b", "g_k_b", "g_mem", "g_mq", "g_mk"]
    env = dict(g_mix=g_mix, g_ffn=g_ffn, g_v_a=g_v_a, w_spatial=w_spatial, b_spatial=b_spatial, g_q_b=g_q_b, g_k_b=g_k_b,
               g_mem=g_mem, g_mq=g_mq, g_mk=g_mk,
               m_g_mix=m_g_mix, m_g_ffn=m_g_ffn, m_g_v_a=m_g_v_a, m_w_spatial=m_w_spatial, m_b_spatial=m_b_spatial,
               m_g_q_b=m_g_q_b, m_g_k_b=m_g_k_b, m_g_mem=m_g_mem, m_g_mq=m_g_mq, m_g_mk=m_g_mk,
               v_g_mix=v_g_mix, v_g_ffn=v_g_ffn, v_g_v_a=v_g_v_a, v_w_spatial=v_w_spatial, v_b_spatial=v_b_spatial,
               v_g_q_b=v_g_q_b, v_g_k_b=v_g_k_b, v_g_mem=v_g_mem, v_g_mq=v_g_mq, v_g_mk=v_g_mk,
               m_w_in_a=m_w_in_a, m_w_in_b=m_w_in_b, m_w_mem_kv=m_w_mem_kv, m_w_out=m_w_out, m_w_gate_up=m_w_gate_up, m_w_down=m_w_down,
               v_w_in_a=v_w_in_a, v_w_in_b=v_w_in_b, v_w_mem_kv=v_w_mem_kv, v_w_out=v_w_out, v_w_gate_up=v_w_gate_up, v_w_down=v_w_down)
    like = [env[n] for n in small]
    g_small = _all_reduce_small(_pack([jnp.stack(gsm[n]) for n in small]), dep=token)

    res = {}
    outs = _adamw(_pack(like)[None], g_small[None], _pack([env["m_" + n] for n in small])[None],
                  _pack([env["v_" + n] for n in small])[None], "adamw_small")
    unpacked = [_unpack(o[0], like) for o in outs]
    for k, n in enumerate(small):
        res[n] = [u[k] for u in unpacked]
    advance(outs[1])
    pending = chipping
    last = {n: i for n, i, _ in layer_tensors(pending[1])}
    early = {}
    for n, (w, _) in big.items():
        L = w.shape[0]
        if n not in last:
            res[n] = _adamw(w, gstack[n], env["m_" + n], env["v_" + n], "adamw_" + n)
        elif L > 1:
            assert last[n] == 0
            early[n] = _adamw(w, gstack[n], env["m_" + n], env["v_" + n], "adamw_early_" + n, l0=1)
    done = [o[1] for o in early.values()] + [res[n][1] for n in big if n in res] + [res[small[0]][1]]
    _reduce_scatter_end(pending[0], done, layer_tensors(pending[1]), layer_shards(pending[1]), gstack, idx, pending[1])
    for n in last:
        res[n] = _adamw(big[n][0], gstack[n], env["m_" + n], env["v_" + n], "adamw_last_" + n, l0=0, l1=1, prev=early.get(n))

    order = ["g_mix", "g_ffn", "w_in_a", "g_v_a", "w_spatial", "b_spatial", "w_in_b", "g_q_b", "g_k_b", "g_mem", "w_mem_kv",
             "g_mq", "g_mk", "w_out", "w_gate_up", "w_down"]
    return (loss, dx.reshape(1, T, D), *[res[n][0] for n in order], *[res[n][1] for n in order],
            *[res[n][2] for n in order], *[res[n][3] for n in order])
```

```python
import jax
import jax.numpy as jnp
import numpy as np
from jax import lax
from jax.experimental import pallas as pl
from jax.experimental.pallas import tpu as pltpu

_F32 = jnp.float32
_MXU = jnp.bfloat16
_WIRE = jnp.bfloat16
_KW = {}

EPS = 1e-6
HEAD = 128
CHUNK = 128
GRID_W = 64
ROPE_THETA = 10000.0
ADAM_LR, ADAM_B1, ADAM_B2, ADAM_EPS, ADAM_WD, ADAM_STEP = 0.001, 0.9, 0.999, 1e-08, 0.01, 10
_SQRT_HALF = float(np.sqrt(0.5))
_INV_SQRT_2PI = float(1.0 / np.sqrt(2.0 * np.pi))
_VMEM_LIMIT = 56 * 1024 * 1024
_MESH = pl.DeviceIdType.MESH

_NN = (((1,), (0,)), ((), ()))
_NT = (((1,), (1,)), ((), ()))
_TN = (((0,), (0,)), ((), ()))

S = jax.ShapeDtypeStruct
BS = pl.BlockSpec
_ANY = pl.BlockSpec(memory_space=pl.ANY)


def _tile(n, pref, mult=128):
    if n <= pref:
        return n
    d = (pref // mult) * mult
    while d >= mult:
        if n % d == 0:
            return d
        d -= mult
    raise ValueError(f"no tile for {n} (pref {pref}, mult {mult})")


def _mo(v, m):
    return v if isinstance(v, int) else pl.multiple_of(v, m)


def _cp(*sem, **kw):
    return pltpu.CompilerParams(dimension_semantics=sem or None, vmem_limit_bytes=_VMEM_LIMIT, **kw)


def _call(body, name, **kw):
    return pl.pallas_call(body, name=name, **kw, **_KW)


def _dot(a, b, dn=_NN):
    return lax.dot_general(a, b, dn, preferred_element_type=_F32)


def _gelu(x):
    return 0.5 * x * (1.0 + lax.erf(x * _SQRT_HALF))


def _gelu_grad(x):
    return 0.5 * (1.0 + lax.erf(x * _SQRT_HALF)) + x * jnp.exp(-0.5 * x * x) * _INV_SQRT_2PI


def _rstd(x):
    return lax.rsqrt(jnp.mean(x * x, axis=-1, keepdims=True) + EPS)


def _norm_bwd(dout, xhat, r, g):
    dy = dout * g
    return r * (dy - xhat * jnp.mean(dy * xhat, axis=-1, keepdims=True))


def _softmax(s):
    e = jnp.exp(s - jnp.max(s, axis=-1, keepdims=True))
    return e * (1.0 / jnp.sum(e, axis=-1, keepdims=True))


def _mm(name, a, b, a_spec, b_spec, dn, grid, acc_shape, out_shape, out_specs, epilogue, extra=(), extra_specs=(), dep=None):
    nk = grid[2]
    n_ex = len(extra)
    deps = [] if dep is None else [dep]
    multi = isinstance(out_shape, (list, tuple))
    n_out = len(out_shape) if multi else 1

    def body(*refs):
        a_ref, b_ref = refs[0], refs[1]
        ex = refs[2:2 + n_ex]
        outs = refs[2 + n_ex + len(deps):2 + n_ex + len(deps) + n_out]

        def prod():
            return _dot(a_ref[...].astype(_MXU), b_ref[...].astype(_MXU), dn)

        if nk == 1:
            epilogue(prod(), ex, outs)
        else:
            acc = refs[-1]
            k = pl.program_id(2)

            @pl.when(k == 0)
            def _():
                acc[...] = jnp.zeros_like(acc)

            acc[...] += prod()

            @pl.when(k == nk - 1)
            def _():
                epilogue(acc[...], ex, outs)

    return _call(
        body, name, grid=grid, in_specs=[a_spec, b_spec, *extra_specs] + [_ANY] * len(deps), out_specs=out_specs, out_shape=out_shape,
        scratch_shapes=[] if nk == 1 else [pltpu.VMEM(acc_shape, _F32)],
        compiler_params=_cp("parallel", "parallel", "arbitrary"),
    )(a, b, *extra, *deps)


def _ep_store(acc, ex, outs):
    for o in outs:
        o[...] = acc.astype(o.dtype)


def _ep_add(acc, ex, outs):
    outs[0][...] = (acc + ex[0][...]).astype(outs[0].dtype)


def _mm_nn(name, a, b, out_dtype=_F32, add=None, pm=1024, pn=1024, pk=2048, dep=None):
    M, K = a.shape
    N = b.shape[1]
    tm, tn, tk = _tile(M, pm, 8), _tile(N, pn), _tile(K, pk)
    o_spec = BS((tm, tn), lambda i, j, k: (i, j))
    return _mm(name, a, b, BS((tm, tk), lambda i, j, k: (i, k)), BS((tk, tn), lambda i, j, k: (k, j)), _NN,
               (M // tm, N // tn, K // tk), (tm, tn), S((M, N), out_dtype), o_spec,
               _ep_store if add is None else _ep_add,
               extra=() if add is None else (add,), extra_specs=() if add is None else (o_spec,), dep=dep)


def _mm_nt(name, a, b, out_dtype=_F32, pm=1024, pn=1024, pk=4096):
    M, K = a.shape
    N = b.shape[0]
    tm, tn, tk = _tile(M, pm, 8), _tile(N, pn), _tile(K, pk)
    return _mm(name, a, b, BS((tm, tk), lambda i, j, k: (i, k)), BS((tn, tk), lambda i, j, k: (j, k)), _NT,
               (M // tm, N // tn, K // tk), (tm, tn), S((M, N), out_dtype), BS((tm, tn), lambda i, j, k: (i, j)), _ep_store)


def _mm_tn_dual(name, a, b, pm=1024, pn=1024, pk=2048, dep=None):
    K, M = a.shape
    N = b.shape[1]
    tm, tn, tk = _tile(M, pm), _tile(N, pn), _tile(K, pk, 16)
    o_spec = BS((tm, tn), lambda i, j, k: (i, j))
    return _mm(name, a, b, BS((tk, tm), lambda i, j, k: (k, i)), BS((tk, tn), lambda i, j, k: (k, j)), _TN,
               (M // tm, N // tn, K // tk), (tm, tn), [S((M, N), _F32), S((M, N), _WIRE)], [o_spec, o_spec], _ep_store, dep=dep)


def _ffn_gate_up(h2, wgu):
    T, D = h2.shape
    F = wgu.shape[1] // 2
    tm, tn = _tile(T, 1024, 8), _tile(F, 512)
    nj = F // tn

    def body(a_ref, bg_ref, bu_ref, act_ref, gu_ref):
        a = a_ref[...]
        g = _dot(a, bg_ref[...])
        u = _dot(a, bu_ref[...])
        sg = 1.0 / (1.0 + jnp.exp(-g))
        silu = g * sg
        gu_ref[0] = (u * (sg * (1.0 + g * (1.0 - sg)))).astype(gu_ref.dtype)
        gu_ref[1] = silu.astype(gu_ref.dtype)
        act_ref[...] = (silu * u).astype(act_ref.dtype)

    return _call(
        body, "ffn_gate_up", grid=(T // tm, nj),
        in_specs=[BS((tm, D), lambda i, j: (i, 0)), BS((D, tn), lambda i, j: (0, j)), BS((D, tn), lambda i, j: (0, j + nj))],
        out_specs=[BS((tm, tn), lambda i, j: (i, j)), BS((2, tm, tn), lambda i, j: (0, i, j))],
        out_shape=[S((T, F), _MXU), S((2, T, F), _MXU)],
        compiler_params=_cp("parallel", "parallel"),
    )(h2, wgu, wgu)


def _ffn_dact(dxm, wdown, gu, dep=None):
    T, D = dxm.shape
    F = wdown.shape[0]
    tm, tn = _tile(T, 2048, 8), _tile(F, 512)
    deps = [] if dep is None else [dep]

    def body(a_ref, b_ref, gu_ref, *rest):
        o_ref = rest[-1]
        d = _dot(a_ref[...], b_ref[...], _NT)
        o_ref[0] = (d * gu_ref[0].astype(_F32)).astype(o_ref.dtype)
        o_ref[1] = (d * gu_ref[1].astype(_F32)).astype(o_ref.dtype)

    return _call(
        body, "ffn_dact", grid=(T // tm, F // tn),
        in_specs=[BS((tm, D), lambda i, j: (i, 0)), BS((tn, D), lambda i, j: (j, 0)), BS((2, tm, tn), lambda i, j: (0, i, j))]
        + [_ANY] * len(deps),
        out_specs=BS((2, tm, tn), lambda i, j: (0, i, j)), out_shape=S((2, T, F), _MXU),
        compiler_params=_cp("parallel", "parallel"),
    )(dxm, wdown, gu, *deps)


def _ffn_dh(dgu, wgu, dep=None):
    _, T, F = dgu.shape
    D = wgu.shape[0]
    tm, tn, tk = _tile(T, 512, 8), _tile(D, 2048), _tile(F, 2816, 256)
    nkf = F // tk
    return _mm("ffn_dh", dgu, wgu, BS((None, tm, tk), lambda i, j, k: (k // nkf, i, k % nkf)),
               BS((tn, tk), lambda i, j, k: (j, k)), _NT, (T // tm, D // tn, 2 * nkf), (tm, tn),
               S((T, D), _F32), BS((tm, tn), lambda i, j, k: (i, j)), _ep_store, dep=dep)


def _ffn_dwgu(h2t, dgu):
    _, T, F = dgu.shape
    D = h2t.shape[0]
    tm, tn, tk = _tile(D, 1024), _tile(F, 512), _tile(T, 4096)
    njf = F // tn
    o_spec = BS((tm, tn), lambda i, j, k: (i, j))
    return _mm("ffn_dwgu", h2t, dgu, BS((tm, tk), lambda i, j, k: (i, k)),
               BS((None, tk, tn), lambda i, j, k: (j // njf, k, j % njf)), _NN, (D // tm, 2 * njf, T // tk), (tm, tn),
               [S((D, 2 * F), _F32), S((D, 2 * F), _WIRE)], [o_spec, o_spec], _ep_store)


def _mm_nn_dual(name, a, b, pm=1024, pn=1024, pk=2048):
    M, K = a.shape
    N = b.shape[1]
    tm, tn, tk = _tile(M, pm), _tile(N, pn), _tile(K, pk)
    o_spec = BS((tm, tn), lambda i, j, k: (i, j))
    return _mm(name, a, b, BS((tm, tk), lambda i, j, k: (i, k)), BS((tk, tn), lambda i, j, k: (k, j)), _NN,
               (M // tm, N // tn, K // tk), (tm, tn), [S((M, N), _F32), S((M, N), _WIRE)], [o_spec, o_spec], _ep_store)


def _rmsnorm_fwd(x, g, name, dep=None, transposed=False):
    T, D = x.shape
    tr = _tile(T, 512, 128)
    n_out = 3 if transposed else 1

    def body(x_ref, g_ref, *rest):
        outs = rest[-n_out:]
        xv = x_ref[...]
        r = _rstd(xv)
        h = (xv * r * g_ref[...]).astype(outs[0].dtype)
        outs[0][...] = h
        if transposed:
            outs[1][...] = h.T
            outs[2][...] = r

    row = BS((tr, D), lambda i: (i, 0))
    deps = [] if dep is None else [dep]
    return _call(body, name, grid=(T // tr,), in_specs=[row, BS((1, D), lambda i: (0, 0))] + [_ANY] * len(deps),
                 out_specs=[row, BS((D, tr), lambda i: (0, i)), BS((tr, 1), lambda i: (i, 0))] if transposed else row,
                 out_shape=[S((T, D), _MXU), S((D, T), _MXU), S((T, 1), _F32)] if transposed else S((T, D), _MXU),
                 compiler_params=_cp("parallel"))(x, g.reshape(1, D), *deps)


def _rmsnorm_bwd(x, g, dh, dres, name, rstd=None):
    T, D = x.shape
    tr = _tile(T, 512, 8)
    has_res = dres is not None
    has_r = rstd is not None

    def body(*refs):
        x_ref, g_ref, dh_ref = refs[:3]
        dx_ref, dxm_ref, dg_ref = refs[-3:]

        @pl.when(pl.program_id(0) == 0)
        def _():
            dg_ref[...] = jnp.zeros_like(dg_ref)

        xv = x_ref[...]
        r = refs[3 + has_res][...] if has_r else _rstd(xv)
        xhat = xv * r
        dh_v = dh_ref[...]
        dg_ref[...] += jnp.sum(dh_v * xhat, axis=0, keepdims=True)
        dx = _norm_bwd(dh_v, xhat, r, g_ref[...])
        if has_res:
            dx = dx + refs[3][...]
        dx_ref[...] = dx
        dxm_ref[...] = dx.astype(dxm_ref.dtype)

    row = BS((tr, D), lambda i: (i, 0))
    vec = BS((1, D), lambda i: (0, 0))
    extra = ([dres] if has_res else []) + ([rstd] if has_r else [])
    return _call(body, name, grid=(T // tr,),
                 in_specs=[row, vec, row] + ([row] if has_res else []) + ([BS((tr, 1), lambda i: (i, 0))] if has_r else []),
                 out_specs=[row, row, vec], out_shape=[S((T, D), _F32), S((T, D), _MXU), S((1, D), _F32)],
                 compiler_params=_cp("arbitrary"))(x, g.reshape(1, D), dh, *extra)


def _loss_head(y, target):
    T, D = y.shape
    tr = _tile(T, 512, 8)

    def body(y_ref, t_ref, dy_ref, dym_ref, acc_ref):
        @pl.when(pl.program_id(0) == 0)
        def _():
            acc_ref[...] = jnp.zeros_like(acc_ref)

        err = y_ref[...] - t_ref[...]
        acc_ref[...] += jnp.sum(jnp.sum(err * err, axis=-1, keepdims=True), axis=0, keepdims=True)
        dy = err * (1.0 / D)
        dy_ref[...] = dy
        dym_ref[...] = dy.astype(dym_ref.dtype)

    row = BS((tr, D), lambda i: (i, 0))
    return _call(body, "loss_head", grid=(T // tr,), in_specs=[row, row],
                 out_specs=[row, row, BS((1, 128), lambda i: (0, 0))],
                 out_shape=[S((T, D), _F32), S((T, D), _MXU), S((1, 128), _F32)],
                 compiler_params=_cp("arbitrary"))(y, target)


def _mixa_blocks(T):
    return next(cb for cb in (4, 2, 1) if T % (cb * CHUNK) == 0)


def _mixer_a_fwd(z, gv, ws_m, b_t, TOK, width):
    T = z.shape[0]
    G = TOK // HEAD
    CB = _mixa_blocks(T)
    R = CB * CHUNK

    def body(z_ref, gv_ref, ws_ref, bt_ref, o_ref):
        u = _gelu(z_ref[:, :TOK])
        v = _gelu(z_ref[:, TOK:])
        vn = (v * _rstd(v) * gv_ref[...]).astype(_MXU)
        for c in range(CB):
            rows = slice(c * CHUNK, (c + 1) * CHUNK)
            for g in range(G):
                cols = slice(g * HEAD, (g + 1) * HEAD)
                s = _dot(ws_ref[g], vn[rows, cols]) + bt_ref[:, g:g + 1]
                o_ref[rows, cols] = (u[rows, cols] * s).astype(o_ref.dtype)

    return _call(
        body, "mixer_a_fwd", grid=(T // R,),
        in_specs=[BS((R, 2 * TOK), lambda i: (i, 0)), BS((1, TOK), lambda i: (0, 0)),
                  BS((G, CHUNK, CHUNK), lambda i: (0, 0, 0)), BS((CHUNK, G), lambda i: (0, 0))],
        out_specs=BS((R, TOK), lambda i: (i, 0)), out_shape=S((T, width), _MXU), compiler_params=_cp("parallel"),
    )(z, gv.reshape(1, TOK), ws_m, b_t)


def _mixer_a_bwd(z, dcat, gv, ws_m, wst_m, b_t, TOK):
    T = z.shape[0]
    G = TOK // HEAD
    CB = _mixa_blocks(T)
    R = CB * CHUNK
    n = T // R

    def body(z_ref, d_ref, gv_ref, ws_ref, wst_ref, bt_ref, dz_ref, dws_ref, db_ref, dgv_ref, dvn_scr):
        i = pl.program_id(0)

        @pl.when(i == 0)
        def _():
            dws_ref[...] = jnp.zeros_like(dws_ref)
            db_ref[...] = jnp.zeros_like(db_ref)
            dgv_ref[...] = jnp.zeros_like(dgv_ref)

        zu = z_ref[:, :TOK]
        zv = z_ref[:, TOK:]
        u = _gelu(zu)
        v = _gelu(zv)
        r = _rstd(v)
        vhat = v * r
        gvv = gv_ref[...]
        vn = (vhat * gvv).astype(_MXU)
        d = d_ref[...]
        gpu = _gelu_grad(zu)
        for c in range(CB):
            rows = slice(c * CHUNK, (c + 1) * CHUNK)
            for g in range(G):
                cols = slice(g * HEAD, (g + 1) * HEAD)
                vn_cg = vn[rows, cols]
                s = _dot(ws_ref[g], vn_cg) + bt_ref[:, g:g + 1]
                d_cg = d[rows, cols]
                dz_ref[rows, cols] = (d_cg * s * gpu[rows, cols]).astype(dz_ref.dtype)
                ds = d_cg * u[rows, cols]
                ds_m = ds.astype(_MXU)
                dvn_scr[rows, cols] = _dot(wst_ref[g], ds_m)
                dws_ref[g] += _dot(ds_m, vn_cg, _NT)
                db_ref[g] += ds
        dvn = dvn_scr[...]
        dgv_ref[...] += jnp.sum(dvn * vhat, axis=0, keepdims=True)
        dv = _norm_bwd(dvn, vhat, r, gvv)
        dz_ref[:, TOK:] = (dv * _gelu_grad(zv)).astype(dz_ref.dtype)

        @pl.when(i == n - 1)
        def _():
            for g in range(G):
                db_ref[g] = jnp.broadcast_to(jnp.sum(db_ref[g], axis=1, keepdims=True), (CHUNK, CHUNK))

    full3 = BS((G, CHUNK, CHUNK), lambda i: (0, 0, 0))
    return _call(
        body, "mixer_a_bwd", grid=(n,),
        in_specs=[BS((R, 2 * TOK), lambda i: (i, 0)), BS((R, TOK), lambda i: (i, 0)), BS((1, TOK), lambda i: (0, 0)),
                  full3, full3, BS((CHUNK, G), lambda i: (0, 0))],
        out_specs=[BS((R, 2 * TOK), lambda i: (i, 0)), full3, full3, BS((1, TOK), lambda i: (0, 0))],
        out_shape=[S((T, z.shape[1]), _MXU), S((G, CHUNK, CHUNK), _F32), S((G, CHUNK, CHUNK), _F32), S((1, TOK), _F32)],
        scratch_shapes=[pltpu.VMEM((R, TOK), _F32)], compiler_params=_cp("arbitrary"),
    )(z, dcat, gv.reshape(1, TOK), ws_m, wst_m, b_t)


def _rope_tables(T):
    n_rows = T // GRID_W
    rows = jnp.broadcast_to(jnp.arange(n_rows)[:, None], (n_rows, GRID_W)).reshape(T)
    cols = jnp.broadcast_to(jnp.arange(GRID_W)[None, :], (n_rows, GRID_W)).reshape(T)
    pairs = HEAD // 4
    freqs = ROPE_THETA ** (-jnp.arange(pairs, dtype=_F32) / pairs)
    ang_r = rows.astype(_F32)[:, None] * freqs
    ang_c = cols.astype(_F32)[:, None] * freqs
    ang = jnp.concatenate([ang_r, ang_r, ang_c, ang_c], axis=-1)
    cos, sin = jnp.cos(ang), jnp.sin(ang)
    first = (jnp.arange(HEAD) % (HEAD // 2)) < (HEAD // 4)
    return cos, jnp.where(first, -sin, 0.0), jnp.where(first, 0.0, sin)


def _rope(x, cs, sa, sb):
    return x * cs + pltpu.roll(x, 96, 1) * sa + pltpu.roll(x, 32, 1) * sb


def _qk_rope_fwd(z, gq, gk, tabs, TOK, KV):
    T = z.shape[0]
    R = _tile(T, 512, 8)
    W = TOK + 2 * KV

    def body(z_ref, gq_ref, gk_ref, cos_ref, sa_ref, sb_ref, q_ref, k_ref, v_ref):
        cs, sa, sb = cos_ref[...], sa_ref[...], sb_ref[...]
        for h in range((TOK + KV) // HEAD):
            cols = slice(h * HEAD, (h + 1) * HEAD)
            xv = z_ref[:, cols]
            xn = xv * _rstd(xv) * (gq_ref[...] if h < TOK // HEAD else gk_ref[...])
            out = _rope(xn, cs, sa, sb)
            if h < TOK // HEAD:
                q_ref[:, cols] = out.astype(q_ref.dtype)
            else:
                k_ref[:, h * HEAD - TOK:(h + 1) * HEAD - TOK] = out.astype(k_ref.dtype)
        v_ref[...] = z_ref[:, TOK + KV:].astype(v_ref.dtype)

    vec = BS((1, HEAD), lambda i: (0, 0))
    tab = BS((R, HEAD), lambda i: (i, 0))
    return _call(
        body, "qk_rope_fwd", grid=(T // R,), in_specs=[BS((R, W), lambda i: (i, 0)), vec, vec, tab, tab, tab],
        out_specs=[BS((R, TOK), lambda i: (i, 0)), BS((R, KV), lambda i: (i, 0)), BS((R, KV), lambda i: (i, 0))],
        out_shape=[S((T, TOK), _MXU), S((T, KV), _MXU), S((T, KV), _MXU)], compiler_params=_cp("parallel"),
    )(z, gq.reshape(1, HEAD), gk.reshape(1, HEAD), *tabs)


def _qk_rope_bwd(z, dq, dk, gq, gk, tabs, TOK, KV, into):
    T = z.shape[0]
    R = _tile(T, 512, 8)
    W = TOK + KV

    def body(z_ref, dq_ref, dk_ref, gq_ref, gk_ref, cos_ref, sa_ref, sb_ref, into_ref, dz_ref, dgq_ref, dgk_ref):
        @pl.when(pl.program_id(0) == 0)
        def _():
            dgq_ref[...] = jnp.zeros_like(dgq_ref)
            dgk_ref[...] = jnp.zeros_like(dgk_ref)

        cs, sa, sb = cos_ref[...], sa_ref[...], sb_ref[...]
        for h in range(W // HEAD):
            cols = slice(h * HEAD, (h + 1) * HEAD)
            is_q = h < TOK // HEAD
            do = dq_ref[:, cols] if is_q else dk_ref[:, h * HEAD - TOK:(h + 1) * HEAD - TOK]
            dxn = do * cs - pltpu.roll(do, 96, 1) * sa - pltpu.roll(do, 32, 1) * sb
            xv = z_ref[:, cols]
            r = _rstd(xv)
            xhat = xv * r
            dg_ref = dgq_ref if is_q else dgk_ref
            dg_ref[...] += jnp.sum(dxn * xhat, axis=0, keepdims=True)
            dz_ref[:, cols] = _norm_bwd(dxn, xhat, r, gq_ref[...] if is_q else gk_ref[...]).astype(dz_ref.dtype)

    vec = BS((1, HEAD), lambda i: (0, 0))
    tab = BS((R, HEAD), lambda i: (i, 0))
    return _call(
        body, "qk_rope_bwd", grid=(T // R,),
        in_specs=[BS((R, W), lambda i: (i, 0)), BS((R, TOK), lambda i: (i, 0)), BS((R, KV), lambda i: (i, 0)), vec, vec, tab, tab, tab,
                  _ANY],
        out_specs=[BS((R, W), lambda i: (i, 0)), vec, vec],
        out_shape=[S(into.shape, into.dtype), S((1, HEAD), _F32), S((1, HEAD), _F32)], input_output_aliases={8: 0},
        compiler_params=_cp("arbitrary"),
    )(z, dq, dk, gq.reshape(1, HEAD), gk.reshape(1, HEAD), *tabs, into)


_ATTN_C2 = float(HEAD ** -0.5 * np.log2(np.e))


def _attn_fwd(q, k, v, QPK, width):
    T, TOK = q.shape
    KVH = k.shape[1] // HEAD
    tq = _tile(T, 1024, 8)
    ts = 256 if tq % 256 == 0 else tq
    W = QPK * HEAD

    def body(q_ref, k_ref, v_ref, o_ref, st_ref, vaug):
        @pl.when(pl.program_id(1) == 0)
        def _():
            vaug[:, :HEAD] = v_ref[...]
            vaug[:, HEAD:] = jnp.ones((T, HEAD), vaug.dtype)

        kk, va = k_ref[...], vaug[...]
        for g in range(QPK):
            cols = slice(g * HEAD, (g + 1) * HEAD)
            for r in range(tq // ts):
                rows = slice(r * ts, (r + 1) * ts)
                s = _dot(q_ref[rows, cols], kk, _NT)
                m = jnp.max(s, axis=-1, keepdims=True)
                ov = _dot(jnp.exp2((s - m) * _ATTN_C2).astype(_MXU), va)
                l = ov[:, HEAD:HEAD + 1]
                o_ref[rows, cols] = (ov[:, :HEAD] * (1.0 / l)).astype(o_ref.dtype)
                st_ref[rows, g:g + 1] = m + jnp.log2(l) * (1.0 / _ATTN_C2)

    qs = BS((tq, W), lambda h, i: (i, h))
    ks = BS((T, HEAD), lambda h, i: (0, h))
    return _call(body, "attn_fwd", grid=(KVH, T // tq), in_specs=[qs, ks, ks],
                 out_specs=[qs, BS((None, tq, QPK), lambda h, i: (h, i, 0))],
                 out_shape=[S((T, width), _MXU), S((KVH, T, QPK), _F32)],
                 scratch_shapes=[pltpu.VMEM((T, 2 * HEAD), _MXU)],
                 compiler_params=_cp("parallel", "arbitrary"))(q, k, v)


def _attn_bwd(q, k, v, dcat, o, stat, QPK, width):
    T, TOK = q.shape
    KV = k.shape[1]
    KVH = KV // HEAD
    tq = _tile(T, 512, 8)
    ts = 256 if tq % 256 == 0 else tq
    nq = T // tq
    W = QPK * HEAD
    scale = HEAD ** -0.5

    def body(q_ref, k_ref, v_ref, do_ref, o_ref, st_ref, dq_ref, dk_ref, dv_ref, dk_acc, dv_acc, ds_all, p_all, q_all, do_all):
        i = pl.program_id(1)

        @pl.when(i == 0)
        def _():
            dk_acc[...] = jnp.zeros_like(dk_acc)
            dv_acc[...] = jnp.zeros_like(dv_acc)

        kk, vv = k_ref[...], v_ref[...]
        for r in range(tq // ts):
            rows = slice(r * ts, (r + 1) * ts)
            for g in range(QPK):
                cols = slice(g * HEAD, (g + 1) * HEAD)
                stack = slice(g * ts, (g + 1) * ts)
                qg = q_ref[rows, cols]
                p = jnp.exp2((_dot(qg, kk, _NT) - st_ref[rows, g:g + 1]) * _ATTN_C2)
                do32 = do_ref[rows, cols]
                do = do32.astype(_MXU)
                delta = jnp.sum(do32 * o_ref[rows, cols].astype(_F32), axis=-1, keepdims=True)
                ds = (p * (_dot(do, vv, _NT) - delta)).astype(_MXU)
                dq_ref[rows, cols] = _dot(ds, kk) * scale
                ds_all[stack, :] = ds
                p_all[stack, :] = p.astype(_MXU)
                q_all[stack, :] = qg
                do_all[stack, :] = do
            dk_acc[...] += _dot(ds_all[...], q_all[...], _TN)
            dv_acc[...] += _dot(p_all[...], do_all[...], _TN)

        @pl.when(i == nq - 1)
        def _():
            dk_ref[...] = dk_acc[...] * scale
            dv_ref[...] = dv_acc[...].astype(dv_ref.dtype)

    qs = BS((tq, W), lambda h, i: (i, h))
    ks = BS((T, HEAD), lambda h, i: (0, h))
    return _call(
        body, "attn_bwd", grid=(KVH, nq), in_specs=[qs, ks, ks, qs, qs, BS((None, tq, QPK), lambda h, i: (h, i, 0))],
        out_specs=[qs, ks, BS((T, HEAD), lambda h, i: (0, (TOK + KV) // HEAD + h))],
        out_shape=[S((T, TOK), _F32), S((T, KV), _F32), S((T, width), _MXU)],
        scratch_shapes=[pltpu.VMEM((T, HEAD), _F32), pltpu.VMEM((T, HEAD), _F32), pltpu.VMEM((QPK * ts, T), _MXU),
                        pltpu.VMEM((QPK * ts, T), _MXU), pltpu.VMEM((QPK * ts, HEAD), _MXU), pltpu.VMEM((QPK * ts, HEAD), _MXU)],
        compiler_params=_cp("parallel", "arbitrary"),
    )(q, k, v, dcat, o, stat)


def _mem_fwd(z, qblk, kv, gmq, gmk, MEMW, into):
    T = z.shape[0]
    NM = kv.shape[0]
    tq = _tile(T, 512, 8)
    scale = HEAD ** -0.5
    oblk = into.shape[1] // MEMW - 1

    def body(q_ref, kv_ref, gq_ref, gk_ref, into_ref, o_ref):
        for h in range(MEMW // HEAD):
            cols = slice(h * HEAD, (h + 1) * HEAD)
            kx = kv_ref[:, cols]
            kn = (kx * _rstd(kx) * gk_ref[...]).astype(_MXU)
            vv = kv_ref[:, MEMW + h * HEAD:MEMW + (h + 1) * HEAD].astype(_MXU)
            qx = q_ref[:, cols]
            qn = (qx * _rstd(qx) * gq_ref[...]).astype(_MXU)
            p = _softmax(_dot(qn, kn, _NT) * scale)
            o_ref[:, cols] = _dot(p.astype(_MXU), vv).astype(o_ref.dtype)

    vec = BS((1, HEAD), lambda i: (0, 0))
    return _call(
        body, "mem_fwd", grid=(T // tq,),
        in_specs=[BS((tq, MEMW), lambda i: (i, qblk)), BS((NM, 2 * MEMW), lambda i: (0, 0)), vec, vec, _ANY],
        out_specs=BS((tq, MEMW), lambda i: (i, oblk)), out_shape=S(into.shape, into.dtype), input_output_aliases={4: 0},
        compiler_params=_cp("parallel"),
    )(z, kv, gmq.reshape(1, HEAD), gmk.reshape(1, HEAD), into)


def _mem_bwd(z, qblk, kv, gmq, gmk, dcat, dblk, MEMW, into):
    T = z.shape[0]
    NM = kv.shape[0]
    tq = _tile(T, 512, 8)
    scale = HEAD ** -0.5

    def body(q_ref, kv_ref, gq_ref, gk_ref, do_ref, into_ref, dz_ref, dkn_ref, dv_ref, dgq_ref):
        @pl.when(pl.program_id(0) == 0)
        def _():
            dkn_ref[...] = jnp.zeros_like(dkn_ref)
            dv_ref[...] = jnp.zeros_like(dv_ref)
            dgq_ref[...] = jnp.zeros_like(dgq_ref)

        for h in range(MEMW // HEAD):
            cols = slice(h * HEAD, (h + 1) * HEAD)
            kx = kv_ref[:, cols]
            kn = (kx * _rstd(kx) * gk_ref[...]).astype(_MXU)
            vv = kv_ref[:, MEMW + h * HEAD:MEMW + (h + 1) * HEAD].astype(_MXU)
            qx = q_ref[:, cols]
            rq = _rstd(qx)
            qhat = qx * rq
            qn = (qhat * gq_ref[...]).astype(_MXU)
            p = _softmax(_dot(qn, kn, _NT) * scale)
            do = do_ref[:, cols].astype(_MXU)
            dp = _dot(do, vv, _NT)
            ds = (p * (dp - jnp.sum(p * dp, axis=-1, keepdims=True)) * scale).astype(_MXU)
            dqn = _dot(ds, kn)
            dkn_ref[:, cols] += _dot(ds, qn, _TN)
            dv_ref[:, cols] += _dot(p.astype(_MXU), do, _TN)
            dgq_ref[...] += jnp.sum(dqn * qhat, axis=0, keepdims=True)
            dz_ref[:, cols] = _norm_bwd(dqn, qhat, rq, gq_ref[...]).astype(dz_ref.dtype)

    vec = BS((1, HEAD), lambda i: (0, 0))
    kvs = BS((NM, MEMW), lambda i: (0, 0))
    return _call(
        body, "mem_bwd", grid=(T // tq,),
        in_specs=[BS((tq, MEMW), lambda i: (i, qblk)), BS((NM, 2 * MEMW), lambda i: (0, 0)), vec, vec,
                  BS((tq, MEMW), lambda i: (i, dblk)), _ANY],
        out_specs=[BS((tq, MEMW), lambda i: (i, qblk)), kvs, kvs, vec],
        out_shape=[S(into.shape, into.dtype), S((NM, MEMW), _F32), S((NM, MEMW), _F32), S((1, HEAD), _F32)],
        input_output_aliases={5: 0}, compiler_params=_cp("arbitrary"),
    )(z, kv, gmq.reshape(1, HEAD), gmk.reshape(1, HEAD), dcat, into)


def _memkv_bwd(kv, dkn, dv, gmk, MEMW):
    NM = kv.shape[0]

    def body(kv_ref, dkn_ref, dv_ref, gk_ref, dkv_ref, dgk_ref):
        dgk = jnp.zeros((1, HEAD), _F32)
        for h in range(MEMW // HEAD):
            cols = slice(h * HEAD, (h + 1) * HEAD)
            kx = kv_ref[:, cols]
            r = _rstd(kx)
            khat = kx * r
            dkn = dkn_ref[:, cols]
            dgk = dgk + jnp.sum(dkn * khat, axis=0, keepdims=True)
            dkv_ref[:, cols] = _norm_bwd(dkn, khat, r, gk_ref[...]).astype(dkv_ref.dtype)
        dgk_ref[...] = dgk
        dkv_ref[:, MEMW:] = dv_ref[...].astype(dkv_ref.dtype)

    return _call(body, "memkv_bwd", out_shape=[S((NM, 2 * MEMW), _MXU), S((1, HEAD), _F32)],
                 compiler_params=_cp())(kv, dkn, dv, gmk.reshape(1, HEAD))


def _cast_into_full(w, l, sh, idx, dep=None):
    tr, tc = _tile(sh.Rs, 512, 16), _tile(sh.Cs, 2048)
    nr, nc = sh.Rs // tr, sh.Cs // tc
    deps = [] if dep is None else [dep]

    def body(i_ref, c_ref, w_ref, *rest):
        rest[-1][...] = w_ref[...].astype(rest[-1].dtype)

    if sh.by_cols:
        o_map = lambda a, b, si, sc: (a, si[0] * nc + b)
    else:
        o_map = lambda a, b, si, sc: (si[0] * nr + a, b)
    return _call(
        body, "cast_into_full",
        grid_spec=pltpu.PrefetchScalarGridSpec(
            num_scalar_prefetch=2, grid=(nr, nc),
            in_specs=[BS((None, tr, tc), lambda a, b, si, sc: (l, a, b))] + [_ANY] * len(deps), out_specs=BS((tr, tc), o_map)),
        out_shape=S((sh.R, sh.C), _WIRE), compiler_params=_cp("parallel", "parallel"),
    )(*idx, w, *deps)


def _adamw(w, g, m, v, name, l0=0, l1=None, prev=None):
    L, R, C = w.shape
    l1 = L if l1 is None else l1
    tc = _tile(C, 2048)
    tr = _tile(R, max(8, (512 * 1024) // tc), 8)
    c_m = 1.0 / (1.0 - ADAM_B1 ** ADAM_STEP)
    c_v = 1.0 / (1.0 - ADAM_B2 ** ADAM_STEP)

    def body(w_ref, g_ref, m_ref, v_ref, *rest):
        go_ref, d_ref, mo_ref, vo_ref = rest[-4:]
        gv = g_ref[...]
        mn = ADAM_B1 * m_ref[...] + (1.0 - ADAM_B1) * gv
        vn = ADAM_B2 * v_ref[...] + (1.0 - ADAM_B2) * (gv * gv)
        go_ref[...] = gv
        mo_ref[...] = mn
        vo_ref[...] = vn
        d_ref[...] = -ADAM_LR * ((mn * c_m) / (jnp.sqrt(vn * c_v) + ADAM_EPS) + ADAM_WD * w_ref[...])

    blk = BS((None, tr, tc), lambda a, i, j: (l0 + a, i, j))
    prevs = [] if prev is None else list(prev)
    return _call(body, name, grid=(l1 - l0, R // tr, C // tc), in_specs=[blk] * 4 + [_ANY] * len(prevs), out_specs=[blk] * 4,
                 out_shape=[S((L, R, C), _F32)] * 4, input_output_aliases={4 + k: k for k in range(len(prevs))},
                 compiler_params=_cp("parallel", "parallel", "parallel"))(w, g, m, v, *prevs)


def _where_am_i():
    x, y, c = lax.axis_index("x"), lax.axis_index("y"), lax.axis_index("c")
    chips = [(1 - x, y), (x, 1 - y), (1 - x, 1 - y)]
    return x, y, c, 2 * x + y, chips, [2 * cx + cy for cx, cy in chips]


class _Shard:
    def __init__(self, R, C, by_cols):
        self.R, self.C, self.by_cols = R, C, by_cols
        self.Rs, self.Cs = (R, C // 4) if by_cols else (R // 4, C)
        self.Rh = self.Rs // 2
        self.Q = R // 8

    def full_piece(self, ref, j, cc):
        if self.by_cols:
            return ref.at[pl.ds(cc * self.Rh, self.Rh), pl.ds(_mo(j * self.Cs, 128), self.Cs)]
        return ref.at[pl.ds(_mo(j * self.Rs + cc * self.Rh, 16), self.Rh), :]

    def full_shard(self, ref, j):
        if self.by_cols:
            return ref.at[:, pl.ds(_mo(j * self.Cs, 128), self.Cs)]
        return ref.at[pl.ds(_mo(j * self.Rs, 16), self.Rs), :]

    def shard_half(self, ref, cc):
        return ref.at[pl.ds(_mo(cc * self.Rh, 16), self.Rh), :]

    def half_piece(self, ref, j):
        if self.by_cols:
            return ref.at[:, pl.ds(_mo(j * self.Cs, 128), self.Cs)]
        return ref.at[pl.ds(_mo(j * self.Rh, 16), self.Rh), :]


def _remote(src, dst, ssem, rsem, dev):
    return pltpu.make_async_remote_copy(src_ref=src, dst_ref=dst, send_sem=ssem, recv_sem=rsem, device_id=dev, device_id_type=_MESH)


_HBM = pl.BlockSpec(memory_space=pltpu.HBM)
_SEM = pl.BlockSpec(memory_space=pltpu.SEMAPHORE)
_EFFECT = pltpu.SideEffectType.DATAFLOW_SIDE_EFFECTING


def _in_hbm(a):
    return pltpu.with_memory_space_constraint(a, pltpu.HBM)


def _gather_start(fulls, shs, name):
    n = len(fulls)

    def body(*refs):
        bufs = refs[:n]
        send_sems, recv_sems = refs[n], refs[n + 1]
        token = refs[-1]
        x, y, c, me, chips, chip_ids = _where_am_i()
        for t in range(n):
            mine = shs[t].full_piece(bufs[t], me, c)
            for r in range(3):
                _remote(mine, mine, send_sems.at[3 * t + r], recv_sems.at[3 * t + r], (*chips[r], c)).start()
        token[...] = jnp.zeros_like(token)

    out = pl.pallas_call(
        body, name=name, in_specs=[_HBM] * n,
        out_shape=(pltpu.SemaphoreType.DMA((3 * n,)), pltpu.SemaphoreType.DMA((3 * n,)), *[pltpu.HBM(f.shape, f.dtype) for f in fulls],
                   S((8, 128), _F32)),
        out_specs=(_SEM, _SEM, *[_HBM] * n, pl.BlockSpec(memory_space=pltpu.VMEM)),
        input_output_aliases={t: 2 + t for t in range(n)},
        compiler_params=pltpu.CompilerParams(has_side_effects=_EFFECT), **_KW,
    )(*[_in_hbm(f) for f in fulls])
    return out[0], out[1], list(out[2:2 + n]), out[-1]


def _gather_wait(fulls, send_sems, recv_sems, after, shs, name):
    n = len(fulls)

    def body(*refs):
        bufs = refs[:n]
        ssem, rsem = refs[n], refs[n + 1]
        x, y, c, me, chips, chip_ids = _where_am_i()
        for t in range(n):
            mine = shs[t].full_piece(bufs[t], me, c)
            for r in range(3):
                _remote(mine, mine, ssem.at[3 * t + r], rsem.at[3 * t + r], (*chips[r], c)).wait_send()
        for t in range(n):
            for r in range(3):
                piece = shs[t].full_piece(bufs[t], chip_ids[r], c)
                _remote(piece, piece, ssem.at[3 * t + r], rsem.at[3 * t + r], (*chips[r], c)).wait_recv()

    out = pl.pallas_call(
        body, name=name, in_specs=[*[_HBM] * n, _SEM, _SEM, _ANY], out_specs=[_HBM] * n,
        out_shape=[pltpu.HBM(f.shape, f.dtype) for f in fulls], input_output_aliases={t: t for t in range(n)},
        compiler_params=pltpu.CompilerParams(has_side_effects=_EFFECT), **_KW,
    )(*fulls, send_sems, recv_sems, after)
    return list(out)


def _gather_pass_on(fulls, shs, name):
    n = len(fulls)

    def body(*refs):
        bufs = refs[n:2 * n]
        send_sems, recv_sems = refs[2 * n:]
        x, y, c, me, chips, chip_ids = _where_am_i()
        sib = (x, y, 1 - c)
        cps = []
        for t in range(n):
            for r in range(3):
                piece = shs[t].full_piece(bufs[t], chip_ids[r], c)
                cps.append(_remote(piece, piece, send_sems.at[t, r], recv_sems.at[t, r], sib))
        for cp in cps:
            cp.start()
        for t in range(n):
            for r in range(3):
                piece = shs[t].full_piece(bufs[t], chip_ids[r], 1 - c)
                _remote(piece, piece, send_sems.at[t, r], recv_sems.at[t, r], sib).wait_recv()
        for cp in cps:
            cp.wait_send()

    return _call(
        body, name, in_specs=[_ANY] * n, out_specs=[_ANY] * n, out_shape=[S(f.shape, f.dtype) for f in fulls],
        input_output_aliases={t: t for t in range(n)},
        scratch_shapes=[pltpu.SemaphoreType.DMA((n, 3)), pltpu.SemaphoreType.DMA((n, 3))],
        compiler_params=pltpu.CompilerParams(has_side_effects=True),
    )(*fulls)


def _pass_on_copies(bufs, shs, send_sems, recv_sems):
    x, y, c, me, chips, chip_ids = _where_am_i()
    sib = (x, y, 1 - c)
    out, back = [], []
    for t in range(len(bufs)):
        for r in range(3):
            piece = shs[t].full_piece(bufs[t], chip_ids[r], c)
            out.append(_remote(piece, piece, send_sems.at[3 * t + r], recv_sems.at[3 * t + r], sib))
            other = shs[t].full_piece(bufs[t], chip_ids[r], 1 - c)
            back.append(_remote(other, other, send_sems.at[3 * t + r], recv_sems.at[3 * t + r], sib))
    return out, back


def _pass_on_start(fulls, shs, name):
    n = len(fulls)

    def body(*refs):
        for cp in _pass_on_copies(refs[:n], shs, refs[n], refs[n + 1])[0]:
            cp.start()
        refs[-1][...] = jnp.zeros_like(refs[-1])

    out = pl.pallas_call(
        body, name=name, in_specs=[_HBM] * n,
        out_shape=(pltpu.SemaphoreType.DMA((3 * n,)), pltpu.SemaphoreType.DMA((3 * n,)), *[pltpu.HBM(f.shape, f.dtype) for f in fulls],
                   S((8, 128), _F32)),
        out_specs=(_SEM, _SEM, *[_HBM] * n, pl.BlockSpec(memory_space=pltpu.VMEM)),
        input_output_aliases={t: 2 + t for t in range(n)},
        compiler_params=pltpu.CompilerParams(has_side_effects=_EFFECT), **_KW,
    )(*[_in_hbm(f) for f in fulls])
    return out[0], out[1], list(out[2:2 + n]), out[-1]


def _pass_on_wait(fulls, send_sems, recv_sems, after, shs, name):
    n = len(fulls)

    def body(*refs):
        out, back = _pass_on_copies(refs[:n], shs, refs[n], refs[n + 1])
        for cp in out:
            cp.wait_send()
        for cp in back:
            cp.wait_recv()

    out = pl.pallas_call(
        body, name=name, in_specs=[*[_HBM] * n, _SEM, _SEM, _ANY], out_specs=[_HBM] * n,
        out_shape=[pltpu.HBM(f.shape, f.dtype) for f in fulls], input_output_aliases={t: t for t in range(n)},
        compiler_params=pltpu.CompilerParams(has_side_effects=_EFFECT), **_KW,
    )(*fulls, send_sems, recv_sems, after)
    return list(out)


def _rs_pair_copies(ins, outs, shs, send_sems, recv_sems):
    x, y, c, *_ = _where_am_i()
    sib = (x, y, 1 - c)
    cps = []
    for t in range(len(ins)):
        sh = shs[t]
        if sh.by_cols:
            cps.append(_remote(ins[t].at[pl.ds((1 - c) * sh.Rh, sh.Rh), :], outs[t], send_sems.at[4 * t], recv_sems.at[4 * t], sib))
        else:
            for j in range(4):
                cps.append(_remote(sh.full_piece(ins[t], j, 1 - c), sh.half_piece(outs[t], j),
                                   send_sems.at[4 * t + j], recv_sems.at[4 * t + j], sib))
    return cps


def _rs_pair_start(dws, shs, name):
    n = len(dws)
    lands = [lax.empty((sh.R // 2, sh.C), _WIRE) for sh in shs]

    def body(*refs):
        for cp in _rs_pair_copies(refs[:n], refs[n:2 * n], shs, refs[2 * n], refs[2 * n + 1]):
            cp.start()
        refs[-1][...] = jnp.zeros_like(refs[-1])

    out = pl.pallas_call(
        body, name=name, in_specs=[_HBM] * (2 * n),
        out_shape=(pltpu.SemaphoreType.DMA((4 * n,)), pltpu.SemaphoreType.DMA((4 * n,)),
                   *[pltpu.HBM(a.shape, a.dtype) for a in (*dws, *lands)], S((8, 128), _F32)),
        out_specs=(_SEM, _SEM, *[_HBM] * (2 * n), pl.BlockSpec(memory_space=pltpu.VMEM)),
        input_output_aliases={t: 2 + t for t in range(2 * n)},
        compiler_params=pltpu.CompilerParams(has_side_effects=_EFFECT), **_KW,
    )(*[_in_hbm(a) for a in (*dws, *lands)])
    return out[0], out[1], list(out[2:2 + n]), list(out[2 + n:2 + 2 * n]), out[-1]


def _rs_pair_wait(dws, lands, send_sems, recv_sems, after, shs, name):
    n = len(dws)

    def body(*refs):
        cps = _rs_pair_copies(refs[:n], refs[n:2 * n], shs, refs[2 * n], refs[2 * n + 1])
        for cp in cps:
            cp.wait_send()
        for cp in cps:
            cp.wait_recv()

    out = pl.pallas_call(
        body, name=name, in_specs=[*[_HBM] * (2 * n), _SEM, _SEM, _ANY], out_specs=[_HBM] * (2 * n),
        out_shape=[pltpu.HBM(a.shape, a.dtype) for a in (*dws, *lands)], input_output_aliases={t: t for t in range(2 * n)},
        compiler_params=pltpu.CompilerParams(has_side_effects=_EFFECT), **_KW,
    )(*dws, *lands, send_sems, recv_sems, after)
    return list(out[n:])


def _rs_pair_add(dw32, recv, sh, idx):
    tr, tc = _tile(sh.Q, 512, 16), _tile(sh.C, 2048)
    nb = sh.Q // tr

    def body(i_ref, c_ref, a_ref, b_ref, ow_ref):
        ow_ref[...] = (a_ref[...] + b_ref[...].astype(_F32)).astype(ow_ref.dtype)

    if sh.by_cols:
        a_map = lambda j, i, b, si, sc: (sc[0] * 4 * nb + j * nb + i, b)
    else:
        a_map = lambda j, i, b, si, sc: (j * 2 * nb + sc[0] * nb + i, b)
    h_spec = BS((tr, tc), lambda j, i, b, si, sc: (j * nb + i, b))
    return _call(
        body, "rs_pair_add",
        grid_spec=pltpu.PrefetchScalarGridSpec(num_scalar_prefetch=2, grid=(4, nb, sh.C // tc),
                                               in_specs=[BS((tr, tc), a_map), h_spec], out_specs=h_spec),
        out_shape=S((sh.R // 2, sh.C), _WIRE), compiler_params=_cp("parallel", "parallel", "parallel"),
    )(*idx, dw32, recv)


def _rs_chip_start(pws, shs, name):
    n = len(pws)
    lands = [lax.empty((3, sh.Rh, sh.Cs), _WIRE) for sh in shs]

    def body(*refs):
        ins, lnd = refs[:n], refs[n:2 * n]
        send_sems, recv_sems = refs[2 * n], refs[2 * n + 1]
        token = refs[-1]
        x, y, c, me, chips, chip_ids = _where_am_i()
        for t in range(n):
            for r in range(3):
                _remote(shs[t].half_piece(ins[t], chip_ids[r]), lnd[t].at[r], send_sems.at[3 * t + r], recv_sems.at[3 * t + r],
                        (*chips[r], c)).start()
        token[...] = jnp.zeros_like(token)

    out = pl.pallas_call(
        body, name=name, in_specs=[_HBM] * (2 * n),
        out_shape=(pltpu.SemaphoreType.DMA((3 * n,)), pltpu.SemaphoreType.DMA((3 * n,)),
                   *[pltpu.HBM(a.shape, a.dtype) for a in (*pws, *lands)], S((8, 128), _F32)),
        out_specs=(_SEM, _SEM, *[_HBM] * (2 * n), pl.BlockSpec(memory_space=pltpu.VMEM)),
        input_output_aliases={t: 2 + t for t in range(2 * n)},
        compiler_params=pltpu.CompilerParams(has_side_effects=_EFFECT), **_KW,
    )(*[_in_hbm(a) for a in (*pws, *lands)])
    return out[0], out[1], list(out[2:2 + n]), list(out[2 + n:2 + 2 * n]), out[-1]


def _rs_chip_wait(pws, lands, send_sems, recv_sems, after, shs, name):
    n = len(pws)
    after = list(after) if isinstance(after, (list, tuple)) else [after]

    def body(*refs):
        ins, lnd = refs[:n], refs[n:2 * n]
        ssem, rsem = refs[2 * n], refs[2 * n + 1]
        x, y, c, me, chips, chip_ids = _where_am_i()
        for t in range(n):
            for r in range(3):
                cp = _remote(shs[t].half_piece(ins[t], chip_ids[r]), lnd[t].at[r], ssem.at[3 * t + r], rsem.at[3 * t + r], (*chips[r], c))
                cp.wait_send()
        for t in range(n):
            for r in range(3):
                cp = _remote(shs[t].half_piece(ins[t], chip_ids[r]), lnd[t].at[r], ssem.at[3 * t + r], rsem.at[3 * t + r], (*chips[r], c))
                cp.wait_recv()

    out = pl.pallas_call(
        body, name=name, in_specs=[*[_HBM] * (2 * n), _SEM, _SEM, *[_ANY] * len(after)], out_specs=[_HBM] * (2 * n),
        out_shape=[pltpu.HBM(a.shape, a.dtype) for a in (*pws, *lands)], input_output_aliases={t: t for t in range(2 * n)},
        compiler_params=pltpu.CompilerParams(has_side_effects=_EFFECT), **_KW,
    )(*pws, *lands, send_sems, recv_sems, *after)
    return list(out[n:])


def _rs_chip_add(dw32, pair, recv, sh, idx, g_prev, l, L):
    tr, tc = _tile(sh.Rh, 512, 16), _tile(sh.Cs, 2048)
    nr, nc = sh.Rh // tr, sh.Cs // tc

    def body(i_ref, c_ref, d_ref, a_ref, b_ref, *rest):
        rest[-1][...] = ((d_ref[...] + a_ref[...].astype(_F32)) + b_ref[0].astype(_F32) + b_ref[1].astype(_F32)
                         + b_ref[2].astype(_F32))

    if sh.by_cols:
        d_map = lambda a, b, si, sc: (sc[0] * nr + a, si[0] * nc + b)
        a_map = lambda a, b, si, sc: (a, si[0] * nc + b)
    else:
        d_map = lambda a, b, si, sc: ((2 * si[0] + sc[0]) * nr + a, b)
        a_map = lambda a, b, si, sc: (si[0] * nr + a, b)
    in_specs = [BS((tr, tc), d_map), BS((tr, tc), a_map), BS((3, tr, tc), lambda a, b, si, sc: (0, a, b))]
    args = [*idx, dw32, pair, recv]
    if g_prev is not None:
        in_specs.append(_ANY)
        args.append(g_prev)
    return _call(
        body, "rs_chip_add",
        grid_spec=pltpu.PrefetchScalarGridSpec(num_scalar_prefetch=2, grid=(nr, nc), in_specs=in_specs,
                                               out_specs=BS((None, tr, tc), lambda a, b, si, sc: (l, sc[0] * nr + a, b))),
        out_shape=S((L, sh.Rs, sh.Cs), _F32), input_output_aliases={} if g_prev is None else {5: 0},
        compiler_params=_cp("parallel", "parallel"),
    )(*args)


def _rs_pair_share(gs, ls, shs, name):
    n = len(gs)

    def body(*refs):
        bufs = refs[n:2 * n]
        send_sems, recv_sems = refs[2 * n:]
        x, y, c, *_ = _where_am_i()
        sib = (x, y, 1 - c)
        cps = []
        for t in range(n):
            mine = shs[t].shard_half(bufs[t].at[ls[t]], c)
            cps.append(_remote(mine, mine, send_sems.at[t], recv_sems.at[t], sib))
        for cp in cps:
            cp.start()
        for t in range(n):
            other = shs[t].shard_half(bufs[t].at[ls[t]], 1 - c)
            _remote(other, other, send_sems.at[t], recv_sems.at[t], sib).wait_recv()
        for cp in cps:
            cp.wait_send()

    return _call(
        body, name, in_specs=[_ANY] * n, out_specs=[_ANY] * n, out_shape=[S(g.shape, g.dtype) for g in gs],
        input_output_aliases={t: t for t in range(n)},
        scratch_shapes=[pltpu.SemaphoreType.DMA((n,)), pltpu.SemaphoreType.DMA((n,))],
        compiler_params=pltpu.CompilerParams(has_side_effects=True),
    )(*gs)


def _all_reduce_small(xs, dep=None):
    M = xs.shape[0]
    deps = [] if dep is None else [dep]

    def body(x_ref, *rest):
        tot_ref, out_ref, send_sems, recv_sems, local_sem = rest[len(deps):]
        x, y, c, me, chips, chip_ids = _where_am_i()
        sib = (x, y, 1 - c)

        def rows(dev):
            return out_ref.at[pl.ds(_mo((4 * dev[0] + 2 * dev[1] + dev[2]) * M, 8), M), :]

        def copy(k, block, to, src=None):
            return _remote(rows(block) if src is None else src, rows(block), send_sems.at[k], recv_sems.at[k], to)

        mine = pltpu.make_async_copy(x_ref, rows((x, y, c)), local_sem)
        mine.start()
        first = [copy(0, (x, y, c), sib, src=x_ref)]
        first += [copy(1 + j, (x, y, c), (*chip, c), src=x_ref) for j, chip in enumerate(chips)]
        for cp in first:
            cp.start()
        passed = [copy(4 + j, (*chip, c), sib) for j, chip in enumerate(chips)]
        for j, chip in enumerate(chips):
            copy(1 + j, (*chip, c), (x, y, c)).wait_recv()
            passed[j].start()
        copy(0, sib, (x, y, c)).wait_recv()
        for j, chip in enumerate(chips):
            copy(4 + j, (*chip, 1 - c), (x, y, c)).wait_recv()
        for cp in first + passed:
            cp.wait_send()
        mine.wait()
        tot = out_ref[pl.ds(0, M), :]
        for d in range(1, 8):
            tot = tot + out_ref[pl.ds(d * M, M), :]
        tot_ref[...] = tot

    vm = pl.BlockSpec(memory_space=pltpu.VMEM)
    return _call(
        body, "all_reduce_small", in_specs=[vm] + [_ANY] * len(deps), out_specs=[vm, vm],
        out_shape=[S((M, 128), _F32), S((8 * M, 128), _F32)],
        scratch_shapes=[pltpu.SemaphoreType.DMA((7,)), pltpu.SemaphoreType.DMA((7,)), pltpu.SemaphoreType.DMA],
        compiler_params=_cp(has_side_effects=True),
    )(xs, *deps)[0]


def _reduce_scatter_begin(dws, shs, l):
    ssem, rsem, dww, lands, token = _rs_pair_start([d[1] for d in dws], shs, f"rs_pair_start_{l}")
    return ([d[0] for d in dws], dww, lands, ssem, rsem), token


def _reduce_scatter_middle(state, after, shs, idx, l):
    dw32s, dww, lands, ssem, rsem = state
    recv_a = _rs_pair_wait(dww, lands, ssem, rsem, after, shs, f"rs_pair_wait_{l}")
    pws = [_rs_pair_add(d32, ra, sh, idx) for d32, ra, sh in zip(dw32s, recv_a, shs)]
    ssem, rsem, pws, lands, token = _rs_chip_start(pws, shs, f"rs_chip_start_{l}")
    return (dw32s, recv_a, pws, lands, ssem, rsem), token


def _reduce_scatter_end(state, after, tensors, shs, gstack, idx, l):
    dw32s, recv_a, pws, lands, ssem, rsem = state
    recv_b = _rs_chip_wait(pws, lands, ssem, rsem, after, shs, f"rs_chip_wait_{l}")
    gs = [_rs_chip_add(d32, ra, rb, sh, idx, gstack[name], i, L)
          for d32, ra, rb, sh, (name, i, L) in zip(dw32s, recv_a, recv_b, shs, tensors)]
    gs = _rs_pair_share(gs, [i for _, i, _ in tensors], shs, "rs_pair_share")
    for (name, _, _), g in zip(tensors, gs):
        gstack[name] = g


def _pack(parts):
    out = []
    for p in parts:
        p2 = p.reshape(-1, 128)
        pad = (-p2.shape[0]) % 8
        out.append(jnp.pad(p2, ((0, pad), (0, 0))) if pad else p2)
    return jnp.concatenate(out, axis=0)


def _unpack(packed, like):
    out, at = [], 0
    for p in like:
        n = p.size // 128
        out.append(packed[at:at + n].reshape(p.shape))
        at += n + ((-n) % 8)
    return out


def kernel(x, mem, g_mix, g_ffn, w_in_a, g_v_a, w_spatial, b_spatial, w_in_b, g_q_b, g_k_b, g_mem, w_mem_kv, g_mq, g_mk, w_out, w_gate_up, w_down, loss_target, m_g_mix, m_g_ffn, m_w_in_a, m_g_v_a, m_w_spatial, m_b_spatial, m_w_in_b, m_g_q_b, m_g_k_b, m_g_mem, m_w_mem_kv, m_g_mq, m_g_mk, m_w_out, m_w_gate_up, m_w_down, v_g_mix, v_g_ffn, v_w_in_a, v_g_v_a, v_w_spatial, v_b_spatial, v_w_in_b, v_g_q_b, v_g_k_b, v_g_mem, v_w_mem_kv, v_g_mq, v_g_mk, v_w_out, v_w_gate_up, v_w_down):
    xs = x[0]
    mem2 = mem[0]
    target = loss_target[0]
    T, D = xs.shape
    depth = g_mix.shape[0]
    MEMW = w_mem_kv.shape[2] // 2
    TOK = D - MEMW
    KV = (w_in_b.shape[2] * 4 - TOK - MEMW) // 2
    QPK = TOK // KV
    F = w_gate_up.shape[2] * 4 // 2

    idx = ((2 * lax.axis_index("x") + lax.axis_index("y")).astype(jnp.int32).reshape(1), lax.axis_index("c").astype(jnp.int32).reshape(1))

    big = {
        "w_in_a": (w_in_a, _Shard(D, w_in_a.shape[2] * 4, True)),
        "w_in_b": (w_in_b, _Shard(D, w_in_b.shape[2] * 4, True)),
        "w_mem_kv": (w_mem_kv, _Shard(D, 2 * MEMW, False)),
        "w_out": (w_out, _Shard(D, D, False)),
        "w_gate_up": (w_gate_up, _Shard(D, 2 * F, True)),
        "w_down": (w_down, _Shard(F, D, False)),
    }

    def layer_tensors(l):
        n_in = "w_in_a" if l % 2 == 0 else "w_in_b"
        return [(n_in, l // 2, big[n_in][0].shape[0])] + [(n, l, depth) for n in ("w_mem_kv", "w_out", "w_gate_up", "w_down")]

    def layer_shards(l):
        return [big[n][1] for n, _, _ in layer_tensors(l)]

    full = {n: [None] * w.shape[0] for n, (w, _) in big.items()}
    flying = {}

    def start_layer(l, dep):
        tens = layer_tensors(l)
        token = dep
        for gi, group in enumerate([tens[:2], tens[2:3], tens[3:4], tens[4:]]):
            shs = [big[n][1] for n, _, _ in group]
            bufs = [_cast_into_full(big[n][0], i, big[n][1], idx, dep=token) for n, i, _ in group]
            ssem, rsem, bufs, token = _gather_start(bufs, shs, f"gather_start_{l}_{gi}")
            for n, i, _ in group:
                flying[(n, i)] = dict(group=group, shs=shs, state=(ssem, rsem, bufs), name=f"{l}_{gi}", passing=False)
        return token

    def land(members, after):
        for n, i in members:
            fl = flying[(n, i)]
            ssem, rsem, bufs = fl["state"]
            bufs = _gather_wait(bufs, ssem, rsem, after, fl["shs"], "gather_wait_" + fl["name"])
            ssem, rsem, bufs, after = _pass_on_start(bufs, fl["shs"], "pass_on_start_" + fl["name"])
            fl.update(state=(ssem, rsem, bufs), passing=True)
        return after

    def weight(n, i, after):
        if full[n][i] is None:
            fl = flying[(n, i)]
            ssem, rsem, bufs = fl["state"]
            if fl["passing"]:
                bufs = _pass_on_wait(bufs, ssem, rsem, after, fl["shs"], "pass_on_wait_" + fl["name"])
            else:
                bufs = _gather_wait(bufs, ssem, rsem, after, fl["shs"], "gather_wait_" + fl["name"])
                bufs = _gather_pass_on(bufs, fl["shs"], "gather_pass_on")
            for (m, j, _), b in zip(fl["group"], bufs):
                full[m][j] = b
        return full[n][i]

    after = None
    for l in range(depth):
        after = start_layer(l, after)
    tabs = _rope_tables(T)

    saved = []
    xc = xs
    for l in range(depth):
        is_a = l % 2 == 0
        li = l // 2
        w_in = weight("w_in_a" if is_a else "w_in_b", li, after)
        h, ht, r1 = _rmsnorm_fwd(xc, g_mix[l], "rmsnorm_fwd", transposed=True)
        z = _mm_nn("mm_in", h, w_in, pm=2048, pn=512)
        st = dict(x=xc, ht=ht, r1=r1, z=z)
        if is_a:
            ws_m = w_spatial[li].astype(_MXU)
            st["ws_m"], st["wst_m"], st["b_t"] = ws_m, jnp.swapaxes(ws_m, 1, 2), b_spatial[li].T
            tok = _mixer_a_fwd(z, g_v_a[li], ws_m, st["b_t"], TOK, D)
            qblk = 2 * TOK // MEMW
        else:
            q, k, v = _qk_rope_fwd(z, g_q_b[li], g_k_b[li], tabs, TOK, KV)
            tok, stat = _attn_fwd(q, k, v, QPK, D)
            st["q"], st["k"], st["v"], st["stat"] = q, k, v, stat
            qblk = (TOK + 2 * KV) // MEMW
        mem_n = _rmsnorm_fwd(mem2, g_mem[l], "rmsnorm_mem")
        kv = _mm_nn("mm_memkv", mem_n, weight("w_mem_kv", l, z))
        cat = _mem_fwd(z, qblk, kv, g_mq[l], g_mk[l], MEMW, tok)
        token = land([("w_out", l), ("w_gate_up", l), ("w_down", l)], cat) if l > 0 else None
        x1 = _mm_nn("mm_out", cat, weight("w_out", l, cat), add=xc, dep=token)
        h2, h2t, r2 = _rmsnorm_fwd(x1, g_ffn[l], "rmsnorm_fwd", transposed=True)
        st["r2"] = r2
        act, gu = _ffn_gate_up(h2, weight("w_gate_up", l, h2))
        w_down_l = weight("w_down", l, act)
        token = land([layer_tensors(l + 1)[0][:2]], act) if l + 1 < depth else None
        xc = _mm_nn("mm_down", act, w_down_l, add=x1, pm=512, pn=1024, pk=8192, dep=token)
        after = xc
        st.update(mem_n=mem_n, kv=kv, qblk=qblk, cat=cat, x1=x1, h2t=h2t, act=act, gu=gu)
        saved.append(st)

    dx, dxm, sq = _loss_head(xc, target)
    loss = lax.psum(sq[0, 0] * (0.5 / D), ("x", "y", "c"))

    gsm = {n: [None] * len(a) for n, a in dict(g_mix=g_mix, g_ffn=g_ffn, g_v_a=g_v_a, w_spatial=w_spatial, b_spatial=b_spatial,
                                                g_q_b=g_q_b, g_k_b=g_k_b, g_mem=g_mem, g_mq=g_mq, g_mk=g_mk).items()}
    gstack = {n: None for n in big}
    pairing, chipping, token = None, None, None

    def advance(after):
        nonlocal pairing, chipping
        state, tok = _reduce_scatter_middle(pairing[0], after, layer_shards(pairing[1]), idx, pairing[1])
        if chipping is not None:
            _reduce_scatter_end(chipping[0], tok, layer_tensors(chipping[1]), layer_shards(chipping[1]), gstack, idx, chipping[1])
        pairing, chipping = None, (state, pairing[1])
        return tok

    for l in reversed(range(depth)):
        st = saved[l]
        is_a = l % 2 == 0
        li = l // 2
        gbig = {}
        dgu = _ffn_dact(dxm, full["w_down"][l], st["gu"], dep=token)
        token = advance(dgu) if pairing is not None else None
        gbig["w_down"] = _mm_tn_dual("mm_dw_down", st["act"], dxm, pm=512, pn=1024, pk=4096)
        dh2 = _ffn_dh(dgu, full["w_gate_up"][l], dep=token)
        gbig["w_gate_up"] = _ffn_dwgu(st["h2t"], dgu)
        dx, dxm, dg = _rmsnorm_bwd(st["x1"], g_ffn[l], dh2, dx, "rmsnorm_bwd", rstd=st["r2"])
        gsm["g_ffn"][l] = dg[0]
        dcat = _mm_nt("mm_dcat", dxm, full["w_out"][l])
        gbig["w_out"] = _mm_tn_dual("mm_dw_out", st["cat"], dxm, pm=512, pn=1024, pk=4096)
        if is_a:
            dz, dws, dbs, dgv = _mixer_a_bwd(st["z"], dcat, g_v_a[li], st["ws_m"], st["wst_m"], st["b_t"], TOK)
            gsm["w_spatial"][li], gsm["b_spatial"][li], gsm["g_v_a"][li] = dws, dbs[:, :, 0], dgv[0]
        else:
            dq, dk, dz = _attn_bwd(st["q"], st["k"], st["v"], dcat, st["cat"], st["stat"], QPK, st["z"].shape[1])
            dz, dgq_b, dgk_b = _qk_rope_bwd(st["z"], dq, dk, g_q_b[li], g_k_b[li], tabs, TOK, KV, dz)
            gsm["g_q_b"][li], gsm["g_k_b"][li] = dgq_b[0], dgk_b[0]
        dz, dkn, dvm, dgq = _mem_bwd(st["z"], st["qblk"], st["kv"], g_mq[l], g_mk[l], dcat, TOK // MEMW, MEMW, dz)
        dkv, dgk = _memkv_bwd(st["kv"], dkn, dvm, g_mk[l], MEMW)
        gsm["g_mq"][l], gsm["g_mk"][l] = dgq[0], dgk[0]
        gbig["w_mem_kv"] = _mm_tn_dual("mm_dw_memkv", st["mem_n"], dkv)
        dmem_n = _mm_nt("mm_dmemn", dkv, full["w_mem_kv"][l])
        gsm["g_mem"][l] = _rmsnorm_bwd(mem2, g_mem[l], dmem_n, None, "rmsnorm_bwd_mem")[2][0]
        n_in = "w_in_a" if is_a else "w_in_b"
        dh = _mm_nt("mm_dh", dz, full[n_in][li])
        gbig[n_in] = _mm_nn_dual("mm_dw_in", st["ht"], dz, pm=1024, pn=512, pk=4096)
        dx, dxm, dg = _rmsnorm_bwd(st["x"], g_mix[l], dh, dx, "rmsnorm_bwd", rstd=st["r1"])
        gsm["g_mix"][l] = dg[0]
        state, token = _reduce_scatter_begin([gbig[n] for n, _, _ in layer_tensors(l)], layer_shards(l), l)
        pairing = (state, l)

    small = ["g_mix", "g_ffn", "g_v_a", "w_spatial", "b_spatial", "g_q_b", "g_k_b", "g_mem", "g_mq", "g_mk"]
    env = dict(g_mix=g_mix, g_ffn=g_ffn, g_v_a=g_v_a, w_spatial=w_spatial, b_spatial=b_spatial, g_q_b=g_q_b, g_k_b=g_k_b,
               g_mem=g_mem, g_mq=g_mq, g_mk=g_mk,
               m_g_mix=m_g_mix, m_g_ffn=m_g_ffn, m_g_v_a=m_g_v_a, m_w_spatial=m_w_spatial, m_b_spatial=m_b_spatial,
               m_g_q_b=m_g_q_b, m_g_k_b=m_g_k_b, m_g_mem=m_g_mem, m_g_mq=m_g_mq, m_g_mk=m_g_mk,
               v_g_mix=v_g_mix, v_g_ffn=v_g_ffn, v_g_v_a=v_g_v_a, v_w_spatial=v_w_spatial, v_b_spatial=v_b_spatial,
               v_g_q_b=v_g_q_b, v_g_k_b=v_g_k_b, v_g_mem=v_g_mem, v_g_mq=v_g_mq, v_g_mk=v_g_mk,
               m_w_in_a=m_w_in_a, m_w_in_b=m_w_in_b, m_w_mem_kv=m_w_mem_kv, m_w_out=m_w_out, m_w_gate_up=m_w_gate_up, m_w_down=m_w_down,
               v_w_in_a=v_w_in_a, v_w_in_b=v_w_in_b, v_w_mem_kv=v_w_mem_kv, v_w_out=v_w_out, v_w_gate_up=v_w_gate_up, v_w_down=v_w_down)
    like = [env[n] for n in small]
    g_small = _all_reduce_small(_pack([jnp.stack(gsm[n]) for n in small]), dep=token)

    res = {}
    outs = _adamw(_pack(like)[None], g_small[None], _pack([env["m_" + n] for n in small])[None],
                  _pack([env["v_" + n] for n in small])[None], "adamw_small")
    unpacked = [_unpack(o[0], like) for o in outs]
    for k, n in enumerate(small):
        res[n] = [u[k] for u in unpacked]
    advance(outs[1])
    pending = chipping
    last = {n: i for n, i, _ in layer_tensors(pending[1])}
    early = {}
    for n, (w, _) in big.items():
        L = w.shape[0]
        if n not in last:
            res[n] = _adamw(w, gstack[n], env["m_" + n], env["v_" + n], "adamw_" + n)
        elif L > 1:
            assert last[n] == 0
            early[n] = _adamw(w, gstack[n], env["m_" + n], env["v_" + n], "adamw_early_" + n, l0=1)
    done = [o[1] for o in early.values()] + [res[n][1] for n in big if n in res] + [res[small[0]][1]]
    _reduce_scatter_end(pending[0], done, layer_tensors(pending[1]), layer_shards(pending[1]), gstack, idx, pending[1])
    for n in last:
        res[n] = _adamw(big[n][0], gstack[n], env["m_" + n], env["v_" + n], "adamw_last_" + n, l0=0, l1=1, prev=early.get(n))

    order = ["g_mix", "g_ffn", "w_in_a", "g_v_a", "w_spatial", "b_spatial", "w_in_b", "g_q_b", "g_k_b", "g_mem", "w_mem_kv",
             "g_mq", "g_mk", "w_out", "w_gate_up", "w_down"]
    return (loss, dx.reshape(1, T, D), *[res[n][0] for n in order], *[res[n][1] for n in order],
            *[res[n][2] for n in order], *[res[n][3] for n in order])
```

```python
import jax
import jax.numpy as jnp
import numpy as np
from jax import lax
from jax.experimental import pallas as pl
from jax.experimental.pallas import tpu as pltpu

_F32 = jnp.float32
_MXU = jnp.bfloat16
_WIRE = jnp.bfloat16
_KW = {}

EPS = 1e-6
HEAD = 128
CHUNK = 128
GRID_W = 64
ROPE_THETA = 10000.0
ADAM_LR, ADAM_B1, ADAM_B2, ADAM_EPS, ADAM_WD, ADAM_STEP = 0.001, 0.9, 0.999, 1e-08, 0.01, 10
_SQRT_HALF = float(np.sqrt(0.5))
_INV_SQRT_2PI = float(1.0 / np.sqrt(2.0 * np.pi))
_VMEM_LIMIT = 56 * 1024 * 1024
_MESH = pl.DeviceIdType.MESH

_NN = (((1,), (0,)), ((), ()))
_NT = (((1,), (1,)), ((), ()))
_TN = (((0,), (0,)), ((), ()))

S = jax.ShapeDtypeStruct
BS = pl.BlockSpec
_ANY = pl.BlockSpec(memory_space=pl.ANY)


def _tile(n, pref, mult=128):
    if n <= pref:
        return n
    d = (pref // mult) * mult
    while d >= mult:
        if n % d == 0:
            return d
        d -= mult
    raise ValueError(f"no tile for {n} (pref {pref}, mult {mult})")


def _mo(v, m):
    return v if isinstance(v, int) else pl.multiple_of(v, m)


def _cp(*sem, **kw):
    return pltpu.CompilerParams(dimension_semantics=sem or None, vmem_limit_bytes=_VMEM_LIMIT, **kw)


def _call(body, name, **kw):
    return pl.pallas_call(body, name=name, **kw, **_KW)


def _dot(a, b, dn=_NN):
    return lax.dot_general(a, b, dn, preferred_element_type=_F32)


def _gelu(x):
    return 0.5 * x * (1.0 + lax.erf(x * _SQRT_HALF))


def _gelu_grad(x):
    return 0.5 * (1.0 + lax.erf(x * _SQRT_HALF)) + x * jnp.exp(-0.5 * x * x) * _INV_SQRT_2PI


def _rstd(x):
    return lax.rsqrt(jnp.mean(x * x, axis=-1, keepdims=True) + EPS)


def _norm_bwd(dout, xhat, r, g):
    dy = dout * g
    return r * (dy - xhat * jnp.mean(dy * xhat, axis=-1, keepdims=True))


def _softmax(s):
    e = jnp.exp(s - jnp.max(s, axis=-1, keepdims=True))
    return e * (1.0 / jnp.sum(e, axis=-1, keepdims=True))


def _mm(name, a, b, a_spec, b_spec, dn, grid, acc_shape, out_shape, out_specs, epilogue, extra=(), extra_specs=(), dep=None):
    nk = grid[2]
    n_ex = len(extra)
    deps = [] if dep is None else [dep]
    multi = isinstance(out_shape, (list, tuple))
    n_out = len(out_shape) if multi else 1

    def body(*refs):
        a_ref, b_ref = refs[0], refs[1]
        ex = refs[2:2 + n_ex]
        outs = refs[2 + n_ex + len(deps):2 + n_ex + len(deps) + n_out]

        def prod():
            return _dot(a_ref[...].astype(_MXU), b_ref[...].astype(_MXU), dn)

        if nk == 1:
            epilogue(prod(), ex, outs)
        else:
            acc = refs[-1]
            k = pl.program_id(2)

            @pl.when(k == 0)
            def _():
                acc[...] = jnp.zeros_like(acc)

            acc[...] += prod()

            @pl.when(k == nk - 1)
            def _():
                epilogue(acc[...], ex, outs)

    return _call(
        body, name, grid=grid, in_specs=[a_spec, b_spec, *extra_specs] + [_ANY] * len(deps), out_specs=out_specs, out_shape=out_shape,
        scratch_shapes=[] if nk == 1 else [pltpu.VMEM(acc_shape, _F32)],
        compiler_params=_cp("parallel", "parallel", "arbitrary"),
    )(a, b, *extra, *deps)


def _ep_store(acc, ex, outs):
    for o in outs:
        o[...] = acc.astype(o.dtype)


def _ep_add(acc, ex, outs):
    outs[0][...] = (acc + ex[0][...]).astype(outs[0].dtype)


def _mm_nn(name, a, b, out_dtype=_F32, add=None, pm=1024, pn=1024, pk=2048, dep=None):
    M, K = a.shape
    N = b.shape[1]
    tm, tn, tk = _tile(M, pm, 8), _tile(N, pn), _tile(K, pk)
    o_spec = BS((tm, tn), lambda i, j, k: (i, j))
    return _mm(name, a, b, BS((tm, tk), lambda i, j, k: (i, k)), BS((tk, tn), lambda i, j, k: (k, j)), _NN,
               (M // tm, N // tn, K // tk), (tm, tn), S((M, N), out_dtype), o_spec,
               _ep_store if add is None else _ep_add,
               extra=() if add is None else (add,), extra_specs=() if add is None else (o_spec,), dep=dep)


def _mm_nt(name, a, b, out_dtype=_F32, pm=1024, pn=1024, pk=4096):
    M, K = a.shape
    N = b.shape[0]
    tm, tn, tk = _tile(M, pm, 8), _tile(N, pn), _tile(K, pk)
    return _mm(name, a, b, BS((tm, tk), lambda i, j, k: (i, k)), BS((tn, tk), lambda i, j, k: (j, k)), _NT,
               (M // tm, N // tn, K // tk), (tm, tn), S((M, N), out_dtype), BS((tm, tn), lambda i, j, k: (i, j)), _ep_store)


def _mm_tn_dual(name, a, b, pm=1024, pn=1024, pk=2048, dep=None):
    K, M = a.shape
    N = b.shape[1]
    tm, tn, tk = _tile(M, pm), _tile(N, pn), _tile(K, pk, 16)
    o_spec = BS((tm, tn), lambda i, j, k: (i, j))
    return _mm(name, a, b, BS((tk, tm), lambda i, j, k: (k, i)), BS((tk, tn), lambda i, j, k: (k, j)), _TN,
               (M // tm, N // tn, K // tk), (tm, tn), [S((M, N), _F32), S((M, N), _WIRE)], [o_spec, o_spec], _ep_store, dep=dep)


def _ffn_gate_up(h2, wgu):
    T, D = h2.shape
    F = wgu.shape[1] // 2
    tm, tn = _tile(T, 1024, 8), _tile(F, 512)
    nj = F // tn

    def body(a_ref, bg_ref, bu_ref, act_ref, gu_ref):
        a = a_ref[...]
        g = _dot(a, bg_ref[...])
        u = _dot(a, bu_ref[...])
        sg = 1.0 / (1.0 + jnp.exp(-g))
        silu = g * sg
        gu_ref[0] = (u * (sg * (1.0 + g * (1.0 - sg)))).astype(gu_ref.dtype)
        gu_ref[1] = silu.astype(gu_ref.dtype)
        act_ref[...] = (silu * u).astype(act_ref.dtype)

    return _call(
        body, "ffn_gate_up", grid=(T // tm, nj),
        in_specs=[BS((tm, D), lambda i, j: (i, 0)), BS((D, tn), lambda i, j: (0, j)), BS((D, tn), lambda i, j: (0, j + nj))],
        out_specs=[BS((tm, tn), lambda i, j: (i, j)), BS((2, tm, tn), lambda i, j: (0, i, j))],
        out_shape=[S((T, F), _MXU), S((2, T, F), _MXU)],
        compiler_params=_cp("parallel", "parallel"),
    )(h2, wgu, wgu)


def _ffn_dact(dxm, wdown, gu, dep=None):
    T, D = dxm.shape
    F = wdown.shape[0]
    tm, tn = _tile(T, 2048, 8), _tile(F, 512)
    deps = [] if dep is None else [dep]

    def body(a_ref, b_ref, gu_ref, *rest):
        o_ref = rest[-1]
        d = _dot(a_ref[...], b_ref[...], _NT)
        o_ref[0] = (d * gu_ref[0].astype(_F32)).astype(o_ref.dtype)
        o_ref[1] = (d * gu_ref[1].astype(_F32)).astype(o_ref.dtype)

    return _call(
        body, "ffn_dact", grid=(T // tm, F // tn),
        in_specs=[BS((tm, D), lambda i, j: (i, 0)), BS((tn, D), lambda i, j: (j, 0)), BS((2, tm, tn), lambda i, j: (0, i, j))]
        + [_ANY] * len(deps),
        out_specs=BS((2, tm, tn), lambda i, j: (0, i, j)), out_shape=S((2, T, F), _MXU),
        compiler_params=_cp("parallel", "parallel"),
    )(dxm, wdown, gu, *deps)


def _ffn_dh(dgu, wgu, dep=None):
    _, T, F = dgu.shape
    D = wgu.shape[0]
    tm, tn, tk = _tile(T, 512, 8), _tile(D, 2048), _tile(F, 2816, 256)
    nkf = F // tk
    return _mm("ffn_dh", dgu, wgu, BS((None, tm, tk), lambda i, j, k: (k // nkf, i, k % nkf)),
               BS((tn, tk), lambda i, j, k: (j, k)), _NT, (T // tm, D // tn, 2 * nkf), (tm, tn),
               S((T, D), _F32), BS((tm, tn), lambda i, j, k: (i, j)), _ep_store, dep=dep)


def _ffn_dwgu(h2t, dgu):
    _, T, F = dgu.shape
    D = h2t.shape[0]
    tm, tn, tk = _tile(D, 1024), _tile(F, 512), _tile(T, 4096)
    njf = F // tn
    o_spec = BS((tm, tn), lambda i, j, k: (i, j))
    return _mm("ffn_dwgu", h2t, dgu, BS((tm, tk), lambda i, j, k: (i, k)),
               BS((None, tk, tn), lambda i, j, k: (j // njf, k, j % njf)), _NN, (D // tm, 2 * njf, T // tk), (tm, tn),
               [S((D, 2 * F), _F32), S((D, 2 * F), _WIRE)], [o_spec, o_spec], _ep_store)


def _mm_nn_dual(name, a, b, pm=1024, pn=1024, pk=2048):
    M, K = a.shape
    N = b.shape[1]
    tm, tn, tk = _tile(M, pm), _tile(N, pn), _tile(K, pk)
    o_spec = BS((tm, tn), lambda i, j, k: (i, j))
    return _mm(name, a, b, BS((tm, tk), lambda i, j, k: (i, k)), BS((tk, tn), lambda i, j, k: (k, j)), _NN,
               (M // tm, N // tn, K // tk), (tm, tn), [S((M, N), _F32), S((M, N), _WIRE)], [o_spec, o_spec], _ep_store)


def _rmsnorm_fwd(x, g, name, dep=None, transposed=False):
    T, D = x.shape
    tr = _tile(T, 512, 128)
    n_out = 3 if transposed else 1

    def body(x_ref, g_ref, *rest):
        outs = rest[-n_out:]
        xv = x_ref[...]
        r = _rstd(xv)
        h = (xv * r * g_ref[...]).astype(outs[0].dtype)
        outs[0][...] = h
        if transposed:
            outs[1][...] = h.T
            outs[2][...] = r

    row = BS((tr, D), lambda i: (i, 0))
    deps = [] if dep is None else [dep]
    return _call(body, name, grid=(T // tr,), in_specs=[row, BS((1, D), lambda i: (0, 0))] + [_ANY] * len(deps),
                 out_specs=[row, BS((D, tr), lambda i: (0, i)), BS((tr, 1), lambda i: (i, 0))] if transposed else row,
                 out_shape=[S((T, D), _MXU), S((D, T), _MXU), S((T, 1), _F32)] if transposed else S((T, D), _MXU),
                 compiler_params=_cp("parallel"))(x, g.reshape(1, D), *deps)


def _rmsnorm_bwd(x, g, dh, dres, name, rstd=None):
    T, D = x.shape
    tr = _tile(T, 512, 8)
    has_res = dres is not None
    has_r = rstd is not None

    def body(*refs):
        x_ref, g_ref, dh_ref = refs[:3]
        dx_ref, dxm_ref, dg_ref = refs[-3:]

        @pl.when(pl.program_id(0) == 0)
        def _():
            dg_ref[...] = jnp.zeros_like(dg_ref)

        xv = x_ref[...]
        r = refs[3 + has_res][...] if has_r else _rstd(xv)
        xhat = xv * r
        dh_v = dh_ref[...]
        dg_ref[...] += jnp.sum(dh_v * xhat, axis=0, keepdims=True)
        dx = _norm_bwd(dh_v, xhat, r, g_ref[...])
        if has_res:
            dx = dx + refs[3][...]
        dx_ref[...] = dx
        dxm_ref[...] = dx.astype(dxm_ref.dtype)

    row = BS((tr, D), lambda i: (i, 0))
    vec = BS((1, D), lambda i: (0, 0))
    extra = ([dres] if has_res else []) + ([rstd] if has_r else [])
    return _call(body, name, grid=(T // tr,),
                 in_specs=[row, vec, row] + ([row] if has_res else []) + ([BS((tr, 1), lambda i: (i, 0))] if has_r else []),
                 out_specs=[row, row, vec], out_shape=[S((T, D), _F32), S((T, D), _MXU), S((1, D), _F32)],
                 compiler_params=_cp("arbitrary"))(x, g.reshape(1, D), dh, *extra)


def _loss_head(y, target):
    T, D = y.shape
    tr = _tile(T, 512, 8)

    def body(y_ref, t_ref, dy_ref, dym_ref, acc_ref):
        @pl.when(pl.program_id(0) == 0)
        def _():
            acc_ref[...] = jnp.zeros_like(acc_ref)

        err = y_ref[...] - t_ref[...]
        acc_ref[...] += jnp.sum(jnp.sum(err * err, axis=-1, keepdims=True), axis=0, keepdims=True)
        dy = err * (1.0 / D)
        dy_ref[...] = dy
        dym_ref[...] = dy.astype(dym_ref.dtype)

    row = BS((tr, D), lambda i: (i, 0))
    return _call(body, "loss_head", grid=(T // tr,), in_specs=[row, row],
                 out_specs=[row, row, BS((1, 128), lambda i: (0, 0))],
                 out_shape=[S((T, D), _F32), S((T, D), _MXU), S((1, 128), _F32)],
                 compiler_params=_cp("arbitrary"))(y, target)


def _mixa_blocks(T):
    return next(cb for cb in (4, 2, 1) if T % (cb * CHUNK) == 0)


def _mixer_a_fwd(z, gv, ws_m, b_t, TOK, width):
    T = z.shape[0]
    G = TOK // HEAD
    CB = _mixa_blocks(T)
    R = CB * CHUNK

    def body(z_ref, gv_ref, ws_ref, bt_ref, o_ref):
        u = _gelu(z_ref[:, :TOK])
        v = _gelu(z_ref[:, TOK:])
        vn = (v * _rstd(v) * gv_ref[...]).astype(_MXU)
        for c in range(CB):
            rows = slice(c * CHUNK, (c + 1) * CHUNK)
            for g in range(G):
                cols = slice(g * HEAD, (g + 1) * HEAD)
                s = _dot(ws_ref[g], vn[rows, cols]) + bt_ref[:, g:g + 1]
                o_ref[rows, cols] = (u[rows, cols] * s).astype(o_ref.dtype)

    return _call(
        body, "mixer_a_fwd", grid=(T // R,),
        in_specs=[BS((R, 2 * TOK), lambda i: (i, 0)), BS((1, TOK), lambda i: (0, 0)),
                  BS((G, CHUNK, CHUNK), lambda i: (0, 0, 0)), BS((CHUNK, G), lambda i: (0, 0))],
        out_specs=BS((R, TOK), lambda i: (i, 0)), out_shape=S((T, width), _MXU), compiler_params=_cp("parallel"),
    )(z, gv.reshape(1, TOK), ws_m, b_t)


def _mixer_a_bwd(z, dcat, gv, ws_m, wst_m, b_t, TOK):
    T = z.shape[0]
    G = TOK // HEAD
    CB = _mixa_blocks(T)
    R = CB * CHUNK
    n = T // R

    def body(z_ref, d_ref, gv_ref, ws_ref, wst_ref, bt_ref, dz_ref, dws_ref, db_ref, dgv_ref, dvn_scr):
        i = pl.program_id(0)

        @pl.when(i == 0)
        def _():
            dws_ref[...] = jnp.zeros_like(dws_ref)
            db_ref[...] = jnp.zeros_like(db_ref)
            dgv_ref[...] = jnp.zeros_like(dgv_ref)

        zu = z_ref[:, :TOK]
        zv = z_ref[:, TOK:]
        u = _gelu(zu)
        v = _gelu(zv)
        r = _rstd(v)
        vhat = v * r
        gvv = gv_ref[...]
        vn = (vhat * gvv).astype(_MXU)
        d = d_ref[...]
        gpu = _gelu_grad(zu)
        for c in range(CB):
            rows = slice(c * CHUNK, (c + 1) * CHUNK)
            for g in range(G):
                cols = slice(g * HEAD, (g + 1) * HEAD)
                vn_cg = vn[rows, cols]
                s = _dot(ws_ref[g], vn_cg) + bt_ref[:, g:g + 1]
                d_cg = d[rows, cols]
                dz_ref[rows, cols] = (d_cg * s * gpu[rows, cols]).astype(dz_ref.dtype)
                ds = d_cg * u[rows, cols]
                ds_m = ds.astype(_MXU)
                dvn_scr[rows, cols] = _dot(wst_ref[g], ds_m)
                dws_ref[g] += _dot(ds_m, vn_cg, _NT)
                db_ref[g] += ds
        dvn = dvn_scr[...]
        dgv_ref[...] += jnp.sum(dvn * vhat, axis=0, keepdims=True)
        dv = _norm_bwd(dvn, vhat, r, gvv)
        dz_ref[:, TOK:] = (dv * _gelu_grad(zv)).astype(dz_ref.dtype)

        @pl.when(i == n - 1)
        def _():
            for g in range(G):
                db_ref[g] = jnp.broadcast_to(jnp.sum(db_ref[g], axis=1, keepdims=True), (CHUNK, CHUNK))

    full3 = BS((G, CHUNK, CHUNK), lambda i: (0, 0, 0))
    return _call(
        body, "mixer_a_bwd", grid=(n,),
        in_specs=[BS((R, 2 * TOK), lambda i: (i, 0)), BS((R, TOK), lambda i: (i, 0)), BS((1, TOK), lambda i: (0, 0)),
                  full3, full3, BS((CHUNK, G), lambda i: (0, 0))],
        out_specs=[BS((R, 2 * TOK), lambda i: (i, 0)), full3, full3, BS((1, TOK), lambda i: (0, 0))],
        out_shape=[S((T, z.shape[1]), _MXU), S((G, CHUNK, CHUNK), _F32), S((G, CHUNK, CHUNK), _F32), S((1, TOK), _F32)],
        scratch_shapes=[pltpu.VMEM((R, TOK), _F32)], compiler_params=_cp("arbitrary"),
    )(z, dcat, gv.reshape(1, TOK), ws_m, wst_m, b_t)


def _rope_tables(T):
    n_rows = T // GRID_W
    rows = jnp.broadcast_to(jnp.arange(n_rows)[:, None], (n_rows, GRID_W)).reshape(T)
    cols = jnp.broadcast_to(jnp.arange(GRID_W)[None, :], (n_rows, GRID_W)).reshape(T)
    pairs = HEAD // 4
    freqs = ROPE_THETA ** (-jnp.arange(pairs, dtype=_F32) / pairs)
    ang_r = rows.astype(_F32)[:, None] * freqs
    ang_c = cols.astype(_F32)[:, None] * freqs
    ang = jnp.concatenate([ang_r, ang_r, ang_c, ang_c], axis=-1)
    cos, sin = jnp.cos(ang), jnp.sin(ang)
    first = (jnp.arange(HEAD) % (HEAD // 2)) < (HEAD // 4)
    return cos, jnp.where(first, -sin, 0.0), jnp.where(first, 0.0, sin)


def _rope(x, cs, sa, sb):
    return x * cs + pltpu.roll(x, 96, 1) * sa + pltpu.roll(x, 32, 1) * sb


def _qk_rope_fwd(z, gq, gk, tabs, TOK, KV):
    T = z.shape[0]
    R = _tile(T, 512, 8)
    W = TOK + 2 * KV

    def body(z_ref, gq_ref, gk_ref, cos_ref, sa_ref, sb_ref, q_ref, k_ref, v_ref):
        cs, sa, sb = cos_ref[...], sa_ref[...], sb_ref[...]
        for h in range((TOK + KV) // HEAD):
            cols = slice(h * HEAD, (h + 1) * HEAD)
            xv = z_ref[:, cols]
            xn = xv * _rstd(xv) * (gq_ref[...] if h < TOK // HEAD else gk_ref[...])
            out = _rope(xn, cs, sa, sb)
            if h < TOK // HEAD:
                q_ref[:, cols] = out.astype(q_ref.dtype)
            else:
                k_ref[:, h * HEAD - TOK:(h + 1) * HEAD - TOK] = out.astype(k_ref.dtype)
        v_ref[...] = z_ref[:, TOK + KV:].astype(v_ref.dtype)

    vec = BS((1, HEAD), lambda i: (0, 0))
    tab = BS((R, HEAD), lambda i: (i, 0))
    return _call(
        body, "qk_rope_fwd", grid=(T // R,), in_specs=[BS((R, W), lambda i: (i, 0)), vec, vec, tab, tab, tab],
        out_specs=[BS((R, TOK), lambda i: (i, 0)), BS((R, KV), lambda i: (i, 0)), BS((R, KV), lambda i: (i, 0))],
        out_shape=[S((T, TOK), _MXU), S((T, KV), _MXU), S((T, KV), _MXU)], compiler_params=_cp("parallel"),
    )(z, gq.reshape(1, HEAD), gk.reshape(1, HEAD), *tabs)


def _qk_rope_bwd(z, dq, dk, gq, gk, tabs, TOK, KV, into):
    T = z.shape[0]
    R = _tile(T, 512, 8)
    W = TOK + KV

    def body(z_ref, dq_ref, dk_ref, gq_ref, gk_ref, cos_ref, sa_ref, sb_ref, into_ref, dz_ref, dgq_ref, dgk_ref):
        @pl.when(pl.program_id(0) == 0)
        def _():
            dgq_ref[...] = jnp.zeros_like(dgq_ref)
            dgk_ref[...] = jnp.zeros_like(dgk_ref)

        cs, sa, sb = cos_ref[...], sa_ref[...], sb_ref[...]
        for h in range(W // HEAD):
            cols = slice(h * HEAD, (h + 1) * HEAD)
            is_q = h < TOK // HEAD
            do = dq_ref[:, cols] if is_q else dk_ref[:, h * HEAD - TOK:(h + 1) * HEAD - TOK]
            dxn = do * cs - pltpu.roll(do, 96, 1) * sa - pltpu.roll(do, 32, 1) * sb
            xv = z_ref[:, cols]
            r = _rstd(xv)
            xhat = xv * r
            dg_ref = dgq_ref if is_q else dgk_ref
            dg_ref[...] += jnp.sum(dxn * xhat, axis=0, keepdims=True)
            dz_ref[:, cols] = _norm_bwd(dxn, xhat, r, gq_ref[...] if is_q else gk_ref[...]).astype(dz_ref.dtype)

    vec = BS((1, HEAD), lambda i: (0, 0))
    tab = BS((R, HEAD), lambda i: (i, 0))
    return _call(
        body, "qk_rope_bwd", grid=(T // R,),
        in_specs=[BS((R, W), lambda i: (i, 0)), BS((R, TOK), lambda i: (i, 0)), BS((R, KV), lambda i: (i, 0)), vec, vec, tab, tab, tab,
                  _ANY],
        out_specs=[BS((R, W), lambda i: (i, 0)), vec, vec],
        out_shape=[S(into.shape, into.dtype), S((1, HEAD), _F32), S((1, HEAD), _F32)], input_output_aliases={8: 0},
        compiler_params=_cp("arbitrary"),
    )(z, dq, dk, gq.reshape(1, HEAD), gk.reshape(1, HEAD), *tabs, into)


_ATTN_C2 = float(HEAD ** -0.5 * np.log2(np.e))


def _attn_fwd(q, k, v, QPK, width):
    T, TOK = q.shape
    KVH = k.shape[1] // HEAD
    tq = _tile(T, 1024, 8)
    ts = 256 if tq % 256 == 0 else tq
    W = QPK * HEAD

    def body(q_ref, k_ref, v_ref, o_ref, st_ref, vaug):
        @pl.when(pl.program_id(1) == 0)
        def _():
            vaug[:, :HEAD] = v_ref[...]
            vaug[:, HEAD:] = jnp.ones((T, HEAD), vaug.dtype)

        kk, va = k_ref[...], vaug[...]
        for g in range(QPK):
            cols = slice(g * HEAD, (g + 1) * HEAD)
            for r in range(tq // ts):
                rows = slice(r * ts, (r + 1) * ts)
                s = _dot(q_ref[rows, cols], kk, _NT)
                m = jnp.max(s, axis=-1, keepdims=True)
                ov = _dot(jnp.exp2((s - m) * _ATTN_C2).astype(_MXU), va)
                l = ov[:, HEAD:HEAD + 1]
                o_ref[rows, cols] = (ov[:, :HEAD] * (1.0 / l)).astype(o_ref.dtype)
                st_ref[rows, g:g + 1] = m + jnp.log2(l) * (1.0 / _ATTN_C2)

    qs = BS((tq, W), lambda h, i: (i, h))
    ks = BS((T, HEAD), lambda h, i: (0, h))
    return _call(body, "attn_fwd", grid=(KVH, T // tq), in_specs=[qs, ks, ks],
                 out_specs=[qs, BS((None, tq, QPK), lambda h, i: (h, i, 0))],
                 out_shape=[S((T, width), _MXU), S((KVH, T, QPK), _F32)],
                 scratch_shapes=[pltpu.VMEM((T, 2 * HEAD), _MXU)],
                 compiler_params=_cp("parallel", "arbitrary"))(q, k, v)


def _attn_bwd(q, k, v, dcat, o, stat, QPK, width):
    T, TOK = q.shape
    KV = k.shape[1]
    KVH = KV // HEAD
    tq = _tile(T, 512, 8)
    ts = 256 if tq % 256 == 0 else tq
    nq = T // tq
    W = QPK * HEAD
    scale = HEAD ** -0.5

    def body(q_ref, k_ref, v_ref, do_ref, o_ref, st_ref, dq_ref, dk_ref, dv_ref, dk_acc, dv_acc, ds_all, p_all, q_all, do_all):
        i = pl.program_id(1)

        @pl.when(i == 0)
        def _():
            dk_acc[...] = jnp.zeros_like(dk_acc)
            dv_acc[...] = jnp.zeros_like(dv_acc)

        kk, vv = k_ref[...], v_ref[...]
        for r in range(tq // ts):
            rows = slice(r * ts, (r + 1) * ts)
            for g in range(QPK):
                cols = slice(g * HEAD, (g + 1) * HEAD)
                stack = slice(g * ts, (g + 1) * ts)
                qg = q_ref[rows, cols]
                p = jnp.exp2((_dot(qg, kk, _NT) - st_ref[rows, g:g + 1]) * _ATTN_C2)
                do32 = do_ref[rows, cols]
                do = do32.astype(_MXU)
                delta = jnp.sum(do32 * o_ref[rows, cols].astype(_F32), axis=-1, keepdims=True)
                ds = (p * (_dot(do, vv, _NT) - delta)).astype(_MXU)
                dq_ref[rows, cols] = _dot(ds, kk) * scale
                ds_all[stack, :] = ds
                p_all[stack, :] = p.astype(_MXU)
                q_all[stack, :] = qg
                do_all[stack, :] = do
            dk_acc[...] += _dot(ds_all[...], q_all[...], _TN)
            dv_acc[...] += _dot(p_all[...], do_all[...], _TN)

        @pl.when(i == nq - 1)
        def _():
            dk_ref[...] = dk_acc[...] * scale
            dv_ref[...] = dv_acc[...].astype(dv_ref.dtype)

    qs = BS((tq, W), lambda h, i: (i, h))
    ks = BS((T, HEAD), lambda h, i: (0, h))
    return _call(
        body, "attn_bwd", grid=(KVH, nq), in_specs=[qs, ks, ks, qs, qs, BS((None, tq, QPK), lambda h, i: (h, i, 0))],
        out_specs=[qs, ks, BS((T, HEAD), lambda h, i: (0, (TOK + KV) // HEAD + h))],
        out_shape=[S((T, TOK), _F32), S((T, KV), _F32), S((T, width), _MXU)],
        scratch_shapes=[pltpu.VMEM((T, HEAD), _F32), pltpu.VMEM((T, HEAD), _F32), pltpu.VMEM((QPK * ts, T), _MXU),
                        pltpu.VMEM((QPK * ts, T), _MXU), pltpu.VMEM((QPK * ts, HEAD), _MXU), pltpu.VMEM((QPK * ts, HEAD), _MXU)],
        compiler_params=_cp("parallel", "arbitrary"),
    )(q, k, v, dcat, o, stat)


def _mem_fwd(z, qblk, kv, gmq, gmk, MEMW, into):
    T = z.shape[0]
    NM = kv.shape[0]
    tq = _tile(T, 512, 8)
    scale = HEAD ** -0.5
    oblk = into.shape[1] // MEMW - 1

    def body(q_ref, kv_ref, gq_ref, gk_ref, into_ref, o_ref):
        for h in range(MEMW // HEAD):
            cols = slice(h * HEAD, (h + 1) * HEAD)
            kx = kv_ref[:, cols]
            kn = (kx * _rstd(kx) * gk_ref[...]).astype(_MXU)
            vv = kv_ref[:, MEMW + h * HEAD:MEMW + (h + 1) * HEAD].astype(_MXU)
            qx = q_ref[:, cols]
            qn = (qx * _rstd(qx) * gq_ref[...]).astype(_MXU)
            p = _softmax(_dot(qn, kn, _NT) * scale)
            o_ref[:, cols] = _dot(p.astype(_MXU), vv).astype(o_ref.dtype)

    vec = BS((1, HEAD), lambda i: (0, 0))
    return _call(
        body, "mem_fwd", grid=(T // tq,),
        in_specs=[BS((tq, MEMW), lambda i: (i, qblk)), BS((NM, 2 * MEMW), lambda i: (0, 0)), vec, vec, _ANY],
        out_specs=BS((tq, MEMW), lambda i: (i, oblk)), out_shape=S(into.shape, into.dtype), input_output_aliases={4: 0},
        compiler_params=_cp("parallel"),
    )(z, kv, gmq.reshape(1, HEAD), gmk.reshape(1, HEAD), into)


def _mem_bwd(z, qblk, kv, gmq, gmk, dcat, dblk, MEMW, into):
    T = z.shape[0]
    NM = kv.shape[0]
    tq = _tile(T, 512, 8)
    scale = HEAD ** -0.5

    def body(q_ref, kv_ref, gq_ref, gk_ref, do_ref, into_ref, dz_ref, dkn_ref, dv_ref, dgq_ref):
        @pl.when(pl.program_id(0) == 0)
        def _():
            dkn_ref[...] = jnp.zeros_like(dkn_ref)
            dv_ref[...] = jnp.zeros_like(dv_ref)
            dgq_ref[...] = jnp.zeros_like(dgq_ref)

        for h in range(MEMW // HEAD):
            cols = slice(h * HEAD, (h + 1) * HEAD)
            kx = kv_ref[:, cols]
            kn = (kx * _rstd(kx) * gk_ref[...]).astype(_MXU)
            vv = kv_ref[:, MEMW + h * HEAD:MEMW + (h + 1) * HEAD].astype(_MXU)
            qx = q_ref[:, cols]
            rq = _rstd(qx)
            qhat = qx * rq
            qn = (qhat * gq_ref[...]).astype(_MXU)
            p = _softmax(_dot(qn, kn, _NT) * scale)
            do = do_ref[:, cols].astype(_MXU)
            dp = _dot(do, vv, _NT)
            ds = (p * (dp - jnp.sum(p * dp, axis=-1, keepdims=True)) * scale).astype(_MXU)
            dqn = _dot(ds, kn)
            dkn_ref[:, cols] += _dot(ds, qn, _TN)
            dv_ref[:, cols] += _dot(p.astype(_MXU), do, _TN)
            dgq_ref[...] += jnp.sum(dqn * qhat, axis=0, keepdims=True)
            dz_ref[:, cols] = _norm_bwd(dqn, qhat, rq, gq_ref[...]).astype(dz_ref.dtype)

    vec = BS((1, HEAD), lambda i: (0, 0))
    kvs = BS((NM, MEMW), lambda i: (0, 0))
    return _call(
        body, "mem_bwd", grid=(T // tq,),
        in_specs=[BS((tq, MEMW), lambda i: (i, qblk)), BS((NM, 2 * MEMW), lambda i: (0, 0)), vec, vec,
                  BS((tq, MEMW), lambda i: (i, dblk)), _ANY],
        out_specs=[BS((tq, MEMW), lambda i: (i, qblk)), kvs, kvs, vec],
        out_shape=[S(into.shape, into.dtype), S((NM, MEMW), _F32), S((NM, MEMW), _F32), S((1, HEAD), _F32)],
        input_output_aliases={5: 0}, compiler_params=_cp("arbitrary"),
    )(z, kv, gmq.reshape(1, HEAD), gmk.reshape(1, HEAD), dcat, into)


def _memkv_bwd(kv, dkn, dv, gmk, MEMW):
    NM = kv.shape[0]

    def body(kv_ref, dkn_ref, dv_ref, gk_ref, dkv_ref, dgk_ref):
        dgk = jnp.zeros((1, HEAD), _F32)
        for h in range(MEMW // HEAD):
            cols = slice(h * HEAD, (h + 1) * HEAD)
            kx = kv_ref[:, cols]
            r = _rstd(kx)
            khat = kx * r
            dkn = dkn_ref[:, cols]
            dgk = dgk + jnp.sum(dkn * khat, axis=0, keepdims=True)
            dkv_ref[:, cols] = _norm_bwd(dkn, khat, r, gk_ref[...]).astype(dkv_ref.dtype)
        dgk_ref[...] = dgk
        dkv_ref[:, MEMW:] = dv_ref[...].astype(dkv_ref.dtype)

    return _call(body, "memkv_bwd", out_shape=[S((NM, 2 * MEMW), _MXU), S((1, HEAD), _F32)],
                 compiler_params=_cp())(kv, dkn, dv, gmk.reshape(1, HEAD))


def _cast_into_full(w, l, sh, idx, dep=None):
    tr, tc = _tile(sh.Rs, 512, 16), _tile(sh.Cs, 2048)
    nr, nc = sh.Rs // tr, sh.Cs // tc
    deps = [] if dep is None else [dep]

    def body(i_ref, c_ref, w_ref, *rest):
        rest[-1][...] = w_ref[...].astype(rest[-1].dtype)

    if sh.by_cols:
        o_map = lambda a, b, si, sc: (a, si[0] * nc + b)
    else:
        o_map = lambda a, b, si, sc: (si[0] * nr + a, b)
    return _call(
        body, "cast_into_full",
        grid_spec=pltpu.PrefetchScalarGridSpec(
            num_scalar_prefetch=2, grid=(nr, nc),
            in_specs=[BS((None, tr, tc), lambda a, b, si, sc: (l, a, b))] + [_ANY] * len(deps), out_specs=BS((tr, tc), o_map)),
        out_shape=S((sh.R, sh.C), _WIRE), compiler_params=_cp("parallel", "parallel"),
    )(*idx, w, *deps)


def _adamw(w, g, m, v, name, l0=0, l1=None, prev=None):
    L, R, C = w.shape
    l1 = L if l1 is None else l1
    tc = _tile(C, 2048)
    tr = _tile(R, max(8, (512 * 1024) // tc), 8)
    c_m = 1.0 / (1.0 - ADAM_B1 ** ADAM_STEP)
    c_v = 1.0 / (1.0 - ADAM_B2 ** ADAM_STEP)

    def body(w_ref, g_ref, m_ref, v_ref, *rest):
        go_ref, d_ref, mo_ref, vo_ref = rest[-4:]
        gv = g_ref[...]
        mn = ADAM_B1 * m_ref[...] + (1.0 - ADAM_B1) * gv
        vn = ADAM_B2 * v_ref[...] + (1.0 - ADAM_B2) * (gv * gv)
        go_ref[...] = gv
        mo_ref[...] = mn
        vo_ref[...] = vn
        d_ref[...] = -ADAM_LR * ((mn * c_m) / (jnp.sqrt(vn * c_v) + ADAM_EPS) + ADAM_WD * w_ref[...])

    blk = BS((None, tr, tc), lambda a, i, j: (l0 + a, i, j))
    prevs = [] if prev is None else list(prev)
    return _call(body, name, grid=(l1 - l0, R // tr, C // tc), in_specs=[blk] * 4 + [_ANY] * len(prevs), out_specs=[blk] * 4,
                 out_shape=[S((L, R, C), _F32)] * 4, input_output_aliases={4 + k: k for k in range(len(prevs))},
                 compiler_params=_cp("parallel", "parallel", "parallel"))(w, g, m, v, *prevs)


def _where_am_i():
    x, y, c = lax.axis_index("x"), lax.axis_index("y"), lax.axis_index("c")
    chips = [(1 - x, y), (x, 1 - y), (1 - x, 1 - y)]
    return x, y, c, 2 * x + y, chips, [2 * cx + cy for cx, cy in chips]


class _Shard:
    def __init__(self, R, C, by_cols):
        self.R, self.C, self.by_cols = R, C, by_cols
        self.Rs, self.Cs = (R, C // 4) if by_cols else (R // 4, C)
        self.Rh = self.Rs // 2
        self.Q = R // 8

    def full_piece(self, ref, j, cc):
        if self.by_cols:
            return ref.at[pl.ds(cc * self.Rh, self.Rh), pl.ds(_mo(j * self.Cs, 128), self.Cs)]
        return ref.at[pl.ds(_mo(j * self.Rs + cc * self.Rh, 16), self.Rh), :]

    def full_shard(self, ref, j):
        if self.by_cols:
            return ref.at[:, pl.ds(_mo(j * self.Cs, 128), self.Cs)]
        return ref.at[pl.ds(_mo(j * self.Rs, 16), self.Rs), :]

    def shard_half(self, ref, cc):
        return ref.at[pl.ds(_mo(cc * self.Rh, 16), self.Rh), :]

    def half_piece(self, ref, j):
        if self.by_cols:
            return ref.at[:, pl.ds(_mo(j * self.Cs, 128), self.Cs)]
        return ref.at[pl.ds(_mo(j * self.Rh, 16), self.Rh), :]


def _remote(src, dst, ssem, rsem, dev):
    return pltpu.make_async_remote_copy(src_ref=src, dst_ref=dst, send_sem=ssem, recv_sem=rsem, device_id=dev, device_id_type=_MESH)


_HBM = pl.BlockSpec(memory_space=pltpu.HBM)
_SEM = pl.BlockSpec(memory_space=pltpu.SEMAPHORE)
_EFFECT = pltpu.SideEffectType.DATAFLOW_SIDE_EFFECTING


def _in_hbm(a):
    return pltpu.with_memory_space_constraint(a, pltpu.HBM)


def _gather_start(fulls, shs, name):
    n = len(fulls)

    def body(*refs):
        bufs = refs[:n]
        send_sems, recv_sems = refs[n], refs[n + 1]
        token = refs[-1]
        x, y, c, me, chips, chip_ids = _where_am_i()
        for t in range(n):
            mine = shs[t].full_piece(bufs[t], me, c)
            for r in range(3):
                _remote(mine, mine, send_sems.at[3 * t + r], recv_sems.at[3 * t + r], (*chips[r], c)).start()
        token[...] = jnp.zeros_like(token)

    out = pl.pallas_call(
        body, name=name, in_specs=[_HBM] * n,
        out_shape=(pltpu.SemaphoreType.DMA((3 * n,)), pltpu.SemaphoreType.DMA((3 * n,)), *[pltpu.HBM(f.shape, f.dtype) for f in fulls],
                   S((8, 128), _F32)),
        out_specs=(_SEM, _SEM, *[_HBM] * n, pl.BlockSpec(memory_space=pltpu.VMEM)),
        input_output_aliases={t: 2 + t for t in range(n)},
        compiler_params=pltpu.CompilerParams(has_side_effects=_EFFECT), **_KW,
    )(*[_in_hbm(f) for f in fulls])
    return out[0], out[1], list(out[2:2 + n]), out[-1]


def _gather_wait(fulls, send_sems, recv_sems, after, shs, name):
    n = len(fulls)

    def body(*refs):
        bufs = refs[:n]
        ssem, rsem = refs[n], refs[n + 1]
        x, y, c, me, chips, chip_ids = _where_am_i()
        for t in range(n):
            mine = shs[t].full_piece(bufs[t], me, c)
            for r in range(3):
                _remote(mine, mine, ssem.at[3 * t + r], rsem.at[3 * t + r], (*chips[r], c)).wait_send()
        for t in range(n):
            for r in range(3):
                piece = shs[t].full_piece(bufs[t], chip_ids[r], c)
                _remote(piece, piece, ssem.at[3 * t + r], rsem.at[3 * t + r], (*chips[r], c)).wait_recv()

    out = pl.pallas_call(
        body, name=name, in_specs=[*[_HBM] * n, _SEM, _SEM, _ANY], out_specs=[_HBM] * n,
        out_shape=[pltpu.HBM(f.shape, f.dtype) for f in fulls], input_output_aliases={t: t for t in range(n)},
        compiler_params=pltpu.CompilerParams(has_side_effects=_EFFECT), **_KW,
    )(*fulls, send_sems, recv_sems, after)
    return list(out)


def _gather_pass_on(fulls, shs, name):
    n = len(fulls)

    def body(*refs):
        bufs = refs[n:2 * n]
        send_sems, recv_sems = refs[2 * n:]
        x, y, c, me, chips, chip_ids = _where_am_i()
        sib = (x, y, 1 - c)
        cps = []
        for t in range(n):
            for r in range(3):
                piece = shs[t].full_piece(bufs[t], chip_ids[r], c)
                cps.append(_remote(piece, piece, send_sems.at[t, r], recv_sems.at[t, r], sib))
        for cp in cps:
            cp.start()
        for t in range(n):
            for r in range(3):
                piece = shs[t].full_piece(bufs[t], chip_ids[r], 1 - c)
                _remote(piece, piece, send_sems.at[t, r], recv_sems.at[t, r], sib).wait_recv()
        for cp in cps:
            cp.wait_send()

    return _call(
        body, name, in_specs=[_ANY] * n, out_specs=[_ANY] * n, out_shape=[S(f.shape, f.dtype) for f in fulls],
        input_output_aliases={t: t for t in range(n)},
        scratch_shapes=[pltpu.SemaphoreType.DMA((n, 3)), pltpu.SemaphoreType.DMA((n, 3))],
        compiler_params=pltpu.CompilerParams(has_side_effects=True),
    )(*fulls)


def _pass_on_copies(bufs, shs, send_sems, recv_sems):
    x, y, c, me, chips, chip_ids = _where_am_i()
    sib = (x, y, 1 - c)
    out, back = [], []
    for t in range(len(bufs)):
        for r in range(3):
            piece = shs[t].full_piece(bufs[t], chip_ids[r], c)
            out.append(_remote(piece, piece, send_sems.at[3 * t + r], recv_sems.at[3 * t + r], sib))
            other = shs[t].full_piece(bufs[t], chip_ids[r], 1 - c)
            back.append(_remote(other, other, send_sems.at[3 * t + r], recv_sems.at[3 * t + r], sib))
    return out, back


def _pass_on_start(fulls, shs, name):
    n = len(fulls)

    def body(*refs):
        for cp in _pass_on_copies(refs[:n], shs, refs[n], refs[n + 1])[0]:
            cp.start()
        refs[-1][...] = jnp.zeros_like(refs[-1])

    out = pl.pallas_call(
        body, name=name, in_specs=[_HBM] * n,
        out_shape=(pltpu.SemaphoreType.DMA((3 * n,)), pltpu.SemaphoreType.DMA((3 * n,)), *[pltpu.HBM(f.shape, f.dtype) for f in fulls],
                   S((8, 128), _F32)),
        out_specs=(_SEM, _SEM, *[_HBM] * n, pl.BlockSpec(memory_space=pltpu.VMEM)),
        input_output_aliases={t: 2 + t for t in range(n)},
        compiler_params=pltpu.CompilerParams(has_side_effects=_EFFECT), **_KW,
    )(*[_in_hbm(f) for f in fulls])
    return out[0], out[1], list(out[2:2 + n]), out[-1]


def _pass_on_wait(fulls, send_sems, recv_sems, after, shs, name):
    n = len(fulls)

    def body(*refs):
        out, back = _pass_on_copies(refs[:n], shs, refs[n], refs[n + 1])
        for cp in out:
            cp.wait_send()
        for cp in back:
            cp.wait_recv()

    out = pl.pallas_call(
        body, name=name, in_specs=[*[_HBM] * n, _SEM, _SEM, _ANY], out_specs=[_HBM] * n,
        out_shape=[pltpu.HBM(f.shape, f.dtype) for f in fulls], input_output_aliases={t: t for t in range(n)},
        compiler_params=pltpu.CompilerParams(has_side_effects=_EFFECT), **_KW,
    )(*fulls, send_sems, recv_sems, after)
    return list(out)


def _rs_pair_copies(ins, outs, shs, send_sems, recv_sems):
    x, y, c, *_ = _where_am_i()
    sib = (x, y, 1 - c)
    cps = []
    for t in range(len(ins)):
        sh = shs[t]
        if sh.by_cols:
            cps.append(_remote(ins[t].at[pl.ds((1 - c) * sh.Rh, sh.Rh), :], outs[t], send_sems.at[4 * t], recv_sems.at[4 * t], sib))
        else:
            for j in range(4):
                cps.append(_remote(sh.full_piece(ins[t], j, 1 - c), sh.half_piece(outs[t], j),
                                   send_sems.at[4 * t + j], recv_sems.at[4 * t + j], sib))
    return cps


def _rs_pair_start(dws, shs, name):
    n = len(dws)
    lands = [lax.empty((sh.R // 2, sh.C), _WIRE) for sh in shs]

    def body(*refs):
        for cp in _rs_pair_copies(refs[:n], refs[n:2 * n], shs, refs[2 * n], refs[2 * n + 1]):
            cp.start()
        refs[-1][...] = jnp.zeros_like(refs[-1])

    out = pl.pallas_call(
        body, name=name, in_specs=[_HBM] * (2 * n),
        out_shape=(pltpu.SemaphoreType.DMA((4 * n,)), pltpu.SemaphoreType.DMA((4 * n,)),
                   *[pltpu.HBM(a.shape, a.dtype) for a in (*dws, *lands)], S((8, 128), _F32)),
        out_specs=(_SEM, _SEM, *[_HBM] * (2 * n), pl.BlockSpec(memory_space=pltpu.VMEM)),
        input_output_aliases={t: 2 + t for t in range(2 * n)},
        compiler_params=pltpu.CompilerParams(has_side_effects=_EFFECT), **_KW,
    )(*[_in_hbm(a) for a in (*dws, *lands)])
    return out[0], out[1], list(out[2:2 + n]), list(out[2 + n:2 + 2 * n]), out[-1]


def _rs_pair_wait(dws, lands, send_sems, recv_sems, after, shs, name):
    n = len(dws)

    def body(*refs):
        cps = _rs_pair_copies(refs[:n], refs[n:2 * n], shs, refs[2 * n], refs[2 * n + 1])
        for cp in cps:
            cp.wait_send()
        for cp in cps:
            cp.wait_recv()

    out = pl.pallas_call(
        body, name=name, in_specs=[*[_HBM] * (2 * n), _SEM, _SEM, _ANY], out_specs=[_HBM] * (2 * n),
        out_shape=[pltpu.HBM(a.shape, a.dtype) for a in (*dws, *lands)], input_output_aliases={t: t for t in range(2 * n)},
        compiler_params=pltpu.CompilerParams(has_side_effects=_EFFECT), **_KW,
    )(*dws, *lands, send_sems, recv_sems, after)
    return list(out[n:])


def _rs_pair_add(dw32, recv, sh, idx):
    tr, tc = _tile(sh.Q, 512, 16), _tile(sh.C, 2048)
    nb = sh.Q // tr

    def body(i_ref, c_ref, a_ref, b_ref, ow_ref):
        ow_ref[...] = (a_ref[...] + b_ref[...].astype(_F32)).astype(ow_ref.dtype)

    if sh.by_cols:
        a_map = lambda j, i, b, si, sc: (sc[0] * 4 * nb + j * nb + i, b)
    else:
        a_map = lambda j, i, b, si, sc: (j * 2 * nb + sc[0] * nb + i, b)
    h_spec = BS((tr, tc), lambda j, i, b, si, sc: (j * nb + i, b))
    return _call(
        body, "rs_pair_add",
        grid_spec=pltpu.PrefetchScalarGridSpec(num_scalar_prefetch=2, grid=(4, nb, sh.C // tc),
                                               in_specs=[BS((tr, tc), a_map), h_spec], out_specs=h_spec),
        out_shape=S((sh.R // 2, sh.C), _WIRE), compiler_params=_cp("parallel", "parallel", "parallel"),
    )(*idx, dw32, recv)


def _rs_chip_start(pws, shs, name):
    n = len(pws)
    lands = [lax.empty((3, sh.Rh, sh.Cs), _WIRE) for sh in shs]

    def body(*refs):
        ins, lnd = refs[:n], refs[n:2 * n]
        send_sems, recv_sems = refs[2 * n], refs[2 * n + 1]
        token = refs[-1]
        x, y, c, me, chips, chip_ids = _where_am_i()
        for t in range(n):
            for r in range(3):
                _remote(shs[t].half_piece(ins[t], chip_ids[r]), lnd[t].at[r], send_sems.at[3 * t + r], recv_sems.at[3 * t + r],
                        (*chips[r], c)).start()
        token[...] = jnp.zeros_like(token)

    out = pl.pallas_call(
        body, name=name, in_specs=[_HBM] * (2 * n),
        out_shape=(pltpu.SemaphoreType.DMA((3 * n,)), pltpu.SemaphoreType.DMA((3 * n,)),
                   *[pltpu.HBM(a.shape, a.dtype) for a in (*pws, *lands)], S((8, 128), _F32)),
        out_specs=(_SEM, _SEM, *[_HBM] * (2 * n), pl.BlockSpec(memory_space=pltpu.VMEM)),
        input_output_aliases={t: 2 + t for t in range(2 * n)},
        compiler_params=pltpu.CompilerParams(has_side_effects=_EFFECT), **_KW,
    )(*[_in_hbm(a) for a in (*pws, *lands)])
    return out[0], out[1], list(out[2:2 + n]), list(out[2 + n:2 + 2 * n]), out[-1]


def _rs_chip_wait(pws, lands, send_sems, recv_sems, after, shs, name):
    n = len(pws)
    after = list(after) if isinstance(after, (list, tuple)) else [after]

    def body(*refs):
        ins, lnd = refs[:n], refs[n:2 * n]
        ssem, rsem = refs[2 * n], refs[2 * n + 1]
        x, y, c, me, chips, chip_ids = _where_am_i()
        for t in range(n):
            for r in range(3):
                cp = _remote(shs[t].half_piece(ins[t], chip_ids[r]), lnd[t].at[r], ssem.at[3 * t + r], rsem.at[3 * t + r], (*chips[r], c))
                cp.wait_send()
        for t in range(n):
            for r in range(3):
                cp = _remote(shs[t].half_piece(ins[t], chip_ids[r]), lnd[t].at[r], ssem.at[3 * t + r], rsem.at[3 * t + r], (*chips[r], c))
                cp.wait_recv()

    out = pl.pallas_call(
        body, name=name, in_specs=[*[_HBM] * (2 * n), _SEM, _SEM, *[_ANY] * len(after)], out_specs=[_HBM] * (2 * n),
        out_shape=[pltpu.HBM(a.shape, a.dtype) for a in (*pws, *lands)], input_output_aliases={t: t for t in range(2 * n)},
        compiler_params=pltpu.CompilerParams(has_side_effects=_EFFECT), **_KW,
    )(*pws, *lands, send_sems, recv_sems, *after)
    return list(out[n:])


def _rs_chip_add(dw32, pair, recv, sh, idx, g_prev, l, L):
    tr, tc = _tile(sh.Rh, 512, 16), _tile(sh.Cs, 2048)
    nr, nc = sh.Rh // tr, sh.Cs // tc

    def body(i_ref, c_ref, d_ref, a_ref, b_ref, *rest):
        rest[-1][...] = ((d_ref[...] + a_ref[...].astype(_F32)) + b_ref[0].astype(_F32) + b_ref[1].astype(_F32)
                         + b_ref[2].astype(_F32))

    if sh.by_cols:
        d_map = lambda a, b, si, sc: (sc[0] * nr + a, si[0] * nc + b)
        a_map = lambda a, b, si, sc: (a, si[0] * nc + b)
    else:
        d_map = lambda a, b, si, sc: ((2 * si[0] + sc[0]) * nr + a, b)
        a_map = lambda a, b, si, sc: (si[0] * nr + a, b)
    in_specs = [BS((tr, tc), d_map), BS((tr, tc), a_map), BS((3, tr, tc), lambda a, b, si, sc: (0, a, b))]
    args = [*idx, dw32, pair, recv]
    if g_prev is not None:
        in_specs.append(_ANY)
        args.append(g_prev)
    return _call(
        body, "rs_chip_add",
        grid_spec=pltpu.PrefetchScalarGridSpec(num_scalar_prefetch=2, grid=(nr, nc), in_specs=in_specs,
                                               out_specs=BS((None, tr, tc), lambda a, b, si, sc: (l, sc[0] * nr + a, b))),
        out_shape=S((L, sh.Rs, sh.Cs), _F32), input_output_aliases={} if g_prev is None else {5: 0},
        compiler_params=_cp("parallel", "parallel"),
    )(*args)


def _pair_share_copies(bufs, ls, shs, send_sems, recv_sems):
    x, y, c, *_ = _where_am_i()
    sib = (x, y, 1 - c)
    out, back = [], []
    for t in range(len(bufs)):
        mine = shs[t].shard_half(bufs[t].at[ls[t]], c)
        out.append(_remote(mine, mine, send_sems.at[t], recv_sems.at[t], sib))
        other = shs[t].shard_half(bufs[t].at[ls[t]], 1 - c)
        back.append(_remote(other, other, send_sems.at[t], recv_sems.at[t], sib))
    return out, back


def _pair_share_start(gs, ls, shs, name):
    n = len(gs)

    def body(*refs):
        for cp in _pair_share_copies(refs[:n], ls, shs, refs[n], refs[n + 1])[0]:
            cp.start()
        refs[-1][...] = jnp.zeros_like(refs[-1])

    out = pl.pallas_call(
        body, name=name, in_specs=[_HBM] * n,
        out_shape=(pltpu.SemaphoreType.DMA((n,)), pltpu.SemaphoreType.DMA((n,)), *[pltpu.HBM(g.shape, g.dtype) for g in gs],
                   S((8, 128), _F32)),
        out_specs=(_SEM, _SEM, *[_HBM] * n, pl.BlockSpec(memory_space=pltpu.VMEM)),
        input_output_aliases={t: 2 + t for t in range(n)},
        compiler_params=pltpu.CompilerParams(has_side_effects=_EFFECT), **_KW,
    )(*[_in_hbm(g) for g in gs])
    return out[0], out[1], list(out[2:2 + n]), out[-1]


def _pair_share_wait(gs, send_sems, recv_sems, after, ls, shs, name):
    n = len(gs)

    def body(*refs):
        out, back = _pair_share_copies(refs[:n], ls, shs, refs[n], refs[n + 1])
        for cp in out:
            cp.wait_send()
        for cp in back:
            cp.wait_recv()

    out = pl.pallas_call(
        body, name=name, in_specs=[*[_HBM] * n, _SEM, _SEM, _ANY], out_specs=[_HBM] * n,
        out_shape=[pltpu.HBM(g.shape, g.dtype) for g in gs], input_output_aliases={t: t for t in range(n)},
        compiler_params=pltpu.CompilerParams(has_side_effects=_EFFECT), **_KW,
    )(*gs, send_sems, recv_sems, after)
    return list(out)


def _all_reduce_small(xs, dep=None):
    M = xs.shape[0]
    deps = [] if dep is None else [dep]

    def body(x_ref, *rest):
        tot_ref, out_ref, send_sems, recv_sems, local_sem = rest[len(deps):]
        x, y, c, me, chips, chip_ids = _where_am_i()
        sib = (x, y, 1 - c)

        def rows(dev):
            return out_ref.at[pl.ds(_mo((4 * dev[0] + 2 * dev[1] + dev[2]) * M, 8), M), :]

        def copy(k, block, to, src=None):
            return _remote(rows(block) if src is None else src, rows(block), send_sems.at[k], recv_sems.at[k], to)

        mine = pltpu.make_async_copy(x_ref, rows((x, y, c)), local_sem)
        mine.start()
        first = [copy(0, (x, y, c), sib, src=x_ref)]
        first += [copy(1 + j, (x, y, c), (*chip, c), src=x_ref) for j, chip in enumerate(chips)]
        for cp in first:
            cp.start()
        passed = [copy(4 + j, (*chip, c), sib) for j, chip in enumerate(chips)]
        for j, chip in enumerate(chips):
            copy(1 + j, (*chip, c), (x, y, c)).wait_recv()
            passed[j].start()
        copy(0, sib, (x, y, c)).wait_recv()
        for j, chip in enumerate(chips):
            copy(4 + j, (*chip, 1 - c), (x, y, c)).wait_recv()
        for cp in first + passed:
            cp.wait_send()
        mine.wait()
        tot = out_ref[pl.ds(0, M), :]
        for d in range(1, 8):
            tot = tot + out_ref[pl.ds(d * M, M), :]
        tot_ref[...] = tot

    vm = pl.BlockSpec(memory_space=pltpu.VMEM)
    return _call(
        body, "all_reduce_small", in_specs=[vm] + [_ANY] * len(deps), out_specs=[vm, vm],
        out_shape=[S((M, 128), _F32), S((8 * M, 128), _F32)],
        scratch_shapes=[pltpu.SemaphoreType.DMA((7,)), pltpu.SemaphoreType.DMA((7,)), pltpu.SemaphoreType.DMA],
        compiler_params=_cp(has_side_effects=True),
    )(xs, *deps)[0]


def _reduce_scatter_begin(dws, shs, l):
    ssem, rsem, dww, lands, token = _rs_pair_start([d[1] for d in dws], shs, f"rs_pair_start_{l}")
    return ([d[0] for d in dws], dww, lands, ssem, rsem), token


def _reduce_scatter_middle(state, after, shs, idx, l):
    dw32s, dww, lands, ssem, rsem = state
    recv_a = _rs_pair_wait(dww, lands, ssem, rsem, after, shs, f"rs_pair_wait_{l}")
    pws = [_rs_pair_add(d32, ra, sh, idx) for d32, ra, sh in zip(dw32s, recv_a, shs)]
    ssem, rsem, pws, lands, token = _rs_chip_start(pws, shs, f"rs_chip_start_{l}")
    return (dw32s, recv_a, pws, lands, ssem, rsem), token


def _reduce_scatter_end(state, after, tensors, shs, gstack, idx, l):
    dw32s, recv_a, pws, lands, ssem, rsem = state
    recv_b = _rs_chip_wait(pws, lands, ssem, rsem, after, shs, f"rs_chip_wait_{l}")
    gs = [_rs_chip_add(d32, ra, rb, sh, idx, gstack[name], i, L)
          for d32, ra, rb, sh, (name, i, L) in zip(dw32s, recv_a, recv_b, shs, tensors)]
    ssem, rsem, gs, token = _pair_share_start(gs, [i for _, i, _ in tensors], shs, f"pair_share_start_{l}")
    for (name, _, _), g in zip(tensors, gs):
        gstack[name] = g
    return ssem, rsem, token, tensors, shs, l


def _reduce_scatter_shared(share, gstack):
    ssem, rsem, token, tensors, shs, l = share
    gs = _pair_share_wait([gstack[name] for name, _, _ in tensors], ssem, rsem, token, [i for _, i, _ in tensors], shs,
                          f"pair_share_wait_{l}")
    for (name, _, _), g in zip(tensors, gs):
        gstack[name] = g


def _pack(parts):
    out = []
    for p in parts:
        p2 = p.reshape(-1, 128)
        pad = (-p2.shape[0]) % 8
        out.append(jnp.pad(p2, ((0, pad), (0, 0))) if pad else p2)
    return jnp.concatenate(out, axis=0)


def _unpack(packed, like):
    out, at = [], 0
    for p in like:
        n = p.size // 128
        out.append(packed[at:at + n].reshape(p.shape))
        at += n + ((-n) % 8)
    return out


def kernel(x, mem, g_mix, g_ffn, w_in_a, g_v_a, w_spatial, b_spatial, w_in_b, g_q_b, g_k_b, g_mem, w_mem_kv, g_mq, g_mk, w_out, w_gate_up, w_down, loss_target, m_g_mix, m_g_ffn, m_w_in_a, m_g_v_a, m_w_spatial, m_b_spatial, m_w_in_b, m_g_q_b, m_g_k_b, m_g_mem, m_w_mem_kv, m_g_mq, m_g_mk, m_w_out, m_w_gate_up, m_w_down, v_g_mix, v_g_ffn, v_w_in_a, v_g_v_a, v_w_spatial, v_b_spatial, v_w_in_b, v_g_q_b, v_g_k_b, v_g_mem, v_w_mem_kv, v_g_mq, v_g_mk, v_w_out, v_w_gate_up, v_w_down):
    xs = x[0]
    mem2 = mem[0]
    target = loss_target[0]
    T, D = xs.shape
    depth = g_mix.shape[0]
    MEMW = w_mem_kv.shape[2] // 2
    TOK = D - MEMW
    KV = (w_in_b.shape[2] * 4 - TOK - MEMW) // 2
    QPK = TOK // KV
    F = w_gate_up.shape[2] * 4 // 2

    idx = ((2 * lax.axis_index("x") + lax.axis_index("y")).astype(jnp.int32).reshape(1), lax.axis_index("c").astype(jnp.int32).reshape(1))

    big = {
        "w_in_a": (w_in_a, _Shard(D, w_in_a.shape[2] * 4, True)),
        "w_in_b": (w_in_b, _Shard(D, w_in_b.shape[2] * 4, True)),
        "w_mem_kv": (w_mem_kv, _Shard(D, 2 * MEMW, False)),
        "w_out": (w_out, _Shard(D, D, False)),
        "w_gate_up": (w_gate_up, _Shard(D, 2 * F, True)),
        "w_down": (w_down, _Shard(F, D, False)),
    }

    def layer_tensors(l):
        n_in = "w_in_a" if l % 2 == 0 else "w_in_b"
        return [(n_in, l // 2, big[n_in][0].shape[0])] + [(n, l, depth) for n in ("w_mem_kv", "w_out", "w_gate_up", "w_down")]

    def layer_shards(l):
        return [big[n][1] for n, _, _ in layer_tensors(l)]

    full = {n: [None] * w.shape[0] for n, (w, _) in big.items()}
    flying = {}

    def start_layer(l, dep):
        tens = layer_tensors(l)
        token = dep
        for gi, group in enumerate([tens[:2], tens[2:3], tens[3:4], tens[4:]]):
            shs = [big[n][1] for n, _, _ in group]
            bufs = [_cast_into_full(big[n][0], i, big[n][1], idx, dep=token) for n, i, _ in group]
            ssem, rsem, bufs, token = _gather_start(bufs, shs, f"gather_start_{l}_{gi}")
            for n, i, _ in group:
                flying[(n, i)] = dict(group=group, shs=shs, state=(ssem, rsem, bufs), name=f"{l}_{gi}", passing=False)
        return token

    def land(members, after):
        for n, i in members:
            fl = flying[(n, i)]
            ssem, rsem, bufs = fl["state"]
            bufs = _gather_wait(bufs, ssem, rsem, after, fl["shs"], "gather_wait_" + fl["name"])
            ssem, rsem, bufs, after = _pass_on_start(bufs, fl["shs"], "pass_on_start_" + fl["name"])
            fl.update(state=(ssem, rsem, bufs), passing=True)
        return after

    def weight(n, i, after):
        if full[n][i] is None:
            fl = flying[(n, i)]
            ssem, rsem, bufs = fl["state"]
            if fl["passing"]:
                bufs = _pass_on_wait(bufs, ssem, rsem, after, fl["shs"], "pass_on_wait_" + fl["name"])
            else:
                bufs = _gather_wait(bufs, ssem, rsem, after, fl["shs"], "gather_wait_" + fl["name"])
                bufs = _gather_pass_on(bufs, fl["shs"], "gather_pass_on")
            for (m, j, _), b in zip(fl["group"], bufs):
                full[m][j] = b
        return full[n][i]

    after = None
    for l in range(depth):
        after = start_layer(l, after)
    tabs = _rope_tables(T)

    saved = []
    xc = xs
    for l in range(depth):
        is_a = l % 2 == 0
        li = l // 2
        w_in = weight("w_in_a" if is_a else "w_in_b", li, after)
        h, ht, r1 = _rmsnorm_fwd(xc, g_mix[l], "rmsnorm_fwd", transposed=True)
        z = _mm_nn("mm_in", h, w_in, pm=2048, pn=512)
        st = dict(x=xc, ht=ht, r1=r1, z=z)
        if is_a:
            ws_m = w_spatial[li].astype(_MXU)
            st["ws_m"], st["wst_m"], st["b_t"] = ws_m, jnp.swapaxes(ws_m, 1, 2), b_spatial[li].T
            tok = _mixer_a_fwd(z, g_v_a[li], ws_m, st["b_t"], TOK, D)
            qblk = 2 * TOK // MEMW
        else:
            q, k, v = _qk_rope_fwd(z, g_q_b[li], g_k_b[li], tabs, TOK, KV)
            tok, stat = _attn_fwd(q, k, v, QPK, D)
            st["q"], st["k"], st["v"], st["stat"] = q, k, v, stat
            qblk = (TOK + 2 * KV) // MEMW
        mem_n = _rmsnorm_fwd(mem2, g_mem[l], "rmsnorm_mem")
        kv = _mm_nn("mm_memkv", mem_n, weight("w_mem_kv", l, z))
        cat = _mem_fwd(z, qblk, kv, g_mq[l], g_mk[l], MEMW, tok)
        token = land([("w_out", l), ("w_gate_up", l), ("w_down", l)], cat) if l > 0 else None
        x1 = _mm_nn("mm_out", cat, weight("w_out", l, cat), add=xc, dep=token)
        h2, h2t, r2 = _rmsnorm_fwd(x1, g_ffn[l], "rmsnorm_fwd", transposed=True)
        st["r2"] = r2
        act, gu = _ffn_gate_up(h2, weight("w_gate_up", l, h2))
        w_down_l = weight("w_down", l, act)
        token = land([layer_tensors(l + 1)[0][:2]], act) if l + 1 < depth else None
        xc = _mm_nn("mm_down", act, w_down_l, add=x1, pm=512, pn=1024, pk=8192, dep=token)
        after = xc
        st.update(mem_n=mem_n, kv=kv, qblk=qblk, cat=cat, x1=x1, h2t=h2t, act=act, gu=gu)
        saved.append(st)

    dx, dxm, sq = _loss_head(xc, target)
    loss = lax.psum(sq[0, 0] * (0.5 / D), ("x", "y", "c"))

    gsm = {n: [None] * len(a) for n, a in dict(g_mix=g_mix, g_ffn=g_ffn, g_v_a=g_v_a, w_spatial=w_spatial, b_spatial=b_spatial,
                                                g_q_b=g_q_b, g_k_b=g_k_b, g_mem=g_mem, g_mq=g_mq, g_mk=g_mk).items()}
    gstack = {n: None for n in big}
    pairing, chipping, token = None, None, None
    shares = []

    def advance(after):
        nonlocal pairing, chipping
        state, tok = _reduce_scatter_middle(pairing[0], after, layer_shards(pairing[1]), idx, pairing[1])
        if chipping is not None:
            shares.append(_reduce_scatter_end(chipping[0], tok, layer_tensors(chipping[1]), layer_shards(chipping[1]), gstack, idx,
                                              chipping[1]))
        pairing, chipping = None, (state, pairing[1])
        return tok

    for l in reversed(range(depth)):
        st = saved[l]
        is_a = l % 2 == 0
        li = l // 2
        gbig = {}
        dgu = _ffn_dact(dxm, full["w_down"][l], st["gu"], dep=token)
        token = advance(dgu) if pairing is not None else None
        gbig["w_down"] = _mm_tn_dual("mm_dw_down", st["act"], dxm, pm=512, pn=1024, pk=4096)
        dh2 = _ffn_dh(dgu, full["w_gate_up"][l], dep=token)
        gbig["w_gate_up"] = _ffn_dwgu(st["h2t"], dgu)
        dx, dxm, dg = _rmsnorm_bwd(st["x1"], g_ffn[l], dh2, dx, "rmsnorm_bwd", rstd=st["r2"])
        gsm["g_ffn"][l] = dg[0]
        dcat = _mm_nt("mm_dcat", dxm, full["w_out"][l])
        gbig["w_out"] = _mm_tn_dual("mm_dw_out", st["cat"], dxm, pm=512, pn=1024, pk=4096)
        if is_a:
            dz, dws, dbs, dgv = _mixer_a_bwd(st["z"], dcat, g_v_a[li], st["ws_m"], st["wst_m"], st["b_t"], TOK)
            gsm["w_spatial"][li], gsm["b_spatial"][li], gsm["g_v_a"][li] = dws, dbs[:, :, 0], dgv[0]
        else:
            dq, dk, dz = _attn_bwd(st["q"], st["k"], st["v"], dcat, st["cat"], st["stat"], QPK, st["z"].shape[1])
            dz, dgq_b, dgk_b = _qk_rope_bwd(st["z"], dq, dk, g_q_b[li], g_k_b[li], tabs, TOK, KV, dz)
            gsm["g_q_b"][li], gsm["g_k_b"][li] = dgq_b[0], dgk_b[0]
        dz, dkn, dvm, dgq = _mem_bwd(st["z"], st["qblk"], st["kv"], g_mq[l], g_mk[l], dcat, TOK // MEMW, MEMW, dz)
        dkv, dgk = _memkv_bwd(st["kv"], dkn, dvm, g_mk[l], MEMW)
        gsm["g_mq"][l], gsm["g_mk"][l] = dgq[0], dgk[0]
        gbig["w_mem_kv"] = _mm_tn_dual("mm_dw_memkv", st["mem_n"], dkv)
        dmem_n = _mm_nt("mm_dmemn", dkv, full["w_mem_kv"][l])
        gsm["g_mem"][l] = _rmsnorm_bwd(mem2, g_mem[l], dmem_n, None, "rmsnorm_bwd_mem")[2][0]
        n_in = "w_in_a" if is_a else "w_in_b"
        dh = _mm_nt("mm_dh", dz, full[n_in][li])
        gbig[n_in] = _mm_nn_dual("mm_dw_in", st["ht"], dz, pm=1024, pn=512, pk=4096)
        dx, dxm, dg = _rmsnorm_bwd(st["x"], g_mix[l], dh, dx, "rmsnorm_bwd", rstd=st["r1"])
        gsm["g_mix"][l] = dg[0]
        state, token = _reduce_scatter_begin([gbig[n] for n, _, _ in layer_tensors(l)], layer_shards(l), l)
        pairing = (state, l)

    small = ["g_mix", "g_ffn", "g_v_a", "w_spatial", "b_spatial", "g_q_b", "g_k_b", "g_mem", "g_mq", "g_mk"]
    env = dict(g_mix=g_mix, g_ffn=g_ffn, g_v_a=g_v_a, w_spatial=w_spatial, b_spatial=b_spatial, g_q_b=g_q_b, g_k_b=g_k_b,
               g_mem=g_mem, g_mq=g_mq, g_mk=g_mk,
               m_g_mix=m_g_mix, m_g_ffn=m_g_ffn, m_g_v_a=m_g_v_a, m_w_spatial=m_w_spatial, m_b_spatial=m_b_spatial,
               m_g_q_b=m_g_q_b, m_g_k_b=m_g_k_b, m_g_mem=m_g_mem, m_g_mq=m_g_mq, m_g_mk=m_g_mk,
               v_g_mix=v_g_mix, v_g_ffn=v_g_ffn, v_g_v_a=v_g_v_a, v_w_spatial=v_w_spatial, v_b_spatial=v_b_spatial,
               v_g_q_b=v_g_q_b, v_g_k_b=v_g_k_b, v_g_mem=v_g_mem, v_g_mq=v_g_mq, v_g_mk=v_g_mk,
               m_w_in_a=m_w_in_a, m_w_in_b=m_w_in_b, m_w_mem_kv=m_w_mem_kv, m_w_out=m_w_out, m_w_gate_up=m_w_gate_up, m_w_down=m_w_down,
               v_w_in_a=v_w_in_a, v_w_in_b=v_w_in_b, v_w_mem_kv=v_w_mem_kv, v_w_out=v_w_out, v_w_gate_up=v_w_gate_up, v_w_down=v_w_down)
    like = [env[n] for n in small]
    g_small = _all_reduce_small(_pack([jnp.stack(gsm[n]) for n in small]), dep=token)

    res = {}
    outs = _adamw(_pack(like)[None], g_small[None], _pack([env["m_" + n] for n in small])[None],
                  _pack([env["v_" + n] for n in small])[None], "adamw_small")
    unpacked = [_unpack(o[0], like) for o in outs]
    for k, n in enumerate(small):
        res[n] = [u[k] for u in unpacked]
    advance(outs[1])
    pending = chipping
    for share in shares:
        _reduce_scatter_shared(share, gstack)
    last = {n: i for n, i, _ in layer_tensors(pending[1])}
    early = {}
    for n, (w, _) in big.items():
        L = w.shape[0]
        if n not in last:
            res[n] = _adamw(w, gstack[n], env["m_" + n], env["v_" + n], "adamw_" + n)
        elif L > 1:
            assert last[n] == 0
            early[n] = _adamw(w, gstack[n], env["m_" + n], env["v_" + n], "adamw_early_" + n, l0=1)
    done = [o[1] for o in early.values()] + [res[n][1] for n in big if n in res] + [res[small[0]][1]]
    _reduce_scatter_shared(
        _reduce_scatter_end(pending[0], done, layer_tensors(pending[1]), layer_shards(pending[1]), gstack, idx, pending[1]), gstack)
    for n in last:
        res[n] = _adamw(big[n][0], gstack[n], env["m_" + n], env["v_" + n], "adamw_last_" + n, l0=0, l1=1, prev=early.get(n))

    order = ["g_mix", "g_ffn", "w_in_a", "g_v_a", "w_spatial", "b_spatial", "w_in_b", "g_q_b", "g_k_b", "g_mem", "w_mem_kv",
             "g_mq", "g_mk", "w_out", "w_gate_up", "w_down"]
    return (loss, dx.reshape(1, T, D), *[res[n][0] for n in order], *[res[n][1] for n in order],
            *[res[n][2] for n in order], *[res[n][3] for n in order])
```

```python
import jax
import jax.numpy as jnp
import numpy as np
from jax import lax
from jax.experimental import pallas as pl
from jax.experimental.pallas import tpu as pltpu

_F32 = jnp.float32
_MXU = jnp.bfloat16
_WIRE = jnp.bfloat16
_KW = {}

EPS = 1e-6
HEAD = 128
CHUNK = 128
GRID_W = 64
ROPE_THETA = 10000.0
ADAM_LR, ADAM_B1, ADAM_B2, ADAM_EPS, ADAM_WD, ADAM_STEP = 0.001, 0.9, 0.999, 1e-08, 0.01, 10
_SQRT_HALF = float(np.sqrt(0.5))
_INV_SQRT_2PI = float(1.0 / np.sqrt(2.0 * np.pi))
_VMEM_LIMIT = 56 * 1024 * 1024
_MESH = pl.DeviceIdType.MESH

_NN = (((1,), (0,)), ((), ()))
_NT = (((1,), (1,)), ((), ()))
_TN = (((0,), (0,)), ((), ()))

S = jax.ShapeDtypeStruct
BS = pl.BlockSpec
_ANY = pl.BlockSpec(memory_space=pl.ANY)


def _tile(n, pref, mult=128):
    if n <= pref:
        return n
    d = (pref // mult) * mult
    while d >= mult:
        if n % d == 0:
            return d
        d -= mult
    raise ValueError(f"no tile for {n} (pref {pref}, mult {mult})")


def _mo(v, m):
    return v if isinstance(v, int) else pl.multiple_of(v, m)


def _cp(*sem, **kw):
    return pltpu.CompilerParams(dimension_semantics=sem or None, vmem_limit_bytes=_VMEM_LIMIT, **kw)


def _call(body, name, **kw):
    return pl.pallas_call(body, name=name, **kw, **_KW)


def _dot(a, b, dn=_NN):
    return lax.dot_general(a, b, dn, preferred_element_type=_F32)


def _gelu(x):
    return 0.5 * x * (1.0 + lax.erf(x * _SQRT_HALF))


def _gelu_grad(x):
    return 0.5 * (1.0 + lax.erf(x * _SQRT_HALF)) + x * jnp.exp(-0.5 * x * x) * _INV_SQRT_2PI


def _rstd(x):
    return lax.rsqrt(jnp.mean(x * x, axis=-1, keepdims=True) + EPS)


def _norm_bwd(dout, xhat, r, g):
    dy = dout * g
    return r * (dy - xhat * jnp.mean(dy * xhat, axis=-1, keepdims=True))


def _softmax(s):
    e = jnp.exp(s - jnp.max(s, axis=-1, keepdims=True))
    return e * (1.0 / jnp.sum(e, axis=-1, keepdims=True))


def _mm(name, a, b, a_spec, b_spec, dn, grid, acc_shape, out_shape, out_specs, epilogue, extra=(), extra_specs=(), dep=None):
    nk = grid[2]
    n_ex = len(extra)
    deps = [] if dep is None else [dep]
    multi = isinstance(out_shape, (list, tuple))
    n_out = len(out_shape) if multi else 1

    def body(*refs):
        a_ref, b_ref = refs[0], refs[1]
        ex = refs[2:2 + n_ex]
        outs = refs[2 + n_ex + len(deps):2 + n_ex + len(deps) + n_out]

        def prod():
            return _dot(a_ref[...].astype(_MXU), b_ref[...].astype(_MXU), dn)

        if nk == 1:
            epilogue(prod(), ex, outs)
        else:
            acc = refs[-1]
            k = pl.program_id(2)

            @pl.when(k == 0)
            def _():
                acc[...] = jnp.zeros_like(acc)

            acc[...] += prod()

            @pl.when(k == nk - 1)
            def _():
                epilogue(acc[...], ex, outs)

    return _call(
        body, name, grid=grid, in_specs=[a_spec, b_spec, *extra_specs] + [_ANY] * len(deps), out_specs=out_specs, out_shape=out_shape,
        scratch_shapes=[] if nk == 1 else [pltpu.VMEM(acc_shape, _F32)],
        compiler_params=_cp("parallel", "parallel", "arbitrary"),
    )(a, b, *extra, *deps)


def _ep_store(acc, ex, outs):
    for o in outs:
        o[...] = acc.astype(o.dtype)


def _ep_add(acc, ex, outs):
    outs[0][...] = (acc + ex[0][...]).astype(outs[0].dtype)


def _mm_nn(name, a, b, out_dtype=_F32, add=None, pm=1024, pn=1024, pk=2048, dep=None):
    M, K = a.shape
    N = b.shape[1]
    tm, tn, tk = _tile(M, pm, 8), _tile(N, pn), _tile(K, pk)
    o_spec = BS((tm, tn), lambda i, j, k: (i, j))
    return _mm(name, a, b, BS((tm, tk), lambda i, j, k: (i, k)), BS((tk, tn), lambda i, j, k: (k, j)), _NN,
               (M // tm, N // tn, K // tk), (tm, tn), S((M, N), out_dtype), o_spec,
               _ep_store if add is None else _ep_add,
               extra=() if add is None else (add,), extra_specs=() if add is None else (o_spec,), dep=dep)


def _mm_nt(name, a, b, out_dtype=_F32, pm=1024, pn=1024, pk=4096):
    M, K = a.shape
    N = b.shape[0]
    tm, tn, tk = _tile(M, pm, 8), _tile(N, pn), _tile(K, pk)
    return _mm(name, a, b, BS((tm, tk), lambda i, j, k: (i, k)), BS((tn, tk), lambda i, j, k: (j, k)), _NT,
               (M // tm, N // tn, K // tk), (tm, tn), S((M, N), out_dtype), BS((tm, tn), lambda i, j, k: (i, j)), _ep_store)


def _mm_tn_dual(name, a, b, pm=1024, pn=1024, pk=2048, dep=None):
    K, M = a.shape
    N = b.shape[1]
    tm, tn, tk = _tile(M, pm), _tile(N, pn), _tile(K, pk, 16)
    o_spec = BS((tm, tn), lambda i, j, k: (i, j))
    return _mm(name, a, b, BS((tk, tm), lambda i, j, k: (k, i)), BS((tk, tn), lambda i, j, k: (k, j)), _TN,
               (M // tm, N // tn, K // tk), (tm, tn), [S((M, N), _F32), S((M, N), _WIRE)], [o_spec, o_spec], _ep_store, dep=dep)


def _ffn_gate_up(h2, wgu):
    T, D = h2.shape
    F = wgu.shape[1] // 2
    tm, tn = _tile(T, 1024, 8), _tile(F, 512)
    nj = F // tn

    def body(a_ref, bg_ref, bu_ref, act_ref, gu_ref):
        a = a_ref[...]
        g = _dot(a, bg_ref[...])
        u = _dot(a, bu_ref[...])
        sg = 1.0 / (1.0 + jnp.exp(-g))
        silu = g * sg
        gu_ref[0] = (u * (sg * (1.0 + g * (1.0 - sg)))).astype(gu_ref.dtype)
        gu_ref[1] = silu.astype(gu_ref.dtype)
        act_ref[...] = (silu * u).astype(act_ref.dtype)

    return _call(
        body, "ffn_gate_up", grid=(T // tm, nj),
        in_specs=[BS((tm, D), lambda i, j: (i, 0)), BS((D, tn), lambda i, j: (0, j)), BS((D, tn), lambda i, j: (0, j + nj))],
        out_specs=[BS((tm, tn), lambda i, j: (i, j)), BS((2, tm, tn), lambda i, j: (0, i, j))],
        out_shape=[S((T, F), _MXU), S((2, T, F), _MXU)],
        compiler_params=_cp("parallel", "parallel"),
    )(h2, wgu, wgu)


def _ffn_dact(dxm, wdown, gu, dep=None):
    T, D = dxm.shape
    F = wdown.shape[0]
    tm, tn = _tile(T, 2048, 8), _tile(F, 512)
    deps = [] if dep is None else [dep]

    def body(a_ref, b_ref, gu_ref, *rest):
        o_ref = rest[-1]
        d = _dot(a_ref[...], b_ref[...], _NT)
        o_ref[0] = (d * gu_ref[0].astype(_F32)).astype(o_ref.dtype)
        o_ref[1] = (d * gu_ref[1].astype(_F32)).astype(o_ref.dtype)

    return _call(
        body, "ffn_dact", grid=(T // tm, F // tn),
        in_specs=[BS((tm, D), lambda i, j: (i, 0)), BS((tn, D), lambda i, j: (j, 0)), BS((2, tm, tn), lambda i, j: (0, i, j))]
        + [_ANY] * len(deps),
        out_specs=BS((2, tm, tn), lambda i, j: (0, i, j)), out_shape=S((2, T, F), _MXU),
        compiler_params=_cp("parallel", "parallel"),
    )(dxm, wdown, gu, *deps)


def _ffn_dh(dgu, wgu, dep=None):
    _, T, F = dgu.shape
    D = wgu.shape[0]
    tm, tn, tk = _tile(T, 512, 8), _tile(D, 2048), _tile(F, 2816, 256)
    nkf = F // tk
    return _mm("ffn_dh", dgu, wgu, BS((None, tm, tk), lambda i, j, k: (k // nkf, i, k % nkf)),
               BS((tn, tk), lambda i, j, k: (j, k)), _NT, (T // tm, D // tn, 2 * nkf), (tm, tn),
               S((T, D), _F32), BS((tm, tn), lambda i, j, k: (i, j)), _ep_store, dep=dep)


def _ffn_dwgu(h2t, dgu):
    _, T, F = dgu.shape
    D = h2t.shape[0]
    tm, tn, tk = _tile(D, 1024), _tile(F, 512), _tile(T, 4096)
    njf = F // tn
    o_spec = BS((tm, tn), lambda i, j, k: (i, j))
    return _mm("ffn_dwgu", h2t, dgu, BS((tm, tk), lambda i, j, k: (i, k)),
               BS((None, tk, tn), lambda i, j, k: (j // njf, k, j % njf)), _NN, (D // tm, 2 * njf, T // tk), (tm, tn),
               [S((D, 2 * F), _F32), S((D, 2 * F), _WIRE)], [o_spec, o_spec], _ep_store)


def _mm_nn_dual(name, a, b, pm=1024, pn=1024, pk=2048):
    M, K = a.shape
    N = b.shape[1]
    tm, tn, tk = _tile(M, pm), _tile(N, pn), _tile(K, pk)
    o_spec = BS((tm, tn), lambda i, j, k: (i, j))
    return _mm(name, a, b, BS((tm, tk), lambda i, j, k: (i, k)), BS((tk, tn), lambda i, j, k: (k, j)), _NN,
               (M // tm, N // tn, K // tk), (tm, tn), [S((M, N), _F32), S((M, N), _WIRE)], [o_spec, o_spec], _ep_store)


def _rmsnorm_fwd(x, g, name, dep=None, transposed=False):
    T, D = x.shape
    tr = _tile(T, 512, 128)
    n_out = 3 if transposed else 1

    def body(x_ref, g_ref, *rest):
        outs = rest[-n_out:]
        xv = x_ref[...]
        r = _rstd(xv)
        h = (xv * r * g_ref[...]).astype(outs[0].dtype)
        outs[0][...] = h
        if transposed:
            outs[1][...] = h.T
            outs[2][...] = r

    row = BS((tr, D), lambda i: (i, 0))
    deps = [] if dep is None else [dep]
    return _call(body, name, grid=(T // tr,), in_specs=[row, BS((1, D), lambda i: (0, 0))] + [_ANY] * len(deps),
                 out_specs=[row, BS((D, tr), lambda i: (0, i)), BS((tr, 1), lambda i: (i, 0))] if transposed else row,
                 out_shape=[S((T, D), _MXU), S((D, T), _MXU), S((T, 1), _F32)] if transposed else S((T, D), _MXU),
                 compiler_params=_cp("parallel"))(x, g.reshape(1, D), *deps)


def _rmsnorm_bwd(x, g, dh, dres, name, rstd=None):
    T, D = x.shape
    tr = _tile(T, 512, 8)
    has_res = dres is not None
    has_r = rstd is not None

    def body(*refs):
        x_ref, g_ref, dh_ref = refs[:3]
        dx_ref, dxm_ref, dg_ref = refs[-3:]

        @pl.when(pl.program_id(0) == 0)
        def _():
            dg_ref[...] = jnp.zeros_like(dg_ref)

        xv = x_ref[...]
        r = refs[3 + has_res][...] if has_r else _rstd(xv)
        xhat = xv * r
        dh_v = dh_ref[...]
        dg_ref[...] += jnp.sum(dh_v * xhat, axis=0, keepdims=True)
        dx = _norm_bwd(dh_v, xhat, r, g_ref[...])
        if has_res:
            dx = dx + refs[3][...]
        dx_ref[...] = dx
        dxm_ref[...] = dx.astype(dxm_ref.dtype)

    row = BS((tr, D), lambda i: (i, 0))
    vec = BS((1, D), lambda i: (0, 0))
    extra = ([dres] if has_res else []) + ([rstd] if has_r else [])
    return _call(body, name, grid=(T // tr,),
                 in_specs=[row, vec, row] + ([row] if has_res else []) + ([BS((tr, 1), lambda i: (i, 0))] if has_r else []),
                 out_specs=[row, row, vec], out_shape=[S((T, D), _F32), S((T, D), _MXU), S((1, D), _F32)],
                 compiler_params=_cp("arbitrary"))(x, g.reshape(1, D), dh, *extra)


def _loss_head(y, target):
    T, D = y.shape
    tr = _tile(T, 512, 8)

    def body(y_ref, t_ref, dy_ref, dym_ref, acc_ref):
        @pl.when(pl.program_id(0) == 0)
        def _():
            acc_ref[...] = jnp.zeros_like(acc_ref)

        err = y_ref[...] - t_ref[...]
        acc_ref[...] += jnp.sum(jnp.sum(err * err, axis=-1, keepdims=True), axis=0, keepdims=True)
        dy = err * (1.0 / D)
        dy_ref[...] = dy
        dym_ref[...] = dy.astype(dym_ref.dtype)

    row = BS((tr, D), lambda i: (i, 0))
    return _call(body, "loss_head", grid=(T // tr,), in_specs=[row, row],
                 out_specs=[row, row, BS((1, 128), lambda i: (0, 0))],
                 out_shape=[S((T, D), _F32), S((T, D), _MXU), S((1, 128), _F32)],
                 compiler_params=_cp("arbitrary"))(y, target)


def _mixa_blocks(T):
    return next(cb for cb in (4, 2, 1) if T % (cb * CHUNK) == 0)


def _mixer_a_fwd(z, gv, ws_m, b_t, TOK, width):
    T = z.shape[0]
    G = TOK // HEAD
    CB = _mixa_blocks(T)
    R = CB * CHUNK

    def body(z_ref, gv_ref, ws_ref, bt_ref, o_ref):
        u = _gelu(z_ref[:, :TOK])
        v = _gelu(z_ref[:, TOK:])
        vn = (v * _rstd(v) * gv_ref[...]).astype(_MXU)
        for c in range(CB):
            rows = slice(c * CHUNK, (c + 1) * CHUNK)
            for g in range(G):
                cols = slice(g * HEAD, (g + 1) * HEAD)
                s = _dot(ws_ref[g], vn[rows, cols]) + bt_ref[:, g:g + 1]
                o_ref[rows, cols] = (u[rows, cols] * s).astype(o_ref.dtype)

    return _call(
        body, "mixer_a_fwd", grid=(T // R,),
        in_specs=[BS((R, 2 * TOK), lambda i: (i, 0)), BS((1, TOK), lambda i: (0, 0)),
                  BS((G, CHUNK, CHUNK), lambda i: (0, 0, 0)), BS((CHUNK, G), lambda i: (0, 0))],
        out_specs=BS((R, TOK), lambda i: (i, 0)), out_shape=S((T, width), _MXU), compiler_params=_cp("parallel"),
    )(z, gv.reshape(1, TOK), ws_m, b_t)


def _mixer_a_bwd(z, dcat, gv, ws_m, wst_m, b_t, TOK):
    T = z.shape[0]
    G = TOK // HEAD
    CB = _mixa_blocks(T)
    R = CB * CHUNK
    n = T // R

    def body(z_ref, d_ref, gv_ref, ws_ref, wst_ref, bt_ref, dz_ref, dws_ref, db_ref, dgv_ref, dvn_scr):
        i = pl.program_id(0)

        @pl.when(i == 0)
        def _():
            dws_ref[...] = jnp.zeros_like(dws_ref)
            db_ref[...] = jnp.zeros_like(db_ref)
            dgv_ref[...] = jnp.zeros_like(dgv_ref)

        zu = z_ref[:, :TOK]
        zv = z_ref[:, TOK:]
        u = _gelu(zu)
        v = _gelu(zv)
        r = _rstd(v)
        vhat = v * r
        gvv = gv_ref[...]
        vn = (vhat * gvv).astype(_MXU)
        d = d_ref[...]
        gpu = _gelu_grad(zu)
        for c in range(CB):
            rows = slice(c * CHUNK, (c + 1) * CHUNK)
            for g in range(G):
                cols = slice(g * HEAD, (g + 1) * HEAD)
                vn_cg = vn[rows, cols]
                s = _dot(ws_ref[g], vn_cg) + bt_ref[:, g:g + 1]
                d_cg = d[rows, cols]
                dz_ref[rows, cols] = (d_cg * s * gpu[rows, cols]).astype(dz_ref.dtype)
                ds = d_cg * u[rows, cols]
                ds_m = ds.astype(_MXU)
                dvn_scr[rows, cols] = _dot(wst_ref[g], ds_m)
                dws_ref[g] += _dot(ds_m, vn_cg, _NT)
                db_ref[g] += ds
        dvn = dvn_scr[...]
        dgv_ref[...] += jnp.sum(dvn * vhat, axis=0, keepdims=True)
        dv = _norm_bwd(dvn, vhat, r, gvv)
        dz_ref[:, TOK:] = (dv * _gelu_grad(zv)).astype(dz_ref.dtype)

        @pl.when(i == n - 1)
        def _():
            for g in range(G):
                db_ref[g] = jnp.broadcast_to(jnp.sum(db_ref[g], axis=1, keepdims=True), (CHUNK, CHUNK))

    full3 = BS((G, CHUNK, CHUNK), lambda i: (0, 0, 0))
    return _call(
        body, "mixer_a_bwd", grid=(n,),
        in_specs=[BS((R, 2 * TOK), lambda i: (i, 0)), BS((R, TOK), lambda i: (i, 0)), BS((1, TOK), lambda i: (0, 0)),
                  full3, full3, BS((CHUNK, G), lambda i: (0, 0))],
        out_specs=[BS((R, 2 * TOK), lambda i: (i, 0)), full3, full3, BS((1, TOK), lambda i: (0, 0))],
        out_shape=[S((T, z.shape[1]), _MXU), S((G, CHUNK, CHUNK), _F32), S((G, CHUNK, CHUNK), _F32), S((1, TOK), _F32)],
        scratch_shapes=[pltpu.VMEM((R, TOK), _F32)], compiler_params=_cp("arbitrary"),
    )(z, dcat, gv.reshape(1, TOK), ws_m, wst_m, b_t)


def _rope_tables(T):
    n_rows = T // GRID_W
    rows = jnp.broadcast_to(jnp.arange(n_rows)[:, None], (n_rows, GRID_W)).reshape(T)
    cols = jnp.broadcast_to(jnp.arange(GRID_W)[None, :], (n_rows, GRID_W)).reshape(T)
    pairs = HEAD // 4
    freqs = ROPE_THETA ** (-jnp.arange(pairs, dtype=_F32) / pairs)
    ang_r = rows.astype(_F32)[:, None] * freqs
    ang_c = cols.astype(_F32)[:, None] * freqs
    ang = jnp.concatenate([ang_r, ang_r, ang_c, ang_c], axis=-1)
    cos, sin = jnp.cos(ang), jnp.sin(ang)
    first = (jnp.arange(HEAD) % (HEAD // 2)) < (HEAD // 4)
    return cos, jnp.where(first, -sin, 0.0), jnp.where(first, 0.0, sin)


def _rope(x, cs, sa, sb):
    return x * cs + pltpu.roll(x, 96, 1) * sa + pltpu.roll(x, 32, 1) * sb


def _qk_rope_fwd(z, gq, gk, tabs, TOK, KV):
    T = z.shape[0]
    R = _tile(T, 512, 8)
    W = TOK + 2 * KV

    def body(z_ref, gq_ref, gk_ref, cos_ref, sa_ref, sb_ref, q_ref, k_ref, v_ref):
        cs, sa, sb = cos_ref[...], sa_ref[...], sb_ref[...]
        for h in range((TOK + KV) // HEAD):
            cols = slice(h * HEAD, (h + 1) * HEAD)
            xv = z_ref[:, cols]
            xn = xv * _rstd(xv) * (gq_ref[...] if h < TOK // HEAD else gk_ref[...])
            out = _rope(xn, cs, sa, sb)
            if h < TOK // HEAD:
                q_ref[:, cols] = out.astype(q_ref.dtype)
            else:
                k_ref[:, h * HEAD - TOK:(h + 1) * HEAD - TOK] = out.astype(k_ref.dtype)
        v_ref[...] = z_ref[:, TOK + KV:].astype(v_ref.dtype)

    vec = BS((1, HEAD), lambda i: (0, 0))
    tab = BS((R, HEAD), lambda i: (i, 0))
    return _call(
        body, "qk_rope_fwd", grid=(T // R,), in_specs=[BS((R, W), lambda i: (i, 0)), vec, vec, tab, tab, tab],
        out_specs=[BS((R, TOK), lambda i: (i, 0)), BS((R, KV), lambda i: (i, 0)), BS((R, KV), lambda i: (i, 0))],
        out_shape=[S((T, TOK), _MXU), S((T, KV), _MXU), S((T, KV), _MXU)], compiler_params=_cp("parallel"),
    )(z, gq.reshape(1, HEAD), gk.reshape(1, HEAD), *tabs)


def _qk_rope_bwd(z, dq, dk, gq, gk, tabs, TOK, KV, into):
    T = z.shape[0]
    R = _tile(T, 512, 8)
    W = TOK + KV

    def body(z_ref, dq_ref, dk_ref, gq_ref, gk_ref, cos_ref, sa_ref, sb_ref, into_ref, dz_ref, dgq_ref, dgk_ref):
        @pl.when(pl.program_id(0) == 0)
        def _():
            dgq_ref[...] = jnp.zeros_like(dgq_ref)
            dgk_ref[...] = jnp.zeros_like(dgk_ref)

        cs, sa, sb = cos_ref[...], sa_ref[...], sb_ref[...]
        for h in range(W // HEAD):
            cols = slice(h * HEAD, (h + 1) * HEAD)
            is_q = h < TOK // HEAD
            do = dq_ref[:, cols] if is_q else dk_ref[:, h * HEAD - TOK:(h + 1) * HEAD - TOK]
            dxn = do * cs - pltpu.roll(do, 96, 1) * sa - pltpu.roll(do, 32, 1) * sb
            xv = z_ref[:, cols]
            r = _rstd(xv)
            xhat = xv * r
            dg_ref = dgq_ref if is_q else dgk_ref
            dg_ref[...] += jnp.sum(dxn * xhat, axis=0, keepdims=True)
            dz_ref[:, cols] = _norm_bwd(dxn, xhat, r, gq_ref[...] if is_q else gk_ref[...]).astype(dz_ref.dtype)

    vec = BS((1, HEAD), lambda i: (0, 0))
    tab = BS((R, HEAD), lambda i: (i, 0))
    return _call(
        body, "qk_rope_bwd", grid=(T // R,),
        in_specs=[BS((R, W), lambda i: (i, 0)), BS((R, TOK), lambda i: (i, 0)), BS((R, KV), lambda i: (i, 0)), vec, vec, tab, tab, tab,
                  _ANY],
        out_specs=[BS((R, W), lambda i: (i, 0)), vec, vec],
        out_shape=[S(into.shape, into.dtype), S((1, HEAD), _F32), S((1, HEAD), _F32)], input_output_aliases={8: 0},
        compiler_params=_cp("arbitrary"),
    )(z, dq, dk, gq.reshape(1, HEAD), gk.reshape(1, HEAD), *tabs, into)


_ATTN_C2 = float(HEAD ** -0.5 * np.log2(np.e))


def _attn_fwd(q, k, v, QPK, width):
    T, TOK = q.shape
    KVH = k.shape[1] // HEAD
    tq = _tile(T, 1024, 8)
    ts = 256 if tq % 256 == 0 else tq
    W = QPK * HEAD

    def body(q_ref, k_ref, v_ref, o_ref, st_ref, vaug):
        @pl.when(pl.program_id(1) == 0)
        def _():
            vaug[:, :HEAD] = v_ref[...]
            vaug[:, HEAD:] = jnp.ones((T, HEAD), vaug.dtype)

        kk, va = k_ref[...], vaug[...]
        for g in range(QPK):
            cols = slice(g * HEAD, (g + 1) * HEAD)
            for r in range(tq // ts):
                rows = slice(r * ts, (r + 1) * ts)
                s = _dot(q_ref[rows, cols], kk, _NT)
                m = jnp.max(s, axis=-1, keepdims=True)
                ov = _dot(jnp.exp2((s - m) * _ATTN_C2).astype(_MXU), va)
                l = ov[:, HEAD:HEAD + 1]
                o_ref[rows, cols] = (ov[:, :HEAD] * (1.0 / l)).astype(o_ref.dtype)
                st_ref[rows, g:g + 1] = m + jnp.log2(l) * (1.0 / _ATTN_C2)

    qs = BS((tq, W), lambda h, i: (i, h))
    ks = BS((T, HEAD), lambda h, i: (0, h))
    return _call(body, "attn_fwd", grid=(KVH, T // tq), in_specs=[qs, ks, ks],
                 out_specs=[qs, BS((None, tq, QPK), lambda h, i: (h, i, 0))],
                 out_shape=[S((T, width), _MXU), S((KVH, T, QPK), _F32)],
                 scratch_shapes=[pltpu.VMEM((T, 2 * HEAD), _MXU)],
                 compiler_params=_cp("parallel", "arbitrary"))(q, k, v)


def _attn_bwd(q, k, v, dcat, o, stat, QPK, width):
    T, TOK = q.shape
    KV = k.shape[1]
    KVH = KV // HEAD
    tq = _tile(T, 1024, 8)
    ts = 256 if tq % 256 == 0 else tq
    nq = T // tq
    W = QPK * HEAD
    scale = HEAD ** -0.5

    def body(q_ref, k_ref, v_ref, do_ref, o_ref, st_ref, dq_ref, dk_ref, dv_ref, dk_acc, dv_acc, ds_all, p_all, q_all, do_all):
        i = pl.program_id(1)

        @pl.when(i == 0)
        def _():
            dk_acc[...] = jnp.zeros_like(dk_acc)
            dv_acc[...] = jnp.zeros_like(dv_acc)

        kk, vv = k_ref[...], v_ref[...]
        for r in range(tq // ts):
            rows = slice(r * ts, (r + 1) * ts)
            for g in range(QPK):
                cols = slice(g * HEAD, (g + 1) * HEAD)
                stack = slice(g * ts, (g + 1) * ts)
                qg = q_ref[rows, cols]
                p = jnp.exp2((_dot(qg, kk, _NT) - st_ref[rows, g:g + 1]) * _ATTN_C2)
                do32 = do_ref[rows, cols]
                do = do32.astype(_MXU)
                delta = jnp.sum(do32 * o_ref[rows, cols].astype(_F32), axis=-1, keepdims=True)
                ds = (p * (_dot(do, vv, _NT) - delta)).astype(_MXU)
                dq_ref[rows, cols] = _dot(ds, kk) * scale
                ds_all[stack, :] = ds
                p_all[stack, :] = p.astype(_MXU)
                q_all[stack, :] = qg
                do_all[stack, :] = do
            dk_acc[...] += _dot(ds_all[...], q_all[...], _TN)
            dv_acc[...] += _dot(p_all[...], do_all[...], _TN)

        @pl.when(i == nq - 1)
        def _():
            dk_ref[...] = dk_acc[...] * scale
            dv_ref[...] = dv_acc[...].astype(dv_ref.dtype)

    qs = BS((tq, W), lambda h, i: (i, h))
    ks = BS((T, HEAD), lambda h, i: (0, h))
    return _call(
        body, "attn_bwd", grid=(KVH, nq), in_specs=[qs, ks, ks, qs, qs, BS((None, tq, QPK), lambda h, i: (h, i, 0))],
        out_specs=[qs, ks, BS((T, HEAD), lambda h, i: (0, (TOK + KV) // HEAD + h))],
        out_shape=[S((T, TOK), _F32), S((T, KV), _F32), S((T, width), _MXU)],
        scratch_shapes=[pltpu.VMEM((T, HEAD), _F32), pltpu.VMEM((T, HEAD), _F32), pltpu.VMEM((QPK * ts, T), _MXU),
                        pltpu.VMEM((QPK * ts, T), _MXU), pltpu.VMEM((QPK * ts, HEAD), _MXU), pltpu.VMEM((QPK * ts, HEAD), _MXU)],
        compiler_params=_cp("parallel", "arbitrary"),
    )(q, k, v, dcat, o, stat)


def _mem_fwd(z, qblk, kv, gmq, gmk, MEMW, into):
    T = z.shape[0]
    NM = kv.shape[0]
    tq = _tile(T, 512, 8)
    scale = HEAD ** -0.5
    oblk = into.shape[1] // MEMW - 1

    def body(q_ref, kv_ref, gq_ref, gk_ref, into_ref, o_ref):
        for h in range(MEMW // HEAD):
            cols = slice(h * HEAD, (h + 1) * HEAD)
            kx = kv_ref[:, cols]
            kn = (kx * _rstd(kx) * gk_ref[...]).astype(_MXU)
            vv = kv_ref[:, MEMW + h * HEAD:MEMW + (h + 1) * HEAD].astype(_MXU)
            qx = q_ref[:, cols]
            qn = (qx * _rstd(qx) * gq_ref[...]).astype(_MXU)
            p = _softmax(_dot(qn, kn, _NT) * scale)
            o_ref[:, cols] = _dot(p.astype(_MXU), vv).astype(o_ref.dtype)

    vec = BS((1, HEAD), lambda i: (0, 0))
    return _call(
        body, "mem_fwd", grid=(T // tq,),
        in_specs=[BS((tq, MEMW), lambda i: (i, qblk)), BS((NM, 2 * MEMW), lambda i: (0, 0)), vec, vec, _ANY],
        out_specs=BS((tq, MEMW), lambda i: (i, oblk)), out_shape=S(into.shape, into.dtype), input_output_aliases={4: 0},
        compiler_params=_cp("parallel"),
    )(z, kv, gmq.reshape(1, HEAD), gmk.reshape(1, HEAD), into)


def _mem_bwd(z, qblk, kv, gmq, gmk, dcat, dblk, MEMW, into):
    T = z.shape[0]
    NM = kv.shape[0]
    tq = _tile(T, 512, 8)
    scale = HEAD ** -0.5

    def body(q_ref, kv_ref, gq_ref, gk_ref, do_ref, into_ref, dz_ref, dkn_ref, dv_ref, dgq_ref):
        @pl.when(pl.program_id(0) == 0)
        def _():
            dkn_ref[...] = jnp.zeros_like(dkn_ref)
            dv_ref[...] = jnp.zeros_like(dv_ref)
            dgq_ref[...] = jnp.zeros_like(dgq_ref)

        for h in range(MEMW // HEAD):
            cols = slice(h * HEAD, (h + 1) * HEAD)
            kx = kv_ref[:, cols]
            kn = (kx * _rstd(kx) * gk_ref[...]).astype(_MXU)
            vv = kv_ref[:, MEMW + h * HEAD:MEMW + (h + 1) * HEAD].astype(_MXU)
            qx = q_ref[:, cols]
            rq = _rstd(qx)
            qhat = qx * rq
            qn = (qhat * gq_ref[...]).astype(_MXU)
            p = _softmax(_dot(qn, kn, _NT) * scale)
            do = do_ref[:, cols].astype(_MXU)
            dp = _dot(do, vv, _NT)
            ds = (p * (dp - jnp.sum(p * dp, axis=-1, keepdims=True)) * scale).astype(_MXU)
            dqn = _dot(ds, kn)
            dkn_ref[:, cols] += _dot(ds, qn, _TN)
            dv_ref[:, cols] += _dot(p.astype(_MXU), do, _TN)
            dgq_ref[...] += jnp.sum(dqn * qhat, axis=0, keepdims=True)
            dz_ref[:, cols] = _norm_bwd(dqn, qhat, rq, gq_ref[...]).astype(dz_ref.dtype)

    vec = BS((1, HEAD), lambda i: (0, 0))
    kvs = BS((NM, MEMW), lambda i: (0, 0))
    return _call(
        body, "mem_bwd", grid=(T // tq,),
        in_specs=[BS((tq, MEMW), lambda i: (i, qblk)), BS((NM, 2 * MEMW), lambda i: (0, 0)), vec, vec,
                  BS((tq, MEMW), lambda i: (i, dblk)), _ANY],
        out_specs=[BS((tq, MEMW), lambda i: (i, qblk)), kvs, kvs, vec],
        out_shape=[S(into.shape, into.dtype), S((NM, MEMW), _F32), S((NM, MEMW), _F32), S((1, HEAD), _F32)],
        input_output_aliases={5: 0}, compiler_params=_cp("arbitrary"),
    )(z, kv, gmq.reshape(1, HEAD), gmk.reshape(1, HEAD), dcat, into)


def _memkv_bwd(kv, dkn, dv, gmk, MEMW):
    NM = kv.shape[0]

    def body(kv_ref, dkn_ref, dv_ref, gk_ref, dkv_ref, dgk_ref):
        dgk = jnp.zeros((1, HEAD), _F32)
        for h in range(MEMW // HEAD):
            cols = slice(h * HEAD, (h + 1) * HEAD)
            kx = kv_ref[:, cols]
            r = _rstd(kx)
            khat = kx * r
            dkn = dkn_ref[:, cols]
            dgk = dgk + jnp.sum(dkn * khat, axis=0, keepdims=True)
            dkv_ref[:, cols] = _norm_bwd(dkn, khat, r, gk_ref[...]).astype(dkv_ref.dtype)
        dgk_ref[...] = dgk
        dkv_ref[:, MEMW:] = dv_ref[...].astype(dkv_ref.dtype)

    return _call(body, "memkv_bwd", out_shape=[S((NM, 2 * MEMW), _MXU), S((1, HEAD), _F32)],
                 compiler_params=_cp())(kv, dkn, dv, gmk.reshape(1, HEAD))


def _cast_into_full(w, l, sh, idx, dep=None):
    tr, tc = _tile(sh.Rs, 512, 16), _tile(sh.Cs, 2048)
    nr, nc = sh.Rs // tr, sh.Cs // tc
    deps = [] if dep is None else [dep]

    def body(i_ref, c_ref, w_ref, *rest):
        rest[-1][...] = w_ref[...].astype(rest[-1].dtype)

    if sh.by_cols:
        o_map = lambda a, b, si, sc: (a, si[0] * nc + b)
    else:
        o_map = lambda a, b, si, sc: (si[0] * nr + a, b)
    return _call(
        body, "cast_into_full",
        grid_spec=pltpu.PrefetchScalarGridSpec(
            num_scalar_prefetch=2, grid=(nr, nc),
            in_specs=[BS((None, tr, tc), lambda a, b, si, sc: (l, a, b))] + [_ANY] * len(deps), out_specs=BS((tr, tc), o_map)),
        out_shape=S((sh.R, sh.C), _WIRE), compiler_params=_cp("parallel", "parallel"),
    )(*idx, w, *deps)


def _adamw(w, g, m, v, name, l0=0, l1=None, prev=None):
    L, R, C = w.shape
    l1 = L if l1 is None else l1
    tc = _tile(C, 2048)
    tr = _tile(R, max(8, (512 * 1024) // tc), 8)
    c_m = 1.0 / (1.0 - ADAM_B1 ** ADAM_STEP)
    c_v = 1.0 / (1.0 - ADAM_B2 ** ADAM_STEP)

    def body(w_ref, g_ref, m_ref, v_ref, *rest):
        go_ref, d_ref, mo_ref, vo_ref = rest[-4:]
        gv = g_ref[...]
        mn = ADAM_B1 * m_ref[...] + (1.0 - ADAM_B1) * gv
        vn = ADAM_B2 * v_ref[...] + (1.0 - ADAM_B2) * (gv * gv)
        go_ref[...] = gv
        mo_ref[...] = mn
        vo_ref[...] = vn
        d_ref[...] = -ADAM_LR * ((mn * c_m) / (jnp.sqrt(vn * c_v) + ADAM_EPS) + ADAM_WD * w_ref[...])

    blk = BS((None, tr, tc), lambda a, i, j: (l0 + a, i, j))
    prevs = [] if prev is None else list(prev)
    return _call(body, name, grid=(l1 - l0, R // tr, C // tc), in_specs=[blk] * 4 + [_ANY] * len(prevs), out_specs=[blk] * 4,
                 out_shape=[S((L, R, C), _F32)] * 4, input_output_aliases={4 + k: k for k in range(len(prevs))},
                 compiler_params=_cp("parallel", "parallel", "parallel"))(w, g, m, v, *prevs)


def _where_am_i():
    x, y, c = lax.axis_index("x"), lax.axis_index("y"), lax.axis_index("c")
    chips = [(1 - x, y), (x, 1 - y), (1 - x, 1 - y)]
    return x, y, c, 2 * x + y, chips, [2 * cx + cy for cx, cy in chips]


class _Shard:
    def __init__(self, R, C, by_cols):
        self.R, self.C, self.by_cols = R, C, by_cols
        self.Rs, self.Cs = (R, C // 4) if by_cols else (R // 4, C)
        self.Rh = self.Rs // 2
        self.Q = R // 8

    def full_piece(self, ref, j, cc):
        if self.by_cols:
            return ref.at[pl.ds(cc * self.Rh, self.Rh), pl.ds(_mo(j * self.Cs, 128), self.Cs)]
        return ref.at[pl.ds(_mo(j * self.Rs + cc * self.Rh, 16), self.Rh), :]

    def full_shard(self, ref, j):
        if self.by_cols:
            return ref.at[:, pl.ds(_mo(j * self.Cs, 128), self.Cs)]
        return ref.at[pl.ds(_mo(j * self.Rs, 16), self.Rs), :]

    def shard_half(self, ref, cc):
        return ref.at[pl.ds(_mo(cc * self.Rh, 16), self.Rh), :]

    def half_piece(self, ref, j):
        if self.by_cols:
            return ref.at[:, pl.ds(_mo(j * self.Cs, 128), self.Cs)]
        return ref.at[pl.ds(_mo(j * self.Rh, 16), self.Rh), :]


def _remote(src, dst, ssem, rsem, dev):
    return pltpu.make_async_remote_copy(src_ref=src, dst_ref=dst, send_sem=ssem, recv_sem=rsem, device_id=dev, device_id_type=_MESH)


_HBM = pl.BlockSpec(memory_space=pltpu.HBM)
_SEM = pl.BlockSpec(memory_space=pltpu.SEMAPHORE)
_EFFECT = pltpu.SideEffectType.DATAFLOW_SIDE_EFFECTING


def _in_hbm(a):
    return pltpu.with_memory_space_constraint(a, pltpu.HBM)


def _gather_start(fulls, shs, name):
    n = len(fulls)

    def body(*refs):
        bufs = refs[:n]
        send_sems, recv_sems = refs[n], refs[n + 1]
        token = refs[-1]
        x, y, c, me, chips, chip_ids = _where_am_i()
        for t in range(n):
            mine = shs[t].full_piece(bufs[t], me, c)
            for r in range(3):
                _remote(mine, mine, send_sems.at[3 * t + r], recv_sems.at[3 * t + r], (*chips[r], c)).start()
        token[...] = jnp.zeros_like(token)

    out = pl.pallas_call(
        body, name=name, in_specs=[_HBM] * n,
        out_shape=(pltpu.SemaphoreType.DMA((3 * n,)), pltpu.SemaphoreType.DMA((3 * n,)), *[pltpu.HBM(f.shape, f.dtype) for f in fulls],
                   S((8, 128), _F32)),
        out_specs=(_SEM, _SEM, *[_HBM] * n, pl.BlockSpec(memory_space=pltpu.VMEM)),
        input_output_aliases={t: 2 + t for t in range(n)},
        compiler_params=pltpu.CompilerParams(has_side_effects=_EFFECT), **_KW,
    )(*[_in_hbm(f) for f in fulls])
    return out[0], out[1], list(out[2:2 + n]), out[-1]


def _gather_wait(fulls, send_sems, recv_sems, after, shs, name):
    n = len(fulls)

    def body(*refs):
        bufs = refs[:n]
        ssem, rsem = refs[n], refs[n + 1]
        x, y, c, me, chips, chip_ids = _where_am_i()
        for t in range(n):
            mine = shs[t].full_piece(bufs[t], me, c)
            for r in range(3):
                _remote(mine, mine, ssem.at[3 * t + r], rsem.at[3 * t + r], (*chips[r], c)).wait_send()
        for t in range(n):
            for r in range(3):
                piece = shs[t].full_piece(bufs[t], chip_ids[r], c)
                _remote(piece, piece, ssem.at[3 * t + r], rsem.at[3 * t + r], (*chips[r], c)).wait_recv()

    out = pl.pallas_call(
        body, name=name, in_specs=[*[_HBM] * n, _SEM, _SEM, _ANY], out_specs=[_HBM] * n,
        out_shape=[pltpu.HBM(f.shape, f.dtype) for f in fulls], input_output_aliases={t: t for t in range(n)},
        compiler_params=pltpu.CompilerParams(has_side_effects=_EFFECT), **_KW,
    )(*fulls, send_sems, recv_sems, after)
    return list(out)


def _gather_pass_on(fulls, shs, name):
    n = len(fulls)

    def body(*refs):
        bufs = refs[n:2 * n]
        send_sems, recv_sems = refs[2 * n:]
        x, y, c, me, chips, chip_ids = _where_am_i()
        sib = (x, y, 1 - c)
        cps = []
        for t in range(n):
            for r in range(3):
                piece = shs[t].full_piece(bufs[t], chip_ids[r], c)
                cps.append(_remote(piece, piece, send_sems.at[t, r], recv_sems.at[t, r], sib))
        for cp in cps:
            cp.start()
        for t in range(n):
            for r in range(3):
                piece = shs[t].full_piece(bufs[t], chip_ids[r], 1 - c)
                _remote(piece, piece, send_sems.at[t, r], recv_sems.at[t, r], sib).wait_recv()
        for cp in cps:
            cp.wait_send()

    return _call(
        body, name, in_specs=[_ANY] * n, out_specs=[_ANY] * n, out_shape=[S(f.shape, f.dtype) for f in fulls],
        input_output_aliases={t: t for t in range(n)},
        scratch_shapes=[pltpu.SemaphoreType.DMA((n, 3)), pltpu.SemaphoreType.DMA((n, 3))],
        compiler_params=pltpu.CompilerParams(has_side_effects=True),
    )(*fulls)


def _pass_on_copies(bufs, shs, send_sems, recv_sems):
    x, y, c, me, chips, chip_ids = _where_am_i()
    sib = (x, y, 1 - c)
    out, back = [], []
    for t in range(len(bufs)):
        for r in range(3):
            piece = shs[t].full_piece(bufs[t], chip_ids[r], c)
            out.append(_remote(piece, piece, send_sems.at[3 * t + r], recv_sems.at[3 * t + r], sib))
            other = shs[t].full_piece(bufs[t], chip_ids[r], 1 - c)
            back.append(_remote(other, other, send_sems.at[3 * t + r], recv_sems.at[3 * t + r], sib))
    return out, back


def _pass_on_start(fulls, shs, name):
    n = len(fulls)

    def body(*refs):
        for cp in _pass_on_copies(refs[:n], shs, refs[n], refs[n + 1])[0]:
            cp.start()
        refs[-1][...] = jnp.zeros_like(refs[-1])

    out = pl.pallas_call(
        body, name=name, in_specs=[_HBM] * n,
        out_shape=(pltpu.SemaphoreType.DMA((3 * n,)), pltpu.SemaphoreType.DMA((3 * n,)), *[pltpu.HBM(f.shape, f.dtype) for f in fulls],
                   S((8, 128), _F32)),
        out_specs=(_SEM, _SEM, *[_HBM] * n, pl.BlockSpec(memory_space=pltpu.VMEM)),
        input_output_aliases={t: 2 + t for t in range(n)},
        compiler_params=pltpu.CompilerParams(has_side_effects=_EFFECT), **_KW,
    )(*[_in_hbm(f) for f in fulls])
    return out[0], out[1], list(out[2:2 + n]), out[-1]


def _pass_on_wait(fulls, send_sems, recv_sems, after, shs, name):
    n = len(fulls)

    def body(*refs):
        out, back = _pass_on_copies(refs[:n], shs, refs[n], refs[n + 1])
        for cp in out:
            cp.wait_send()
        for cp in back:
            cp.wait_recv()

    out = pl.pallas_call(
        body, name=name, in_specs=[*[_HBM] * n, _SEM, _SEM, _ANY], out_specs=[_HBM] * n,
        out_shape=[pltpu.HBM(f.shape, f.dtype) for f in fulls], input_output_aliases={t: t for t in range(n)},
        compiler_params=pltpu.CompilerParams(has_side_effects=_EFFECT), **_KW,
    )(*fulls, send_sems, recv_sems, after)
    return list(out)


def _rs_pair_copies(ins, outs, shs, send_sems, recv_sems):
    x, y, c, *_ = _where_am_i()
    sib = (x, y, 1 - c)
    cps = []
    for t in range(len(ins)):
        sh = shs[t]
        if sh.by_cols:
            cps.append(_remote(ins[t].at[pl.ds((1 - c) * sh.Rh, sh.Rh), :], outs[t], send_sems.at[4 * t], recv_sems.at[4 * t], sib))
        else:
            for j in range(4):
                cps.append(_remote(sh.full_piece(ins[t], j, 1 - c), sh.half_piece(outs[t], j),
                                   send_sems.at[4 * t + j], recv_sems.at[4 * t + j], sib))
    return cps


def _rs_pair_start(dws, shs, name):
    n = len(dws)
    lands = [lax.empty((sh.R // 2, sh.C), _WIRE) for sh in shs]

    def body(*refs):
        for cp in _rs_pair_copies(refs[:n], refs[n:2 * n], shs, refs[2 * n], refs[2 * n + 1]):
            cp.start()
        refs[-1][...] = jnp.zeros_like(refs[-1])

    out = pl.pallas_call(
        body, name=name, in_specs=[_HBM] * (2 * n),
        out_shape=(pltpu.SemaphoreType.DMA((4 * n,)), pltpu.SemaphoreType.DMA((4 * n,)),
                   *[pltpu.HBM(a.shape, a.dtype) for a in (*dws, *lands)], S((8, 128), _F32)),
        out_specs=(_SEM, _SEM, *[_HBM] * (2 * n), pl.BlockSpec(memory_space=pltpu.VMEM)),
        input_output_aliases={t: 2 + t for t in range(2 * n)},
        compiler_params=pltpu.CompilerParams(has_side_effects=_EFFECT), **_KW,
    )(*[_in_hbm(a) for a in (*dws, *lands)])
    return out[0], out[1], list(out[2:2 + n]), list(out[2 + n:2 + 2 * n]), out[-1]


def _rs_pair_wait(dws, lands, send_sems, recv_sems, after, shs, name):
    n = len(dws)

    def body(*refs):
        cps = _rs_pair_copies(refs[:n], refs[n:2 * n], shs, refs[2 * n], refs[2 * n + 1])
        for cp in cps:
            cp.wait_send()
        for cp in cps:
            cp.wait_recv()

    out = pl.pallas_call(
        body, name=name, in_specs=[*[_HBM] * (2 * n), _SEM, _SEM, _ANY], out_specs=[_HBM] * (2 * n),
        out_shape=[pltpu.HBM(a.shape, a.dtype) for a in (*dws, *lands)], input_output_aliases={t: t for t in range(2 * n)},
        compiler_params=pltpu.CompilerParams(has_side_effects=_EFFECT), **_KW,
    )(*dws, *lands, send_sems, recv_sems, after)
    return list(out[n:])


def _rs_pair_add(dw32, recv, sh, idx):
    tr, tc = _tile(sh.Q, 512, 16), _tile(sh.C, 2048)
    nb = sh.Q // tr

    def body(i_ref, c_ref, a_ref, b_ref, ow_ref):
        ow_ref[...] = (a_ref[...] + b_ref[...].astype(_F32)).astype(ow_ref.dtype)

    if sh.by_cols:
        a_map = lambda j, i, b, si, sc: (sc[0] * 4 * nb + j * nb + i, b)
    else:
        a_map = lambda j, i, b, si, sc: (j * 2 * nb + sc[0] * nb + i, b)
    h_spec = BS((tr, tc), lambda j, i, b, si, sc: (j * nb + i, b))
    return _call(
        body, "rs_pair_add",
        grid_spec=pltpu.PrefetchScalarGridSpec(num_scalar_prefetch=2, grid=(4, nb, sh.C // tc),
                                               in_specs=[BS((tr, tc), a_map), h_spec], out_specs=h_spec),
        out_shape=S((sh.R // 2, sh.C), _WIRE), compiler_params=_cp("parallel", "parallel", "parallel"),
    )(*idx, dw32, recv)


def _rs_chip_start(pws, shs, name):
    n = len(pws)
    lands = [lax.empty((3, sh.Rh, sh.Cs), _WIRE) for sh in shs]

    def body(*refs):
        ins, lnd = refs[:n], refs[n:2 * n]
        send_sems, recv_sems = refs[2 * n], refs[2 * n + 1]
        token = refs[-1]
        x, y, c, me, chips, chip_ids = _where_am_i()
        for t in range(n):
            for r in range(3):
                _remote(shs[t].half_piece(ins[t], chip_ids[r]), lnd[t].at[r], send_sems.at[3 * t + r], recv_sems.at[3 * t + r],
                        (*chips[r], c)).start()
        token[...] = jnp.zeros_like(token)

    out = pl.pallas_call(
        body, name=name, in_specs=[_HBM] * (2 * n),
        out_shape=(pltpu.SemaphoreType.DMA((3 * n,)), pltpu.SemaphoreType.DMA((3 * n,)),
                   *[pltpu.HBM(a.shape, a.dtype) for a in (*pws, *lands)], S((8, 128), _F32)),
        out_specs=(_SEM, _SEM, *[_HBM] * (2 * n), pl.BlockSpec(memory_space=pltpu.VMEM)),
        input_output_aliases={t: 2 + t for t in range(2 * n)},
        compiler_params=pltpu.CompilerParams(has_side_effects=_EFFECT), **_KW,
    )(*[_in_hbm(a) for a in (*pws, *lands)])
    return out[0], out[1], list(out[2:2 + n]), list(out[2 + n:2 + 2 * n]), out[-1]


def _rs_chip_wait(pws, lands, send_sems, recv_sems, after, shs, name):
    n = len(pws)
    after = list(after) if isinstance(after, (list, tuple)) else [after]

    def body(*refs):
        ins, lnd = refs[:n], refs[n:2 * n]
        ssem, rsem = refs[2 * n], refs[2 * n + 1]
        x, y, c, me, chips, chip_ids = _where_am_i()
        for t in range(n):
            for r in range(3):
                cp = _remote(shs[t].half_piece(ins[t], chip_ids[r]), lnd[t].at[r], ssem.at[3 * t + r], rsem.at[3 * t + r], (*chips[r], c))
                cp.wait_send()
        for t in range(n):
            for r in range(3):
                cp = _remote(shs[t].half_piece(ins[t], chip_ids[r]), lnd[t].at[r], ssem.at[3 * t + r], rsem.at[3 * t + r], (*chips[r], c))
                cp.wait_recv()

    out = pl.pallas_call(
        body, name=name, in_specs=[*[_HBM] * (2 * n), _SEM, _SEM, *[_ANY] * len(after)], out_specs=[_HBM] * (2 * n),
        out_shape=[pltpu.HBM(a.shape, a.dtype) for a in (*pws, *lands)], input_output_aliases={t: t for t in range(2 * n)},
        compiler_params=pltpu.CompilerParams(has_side_effects=_EFFECT), **_KW,
    )(*pws, *lands, send_sems, recv_sems, *after)
    return list(out[n:])


def _rs_chip_add(dw32, pair, recv, sh, idx, g_prev, l, L):
    tr, tc = _tile(sh.Rh, 512, 16), _tile(sh.Cs, 2048)
    nr, nc = sh.Rh // tr, sh.Cs // tc

    def body(i_ref, c_ref, d_ref, a_ref, b_ref, *rest):
        rest[-1][...] = ((d_ref[...] + a_ref[...].astype(_F32)) + b_ref[0].astype(_F32) + b_ref[1].astype(_F32)
                         + b_ref[2].astype(_F32))

    if sh.by_cols:
        d_map = lambda a, b, si, sc: (sc[0] * nr + a, si[0] * nc + b)
        a_map = lambda a, b, si, sc: (a, si[0] * nc + b)
    else:
        d_map = lambda a, b, si, sc: ((2 * si[0] + sc[0]) * nr + a, b)
        a_map = lambda a, b, si, sc: (si[0] * nr + a, b)
    in_specs = [BS((tr, tc), d_map), BS((tr, tc), a_map), BS((3, tr, tc), lambda a, b, si, sc: (0, a, b))]
    args = [*idx, dw32, pair, recv]
    if g_prev is not None:
        in_specs.append(_ANY)
        args.append(g_prev)
    return _call(
        body, "rs_chip_add",
        grid_spec=pltpu.PrefetchScalarGridSpec(num_scalar_prefetch=2, grid=(nr, nc), in_specs=in_specs,
                                               out_specs=BS((None, tr, tc), lambda a, b, si, sc: (l, sc[0] * nr + a, b))),
        out_shape=S((L, sh.Rs, sh.Cs), _F32), input_output_aliases={} if g_prev is None else {5: 0},
        compiler_params=_cp("parallel", "parallel"),
    )(*args)


def _pair_share_copies(bufs, ls, shs, send_sems, recv_sems):
    x, y, c, *_ = _where_am_i()
    sib = (x, y, 1 - c)
    out, back = [], []
    for t in range(len(bufs)):
        mine = shs[t].shard_half(bufs[t].at[ls[t]], c)
        out.append(_remote(mine, mine, send_sems.at[t], recv_sems.at[t], sib))
        other = shs[t].shard_half(bufs[t].at[ls[t]], 1 - c)
        back.append(_remote(other, other, send_sems.at[t], recv_sems.at[t], sib))
    return out, back


def _pair_share_start(gs, ls, shs, name):
    n = len(gs)

    def body(*refs):
        for cp in _pair_share_copies(refs[:n], ls, shs, refs[n], refs[n + 1])[0]:
            cp.start()
        refs[-1][...] = jnp.zeros_like(refs[-1])

    out = pl.pallas_call(
        body, name=name, in_specs=[_HBM] * n,
        out_shape=(pltpu.SemaphoreType.DMA((n,)), pltpu.SemaphoreType.DMA((n,)), *[pltpu.HBM(g.shape, g.dtype) for g in gs],
                   S((8, 128), _F32)),
        out_specs=(_SEM, _SEM, *[_HBM] * n, pl.BlockSpec(memory_space=pltpu.VMEM)),
        input_output_aliases={t: 2 + t for t in range(n)},
        compiler_params=pltpu.CompilerParams(has_side_effects=_EFFECT), **_KW,
    )(*[_in_hbm(g) for g in gs])
    return out[0], out[1], list(out[2:2 + n]), out[-1]


def _pair_share_wait(gs, send_sems, recv_sems, after, ls, shs, name):
    n = len(gs)

    def body(*refs):
        out, back = _pair_share_copies(refs[:n], ls, shs, refs[n], refs[n + 1])
        for cp in out:
            cp.wait_send()
        for cp in back:
            cp.wait_recv()

    out = pl.pallas_call(
        body, name=name, in_specs=[*[_HBM] * n, _SEM, _SEM, _ANY], out_specs=[_HBM] * n,
        out_shape=[pltpu.HBM(g.shape, g.dtype) for g in gs], input_output_aliases={t: t for t in range(n)},
        compiler_params=pltpu.CompilerParams(has_side_effects=_EFFECT), **_KW,
    )(*gs, send_sems, recv_sems, after)
    return list(out)


def _all_reduce_small(xs, dep=None):
    M = xs.shape[0]
    deps = [] if dep is None else [dep]

    def body(x_ref, *rest):
        tot_ref, out_ref, send_sems, recv_sems, local_sem = rest[len(deps):]
        x, y, c, me, chips, chip_ids = _where_am_i()
        sib = (x, y, 1 - c)

        def rows(dev):
            return out_ref.at[pl.ds(_mo((4 * dev[0] + 2 * dev[1] + dev[2]) * M, 8), M), :]

        def copy(k, block, to, src=None):
            return _remote(rows(block) if src is None else src, rows(block), send_sems.at[k], recv_sems.at[k], to)

        mine = pltpu.make_async_copy(x_ref, rows((x, y, c)), local_sem)
        mine.start()
        first = [copy(0, (x, y, c), sib, src=x_ref)]
        first += [copy(1 + j, (x, y, c), (*chip, c), src=x_ref) for j, chip in enumerate(chips)]
        for cp in first:
            cp.start()
        passed = [copy(4 + j, (*chip, c), sib) for j, chip in enumerate(chips)]
        for j, chip in enumerate(chips):
            copy(1 + j, (*chip, c), (x, y, c)).wait_recv()
            passed[j].start()
        copy(0, sib, (x, y, c)).wait_recv()
        for j, chip in enumerate(chips):
            copy(4 + j, (*chip, 1 - c), (x, y, c)).wait_recv()
        for cp in first + passed:
            cp.wait_send()
        mine.wait()
        tot = out_ref[pl.ds(0, M), :]
        for d in range(1, 8):
            tot = tot + out_ref[pl.ds(d * M, M), :]
        tot_ref[...] = tot

    vm = pl.BlockSpec(memory_space=pltpu.VMEM)
    return _call(
        body, "all_reduce_small", in_specs=[vm] + [_ANY] * len(deps), out_specs=[vm, vm],
        out_shape=[S((M, 128), _F32), S((8 * M, 128), _F32)],
        scratch_shapes=[pltpu.SemaphoreType.DMA((7,)), pltpu.SemaphoreType.DMA((7,)), pltpu.SemaphoreType.DMA],
        compiler_params=_cp(has_side_effects=True),
    )(xs, *deps)[0]


def _reduce_scatter_begin(dws, shs, l):
    ssem, rsem, dww, lands, token = _rs_pair_start([d[1] for d in dws], shs, f"rs_pair_start_{l}")
    return ([d[0] for d in dws], dww, lands, ssem, rsem), token


def _reduce_scatter_middle(state, after, shs, idx, l):
    dw32s, dww, lands, ssem, rsem = state
    recv_a = _rs_pair_wait(dww, lands, ssem, rsem, after, shs, f"rs_pair_wait_{l}")
    pws = [_rs_pair_add(d32, ra, sh, idx) for d32, ra, sh in zip(dw32s, recv_a, shs)]
    ssem, rsem, pws, lands, token = _rs_chip_start(pws, shs, f"rs_chip_start_{l}")
    return (dw32s, recv_a, pws, lands, ssem, rsem), token


def _reduce_scatter_end(state, after, tensors, shs, gstack, idx, l):
    dw32s, recv_a, pws, lands, ssem, rsem = state
    recv_b = _rs_chip_wait(pws, lands, ssem, rsem, after, shs, f"rs_chip_wait_{l}")
    gs = [_rs_chip_add(d32, ra, rb, sh, idx, gstack[name], i, L)
          for d32, ra, rb, sh, (name, i, L) in zip(dw32s, recv_a, recv_b, shs, tensors)]
    ssem, rsem, gs, token = _pair_share_start(gs, [i for _, i, _ in tensors], shs, f"pair_share_start_{l}")
    for (name, _, _), g in zip(tensors, gs):
        gstack[name] = g
    return ssem, rsem, token, tensors, shs, l


def _reduce_scatter_shared(share, gstack):
    ssem, rsem, token, tensors, shs, l = share
    gs = _pair_share_wait([gstack[name] for name, _, _ in tensors], ssem, rsem, token, [i for _, i, _ in tensors], shs,
                          f"pair_share_wait_{l}")
    for (name, _, _), g in zip(tensors, gs):
        gstack[name] = g


def _pack(parts):
    out = []
    for p in parts:
        p2 = p.reshape(-1, 128)
        pad = (-p2.shape[0]) % 8
        out.append(jnp.pad(p2, ((0, pad), (0, 0))) if pad else p2)
    return jnp.concatenate(out, axis=0)


def _unpack(packed, like):
    out, at = [], 0
    for p in like:
        n = p.size // 128
        out.append(packed[at:at + n].reshape(p.shape))
        at += n + ((-n) % 8)
    return out


def kernel(x, mem, g_mix, g_ffn, w_in_a, g_v_a, w_spatial, b_spatial, w_in_b, g_q_b, g_k_b, g_mem, w_mem_kv, g_mq, g_mk, w_out, w_gate_up, w_down, loss_target, m_g_mix, m_g_ffn, m_w_in_a, m_g_v_a, m_w_spatial, m_b_spatial, m_w_in_b, m_g_q_b, m_g_k_b, m_g_mem, m_w_mem_kv, m_g_mq, m_g_mk, m_w_out, m_w_gate_up, m_w_down, v_g_mix, v_g_ffn, v_w_in_a, v_g_v_a, v_w_spatial, v_b_spatial, v_w_in_b, v_g_q_b, v_g_k_b, v_g_mem, v_w_mem_kv, v_g_mq, v_g_mk, v_w_out, v_w_gate_up, v_w_down):
    xs = x[0]
    mem2 = mem[0]
    target = loss_target[0]
    T, D = xs.shape
    depth = g_mix.shape[0]
    MEMW = w_mem_kv.shape[2] // 2
    TOK = D - MEMW
    KV = (w_in_b.shape[2] * 4 - TOK - MEMW) // 2
    QPK = TOK // KV
    F = w_gate_up.shape[2] * 4 // 2

    idx = ((2 * lax.axis_index("x") + lax.axis_index("y")).astype(jnp.int32).reshape(1), lax.axis_index("c").astype(jnp.int32).reshape(1))

    big = {
        "w_in_a": (w_in_a, _Shard(D, w_in_a.shape[2] * 4, True)),
        "w_in_b": (w_in_b, _Shard(D, w_in_b.shape[2] * 4, True)),
        "w_mem_kv": (w_mem_kv, _Shard(D, 2 * MEMW, False)),
        "w_out": (w_out, _Shard(D, D, False)),
        "w_gate_up": (w_gate_up, _Shard(D, 2 * F, True)),
        "w_down": (w_down, _Shard(F, D, False)),
    }

    def layer_tensors(l):
        n_in = "w_in_a" if l % 2 == 0 else "w_in_b"
        return [(n_in, l // 2, big[n_in][0].shape[0])] + [(n, l, depth) for n in ("w_mem_kv", "w_out", "w_gate_up", "w_down")]

    def layer_shards(l):
        return [big[n][1] for n, _, _ in layer_tensors(l)]

    full = {n: [None] * w.shape[0] for n, (w, _) in big.items()}
    flying = {}

    def start_layer(l, dep):
        tens = layer_tensors(l)
        token = dep
        for gi, group in enumerate([tens[:2], tens[2:3], tens[3:4], tens[4:]]):
            shs = [big[n][1] for n, _, _ in group]
            bufs = [_cast_into_full(big[n][0], i, big[n][1], idx, dep=token) for n, i, _ in group]
            ssem, rsem, bufs, token = _gather_start(bufs, shs, f"gather_start_{l}_{gi}")
            for n, i, _ in group:
                flying[(n, i)] = dict(group=group, shs=shs, state=(ssem, rsem, bufs), name=f"{l}_{gi}", passing=False)
        return token

    def land(members, after):
        for n, i in members:
            fl = flying[(n, i)]
            ssem, rsem, bufs = fl["state"]
            bufs = _gather_wait(bufs, ssem, rsem, after, fl["shs"], "gather_wait_" + fl["name"])
            ssem, rsem, bufs, after = _pass_on_start(bufs, fl["shs"], "pass_on_start_" + fl["name"])
            fl.update(state=(ssem, rsem, bufs), passing=True)
        return after

    def weight(n, i, after):
        if full[n][i] is None:
            fl = flying[(n, i)]
            ssem, rsem, bufs = fl["state"]
            if fl["passing"]:
                bufs = _pass_on_wait(bufs, ssem, rsem, after, fl["shs"], "pass_on_wait_" + fl["name"])
            else:
                bufs = _gather_wait(bufs, ssem, rsem, after, fl["shs"], "gather_wait_" + fl["name"])
                bufs = _gather_pass_on(bufs, fl["shs"], "gather_pass_on")
            for (m, j, _), b in zip(fl["group"], bufs):
                full[m][j] = b
        return full[n][i]

    after = None
    for l in range(depth):
        after = start_layer(l, after)
    tabs = _rope_tables(T)

    saved = []
    xc = xs
    for l in range(depth):
        is_a = l % 2 == 0
        li = l // 2
        w_in = weight("w_in_a" if is_a else "w_in_b", li, after)
        h, ht, r1 = _rmsnorm_fwd(xc, g_mix[l], "rmsnorm_fwd", transposed=True)
        z = _mm_nn("mm_in", h, w_in, pm=2048, pn=512)
        st = dict(x=xc, ht=ht, r1=r1, z=z)
        if is_a:
            ws_m = w_spatial[li].astype(_MXU)
            st["ws_m"], st["wst_m"], st["b_t"] = ws_m, jnp.swapaxes(ws_m, 1, 2), b_spatial[li].T
            tok = _mixer_a_fwd(z, g_v_a[li], ws_m, st["b_t"], TOK, D)
            qblk = 2 * TOK // MEMW
        else:
            q, k, v = _qk_rope_fwd(z, g_q_b[li], g_k_b[li], tabs, TOK, KV)
            tok, stat = _attn_fwd(q, k, v, QPK, D)
            st["q"], st["k"], st["v"], st["stat"] = q, k, v, stat
            qblk = (TOK + 2 * KV) // MEMW
        mem_n = _rmsnorm_fwd(mem2, g_mem[l], "rmsnorm_mem")
        kv = _mm_nn("mm_memkv", mem_n, weight("w_mem_kv", l, z))
        cat = _mem_fwd(z, qblk, kv, g_mq[l], g_mk[l], MEMW, tok)
        token = land([("w_out", l), ("w_gate_up", l), ("w_down", l)], cat) if l > 0 else None
        x1 = _mm_nn("mm_out", cat, weight("w_out", l, cat), add=xc, dep=token)
        h2, h2t, r2 = _rmsnorm_fwd(x1, g_ffn[l], "rmsnorm_fwd", transposed=True)
        st["r2"] = r2
        act, gu = _ffn_gate_up(h2, weight("w_gate_up", l, h2))
        w_down_l = weight("w_down", l, act)
        token = land([layer_tensors(l + 1)[0][:2]], act) if l + 1 < depth else None
        xc = _mm_nn("mm_down", act, w_down_l, add=x1, pm=512, pn=1024, pk=8192, dep=token)
        after = xc
        st.update(mem_n=mem_n, kv=kv, qblk=qblk, cat=cat, x1=x1, h2t=h2t, act=act, gu=gu)
        saved.append(st)

    dx, dxm, sq = _loss_head(xc, target)
    loss = lax.psum(sq[0, 0] * (0.5 / D), ("x", "y", "c"))

    gsm = {n: [None] * len(a) for n, a in dict(g_mix=g_mix, g_ffn=g_ffn, g_v_a=g_v_a, w_spatial=w_spatial, b_spatial=b_spatial,
                                                g_q_b=g_q_b, g_k_b=g_k_b, g_mem=g_mem, g_mq=g_mq, g_mk=g_mk).items()}
    gstack = {n: None for n in big}
    pairing, chipping, token = None, None, None
    shares = []

    def advance(after):
        nonlocal pairing, chipping
        state, tok = _reduce_scatter_middle(pairing[0], after, layer_shards(pairing[1]), idx, pairing[1])
        if chipping is not None:
            shares.append(_reduce_scatter_end(chipping[0], tok, layer_tensors(chipping[1]), layer_shards(chipping[1]), gstack, idx,
                                              chipping[1]))
        pairing, chipping = None, (state, pairing[1])
        return tok

    for l in reversed(range(depth)):
        st = saved[l]
        is_a = l % 2 == 0
        li = l // 2
        gbig = {}
        dgu = _ffn_dact(dxm, full["w_down"][l], st["gu"], dep=token)
        token = advance(dgu) if pairing is not None else None
        gbig["w_down"] = _mm_tn_dual("mm_dw_down", st["act"], dxm, pm=512, pn=1024, pk=4096)
        dh2 = _ffn_dh(dgu, full["w_gate_up"][l], dep=token)
        gbig["w_gate_up"] = _ffn_dwgu(st["h2t"], dgu)
        dx, dxm, dg = _rmsnorm_bwd(st["x1"], g_ffn[l], dh2, dx, "rmsnorm_bwd", rstd=st["r2"])
        gsm["g_ffn"][l] = dg[0]
        dcat = _mm_nt("mm_dcat", dxm, full["w_out"][l])
        gbig["w_out"] = _mm_tn_dual("mm_dw_out", st["cat"], dxm, pm=512, pn=1024, pk=4096)
        if is_a:
            dz, dws, dbs, dgv = _mixer_a_bwd(st["z"], dcat, g_v_a[li], st["ws_m"], st["wst_m"], st["b_t"], TOK)
            gsm["w_spatial"][li], gsm["b_spatial"][li], gsm["g_v_a"][li] = dws, dbs[:, :, 0], dgv[0]
        else:
            dq, dk, dz = _attn_bwd(st["q"], st["k"], st["v"], dcat, st["cat"], st["stat"], QPK, st["z"].shape[1])
            dz, dgq_b, dgk_b = _qk_rope_bwd(st["z"], dq, dk, g_q_b[li], g_k_b[li], tabs, TOK, KV, dz)
            gsm["g_q_b"][li], gsm["g_k_b"][li] = dgq_b[0], dgk_b[0]
        dz, dkn, dvm, dgq = _mem_bwd(st["z"], st["qblk"], st["kv"], g_mq[l], g_mk[l], dcat, TOK // MEMW, MEMW, dz)
        dkv, dgk = _memkv_bwd(st["kv"], dkn, dvm, g_mk[l], MEMW)
        gsm["g_mq"][l], gsm["g_mk"][l] = dgq[0], dgk[0]
        gbig["w_mem_kv"] = _mm_tn_dual("mm_dw_memkv", st["mem_n"], dkv)
        dmem_n = _mm_nt("mm_dmemn", dkv, full["w_mem_kv"][l])
        gsm["g_mem"][l] = _rmsnorm_bwd(mem2, g_mem[l], dmem_n, None, "rmsnorm_bwd_mem")[2][0]
        n_in = "w_in_a" if is_a else "w_in_b"
        dh = _mm_nt("mm_dh", dz, full[n_in][li])
        gbig[n_in] = _mm_nn_dual("mm_dw_in", st["ht"], dz, pm=1024, pn=512, pk=4096)
        dx, dxm, dg = _rmsnorm_bwd(st["x"], g_mix[l], dh, dx, "rmsnorm_bwd", rstd=st["r1"])
        gsm["g_mix"][l] = dg[0]
        state, token = _reduce_scatter_begin([gbig[n] for n, _, _ in layer_tensors(l)], layer_shards(l), l)
        pairing = (state, l)

    small = ["g_mix", "g_ffn", "g_v_a", "w_spatial", "b_spatial", "g_q_b", "g_k_b", "g_mem", "g_mq", "g_mk"]
    env = dict(g_mix=g_mix, g_ffn=g_ffn, g_v_a=g_v_a, w_spatial=w_spatial, b_spatial=b_spatial, g_q_b=g_q_b, g_k_b=g_k_b,
               g_mem=g_mem, g_mq=g_mq, g_mk=g_mk,
               m_g_mix=m_g_mix, m_g_ffn=m_g_ffn, m_g_v_a=m_g_v_a, m_w_spatial=m_w_spatial, m_b_spatial=m_b_spatial,
               m_g_q_b=m_g_q_b, m_g_k_b=m_g_k_b, m_g_mem=m_g_mem, m_g_mq=m_g_mq, m_g_mk=m_g_mk,
               v_g_mix=v_g_mix, v_g_ffn=v_g_ffn, v_g_v_a=v_g_v_a, v_w_spatial=v_w_spatial, v_b_spatial=v_b_spatial,
               v_g_q_b=v_g_q_b, v_g_k_b=v_g_k_b, v_g_mem=v_g_mem, v_g_mq=v_g_mq, v_g_mk=v_g_mk,
               m_w_in_a=m_w_in_a, m_w_in_b=m_w_in_b, m_w_mem_kv=m_w_mem_kv, m_w_out=m_w_out, m_w_gate_up=m_w_gate_up, m_w_down=m_w_down,
               v_w_in_a=v_w_in_a, v_w_in_b=v_w_in_b, v_w_mem_kv=v_w_mem_kv, v_w_out=v_w_out, v_w_gate_up=v_w_gate_up, v_w_down=v_w_down)
    like = [env[n] for n in small]
    g_small = _all_reduce_small(_pack([jnp.stack(gsm[n]) for n in small]), dep=token)

    res = {}
    outs = _adamw(_pack(like)[None], g_small[None], _pack([env["m_" + n] for n in small])[None],
                  _pack([env["v_" + n] for n in small])[None], "adamw_small")
    unpacked = [_unpack(o[0], like) for o in outs]
    for k, n in enumerate(small):
        res[n] = [u[k] for u in unpacked]
    advance(outs[1])
    pending = chipping
    for share in shares:
        _reduce_scatter_shared(share, gstack)
    last = {n: i for n, i, _ in layer_tensors(pending[1])}
    early = {}
    for n, (w, _) in big.items():
        L = w.shape[0]
        if n not in last:
            res[n] = _adamw(w, gstack[n], env["m_" + n], env["v_" + n], "adamw_" + n)
        elif L > 1:
            assert last[n] == 0
            early[n] = _adamw(w, gstack[n], env["m_" + n], env["v_" + n], "adamw_early_" + n, l0=1)
    done = [o[1] for o in early.values()] + [res[n][1] for n in big if n in res] + [res[small[0]][1]]
    _reduce_scatter_shared(
        _reduce_scatter_end(pending[0], done, layer_tensors(pending[1]), layer_shards(pending[1]), gstack, idx, pending[1]), gstack)
    for n in last:
        res[n] = _adamw(big[n][0], gstack[n], env["m_" + n], env["v_" + n], "adamw_last_" + n, l0=0, l1=1, prev=early.get(n))

    order = ["g_mix", "g_ffn", "w_in_a", "g_v_a", "w_spatial", "b_spatial", "w_in_b", "g_q_b", "g_k_b", "g_mem", "w_mem_kv",
             "g_mq", "g_mk", "w_out", "w_gate_up", "w_down"]
    return (loss, dx.reshape(1, T, D), *[res[n][0] for n in order], *[res[n][1] for n in order],
            *[res[n][2] for n in order], *[res[n][3] for n in order])
```
